```python
import jax, jax.numpy as jnp
from jax import lax
import numpy as np

D_MODEL = 2048
BATCH = 2
SEQ = 4096
DEPTH = 2

HEAD_DIM = 64
N_Q_HEADS = 16
N_KV_HEADS = 4
ATTN_WIDTH = N_Q_HEADS * HEAD_DIM
KV_WIDTH = N_KV_HEADS * HEAD_DIM
WINDOW = 128
BLOCK = 128

RWKV_HEAD = 64
N_RWKV_HEADS = 16
RWKV_WIDTH = N_RWKV_HEADS * RWKV_HEAD
DECAY_LORA = 64
AAA_LORA = 64
MV_LORA = 32
GATE_LORA = 160
RWKV_GN_EPS = 64e-5

ATTN_COLS = ATTN_WIDTH + 2 * KV_WIDTH
SHIFT_WIDTH = 3 * RWKV_WIDTH + DECAY_LORA + AAA_LORA + GATE_LORA
IN_WIDTH = ATTN_COLS + SHIFT_WIDTH + 2 * D_MODEL

N_GROUPS = 4
EXPERTS_PER_GROUP = 8
N_EXPERTS = N_GROUPS * EXPERTS_PER_GROUP
TOP_K = 2
EXPERT_HIDDEN = D_MODEL // 4
MOE_BLOCK = 128

NORM_EPS = 1e-5

kernel_name = "hybrid_swa_rwkv7_hmoe"


def rms_norm(x, w):
    xf = x.astype(jnp.float32)
    y = xf * lax.rsqrt(jnp.mean(xf * xf, axis=-1, keepdims=True) + NORM_EPS)
    return (y * w.astype(jnp.float32)).astype(x.dtype)


def token_shift(z):
    return jnp.pad(z[:, :-1], ((0, 0), (1, 0), (0, 0)))


def _split(z, sizes):
    out, off = [], 0
    for s in sizes:
        out.append(z[..., off:off + s])
        off += s
    return out


def alibi_slopes(n):
    return jnp.exp2(-8.0 * jnp.arange(1, n + 1, dtype=jnp.float32) / n)


def sliding_window_attention(q, k, v, sinks):
    B, T = q.shape[:2]
    nb = T // BLOCK
    G = N_Q_HEADS // N_KV_HEADS
    f32 = jnp.float32
    qb = q.reshape(B, nb, BLOCK, N_KV_HEADS, G, HEAD_DIM).astype(f32)

    def with_prev(z):
        zb = z.reshape(B, nb, BLOCK, N_KV_HEADS, HEAD_DIM)
        prev = jnp.pad(zb[:, :-1], ((0, 0), (1, 0), (0, 0), (0, 0), (0, 0)))
        return jnp.concatenate([prev, zb], axis=2).astype(f32)

    kw, vw = with_prev(k), with_prev(v)
    scores = jnp.einsum('bnqhgd,bnkhd->bnhgqk', qb, kw) * (HEAD_DIM ** -0.5)
    qi = jnp.arange(BLOCK)[:, None]
    kj = jnp.arange(2 * BLOCK)[None, :]
    dist = qi + BLOCK - kj
    key_pos = (jnp.arange(nb)[:, None, None] - 1) * BLOCK + kj
    valid = (dist >= 0) & (dist < WINDOW) & (key_pos >= 0)
    slopes = alibi_slopes(N_Q_HEADS).reshape(N_KV_HEADS, G)[:, :, None, None]
    scores = scores - slopes * dist.astype(f32)
    scores = jnp.where(valid[None, :, None, None], scores, -jnp.inf)
    sink = sinks.astype(f32).reshape(N_KV_HEADS, G)[:, :, None, None]
    m = jnp.maximum(jnp.max(scores, axis=-1, keepdims=True), sink)
    p = jnp.exp(scores - m)
    probs = p / (jnp.sum(p, axis=-1, keepdims=True) + jnp.exp(sink - m))
    out = jnp.einsum('bnhgqk,bnkhd->bnqhgd', probs, vw)
    return out.reshape(B, T, ATTN_WIDTH)


def rwkv7_scan(r, decay, k, v, kk, a):
    B, T, H, N = r.shape

    def step(S, inp):
        r_t, w_t, k_t, v_t, kk_t, a_t = inp
        sa = jnp.einsum('bhvk,bhk->bhv', S, -kk_t)
        S = S * w_t[:, :, None, :] + sa[..., None] * (kk_t * a_t)[:, :, None, :] \
            + v_t[..., None] * k_t[:, :, None, :]
        return S, jnp.einsum('bhvk,bhk->bhv', S, r_t)

    xs = tuple(jnp.moveaxis(z, 1, 0) for z in (r, decay, k, v, kk, a))
    S0 = jnp.zeros((B, H, N, N), jnp.float32)
    _, ys = lax.scan(step, S0, xs)
    return jnp.moveaxis(ys, 0, 1)


def hybrid_mixer(h, v_first, w_in, shift_mix, attn_sinks, w0, w_up, a0, a_up, g_up,
                 k_k, k_a, r_k, ln_w, ln_b, w_branch_a, w_branch_b, w_out, vres):
    B, T, _ = h.shape
    f32 = jnp.float32
    proj = jnp.einsum('btd,dc->btc', h, w_in)
    q, k, v, z, gates = _split(proj, (ATTN_WIDTH, KV_WIDTH, KV_WIDTH, SHIFT_WIDTH, 2 * D_MODEL))

    attn = sliding_window_attention(q.reshape(B, T, N_Q_HEADS, HEAD_DIM),
                                    k.reshape(B, T, N_KV_HEADS, HEAD_DIM),
                                    v.reshape(B, T, N_KV_HEADS, HEAD_DIM), attn_sinks)

    z = (z + (token_shift(z) - z) * shift_mix).astype(f32)
    r, kr, vr, wd, ad, gd = _split(z, (RWKV_WIDTH, RWKV_WIDTH, RWKV_WIDTH, DECAY_LORA, AAA_LORA, GATE_LORA))
    w_log = -jax.nn.softplus(-(w0.astype(f32) + jnp.tanh(wd) @ w_up.astype(f32))) - 0.5
    a = jax.nn.sigmoid(a0.astype(f32) + ad @ a_up.astype(f32))
    g = jax.nn.sigmoid(gd) @ g_up.astype(f32)
    if vres is None:
        v_first = vr
    else:
        vd, vmix, vup, v0 = vres
        zv = h @ vd
        zv = (zv + (token_shift(zv) - zv) * vmix).astype(f32)
        vr = vr + (v_first - vr) * jax.nn.sigmoid(v0.astype(f32) + zv @ vup.astype(f32))

    def heads(t):
        return t.reshape(B, T, N_RWKV_HEADS, RWKV_HEAD)

    kk = heads(kr * k_k.astype(f32))
    kk = kk / jnp.maximum(jnp.sqrt(jnp.sum(kk * kk, axis=-1, keepdims=True)), 1e-12)
    kr = kr * (1.0 + (a - 1.0) * k_a.astype(f32))
    rh, kh, vh, ah = heads(r), heads(kr), heads(vr), heads(a)
    decay = heads(jnp.exp(-jnp.exp(w_log)))
    y = rwkv7_scan(rh, decay, kh, vh, kk, ah)
    mu = jnp.mean(y, axis=-1, keepdims=True)
    var = jnp.mean(jnp.square(y - mu), axis=-1, keepdims=True)
    y = ((y - mu) * lax.rsqrt(var + RWKV_GN_EPS)).reshape(B, T, RWKV_WIDTH)
    y = y * ln_w.astype(f32) + ln_b.astype(f32)
    bonus = jnp.sum(rh * kh * r_k.astype(f32), axis=-1, keepdims=True) * vh
    y = (y + bonus.reshape(B, T, RWKV_WIDTH)) * g

    ga, gb = _split(gates, (D_MODEL, D_MODEL))
    merged = jax.nn.sigmoid(ga) * (attn.astype(h.dtype) @ w_branch_a) \
        + jax.nn.sigmoid(gb) * (y.astype(h.dtype) @ w_branch_b)
    return merged @ w_out, v_first


def hierarchical_moe(h, rg_w, rg_b, re_w, re_b, w_gate, w_up, w_down):
    B, T, D = h.shape
    N = B * T
    f32 = jnp.float32
    xt = h.reshape(N, D)
    glog = (xt @ rg_w).astype(f32) + rg_b.astype(f32)
    gprob = jax.nn.softmax(glog, axis=-1)
    gsel = jnp.argmax(glog, axis=-1)
    gp = jnp.take_along_axis(gprob, gsel[:, None], axis=-1)
    elog = ((xt @ re_w).astype(f32) + re_b.astype(f32)).reshape(N, N_GROUPS, EXPERTS_PER_GROUP)
    elog_g = jnp.take_along_axis(elog, gsel[:, None, None], axis=1)[:, 0]
    top_vals, top_idx = lax.top_k(elog_g, TOP_K)
    ew = jax.nn.softmax(top_vals, axis=-1) * gp

    expert_ids = (gsel[:, None] * EXPERTS_PER_GROUP + top_idx).reshape(-1)
    token_ids = jnp.repeat(jnp.arange(N, dtype=jnp.int32), TOP_K)
    weights = ew.reshape(-1)
    A = N * TOP_K
    n_blocks = -(-A // MOE_BLOCK) + N_EXPERTS
    cap = n_blocks * MOE_BLOCK

    order = jnp.argsort(expert_ids)
    se = expert_ids[order]
    counts = jnp.bincount(expert_ids, length=N_EXPERTS)
    starts = jnp.cumsum(counts) - counts
    padded = (counts + MOE_BLOCK - 1) // MOE_BLOCK * MOE_BLOCK
    pad_ends = jnp.cumsum(padded)
    pad_starts = pad_ends - padded
    dest = pad_starts[se] + (jnp.arange(A) - starts[se])
    slot_tok = jnp.full((cap,), N, jnp.int32).at[dest].set(token_ids[order])
    slot_w = jnp.zeros((cap,), f32).at[dest].set(weights[order])
    block_expert = jnp.minimum(
        jnp.searchsorted(pad_ends, jnp.arange(n_blocks) * MOE_BLOCK, side='right'), N_EXPERTS - 1)

    x_pad = jnp.concatenate([xt, jnp.zeros((1, D), xt.dtype)], axis=0)
    xb = x_pad[slot_tok].reshape(n_blocks, MOE_BLOCK, D)

    def expert_block(args):
        xblk, e = args
        hid = jax.nn.silu(xblk @ w_gate[e]) * (xblk @ w_up[e])
        return hid @ w_down[e]

    yb = lax.map(expert_block, (xb, block_expert)).reshape(cap, D).astype(f32)
    y = jnp.zeros((N + 1, D), f32).at[slot_tok].add(yb * slot_w[:, None])
    return y[:N].reshape(B, T, D).astype(h.dtype)


def setup_inputs(seed: int = 0) -> dict:
    key = jax.random.key(seed)
    ks = iter(jax.random.split(key, 40))
    f32 = jnp.float32

    def nrm(shape, scale):
        return jax.random.normal(next(ks), shape, f32) * scale

    def uni(shape):
        return jax.random.uniform(next(ks), shape, f32)

    L, Lv = DEPTH, DEPTH - 1
    return {
        "x": nrm((BATCH, SEQ, D_MODEL), 1.0),
        "attn_norm_w": 1.0 + nrm((L, D_MODEL), 0.02),
        "w_in": nrm((L, D_MODEL, IN_WIDTH), D_MODEL ** -0.5),
        "shift_mix": uni((L, SHIFT_WIDTH)),
        "attn_sinks": nrm((L, N_Q_HEADS), 0.5),
        "rwkv_w0": nrm((L, RWKV_WIDTH), 1.0) - 0.5,
        "rwkv_w_up": nrm((L, DECAY_LORA, RWKV_WIDTH), 0.5 * DECAY_LORA ** -0.5),
        "rwkv_a0": nrm((L, RWKV_WIDTH), 0.5),
        "rwkv_a_up": nrm((L, AAA_LORA, RWKV_WIDTH), 0.5 * AAA_LORA ** -0.5),
        "rwkv_g_up": nrm((L, GATE_LORA, RWKV_WIDTH), GATE_LORA ** -0.5),
        "rwkv_k_k": 0.85 + nrm((L, RWKV_WIDTH), 0.05),
        "rwkv_k_a": 1.0 + nrm((L, RWKV_WIDTH), 0.05),
        "rwkv_r_k": nrm((L, N_RWKV_HEADS, RWKV_HEAD), 0.1),
        "rwkv_ln_w": 1.0 + nrm((L, RWKV_WIDTH), 0.02),
        "rwkv_ln_b": nrm((L, RWKV_WIDTH), 0.02),
        "vres_down": nrm((Lv, D_MODEL, MV_LORA), D_MODEL ** -0.5),
        "vres_mix": uni((Lv, MV_LORA)),
        "vres_up": nrm((Lv, MV_LORA, RWKV_WIDTH), 0.5 * MV_LORA ** -0.5),
        "vres_v0": nrm((Lv, RWKV_WIDTH), 0.5),
        "w_branch_a": nrm((L, ATTN_WIDTH, D_MODEL), ATTN_WIDTH ** -0.5),
        "w_branch_b": nrm((L, RWKV_WIDTH, D_MODEL), RWKV_WIDTH ** -0.5),
        "w_out": nrm((L, D_MODEL, D_MODEL), D_MODEL ** -0.5),
        "ffn_norm_w": 1.0 + nrm((L, D_MODEL), 0.02),
        "router_group_w": nrm((L, D_MODEL, N_GROUPS), D_MODEL ** -0.5),
        "router_group_b": nrm((L, N_GROUPS), 0.01),
        "router_expert_w": nrm((L, D_MODEL, N_EXPERTS), D_MODEL ** -0.5),
        "router_expert_b": nrm((L, N_EXPERTS), 0.01),
        "expert_w_gate": nrm((L, N_EXPERTS, D_MODEL, EXPERT_HIDDEN), D_MODEL ** -0.5),
        "expert_w_up": nrm((L, N_EXPERTS, D_MODEL, EXPERT_HIDDEN), D_MODEL ** -0.5),
        "expert_w_down": nrm((L, N_EXPERTS, EXPERT_HIDDEN, D_MODEL), EXPERT_HIDDEN ** -0.5),
        "final_norm_w": 1.0 + nrm((D_MODEL,), 0.02),
    }


def reference(x, attn_norm_w, w_in, shift_mix, attn_sinks, rwkv_w0, rwkv_w_up, rwkv_a0,
              rwkv_a_up, rwkv_g_up, rwkv_k_k, rwkv_k_a, rwkv_r_k, rwkv_ln_w, rwkv_ln_b,
              vres_down, vres_mix, vres_up, vres_v0, w_branch_a, w_branch_b, w_out,
              ffn_norm_w, router_group_w, router_group_b, router_expert_w, router_expert_b,
              expert_w_gate, expert_w_up, expert_w_down, final_norm_w):
    v_first = None
    for i in range(DEPTH):
        vres = None if i == 0 else (vres_down[i - 1], vres_mix[i - 1], vres_up[i - 1], vres_v0[i - 1])
        h = rms_norm(x, attn_norm_w[i])
        mix, v_first = hybrid_mixer(h, v_first, w_in[i], shift_mix[i], attn_sinks[i],
                                    rwkv_w0[i], rwkv_w_up[i], rwkv_a0[i], rwkv_a_up[i], rwkv_g_up[i],
                                    rwkv_k_k[i], rwkv_k_a[i], rwkv_r_k[i], rwkv_ln_w[i], rwkv_ln_b[i],
                                    w_branch_a[i], w_branch_b[i], w_out[i], vres)
        x = x + mix.astype(x.dtype)
        hf = rms_norm(x, ffn_norm_w[i])
        x = x + hierarchical_moe(hf, router_group_w[i], router_group_b[i], router_expert_w[i],
                                 router_expert_b[i], expert_w_gate[i], expert_w_up[i],
                                 expert_w_down[i]).astype(x.dtype)
    return rms_norm(x, final_norm_w)
```

```python
import functools

import jax
import jax.numpy as jnp
from jax import lax
from jax.experimental import pallas as pl
from jax.experimental.pallas import tpu as pltpu

F32 = jnp.float32
BF16 = jnp.bfloat16

D_MODEL = 2048
HEAD_DIM = 64
N_Q_HEADS = 16
N_KV_HEADS = 4
GQA_GROUP = N_Q_HEADS // N_KV_HEADS
ATTN_WIDTH = N_Q_HEADS * HEAD_DIM
KV_WIDTH = N_KV_HEADS * HEAD_DIM
WINDOW = 128
ATTN_BLOCK = 128

RWKV_HEAD = 64
N_RWKV_HEADS = 16
RWKV_WIDTH = N_RWKV_HEADS * RWKV_HEAD
DECAY_LORA = 64
AAA_LORA = 64
MV_LORA = 32
GATE_LORA = 160
RWKV_GN_EPS = 64e-5
LORA_WIDTH = DECAY_LORA + AAA_LORA + GATE_LORA

N_GROUPS = 4
EXPERTS_PER_GROUP = 8
N_EXPERTS = N_GROUPS * EXPERTS_PER_GROUP
TOP_K = 2
EXPERT_HIDDEN = D_MODEL // 4
NORM_EPS = 1e-5

LANES = 128
SUBLANES = 8

SEG = 512
QKV_COLS = ATTN_WIDTH + 2 * KV_WIDTH
RKV_COLS = 3 * RWKV_WIDTH
LORA_COLS = SEG
GATE_COLS = 2 * D_MODEL
QKV_TILES = QKV_COLS // SEG
RKV_TILES = RKV_COLS // SEG
LORA_TILES = LORA_COLS // SEG
GATE_TILES = GATE_COLS // SEG
IN_TILES = QKV_TILES + RKV_TILES + LORA_TILES + GATE_TILES

SCAN_CHUNK = 64
MOE_ROWS = 256
ROUTER_LANES = LANES
EXPERT_LANE0 = N_GROUPS

ROW_SUB = D_MODEL // LANES

VMEM_LIMIT = 56 * 1024 * 1024


def _rows_to_2d(ref3, *lead):
    return jnp.concatenate([ref3[lead + (slice(None), s, slice(None))] for s in range(ROW_SUB)], axis=1)


def _store_rows(ref3, val):
    for s in range(ROW_SUB):
        ref3[:, s, :] = val[:, s * LANES:(s + 1) * LANES]


def _cparams(*sem):
    return pltpu.CompilerParams(dimension_semantics=sem, vmem_limit_bytes=VMEM_LIMIT)


def _bdot(a, b):
    return jnp.dot(a.astype(BF16), b.astype(BF16), preferred_element_type=F32)


def _bdot_nt(a, b):
    return lax.dot_general(a.astype(BF16), b.astype(BF16), (((1,), (1,)), ((), ())),
                           preferred_element_type=F32)


def _bdot_tn(a, b):
    return lax.dot_general(a.astype(BF16), b.astype(BF16), (((0,), (0,)), ((), ())),
                           preferred_element_type=F32)


def _dot_split_rhs(w01, x, passes):
    acc, rem = None, x
    for p in range(passes):
        part = rem.astype(BF16)
        d = jnp.dot(w01, part, preferred_element_type=F32)
        acc = d if acc is None else acc + d
        if p + 1 < passes:
            rem = rem - part.astype(F32)
    return acc


def _dot_split_lhs(x, w01, passes):
    acc, rem = None, x
    for p in range(passes):
        part = rem.astype(BF16)
        d = jnp.dot(part, w01, preferred_element_type=F32)
        acc = d if acc is None else acc + d
        if p + 1 < passes:
            rem = rem - part.astype(F32)
    return acc


def _rms(x, w):
    ms = jnp.mean(x * x, axis=-1, keepdims=True)
    return x * lax.rsqrt(ms + NORM_EPS) * w


def _inproj_kernel(x_ref, nw_ref, w_ref, qkv_ref, rkv_ref, lora_ref, gate_ref, h_scr):
    j = pl.program_id(1)

    @pl.when(j == 0)
    def _():
        h_scr[...] = _rms(x_ref[...], nw_ref[...]).astype(BF16)

    acc = jnp.dot(h_scr[...], w_ref[...], preferred_element_type=F32)

    @pl.when(j < QKV_TILES)
    def _():
        qkv_ref[...] = acc.astype(BF16)

    @pl.when((j >= QKV_TILES) & (j < QKV_TILES + RKV_TILES))
    def _():
        rkv_ref[...] = acc

    @pl.when(j == QKV_TILES + RKV_TILES)
    def _():
        lora_ref[...] = acc

    @pl.when(j > QKV_TILES + RKV_TILES)
    def _():
        gate_ref[...] = acc.astype(BF16)


def _inproj(x, norm_w, w_all, tm=1024):
    n = x.shape[0]
    s_rkv = QKV_TILES
    s_gate = QKV_TILES + RKV_TILES + LORA_TILES
    return pl.pallas_call(
        _inproj_kernel,
        grid=(n // tm, IN_TILES),
        in_specs=[
            pl.BlockSpec((tm, D_MODEL), lambda i, j: (i, 0)),
            pl.BlockSpec((1, D_MODEL), lambda i, j: (0, 0)),
            pl.BlockSpec((D_MODEL, SEG), lambda i, j: (0, j)),
        ],
        out_specs=[
            pl.BlockSpec((tm, SEG), lambda i, j: (i, jnp.clip(j, 0, QKV_TILES - 1))),
            pl.BlockSpec((tm, SEG), lambda i, j: (i, jnp.clip(j - s_rkv, 0, RKV_TILES - 1))),
            pl.BlockSpec((tm, SEG), lambda i, j: (i, 0)),
            pl.BlockSpec((tm, SEG), lambda i, j: (i, jnp.clip(j - s_gate, 0, GATE_TILES - 1))),
        ],
        out_shape=[
            jax.ShapeDtypeStruct((n, QKV_COLS), BF16),
            jax.ShapeDtypeStruct((n, RKV_COLS), F32),
            jax.ShapeDtypeStruct((n, LORA_COLS), F32),
            jax.ShapeDtypeStruct((n, GATE_COLS), BF16),
        ],
        scratch_shapes=[pltpu.VMEM((tm, D_MODEL), BF16)],
        compiler_params=_cparams("parallel", "arbitrary"),
        name="inproj",
    )(x, norm_w, w_all)


def _attn_kernel(sink_ref, q_ref, kp_ref, kc_ref, vp_ref, vc_ref, bias_ref, o_ref, *, nb):
    first = (pl.program_id(0) % nb) == 0
    col = lax.broadcasted_iota(jnp.int32, (ATTN_BLOCK, 2 * ATTN_BLOCK), 1)
    pad_keys = jnp.logical_and(first, col < ATTN_BLOCK)
    scale = HEAD_DIM ** -0.5
    for hk in range(N_KV_HEADS):
        ks = slice(hk * HEAD_DIM, (hk + 1) * HEAD_DIM)
        kw = jnp.concatenate([kp_ref[:, ks], kc_ref[:, ks]], axis=0)
        vw = jnp.concatenate([vp_ref[:, ks], vc_ref[:, ks]], axis=0)
        for g in range(GQA_GROUP):
            h = hk * GQA_GROUP + g
            qh = q_ref[:, h * HEAD_DIM:(h + 1) * HEAD_DIM]
            s = lax.dot_general(qh, kw, (((1,), (1,)), ((), ())), preferred_element_type=F32)
            s = s * scale + bias_ref[h]
            s = jnp.where(pad_keys, -jnp.inf, s)
            sink = sink_ref[h]
            m = jnp.maximum(jnp.max(s, axis=-1, keepdims=True), sink)
            p = jnp.exp(s - m)
            denom = jnp.sum(p, axis=-1, keepdims=True) + jnp.exp(sink - m)
            o = jnp.dot(p.astype(BF16), vw, preferred_element_type=F32) / denom
            o_ref[:, h * HEAD_DIM:(h + 1) * HEAD_DIM] = o.astype(BF16)


def _attn_bias():
    qi = jnp.arange(ATTN_BLOCK)[:, None]
    kj = jnp.arange(2 * ATTN_BLOCK)[None, :]
    dist = qi + ATTN_BLOCK - kj
    valid = (dist >= 0) & (dist < WINDOW)
    slopes = jnp.exp2(-8.0 * jnp.arange(1, N_Q_HEADS + 1, dtype=F32) / N_Q_HEADS)
    bias = -slopes[:, None, None] * dist.astype(F32)[None]
    return jnp.where(valid[None], bias, -jnp.inf)


def _attention(qkv, sinks, seq):
    n = qkv.shape[0]
    nb = seq // ATTN_BLOCK
    kcol = ATTN_WIDTH // KV_WIDTH
    prev = lambda i: jnp.where(i % nb == 0, i, i - 1)
    return pl.pallas_call(
        functools.partial(_attn_kernel, nb=nb),
        grid=(n // ATTN_BLOCK,),
        in_specs=[
            pl.BlockSpec(memory_space=pltpu.SMEM),
            pl.BlockSpec((ATTN_BLOCK, ATTN_WIDTH), lambda i: (i, 0)),
            pl.BlockSpec((ATTN_BLOCK, KV_WIDTH), lambda i: (prev(i), kcol)),
            pl.BlockSpec((ATTN_BLOCK, KV_WIDTH), lambda i: (i, kcol)),
            pl.BlockSpec((ATTN_BLOCK, KV_WIDTH), lambda i: (prev(i), kcol + 1)),
            pl.BlockSpec((ATTN_BLOCK, KV_WIDTH), lambda i: (i, kcol + 1)),
            pl.BlockSpec((N_Q_HEADS, ATTN_BLOCK, 2 * ATTN_BLOCK), lambda i: (0, 0, 0)),
        ],
        out_specs=pl.BlockSpec((ATTN_BLOCK, ATTN_WIDTH), lambda i: (i, 0)),
        out_shape=jax.ShapeDtypeStruct((n, ATTN_WIDTH), BF16),
        compiler_params=_cparams("parallel"),
        name="swa_attention",
    )(sinks.astype(F32), qkv, qkv, qkv, qkv, qkv, _attn_bias())


def _head_indicator():
    ch = jnp.arange(RWKV_WIDTH)[:, None] // RWKV_HEAD
    e = (ch == jnp.arange(LANES)[None, :]).astype(BF16)
    return e, e.T


def _rwkv_pre_kernel(*refs, has_vres, tiles_per_seq):
    refs = list(refs)
    rkv_ref, rkvp_ref, lo_ref, lop_ref = refs[:4]
    refs = refs[4:]
    vf_ref = refs.pop(0) if has_vres else None
    (mixr_ref, mixl_ref, pv_ref, wup_ref, aup_ref, gup_ref, vup_ref, e_ref, et_ref,
     r_out, lw_out, k_out, v_out, al_out, be_out, g_out) = refs

    first = (pl.program_id(0) % tiles_per_seq) == 0

    def shifted(z, zp_ref):
        prev_row = jnp.where(first, 0.0, zp_ref[SUBLANES - 1:SUBLANES, :])
        row = lax.broadcasted_iota(jnp.int32, z.shape, 0)
        return jnp.where(row == 0, prev_row, pltpu.roll(z, 1, 0))

    z = rkv_ref[...]
    zs = z + (shifted(z, rkvp_ref) - z) * mixr_ref[...]
    lo = lo_ref[...]
    los = lo + (shifted(lo, lop_ref) - lo) * mixl_ref[...]

    r = zs[:, :RWKV_WIDTH]
    kr = zs[:, RWKV_WIDTH:2 * RWKV_WIDTH]
    vr = zs[:, 2 * RWKV_WIDTH:]

    col = lax.broadcasted_iota(jnp.int32, los.shape, 1)
    gate_cols = (col >= DECAY_LORA + AAA_LORA) & (col < LORA_WIDTH)
    act = jnp.where(col < DECAY_LORA, jnp.tanh(los), jnp.where(gate_cols, jax.nn.sigmoid(los), los))
    act = act.astype(BF16)
    wa_in = act[:, :LANES]
    dw = jnp.dot(wa_in, wup_ref[...], preferred_element_type=F32)
    da = jnp.dot(wa_in, aup_ref[...], preferred_element_type=F32)
    g = jnp.dot(act[:, LANES:3 * LANES], gup_ref[...], preferred_element_type=F32)

    w0, a0, k_k, k_a, v0 = (pv_ref[i:i + 1, :] for i in range(5))
    u = -(w0 + dw)
    softplus = jnp.maximum(u, 0.0) + jnp.log1p(jnp.exp(-jnp.abs(u)))
    w_log = -softplus - 0.5
    a = jax.nn.sigmoid(a0 + da)
    if has_vres:
        dv = jnp.dot(act[:, 2 * LANES:3 * LANES], vup_ref[...], preferred_element_type=F32)
        vr = vr + (vf_ref[...] - vr) * jax.nn.sigmoid(v0 + dv)

    kk0 = kr * k_k
    ss = _dot_split_lhs(kk0 * kk0, e_ref[...], 2)
    inv = 1.0 / jnp.maximum(jnp.sqrt(ss), 1e-12)
    kk = kk0 * _dot_split_lhs(inv, et_ref[...], 2)

    r_out[...] = r
    lw_out[...] = -jnp.exp(w_log)
    k_out[...] = kr * (1.0 + (a - 1.0) * k_a)
    v_out[...] = vr
    al_out[...] = -kk
    be_out[...] = kk * a
    g_out[...] = g


def _rwkv_pre(rkv, lora, v_first, mix_r, mix_l, pvec, wup, aup, gup, vup, seq, tt=256):
    n = rkv.shape[0]
    has_vres = v_first is not None
    e, et = _head_indicator()
    row = lambda i: (i, 0)
    prev8 = lambda i: (jnp.maximum(i * (tt // SUBLANES) - 1, 0), 0)
    const = lambda i: (0, 0)
    in_specs = [
        pl.BlockSpec((tt, RKV_COLS), row),
        pl.BlockSpec((SUBLANES, RKV_COLS), prev8),
        pl.BlockSpec((tt, LORA_COLS), row),
        pl.BlockSpec((SUBLANES, LORA_COLS), prev8),
    ]
    args = [rkv, rkv, lora, lora]
    if has_vres:
        in_specs.append(pl.BlockSpec((tt, RWKV_WIDTH), row))
        args.append(v_first)
    in_specs += [
        pl.BlockSpec((1, RKV_COLS), const),
        pl.BlockSpec((1, LORA_COLS), const),
        pl.BlockSpec((SUBLANES, RWKV_WIDTH), const),
        pl.BlockSpec((LANES, RWKV_WIDTH), const),
        pl.BlockSpec((LANES, RWKV_WIDTH), const),
        pl.BlockSpec((2 * LANES, RWKV_WIDTH), const),
        pl.BlockSpec((LANES, RWKV_WIDTH), const),
        pl.BlockSpec((RWKV_WIDTH, LANES), const),
        pl.BlockSpec((LANES, RWKV_WIDTH), const),
    ]
    args += [mix_r, mix_l, pvec, wup, aup, gup, vup, e, et]
    out = jax.ShapeDtypeStruct((n, RWKV_WIDTH), F32)
    return pl.pallas_call(
        functools.partial(_rwkv_pre_kernel, has_vres=has_vres, tiles_per_seq=seq // tt),
        grid=(n // tt,),
        in_specs=in_specs,
        out_specs=[pl.BlockSpec((tt, RWKV_WIDTH), row)] * 7,
        out_shape=[out] * 7,
        compiler_params=_cparams("parallel"),
        name="rwkv_pre",
    )(*args)


def _scan_kernel(r_ref, lw_ref, k_ref, v_ref, al_ref, be_ref, y_ref, s_scr):
    L = SCAN_CHUNK

    @pl.when(pl.program_id(1) == 0)
    def _():
        s_scr[...] = jnp.zeros_like(s_scr)

    row = lax.broadcasted_iota(jnp.int32, (L, L), 0)
    col = lax.broadcasted_iota(jnp.int32, (L, L), 1)
    strict = row > col
    incl = row >= col
    tri = incl.astype(BF16)
    eye = (row == col).astype(F32)

    lw = lw_ref[...]
    cum = _dot_split_rhs(tri, lw, 3)
    p_inc = jnp.exp(cum)
    p_exc = jnp.exp(cum - lw)
    p_inv = jnp.exp(-cum)
    p_last = p_inc[L - 1:L, :]
    a_t = al_ref[...] * p_exc
    r_t = r_ref[...] * p_inc
    b_t = be_ref[...] * p_inv
    k_t = k_ref[...] * p_inv
    b_end = b_t * p_last
    k_end = k_t * p_last
    v = v_ref[...]

    for h in range(N_RWKV_HEADS):
        sl = slice(h * RWKV_HEAD, (h + 1) * RWKV_HEAD)
        ar = jnp.concatenate([a_t[:, sl], r_t[:, sl]], axis=0)
        bk = jnp.concatenate([b_t[:, sl], k_t[:, sl]], axis=0)
        gm = _bdot_nt(ar, bk)
        a_ab = jnp.where(strict, gm[:L, :L], 0.0)
        a_ak = jnp.where(strict, gm[:L, L:], 0.0)
        a_rb = jnp.where(incl, gm[L:, :L], 0.0)
        a_rk = jnp.where(incl, gm[L:, L:], 0.0)

        pw = a_ab
        tinv = eye + a_ab
        span = 2
        while span < L:
            pw_new = _bdot(pw, pw)
            tinv = tinv + _bdot(pw_new, tinv)
            pw = pw_new
            span *= 2

        s0 = s_scr[h]
        vh = v[:, sl]
        xs = _bdot_nt(ar, s0)
        av = _bdot(jnp.concatenate([a_ak, a_rk], axis=0), vh)
        u = _bdot(tinv, xs[:L] + av[:L])
        y = xs[L:] + av[L:] + _bdot(a_rb, u)
        y_ref[:, sl] = y
        uv = jnp.concatenate([u, vh], axis=0)
        bke = jnp.concatenate([b_end[:, sl], k_end[:, sl]], axis=0)
        s_scr[h] = s0 * p_last[:, sl] + _bdot_tn(uv, bke)


def _rwkv_scan(r, lw, k, v, al, be, seq):
    n = r.shape[0]
    nc = seq // SCAN_CHUNK
    spec = pl.BlockSpec((SCAN_CHUNK, RWKV_WIDTH), lambda b, c: (b * nc + c, 0))
    return pl.pallas_call(
        _scan_kernel,
        grid=(n // seq, nc),
        in_specs=[spec] * 6,
        out_specs=spec,
        out_shape=jax.ShapeDtypeStruct((n, RWKV_WIDTH), F32),
        scratch_shapes=[pltpu.VMEM((N_RWKV_HEADS, RWKV_HEAD, RWKV_HEAD), F32)],
        compiler_params=_cparams("parallel", "arbitrary"),
        name="rwkv_scan",
    )(r, lw, k, v, al, be)


def _rwkv_post_kernel(y_ref, r_ref, k_ref, v_ref, g_ref, pv_ref, e_ref, et_ref, o_ref):
    e, et = e_ref[...], et_ref[...]
    ln_w, ln_b, r_k = (pv_ref[i:i + 1, :] for i in range(3))
    y = y_ref[...]
    inv_n = 1.0 / RWKV_HEAD
    mu = _dot_split_lhs(y, e, 2) * inv_n
    d = y - _dot_split_lhs(mu, et, 2)
    var = _dot_split_lhs(d * d, e, 2) * inv_n
    rstd = lax.rsqrt(var + RWKV_GN_EPS)
    yn = d * _dot_split_lhs(rstd, et, 2) * ln_w + ln_b
    v = v_ref[...]
    coef = _dot_split_lhs(r_ref[...] * k_ref[...] * r_k, e, 2)
    bonus = _dot_split_lhs(coef, et, 2) * v
    o_ref[...] = ((yn + bonus) * g_ref[...]).astype(BF16)


def _rwkv_post(y, r, k, v, g, pvec, tt=256):
    n = y.shape[0]
    e, et = _head_indicator()
    row = lambda i: (i, 0)
    const = lambda i: (0, 0)
    return pl.pallas_call(
        _rwkv_post_kernel,
        grid=(n // tt,),
        in_specs=[pl.BlockSpec((tt, RWKV_WIDTH), row)] * 5 + [
            pl.BlockSpec((SUBLANES, RWKV_WIDTH), const),
            pl.BlockSpec((RWKV_WIDTH, LANES), const),
            pl.BlockSpec((LANES, RWKV_WIDTH), const),
        ],
        out_specs=pl.BlockSpec((tt, RWKV_WIDTH), row),
        out_shape=jax.ShapeDtypeStruct((n, RWKV_WIDTH), BF16),
        compiler_params=_cparams("parallel"),
        name="rwkv_post",
    )(y, r, k, v, g, pvec, e, et)


def _merge_kernel(a_ref, y_ref, wa_ref, wb_ref, ga_ref, gb_ref, o_ref):
    pa = jnp.dot(a_ref[...], wa_ref[...], preferred_element_type=F32)
    pb = jnp.dot(y_ref[...], wb_ref[...], preferred_element_type=F32)
    ga = jax.nn.sigmoid(ga_ref[...].astype(F32))
    gb = jax.nn.sigmoid(gb_ref[...].astype(F32))
    o_ref[...] = (ga * pa + gb * pb).astype(BF16)


def _merge(attn, y, wa, wb, gates, tm=1024, tn=512):
    n = attn.shape[0]
    nj = D_MODEL // tn
    return pl.pallas_call(
        _merge_kernel,
        grid=(n // tm, nj),
        in_specs=[
            pl.BlockSpec((tm, ATTN_WIDTH), lambda i, j: (i, 0)),
            pl.BlockSpec((tm, RWKV_WIDTH), lambda i, j: (i, 0)),
            pl.BlockSpec((ATTN_WIDTH, tn), lambda i, j: (0, j)),
            pl.BlockSpec((RWKV_WIDTH, tn), lambda i, j: (0, j)),
            pl.BlockSpec((tm, tn), lambda i, j: (i, j)),
            pl.BlockSpec((tm, tn), lambda i, j: (i, j + nj)),
        ],
        out_specs=pl.BlockSpec((tm, tn), lambda i, j: (i, j)),
        out_shape=jax.ShapeDtypeStruct((n, D_MODEL), BF16),
        compiler_params=_cparams("parallel", "arbitrary"),
        name="gated_merge",
    )(attn, y, wa, wb, gates, gates)


def _outproj_router_kernel(m_ref, wo_ref, x_ref, nw_ref, wr_ref, br_ref,
                           xo_ref, hf_ref, slab_ref, cnt_ref, run_scr):
    @pl.when(pl.program_id(0) == 0)
    def _():
        run_scr[...] = jnp.zeros_like(run_scr)

    xn = x_ref[...] + jnp.dot(m_ref[...], wo_ref[...], preferred_element_type=F32)
    xo_ref[...] = xn
    hf = _rms(xn, nw_ref[...])
    _store_rows(hf_ref, hf)

    h_hi = hf.astype(BF16)
    h_lo = (hf - h_hi.astype(F32)).astype(BF16)
    wr = wr_ref[...]
    w_hi = wr.astype(BF16)
    w_lo = (wr - w_hi.astype(F32)).astype(BF16)
    lg = (jnp.dot(h_hi, w_hi, preferred_element_type=F32)
          + jnp.dot(h_hi, w_lo, preferred_element_type=F32)
          + jnp.dot(h_lo, w_hi, preferred_element_type=F32)) + br_ref[...]

    tm = lg.shape[0]
    lane = lax.broadcasted_iota(jnp.int32, lg.shape, 1).astype(F32)
    neg = -jnp.inf
    big = float(ROUTER_LANES)

    def first_argmax(vals):
        mx = jnp.max(vals, axis=-1, keepdims=True)
        idx = jnp.min(jnp.where(vals == mx, lane, big), axis=-1, keepdims=True)
        return mx, idx

    is_group = lane < N_GROUPS
    gmax, gsel = first_argmax(jnp.where(is_group, lg, neg))
    gp = 1.0 / jnp.sum(jnp.where(is_group, jnp.exp(lg - gmax), 0.0), axis=-1, keepdims=True)
    lo_lane = EXPERT_LANE0 + EXPERTS_PER_GROUP * gsel
    in_group = (lane >= lo_lane) & (lane < lo_lane + EXPERTS_PER_GROUP)
    el = jnp.where(in_group, lg, neg)
    v1, i1 = first_argmax(el)
    v2, i2 = first_argmax(jnp.where(lane == i1, neg, el))
    e21 = jnp.exp(v2 - v1)
    ew1 = gp / (1.0 + e21)
    ew2 = gp * e21 / (1.0 + e21)

    oh1 = lane == i1
    oh2 = lane == i2
    cnt = oh1.astype(F32) + oh2.astype(F32)
    r_i = lax.broadcasted_iota(jnp.int32, (tm, tm), 0)
    c_i = lax.broadcasted_iota(jnp.int32, (tm, tm), 1)
    before = jnp.dot((r_i > c_i).astype(BF16), cnt.astype(BF16), preferred_element_type=F32)
    tot = before + run_scr[0:1, :]
    rank1 = jnp.sum(jnp.where(oh1, tot, 0.0), axis=-1, keepdims=True)
    rank2 = jnp.sum(jnp.where(oh2, tot, 0.0), axis=-1, keepdims=True)
    run = run_scr[0:1, :] + jnp.sum(cnt, axis=0, keepdims=True)
    run_scr[...] = jnp.broadcast_to(run, run_scr.shape)
    cnt_ref[...] = jnp.broadcast_to(run, cnt_ref.shape)

    slab = jnp.where(lane == 0, i1 - EXPERT_LANE0,
           jnp.where(lane == 1, i2 - EXPERT_LANE0,
           jnp.where(lane == 2, ew1,
           jnp.where(lane == 3, ew2,
           jnp.where(lane == 4, rank1,
           jnp.where(lane == 5, rank2, 0.0))))))
    slab_ref[...] = slab


def _outproj_router(merged, w_out, x, norm_w, w_router, b_router, tm=256):
    n = x.shape[0]
    row = lambda i: (i, 0)
    const = lambda i: (0, 0)
    return pl.pallas_call(
        _outproj_router_kernel,
        grid=(n // tm,),
        in_specs=[
            pl.BlockSpec((tm, D_MODEL), row),
            pl.BlockSpec((D_MODEL, D_MODEL), const),
            pl.BlockSpec((tm, D_MODEL), row),
            pl.BlockSpec((1, D_MODEL), const),
            pl.BlockSpec((D_MODEL, ROUTER_LANES), const),
            pl.BlockSpec((1, ROUTER_LANES), const),
        ],
        out_specs=[
            pl.BlockSpec((tm, D_MODEL), row),
            pl.BlockSpec((tm, ROW_SUB, LANES), lambda i: (i, 0, 0)),
            pl.BlockSpec((tm, ROUTER_LANES), row),
            pl.BlockSpec((SUBLANES, ROUTER_LANES), const),
        ],
        out_shape=[
            jax.ShapeDtypeStruct((n, D_MODEL), F32),
            jax.ShapeDtypeStruct((n, ROW_SUB, LANES), F32),
            jax.ShapeDtypeStruct((n, ROUTER_LANES), F32),
            jax.ShapeDtypeStruct((SUBLANES, ROUTER_LANES), F32),
        ],
        scratch_shapes=[pltpu.VMEM((SUBLANES, ROUTER_LANES), F32)],
        compiler_params=_cparams("arbitrary"),
        name="outproj_router",
    )(merged, w_out, x, norm_w, w_router, b_router)


def _dispatch_kernel(slot_ref, hf_hbm, xb_in_hbm, xb_hbm, sem, *, tm):
    del xb_in_hbm
    base = pl.program_id(0) * tm

    def copy(u, j):
        return pltpu.make_async_copy(hf_hbm.at[pl.ds(base + u, 1)],
                                     xb_hbm.at[pl.ds(slot_ref[TOP_K * u + j], 1)], sem)

    def start(u, c):
        for j in range(TOP_K):
            copy(u, j).start()
        return c

    def wait(u, c):
        for j in range(TOP_K):
            copy(u, j).wait()
        return c

    lax.fori_loop(0, tm, start, 0, unroll=8)
    lax.fori_loop(0, tm, wait, 0, unroll=8)


def _dispatch(slot_flat, hf, cap, tm=128):
    n = hf.shape[0]
    xb0 = jnp.zeros((cap, ROW_SUB, LANES), F32)
    return pl.pallas_call(
        functools.partial(_dispatch_kernel, tm=tm),
        grid=(n // tm,),
        in_specs=[
            pl.BlockSpec((TOP_K * tm,), lambda i: (i,), memory_space=pltpu.SMEM),
            pl.BlockSpec(memory_space=pl.ANY),
            pl.BlockSpec(memory_space=pl.ANY),
        ],
        out_specs=pl.BlockSpec(memory_space=pl.ANY),
        out_shape=jax.ShapeDtypeStruct((cap, ROW_SUB, LANES), F32),
        scratch_shapes=[pltpu.SemaphoreType.DMA(())],
        input_output_aliases={2: 0},
        compiler_params=_cparams("arbitrary"),
        name="moe_dispatch",
    )(slot_flat, hf, xb0)


def _ffn_kernel(be_ref, nu_ref, x_ref, wg_ref, wu_ref, wd_ref, o_ref, wg_s, wu_s, wd_s):
    b = pl.program_id(0)
    used = b < nu_ref[0]
    prev_e = be_ref[jnp.maximum(b - 1, 0)]
    new_expert = jnp.logical_or(b == 0, be_ref[b] != prev_e)

    @pl.when(jnp.logical_and(used, new_expert))
    def _():
        wg_s[...] = wg_ref[...].astype(BF16)
        wu_s[...] = wu_ref[...].astype(BF16)
        wd_s[...] = wd_ref[...].astype(BF16)

    @pl.when(used)
    def _():
        x = _rows_to_2d(x_ref).astype(BF16)
        gt = jnp.dot(x, wg_s[...], preferred_element_type=F32)
        up = jnp.dot(x, wu_s[...], preferred_element_type=F32)
        hid = (gt * jax.nn.sigmoid(gt) * up).astype(BF16)
        _store_rows(o_ref, jnp.dot(hid, wd_s[...], preferred_element_type=F32))

    @pl.when(jnp.logical_not(used))
    def _():
        o_ref[...] = jnp.zeros_like(o_ref)


def _expert_ffn(block_expert, n_used, xb, w_gate, w_up, w_down):
    cap = xb.shape[0]
    nblk = cap // MOE_ROWS
    wmap = lambda b, be, nu: (be[b], 0, 0)
    grid_spec = pltpu.PrefetchScalarGridSpec(
        num_scalar_prefetch=2,
        grid=(nblk,),
        in_specs=[
            pl.BlockSpec((MOE_ROWS, ROW_SUB, LANES), lambda b, be, nu: (b, 0, 0)),
            pl.BlockSpec((None, D_MODEL, EXPERT_HIDDEN), wmap),
            pl.BlockSpec((None, D_MODEL, EXPERT_HIDDEN), wmap),
            pl.BlockSpec((None, EXPERT_HIDDEN, D_MODEL), wmap),
        ],
        out_specs=pl.BlockSpec((MOE_ROWS, ROW_SUB, LANES), lambda b, be, nu: (b, 0, 0)),
        scratch_shapes=[
            pltpu.VMEM((D_MODEL, EXPERT_HIDDEN), BF16),
            pltpu.VMEM((D_MODEL, EXPERT_HIDDEN), BF16),
            pltpu.VMEM((EXPERT_HIDDEN, D_MODEL), BF16),
        ],
    )
    return pl.pallas_call(
        _ffn_kernel,
        grid_spec=grid_spec,
        out_shape=jax.ShapeDtypeStruct((cap, ROW_SUB, LANES), F32),
        compiler_params=_cparams("arbitrary"),
        name="expert_ffn",
    )(block_expert, n_used, xb, w_gate, w_up, w_down)


def _combine_kernel(slot_ref, x_ref, slab_ref, fw_ref, yb_hbm, o_ref, ybuf, sem, *, tm, final_norm):
    def copy(u, j):
        return pltpu.make_async_copy(yb_hbm.at[pl.ds(slot_ref[TOP_K * u + j], 1)],
                                     ybuf.at[j, pl.ds(u, 1)], sem)

    def start(u, c):
        for j in range(TOP_K):
            copy(u, j).start()
        return c

    def wait(u, c):
        for j in range(TOP_K):
            copy(u, j).wait()
        return c

    lax.fori_loop(0, tm, start, 0, unroll=8)
    lax.fori_loop(0, tm, wait, 0, unroll=8)

    slab = slab_ref[...]
    w1 = slab[:, 2:3]
    w2 = slab[:, 3:4]
    out = x_ref[...] + (_rows_to_2d(ybuf, 0) * w1 + _rows_to_2d(ybuf, 1) * w2)
    if final_norm:
        out = _rms(out, fw_ref[...])
    o_ref[...] = out


def _combine(slot_flat, x, slab, final_w, yb, final_norm, tm=128):
    n = x.shape[0]
    row = lambda i: (i, 0)
    return pl.pallas_call(
        functools.partial(_combine_kernel, tm=tm, final_norm=final_norm),
        grid=(n // tm,),
        in_specs=[
            pl.BlockSpec((TOP_K * tm,), lambda i: (i,), memory_space=pltpu.SMEM),
            pl.BlockSpec((tm, D_MODEL), row),
            pl.BlockSpec((tm, ROUTER_LANES), row),
            pl.BlockSpec((1, D_MODEL), lambda i: (0, 0)),
            pl.BlockSpec(memory_space=pl.ANY),
        ],
        out_specs=pl.BlockSpec((tm, D_MODEL), row),
        out_shape=jax.ShapeDtypeStruct((n, D_MODEL), F32),
        scratch_shapes=[pltpu.VMEM((TOP_K, tm, ROW_SUB, LANES), F32), pltpu.SemaphoreType.DMA(())],
        compiler_params=_cparams("arbitrary"),
        name="moe_combine",
    )(slot_flat, x, slab, final_w, yb)


def _pad_rows(w, rows, offset=0):
    out = jnp.zeros((rows, w.shape[1]), BF16)
    return out.at[offset:offset + w.shape[0]].set(w.astype(BF16))


def _pad_cols(vec, cols, offset=0):
    out = jnp.zeros((cols,), F32)
    return out.at[offset:offset + vec.shape[0]].set(vec.astype(F32))


def _moe_plan(slab, counts_row, n_tokens):
    eid = slab[:, 0:TOP_K].astype(jnp.int32)
    rank = slab[:, 4:4 + TOP_K].astype(jnp.int32)
    counts = counts_row[EXPERT_LANE0:EXPERT_LANE0 + N_EXPERTS].astype(jnp.int32)
    padded = (counts + MOE_ROWS - 1) // MOE_ROWS * MOE_ROWS
    pad_ends = jnp.cumsum(padded)
    pad_starts = pad_ends - padded
    slot = (pad_starts[eid] + rank).reshape(-1)
    nblk = n_tokens * TOP_K // MOE_ROWS + N_EXPERTS
    block_expert = jnp.minimum(
        jnp.searchsorted(pad_ends, jnp.arange(nblk, dtype=jnp.int32) * MOE_ROWS, side='right'),
        N_EXPERTS - 1).astype(jnp.int32)
    n_used = (pad_ends[-1:] // MOE_ROWS).astype(jnp.int32)
    return slot, block_expert, n_used, nblk * MOE_ROWS


def kernel(x, attn_norm_w, w_in, shift_mix, attn_sinks, rwkv_w0, rwkv_w_up, rwkv_a0, rwkv_a_up, rwkv_g_up, rwkv_k_k, rwkv_k_a, rwkv_r_k, rwkv_ln_w, rwkv_ln_b, vres_down, vres_mix, vres_up, vres_v0, w_branch_a, w_branch_b, w_out, ffn_norm_w, router_group_w, router_group_b, router_expert_w, router_expert_b, expert_w_gate, expert_w_up, expert_w_down, final_norm_w):
    batch, seq, _ = x.shape
    n = batch * seq
    depth = w_in.shape[0]
    xf = x.reshape(n, D_MODEL)
    z_off = QKV_COLS
    g_off = QKV_COLS + RKV_COLS + LORA_WIDTH
    v_first = None
    for i in range(depth):
        has_vres = i > 0
        wi = w_in[i]
        vd = vres_down[i - 1] if has_vres else jnp.zeros((D_MODEL, MV_LORA), F32)
        w_lora = jnp.concatenate(
            [wi[:, z_off + RKV_COLS:g_off], vd,
             jnp.zeros((D_MODEL, LORA_COLS - LORA_WIDTH - MV_LORA), F32)], axis=1)
        w_all = jnp.concatenate([wi[:, :z_off + RKV_COLS], w_lora, wi[:, g_off:]], axis=1).astype(BF16)
        mix_r = shift_mix[i, :RKV_COLS].reshape(1, RKV_COLS)
        vmix = vres_mix[i - 1] if has_vres else jnp.zeros((MV_LORA,), F32)
        mix_l = _pad_cols(jnp.concatenate([shift_mix[i, RKV_COLS:], vmix]), LORA_COLS).reshape(1, LORA_COLS)
        v0 = vres_v0[i - 1] if has_vres else jnp.zeros((RWKV_WIDTH,), F32)
        zero = jnp.zeros((RWKV_WIDTH,), F32)
        pv_pre = jnp.stack([rwkv_w0[i], rwkv_a0[i], rwkv_k_k[i], rwkv_k_a[i], v0, zero, zero, zero]).astype(F32)
        pv_post = jnp.stack([rwkv_ln_w[i], rwkv_ln_b[i], rwkv_r_k[i].reshape(-1),
                             zero, zero, zero, zero, zero]).astype(F32)
        wup = _pad_rows(rwkv_w_up[i], LANES, 0)
        aup = _pad_rows(rwkv_a_up[i], LANES, DECAY_LORA)
        gup = _pad_rows(rwkv_g_up[i], 2 * LANES, 0)
        vup_w = vres_up[i - 1] if has_vres else jnp.zeros((MV_LORA, RWKV_WIDTH), F32)
        vup = _pad_rows(vup_w, LANES, LORA_WIDTH - 2 * LANES)
        w_router = jnp.concatenate(
            [router_group_w[i], router_expert_w[i],
             jnp.zeros((D_MODEL, ROUTER_LANES - N_GROUPS - N_EXPERTS), F32)], axis=1)
        b_router = _pad_cols(jnp.concatenate([router_group_b[i], router_expert_b[i]]),
                             ROUTER_LANES).reshape(1, ROUTER_LANES)

        qkv, rkv, lora, gates = _inproj(xf, attn_norm_w[i].reshape(1, D_MODEL), w_all)
        attn = _attention(qkv, attn_sinks[i], seq)
        r, lw, k2, v, al, be, g = _rwkv_pre(rkv, lora, v_first, mix_r, mix_l, pv_pre,
                                            wup, aup, gup, vup, seq)
        if not has_vres:
            v_first = v
        y_raw = _rwkv_scan(r, lw, k2, v, al, be, seq)
        y = _rwkv_post(y_raw, r, k2, v, g, pv_post)
        merged = _merge(attn, y, w_branch_a[i].astype(BF16), w_branch_b[i].astype(BF16), gates)

        x_mid, hf, slab, counts = _outproj_router(
            merged, w_out[i].astype(BF16), xf, ffn_norm_w[i].reshape(1, D_MODEL), w_router, b_router)
        slot, block_expert, n_used, cap = _moe_plan(slab, counts[0], n)
        xb = _dispatch(slot, hf, cap)
        yb = _expert_ffn(block_expert, n_used, xb, expert_w_gate[i], expert_w_up[i], expert_w_down[i])
        xf = _combine(slot, x_mid, slab, final_norm_w.reshape(1, D_MODEL), yb, final_norm=(i == depth - 1))
    return xf.reshape(batch, seq, D_MODEL)
```

```python
import functools

import jax
import jax.numpy as jnp
from jax import lax
from jax.experimental import pallas as pl
from jax.experimental.pallas import tpu as pltpu

F32 = jnp.float32
BF16 = jnp.bfloat16

D_MODEL = 2048
HEAD_DIM = 64
N_Q_HEADS = 16
N_KV_HEADS = 4
GQA_GROUP = N_Q_HEADS // N_KV_HEADS
ATTN_WIDTH = N_Q_HEADS * HEAD_DIM
KV_WIDTH = N_KV_HEADS * HEAD_DIM
WINDOW = 128
ATTN_BLOCK = 128

RWKV_HEAD = 64
N_RWKV_HEADS = 16
RWKV_WIDTH = N_RWKV_HEADS * RWKV_HEAD
DECAY_LORA = 64
AAA_LORA = 64
MV_LORA = 32
GATE_LORA = 160
RWKV_GN_EPS = 64e-5
LORA_WIDTH = DECAY_LORA + AAA_LORA + GATE_LORA

N_GROUPS = 4
EXPERTS_PER_GROUP = 8
N_EXPERTS = N_GROUPS * EXPERTS_PER_GROUP
TOP_K = 2
EXPERT_HIDDEN = D_MODEL // 4
NORM_EPS = 1e-5

LANES = 128
SUBLANES = 8

SEG = 512
QKV_COLS = ATTN_WIDTH + 2 * KV_WIDTH
RKV_COLS = 3 * RWKV_WIDTH
LORA_COLS = SEG
GATE_COLS = 2 * D_MODEL
QKV_TILES = QKV_COLS // SEG
RKV_TILES = RKV_COLS // SEG
LORA_TILES = LORA_COLS // SEG
GATE_TILES = GATE_COLS // SEG
IN_TILES = QKV_TILES + RKV_TILES + LORA_TILES + GATE_TILES

SCAN_CHUNK = 64
MOE_ROWS = 256
ROUTER_LANES = LANES
EXPERT_LANE0 = N_GROUPS

ROW_SUB = D_MODEL // LANES

VMEM_LIMIT = 56 * 1024 * 1024


def _rows_to_2d(ref3, *lead):
    return jnp.concatenate([ref3[lead + (slice(None), s, slice(None))] for s in range(ROW_SUB)], axis=1)


def _store_rows(ref3, val):
    for s in range(ROW_SUB):
        ref3[:, s, :] = val[:, s * LANES:(s + 1) * LANES]


def _cparams(*sem):
    return pltpu.CompilerParams(dimension_semantics=sem, vmem_limit_bytes=VMEM_LIMIT)


def _bdot(a, b):
    return jnp.dot(a.astype(BF16), b.astype(BF16), preferred_element_type=F32)


def _bdot_nt(a, b):
    return lax.dot_general(a.astype(BF16), b.astype(BF16), (((1,), (1,)), ((), ())),
                           preferred_element_type=F32)


def _bdot_tn(a, b):
    return lax.dot_general(a.astype(BF16), b.astype(BF16), (((0,), (0,)), ((), ())),
                           preferred_element_type=F32)


def _dot_split_rhs(w01, x, passes):
    acc, rem = None, x
    for p in range(passes):
        part = rem.astype(BF16)
        d = jnp.dot(w01, part, preferred_element_type=F32)
        acc = d if acc is None else acc + d
        if p + 1 < passes:
            rem = rem - part.astype(F32)
    return acc


def _dot_split_lhs(x, w01, passes):
    acc, rem = None, x
    for p in range(passes):
        part = rem.astype(BF16)
        d = jnp.dot(part, w01, preferred_element_type=F32)
        acc = d if acc is None else acc + d
        if p + 1 < passes:
            rem = rem - part.astype(F32)
    return acc


def _rms(x, w):
    ms = jnp.mean(x * x, axis=-1, keepdims=True)
    return x * lax.rsqrt(ms + NORM_EPS) * w


def _inproj_kernel(x_ref, nw_ref, w_ref, qkv_ref, rkv_ref, lora_ref, gate_ref, h_scr):
    j = pl.program_id(1)

    @pl.when(j == 0)
    def _():
        h_scr[...] = _rms(x_ref[...], nw_ref[...]).astype(BF16)

    acc = jnp.dot(h_scr[...], w_ref[...], preferred_element_type=F32)

    @pl.when(j < QKV_TILES)
    def _():
        qkv_ref[...] = acc.astype(BF16)

    @pl.when((j >= QKV_TILES) & (j < QKV_TILES + RKV_TILES))
    def _():
        rkv_ref[...] = acc

    @pl.when(j == QKV_TILES + RKV_TILES)
    def _():
        lora_ref[...] = acc

    @pl.when(j > QKV_TILES + RKV_TILES)
    def _():
        gate_ref[...] = acc.astype(BF16)


def _inproj(x, norm_w, w_all, tm=1024):
    n = x.shape[0]
    s_rkv = QKV_TILES
    s_gate = QKV_TILES + RKV_TILES + LORA_TILES
    return pl.pallas_call(
        _inproj_kernel,
        grid=(n // tm, IN_TILES),
        in_specs=[
            pl.BlockSpec((tm, D_MODEL), lambda i, j: (i, 0)),
            pl.BlockSpec((1, D_MODEL), lambda i, j: (0, 0)),
            pl.BlockSpec((D_MODEL, SEG), lambda i, j: (0, j)),
        ],
        out_specs=[
            pl.BlockSpec((tm, SEG), lambda i, j: (i, jnp.clip(j, 0, QKV_TILES - 1))),
            pl.BlockSpec((tm, SEG), lambda i, j: (i, jnp.clip(j - s_rkv, 0, RKV_TILES - 1))),
            pl.BlockSpec((tm, SEG), lambda i, j: (i, 0)),
            pl.BlockSpec((tm, SEG), lambda i, j: (i, jnp.clip(j - s_gate, 0, GATE_TILES - 1))),
        ],
        out_shape=[
            jax.ShapeDtypeStruct((n, QKV_COLS), BF16),
            jax.ShapeDtypeStruct((n, RKV_COLS), F32),
            jax.ShapeDtypeStruct((n, LORA_COLS), F32),
            jax.ShapeDtypeStruct((n, GATE_COLS), BF16),
        ],
        scratch_shapes=[pltpu.VMEM((tm, D_MODEL), BF16)],
        compiler_params=_cparams("parallel", "arbitrary"),
        name="inproj",
    )(x, norm_w, w_all)


def _attn_kernel(sink_ref, q_ref, kp_ref, kc_ref, vp_ref, vc_ref, bias_ref, o_ref, *, nb):
    first = (pl.program_id(0) % nb) == 0
    col = lax.broadcasted_iota(jnp.int32, (ATTN_BLOCK, 2 * ATTN_BLOCK), 1)
    pad_keys = jnp.logical_and(first, col < ATTN_BLOCK)
    scale = HEAD_DIM ** -0.5
    for hk in range(N_KV_HEADS):
        ks = slice(hk * HEAD_DIM, (hk + 1) * HEAD_DIM)
        kw = jnp.concatenate([kp_ref[:, ks], kc_ref[:, ks]], axis=0)
        vw = jnp.concatenate([vp_ref[:, ks], vc_ref[:, ks]], axis=0)
        for g in range(GQA_GROUP):
            h = hk * GQA_GROUP + g
            qh = q_ref[:, h * HEAD_DIM:(h + 1) * HEAD_DIM]
            s = lax.dot_general(qh, kw, (((1,), (1,)), ((), ())), preferred_element_type=F32)
            s = s * scale + bias_ref[h]
            s = jnp.where(pad_keys, -jnp.inf, s)
            sink = sink_ref[h]
            m = jnp.maximum(jnp.max(s, axis=-1, keepdims=True), sink)
            p = jnp.exp(s - m)
            denom = jnp.sum(p, axis=-1, keepdims=True) + jnp.exp(sink - m)
            o = jnp.dot(p.astype(BF16), vw, preferred_element_type=F32) / denom
            o_ref[:, h * HEAD_DIM:(h + 1) * HEAD_DIM] = o.astype(BF16)


def _attn_bias():
    qi = jnp.arange(ATTN_BLOCK)[:, None]
    kj = jnp.arange(2 * ATTN_BLOCK)[None, :]
    dist = qi + ATTN_BLOCK - kj
    valid = (dist >= 0) & (dist < WINDOW)
    slopes = jnp.exp2(-8.0 * jnp.arange(1, N_Q_HEADS + 1, dtype=F32) / N_Q_HEADS)
    bias = -slopes[:, None, None] * dist.astype(F32)[None]
    return jnp.where(valid[None], bias, -jnp.inf)


def _attention(qkv, sinks, seq):
    n = qkv.shape[0]
    nb = seq // ATTN_BLOCK
    kcol = ATTN_WIDTH // KV_WIDTH
    prev = lambda i: jnp.where(i % nb == 0, i, i - 1)
    return pl.pallas_call(
        functools.partial(_attn_kernel, nb=nb),
        grid=(n // ATTN_BLOCK,),
        in_specs=[
            pl.BlockSpec(memory_space=pltpu.SMEM),
            pl.BlockSpec((ATTN_BLOCK, ATTN_WIDTH), lambda i: (i, 0)),
            pl.BlockSpec((ATTN_BLOCK, KV_WIDTH), lambda i: (prev(i), kcol)),
            pl.BlockSpec((ATTN_BLOCK, KV_WIDTH), lambda i: (i, kcol)),
            pl.BlockSpec((ATTN_BLOCK, KV_WIDTH), lambda i: (prev(i), kcol + 1)),
            pl.BlockSpec((ATTN_BLOCK, KV_WIDTH), lambda i: (i, kcol + 1)),
            pl.BlockSpec((N_Q_HEADS, ATTN_BLOCK, 2 * ATTN_BLOCK), lambda i: (0, 0, 0)),
        ],
        out_specs=pl.BlockSpec((ATTN_BLOCK, ATTN_WIDTH), lambda i: (i, 0)),
        out_shape=jax.ShapeDtypeStruct((n, ATTN_WIDTH), BF16),
        compiler_params=_cparams("parallel"),
        name="swa_attention",
    )(sinks.astype(F32), qkv, qkv, qkv, qkv, qkv, _attn_bias())


def _head_indicator():
    ch = jnp.arange(RWKV_WIDTH)[:, None] // RWKV_HEAD
    e = (ch == jnp.arange(LANES)[None, :]).astype(BF16)
    return e, e.T


def _rwkv_pre_kernel(*refs, has_vres, tiles_per_seq):
    refs = list(refs)
    rkv_ref, rkvp_ref, lo_ref, lop_ref = refs[:4]
    refs = refs[4:]
    vf_ref = refs.pop(0) if has_vres else None
    (mixr_ref, mixl_ref, pv_ref, wup_ref, aup_ref, gup_ref, vup_ref, e_ref, et_ref,
     r_out, lw_out, k_out, v_out, al_out, be_out, g_out) = refs

    first = (pl.program_id(0) % tiles_per_seq) == 0

    def shifted(z, zp_ref):
        prev_row = jnp.where(first, 0.0, zp_ref[SUBLANES - 1:SUBLANES, :])
        row = lax.broadcasted_iota(jnp.int32, z.shape, 0)
        return jnp.where(row == 0, prev_row, pltpu.roll(z, 1, 0))

    z = rkv_ref[...]
    zs = z + (shifted(z, rkvp_ref) - z) * mixr_ref[...]
    lo = lo_ref[...]
    los = lo + (shifted(lo, lop_ref) - lo) * mixl_ref[...]

    r = zs[:, :RWKV_WIDTH]
    kr = zs[:, RWKV_WIDTH:2 * RWKV_WIDTH]
    vr = zs[:, 2 * RWKV_WIDTH:]

    col = lax.broadcasted_iota(jnp.int32, los.shape, 1)
    gate_cols = (col >= DECAY_LORA + AAA_LORA) & (col < LORA_WIDTH)
    act = jnp.where(col < DECAY_LORA, jnp.tanh(los), jnp.where(gate_cols, jax.nn.sigmoid(los), los))
    act = act.astype(BF16)
    wa_in = act[:, :LANES]
    dw = jnp.dot(wa_in, wup_ref[...], preferred_element_type=F32)
    da = jnp.dot(wa_in, aup_ref[...], preferred_element_type=F32)
    g = jnp.dot(act[:, LANES:3 * LANES], gup_ref[...], preferred_element_type=F32)

    w0, a0, k_k, k_a, v0 = (pv_ref[i:i + 1, :] for i in range(5))
    u = -(w0 + dw)
    softplus = jnp.maximum(u, 0.0) + jnp.log1p(jnp.exp(-jnp.abs(u)))
    w_log = -softplus - 0.5
    a = jax.nn.sigmoid(a0 + da)
    if has_vres:
        dv = jnp.dot(act[:, 2 * LANES:3 * LANES], vup_ref[...], preferred_element_type=F32)
        vr = vr + (vf_ref[...] - vr) * jax.nn.sigmoid(v0 + dv)

    kk0 = kr * k_k
    ss = _dot_split_lhs(kk0 * kk0, e_ref[...], 2)
    inv = 1.0 / jnp.maximum(jnp.sqrt(ss), 1e-12)
    kk = kk0 * _dot_split_lhs(inv, et_ref[...], 2)

    r_out[...] = r
    lw_out[...] = -jnp.exp(w_log)
    k_out[...] = kr * (1.0 + (a - 1.0) * k_a)
    v_out[...] = vr
    al_out[...] = -kk
    be_out[...] = kk * a
    g_out[...] = g


def _rwkv_pre(rkv, lora, v_first, mix_r, mix_l, pvec, wup, aup, gup, vup, seq, tt=256):
    n = rkv.shape[0]
    has_vres = v_first is not None
    e, et = _head_indicator()
    row = lambda i: (i, 0)
    prev8 = lambda i: (jnp.maximum(i * (tt // SUBLANES) - 1, 0), 0)
    const = lambda i: (0, 0)
    in_specs = [
        pl.BlockSpec((tt, RKV_COLS), row),
        pl.BlockSpec((SUBLANES, RKV_COLS), prev8),
        pl.BlockSpec((tt, LORA_COLS), row),
        pl.BlockSpec((SUBLANES, LORA_COLS), prev8),
    ]
    args = [rkv, rkv, lora, lora]
    if has_vres:
        in_specs.append(pl.BlockSpec((tt, RWKV_WIDTH), row))
        args.append(v_first)
    in_specs += [
        pl.BlockSpec((1, RKV_COLS), const),
        pl.BlockSpec((1, LORA_COLS), const),
        pl.BlockSpec((SUBLANES, RWKV_WIDTH), const),
        pl.BlockSpec((LANES, RWKV_WIDTH), const),
        pl.BlockSpec((LANES, RWKV_WIDTH), const),
        pl.BlockSpec((2 * LANES, RWKV_WIDTH), const),
        pl.BlockSpec((LANES, RWKV_WIDTH), const),
        pl.BlockSpec((RWKV_WIDTH, LANES), const),
        pl.BlockSpec((LANES, RWKV_WIDTH), const),
    ]
    args += [mix_r, mix_l, pvec, wup, aup, gup, vup, e, et]
    out = jax.ShapeDtypeStruct((n, RWKV_WIDTH), F32)
    return pl.pallas_call(
        functools.partial(_rwkv_pre_kernel, has_vres=has_vres, tiles_per_seq=seq // tt),
        grid=(n // tt,),
        in_specs=in_specs,
        out_specs=[pl.BlockSpec((tt, RWKV_WIDTH), row)] * 7,
        out_shape=[out] * 7,
        compiler_params=_cparams("parallel"),
        name="rwkv_pre",
    )(*args)


def _scan_kernel(r_ref, lw_ref, k_ref, v_ref, al_ref, be_ref, y_ref, s_scr):
    L = SCAN_CHUNK

    @pl.when(pl.program_id(1) == 0)
    def _():
        s_scr[...] = jnp.zeros_like(s_scr)

    row = lax.broadcasted_iota(jnp.int32, (L, L), 0)
    col = lax.broadcasted_iota(jnp.int32, (L, L), 1)
    strict = row > col
    incl = row >= col
    tri = incl.astype(BF16)
    eye = (row == col).astype(F32)

    lw = lw_ref[...]
    cum = _dot_split_rhs(tri, lw, 3)
    p_inc = jnp.exp(cum)
    p_exc = jnp.exp(cum - lw)
    p_inv = jnp.exp(-cum)
    p_last = p_inc[L - 1:L, :]
    a_t = al_ref[...] * p_exc
    r_t = r_ref[...] * p_inc
    b_t = be_ref[...] * p_inv
    k_t = k_ref[...] * p_inv
    b_end = b_t * p_last
    k_end = k_t * p_last
    v = v_ref[...]

    heads = range(N_RWKV_HEADS)
    sls = [slice(h * RWKV_HEAD, (h + 1) * RWKV_HEAD) for h in heads]
    ar = [jnp.concatenate([a_t[:, sl], r_t[:, sl]], axis=0) for sl in sls]
    bk = [jnp.concatenate([b_t[:, sl], k_t[:, sl]], axis=0) for sl in sls]
    gm = [_bdot_nt(ar[h], bk[h]) for h in heads]
    s0 = [s_scr[h] for h in heads]
    vh = [v[:, sl] for sl in sls]
    xs = [_bdot_nt(ar[h], s0[h]) for h in heads]
    a_ab = [jnp.where(strict, g[:L, :L], 0.0) for g in gm]
    a_rb = [jnp.where(incl, g[L:, :L], 0.0) for g in gm]
    a_kk = [jnp.concatenate([jnp.where(strict, g[:L, L:], 0.0), jnp.where(incl, g[L:, L:], 0.0)], axis=0)
            for g in gm]
    av = [_bdot(a_kk[h], vh[h]) for h in heads]

    pw = [_bdot(n, n) for n in a_ab]
    tinv = [eye + n for n in a_ab]
    span = 2
    while 2 * span < L:
        both = [_bdot(jnp.concatenate([pw[h], tinv[h]], axis=0), pw[h]) for h in heads]
        tinv = [tinv[h] + both[h][L:] for h in heads]
        pw = [b[:L] for b in both]
        span *= 2
    tinv = [tinv[h] + _bdot(tinv[h], pw[h]) for h in heads]

    u = [_bdot(tinv[h], xs[h][:L] + av[h][:L]) for h in heads]
    for h in heads:
        y_ref[:, sls[h]] = xs[h][L:] + av[h][L:] + _bdot(a_rb[h], u[h])
    for h in heads:
        uv = jnp.concatenate([u[h], vh[h]], axis=0)
        bke = jnp.concatenate([b_end[:, sls[h]], k_end[:, sls[h]]], axis=0)
        s_scr[h] = s0[h] * p_last[:, sls[h]] + _bdot_tn(uv, bke)


def _rwkv_scan(r, lw, k, v, al, be, seq):
    n = r.shape[0]
    nc = seq // SCAN_CHUNK
    spec = pl.BlockSpec((SCAN_CHUNK, RWKV_WIDTH), lambda b, c: (b * nc + c, 0))
    return pl.pallas_call(
        _scan_kernel,
        grid=(n // seq, nc),
        in_specs=[spec] * 6,
        out_specs=spec,
        out_shape=jax.ShapeDtypeStruct((n, RWKV_WIDTH), F32),
        scratch_shapes=[pltpu.VMEM((N_RWKV_HEADS, RWKV_HEAD, RWKV_HEAD), F32)],
        compiler_params=_cparams("parallel", "arbitrary"),
        name="rwkv_scan",
    )(r, lw, k, v, al, be)


def _rwkv_post_kernel(y_ref, r_ref, k_ref, v_ref, g_ref, pv_ref, e_ref, et_ref, o_ref):
    e, et = e_ref[...], et_ref[...]
    ln_w, ln_b, r_k = (pv_ref[i:i + 1, :] for i in range(3))
    y = y_ref[...]
    inv_n = 1.0 / RWKV_HEAD
    mu = _dot_split_lhs(y, e, 2) * inv_n
    d = y - _dot_split_lhs(mu, et, 2)
    var = _dot_split_lhs(d * d, e, 2) * inv_n
    rstd = lax.rsqrt(var + RWKV_GN_EPS)
    yn = d * _dot_split_lhs(rstd, et, 2) * ln_w + ln_b
    v = v_ref[...]
    coef = _dot_split_lhs(r_ref[...] * k_ref[...] * r_k, e, 2)
    bonus = _dot_split_lhs(coef, et, 2) * v
    o_ref[...] = ((yn + bonus) * g_ref[...]).astype(BF16)


def _rwkv_post(y, r, k, v, g, pvec, tt=256):
    n = y.shape[0]
    e, et = _head_indicator()
    row = lambda i: (i, 0)
    const = lambda i: (0, 0)
    return pl.pallas_call(
        _rwkv_post_kernel,
        grid=(n // tt,),
        in_specs=[pl.BlockSpec((tt, RWKV_WIDTH), row)] * 5 + [
            pl.BlockSpec((SUBLANES, RWKV_WIDTH), const),
            pl.BlockSpec((RWKV_WIDTH, LANES), const),
            pl.BlockSpec((LANES, RWKV_WIDTH), const),
        ],
        out_specs=pl.BlockSpec((tt, RWKV_WIDTH), row),
        out_shape=jax.ShapeDtypeStruct((n, RWKV_WIDTH), BF16),
        compiler_params=_cparams("parallel"),
        name="rwkv_post",
    )(y, r, k, v, g, pvec, e, et)


def _merge_kernel(a_ref, y_ref, wa_ref, wb_ref, ga_ref, gb_ref, o_ref):
    pa = jnp.dot(a_ref[...], wa_ref[...], preferred_element_type=F32)
    pb = jnp.dot(y_ref[...], wb_ref[...], preferred_element_type=F32)
    ga = jax.nn.sigmoid(ga_ref[...].astype(F32))
    gb = jax.nn.sigmoid(gb_ref[...].astype(F32))
    o_ref[...] = (ga * pa + gb * pb).astype(BF16)


def _merge(attn, y, wa, wb, gates, tm=1024, tn=512):
    n = attn.shape[0]
    nj = D_MODEL // tn
    return pl.pallas_call(
        _merge_kernel,
        grid=(n // tm, nj),
        in_specs=[
            pl.BlockSpec((tm, ATTN_WIDTH), lambda i, j: (i, 0)),
            pl.BlockSpec((tm, RWKV_WIDTH), lambda i, j: (i, 0)),
            pl.BlockSpec((ATTN_WIDTH, tn), lambda i, j: (0, j)),
            pl.BlockSpec((RWKV_WIDTH, tn), lambda i, j: (0, j)),
            pl.BlockSpec((tm, tn), lambda i, j: (i, j)),
            pl.BlockSpec((tm, tn), lambda i, j: (i, j + nj)),
        ],
        out_specs=pl.BlockSpec((tm, tn), lambda i, j: (i, j)),
        out_shape=jax.ShapeDtypeStruct((n, D_MODEL), BF16),
        compiler_params=_cparams("parallel", "arbitrary"),
        name="gated_merge",
    )(attn, y, wa, wb, gates, gates)


def _outproj_router_kernel(m_ref, wo_ref, x_ref, nw_ref, wr_ref, br_ref,
                           xo_ref, hf_ref, slab_ref, cnt_ref, run_scr):
    @pl.when(pl.program_id(0) == 0)
    def _():
        run_scr[...] = jnp.zeros_like(run_scr)

    xn = x_ref[...] + jnp.dot(m_ref[...], wo_ref[...], preferred_element_type=F32)
    xo_ref[...] = xn
    hf = _rms(xn, nw_ref[...])
    _store_rows(hf_ref, hf)

    h_hi = hf.astype(BF16)
    h_lo = (hf - h_hi.astype(F32)).astype(BF16)
    wr = wr_ref[...]
    w_hi = wr.astype(BF16)
    w_lo = (wr - w_hi.astype(F32)).astype(BF16)
    lg = (jnp.dot(h_hi, w_hi, preferred_element_type=F32)
          + jnp.dot(h_hi, w_lo, preferred_element_type=F32)
          + jnp.dot(h_lo, w_hi, preferred_element_type=F32)) + br_ref[...]

    tm = lg.shape[0]
    lane = lax.broadcasted_iota(jnp.int32, lg.shape, 1).astype(F32)
    neg = -jnp.inf
    big = float(ROUTER_LANES)

    def first_argmax(vals):
        mx = jnp.max(vals, axis=-1, keepdims=True)
        idx = jnp.min(jnp.where(vals == mx, lane, big), axis=-1, keepdims=True)
        return mx, idx

    is_group = lane < N_GROUPS
    gmax, gsel = first_argmax(jnp.where(is_group, lg, neg))
    gp = 1.0 / jnp.sum(jnp.where(is_group, jnp.exp(lg - gmax), 0.0), axis=-1, keepdims=True)
    lo_lane = EXPERT_LANE0 + EXPERTS_PER_GROUP * gsel
    in_group = (lane >= lo_lane) & (lane < lo_lane + EXPERTS_PER_GROUP)
    el = jnp.where(in_group, lg, neg)
    v1, i1 = first_argmax(el)
    v2, i2 = first_argmax(jnp.where(lane == i1, neg, el))
    e21 = jnp.exp(v2 - v1)
    ew1 = gp / (1.0 + e21)
    ew2 = gp * e21 / (1.0 + e21)

    oh1 = lane == i1
    oh2 = lane == i2
    cnt = oh1.astype(F32) + oh2.astype(F32)
    r_i = lax.broadcasted_iota(jnp.int32, (tm, tm), 0)
    c_i = lax.broadcasted_iota(jnp.int32, (tm, tm), 1)
    before = jnp.dot((r_i > c_i).astype(BF16), cnt.astype(BF16), preferred_element_type=F32)
    tot = before + run_scr[0:1, :]
    rank1 = jnp.sum(jnp.where(oh1, tot, 0.0), axis=-1, keepdims=True)
    rank2 = jnp.sum(jnp.where(oh2, tot, 0.0), axis=-1, keepdims=True)
    run = run_scr[0:1, :] + jnp.sum(cnt, axis=0, keepdims=True)
    run_scr[...] = jnp.broadcast_to(run, run_scr.shape)
    cnt_ref[...] = jnp.broadcast_to(run, cnt_ref.shape)

    slab = jnp.where(lane == 0, i1 - EXPERT_LANE0,
           jnp.where(lane == 1, i2 - EXPERT_LANE0,
           jnp.where(lane == 2, ew1,
           jnp.where(lane == 3, ew2,
           jnp.where(lane == 4, rank1,
           jnp.where(lane == 5, rank2, 0.0))))))
    slab_ref[...] = slab


def _outproj_router(merged, w_out, x, norm_w, w_router, b_router, tm=256):
    n = x.shape[0]
    row = lambda i: (i, 0)
    const = lambda i: (0, 0)
    return pl.pallas_call(
        _outproj_router_kernel,
        grid=(n // tm,),
        in_specs=[
            pl.BlockSpec((tm, D_MODEL), row),
            pl.BlockSpec((D_MODEL, D_MODEL), const),
            pl.BlockSpec((tm, D_MODEL), row),
            pl.BlockSpec((1, D_MODEL), const),
            pl.BlockSpec((D_MODEL, ROUTER_LANES), const),
            pl.BlockSpec((1, ROUTER_LANES), const),
        ],
        out_specs=[
            pl.BlockSpec((tm, D_MODEL), row),
            pl.BlockSpec((tm, ROW_SUB, LANES), lambda i: (i, 0, 0)),
            pl.BlockSpec((tm, ROUTER_LANES), row),
            pl.BlockSpec((SUBLANES, ROUTER_LANES), const),
        ],
        out_shape=[
            jax.ShapeDtypeStruct((n, D_MODEL), F32),
            jax.ShapeDtypeStruct((n, ROW_SUB, LANES), F32),
            jax.ShapeDtypeStruct((n, ROUTER_LANES), F32),
            jax.ShapeDtypeStruct((SUBLANES, ROUTER_LANES), F32),
        ],
        scratch_shapes=[pltpu.VMEM((SUBLANES, ROUTER_LANES), F32)],
        compiler_params=_cparams("arbitrary"),
        name="outproj_router",
    )(merged, w_out, x, norm_w, w_router, b_router)


def _dispatch_kernel(slot_ref, hf_ref, xb_in_hbm, xb_hbm, sem, *, tm):
    del xb_in_hbm

    def copy(u, j):
        return pltpu.make_async_copy(hf_ref.at[pl.ds(u, 1)],
                                     xb_hbm.at[pl.ds(slot_ref[TOP_K * u + j], 1)], sem)

    def start(u, c):
        for j in range(TOP_K):
            copy(u, j).start()
        return c

    def wait(u, c):
        for j in range(TOP_K):
            copy(u, j).wait()
        return c

    lax.fori_loop(0, tm, start, 0, unroll=8)
    lax.fori_loop(0, tm, wait, 0, unroll=8)


def _dispatch(slot_flat, hf, cap, tm=128):
    n = hf.shape[0]
    xb0 = jnp.zeros((cap, ROW_SUB, LANES), F32)
    return pl.pallas_call(
        functools.partial(_dispatch_kernel, tm=tm),
        grid=(n // tm,),
        in_specs=[
            pl.BlockSpec((TOP_K * tm,), lambda i: (i,), memory_space=pltpu.SMEM),
            pl.BlockSpec((tm, ROW_SUB, LANES), lambda i: (i, 0, 0)),
            pl.BlockSpec(memory_space=pl.ANY),
        ],
        out_specs=pl.BlockSpec(memory_space=pl.ANY),
        out_shape=jax.ShapeDtypeStruct((cap, ROW_SUB, LANES), F32),
        scratch_shapes=[pltpu.SemaphoreType.DMA(())],
        input_output_aliases={2: 0},
        compiler_params=_cparams("arbitrary"),
        name="moe_dispatch",
    )(slot_flat, hf, xb0)


def _ffn_kernel(be_ref, nu_ref, x_ref, wg_ref, wu_ref, wd_ref, o_ref, wg_s, wu_s, wd_s):
    b = pl.program_id(0)
    used = b < nu_ref[0]
    prev_e = be_ref[jnp.maximum(b - 1, 0)]
    new_expert = jnp.logical_or(b == 0, be_ref[b] != prev_e)

    @pl.when(jnp.logical_and(used, new_expert))
    def _():
        wg_s[...] = wg_ref[...].astype(BF16)
        wu_s[...] = wu_ref[...].astype(BF16)
        wd_s[...] = wd_ref[...].astype(BF16)

    @pl.when(used)
    def _():
        x = _rows_to_2d(x_ref).astype(BF16)
        gt = jnp.dot(x, wg_s[...], preferred_element_type=F32)
        up = jnp.dot(x, wu_s[...], preferred_element_type=F32)
        hid = (gt * jax.nn.sigmoid(gt) * up).astype(BF16)
        _store_rows(o_ref, jnp.dot(hid, wd_s[...], preferred_element_type=F32))

    @pl.when(jnp.logical_not(used))
    def _():
        o_ref[...] = jnp.zeros_like(o_ref)


def _expert_ffn(layer, block_expert, n_used, xb, w_gate, w_up, w_down):
    cap = xb.shape[0]
    nblk = cap // MOE_ROWS
    wmap = lambda b, be, nu: (layer, be[b], 0, 0)
    grid_spec = pltpu.PrefetchScalarGridSpec(
        num_scalar_prefetch=2,
        grid=(nblk,),
        in_specs=[
            pl.BlockSpec((MOE_ROWS, ROW_SUB, LANES), lambda b, be, nu: (b, 0, 0)),
            pl.BlockSpec((None, None, D_MODEL, EXPERT_HIDDEN), wmap),
            pl.BlockSpec((None, None, D_MODEL, EXPERT_HIDDEN), wmap),
            pl.BlockSpec((None, None, EXPERT_HIDDEN, D_MODEL), wmap),
        ],
        out_specs=pl.BlockSpec((MOE_ROWS, ROW_SUB, LANES), lambda b, be, nu: (b, 0, 0)),
        scratch_shapes=[
            pltpu.VMEM((D_MODEL, EXPERT_HIDDEN), BF16),
            pltpu.VMEM((D_MODEL, EXPERT_HIDDEN), BF16),
            pltpu.VMEM((EXPERT_HIDDEN, D_MODEL), BF16),
        ],
    )
    return pl.pallas_call(
        _ffn_kernel,
        grid_spec=grid_spec,
        out_shape=jax.ShapeDtypeStruct((cap, ROW_SUB, LANES), F32),
        compiler_params=_cparams("arbitrary"),
        name="expert_ffn",
    )(block_expert, n_used, xb, w_gate, w_up, w_down)


def _combine_kernel(slot_ref, x_ref, slab_ref, fw_ref, yb_hbm, o_ref, ybuf, sem, *, tm, final_norm):
    def copy(u, j):
        return pltpu.make_async_copy(yb_hbm.at[pl.ds(slot_ref[TOP_K * u + j], 1)],
                                     ybuf.at[j, pl.ds(u, 1)], sem)

    def start(u, c):
        for j in range(TOP_K):
            copy(u, j).start()
        return c

    def wait(u, c):
        for j in range(TOP_K):
            copy(u, j).wait()
        return c

    lax.fori_loop(0, tm, start, 0, unroll=8)
    lax.fori_loop(0, tm, wait, 0, unroll=8)

    slab = slab_ref[...]
    w1 = slab[:, 2:3]
    w2 = slab[:, 3:4]
    out = x_ref[...] + (_rows_to_2d(ybuf, 0) * w1 + _rows_to_2d(ybuf, 1) * w2)
    if final_norm:
        out = _rms(out, fw_ref[...])
    o_ref[...] = out


def _combine(slot_flat, x, slab, final_w, yb, final_norm, tm=128):
    n = x.shape[0]
    row = lambda i: (i, 0)
    return pl.pallas_call(
        functools.partial(_combine_kernel, tm=tm, final_norm=final_norm),
        grid=(n // tm,),
        in_specs=[
            pl.BlockSpec((TOP_K * tm,), lambda i: (i,), memory_space=pltpu.SMEM),
            pl.BlockSpec((tm, D_MODEL), row),
            pl.BlockSpec((tm, ROUTER_LANES), row),
            pl.BlockSpec((1, D_MODEL), lambda i: (0, 0)),
            pl.BlockSpec(memory_space=pl.ANY),
        ],
        out_specs=pl.BlockSpec((tm, D_MODEL), row),
        out_shape=jax.ShapeDtypeStruct((n, D_MODEL), F32),
        scratch_shapes=[pltpu.VMEM((TOP_K, tm, ROW_SUB, LANES), F32), pltpu.SemaphoreType.DMA(())],
        compiler_params=_cparams("arbitrary"),
        name="moe_combine",
    )(slot_flat, x, slab, final_w, yb)


def _pad_rows(w, rows, offset=0):
    out = jnp.zeros((rows, w.shape[1]), BF16)
    return out.at[offset:offset + w.shape[0]].set(w.astype(BF16))


def _pad_cols(vec, cols, offset=0):
    out = jnp.zeros((cols,), F32)
    return out.at[offset:offset + vec.shape[0]].set(vec.astype(F32))


def _moe_plan(slab, counts_row, n_tokens):
    eid = slab[:, 0:TOP_K].astype(jnp.int32)
    rank = slab[:, 4:4 + TOP_K].astype(jnp.int32)
    counts = counts_row[EXPERT_LANE0:EXPERT_LANE0 + N_EXPERTS].astype(jnp.int32)
    padded = (counts + MOE_ROWS - 1) // MOE_ROWS * MOE_ROWS
    pad_ends = jnp.cumsum(padded)
    pad_starts = pad_ends - padded
    slot = (pad_starts[eid] + rank).reshape(-1)
    nblk = n_tokens * TOP_K // MOE_ROWS + N_EXPERTS
    block_start = jnp.arange(nblk, dtype=jnp.int32) * MOE_ROWS
    block_expert = jnp.minimum(
        jnp.sum((pad_ends[None, :] <= block_start[:, None]).astype(jnp.int32), axis=1),
        N_EXPERTS - 1).astype(jnp.int32)
    n_used = (pad_ends[-1:] // MOE_ROWS).astype(jnp.int32)
    return slot, block_expert, n_used, nblk * MOE_ROWS


def kernel(x, attn_norm_w, w_in, shift_mix, attn_sinks, rwkv_w0, rwkv_w_up, rwkv_a0, rwkv_a_up, rwkv_g_up, rwkv_k_k, rwkv_k_a, rwkv_r_k, rwkv_ln_w, rwkv_ln_b, vres_down, vres_mix, vres_up, vres_v0, w_branch_a, w_branch_b, w_out, ffn_norm_w, router_group_w, router_group_b, router_expert_w, router_expert_b, expert_w_gate, expert_w_up, expert_w_down, final_norm_w):
    batch, seq, _ = x.shape
    n = batch * seq
    depth = w_in.shape[0]
    xf = x.reshape(n, D_MODEL)
    z_off = QKV_COLS
    g_off = QKV_COLS + RKV_COLS + LORA_WIDTH
    v_first = None
    for i in range(depth):
        has_vres = i > 0
        wi = w_in[i]
        vd = vres_down[i - 1] if has_vres else jnp.zeros((D_MODEL, MV_LORA), F32)
        w_lora = jnp.concatenate(
            [wi[:, z_off + RKV_COLS:g_off], vd,
             jnp.zeros((D_MODEL, LORA_COLS - LORA_WIDTH - MV_LORA), F32)], axis=1)
        w_all = jnp.concatenate([wi[:, :z_off + RKV_COLS], w_lora, wi[:, g_off:]], axis=1).astype(BF16)
        mix_r = shift_mix[i, :RKV_COLS].reshape(1, RKV_COLS)
        vmix = vres_mix[i - 1] if has_vres else jnp.zeros((MV_LORA,), F32)
        mix_l = _pad_cols(jnp.concatenate([shift_mix[i, RKV_COLS:], vmix]), LORA_COLS).reshape(1, LORA_COLS)
        v0 = vres_v0[i - 1] if has_vres else jnp.zeros((RWKV_WIDTH,), F32)
        zero = jnp.zeros((RWKV_WIDTH,), F32)
        pv_pre = jnp.stack([rwkv_w0[i], rwkv_a0[i], rwkv_k_k[i], rwkv_k_a[i], v0, zero, zero, zero]).astype(F32)
        pv_post = jnp.stack([rwkv_ln_w[i], rwkv_ln_b[i], rwkv_r_k[i].reshape(-1),
                             zero, zero, zero, zero, zero]).astype(F32)
        wup = _pad_rows(rwkv_w_up[i], LANES, 0)
        aup = _pad_rows(rwkv_a_up[i], LANES, DECAY_LORA)
        gup = _pad_rows(rwkv_g_up[i], 2 * LANES, 0)
        vup_w = vres_up[i - 1] if has_vres else jnp.zeros((MV_LORA, RWKV_WIDTH), F32)
        vup = _pad_rows(vup_w, LANES, LORA_WIDTH - 2 * LANES)
        w_router = jnp.concatenate(
            [router_group_w[i], router_expert_w[i],
             jnp.zeros((D_MODEL, ROUTER_LANES - N_GROUPS - N_EXPERTS), F32)], axis=1)
        b_router = _pad_cols(jnp.concatenate([router_group_b[i], router_expert_b[i]]),
                             ROUTER_LANES).reshape(1, ROUTER_LANES)

        qkv, rkv, lora, gates = _inproj(xf, attn_norm_w[i].reshape(1, D_MODEL), w_all)
        attn = _attention(qkv, attn_sinks[i], seq)
        r, lw, k2, v, al, be, g = _rwkv_pre(rkv, lora, v_first, mix_r, mix_l, pv_pre,
                                            wup, aup, gup, vup, seq)
        if not has_vres:
            v_first = v
        y_raw = _rwkv_scan(r, lw, k2, v, al, be, seq)
        y = _rwkv_post(y_raw, r, k2, v, g, pv_post)
        merged = _merge(attn, y, w_branch_a[i].astype(BF16), w_branch_b[i].astype(BF16), gates)

        x_mid, hf, slab, counts = _outproj_router(
            merged, w_out[i].astype(BF16), xf, ffn_norm_w[i].reshape(1, D_MODEL), w_router, b_router)
        slot, block_expert, n_used, cap = _moe_plan(slab, counts[0], n)
        xb = _dispatch(slot, hf, cap)
        yb = _expert_ffn(i, block_expert, n_used, xb, expert_w_gate, expert_w_up, expert_w_down)
        xf = _combine(slot, x_mid, slab, final_norm_w.reshape(1, D_MODEL), yb, final_norm=(i == depth - 1))
    return xf.reshape(batch, seq, D_MODEL)
```

```python
import functools

import jax
import jax.numpy as jnp
from jax import lax
from jax.experimental import pallas as pl
from jax.experimental.pallas import tpu as pltpu

F32 = jnp.float32
BF16 = jnp.bfloat16

D_MODEL = 2048
HEAD_DIM = 64
N_Q_HEADS = 16
N_KV_HEADS = 4
GQA_GROUP = N_Q_HEADS // N_KV_HEADS
ATTN_WIDTH = N_Q_HEADS * HEAD_DIM
KV_WIDTH = N_KV_HEADS * HEAD_DIM
WINDOW = 128
ATTN_BLOCK = 128

RWKV_HEAD = 64
N_RWKV_HEADS = 16
RWKV_WIDTH = N_RWKV_HEADS * RWKV_HEAD
DECAY_LORA = 64
AAA_LORA = 64
MV_LORA = 32
GATE_LORA = 160
RWKV_GN_EPS = 64e-5
LORA_WIDTH = DECAY_LORA + AAA_LORA + GATE_LORA

N_GROUPS = 4
EXPERTS_PER_GROUP = 8
N_EXPERTS = N_GROUPS * EXPERTS_PER_GROUP
TOP_K = 2
EXPERT_HIDDEN = D_MODEL // 4
NORM_EPS = 1e-5

LANES = 128
SUBLANES = 8

SEG = 512
QKV_COLS = ATTN_WIDTH + 2 * KV_WIDTH
RKV_COLS = 3 * RWKV_WIDTH
LORA_COLS = SEG
GATE_COLS = 2 * D_MODEL
QKV_TILES = QKV_COLS // SEG
RKV_TILES = RKV_COLS // SEG
LORA_TILES = LORA_COLS // SEG
GATE_TILES = GATE_COLS // SEG
IN_TILES = QKV_TILES + RKV_TILES + LORA_TILES + GATE_TILES
FIRST_GATE_TILE = QKV_TILES + RKV_TILES + LORA_TILES
GATE_COL0 = QKV_COLS + RKV_COLS + LORA_WIDTH

SCAN_CHUNK = 64
MOE_ROWS = 256
ROUTER_LANES = LANES
EXPERT_LANE0 = N_GROUPS

ROW_SUB = D_MODEL // LANES

VMEM_LIMIT = 56 * 1024 * 1024


def _rows_to_2d(ref3, *lead):
    return jnp.concatenate([ref3[lead + (slice(None), s, slice(None))] for s in range(ROW_SUB)], axis=1)


def _store_rows(ref3, val):
    for s in range(ROW_SUB):
        ref3[:, s, :] = val[:, s * LANES:(s + 1) * LANES]


def _cparams(*sem):
    return pltpu.CompilerParams(dimension_semantics=sem, vmem_limit_bytes=VMEM_LIMIT)


def _bdot(a, b):
    return jnp.dot(a.astype(BF16), b.astype(BF16), preferred_element_type=F32)


def _bdot_nt(a, b):
    return lax.dot_general(a.astype(BF16), b.astype(BF16), (((1,), (1,)), ((), ())),
                           preferred_element_type=F32)


def _bdot_tn(a, b):
    return lax.dot_general(a.astype(BF16), b.astype(BF16), (((0,), (0,)), ((), ())),
                           preferred_element_type=F32)


def _dot_split_rhs(w01, x, passes):
    acc, rem = None, x
    for p in range(passes):
        part = rem.astype(BF16)
        d = jnp.dot(w01, part, preferred_element_type=F32)
        acc = d if acc is None else acc + d
        if p + 1 < passes:
            rem = rem - part.astype(F32)
    return acc


def _dot_split_lhs(x, w01, passes):
    acc, rem = None, x
    for p in range(passes):
        part = rem.astype(BF16)
        d = jnp.dot(part, w01, preferred_element_type=F32)
        acc = d if acc is None else acc + d
        if p + 1 < passes:
            rem = rem - part.astype(F32)
    return acc


def _rms(x, w):
    ms = jnp.mean(x * x, axis=-1, keepdims=True)
    return x * lax.rsqrt(ms + NORM_EPS) * w


def _inproj_kernel(*refs, has_vres):
    refs = list(refs)
    x_ref, nw_ref, w_ref, wg_ref = refs[:4]
    refs = refs[4:]
    vd_ref = refs.pop(0) if has_vres else None
    qkv_ref, rkv_ref, lora_ref, gate_ref = refs[:4]
    refs = refs[4:]
    zv_ref = refs.pop(0) if has_vres else None
    (h_scr,) = refs
    j = pl.program_id(1)

    @pl.when(j == 0)
    def _():
        h_scr[...] = _rms(x_ref[...], nw_ref[...]).astype(BF16)
        if has_vres:
            zv_ref[...] = jnp.dot(h_scr[...], vd_ref[...], preferred_element_type=F32)

    @pl.when(j < FIRST_GATE_TILE)
    def _():
        acc = jnp.dot(h_scr[...], w_ref[...].astype(BF16), preferred_element_type=F32)

        @pl.when(j < QKV_TILES)
        def _():
            qkv_ref[...] = acc.astype(BF16)

        @pl.when((j >= QKV_TILES) & (j < QKV_TILES + RKV_TILES))
        def _():
            rkv_ref[...] = acc

        @pl.when(j == QKV_TILES + RKV_TILES)
        def _():
            lora_ref[...] = acc

    @pl.when(j >= FIRST_GATE_TILE)
    def _():
        gate_ref[...] = jnp.dot(h_scr[...], wg_ref[...], preferred_element_type=F32).astype(BF16)


def _inproj(layer, x, norm_w, w_in, w_gates, vd, tm=1024):
    n = x.shape[0]
    has_vres = vd is not None
    s_rkv = QKV_TILES
    in_specs = [
        pl.BlockSpec((tm, D_MODEL), lambda i, j: (i, 0)),
        pl.BlockSpec((1, D_MODEL), lambda i, j: (0, 0)),
        pl.BlockSpec((None, D_MODEL, SEG), lambda i, j: (layer, 0, jnp.minimum(j, FIRST_GATE_TILE - 1))),
        pl.BlockSpec((D_MODEL, SEG), lambda i, j: (0, jnp.clip(j - FIRST_GATE_TILE, 0, GATE_TILES - 1))),
    ]
    args = [x, norm_w, w_in, w_gates]
    out_specs = [
        pl.BlockSpec((tm, SEG), lambda i, j: (i, jnp.clip(j, 0, QKV_TILES - 1))),
        pl.BlockSpec((tm, SEG), lambda i, j: (i, jnp.clip(j - s_rkv, 0, RKV_TILES - 1))),
        pl.BlockSpec((tm, SEG), lambda i, j: (i, 0)),
        pl.BlockSpec((tm, SEG), lambda i, j: (i, jnp.clip(j - FIRST_GATE_TILE, 0, GATE_TILES - 1))),
    ]
    out_shape = [
        jax.ShapeDtypeStruct((n, QKV_COLS), BF16),
        jax.ShapeDtypeStruct((n, RKV_COLS), F32),
        jax.ShapeDtypeStruct((n, LORA_COLS), F32),
        jax.ShapeDtypeStruct((n, GATE_COLS), BF16),
    ]
    if has_vres:
        in_specs.append(pl.BlockSpec((D_MODEL, LANES), lambda i, j: (0, 0)))
        args.append(vd)
        out_specs.append(pl.BlockSpec((tm, LANES), lambda i, j: (i, 0)))
        out_shape.append(jax.ShapeDtypeStruct((n, LANES), F32))
    return pl.pallas_call(
        functools.partial(_inproj_kernel, has_vres=has_vres),
        grid=(n // tm, IN_TILES),
        in_specs=in_specs,
        out_specs=out_specs,
        out_shape=out_shape,
        scratch_shapes=[pltpu.VMEM((tm, D_MODEL), BF16)],
        compiler_params=_cparams("parallel", "arbitrary"),
        name="inproj",
    )(*args)


def _attn_kernel(sink_ref, q_ref, kp_ref, kc_ref, vp_ref, vc_ref, bias_ref, o_ref, *, nb):
    first = (pl.program_id(0) % nb) == 0
    col = lax.broadcasted_iota(jnp.int32, (ATTN_BLOCK, 2 * ATTN_BLOCK), 1)
    pad_keys = jnp.logical_and(first, col < ATTN_BLOCK)
    scale = HEAD_DIM ** -0.5
    for hk in range(N_KV_HEADS):
        ks = slice(hk * HEAD_DIM, (hk + 1) * HEAD_DIM)
        kw = jnp.concatenate([kp_ref[:, ks], kc_ref[:, ks]], axis=0)
        vw = jnp.concatenate([vp_ref[:, ks], vc_ref[:, ks]], axis=0)
        for g in range(GQA_GROUP):
            h = hk * GQA_GROUP + g
            qh = q_ref[:, h * HEAD_DIM:(h + 1) * HEAD_DIM]
            s = lax.dot_general(qh, kw, (((1,), (1,)), ((), ())), preferred_element_type=F32)
            s = s * scale + bias_ref[h]
            s = jnp.where(pad_keys, -jnp.inf, s)
            sink = sink_ref[h]
            m = jnp.maximum(jnp.max(s, axis=-1, keepdims=True), sink)
            p = jnp.exp(s - m)
            denom = jnp.sum(p, axis=-1, keepdims=True) + jnp.exp(sink - m)
            o = jnp.dot(p.astype(BF16), vw, preferred_element_type=F32) / denom
            o_ref[:, h * HEAD_DIM:(h + 1) * HEAD_DIM] = o.astype(BF16)


def _attn_bias():
    qi = jnp.arange(ATTN_BLOCK)[:, None]
    kj = jnp.arange(2 * ATTN_BLOCK)[None, :]
    dist = qi + ATTN_BLOCK - kj
    valid = (dist >= 0) & (dist < WINDOW)
    slopes = jnp.exp2(-8.0 * jnp.arange(1, N_Q_HEADS + 1, dtype=F32) / N_Q_HEADS)
    bias = -slopes[:, None, None] * dist.astype(F32)[None]
    return jnp.where(valid[None], bias, -jnp.inf)


def _attention(qkv, sinks, seq):
    n = qkv.shape[0]
    nb = seq // ATTN_BLOCK
    kcol = ATTN_WIDTH // KV_WIDTH
    prev = lambda i: jnp.where(i % nb == 0, i, i - 1)
    return pl.pallas_call(
        functools.partial(_attn_kernel, nb=nb),
        grid=(n // ATTN_BLOCK,),
        in_specs=[
            pl.BlockSpec(memory_space=pltpu.SMEM),
            pl.BlockSpec((ATTN_BLOCK, ATTN_WIDTH), lambda i: (i, 0)),
            pl.BlockSpec((ATTN_BLOCK, KV_WIDTH), lambda i: (prev(i), kcol)),
            pl.BlockSpec((ATTN_BLOCK, KV_WIDTH), lambda i: (i, kcol)),
            pl.BlockSpec((ATTN_BLOCK, KV_WIDTH), lambda i: (prev(i), kcol + 1)),
            pl.BlockSpec((ATTN_BLOCK, KV_WIDTH), lambda i: (i, kcol + 1)),
            pl.BlockSpec((N_Q_HEADS, ATTN_BLOCK, 2 * ATTN_BLOCK), lambda i: (0, 0, 0)),
        ],
        out_specs=pl.BlockSpec((ATTN_BLOCK, ATTN_WIDTH), lambda i: (i, 0)),
        out_shape=jax.ShapeDtypeStruct((n, ATTN_WIDTH), BF16),
        compiler_params=_cparams("parallel"),
        name="swa_attention",
    )(sinks.astype(F32), qkv, qkv, qkv, qkv, qkv, _attn_bias())


def _head_indicator():
    ch = jnp.arange(RWKV_WIDTH)[:, None] // RWKV_HEAD
    e = (ch == jnp.arange(LANES)[None, :]).astype(BF16)
    return e, e.T


def _rwkv_pre_kernel(*refs, has_vres, tiles_per_seq):
    refs = list(refs)
    rkv_ref, rkvp_ref, lo_ref, lop_ref = refs[:4]
    refs = refs[4:]
    if has_vres:
        vf_ref, zv_ref, zvp_ref, mixv_ref = refs[:4]
        refs = refs[4:]
    (mixr_ref, mixl_ref, pv_ref, wup_ref, aup_ref, gup_ref, vup_ref, e_ref, et_ref,
     r_out, lw_out, k_out, v_out, al_out, be_out, g_out) = refs

    first = (pl.program_id(0) % tiles_per_seq) == 0

    def shifted(z, zp_ref):
        prev_row = jnp.where(first, 0.0, zp_ref[SUBLANES - 1:SUBLANES, :])
        row = lax.broadcasted_iota(jnp.int32, z.shape, 0)
        return jnp.where(row == 0, prev_row, pltpu.roll(z, 1, 0))

    z = rkv_ref[...]
    zs = z + (shifted(z, rkvp_ref) - z) * mixr_ref[...]
    lo = lo_ref[...]
    los = lo + (shifted(lo, lop_ref) - lo) * mixl_ref[...]

    r = zs[:, :RWKV_WIDTH]
    kr = zs[:, RWKV_WIDTH:2 * RWKV_WIDTH]
    vr = zs[:, 2 * RWKV_WIDTH:]

    col = lax.broadcasted_iota(jnp.int32, los.shape, 1)
    gate_cols = (col >= DECAY_LORA + AAA_LORA) & (col < LORA_WIDTH)
    act = jnp.where(col < DECAY_LORA, jnp.tanh(los), jnp.where(gate_cols, jax.nn.sigmoid(los), los))
    act = act.astype(BF16)
    wa_in = act[:, :LANES]
    dw = jnp.dot(wa_in, wup_ref[...], preferred_element_type=F32)
    da = jnp.dot(wa_in, aup_ref[...], preferred_element_type=F32)
    g = jnp.dot(act[:, LANES:3 * LANES], gup_ref[...], preferred_element_type=F32)

    w0, a0, k_k, k_a, v0 = (pv_ref[i:i + 1, :] for i in range(5))
    u = -(w0 + dw)
    softplus = jnp.maximum(u, 0.0) + jnp.log1p(jnp.exp(-jnp.abs(u)))
    w_log = -softplus - 0.5
    a = jax.nn.sigmoid(a0 + da)
    if has_vres:
        zv = zv_ref[...]
        zvs = zv + (shifted(zv, zvp_ref) - zv) * mixv_ref[...]
        dv = jnp.dot(zvs.astype(BF16), vup_ref[...], preferred_element_type=F32)
        vr = vr + (vf_ref[...] - vr) * jax.nn.sigmoid(v0 + dv)

    kk0 = kr * k_k
    ss = _dot_split_lhs(kk0 * kk0, e_ref[...], 2)
    inv = 1.0 / jnp.maximum(jnp.sqrt(ss), 1e-12)
    kk = kk0 * _dot_split_lhs(inv, et_ref[...], 2)

    r_out[...] = r
    lw_out[...] = -jnp.exp(w_log)
    k_out[...] = kr * (1.0 + (a - 1.0) * k_a)
    v_out[...] = vr
    al_out[...] = -kk
    be_out[...] = kk * a
    g_out[...] = g


def _rwkv_pre(rkv, lora, v_first, zv, mix_r, mix_l, mix_v, pvec, wup, aup, gup, vup, seq, tt=256):
    n = rkv.shape[0]
    has_vres = v_first is not None
    e, et = _head_indicator()
    row = lambda i: (i, 0)
    prev8 = lambda i: (jnp.maximum(i * (tt // SUBLANES) - 1, 0), 0)
    const = lambda i: (0, 0)
    in_specs = [
        pl.BlockSpec((tt, RKV_COLS), row),
        pl.BlockSpec((SUBLANES, RKV_COLS), prev8),
        pl.BlockSpec((tt, LORA_COLS), row),
        pl.BlockSpec((SUBLANES, LORA_COLS), prev8),
    ]
    args = [rkv, rkv, lora, lora]
    if has_vres:
        in_specs += [
            pl.BlockSpec((tt, RWKV_WIDTH), row),
            pl.BlockSpec((tt, LANES), row),
            pl.BlockSpec((SUBLANES, LANES), prev8),
            pl.BlockSpec((1, LANES), const),
        ]
        args += [v_first, zv, zv, mix_v]
    in_specs += [
        pl.BlockSpec((1, RKV_COLS), const),
        pl.BlockSpec((1, LORA_COLS), const),
        pl.BlockSpec((SUBLANES, RWKV_WIDTH), const),
        pl.BlockSpec((LANES, RWKV_WIDTH), const),
        pl.BlockSpec((LANES, RWKV_WIDTH), const),
        pl.BlockSpec((2 * LANES, RWKV_WIDTH), const),
        pl.BlockSpec((LANES, RWKV_WIDTH), const),
        pl.BlockSpec((RWKV_WIDTH, LANES), const),
        pl.BlockSpec((LANES, RWKV_WIDTH), const),
    ]
    args += [mix_r, mix_l, pvec, wup, aup, gup, vup, e, et]
    out = jax.ShapeDtypeStruct((n, RWKV_WIDTH), F32)
    return pl.pallas_call(
        functools.partial(_rwkv_pre_kernel, has_vres=has_vres, tiles_per_seq=seq // tt),
        grid=(n // tt,),
        in_specs=in_specs,
        out_specs=[pl.BlockSpec((tt, RWKV_WIDTH), row)] * 7,
        out_shape=[out] * 7,
        compiler_params=_cparams("parallel"),
        name="rwkv_pre",
    )(*args)


def _scan_kernel(r_ref, lw_ref, k_ref, v_ref, al_ref, be_ref, y_ref, s_scr):
    L = SCAN_CHUNK

    @pl.when(pl.program_id(1) == 0)
    def _():
        s_scr[...] = jnp.zeros_like(s_scr)

    row = lax.broadcasted_iota(jnp.int32, (L, L), 0)
    col = lax.broadcasted_iota(jnp.int32, (L, L), 1)
    strict = row > col
    incl = row >= col
    tri = incl.astype(BF16)
    eye = (row == col).astype(F32)

    lw = lw_ref[...]
    cum = _dot_split_rhs(tri, lw, 3)
    p_inc = jnp.exp(cum)
    p_exc = jnp.exp(cum - lw)
    p_inv = jnp.exp(-cum)
    p_last = p_inc[L - 1:L, :]
    a_t = al_ref[...] * p_exc
    r_t = r_ref[...] * p_inc
    b_t = be_ref[...] * p_inv
    k_t = k_ref[...] * p_inv
    b_end = b_t * p_last
    k_end = k_t * p_last
    v = v_ref[...]

    heads = range(N_RWKV_HEADS)
    sls = [slice(h * RWKV_HEAD, (h + 1) * RWKV_HEAD) for h in heads]
    ar = [jnp.concatenate([a_t[:, sl], r_t[:, sl]], axis=0) for sl in sls]
    bk = [jnp.concatenate([b_t[:, sl], k_t[:, sl]], axis=0) for sl in sls]
    gm = [_bdot_nt(ar[h], bk[h]) for h in heads]
    s0 = [s_scr[h] for h in heads]
    vh = [v[:, sl] for sl in sls]
    xs = [_bdot_nt(ar[h], s0[h]) for h in heads]
    a_ab = [jnp.where(strict, g[:L, :L], 0.0) for g in gm]
    a_rb = [jnp.where(incl, g[L:, :L], 0.0) for g in gm]
    a_kk = [jnp.concatenate([jnp.where(strict, g[:L, L:], 0.0), jnp.where(incl, g[L:, L:], 0.0)], axis=0)
            for g in gm]
    av = [_bdot(a_kk[h], vh[h]) for h in heads]

    pw = [_bdot(n, n) for n in a_ab]
    tinv = [eye + n for n in a_ab]
    span = 2
    while 2 * span < L:
        both = [_bdot(jnp.concatenate([pw[h], tinv[h]], axis=0), pw[h]) for h in heads]
        tinv = [tinv[h] + both[h][L:] for h in heads]
        pw = [b[:L] for b in both]
        span *= 2
    tinv = [tinv[h] + _bdot(tinv[h], pw[h]) for h in heads]

    u = [_bdot(tinv[h], xs[h][:L] + av[h][:L]) for h in heads]
    for h in heads:
        y_ref[:, sls[h]] = xs[h][L:] + av[h][L:] + _bdot(a_rb[h], u[h])
    for h in heads:
        uv = jnp.concatenate([u[h], vh[h]], axis=0)
        bke = jnp.concatenate([b_end[:, sls[h]], k_end[:, sls[h]]], axis=0)
        s_scr[h] = s0[h] * p_last[:, sls[h]] + _bdot_tn(uv, bke)


def _rwkv_scan(r, lw, k, v, al, be, seq):
    n = r.shape[0]
    nc = seq // SCAN_CHUNK
    spec = pl.BlockSpec((SCAN_CHUNK, RWKV_WIDTH), lambda b, c: (b * nc + c, 0))
    return pl.pallas_call(
        _scan_kernel,
        grid=(n // seq, nc),
        in_specs=[spec] * 6,
        out_specs=spec,
        out_shape=jax.ShapeDtypeStruct((n, RWKV_WIDTH), F32),
        scratch_shapes=[pltpu.VMEM((N_RWKV_HEADS, RWKV_HEAD, RWKV_HEAD), F32)],
        compiler_params=_cparams("parallel", "arbitrary"),
        name="rwkv_scan",
    )(r, lw, k, v, al, be)


def _rwkv_post_kernel(y_ref, r_ref, k_ref, v_ref, g_ref, pv_ref, e_ref, et_ref, o_ref):
    e, et = e_ref[...], et_ref[...]
    ln_w, ln_b, r_k = (pv_ref[i:i + 1, :] for i in range(3))
    y = y_ref[...]
    inv_n = 1.0 / RWKV_HEAD
    mu = _dot_split_lhs(y, e, 2) * inv_n
    d = y - _dot_split_lhs(mu, et, 2)
    var = _dot_split_lhs(d * d, e, 2) * inv_n
    rstd = lax.rsqrt(var + RWKV_GN_EPS)
    yn = d * _dot_split_lhs(rstd, et, 2) * ln_w + ln_b
    v = v_ref[...]
    coef = _dot_split_lhs(r_ref[...] * k_ref[...] * r_k, e, 2)
    bonus = _dot_split_lhs(coef, et, 2) * v
    o_ref[...] = ((yn + bonus) * g_ref[...]).astype(BF16)


def _rwkv_post(y, r, k, v, g, pvec, tt=256):
    n = y.shape[0]
    e, et = _head_indicator()
    row = lambda i: (i, 0)
    const = lambda i: (0, 0)
    return pl.pallas_call(
        _rwkv_post_kernel,
        grid=(n // tt,),
        in_specs=[pl.BlockSpec((tt, RWKV_WIDTH), row)] * 5 + [
            pl.BlockSpec((SUBLANES, RWKV_WIDTH), const),
            pl.BlockSpec((RWKV_WIDTH, LANES), const),
            pl.BlockSpec((LANES, RWKV_WIDTH), const),
        ],
        out_specs=pl.BlockSpec((tt, RWKV_WIDTH), row),
        out_shape=jax.ShapeDtypeStruct((n, RWKV_WIDTH), BF16),
        compiler_params=_cparams("parallel"),
        name="rwkv_post",
    )(y, r, k, v, g, pvec, e, et)


def _merge_kernel(a_ref, y_ref, wa_ref, wb_ref, ga_ref, gb_ref, o_ref):
    pa = jnp.dot(a_ref[...], wa_ref[...].astype(BF16), preferred_element_type=F32)
    pb = jnp.dot(y_ref[...], wb_ref[...].astype(BF16), preferred_element_type=F32)
    ga = jax.nn.sigmoid(ga_ref[...].astype(F32))
    gb = jax.nn.sigmoid(gb_ref[...].astype(F32))
    o_ref[...] = (ga * pa + gb * pb).astype(BF16)


def _merge(layer, attn, y, wa, wb, gates, tm=1024, tn=512):
    n = attn.shape[0]
    nj = D_MODEL // tn
    return pl.pallas_call(
        _merge_kernel,
        grid=(n // tm, nj),
        in_specs=[
            pl.BlockSpec((tm, ATTN_WIDTH), lambda i, j: (i, 0)),
            pl.BlockSpec((tm, RWKV_WIDTH), lambda i, j: (i, 0)),
            pl.BlockSpec((None, ATTN_WIDTH, tn), lambda i, j: (layer, 0, j)),
            pl.BlockSpec((None, RWKV_WIDTH, tn), lambda i, j: (layer, 0, j)),
            pl.BlockSpec((tm, tn), lambda i, j: (i, j)),
            pl.BlockSpec((tm, tn), lambda i, j: (i, j + nj)),
        ],
        out_specs=pl.BlockSpec((tm, tn), lambda i, j: (i, j)),
        out_shape=jax.ShapeDtypeStruct((n, D_MODEL), BF16),
        compiler_params=_cparams("parallel", "arbitrary"),
        name="gated_merge",
    )(attn, y, wa, wb, gates, gates)


def _outproj_router_kernel(m_ref, wo_ref, x_ref, nw_ref, wr_ref, br_ref,
                           xo_ref, hf_ref, slab_ref, cnt_ref, run_scr):
    @pl.when(pl.program_id(0) == 0)
    def _():
        run_scr[...] = jnp.zeros_like(run_scr)

    xn = x_ref[...] + jnp.dot(m_ref[...], wo_ref[...], preferred_element_type=F32)
    xo_ref[...] = xn
    hf = _rms(xn, nw_ref[...])
    _store_rows(hf_ref, hf)

    h_hi = hf.astype(BF16)
    h_lo = (hf - h_hi.astype(F32)).astype(BF16)
    wr = wr_ref[...]
    w_hi = wr.astype(BF16)
    w_lo = (wr - w_hi.astype(F32)).astype(BF16)
    lg = (jnp.dot(h_hi, w_hi, preferred_element_type=F32)
          + jnp.dot(h_hi, w_lo, preferred_element_type=F32)
          + jnp.dot(h_lo, w_hi, preferred_element_type=F32)) + br_ref[...]

    tm = lg.shape[0]
    lane = lax.broadcasted_iota(jnp.int32, lg.shape, 1).astype(F32)
    neg = -jnp.inf
    big = float(ROUTER_LANES)

    def first_argmax(vals):
        mx = jnp.max(vals, axis=-1, keepdims=True)
        idx = jnp.min(jnp.where(vals == mx, lane, big), axis=-1, keepdims=True)
        return mx, idx

    is_group = lane < N_GROUPS
    gmax, gsel = first_argmax(jnp.where(is_group, lg, neg))
    gp = 1.0 / jnp.sum(jnp.where(is_group, jnp.exp(lg - gmax), 0.0), axis=-1, keepdims=True)
    lo_lane = EXPERT_LANE0 + EXPERTS_PER_GROUP * gsel
    in_group = (lane >= lo_lane) & (lane < lo_lane + EXPERTS_PER_GROUP)
    el = jnp.where(in_group, lg, neg)
    v1, i1 = first_argmax(el)
    v2, i2 = first_argmax(jnp.where(lane == i1, neg, el))
    e21 = jnp.exp(v2 - v1)
    ew1 = gp / (1.0 + e21)
    ew2 = gp * e21 / (1.0 + e21)

    oh1 = lane == i1
    oh2 = lane == i2
    cnt = oh1.astype(F32) + oh2.astype(F32)
    r_i = lax.broadcasted_iota(jnp.int32, (tm, tm), 0)
    c_i = lax.broadcasted_iota(jnp.int32, (tm, tm), 1)
    before = jnp.dot((r_i > c_i).astype(BF16), cnt.astype(BF16), preferred_element_type=F32)
    tot = before + run_scr[0:1, :]
    rank1 = jnp.sum(jnp.where(oh1, tot, 0.0), axis=-1, keepdims=True)
    rank2 = jnp.sum(jnp.where(oh2, tot, 0.0), axis=-1, keepdims=True)
    run = run_scr[0:1, :] + jnp.sum(cnt, axis=0, keepdims=True)
    run_scr[...] = jnp.broadcast_to(run, run_scr.shape)
    cnt_ref[...] = jnp.broadcast_to(run, cnt_ref.shape)

    slab = jnp.where(lane == 0, i1 - EXPERT_LANE0,
           jnp.where(lane == 1, i2 - EXPERT_LANE0,
           jnp.where(lane == 2, ew1,
           jnp.where(lane == 3, ew2,
           jnp.where(lane == 4, rank1,
           jnp.where(lane == 5, rank2, 0.0))))))
    slab_ref[...] = slab


def _outproj_router(merged, w_out, x, norm_w, w_router, b_router, tm=256):
    n = x.shape[0]
    row = lambda i: (i, 0)
    const = lambda i: (0, 0)
    return pl.pallas_call(
        _outproj_router_kernel,
        grid=(n // tm,),
        in_specs=[
            pl.BlockSpec((tm, D_MODEL), row),
            pl.BlockSpec((D_MODEL, D_MODEL), const),
            pl.BlockSpec((tm, D_MODEL), row),
            pl.BlockSpec((1, D_MODEL), const),
            pl.BlockSpec((D_MODEL, ROUTER_LANES), const),
            pl.BlockSpec((1, ROUTER_LANES), const),
        ],
        out_specs=[
            pl.BlockSpec((tm, D_MODEL), row),
            pl.BlockSpec((tm, ROW_SUB, LANES), lambda i: (i, 0, 0)),
            pl.BlockSpec((tm, ROUTER_LANES), row),
            pl.BlockSpec((SUBLANES, ROUTER_LANES), const),
        ],
        out_shape=[
            jax.ShapeDtypeStruct((n, D_MODEL), F32),
            jax.ShapeDtypeStruct((n, ROW_SUB, LANES), F32),
            jax.ShapeDtypeStruct((n, ROUTER_LANES), F32),
            jax.ShapeDtypeStruct((SUBLANES, ROUTER_LANES), F32),
        ],
        scratch_shapes=[pltpu.VMEM((SUBLANES, ROUTER_LANES), F32)],
        compiler_params=_cparams("arbitrary"),
        name="outproj_router",
    )(merged, w_out, x, norm_w, w_router, b_router)


def _dispatch_kernel(slot_ref, hf_ref, xb_in_hbm, xb_hbm, sem, *, tm):
    del xb_in_hbm

    def copy(u, j):
        return pltpu.make_async_copy(hf_ref.at[pl.ds(u, 1)],
                                     xb_hbm.at[pl.ds(slot_ref[TOP_K * u + j], 1)], sem)

    def start(u, c):
        for j in range(TOP_K):
            copy(u, j).start()
        return c

    def wait(u, c):
        for j in range(TOP_K):
            copy(u, j).wait()
        return c

    lax.fori_loop(0, tm, start, 0, unroll=8)
    lax.fori_loop(0, tm, wait, 0, unroll=8)


def _dispatch(slot_flat, hf, cap, tm=128):
    n = hf.shape[0]
    xb0 = jnp.zeros((cap, ROW_SUB, LANES), F32)
    return pl.pallas_call(
        functools.partial(_dispatch_kernel, tm=tm),
        grid=(n // tm,),
        in_specs=[
            pl.BlockSpec((TOP_K * tm,), lambda i: (i,), memory_space=pltpu.SMEM),
            pl.BlockSpec((tm, ROW_SUB, LANES), lambda i: (i, 0, 0)),
            pl.BlockSpec(memory_space=pl.ANY),
        ],
        out_specs=pl.BlockSpec(memory_space=pl.ANY),
        out_shape=jax.ShapeDtypeStruct((cap, ROW_SUB, LANES), F32),
        scratch_shapes=[pltpu.SemaphoreType.DMA(())],
        input_output_aliases={2: 0},
        compiler_params=_cparams("arbitrary"),
        name="moe_dispatch",
    )(slot_flat, hf, xb0)


def _ffn_kernel(be_ref, nu_ref, x_ref, wg_ref, wu_ref, wd_ref, o_ref, wg_s, wu_s, wd_s):
    b = pl.program_id(0)
    used = b < nu_ref[0]
    prev_e = be_ref[jnp.maximum(b - 1, 0)]
    new_expert = jnp.logical_or(b == 0, be_ref[b] != prev_e)

    @pl.when(jnp.logical_and(used, new_expert))
    def _():
        wg_s[...] = wg_ref[...].astype(BF16)
        wu_s[...] = wu_ref[...].astype(BF16)
        wd_s[...] = wd_ref[...].astype(BF16)

    @pl.when(used)
    def _():
        x = _rows_to_2d(x_ref).astype(BF16)
        gt = jnp.dot(x, wg_s[...], preferred_element_type=F32)
        up = jnp.dot(x, wu_s[...], preferred_element_type=F32)
        hid = (gt * jax.nn.sigmoid(gt) * up).astype(BF16)
        _store_rows(o_ref, jnp.dot(hid, wd_s[...], preferred_element_type=F32))

    @pl.when(jnp.logical_not(used))
    def _():
        o_ref[...] = jnp.zeros_like(o_ref)


def _expert_ffn(layer, block_expert, n_used, xb, w_gate, w_up, w_down):
    cap = xb.shape[0]
    nblk = cap // MOE_ROWS
    wmap = lambda b, be, nu: (layer, be[b], 0, 0)
    grid_spec = pltpu.PrefetchScalarGridSpec(
        num_scalar_prefetch=2,
        grid=(nblk,),
        in_specs=[
            pl.BlockSpec((MOE_ROWS, ROW_SUB, LANES), lambda b, be, nu: (b, 0, 0)),
            pl.BlockSpec((None, None, D_MODEL, EXPERT_HIDDEN), wmap),
            pl.BlockSpec((None, None, D_MODEL, EXPERT_HIDDEN), wmap),
            pl.BlockSpec((None, None, EXPERT_HIDDEN, D_MODEL), wmap),
        ],
        out_specs=pl.BlockSpec((MOE_ROWS, ROW_SUB, LANES), lambda b, be, nu: (b, 0, 0)),
        scratch_shapes=[
            pltpu.VMEM((D_MODEL, EXPERT_HIDDEN), BF16),
            pltpu.VMEM((D_MODEL, EXPERT_HIDDEN), BF16),
            pltpu.VMEM((EXPERT_HIDDEN, D_MODEL), BF16),
        ],
    )
    return pl.pallas_call(
        _ffn_kernel,
        grid_spec=grid_spec,
        out_shape=jax.ShapeDtypeStruct((cap, ROW_SUB, LANES), F32),
        compiler_params=_cparams("arbitrary"),
        name="expert_ffn",
    )(block_expert, n_used, xb, w_gate, w_up, w_down)


def _combine_kernel(slot_ref, x_ref, slab_ref, fw_ref, yb_hbm, o_ref, ybuf, sem, *, tm, final_norm):
    def copy(u, j):
        return pltpu.make_async_copy(yb_hbm.at[pl.ds(slot_ref[TOP_K * u + j], 1)],
                                     ybuf.at[j, pl.ds(u, 1)], sem)

    def start(u, c):
        for j in range(TOP_K):
            copy(u, j).start()
        return c

    def wait(u, c):
        for j in range(TOP_K):
            copy(u, j).wait()
        return c

    lax.fori_loop(0, tm, start, 0, unroll=8)
    lax.fori_loop(0, tm, wait, 0, unroll=8)

    slab = slab_ref[...]
    w1 = slab[:, 2:3]
    w2 = slab[:, 3:4]
    out = x_ref[...] + (_rows_to_2d(ybuf, 0) * w1 + _rows_to_2d(ybuf, 1) * w2)
    if final_norm:
        out = _rms(out, fw_ref[...])
    o_ref[...] = out


def _combine(slot_flat, x, slab, final_w, yb, final_norm, tm=128):
    n = x.shape[0]
    row = lambda i: (i, 0)
    return pl.pallas_call(
        functools.partial(_combine_kernel, tm=tm, final_norm=final_norm),
        grid=(n // tm,),
        in_specs=[
            pl.BlockSpec((TOP_K * tm,), lambda i: (i,), memory_space=pltpu.SMEM),
            pl.BlockSpec((tm, D_MODEL), row),
            pl.BlockSpec((tm, ROUTER_LANES), row),
            pl.BlockSpec((1, D_MODEL), lambda i: (0, 0)),
            pl.BlockSpec(memory_space=pl.ANY),
        ],
        out_specs=pl.BlockSpec((tm, D_MODEL), row),
        out_shape=jax.ShapeDtypeStruct((n, D_MODEL), F32),
        scratch_shapes=[pltpu.VMEM((TOP_K, tm, ROW_SUB, LANES), F32), pltpu.SemaphoreType.DMA(())],
        compiler_params=_cparams("arbitrary"),
        name="moe_combine",
    )(slot_flat, x, slab, final_w, yb)


def _pad_rows(w, rows, offset=0):
    out = jnp.zeros((rows, w.shape[1]), BF16)
    return out.at[offset:offset + w.shape[0]].set(w.astype(BF16))


def _pad_cols(vec, cols, offset=0):
    out = jnp.zeros((cols,), F32)
    return out.at[offset:offset + vec.shape[0]].set(vec.astype(F32))


def _moe_plan(slab, counts_row, n_tokens):
    eid = slab[:, 0:TOP_K].astype(jnp.int32)
    rank = slab[:, 4:4 + TOP_K].astype(jnp.int32)
    counts = counts_row[EXPERT_LANE0:EXPERT_LANE0 + N_EXPERTS].astype(jnp.int32)
    padded = (counts + MOE_ROWS - 1) // MOE_ROWS * MOE_ROWS
    pad_ends = jnp.cumsum(padded)
    pad_starts = pad_ends - padded
    slot = pad_starts[eid.reshape(-1)] + rank.reshape(-1)
    nblk = n_tokens * TOP_K // MOE_ROWS + N_EXPERTS
    block_start = jnp.arange(nblk, dtype=jnp.int32) * MOE_ROWS
    block_expert = jnp.minimum(
        jnp.sum((pad_ends[None, :] <= block_start[:, None]).astype(jnp.int32), axis=1),
        N_EXPERTS - 1).astype(jnp.int32)
    n_used = (pad_ends[-1:] // MOE_ROWS).astype(jnp.int32)
    return slot, block_expert, n_used, nblk * MOE_ROWS


def kernel(x, attn_norm_w, w_in, shift_mix, attn_sinks, rwkv_w0, rwkv_w_up, rwkv_a0, rwkv_a_up, rwkv_g_up, rwkv_k_k, rwkv_k_a, rwkv_r_k, rwkv_ln_w, rwkv_ln_b, vres_down, vres_mix, vres_up, vres_v0, w_branch_a, w_branch_b, w_out, ffn_norm_w, router_group_w, router_group_b, router_expert_w, router_expert_b, expert_w_gate, expert_w_up, expert_w_down, final_norm_w):
    batch, seq, _ = x.shape
    n = batch * seq
    depth = w_in.shape[0]
    xf = x.reshape(n, D_MODEL)
    v_first = None
    for i in range(depth):
        has_vres = i > 0
        w_gates = w_in[i, :, GATE_COL0:].astype(BF16)
        mix_r = shift_mix[i, :RKV_COLS].reshape(1, RKV_COLS)
        mix_l = _pad_cols(shift_mix[i, RKV_COLS:], LORA_COLS).reshape(1, LORA_COLS)
        v0 = vres_v0[i - 1] if has_vres else jnp.zeros((RWKV_WIDTH,), F32)
        zero = jnp.zeros((RWKV_WIDTH,), F32)
        pv_pre = jnp.stack([rwkv_w0[i], rwkv_a0[i], rwkv_k_k[i], rwkv_k_a[i], v0, zero, zero, zero]).astype(F32)
        pv_post = jnp.stack([rwkv_ln_w[i], rwkv_ln_b[i], rwkv_r_k[i].reshape(-1),
                             zero, zero, zero, zero, zero]).astype(F32)
        wup = _pad_rows(rwkv_w_up[i], LANES, 0)
        aup = _pad_rows(rwkv_a_up[i], LANES, DECAY_LORA)
        gup = _pad_rows(rwkv_g_up[i], 2 * LANES, 0)
        if has_vres:
            vd = _pad_rows(vres_down[i - 1].T, LANES, 0).T
            mix_v = _pad_cols(vres_mix[i - 1], LANES).reshape(1, LANES)
            vup = _pad_rows(vres_up[i - 1], LANES, 0)
        else:
            vd, mix_v = None, None
            vup = jnp.zeros((LANES, RWKV_WIDTH), BF16)
        w_router = jnp.concatenate(
            [router_group_w[i], router_expert_w[i],
             jnp.zeros((D_MODEL, ROUTER_LANES - N_GROUPS - N_EXPERTS), F32)], axis=1)
        b_router = _pad_cols(jnp.concatenate([router_group_b[i], router_expert_b[i]]),
                             ROUTER_LANES).reshape(1, ROUTER_LANES)

        proj = _inproj(i, xf, attn_norm_w[i].reshape(1, D_MODEL), w_in, w_gates, vd)
        qkv, rkv, lora, gates = proj[:4]
        zv = proj[4] if has_vres else None
        attn = _attention(qkv, attn_sinks[i], seq)
        r, lw, k2, v, al, be, g = _rwkv_pre(rkv, lora, v_first, zv, mix_r, mix_l, mix_v, pv_pre,
                                            wup, aup, gup, vup, seq)
        if not has_vres:
            v_first = v
        y_raw = _rwkv_scan(r, lw, k2, v, al, be, seq)
        y = _rwkv_post(y_raw, r, k2, v, g, pv_post)
        merged = _merge(i, attn, y, w_branch_a, w_branch_b, gates)

        x_mid, hf, slab, counts = _outproj_router(
            merged, w_out[i].astype(BF16), xf, ffn_norm_w[i].reshape(1, D_MODEL), w_router, b_router)
        slot, block_expert, n_used, cap = _moe_plan(slab, counts[0], n)
        xb = _dispatch(slot, hf, cap)
        yb = _expert_ffn(i, block_expert, n_used, xb, expert_w_gate, expert_w_up, expert_w_down)
        xf = _combine(slot, x_mid, slab, final_norm_w.reshape(1, D_MODEL), yb, final_norm=(i == depth - 1))
    return xf.reshape(batch, seq, D_MODEL)
```

```python
import functools

import jax
import jax.numpy as jnp
from jax import lax
from jax.experimental import pallas as pl
from jax.experimental.pallas import tpu as pltpu

F32 = jnp.float32
BF16 = jnp.bfloat16

D_MODEL = 2048
HEAD_DIM = 64
N_Q_HEADS = 16
N_KV_HEADS = 4
GQA_GROUP = N_Q_HEADS // N_KV_HEADS
ATTN_WIDTH = N_Q_HEADS * HEAD_DIM
KV_WIDTH = N_KV_HEADS * HEAD_DIM
WINDOW = 128
ATTN_BLOCK = 128

RWKV_HEAD = 64
N_RWKV_HEADS = 16
RWKV_WIDTH = N_RWKV_HEADS * RWKV_HEAD
DECAY_LORA = 64
AAA_LORA = 64
MV_LORA = 32
GATE_LORA = 160
RWKV_GN_EPS = 64e-5
LORA_WIDTH = DECAY_LORA + AAA_LORA + GATE_LORA

N_GROUPS = 4
EXPERTS_PER_GROUP = 8
N_EXPERTS = N_GROUPS * EXPERTS_PER_GROUP
TOP_K = 2
EXPERT_HIDDEN = D_MODEL // 4
NORM_EPS = 1e-5

LANES = 128
SUBLANES = 8
MXU_COLS = 256

SEG = 512
QKV_COLS = ATTN_WIDTH + 2 * KV_WIDTH
RKV_COLS = 3 * RWKV_WIDTH
LORA_COLS = SEG
GATE_COLS = 2 * D_MODEL
QKV_TILES = QKV_COLS // SEG
RKV_TILES = RKV_COLS // SEG
LORA_TILES = LORA_COLS // SEG
GATE_TILES = GATE_COLS // SEG
IN_TILES = QKV_TILES + RKV_TILES + LORA_TILES + GATE_TILES
FIRST_GATE_TILE = QKV_TILES + RKV_TILES + LORA_TILES
GATE_COL0 = QKV_COLS + RKV_COLS + LORA_WIDTH

SCAN_CHUNK = 64
MOE_ROWS = 256
ROUTER_LANES = LANES
EXPERT_LANE0 = N_GROUPS

ROW_SUB = D_MODEL // LANES

VMEM_LIMIT = 56 * 1024 * 1024


def _rows_to_2d(slab_ref, rows):
    return jnp.concatenate([slab_ref[pl.ds(s, rows, stride=ROW_SUB), :] for s in range(ROW_SUB)], axis=1)


def _store_rows(slab_ref, val):
    rows = val.shape[0]
    for s in range(ROW_SUB):
        slab_ref[pl.ds(s, rows, stride=ROW_SUB), :] = val[:, s * LANES:(s + 1) * LANES]


def _slab(ref, row):
    return ref.at[pl.ds(pl.multiple_of(row * ROW_SUB, ROW_SUB), ROW_SUB)]


def _cparams(*sem):
    return pltpu.CompilerParams(dimension_semantics=sem, vmem_limit_bytes=VMEM_LIMIT)


def _bdot(a, b):
    return jnp.dot(a.astype(BF16), b.astype(BF16), preferred_element_type=F32)


def _bdot_nt(a, b):
    return lax.dot_general(a.astype(BF16), b.astype(BF16), (((1,), (1,)), ((), ())),
                           preferred_element_type=F32)


def _bdot_tn(a, b):
    return lax.dot_general(a.astype(BF16), b.astype(BF16), (((0,), (0,)), ((), ())),
                           preferred_element_type=F32)


def _dot_split_rhs(w01, x, passes):
    acc, rem = None, x
    for p in range(passes):
        part = rem.astype(BF16)
        d = jnp.dot(w01, part, preferred_element_type=F32)
        acc = d if acc is None else acc + d
        if p + 1 < passes:
            rem = rem - part.astype(F32)
    return acc


def _dot_split_lhs(x, w01, passes):
    acc, rem = None, x
    for p in range(passes):
        part = rem.astype(BF16)
        d = jnp.dot(part, w01, preferred_element_type=F32)
        acc = d if acc is None else acc + d
        if p + 1 < passes:
            rem = rem - part.astype(F32)
    return acc


def _rms(x, w):
    ms = jnp.mean(x * x, axis=-1, keepdims=True)
    return x * lax.rsqrt(ms + NORM_EPS) * w


def _inproj_kernel(*refs, has_vres):
    refs = list(refs)
    x_ref, nw_ref, w_ref, wg_ref = refs[:4]
    refs = refs[4:]
    vd_ref = refs.pop(0) if has_vres else None
    qkv_ref, rkv_ref, lora_ref, gate_ref = refs[:4]
    refs = refs[4:]
    zv_ref = refs.pop(0) if has_vres else None
    (h_scr,) = refs
    j = pl.program_id(1)

    @pl.when(j == 0)
    def _():
        h_scr[...] = _rms(x_ref[...], nw_ref[...]).astype(BF16)
        if has_vres:
            zv_ref[...] = jnp.dot(h_scr[...], vd_ref[...], preferred_element_type=F32)

    def project(wt_ref, out_ref):
        for c in range(SEG // MXU_COLS):
            cs = slice(c * MXU_COLS, (c + 1) * MXU_COLS)
            acc = jnp.dot(h_scr[...], wt_ref[:, cs], preferred_element_type=F32)
            out_ref[:, cs] = acc.astype(out_ref.dtype)

    @pl.when(j < QKV_TILES)
    def _():
        project(w_ref, qkv_ref)

    @pl.when((j >= QKV_TILES) & (j < QKV_TILES + RKV_TILES))
    def _():
        project(w_ref, rkv_ref)

    @pl.when(j == QKV_TILES + RKV_TILES)
    def _():
        project(w_ref, lora_ref)

    @pl.when(j >= FIRST_GATE_TILE)
    def _():
        project(wg_ref, gate_ref)


def _inproj(layer, x, norm_w, w_in, w_gates, vd, tm=1024):
    n = x.shape[0]
    has_vres = vd is not None
    s_rkv = QKV_TILES
    in_specs = [
        pl.BlockSpec((tm, D_MODEL), lambda i, j: (i, 0)),
        pl.BlockSpec((1, D_MODEL), lambda i, j: (0, 0)),
        pl.BlockSpec((None, D_MODEL, SEG), lambda i, j: (layer, 0, jnp.minimum(j, FIRST_GATE_TILE - 1))),
        pl.BlockSpec((D_MODEL, SEG), lambda i, j: (0, jnp.clip(j - FIRST_GATE_TILE, 0, GATE_TILES - 1))),
    ]
    args = [x, norm_w, w_in, w_gates]
    out_specs = [
        pl.BlockSpec((tm, SEG), lambda i, j: (i, jnp.clip(j, 0, QKV_TILES - 1))),
        pl.BlockSpec((tm, SEG), lambda i, j: (i, jnp.clip(j - s_rkv, 0, RKV_TILES - 1))),
        pl.BlockSpec((tm, SEG), lambda i, j: (i, 0)),
        pl.BlockSpec((tm, SEG), lambda i, j: (i, jnp.clip(j - FIRST_GATE_TILE, 0, GATE_TILES - 1))),
    ]
    out_shape = [
        jax.ShapeDtypeStruct((n, QKV_COLS), BF16),
        jax.ShapeDtypeStruct((n, RKV_COLS), F32),
        jax.ShapeDtypeStruct((n, LORA_COLS), F32),
        jax.ShapeDtypeStruct((n, GATE_COLS), BF16),
    ]
    if has_vres:
        in_specs.append(pl.BlockSpec((D_MODEL, LANES), lambda i, j: (0, 0)))
        args.append(vd)
        out_specs.append(pl.BlockSpec((tm, LANES), lambda i, j: (i, 0)))
        out_shape.append(jax.ShapeDtypeStruct((n, LANES), F32))
    return pl.pallas_call(
        functools.partial(_inproj_kernel, has_vres=has_vres),
        grid=(n // tm, IN_TILES),
        in_specs=in_specs,
        out_specs=out_specs,
        out_shape=out_shape,
        scratch_shapes=[pltpu.VMEM((tm, D_MODEL), BF16)],
        compiler_params=_cparams("parallel", "arbitrary"),
        name="inproj",
    )(*args)


def _attn_kernel(sink_ref, q_ref, kp_ref, kc_ref, vp_ref, vc_ref, bias_ref, o_ref, *, nb):
    first = (pl.program_id(0) % nb) == 0
    col = lax.broadcasted_iota(jnp.int32, (ATTN_BLOCK, 2 * ATTN_BLOCK), 1)
    pad_keys = jnp.logical_and(first, col < ATTN_BLOCK)
    scale = HEAD_DIM ** -0.5
    for hk in range(N_KV_HEADS):
        ks = slice(hk * HEAD_DIM, (hk + 1) * HEAD_DIM)
        kw = jnp.concatenate([kp_ref[:, ks], kc_ref[:, ks]], axis=0)
        vw = jnp.concatenate([vp_ref[:, ks], vc_ref[:, ks]], axis=0)
        for g in range(GQA_GROUP):
            h = hk * GQA_GROUP + g
            qh = q_ref[:, h * HEAD_DIM:(h + 1) * HEAD_DIM]
            s = lax.dot_general(qh, kw, (((1,), (1,)), ((), ())), preferred_element_type=F32)
            s = s * scale + bias_ref[h]
            s = jnp.where(pad_keys, -jnp.inf, s)
            sink = sink_ref[h]
            m = jnp.maximum(jnp.max(s, axis=-1, keepdims=True), sink)
            p = jnp.exp(s - m)
            denom = jnp.sum(p, axis=-1, keepdims=True) + jnp.exp(sink - m)
            o = jnp.dot(p.astype(BF16), vw, preferred_element_type=F32) / denom
            o_ref[:, h * HEAD_DIM:(h + 1) * HEAD_DIM] = o.astype(BF16)


def _attn_bias():
    qi = jnp.arange(ATTN_BLOCK)[:, None]
    kj = jnp.arange(2 * ATTN_BLOCK)[None, :]
    dist = qi + ATTN_BLOCK - kj
    valid = (dist >= 0) & (dist < WINDOW)
    slopes = jnp.exp2(-8.0 * jnp.arange(1, N_Q_HEADS + 1, dtype=F32) / N_Q_HEADS)
    bias = -slopes[:, None, None] * dist.astype(F32)[None]
    return jnp.where(valid[None], bias, -jnp.inf)


def _attention(qkv, sinks, seq):
    n = qkv.shape[0]
    nb = seq // ATTN_BLOCK
    kcol = ATTN_WIDTH // KV_WIDTH
    prev = lambda i: jnp.where(i % nb == 0, i, i - 1)
    return pl.pallas_call(
        functools.partial(_attn_kernel, nb=nb),
        grid=(n // ATTN_BLOCK,),
        in_specs=[
            pl.BlockSpec(memory_space=pltpu.SMEM),
            pl.BlockSpec((ATTN_BLOCK, ATTN_WIDTH), lambda i: (i, 0)),
            pl.BlockSpec((ATTN_BLOCK, KV_WIDTH), lambda i: (prev(i), kcol)),
            pl.BlockSpec((ATTN_BLOCK, KV_WIDTH), lambda i: (i, kcol)),
            pl.BlockSpec((ATTN_BLOCK, KV_WIDTH), lambda i: (prev(i), kcol + 1)),
            pl.BlockSpec((ATTN_BLOCK, KV_WIDTH), lambda i: (i, kcol + 1)),
            pl.BlockSpec((N_Q_HEADS, ATTN_BLOCK, 2 * ATTN_BLOCK), lambda i: (0, 0, 0)),
        ],
        out_specs=pl.BlockSpec((ATTN_BLOCK, ATTN_WIDTH), lambda i: (i, 0)),
        out_shape=jax.ShapeDtypeStruct((n, ATTN_WIDTH), BF16),
        compiler_params=_cparams("parallel"),
        name="swa_attention",
    )(sinks.astype(F32), qkv, qkv, qkv, qkv, qkv, _attn_bias())


def _head_indicator():
    ch = jnp.arange(RWKV_WIDTH)[:, None] // RWKV_HEAD
    e = (ch == jnp.arange(LANES)[None, :]).astype(BF16)
    return e, e.T


def _rwkv_pre_kernel(*refs, has_vres, tiles_per_seq):
    refs = list(refs)
    rkv_ref, rkvp_ref, lo_ref, lop_ref = refs[:4]
    refs = refs[4:]
    if has_vres:
        vf_ref, zv_ref, zvp_ref, mixv_ref = refs[:4]
        refs = refs[4:]
    (mixr_ref, mixl_ref, pv_ref, wup_ref, aup_ref, gup_ref, vup_ref, e_ref, et_ref,
     r_out, lw_out, k_out, v_out, al_out, be_out, g_out) = refs

    first = (pl.program_id(0) % tiles_per_seq) == 0

    def shifted(z, zp_ref):
        prev_row = jnp.where(first, 0.0, zp_ref[SUBLANES - 1:SUBLANES, :])
        row = lax.broadcasted_iota(jnp.int32, z.shape, 0)
        return jnp.where(row == 0, prev_row, pltpu.roll(z, 1, 0))

    z = rkv_ref[...]
    zs = z + (shifted(z, rkvp_ref) - z) * mixr_ref[...]
    lo = lo_ref[...]
    los = lo + (shifted(lo, lop_ref) - lo) * mixl_ref[...]

    r = zs[:, :RWKV_WIDTH]
    kr = zs[:, RWKV_WIDTH:2 * RWKV_WIDTH]
    vr = zs[:, 2 * RWKV_WIDTH:]

    col = lax.broadcasted_iota(jnp.int32, los.shape, 1)
    gate_cols = (col >= DECAY_LORA + AAA_LORA) & (col < LORA_WIDTH)
    act = jnp.where(col < DECAY_LORA, jnp.tanh(los), jnp.where(gate_cols, jax.nn.sigmoid(los), los))
    act = act.astype(BF16)
    wa_in = act[:, :LANES]
    dw = jnp.dot(wa_in, wup_ref[...], preferred_element_type=F32)
    da = jnp.dot(wa_in, aup_ref[...], preferred_element_type=F32)
    g = jnp.dot(act[:, LANES:3 * LANES], gup_ref[...], preferred_element_type=F32)

    w0, a0, k_k, k_a, v0 = (pv_ref[i:i + 1, :] for i in range(5))
    u = -(w0 + dw)
    softplus = jnp.maximum(u, 0.0) + jnp.log1p(jnp.exp(-jnp.abs(u)))
    w_log = -softplus - 0.5
    a = jax.nn.sigmoid(a0 + da)
    if has_vres:
        zv = zv_ref[...]
        zvs = zv + (shifted(zv, zvp_ref) - zv) * mixv_ref[...]
        dv = jnp.dot(zvs.astype(BF16), vup_ref[...], preferred_element_type=F32)
        vr = vr + (vf_ref[...] - vr) * jax.nn.sigmoid(v0 + dv)

    kk0 = kr * k_k
    ss = _dot_split_lhs(kk0 * kk0, e_ref[...], 2)
    inv = 1.0 / jnp.maximum(jnp.sqrt(ss), 1e-12)
    kk = kk0 * _dot_split_lhs(inv, et_ref[...], 2)

    r_out[...] = r
    lw_out[...] = -jnp.exp(w_log)
    k_out[...] = kr * (1.0 + (a - 1.0) * k_a)
    v_out[...] = vr
    al_out[...] = -kk
    be_out[...] = kk * a
    g_out[...] = g


def _rwkv_pre(rkv, lora, v_first, zv, mix_r, mix_l, mix_v, pvec, wup, aup, gup, vup, seq, tt=256):
    n = rkv.shape[0]
    has_vres = v_first is not None
    e, et = _head_indicator()
    row = lambda i: (i, 0)
    prev8 = lambda i: (jnp.maximum(i * (tt // SUBLANES) - 1, 0), 0)
    const = lambda i: (0, 0)
    in_specs = [
        pl.BlockSpec((tt, RKV_COLS), row),
        pl.BlockSpec((SUBLANES, RKV_COLS), prev8),
        pl.BlockSpec((tt, LORA_COLS), row),
        pl.BlockSpec((SUBLANES, LORA_COLS), prev8),
    ]
    args = [rkv, rkv, lora, lora]
    if has_vres:
        in_specs += [
            pl.BlockSpec((tt, RWKV_WIDTH), row),
            pl.BlockSpec((tt, LANES), row),
            pl.BlockSpec((SUBLANES, LANES), prev8),
            pl.BlockSpec((1, LANES), const),
        ]
        args += [v_first, zv, zv, mix_v]
    in_specs += [
        pl.BlockSpec((1, RKV_COLS), const),
        pl.BlockSpec((1, LORA_COLS), const),
        pl.BlockSpec((SUBLANES, RWKV_WIDTH), const),
        pl.BlockSpec((LANES, RWKV_WIDTH), const),
        pl.BlockSpec((LANES, RWKV_WIDTH), const),
        pl.BlockSpec((2 * LANES, RWKV_WIDTH), const),
        pl.BlockSpec((LANES, RWKV_WIDTH), const),
        pl.BlockSpec((RWKV_WIDTH, LANES), const),
        pl.BlockSpec((LANES, RWKV_WIDTH), const),
    ]
    args += [mix_r, mix_l, pvec, wup, aup, gup, vup, e, et]
    out = jax.ShapeDtypeStruct((n, RWKV_WIDTH), F32)
    return pl.pallas_call(
        functools.partial(_rwkv_pre_kernel, has_vres=has_vres, tiles_per_seq=seq // tt),
        grid=(n // tt,),
        in_specs=in_specs,
        out_specs=[pl.BlockSpec((tt, RWKV_WIDTH), row)] * 7,
        out_shape=[out] * 7,
        compiler_params=_cparams("parallel"),
        name="rwkv_pre",
    )(*args)


def _scan_kernel(r_ref, lw_ref, k_ref, v_ref, al_ref, be_ref, y_ref, s_scr):
    L = SCAN_CHUNK

    @pl.when(pl.program_id(1) == 0)
    def _():
        s_scr[...] = jnp.zeros_like(s_scr)

    row = lax.broadcasted_iota(jnp.int32, (L, L), 0)
    col = lax.broadcasted_iota(jnp.int32, (L, L), 1)
    strict = row > col
    incl = row >= col
    tri = incl.astype(BF16)
    eye = (row == col).astype(F32)

    lw = lw_ref[...]
    cum = _dot_split_rhs(tri, lw, 3)
    p_inc = jnp.exp(cum)
    p_exc = jnp.exp(cum - lw)
    p_inv = jnp.exp(-cum)
    p_last = p_inc[L - 1:L, :]
    a_t = al_ref[...] * p_exc
    r_t = r_ref[...] * p_inc
    b_t = be_ref[...] * p_inv
    k_t = k_ref[...] * p_inv
    b_end = b_t * p_last
    k_end = k_t * p_last
    v = v_ref[...]

    heads = range(N_RWKV_HEADS)
    sls = [slice(h * RWKV_HEAD, (h + 1) * RWKV_HEAD) for h in heads]
    ar = [jnp.concatenate([a_t[:, sl], r_t[:, sl]], axis=0) for sl in sls]
    bk = [jnp.concatenate([b_t[:, sl], k_t[:, sl]], axis=0) for sl in sls]
    gm = [_bdot_nt(ar[h], bk[h]) for h in heads]
    s0 = [s_scr[h] for h in heads]
    vh = [v[:, sl] for sl in sls]
    xs = [_bdot_nt(ar[h], s0[h]) for h in heads]
    a_ab = [jnp.where(strict, g[:L, :L], 0.0) for g in gm]
    a_rb = [jnp.where(incl, g[L:, :L], 0.0) for g in gm]
    a_kk = [jnp.concatenate([jnp.where(strict, g[:L, L:], 0.0), jnp.where(incl, g[L:, L:], 0.0)], axis=0)
            for g in gm]
    av = [_bdot(a_kk[h], vh[h]) for h in heads]

    pw = [_bdot(n, n) for n in a_ab]
    tinv = [eye + n for n in a_ab]
    span = 2
    while 2 * span < L:
        both = [_bdot(jnp.concatenate([pw[h], tinv[h]], axis=0), pw[h]) for h in heads]
        tinv = [tinv[h] + both[h][L:] for h in heads]
        pw = [b[:L] for b in both]
        span *= 2
    tinv = [tinv[h] + _bdot(tinv[h], pw[h]) for h in heads]

    u = [_bdot(tinv[h], xs[h][:L] + av[h][:L]) for h in heads]
    for h in heads:
        y_ref[:, sls[h]] = xs[h][L:] + av[h][L:] + _bdot(a_rb[h], u[h])
    for h in heads:
        uv = jnp.concatenate([u[h], vh[h]], axis=0)
        bke = jnp.concatenate([b_end[:, sls[h]], k_end[:, sls[h]]], axis=0)
        s_scr[h] = s0[h] * p_last[:, sls[h]] + _bdot_tn(uv, bke)


def _rwkv_scan(r, lw, k, v, al, be, seq):
    n = r.shape[0]
    nc = seq // SCAN_CHUNK
    spec = pl.BlockSpec((SCAN_CHUNK, RWKV_WIDTH), lambda b, c: (b * nc + c, 0))
    return pl.pallas_call(
        _scan_kernel,
        grid=(n // seq, nc),
        in_specs=[spec] * 6,
        out_specs=spec,
        out_shape=jax.ShapeDtypeStruct((n, RWKV_WIDTH), F32),
        scratch_shapes=[pltpu.VMEM((N_RWKV_HEADS, RWKV_HEAD, RWKV_HEAD), F32)],
        compiler_params=_cparams("parallel", "arbitrary"),
        name="rwkv_scan",
    )(r, lw, k, v, al, be)


def _rwkv_post_kernel(y_ref, r_ref, k_ref, v_ref, g_ref, pv_ref, e_ref, et_ref, o_ref):
    e, et = e_ref[...], et_ref[...]
    ln_w, ln_b, r_k = (pv_ref[i:i + 1, :] for i in range(3))
    y = y_ref[...]
    inv_n = 1.0 / RWKV_HEAD
    mu = _dot_split_lhs(y, e, 2) * inv_n
    d = y - _dot_split_lhs(mu, et, 2)
    var = _dot_split_lhs(d * d, e, 2) * inv_n
    rstd = lax.rsqrt(var + RWKV_GN_EPS)
    yn = d * _dot_split_lhs(rstd, et, 2) * ln_w + ln_b
    v = v_ref[...]
    coef = _dot_split_lhs(r_ref[...] * k_ref[...] * r_k, e, 2)
    bonus = _dot_split_lhs(coef, et, 2) * v
    o_ref[...] = ((yn + bonus) * g_ref[...]).astype(BF16)


def _rwkv_post(y, r, k, v, g, pvec, tt=256):
    n = y.shape[0]
    e, et = _head_indicator()
    row = lambda i: (i, 0)
    const = lambda i: (0, 0)
    return pl.pallas_call(
        _rwkv_post_kernel,
        grid=(n // tt,),
        in_specs=[pl.BlockSpec((tt, RWKV_WIDTH), row)] * 5 + [
            pl.BlockSpec((SUBLANES, RWKV_WIDTH), const),
            pl.BlockSpec((RWKV_WIDTH, LANES), const),
            pl.BlockSpec((LANES, RWKV_WIDTH), const),
        ],
        out_specs=pl.BlockSpec((tt, RWKV_WIDTH), row),
        out_shape=jax.ShapeDtypeStruct((n, RWKV_WIDTH), BF16),
        compiler_params=_cparams("parallel"),
        name="rwkv_post",
    )(y, r, k, v, g, pvec, e, et)


def _merge_kernel(a_ref, y_ref, wa_ref, wb_ref, ga_ref, gb_ref, o_ref):
    pa = jnp.dot(a_ref[...], wa_ref[...].astype(BF16), preferred_element_type=F32)
    pb = jnp.dot(y_ref[...], wb_ref[...].astype(BF16), preferred_element_type=F32)
    ga = jax.nn.sigmoid(ga_ref[...].astype(F32))
    gb = jax.nn.sigmoid(gb_ref[...].astype(F32))
    o_ref[...] = (ga * pa + gb * pb).astype(BF16)


def _merge(layer, attn, y, wa, wb, gates, tm=1024, tn=512):
    n = attn.shape[0]
    nj = D_MODEL // tn
    return pl.pallas_call(
        _merge_kernel,
        grid=(n // tm, nj),
        in_specs=[
            pl.BlockSpec((tm, ATTN_WIDTH), lambda i, j: (i, 0)),
            pl.BlockSpec((tm, RWKV_WIDTH), lambda i, j: (i, 0)),
            pl.BlockSpec((None, ATTN_WIDTH, tn), lambda i, j: (layer, 0, j)),
            pl.BlockSpec((None, RWKV_WIDTH, tn), lambda i, j: (layer, 0, j)),
            pl.BlockSpec((tm, tn), lambda i, j: (i, j)),
            pl.BlockSpec((tm, tn), lambda i, j: (i, j + nj)),
        ],
        out_specs=pl.BlockSpec((tm, tn), lambda i, j: (i, j)),
        out_shape=jax.ShapeDtypeStruct((n, D_MODEL), BF16),
        compiler_params=_cparams("parallel", "arbitrary"),
        name="gated_merge",
    )(attn, y, wa, wb, gates, gates)


def _outproj_router_kernel(m_ref, wo_ref, x_ref, nw_ref, wr_ref, br_ref,
                           xo_ref, hf_ref, slab_ref, cnt_ref, run_scr):
    @pl.when(pl.program_id(0) == 0)
    def _():
        run_scr[...] = jnp.zeros_like(run_scr)

    xn = x_ref[...] + jnp.dot(m_ref[...], wo_ref[...], preferred_element_type=F32)
    xo_ref[...] = xn
    hf = _rms(xn, nw_ref[...])
    _store_rows(hf_ref, hf)

    h_hi = hf.astype(BF16)
    h_lo = (hf - h_hi.astype(F32)).astype(BF16)
    wr = wr_ref[...]
    w_hi = wr.astype(BF16)
    w_lo = (wr - w_hi.astype(F32)).astype(BF16)
    lg = (jnp.dot(h_hi, w_hi, preferred_element_type=F32)
          + jnp.dot(h_hi, w_lo, preferred_element_type=F32)
          + jnp.dot(h_lo, w_hi, preferred_element_type=F32)) + br_ref[...]

    tm = lg.shape[0]
    lane = lax.broadcasted_iota(jnp.int32, lg.shape, 1).astype(F32)
    neg = -jnp.inf
    big = float(ROUTER_LANES)

    def first_argmax(vals):
        mx = jnp.max(vals, axis=-1, keepdims=True)
        idx = jnp.min(jnp.where(vals == mx, lane, big), axis=-1, keepdims=True)
        return mx, idx

    is_group = lane < N_GROUPS
    gmax, gsel = first_argmax(jnp.where(is_group, lg, neg))
    gp = 1.0 / jnp.sum(jnp.where(is_group, jnp.exp(lg - gmax), 0.0), axis=-1, keepdims=True)
    lo_lane = EXPERT_LANE0 + EXPERTS_PER_GROUP * gsel
    in_group = (lane >= lo_lane) & (lane < lo_lane + EXPERTS_PER_GROUP)
    el = jnp.where(in_group, lg, neg)
    v1, i1 = first_argmax(el)
    v2, i2 = first_argmax(jnp.where(lane == i1, neg, el))
    e21 = jnp.exp(v2 - v1)
    ew1 = gp / (1.0 + e21)
    ew2 = gp * e21 / (1.0 + e21)

    oh1 = lane == i1
    oh2 = lane == i2
    cnt = oh1.astype(F32) + oh2.astype(F32)
    r_i = lax.broadcasted_iota(jnp.int32, (tm, tm), 0)
    c_i = lax.broadcasted_iota(jnp.int32, (tm, tm), 1)
    before = jnp.dot((r_i > c_i).astype(BF16), cnt.astype(BF16), preferred_element_type=F32)
    tot = before + run_scr[0:1, :]
    rank1 = jnp.sum(jnp.where(oh1, tot, 0.0), axis=-1, keepdims=True)
    rank2 = jnp.sum(jnp.where(oh2, tot, 0.0), axis=-1, keepdims=True)
    run = run_scr[0:1, :] + jnp.sum(cnt, axis=0, keepdims=True)
    run_scr[...] = jnp.broadcast_to(run, run_scr.shape)
    cnt_ref[...] = jnp.broadcast_to(run, cnt_ref.shape)

    slab = jnp.where(lane == 0, i1 - EXPERT_LANE0,
           jnp.where(lane == 1, i2 - EXPERT_LANE0,
           jnp.where(lane == 2, ew1,
           jnp.where(lane == 3, ew2,
           jnp.where(lane == 4, rank1,
           jnp.where(lane == 5, rank2, 0.0))))))
    slab_ref[...] = slab


def _outproj_router(merged, w_out, x, norm_w, w_router, b_router, tm=256):
    n = x.shape[0]
    row = lambda i: (i, 0)
    const = lambda i: (0, 0)
    return pl.pallas_call(
        _outproj_router_kernel,
        grid=(n // tm,),
        in_specs=[
            pl.BlockSpec((tm, D_MODEL), row),
            pl.BlockSpec((D_MODEL, D_MODEL), const),
            pl.BlockSpec((tm, D_MODEL), row),
            pl.BlockSpec((1, D_MODEL), const),
            pl.BlockSpec((D_MODEL, ROUTER_LANES), const),
            pl.BlockSpec((1, ROUTER_LANES), const),
        ],
        out_specs=[
            pl.BlockSpec((tm, D_MODEL), row),
            pl.BlockSpec((tm * ROW_SUB, LANES), row),
            pl.BlockSpec((tm, ROUTER_LANES), row),
            pl.BlockSpec((SUBLANES, ROUTER_LANES), const),
        ],
        out_shape=[
            jax.ShapeDtypeStruct((n, D_MODEL), F32),
            jax.ShapeDtypeStruct((n * ROW_SUB, LANES), F32),
            jax.ShapeDtypeStruct((n, ROUTER_LANES), F32),
            jax.ShapeDtypeStruct((SUBLANES, ROUTER_LANES), F32),
        ],
        scratch_shapes=[pltpu.VMEM((SUBLANES, ROUTER_LANES), F32)],
        compiler_params=_cparams("arbitrary"),
        name="outproj_router",
    )(merged, w_out, x, norm_w, w_router, b_router)


def _dispatch_kernel(slot_ref, hf_ref, xb_in_hbm, xb_hbm, sem, *, tm):
    del xb_in_hbm

    def copy(u, j):
        return pltpu.make_async_copy(_slab(hf_ref, u), _slab(xb_hbm, slot_ref[TOP_K * u + j]), sem)

    def start(u, c):
        for j in range(TOP_K):
            copy(u, j).start()
        return c

    def wait(u, c):
        for j in range(TOP_K):
            copy(u, j).wait()
        return c

    lax.fori_loop(0, tm, start, 0, unroll=8)
    lax.fori_loop(0, tm, wait, 0, unroll=8)


def _dispatch(slot_flat, hf, cap, tm=128):
    n = hf.shape[0] // ROW_SUB
    xb0 = jnp.zeros((cap * ROW_SUB, LANES), F32)
    return pl.pallas_call(
        functools.partial(_dispatch_kernel, tm=tm),
        grid=(n // tm,),
        in_specs=[
            pl.BlockSpec((TOP_K * tm,), lambda i: (i,), memory_space=pltpu.SMEM),
            pl.BlockSpec((tm * ROW_SUB, LANES), lambda i: (i, 0)),
            pl.BlockSpec(memory_space=pl.ANY),
        ],
        out_specs=pl.BlockSpec(memory_space=pl.ANY),
        out_shape=jax.ShapeDtypeStruct((cap * ROW_SUB, LANES), F32),
        scratch_shapes=[pltpu.SemaphoreType.DMA(())],
        input_output_aliases={2: 0},
        compiler_params=_cparams("arbitrary"),
        name="moe_dispatch",
    )(slot_flat, hf, xb0)


def _ffn_kernel(be_ref, nu_ref, x_ref, wg_ref, wu_ref, wd_ref, o_ref, wg_s, wu_s, wd_s):
    b = pl.program_id(0)
    used = b < nu_ref[0]
    prev_e = be_ref[jnp.maximum(b - 1, 0)]
    new_expert = jnp.logical_or(b == 0, be_ref[b] != prev_e)

    @pl.when(jnp.logical_and(used, new_expert))
    def _():
        wg_s[...] = wg_ref[...].astype(BF16)
        wu_s[...] = wu_ref[...].astype(BF16)
        wd_s[...] = wd_ref[...].astype(BF16)

    @pl.when(used)
    def _():
        x = _rows_to_2d(x_ref, MOE_ROWS).astype(BF16)
        gt = jnp.dot(x, wg_s[...], preferred_element_type=F32)
        up = jnp.dot(x, wu_s[...], preferred_element_type=F32)
        hid = (gt * jax.nn.sigmoid(gt) * up).astype(BF16)
        _store_rows(o_ref, jnp.dot(hid, wd_s[...], preferred_element_type=F32))

    @pl.when(jnp.logical_not(used))
    def _():
        o_ref[...] = jnp.zeros_like(o_ref)


def _expert_ffn(layer, block_expert, n_used, xb, w_gate, w_up, w_down):
    nblk = xb.shape[0] // (MOE_ROWS * ROW_SUB)
    wmap = lambda b, be, nu: (layer, be[b], 0, 0)
    rows = pl.BlockSpec((MOE_ROWS * ROW_SUB, LANES), lambda b, be, nu: (b, 0))
    grid_spec = pltpu.PrefetchScalarGridSpec(
        num_scalar_prefetch=2,
        grid=(nblk,),
        in_specs=[
            rows,
            pl.BlockSpec((None, None, D_MODEL, EXPERT_HIDDEN), wmap),
            pl.BlockSpec((None, None, D_MODEL, EXPERT_HIDDEN), wmap),
            pl.BlockSpec((None, None, EXPERT_HIDDEN, D_MODEL), wmap),
        ],
        out_specs=rows,
        scratch_shapes=[
            pltpu.VMEM((D_MODEL, EXPERT_HIDDEN), BF16),
            pltpu.VMEM((D_MODEL, EXPERT_HIDDEN), BF16),
            pltpu.VMEM((EXPERT_HIDDEN, D_MODEL), BF16),
        ],
    )
    return pl.pallas_call(
        _ffn_kernel,
        grid_spec=grid_spec,
        out_shape=jax.ShapeDtypeStruct(xb.shape, F32),
        compiler_params=_cparams("arbitrary"),
        name="expert_ffn",
    )(block_expert, n_used, xb, w_gate, w_up, w_down)


def _combine_kernel(slot_ref, x_ref, slab_ref, fw_ref, yb_hbm, o_ref, ybuf0, ybuf1, sem, *, tm, final_norm):
    bufs = (ybuf0, ybuf1)

    def copy(u, j):
        return pltpu.make_async_copy(_slab(yb_hbm, slot_ref[TOP_K * u + j]), _slab(bufs[j], u), sem)

    def start(u, c):
        for j in range(TOP_K):
            copy(u, j).start()
        return c

    def wait(u, c):
        for j in range(TOP_K):
            copy(u, j).wait()
        return c

    lax.fori_loop(0, tm, start, 0, unroll=8)
    lax.fori_loop(0, tm, wait, 0, unroll=8)

    slab = slab_ref[...]
    w1 = slab[:, 2:3]
    w2 = slab[:, 3:4]
    out = x_ref[...] + (_rows_to_2d(ybuf0, tm) * w1 + _rows_to_2d(ybuf1, tm) * w2)
    if final_norm:
        out = _rms(out, fw_ref[...])
    o_ref[...] = out


def _combine(slot_flat, x, slab, final_w, yb, final_norm, tm=128):
    n = x.shape[0]
    row = lambda i: (i, 0)
    return pl.pallas_call(
        functools.partial(_combine_kernel, tm=tm, final_norm=final_norm),
        grid=(n // tm,),
        in_specs=[
            pl.BlockSpec((TOP_K * tm,), lambda i: (i,), memory_space=pltpu.SMEM),
            pl.BlockSpec((tm, D_MODEL), row),
            pl.BlockSpec((tm, ROUTER_LANES), row),
            pl.BlockSpec((1, D_MODEL), lambda i: (0, 0)),
            pl.BlockSpec(memory_space=pl.ANY),
        ],
        out_specs=pl.BlockSpec((tm, D_MODEL), row),
        out_shape=jax.ShapeDtypeStruct((n, D_MODEL), F32),
        scratch_shapes=[pltpu.VMEM((tm * ROW_SUB, LANES), F32), pltpu.VMEM((tm * ROW_SUB, LANES), F32),
                        pltpu.SemaphoreType.DMA(())],
        compiler_params=_cparams("arbitrary"),
        name="moe_combine",
    )(slot_flat, x, slab, final_w, yb)


def _pad_rows(w, rows, offset=0):
    out = jnp.zeros((rows, w.shape[1]), BF16)
    return out.at[offset:offset + w.shape[0]].set(w.astype(BF16))


def _pad_cols(vec, cols, offset=0):
    out = jnp.zeros((cols,), F32)
    return out.at[offset:offset + vec.shape[0]].set(vec.astype(F32))


def _moe_plan(slab, counts_row, n_tokens):
    eid = slab[:, 0:TOP_K].astype(jnp.int32)
    rank = slab[:, 4:4 + TOP_K].astype(jnp.int32)
    counts = counts_row[EXPERT_LANE0:EXPERT_LANE0 + N_EXPERTS].astype(jnp.int32)
    padded = (counts + MOE_ROWS - 1) // MOE_ROWS * MOE_ROWS
    pad_ends = jnp.cumsum(padded)
    pad_starts = pad_ends - padded
    slot = pad_starts[eid.reshape(-1)] + rank.reshape(-1)
    nblk = n_tokens * TOP_K // MOE_ROWS + N_EXPERTS
    block_start = jnp.arange(nblk, dtype=jnp.int32) * MOE_ROWS
    block_expert = jnp.minimum(
        jnp.sum((pad_ends[None, :] <= block_start[:, None]).astype(jnp.int32), axis=1),
        N_EXPERTS - 1).astype(jnp.int32)
    n_used = (pad_ends[-1:] // MOE_ROWS).astype(jnp.int32)
    return slot, block_expert, n_used, nblk * MOE_ROWS


def kernel(x, attn_norm_w, w_in, shift_mix, attn_sinks, rwkv_w0, rwkv_w_up, rwkv_a0, rwkv_a_up, rwkv_g_up, rwkv_k_k, rwkv_k_a, rwkv_r_k, rwkv_ln_w, rwkv_ln_b, vres_down, vres_mix, vres_up, vres_v0, w_branch_a, w_branch_b, w_out, ffn_norm_w, router_group_w, router_group_b, router_expert_w, router_expert_b, expert_w_gate, expert_w_up, expert_w_down, final_norm_w):
    batch, seq, _ = x.shape
    n = batch * seq
    depth = w_in.shape[0]
    xf = x.reshape(n, D_MODEL)
    w_in_bf = w_in.astype(BF16)
    v_first = None
    for i in range(depth):
        has_vres = i > 0
        w_gates = w_in_bf[i, :, GATE_COL0:]
        mix_r = shift_mix[i, :RKV_COLS].reshape(1, RKV_COLS)
        mix_l = _pad_cols(shift_mix[i, RKV_COLS:], LORA_COLS).reshape(1, LORA_COLS)
        v0 = vres_v0[i - 1] if has_vres else jnp.zeros((RWKV_WIDTH,), F32)
        zero = jnp.zeros((RWKV_WIDTH,), F32)
        pv_pre = jnp.stack([rwkv_w0[i], rwkv_a0[i], rwkv_k_k[i], rwkv_k_a[i], v0, zero, zero, zero]).astype(F32)
        pv_post = jnp.stack([rwkv_ln_w[i], rwkv_ln_b[i], rwkv_r_k[i].reshape(-1),
                             zero, zero, zero, zero, zero]).astype(F32)
        wup = _pad_rows(rwkv_w_up[i], LANES, 0)
        aup = _pad_rows(rwkv_a_up[i], LANES, DECAY_LORA)
        gup = _pad_rows(rwkv_g_up[i], 2 * LANES, 0)
        if has_vres:
            vd = _pad_rows(vres_down[i - 1].T, LANES, 0).T
            mix_v = _pad_cols(vres_mix[i - 1], LANES).reshape(1, LANES)
            vup = _pad_rows(vres_up[i - 1], LANES, 0)
        else:
            vd, mix_v = None, None
            vup = jnp.zeros((LANES, RWKV_WIDTH), BF16)
        w_router = jnp.concatenate(
            [router_group_w[i], router_expert_w[i],
             jnp.zeros((D_MODEL, ROUTER_LANES - N_GROUPS - N_EXPERTS), F32)], axis=1)
        b_router = _pad_cols(jnp.concatenate([router_group_b[i], router_expert_b[i]]),
                             ROUTER_LANES).reshape(1, ROUTER_LANES)

        proj = _inproj(i, xf, attn_norm_w[i].reshape(1, D_MODEL), w_in_bf, w_gates, vd)
        qkv, rkv, lora, gates = proj[:4]
        zv = proj[4] if has_vres else None
        attn = _attention(qkv, attn_sinks[i], seq)
        r, lw, k2, v, al, be, g = _rwkv_pre(rkv, lora, v_first, zv, mix_r, mix_l, mix_v, pv_pre,
                                            wup, aup, gup, vup, seq)
        if not has_vres:
            v_first = v
        y_raw = _rwkv_scan(r, lw, k2, v, al, be, seq)
        y = _rwkv_post(y_raw, r, k2, v, g, pv_post)
        merged = _merge(i, attn, y, w_branch_a, w_branch_b, gates)

        x_mid, hf, slab, counts = _outproj_router(
            merged, w_out[i].astype(BF16), xf, ffn_norm_w[i].reshape(1, D_MODEL), w_router, b_router)
        slot, block_expert, n_used, cap = _moe_plan(slab, counts[0], n)
        xb = _dispatch(slot, hf, cap)
        yb = _expert_ffn(i, block_expert, n_used, xb, expert_w_gate, expert_w_up, expert_w_down)
        xf = _combine(slot, x_mid, slab, final_norm_w.reshape(1, D_MODEL), yb, final_norm=(i == depth - 1))
    return xf.reshape(batch, seq, D_MODEL)
```

```python
import functools

import jax
import jax.numpy as jnp
from jax import lax
from jax.experimental import pallas as pl
from jax.experimental.pallas import tpu as pltpu

F32 = jnp.float32
BF16 = jnp.bfloat16

D_MODEL = 2048
HEAD_DIM = 64
N_Q_HEADS = 16
N_KV_HEADS = 4
GQA_GROUP = N_Q_HEADS // N_KV_HEADS
ATTN_WIDTH = N_Q_HEADS * HEAD_DIM
KV_WIDTH = N_KV_HEADS * HEAD_DIM
WINDOW = 128
ATTN_BLOCK = 128

RWKV_HEAD = 64
N_RWKV_HEADS = 16
RWKV_WIDTH = N_RWKV_HEADS * RWKV_HEAD
DECAY_LORA = 64
AAA_LORA = 64
MV_LORA = 32
GATE_LORA = 160
RWKV_GN_EPS = 64e-5
LORA_WIDTH = DECAY_LORA + AAA_LORA + GATE_LORA

N_GROUPS = 4
EXPERTS_PER_GROUP = 8
N_EXPERTS = N_GROUPS * EXPERTS_PER_GROUP
TOP_K = 2
EXPERT_HIDDEN = D_MODEL // 4
NORM_EPS = 1e-5

LANES = 128
SUBLANES = 8
MXU_COLS = 256

SEG = 512
QKV_COLS = ATTN_WIDTH + 2 * KV_WIDTH
RKV_COLS = 3 * RWKV_WIDTH
LORA_COLS = SEG
GATE_COLS = 2 * D_MODEL
QKV_TILES = QKV_COLS // SEG
RKV_TILES = RKV_COLS // SEG
LORA_TILES = LORA_COLS // SEG
GATE_TILES = GATE_COLS // SEG
IN_TILES = QKV_TILES + RKV_TILES + LORA_TILES + GATE_TILES
FIRST_GATE_TILE = QKV_TILES + RKV_TILES + LORA_TILES
GATE_COL0 = QKV_COLS + RKV_COLS + LORA_WIDTH

SCAN_CHUNK = 64
SCAN_GROUP = 4
MOE_ROWS = 256
ROUTER_LANES = LANES
EXPERT_LANE0 = N_GROUPS

ROW_SUB = D_MODEL // LANES

VMEM_LIMIT = 56 * 1024 * 1024


def _rows_to_2d(slab_ref, rows):
    return jnp.concatenate([slab_ref[pl.ds(s, rows, stride=ROW_SUB), :] for s in range(ROW_SUB)], axis=1)


def _store_rows(slab_ref, val):
    rows = val.shape[0]
    for s in range(ROW_SUB):
        slab_ref[pl.ds(s, rows, stride=ROW_SUB), :] = val[:, s * LANES:(s + 1) * LANES]


def _slab(ref, row):
    return ref.at[pl.ds(pl.multiple_of(row * ROW_SUB, ROW_SUB), ROW_SUB)]


def _cparams(*sem):
    return pltpu.CompilerParams(dimension_semantics=sem, vmem_limit_bytes=VMEM_LIMIT)


def _bdot(a, b):
    return jnp.dot(a.astype(BF16), b.astype(BF16), preferred_element_type=F32)


def _bdot_nt(a, b):
    return lax.dot_general(a.astype(BF16), b.astype(BF16), (((1,), (1,)), ((), ())),
                           preferred_element_type=F32)


def _bdot_tn(a, b):
    return lax.dot_general(a.astype(BF16), b.astype(BF16), (((0,), (0,)), ((), ())),
                           preferred_element_type=F32)


def _dot_split_rhs(w01, x, passes):
    acc, rem = None, x
    for p in range(passes):
        part = rem.astype(BF16)
        d = jnp.dot(w01, part, preferred_element_type=F32)
        acc = d if acc is None else acc + d
        if p + 1 < passes:
            rem = rem - part.astype(F32)
    return acc


def _dot_split_lhs(x, w01, passes):
    acc, rem = None, x
    for p in range(passes):
        part = rem.astype(BF16)
        d = jnp.dot(part, w01, preferred_element_type=F32)
        acc = d if acc is None else acc + d
        if p + 1 < passes:
            rem = rem - part.astype(F32)
    return acc


def _rms(x, w):
    ms = jnp.mean(x * x, axis=-1, keepdims=True)
    return x * lax.rsqrt(ms + NORM_EPS) * w


def _inproj_kernel(*refs, has_vres):
    refs = list(refs)
    x_ref, nw_ref, w_ref, wg_ref = refs[:4]
    refs = refs[4:]
    vd_ref = refs.pop(0) if has_vres else None
    qkv_ref, rkv_ref, lora_ref, gate_ref = refs[:4]
    refs = refs[4:]
    zv_ref = refs.pop(0) if has_vres else None
    (h_scr,) = refs
    j = pl.program_id(1)

    @pl.when(j == 0)
    def _():
        h_scr[...] = _rms(x_ref[...], nw_ref[...]).astype(BF16)
        if has_vres:
            zv_ref[...] = jnp.dot(h_scr[...], vd_ref[...], preferred_element_type=F32)

    def project(wt_ref, out_ref):
        for c in range(SEG // MXU_COLS):
            cs = slice(c * MXU_COLS, (c + 1) * MXU_COLS)
            acc = jnp.dot(h_scr[...], wt_ref[:, cs], preferred_element_type=F32)
            out_ref[:, cs] = acc.astype(out_ref.dtype)

    @pl.when(j < QKV_TILES)
    def _():
        project(w_ref, qkv_ref)

    @pl.when((j >= QKV_TILES) & (j < QKV_TILES + RKV_TILES))
    def _():
        project(w_ref, rkv_ref)

    @pl.when(j == QKV_TILES + RKV_TILES)
    def _():
        project(w_ref, lora_ref)

    @pl.when(j >= FIRST_GATE_TILE)
    def _():
        project(wg_ref, gate_ref)


def _inproj(layer, x, norm_w, w_in, w_gates, vd, tm=1024):
    n = x.shape[0]
    has_vres = vd is not None
    s_rkv = QKV_TILES
    in_specs = [
        pl.BlockSpec((tm, D_MODEL), lambda i, j: (i, 0)),
        pl.BlockSpec((1, D_MODEL), lambda i, j: (0, 0)),
        pl.BlockSpec((None, D_MODEL, SEG), lambda i, j: (layer, 0, jnp.minimum(j, FIRST_GATE_TILE - 1))),
        pl.BlockSpec((D_MODEL, SEG), lambda i, j: (0, jnp.clip(j - FIRST_GATE_TILE, 0, GATE_TILES - 1))),
    ]
    args = [x, norm_w, w_in, w_gates]
    out_specs = [
        pl.BlockSpec((tm, SEG), lambda i, j: (i, jnp.clip(j, 0, QKV_TILES - 1))),
        pl.BlockSpec((tm, SEG), lambda i, j: (i, jnp.clip(j - s_rkv, 0, RKV_TILES - 1))),
        pl.BlockSpec((tm, SEG), lambda i, j: (i, 0)),
        pl.BlockSpec((tm, SEG), lambda i, j: (i, jnp.clip(j - FIRST_GATE_TILE, 0, GATE_TILES - 1))),
    ]
    out_shape = [
        jax.ShapeDtypeStruct((n, QKV_COLS), BF16),
        jax.ShapeDtypeStruct((n, RKV_COLS), F32),
        jax.ShapeDtypeStruct((n, LORA_COLS), F32),
        jax.ShapeDtypeStruct((n, GATE_COLS), BF16),
    ]
    if has_vres:
        in_specs.append(pl.BlockSpec((D_MODEL, LANES), lambda i, j: (0, 0)))
        args.append(vd)
        out_specs.append(pl.BlockSpec((tm, LANES), lambda i, j: (i, 0)))
        out_shape.append(jax.ShapeDtypeStruct((n, LANES), F32))
    return pl.pallas_call(
        functools.partial(_inproj_kernel, has_vres=has_vres),
        grid=(n // tm, IN_TILES),
        in_specs=in_specs,
        out_specs=out_specs,
        out_shape=out_shape,
        scratch_shapes=[pltpu.VMEM((tm, D_MODEL), BF16)],
        compiler_params=_cparams("parallel", "arbitrary"),
        name="inproj",
    )(*args)


def _attn_kernel(sink_ref, q_ref, kp_ref, kc_ref, vp_ref, vc_ref, bias_ref, o_ref, *, nb):
    first = (pl.program_id(0) % nb) == 0
    col = lax.broadcasted_iota(jnp.int32, (ATTN_BLOCK, 2 * ATTN_BLOCK), 1)
    pad_keys = jnp.logical_and(first, col < ATTN_BLOCK)
    scale = HEAD_DIM ** -0.5
    for hk in range(N_KV_HEADS):
        ks = slice(hk * HEAD_DIM, (hk + 1) * HEAD_DIM)
        kw = jnp.concatenate([kp_ref[:, ks], kc_ref[:, ks]], axis=0)
        vw = jnp.concatenate([vp_ref[:, ks], vc_ref[:, ks]], axis=0)
        for g in range(GQA_GROUP):
            h = hk * GQA_GROUP + g
            qh = q_ref[:, h * HEAD_DIM:(h + 1) * HEAD_DIM]
            s = lax.dot_general(qh, kw, (((1,), (1,)), ((), ())), preferred_element_type=F32)
            s = s * scale + bias_ref[h]
            s = jnp.where(pad_keys, -jnp.inf, s)
            sink = sink_ref[h]
            m = jnp.maximum(jnp.max(s, axis=-1, keepdims=True), sink)
            p = jnp.exp(s - m)
            denom = jnp.sum(p, axis=-1, keepdims=True) + jnp.exp(sink - m)
            o = jnp.dot(p.astype(BF16), vw, preferred_element_type=F32) / denom
            o_ref[:, h * HEAD_DIM:(h + 1) * HEAD_DIM] = o.astype(BF16)


def _attn_bias():
    qi = jnp.arange(ATTN_BLOCK)[:, None]
    kj = jnp.arange(2 * ATTN_BLOCK)[None, :]
    dist = qi + ATTN_BLOCK - kj
    valid = (dist >= 0) & (dist < WINDOW)
    slopes = jnp.exp2(-8.0 * jnp.arange(1, N_Q_HEADS + 1, dtype=F32) / N_Q_HEADS)
    bias = -slopes[:, None, None] * dist.astype(F32)[None]
    return jnp.where(valid[None], bias, -jnp.inf)


def _attention(qkv, sinks, seq):
    n = qkv.shape[0]
    nb = seq // ATTN_BLOCK
    kcol = ATTN_WIDTH // KV_WIDTH
    prev = lambda i: jnp.where(i % nb == 0, i, i - 1)
    return pl.pallas_call(
        functools.partial(_attn_kernel, nb=nb),
        grid=(n // ATTN_BLOCK,),
        in_specs=[
            pl.BlockSpec(memory_space=pltpu.SMEM),
            pl.BlockSpec((ATTN_BLOCK, ATTN_WIDTH), lambda i: (i, 0)),
            pl.BlockSpec((ATTN_BLOCK, KV_WIDTH), lambda i: (prev(i), kcol)),
            pl.BlockSpec((ATTN_BLOCK, KV_WIDTH), lambda i: (i, kcol)),
            pl.BlockSpec((ATTN_BLOCK, KV_WIDTH), lambda i: (prev(i), kcol + 1)),
            pl.BlockSpec((ATTN_BLOCK, KV_WIDTH), lambda i: (i, kcol + 1)),
            pl.BlockSpec((N_Q_HEADS, ATTN_BLOCK, 2 * ATTN_BLOCK), lambda i: (0, 0, 0)),
        ],
        out_specs=pl.BlockSpec((ATTN_BLOCK, ATTN_WIDTH), lambda i: (i, 0)),
        out_shape=jax.ShapeDtypeStruct((n, ATTN_WIDTH), BF16),
        compiler_params=_cparams("parallel"),
        name="swa_attention",
    )(sinks.astype(F32), qkv, qkv, qkv, qkv, qkv, _attn_bias())


def _head_ones():
    ch = jnp.arange(MXU_COLS) // RWKV_HEAD
    return (ch[:, None] == ch[None, :]).astype(BF16)


def _head_sums(x, ones, passes):
    parts = [_dot_split_lhs(x[:, c:c + MXU_COLS], ones, passes) for c in range(0, RWKV_WIDTH, MXU_COLS)]
    return jnp.concatenate(parts, axis=1)


def _rwkv_pre_kernel(*refs, has_vres, tiles_per_seq):
    refs = list(refs)
    rkv_ref, rkvp_ref, lo_ref, lop_ref = refs[:4]
    refs = refs[4:]
    if has_vres:
        vf_ref, zv_ref, zvp_ref, mixv_ref = refs[:4]
        refs = refs[4:]
    (mixr_ref, mixl_ref, pv_ref, wup_ref, aup_ref, gup_ref, vup_ref, ones_ref,
     r_out, lw_out, k_out, v_out, al_out, be_out, g_out) = refs

    first = (pl.program_id(0) % tiles_per_seq) == 0

    def shifted(z, zp_ref):
        prev_row = jnp.where(first, 0.0, zp_ref[SUBLANES - 1:SUBLANES, :])
        row = lax.broadcasted_iota(jnp.int32, z.shape, 0)
        return jnp.where(row == 0, prev_row, pltpu.roll(z, 1, 0))

    z = rkv_ref[...]
    zs = z + (shifted(z, rkvp_ref) - z) * mixr_ref[...]
    lo = lo_ref[...]
    los = lo + (shifted(lo, lop_ref) - lo) * mixl_ref[...]

    r = zs[:, :RWKV_WIDTH]
    kr = zs[:, RWKV_WIDTH:2 * RWKV_WIDTH]
    vr = zs[:, 2 * RWKV_WIDTH:]

    col = lax.broadcasted_iota(jnp.int32, los.shape, 1)
    gate_cols = (col >= DECAY_LORA + AAA_LORA) & (col < LORA_WIDTH)
    act = jnp.where(col < DECAY_LORA, jnp.tanh(los), jnp.where(gate_cols, jax.nn.sigmoid(los), los))
    act = act.astype(BF16)
    wa_in = act[:, :LANES]
    dw = jnp.dot(wa_in, wup_ref[...], preferred_element_type=F32)
    da = jnp.dot(wa_in, aup_ref[...], preferred_element_type=F32)
    g = jnp.dot(act[:, LANES:3 * LANES], gup_ref[...], preferred_element_type=F32)

    w0, a0, k_k, k_a, v0 = (pv_ref[i:i + 1, :] for i in range(5))
    u = -(w0 + dw)
    softplus = jnp.maximum(u, 0.0) + jnp.log1p(jnp.exp(-jnp.abs(u)))
    w_log = -softplus - 0.5
    a = jax.nn.sigmoid(a0 + da)
    if has_vres:
        zv = zv_ref[...]
        zvs = zv + (shifted(zv, zvp_ref) - zv) * mixv_ref[...]
        dv = jnp.dot(zvs.astype(BF16), vup_ref[...], preferred_element_type=F32)
        vr = vr + (vf_ref[...] - vr) * jax.nn.sigmoid(v0 + dv)

    kk0 = kr * k_k
    ss = _head_sums(kk0 * kk0, ones_ref[...], 2)
    kk = kk0 * (1.0 / jnp.maximum(jnp.sqrt(ss), 1e-12))

    r_out[...] = r
    lw_out[...] = -jnp.exp(w_log)
    k_out[...] = kr * (1.0 + (a - 1.0) * k_a)
    v_out[...] = vr
    al_out[...] = -kk
    be_out[...] = kk * a
    g_out[...] = g


def _rwkv_pre(rkv, lora, v_first, zv, mix_r, mix_l, mix_v, pvec, wup, aup, gup, vup, seq, tt=256):
    n = rkv.shape[0]
    has_vres = v_first is not None
    row = lambda i: (i, 0)
    prev8 = lambda i: (jnp.maximum(i * (tt // SUBLANES) - 1, 0), 0)
    const = lambda i: (0, 0)
    in_specs = [
        pl.BlockSpec((tt, RKV_COLS), row),
        pl.BlockSpec((SUBLANES, RKV_COLS), prev8),
        pl.BlockSpec((tt, LORA_COLS), row),
        pl.BlockSpec((SUBLANES, LORA_COLS), prev8),
    ]
    args = [rkv, rkv, lora, lora]
    if has_vres:
        in_specs += [
            pl.BlockSpec((tt, RWKV_WIDTH), row),
            pl.BlockSpec((tt, LANES), row),
            pl.BlockSpec((SUBLANES, LANES), prev8),
            pl.BlockSpec((1, LANES), const),
        ]
        args += [v_first, zv, zv, mix_v]
    in_specs += [
        pl.BlockSpec((1, RKV_COLS), const),
        pl.BlockSpec((1, LORA_COLS), const),
        pl.BlockSpec((SUBLANES, RWKV_WIDTH), const),
        pl.BlockSpec((LANES, RWKV_WIDTH), const),
        pl.BlockSpec((LANES, RWKV_WIDTH), const),
        pl.BlockSpec((2 * LANES, RWKV_WIDTH), const),
        pl.BlockSpec((LANES, RWKV_WIDTH), const),
        pl.BlockSpec((MXU_COLS, MXU_COLS), const),
    ]
    args += [mix_r, mix_l, pvec, wup, aup, gup, vup, _head_ones()]
    out = jax.ShapeDtypeStruct((n, RWKV_WIDTH), F32)
    return pl.pallas_call(
        functools.partial(_rwkv_pre_kernel, has_vres=has_vres, tiles_per_seq=seq // tt),
        grid=(n // tt,),
        in_specs=in_specs,
        out_specs=[pl.BlockSpec((tt, RWKV_WIDTH), row)] * 7,
        out_shape=[out] * 7,
        compiler_params=_cparams("parallel"),
        name="rwkv_pre",
    )(*args)


def _scan_kernel(r_ref, lw_ref, k_ref, v_ref, al_ref, be_ref, y_ref, s_scr):
    L = SCAN_CHUNK
    W = SCAN_GROUP * RWKV_HEAD
    n_seq = r_ref.shape[0]
    n_grp = N_RWKV_HEADS // SCAN_GROUP

    @pl.when(pl.program_id(0) == 0)
    def _():
        s_scr[...] = jnp.zeros_like(s_scr)

    row = lax.broadcasted_iota(jnp.int32, (L, L), 0)
    col = lax.broadcasted_iota(jnp.int32, (L, L), 1)
    tri = (row >= col).astype(BF16)

    t_w = lax.broadcasted_iota(jnp.int32, (L, W), 0)
    lane_w = lax.broadcasted_iota(jnp.int32, (L, W), 1)
    s_w = lane_w % RWKV_HEAD
    strict = t_w > s_w
    incl = t_w >= s_w
    eye = (t_w == s_w).astype(F32)
    blk_w = lane_w // RWKV_HEAD
    bd_mask = (lax.broadcasted_iota(jnp.int32, (W, W), 0) // RWKV_HEAD
               == lax.broadcasted_iota(jnp.int32, (W, W), 1) // RWKV_HEAD)

    def blockdiag(w):
        tiled = jnp.concatenate([w.astype(BF16)] * SCAN_GROUP, axis=0)
        return jnp.where(bd_mask, tiled, jnp.zeros_like(tiled))

    items = [(b, g) for b in range(n_seq) for g in range(n_grp)]
    ops = {}
    for b in range(n_seq):
        lw = lw_ref[b]
        cum = _dot_split_rhs(tri, lw, 3)
        p_inc = jnp.exp(cum)
        p_exc = jnp.exp(cum - lw)
        p_inv = jnp.exp(-cum)
        p_last = p_inc[L - 1:L, :]
        a_t = al_ref[b] * p_exc
        r_t = r_ref[b] * p_inc
        b_t = be_ref[b] * p_inv
        k_t = k_ref[b] * p_inv
        b_end = b_t * p_last
        k_end = k_t * p_last
        v = v_ref[b]
        for g in range(n_grp):
            gs = slice(g * W, (g + 1) * W)
            ops[(b, g)] = dict(
                ar=jnp.concatenate([a_t[:, gs], r_t[:, gs]], axis=0),
                bt=b_t[:, gs], kt=k_t[:, gs], v=v[:, gs], p_last=p_last[:, gs],
                bke=jnp.concatenate([b_end[:, gs], k_end[:, gs]], axis=0))

    gm, xs, av, s0 = {}, {}, {}, {}
    for it in items:
        o = ops[it]
        rhs = jnp.concatenate([blockdiag(o['bt']), blockdiag(o['kt'])], axis=0)
        gm[it] = _bdot_nt(o['ar'], rhs)
        s0[it] = s_scr[it[0], it[1]]
        xs[it] = _bdot_nt(o['ar'], blockdiag(s0[it]))
    a_ab, a_rb = {}, {}
    for it in items:
        g = gm[it]
        a_ab[it] = jnp.where(strict, g[:L, :W], 0.0)
        a_rb[it] = jnp.where(incl, g[L:, :W], 0.0)
        a_kk = jnp.concatenate([jnp.where(strict, g[:L, W:], 0.0), jnp.where(incl, g[L:, W:], 0.0)], axis=0)
        av[it] = _bdot(a_kk, blockdiag(ops[it]['v']))

    pw = {it: _bdot(a_ab[it], blockdiag(a_ab[it])) for it in items}
    tinv = {it: eye + a_ab[it] for it in items}
    span = 2
    while 2 * span < L:
        both = {it: _bdot(jnp.concatenate([pw[it], tinv[it]], axis=0), blockdiag(pw[it])) for it in items}
        tinv = {it: tinv[it] + both[it][L:] for it in items}
        pw = {it: both[it][:L] for it in items}
        span *= 2
    tinv = {it: tinv[it] + _bdot(tinv[it], blockdiag(pw[it])) for it in items}

    u = {it: _bdot(tinv[it], blockdiag(xs[it][:L] + av[it][:L])) for it in items}
    for it in items:
        b, g = it
        y_ref[b, :, g * W:(g + 1) * W] = xs[it][L:] + av[it][L:] + _bdot(a_rb[it], blockdiag(u[it]))
    for it in items:
        o = ops[it]
        full = _bdot_tn(jnp.concatenate([u[it], o['v']], axis=0), o['bke'])
        upd = full[:RWKV_HEAD]
        for h in range(1, SCAN_GROUP):
            upd = jnp.where(blk_w == h, full[h * RWKV_HEAD:(h + 1) * RWKV_HEAD], upd)
        s_scr[it[0], it[1]] = s0[it] * o['p_last'] + upd


def _rwkv_scan(r, lw, k, v, al, be, seq):
    n = r.shape[0]
    n_seq = n // seq
    nc = seq // SCAN_CHUNK
    spec = pl.BlockSpec((n_seq, SCAN_CHUNK, RWKV_WIDTH), lambda c: (0, c, 0))
    args = [a.reshape(n_seq, seq, RWKV_WIDTH) for a in (r, lw, k, v, al, be)]
    y = pl.pallas_call(
        _scan_kernel,
        grid=(nc,),
        in_specs=[spec] * 6,
        out_specs=spec,
        out_shape=jax.ShapeDtypeStruct((n_seq, seq, RWKV_WIDTH), F32),
        scratch_shapes=[pltpu.VMEM((n_seq, N_RWKV_HEADS // SCAN_GROUP, RWKV_HEAD, SCAN_GROUP * RWKV_HEAD), F32)],
        compiler_params=_cparams("arbitrary"),
        name="rwkv_scan",
    )(*args)
    return y.reshape(n, RWKV_WIDTH)


def _rwkv_post_kernel(y_ref, r_ref, k_ref, v_ref, g_ref, pv_ref, ones_ref, o_ref):
    ones = ones_ref[...]
    ln_w, ln_b, r_k = (pv_ref[i:i + 1, :] for i in range(3))
    y = y_ref[...]
    inv_n = 1.0 / RWKV_HEAD
    d = y - _head_sums(y, ones, 2) * inv_n
    var = _head_sums(d * d, ones, 2) * inv_n
    yn = d * lax.rsqrt(var + RWKV_GN_EPS) * ln_w + ln_b
    bonus = _head_sums(r_ref[...] * k_ref[...] * r_k, ones, 2) * v_ref[...]
    o_ref[...] = ((yn + bonus) * g_ref[...]).astype(BF16)


def _rwkv_post(y, r, k, v, g, pvec, tt=256):
    n = y.shape[0]
    row = lambda i: (i, 0)
    const = lambda i: (0, 0)
    return pl.pallas_call(
        _rwkv_post_kernel,
        grid=(n // tt,),
        in_specs=[pl.BlockSpec((tt, RWKV_WIDTH), row)] * 5 + [
            pl.BlockSpec((SUBLANES, RWKV_WIDTH), const),
            pl.BlockSpec((MXU_COLS, MXU_COLS), const),
        ],
        out_specs=pl.BlockSpec((tt, RWKV_WIDTH), row),
        out_shape=jax.ShapeDtypeStruct((n, RWKV_WIDTH), BF16),
        compiler_params=_cparams("parallel"),
        name="rwkv_post",
    )(y, r, k, v, g, pvec, _head_ones())


def _merge_kernel(a_ref, y_ref, wa_ref, wb_ref, ga_ref, gb_ref, o_ref):
    pa = jnp.dot(a_ref[...], wa_ref[...].astype(BF16), preferred_element_type=F32)
    pb = jnp.dot(y_ref[...], wb_ref[...].astype(BF16), preferred_element_type=F32)
    ga = jax.nn.sigmoid(ga_ref[...].astype(F32))
    gb = jax.nn.sigmoid(gb_ref[...].astype(F32))
    o_ref[...] = (ga * pa + gb * pb).astype(BF16)


def _merge(layer, attn, y, wa, wb, gates, tm=1024, tn=512):
    n = attn.shape[0]
    nj = D_MODEL // tn
    return pl.pallas_call(
        _merge_kernel,
        grid=(n // tm, nj),
        in_specs=[
            pl.BlockSpec((tm, ATTN_WIDTH), lambda i, j: (i, 0)),
            pl.BlockSpec((tm, RWKV_WIDTH), lambda i, j: (i, 0)),
            pl.BlockSpec((None, ATTN_WIDTH, tn), lambda i, j: (layer, 0, j)),
            pl.BlockSpec((None, RWKV_WIDTH, tn), lambda i, j: (layer, 0, j)),
            pl.BlockSpec((tm, tn), lambda i, j: (i, j)),
            pl.BlockSpec((tm, tn), lambda i, j: (i, j + nj)),
        ],
        out_specs=pl.BlockSpec((tm, tn), lambda i, j: (i, j)),
        out_shape=jax.ShapeDtypeStruct((n, D_MODEL), BF16),
        compiler_params=_cparams("parallel", "arbitrary"),
        name="gated_merge",
    )(attn, y, wa, wb, gates, gates)


def _outproj_router_kernel(m_ref, wo_ref, x_ref, nw_ref, wr_ref, br_ref,
                           xo_ref, hf_ref, slab_ref, cnt_ref, run_scr):
    @pl.when(pl.program_id(0) == 0)
    def _():
        run_scr[...] = jnp.zeros_like(run_scr)

    xn = x_ref[...] + jnp.dot(m_ref[...], wo_ref[...], preferred_element_type=F32)
    xo_ref[...] = xn
    hf = _rms(xn, nw_ref[...])
    _store_rows(hf_ref, hf)

    h_hi = hf.astype(BF16)
    h_lo = (hf - h_hi.astype(F32)).astype(BF16)
    wr = wr_ref[...]
    w_hi = wr.astype(BF16)
    w_lo = (wr - w_hi.astype(F32)).astype(BF16)
    lg = (jnp.dot(h_hi, w_hi, preferred_element_type=F32)
          + jnp.dot(h_hi, w_lo, preferred_element_type=F32)
          + jnp.dot(h_lo, w_hi, preferred_element_type=F32)) + br_ref[...]

    tm = lg.shape[0]
    lane = lax.broadcasted_iota(jnp.int32, lg.shape, 1).astype(F32)
    neg = -jnp.inf
    big = float(ROUTER_LANES)

    def first_argmax(vals):
        mx = jnp.max(vals, axis=-1, keepdims=True)
        idx = jnp.min(jnp.where(vals == mx, lane, big), axis=-1, keepdims=True)
        return mx, idx

    is_group = lane < N_GROUPS
    gmax, gsel = first_argmax(jnp.where(is_group, lg, neg))
    gp = 1.0 / jnp.sum(jnp.where(is_group, jnp.exp(lg - gmax), 0.0), axis=-1, keepdims=True)
    lo_lane = EXPERT_LANE0 + EXPERTS_PER_GROUP * gsel
    in_group = (lane >= lo_lane) & (lane < lo_lane + EXPERTS_PER_GROUP)
    el = jnp.where(in_group, lg, neg)
    v1, i1 = first_argmax(el)
    v2, i2 = first_argmax(jnp.where(lane == i1, neg, el))
    e21 = jnp.exp(v2 - v1)
    ew1 = gp / (1.0 + e21)
    ew2 = gp * e21 / (1.0 + e21)

    oh1 = lane == i1
    oh2 = lane == i2
    cnt = oh1.astype(F32) + oh2.astype(F32)
    r_i = lax.broadcasted_iota(jnp.int32, (tm, tm), 0)
    c_i = lax.broadcasted_iota(jnp.int32, (tm, tm), 1)
    before = jnp.dot((r_i > c_i).astype(BF16), cnt.astype(BF16), preferred_element_type=F32)
    tot = before + run_scr[0:1, :]
    rank1 = jnp.sum(jnp.where(oh1, tot, 0.0), axis=-1, keepdims=True)
    rank2 = jnp.sum(jnp.where(oh2, tot, 0.0), axis=-1, keepdims=True)
    run = run_scr[0:1, :] + jnp.sum(cnt, axis=0, keepdims=True)
    run_scr[...] = jnp.broadcast_to(run, run_scr.shape)
    cnt_ref[...] = jnp.broadcast_to(run, cnt_ref.shape)

    slab = jnp.where(lane == 0, i1 - EXPERT_LANE0,
           jnp.where(lane == 1, i2 - EXPERT_LANE0,
           jnp.where(lane == 2, ew1,
           jnp.where(lane == 3, ew2,
           jnp.where(lane == 4, rank1,
           jnp.where(lane == 5, rank2, 0.0))))))
    slab_ref[...] = slab


def _outproj_router(merged, w_out, x, norm_w, w_router, b_router, tm=256):
    n = x.shape[0]
    row = lambda i: (i, 0)
    const = lambda i: (0, 0)
    return pl.pallas_call(
        _outproj_router_kernel,
        grid=(n // tm,),
        in_specs=[
            pl.BlockSpec((tm, D_MODEL), row),
            pl.BlockSpec((D_MODEL, D_MODEL), const),
            pl.BlockSpec((tm, D_MODEL), row),
            pl.BlockSpec((1, D_MODEL), const),
            pl.BlockSpec((D_MODEL, ROUTER_LANES), const),
            pl.BlockSpec((1, ROUTER_LANES), const),
        ],
        out_specs=[
            pl.BlockSpec((tm, D_MODEL), row),
            pl.BlockSpec((tm * ROW_SUB, LANES), row),
            pl.BlockSpec((tm, ROUTER_LANES), row),
            pl.BlockSpec((SUBLANES, ROUTER_LANES), const),
        ],
        out_shape=[
            jax.ShapeDtypeStruct((n, D_MODEL), F32),
            jax.ShapeDtypeStruct((n * ROW_SUB, LANES), F32),
            jax.ShapeDtypeStruct((n, ROUTER_LANES), F32),
            jax.ShapeDtypeStruct((SUBLANES, ROUTER_LANES), F32),
        ],
        scratch_shapes=[pltpu.VMEM((SUBLANES, ROUTER_LANES), F32)],
        compiler_params=_cparams("arbitrary"),
        name="outproj_router",
    )(merged, w_out, x, norm_w, w_router, b_router)


def _dispatch_kernel(slot_ref, hf_ref, xb_in_hbm, xb_hbm, sem, *, tm):
    del xb_in_hbm

    def copy(u, j):
        return pltpu.make_async_copy(_slab(hf_ref, u), _slab(xb_hbm, slot_ref[TOP_K * u + j]), sem)

    def start(u, c):
        for j in range(TOP_K):
            copy(u, j).start()
        return c

    def wait(u, c):
        for j in range(TOP_K):
            copy(u, j).wait()
        return c

    lax.fori_loop(0, tm, start, 0, unroll=8)
    lax.fori_loop(0, tm, wait, 0, unroll=8)


def _dispatch(slot_flat, hf, cap, tm=128):
    n = hf.shape[0] // ROW_SUB
    xb0 = jnp.zeros((cap * ROW_SUB, LANES), F32)
    return pl.pallas_call(
        functools.partial(_dispatch_kernel, tm=tm),
        grid=(n // tm,),
        in_specs=[
            pl.BlockSpec((TOP_K * tm,), lambda i: (i,), memory_space=pltpu.SMEM),
            pl.BlockSpec((tm * ROW_SUB, LANES), lambda i: (i, 0)),
            pl.BlockSpec(memory_space=pl.ANY),
        ],
        out_specs=pl.BlockSpec(memory_space=pl.ANY),
        out_shape=jax.ShapeDtypeStruct((cap * ROW_SUB, LANES), F32),
        scratch_shapes=[pltpu.SemaphoreType.DMA(())],
        input_output_aliases={2: 0},
        compiler_params=_cparams("arbitrary"),
        name="moe_dispatch",
    )(slot_flat, hf, xb0)


def _ffn_kernel(be_ref, nu_ref, x_ref, wg_ref, wu_ref, wd_ref, o_ref, wg_s, wu_s, wd_s):
    b = pl.program_id(0)
    used = b < nu_ref[0]
    prev_e = be_ref[jnp.maximum(b - 1, 0)]
    new_expert = jnp.logical_or(b == 0, be_ref[b] != prev_e)

    @pl.when(jnp.logical_and(used, new_expert))
    def _():
        wg_s[...] = wg_ref[...].astype(BF16)
        wu_s[...] = wu_ref[...].astype(BF16)
        wd_s[...] = wd_ref[...].astype(BF16)

    @pl.when(used)
    def _():
        x = _rows_to_2d(x_ref, MOE_ROWS).astype(BF16)
        gt = jnp.dot(x, wg_s[...], preferred_element_type=F32)
        up = jnp.dot(x, wu_s[...], preferred_element_type=F32)
        hid = (gt * jax.nn.sigmoid(gt) * up).astype(BF16)
        _store_rows(o_ref, jnp.dot(hid, wd_s[...], preferred_element_type=F32))

    @pl.when(jnp.logical_not(used))
    def _():
        o_ref[...] = jnp.zeros_like(o_ref)


def _expert_ffn(layer, block_expert, n_used, xb, w_gate, w_up, w_down):
    nblk = xb.shape[0] // (MOE_ROWS * ROW_SUB)
    wmap = lambda b, be, nu: (layer, be[b], 0, 0)
    rows = pl.BlockSpec((MOE_ROWS * ROW_SUB, LANES), lambda b, be, nu: (b, 0))
    grid_spec = pltpu.PrefetchScalarGridSpec(
        num_scalar_prefetch=2,
        grid=(nblk,),
        in_specs=[
            pl.BlockSpec((MOE_ROWS * ROW_SUB, LANES), lambda b, be, nu: (jnp.minimum(b, nu[0] - 1), 0)),
            pl.BlockSpec((None, None, D_MODEL, EXPERT_HIDDEN), wmap),
            pl.BlockSpec((None, None, D_MODEL, EXPERT_HIDDEN), wmap),
            pl.BlockSpec((None, None, EXPERT_HIDDEN, D_MODEL), wmap),
        ],
        out_specs=rows,
        scratch_shapes=[
            pltpu.VMEM((D_MODEL, EXPERT_HIDDEN), BF16),
            pltpu.VMEM((D_MODEL, EXPERT_HIDDEN), BF16),
            pltpu.VMEM((EXPERT_HIDDEN, D_MODEL), BF16),
        ],
    )
    return pl.pallas_call(
        _ffn_kernel,
        grid_spec=grid_spec,
        out_shape=jax.ShapeDtypeStruct(xb.shape, F32),
        compiler_params=_cparams("arbitrary"),
        name="expert_ffn",
    )(block_expert, n_used, xb, w_gate, w_up, w_down)


def _combine_kernel(slot_ref, slot_next_ref, x_ref, slab_ref, fw_ref, yb_hbm, o_ref,
                    buf00, buf01, buf10, buf11, sems, *, tm, n_tiles, final_norm):
    i = pl.program_id(0)
    bufs = ((buf00, buf01), (buf10, buf11))

    def copies(slots, p, u):
        return [pltpu.make_async_copy(_slab(yb_hbm, slots[TOP_K * u + j]), _slab(bufs[p][j], u), sems.at[p])
                for j in range(TOP_K)]

    def start_all(slots, p):
        def body(u, c):
            for cp in copies(slots, p, u):
                cp.start()
            return c
        lax.fori_loop(0, tm, body, 0, unroll=8)

    def wait_all(p):
        def body(u, c):
            for cp in copies(slot_ref, p, u):
                cp.wait()
            return c
        lax.fori_loop(0, tm, body, 0, unroll=8)

    @pl.when(i == 0)
    def _():
        start_all(slot_ref, 0)

    for p in range(2):
        @pl.when(i % 2 == p)
        def _():
            @pl.when(i + 1 < n_tiles)
            def _():
                start_all(slot_next_ref, 1 - p)

            wait_all(p)
            slab = slab_ref[...]
            w1 = slab[:, 2:3]
            w2 = slab[:, 3:4]
            out = x_ref[...] + (_rows_to_2d(bufs[p][0], tm) * w1 + _rows_to_2d(bufs[p][1], tm) * w2)
            if final_norm:
                out = _rms(out, fw_ref[...])
            o_ref[...] = out


def _combine(slot_flat, x, slab, final_w, yb, final_norm, tm=128):
    n = x.shape[0]
    n_tiles = n // tm
    row = lambda i: (i, 0)
    row_buf = pltpu.VMEM((tm * ROW_SUB, LANES), F32)
    return pl.pallas_call(
        functools.partial(_combine_kernel, tm=tm, n_tiles=n_tiles, final_norm=final_norm),
        grid=(n_tiles,),
        in_specs=[
            pl.BlockSpec((TOP_K * tm,), lambda i: (i,), memory_space=pltpu.SMEM),
            pl.BlockSpec((TOP_K * tm,), lambda i: (jnp.minimum(i + 1, n_tiles - 1),), memory_space=pltpu.SMEM),
            pl.BlockSpec((tm, D_MODEL), row),
            pl.BlockSpec((tm, ROUTER_LANES), row),
            pl.BlockSpec((1, D_MODEL), lambda i: (0, 0)),
            pl.BlockSpec(memory_space=pl.ANY),
        ],
        out_specs=pl.BlockSpec((tm, D_MODEL), row),
        out_shape=jax.ShapeDtypeStruct((n, D_MODEL), F32),
        scratch_shapes=[row_buf, row_buf, row_buf, row_buf, pltpu.SemaphoreType.DMA((2,))],
        compiler_params=_cparams("arbitrary"),
        name="moe_combine",
    )(slot_flat, slot_flat, x, slab, final_w, yb)


def _pad_rows(w, rows, offset=0):
    out = jnp.zeros((rows, w.shape[1]), BF16)
    return out.at[offset:offset + w.shape[0]].set(w.astype(BF16))


def _pad_cols(vec, cols, offset=0):
    out = jnp.zeros((cols,), F32)
    return out.at[offset:offset + vec.shape[0]].set(vec.astype(F32))


def _moe_plan(slab, counts_row, n_tokens):
    eid = slab[:, 0:TOP_K].astype(jnp.int32)
    rank = slab[:, 4:4 + TOP_K].astype(jnp.int32)
    counts = counts_row[EXPERT_LANE0:EXPERT_LANE0 + N_EXPERTS].astype(jnp.int32)
    padded = (counts + MOE_ROWS - 1) // MOE_ROWS * MOE_ROWS
    pad_ends = jnp.cumsum(padded)
    pad_starts = pad_ends - padded
    slot = pad_starts[eid.reshape(-1)] + rank.reshape(-1)
    nblk = n_tokens * TOP_K // MOE_ROWS + N_EXPERTS
    block_start = jnp.arange(nblk, dtype=jnp.int32) * MOE_ROWS
    block_expert = jnp.minimum(
        jnp.sum((pad_ends[None, :] <= block_start[:, None]).astype(jnp.int32), axis=1),
        N_EXPERTS - 1).astype(jnp.int32)
    n_used = (pad_ends[-1:] // MOE_ROWS).astype(jnp.int32)
    return slot, block_expert, n_used, nblk * MOE_ROWS


def kernel(x, attn_norm_w, w_in, shift_mix, attn_sinks, rwkv_w0, rwkv_w_up, rwkv_a0, rwkv_a_up, rwkv_g_up, rwkv_k_k, rwkv_k_a, rwkv_r_k, rwkv_ln_w, rwkv_ln_b, vres_down, vres_mix, vres_up, vres_v0, w_branch_a, w_branch_b, w_out, ffn_norm_w, router_group_w, router_group_b, router_expert_w, router_expert_b, expert_w_gate, expert_w_up, expert_w_down, final_norm_w):
    batch, seq, _ = x.shape
    n = batch * seq
    depth = w_in.shape[0]
    xf = x.reshape(n, D_MODEL)
    w_in_bf = w_in.astype(BF16)
    v_first = None
    for i in range(depth):
        has_vres = i > 0
        w_gates = w_in_bf[i, :, GATE_COL0:]
        mix_r = shift_mix[i, :RKV_COLS].reshape(1, RKV_COLS)
        mix_l = _pad_cols(shift_mix[i, RKV_COLS:], LORA_COLS).reshape(1, LORA_COLS)
        v0 = vres_v0[i - 1] if has_vres else jnp.zeros((RWKV_WIDTH,), F32)
        zero = jnp.zeros((RWKV_WIDTH,), F32)
        pv_pre = jnp.stack([rwkv_w0[i], rwkv_a0[i], rwkv_k_k[i], rwkv_k_a[i], v0, zero, zero, zero]).astype(F32)
        pv_post = jnp.stack([rwkv_ln_w[i], rwkv_ln_b[i], rwkv_r_k[i].reshape(-1),
                             zero, zero, zero, zero, zero]).astype(F32)
        wup = _pad_rows(rwkv_w_up[i], LANES, 0)
        aup = _pad_rows(rwkv_a_up[i], LANES, DECAY_LORA)
        gup = _pad_rows(rwkv_g_up[i], 2 * LANES, 0)
        if has_vres:
            vd = _pad_rows(vres_down[i - 1].T, LANES, 0).T
            mix_v = _pad_cols(vres_mix[i - 1], LANES).reshape(1, LANES)
            vup = _pad_rows(vres_up[i - 1], LANES, 0)
        else:
            vd, mix_v = None, None
            vup = jnp.zeros((LANES, RWKV_WIDTH), BF16)
        w_router = jnp.concatenate(
            [router_group_w[i], router_expert_w[i],
             jnp.zeros((D_MODEL, ROUTER_LANES - N_GROUPS - N_EXPERTS), F32)], axis=1)
        b_router = _pad_cols(jnp.concatenate([router_group_b[i], router_expert_b[i]]),
                             ROUTER_LANES).reshape(1, ROUTER_LANES)

        proj = _inproj(i, xf, attn_norm_w[i].reshape(1, D_MODEL), w_in_bf, w_gates, vd)
        qkv, rkv, lora, gates = proj[:4]
        zv = proj[4] if has_vres else None
        attn = _attention(qkv, attn_sinks[i], seq)
        r, lw, k2, v, al, be, g = _rwkv_pre(rkv, lora, v_first, zv, mix_r, mix_l, mix_v, pv_pre,
                                            wup, aup, gup, vup, seq)
        if not has_vres:
            v_first = v
        y_raw = _rwkv_scan(r, lw, k2, v, al, be, seq)
        y = _rwkv_post(y_raw, r, k2, v, g, pv_post)
        merged = _merge(i, attn, y, w_branch_a, w_branch_b, gates)

        x_mid, hf, slab, counts = _outproj_router(
            merged, w_out[i].astype(BF16), xf, ffn_norm_w[i].reshape(1, D_MODEL), w_router, b_router)
        slot, block_expert, n_used, cap = _moe_plan(slab, counts[0], n)
        xb = _dispatch(slot, hf, cap)
        yb = _expert_ffn(i, block_expert, n_used, xb, expert_w_gate, expert_w_up, expert_w_down)
        xf = _combine(slot, x_mid, slab, final_norm_w.reshape(1, D_MODEL), yb, final_norm=(i == depth - 1))
    return xf.reshape(batch, seq, D_MODEL)
```

```python
import functools
from typing import NamedTuple

import jax
import jax.numpy as jnp
from jax import lax
from jax.experimental import pallas as pl
from jax.experimental.pallas import tpu as pltpu

F32 = jnp.float32
BF16 = jnp.bfloat16

D_MODEL = 2048
HEAD_DIM = 64
N_Q_HEADS = 16
N_KV_HEADS = 4
GQA_GROUP = N_Q_HEADS // N_KV_HEADS
ATTN_WIDTH = N_Q_HEADS * HEAD_DIM
KV_WIDTH = N_KV_HEADS * HEAD_DIM
WINDOW = 128
ATTN_BLOCK = 128

RWKV_HEAD = 64
N_RWKV_HEADS = 16
RWKV_WIDTH = N_RWKV_HEADS * RWKV_HEAD
DECAY_LORA = 64
AAA_LORA = 64
MV_LORA = 32
GATE_LORA = 160
RWKV_GN_EPS = 64e-5
LORA_WIDTH = DECAY_LORA + AAA_LORA + GATE_LORA

N_GROUPS = 4
EXPERTS_PER_GROUP = 8
N_EXPERTS = N_GROUPS * EXPERTS_PER_GROUP
TOP_K = 2
EXPERT_HIDDEN = D_MODEL // 4
NORM_EPS = 1e-5

LANES = 128
SUBLANES = 8
MXU_COLS = 256

SEG = 512
QKV_COLS = ATTN_WIDTH + 2 * KV_WIDTH
RKV_COLS = 3 * RWKV_WIDTH
LORA_COLS = SEG
GATE_COLS = 2 * D_MODEL
QKV_TILES = QKV_COLS // SEG
RKV_TILES = RKV_COLS // SEG
LORA_TILES = LORA_COLS // SEG
GATE_TILES = GATE_COLS // SEG
IN_TILES = QKV_TILES + RKV_TILES + LORA_TILES + GATE_TILES
FIRST_GATE_TILE = QKV_TILES + RKV_TILES + LORA_TILES
GATE_COL0 = QKV_COLS + RKV_COLS + LORA_WIDTH

SCAN_CHUNK = 64
SCAN_GROUP = 4
MOE_ROWS = 256
ROUTER_LANES = LANES
EXPERT_LANE0 = N_GROUPS

ROW_SUB = D_MODEL // LANES

VMEM_LIMIT = 56 * 1024 * 1024


def _rows_to_2d(slab_ref, rows):
    return jnp.concatenate([slab_ref[pl.ds(s, rows, stride=ROW_SUB), :] for s in range(ROW_SUB)], axis=1)


def _store_rows(slab_ref, val):
    rows = val.shape[0]
    for s in range(ROW_SUB):
        slab_ref[pl.ds(s, rows, stride=ROW_SUB), :] = val[:, s * LANES:(s + 1) * LANES]


def _slab(ref, row):
    return ref.at[pl.ds(pl.multiple_of(row * ROW_SUB, ROW_SUB), ROW_SUB)]


def _cparams(*sem):
    return pltpu.CompilerParams(dimension_semantics=sem, vmem_limit_bytes=VMEM_LIMIT)


def _bdot(a, b):
    return jnp.dot(a.astype(BF16), b.astype(BF16), preferred_element_type=F32)


def _bdot_nt(a, b):
    return lax.dot_general(a.astype(BF16), b.astype(BF16), (((1,), (1,)), ((), ())),
                           preferred_element_type=F32)


def _bdot_tn(a, b):
    return lax.dot_general(a.astype(BF16), b.astype(BF16), (((0,), (0,)), ((), ())),
                           preferred_element_type=F32)


def _dot_split_rhs(w01, x, passes):
    acc, rem = None, x
    for p in range(passes):
        part = rem.astype(BF16)
        d = jnp.dot(w01, part, preferred_element_type=F32)
        acc = d if acc is None else acc + d
        if p + 1 < passes:
            rem = rem - part.astype(F32)
    return acc


def _dot_split_lhs(x, w01, passes):
    acc, rem = None, x
    for p in range(passes):
        part = rem.astype(BF16)
        d = jnp.dot(part, w01, preferred_element_type=F32)
        acc = d if acc is None else acc + d
        if p + 1 < passes:
            rem = rem - part.astype(F32)
    return acc


def _rms(x, w):
    ms = jnp.mean(x * x, axis=-1, keepdims=True)
    return x * lax.rsqrt(ms + NORM_EPS) * w


def _inproj_kernel(*refs, has_vres):
    refs = list(refs)
    x_ref, nw_ref, w_ref, wg_ref = refs[:4]
    refs = refs[4:]
    vd_ref = refs.pop(0) if has_vres else None
    qkv_ref, rkv_ref, lora_ref, gate_ref = refs[:4]
    refs = refs[4:]
    zv_ref = refs.pop(0) if has_vres else None
    (h_scr,) = refs
    j = pl.program_id(1)

    @pl.when(j == 0)
    def _():
        h_scr[...] = _rms(x_ref[...], nw_ref[...]).astype(BF16)
        if has_vres:
            zv_ref[...] = jnp.dot(h_scr[...], vd_ref[...], preferred_element_type=F32)

    def project(wt_ref, out_ref):
        for c in range(SEG // MXU_COLS):
            cs = slice(c * MXU_COLS, (c + 1) * MXU_COLS)
            acc = jnp.dot(h_scr[...], wt_ref[:, cs], preferred_element_type=F32)
            out_ref[:, cs] = acc.astype(out_ref.dtype)

    @pl.when(j < QKV_TILES)
    def _():
        project(w_ref, qkv_ref)

    @pl.when((j >= QKV_TILES) & (j < QKV_TILES + RKV_TILES))
    def _():
        project(w_ref, rkv_ref)

    @pl.when(j == QKV_TILES + RKV_TILES)
    def _():
        project(w_ref, lora_ref)

    @pl.when(j >= FIRST_GATE_TILE)
    def _():
        project(wg_ref, gate_ref)


def _inproj(layer, x, norm_w, w_in, w_gates, vd, tm=1024):
    n = x.shape[0]
    has_vres = vd is not None
    s_rkv = QKV_TILES
    in_specs = [
        pl.BlockSpec((tm, D_MODEL), lambda i, j: (i, 0)),
        pl.BlockSpec((1, D_MODEL), lambda i, j: (0, 0)),
        pl.BlockSpec((None, D_MODEL, SEG), lambda i, j: (layer, 0, jnp.minimum(j, FIRST_GATE_TILE - 1))),
        pl.BlockSpec((D_MODEL, SEG), lambda i, j: (0, jnp.clip(j - FIRST_GATE_TILE, 0, GATE_TILES - 1))),
    ]
    args = [x, norm_w, w_in, w_gates]
    out_specs = [
        pl.BlockSpec((tm, SEG), lambda i, j: (i, jnp.clip(j, 0, QKV_TILES - 1))),
        pl.BlockSpec((tm, SEG), lambda i, j: (i, jnp.clip(j - s_rkv, 0, RKV_TILES - 1))),
        pl.BlockSpec((tm, SEG), lambda i, j: (i, 0)),
        pl.BlockSpec((tm, SEG), lambda i, j: (i, jnp.clip(j - FIRST_GATE_TILE, 0, GATE_TILES - 1))),
    ]
    out_shape = [
        jax.ShapeDtypeStruct((n, QKV_COLS), BF16),
        jax.ShapeDtypeStruct((n, RKV_COLS), F32),
        jax.ShapeDtypeStruct((n, LORA_COLS), F32),
        jax.ShapeDtypeStruct((n, GATE_COLS), BF16),
    ]
    if has_vres:
        in_specs.append(pl.BlockSpec((D_MODEL, LANES), lambda i, j: (0, 0)))
        args.append(vd)
        out_specs.append(pl.BlockSpec((tm, LANES), lambda i, j: (i, 0)))
        out_shape.append(jax.ShapeDtypeStruct((n, LANES), F32))
    return pl.pallas_call(
        functools.partial(_inproj_kernel, has_vres=has_vres),
        grid=(n // tm, IN_TILES),
        in_specs=in_specs,
        out_specs=out_specs,
        out_shape=out_shape,
        scratch_shapes=[pltpu.VMEM((tm, D_MODEL), BF16)],
        compiler_params=_cparams("parallel", "arbitrary"),
        name="inproj",
    )(*args)


def _attn_kernel(sink_ref, q_ref, kp_ref, kc_ref, vp_ref, vc_ref, bias_ref, o_ref, *, nb):
    first = (pl.program_id(0) % nb) == 0
    col = lax.broadcasted_iota(jnp.int32, (ATTN_BLOCK, 2 * ATTN_BLOCK), 1)
    pad_keys = jnp.logical_and(first, col < ATTN_BLOCK)
    scale = HEAD_DIM ** -0.5
    for hk in range(N_KV_HEADS):
        ks = slice(hk * HEAD_DIM, (hk + 1) * HEAD_DIM)
        kw = jnp.concatenate([kp_ref[:, ks], kc_ref[:, ks]], axis=0)
        vw = jnp.concatenate([vp_ref[:, ks], vc_ref[:, ks]], axis=0)
        for g in range(GQA_GROUP):
            h = hk * GQA_GROUP + g
            qh = q_ref[:, h * HEAD_DIM:(h + 1) * HEAD_DIM]
            s = lax.dot_general(qh, kw, (((1,), (1,)), ((), ())), preferred_element_type=F32)
            s = s * scale + bias_ref[h]
            s = jnp.where(pad_keys, -jnp.inf, s)
            sink = sink_ref[h]
            m = jnp.maximum(jnp.max(s, axis=-1, keepdims=True), sink)
            p = jnp.exp(s - m)
            denom = jnp.sum(p, axis=-1, keepdims=True) + jnp.exp(sink - m)
            o = jnp.dot(p.astype(BF16), vw, preferred_element_type=F32) / denom
            o_ref[:, h * HEAD_DIM:(h + 1) * HEAD_DIM] = o.astype(BF16)


def _attn_bias():
    qi = jnp.arange(ATTN_BLOCK)[:, None]
    kj = jnp.arange(2 * ATTN_BLOCK)[None, :]
    dist = qi + ATTN_BLOCK - kj
    valid = (dist >= 0) & (dist < WINDOW)
    slopes = jnp.exp2(-8.0 * jnp.arange(1, N_Q_HEADS + 1, dtype=F32) / N_Q_HEADS)
    bias = -slopes[:, None, None] * dist.astype(F32)[None]
    return jnp.where(valid[None], bias, -jnp.inf)


def _attention(qkv, sinks, seq):
    n = qkv.shape[0]
    nb = seq // ATTN_BLOCK
    kcol = ATTN_WIDTH // KV_WIDTH
    prev = lambda i: jnp.where(i % nb == 0, i, i - 1)
    return pl.pallas_call(
        functools.partial(_attn_kernel, nb=nb),
        grid=(n // ATTN_BLOCK,),
        in_specs=[
            pl.BlockSpec(memory_space=pltpu.SMEM),
            pl.BlockSpec((ATTN_BLOCK, ATTN_WIDTH), lambda i: (i, 0)),
            pl.BlockSpec((ATTN_BLOCK, KV_WIDTH), lambda i: (prev(i), kcol)),
            pl.BlockSpec((ATTN_BLOCK, KV_WIDTH), lambda i: (i, kcol)),
            pl.BlockSpec((ATTN_BLOCK, KV_WIDTH), lambda i: (prev(i), kcol + 1)),
            pl.BlockSpec((ATTN_BLOCK, KV_WIDTH), lambda i: (i, kcol + 1)),
            pl.BlockSpec((N_Q_HEADS, ATTN_BLOCK, 2 * ATTN_BLOCK), lambda i: (0, 0, 0)),
        ],
        out_specs=pl.BlockSpec((ATTN_BLOCK, ATTN_WIDTH), lambda i: (i, 0)),
        out_shape=jax.ShapeDtypeStruct((n, ATTN_WIDTH), BF16),
        compiler_params=_cparams("parallel"),
        name="swa_attention",
    )(sinks.astype(F32), qkv, qkv, qkv, qkv, qkv, _attn_bias())


def _head_ones():
    ch = jnp.arange(MXU_COLS) // RWKV_HEAD
    return (ch[:, None] == ch[None, :]).astype(BF16)


def _head_sums(x, ones, passes):
    parts = [_dot_split_lhs(x[:, c:c + MXU_COLS], ones, passes) for c in range(0, RWKV_WIDTH, MXU_COLS)]
    return jnp.concatenate(parts, axis=1)


def _rwkv_pre_kernel(*refs, has_vres, tiles_per_seq):
    refs = list(refs)
    rkv_ref, rkvp_ref, lo_ref, lop_ref = refs[:4]
    refs = refs[4:]
    if has_vres:
        vf_ref, zv_ref, zvp_ref, mixv_ref = refs[:4]
        refs = refs[4:]
    (mixr_ref, mixl_ref, pv_ref, wup_ref, aup_ref, gup_ref, vup_ref, ones_ref,
     r_out, lw_out, k_out, v_out, al_out, be_out, g_out) = refs

    first = (pl.program_id(0) % tiles_per_seq) == 0

    def shifted(z, zp_ref):
        prev_row = jnp.where(first, 0.0, zp_ref[SUBLANES - 1:SUBLANES, :])
        row = lax.broadcasted_iota(jnp.int32, z.shape, 0)
        return jnp.where(row == 0, prev_row, pltpu.roll(z, 1, 0))

    z = rkv_ref[...]
    zs = z + (shifted(z, rkvp_ref) - z) * mixr_ref[...]
    lo = lo_ref[...]
    los = lo + (shifted(lo, lop_ref) - lo) * mixl_ref[...]

    r = zs[:, :RWKV_WIDTH]
    kr = zs[:, RWKV_WIDTH:2 * RWKV_WIDTH]
    vr = zs[:, 2 * RWKV_WIDTH:]

    col = lax.broadcasted_iota(jnp.int32, los.shape, 1)
    gate_cols = (col >= DECAY_LORA + AAA_LORA) & (col < LORA_WIDTH)
    act = jnp.where(col < DECAY_LORA, jnp.tanh(los), jnp.where(gate_cols, jax.nn.sigmoid(los), los))
    act = act.astype(BF16)
    wa_in = act[:, :LANES]
    dw = jnp.dot(wa_in, wup_ref[...], preferred_element_type=F32)
    da = jnp.dot(wa_in, aup_ref[...], preferred_element_type=F32)
    g = jnp.dot(act[:, LANES:3 * LANES], gup_ref[...], preferred_element_type=F32)

    w0, a0, k_k, k_a, v0 = (pv_ref[i:i + 1, :] for i in range(5))
    u = -(w0 + dw)
    softplus = jnp.maximum(u, 0.0) + jnp.log1p(jnp.exp(-jnp.abs(u)))
    w_log = -softplus - 0.5
    a = jax.nn.sigmoid(a0 + da)
    if has_vres:
        zv = zv_ref[...]
        zvs = zv + (shifted(zv, zvp_ref) - zv) * mixv_ref[...]
        dv = jnp.dot(zvs.astype(BF16), vup_ref[...], preferred_element_type=F32)
        vr = vr + (vf_ref[...] - vr) * jax.nn.sigmoid(v0 + dv)

    kk0 = kr * k_k
    ss = _head_sums(kk0 * kk0, ones_ref[...], 2)
    kk = kk0 * (1.0 / jnp.maximum(jnp.sqrt(ss), 1e-12))

    r_out[...] = r
    lw_out[...] = -jnp.exp(w_log)
    k_out[...] = kr * (1.0 + (a - 1.0) * k_a)
    v_out[...] = vr
    al_out[...] = -kk
    be_out[...] = kk * a
    g_out[...] = g


def _rwkv_pre(rkv, lora, v_first, zv, mix_r, mix_l, mix_v, pvec, wup, aup, gup, vup, seq, tt=256):
    n = rkv.shape[0]
    has_vres = v_first is not None
    row = lambda i: (i, 0)
    prev8 = lambda i: (jnp.maximum(i * (tt // SUBLANES) - 1, 0), 0)
    const = lambda i: (0, 0)
    in_specs = [
        pl.BlockSpec((tt, RKV_COLS), row),
        pl.BlockSpec((SUBLANES, RKV_COLS), prev8),
        pl.BlockSpec((tt, LORA_COLS), row),
        pl.BlockSpec((SUBLANES, LORA_COLS), prev8),
    ]
    args = [rkv, rkv, lora, lora]
    if has_vres:
        in_specs += [
            pl.BlockSpec((tt, RWKV_WIDTH), row),
            pl.BlockSpec((tt, LANES), row),
            pl.BlockSpec((SUBLANES, LANES), prev8),
            pl.BlockSpec((1, LANES), const),
        ]
        args += [v_first, zv, zv, mix_v]
    in_specs += [
        pl.BlockSpec((1, RKV_COLS), const),
        pl.BlockSpec((1, LORA_COLS), const),
        pl.BlockSpec((SUBLANES, RWKV_WIDTH), const),
        pl.BlockSpec((LANES, RWKV_WIDTH), const),
        pl.BlockSpec((LANES, RWKV_WIDTH), const),
        pl.BlockSpec((2 * LANES, RWKV_WIDTH), const),
        pl.BlockSpec((LANES, RWKV_WIDTH), const),
        pl.BlockSpec((MXU_COLS, MXU_COLS), const),
    ]
    args += [mix_r, mix_l, pvec, wup, aup, gup, vup, _head_ones()]
    out = jax.ShapeDtypeStruct((n, RWKV_WIDTH), F32)
    return pl.pallas_call(
        functools.partial(_rwkv_pre_kernel, has_vres=has_vres, tiles_per_seq=seq // tt),
        grid=(n // tt,),
        in_specs=in_specs,
        out_specs=[pl.BlockSpec((tt, RWKV_WIDTH), row)] * 7,
        out_shape=[out] * 7,
        compiler_params=_cparams("parallel"),
        name="rwkv_pre",
    )(*args)


def _scan_kernel(r_ref, lw_ref, k_ref, v_ref, al_ref, be_ref, y_ref, s_scr):
    L = SCAN_CHUNK
    W = SCAN_GROUP * RWKV_HEAD
    n_seq = r_ref.shape[0]
    n_grp = N_RWKV_HEADS // SCAN_GROUP

    @pl.when(pl.program_id(0) == 0)
    def _():
        s_scr[...] = jnp.zeros_like(s_scr)

    row = lax.broadcasted_iota(jnp.int32, (L, L), 0)
    col = lax.broadcasted_iota(jnp.int32, (L, L), 1)
    tri = (row >= col).astype(BF16)

    t_w = lax.broadcasted_iota(jnp.int32, (L, W), 0)
    lane_w = lax.broadcasted_iota(jnp.int32, (L, W), 1)
    s_w = lane_w % RWKV_HEAD
    strict = t_w > s_w
    incl = t_w >= s_w
    eye = (t_w == s_w).astype(F32)
    blk_w = lane_w // RWKV_HEAD
    bd_mask = (lax.broadcasted_iota(jnp.int32, (W, W), 0) // RWKV_HEAD
               == lax.broadcasted_iota(jnp.int32, (W, W), 1) // RWKV_HEAD)

    def blockdiag(w):
        tiled = jnp.concatenate([w.astype(BF16)] * SCAN_GROUP, axis=0)
        return jnp.where(bd_mask, tiled, jnp.zeros_like(tiled))

    items = [(b, g) for b in range(n_seq) for g in range(n_grp)]
    ops = {}
    for b in range(n_seq):
        lw = lw_ref[b]
        cum = _dot_split_rhs(tri, lw, 3)
        p_inc = jnp.exp(cum)
        p_exc = jnp.exp(cum - lw)
        p_inv = jnp.exp(-cum)
        p_last = p_inc[L - 1:L, :]
        a_t = al_ref[b] * p_exc
        r_t = r_ref[b] * p_inc
        b_t = be_ref[b] * p_inv
        k_t = k_ref[b] * p_inv
        b_end = b_t * p_last
        k_end = k_t * p_last
        v = v_ref[b]
        for g in range(n_grp):
            gs = slice(g * W, (g + 1) * W)
            ops[(b, g)] = dict(
                ar=jnp.concatenate([a_t[:, gs], r_t[:, gs]], axis=0),
                bt=b_t[:, gs], kt=k_t[:, gs], v=v[:, gs], p_last=p_last[:, gs],
                bke=jnp.concatenate([b_end[:, gs], k_end[:, gs]], axis=0))

    gm, xs, av, s0 = {}, {}, {}, {}
    for it in items:
        o = ops[it]
        rhs = jnp.concatenate([blockdiag(o['bt']), blockdiag(o['kt'])], axis=0)
        gm[it] = _bdot_nt(o['ar'], rhs)
        s0[it] = s_scr[it[0], it[1]]
        xs[it] = _bdot_nt(o['ar'], blockdiag(s0[it]))
    a_ab, a_rb = {}, {}
    for it in items:
        g = gm[it]
        a_ab[it] = jnp.where(strict, g[:L, :W], 0.0)
        a_rb[it] = jnp.where(incl, g[L:, :W], 0.0)
        a_kk = jnp.concatenate([jnp.where(strict, g[:L, W:], 0.0), jnp.where(incl, g[L:, W:], 0.0)], axis=0)
        av[it] = _bdot(a_kk, blockdiag(ops[it]['v']))

    pw = {it: _bdot(a_ab[it], blockdiag(a_ab[it])) for it in items}
    tinv = {it: eye + a_ab[it] for it in items}
    span = 2
    while 2 * span < L:
        both = {it: _bdot(jnp.concatenate([pw[it], tinv[it]], axis=0), blockdiag(pw[it])) for it in items}
        tinv = {it: tinv[it] + both[it][L:] for it in items}
        pw = {it: both[it][:L] for it in items}
        span *= 2
    tinv = {it: tinv[it] + _bdot(tinv[it], blockdiag(pw[it])) for it in items}

    u = {it: _bdot(tinv[it], blockdiag(xs[it][:L] + av[it][:L])) for it in items}
    for it in items:
        b, g = it
        y_ref[b, :, g * W:(g + 1) * W] = xs[it][L:] + av[it][L:] + _bdot(a_rb[it], blockdiag(u[it]))
    for it in items:
        o = ops[it]
        full = _bdot_tn(jnp.concatenate([u[it], o['v']], axis=0), o['bke'])
        upd = full[:RWKV_HEAD]
        for h in range(1, SCAN_GROUP):
            upd = jnp.where(blk_w == h, full[h * RWKV_HEAD:(h + 1) * RWKV_HEAD], upd)
        s_scr[it[0], it[1]] = s0[it] * o['p_last'] + upd


def _rwkv_scan(r, lw, k, v, al, be, seq):
    n = r.shape[0]
    n_seq = n // seq
    nc = seq // SCAN_CHUNK
    spec = pl.BlockSpec((n_seq, SCAN_CHUNK, RWKV_WIDTH), lambda c: (0, c, 0))
    args = [a.reshape(n_seq, seq, RWKV_WIDTH) for a in (r, lw, k, v, al, be)]
    y = pl.pallas_call(
        _scan_kernel,
        grid=(nc,),
        in_specs=[spec] * 6,
        out_specs=spec,
        out_shape=jax.ShapeDtypeStruct((n_seq, seq, RWKV_WIDTH), F32),
        scratch_shapes=[pltpu.VMEM((n_seq, N_RWKV_HEADS // SCAN_GROUP, RWKV_HEAD, SCAN_GROUP * RWKV_HEAD), F32)],
        compiler_params=_cparams("arbitrary"),
        name="rwkv_scan",
    )(*args)
    return y.reshape(n, RWKV_WIDTH)


def _rwkv_post_kernel(y_ref, r_ref, k_ref, v_ref, g_ref, pv_ref, ones_ref, o_ref):
    ones = ones_ref[...]
    ln_w, ln_b, r_k = (pv_ref[i:i + 1, :] for i in range(3))
    y = y_ref[...]
    inv_n = 1.0 / RWKV_HEAD
    d = y - _head_sums(y, ones, 2) * inv_n
    var = _head_sums(d * d, ones, 2) * inv_n
    yn = d * lax.rsqrt(var + RWKV_GN_EPS) * ln_w + ln_b
    bonus = _head_sums(r_ref[...] * k_ref[...] * r_k, ones, 2) * v_ref[...]
    o_ref[...] = ((yn + bonus) * g_ref[...]).astype(BF16)


def _rwkv_post(y, r, k, v, g, pvec, tt=256):
    n = y.shape[0]
    row = lambda i: (i, 0)
    const = lambda i: (0, 0)
    return pl.pallas_call(
        _rwkv_post_kernel,
        grid=(n // tt,),
        in_specs=[pl.BlockSpec((tt, RWKV_WIDTH), row)] * 5 + [
            pl.BlockSpec((SUBLANES, RWKV_WIDTH), const),
            pl.BlockSpec((MXU_COLS, MXU_COLS), const),
        ],
        out_specs=pl.BlockSpec((tt, RWKV_WIDTH), row),
        out_shape=jax.ShapeDtypeStruct((n, RWKV_WIDTH), BF16),
        compiler_params=_cparams("parallel"),
        name="rwkv_post",
    )(y, r, k, v, g, pvec, _head_ones())


def _merge_kernel(a_ref, y_ref, wa_ref, wb_ref, ga_ref, gb_ref, o_ref):
    pa = jnp.dot(a_ref[...], wa_ref[...].astype(BF16), preferred_element_type=F32)
    pb = jnp.dot(y_ref[...], wb_ref[...].astype(BF16), preferred_element_type=F32)
    ga = jax.nn.sigmoid(ga_ref[...].astype(F32))
    gb = jax.nn.sigmoid(gb_ref[...].astype(F32))
    o_ref[...] = (ga * pa + gb * pb).astype(BF16)


def _merge(layer, attn, y, wa, wb, gates, tm=1024, tn=512):
    n = attn.shape[0]
    nj = D_MODEL // tn
    return pl.pallas_call(
        _merge_kernel,
        grid=(n // tm, nj),
        in_specs=[
            pl.BlockSpec((tm, ATTN_WIDTH), lambda i, j: (i, 0)),
            pl.BlockSpec((tm, RWKV_WIDTH), lambda i, j: (i, 0)),
            pl.BlockSpec((None, ATTN_WIDTH, tn), lambda i, j: (layer, 0, j)),
            pl.BlockSpec((None, RWKV_WIDTH, tn), lambda i, j: (layer, 0, j)),
            pl.BlockSpec((tm, tn), lambda i, j: (i, j)),
            pl.BlockSpec((tm, tn), lambda i, j: (i, j + nj)),
        ],
        out_specs=pl.BlockSpec((tm, tn), lambda i, j: (i, j)),
        out_shape=jax.ShapeDtypeStruct((n, D_MODEL), BF16),
        compiler_params=_cparams("parallel", "arbitrary"),
        name="gated_merge",
    )(attn, y, wa, wb, gates, gates)


def _outproj_router_kernel(m_ref, wo_ref, x_ref, nw_ref, wr_ref, br_ref,
                           xo_ref, hf_ref, slab_ref, cnt_ref, run_scr):
    @pl.when(pl.program_id(0) == 0)
    def _():
        run_scr[...] = jnp.zeros_like(run_scr)

    xn = x_ref[...] + jnp.dot(m_ref[...], wo_ref[...], preferred_element_type=F32)
    xo_ref[...] = xn
    hf = _rms(xn, nw_ref[...])
    _store_rows(hf_ref, hf)

    h_hi = hf.astype(BF16)
    h_lo = (hf - h_hi.astype(F32)).astype(BF16)
    wr = wr_ref[...]
    w_hi = wr.astype(BF16)
    w_lo = (wr - w_hi.astype(F32)).astype(BF16)
    lg = (jnp.dot(h_hi, w_hi, preferred_element_type=F32)
          + jnp.dot(h_hi, w_lo, preferred_element_type=F32)
          + jnp.dot(h_lo, w_hi, preferred_element_type=F32)) + br_ref[...]

    tm = lg.shape[0]
    lane = lax.broadcasted_iota(jnp.int32, lg.shape, 1).astype(F32)
    neg = -jnp.inf
    big = float(ROUTER_LANES)

    def first_argmax(vals):
        mx = jnp.max(vals, axis=-1, keepdims=True)
        idx = jnp.min(jnp.where(vals == mx, lane, big), axis=-1, keepdims=True)
        return mx, idx

    is_group = lane < N_GROUPS
    gmax, gsel = first_argmax(jnp.where(is_group, lg, neg))
    gp = 1.0 / jnp.sum(jnp.where(is_group, jnp.exp(lg - gmax), 0.0), axis=-1, keepdims=True)
    lo_lane = EXPERT_LANE0 + EXPERTS_PER_GROUP * gsel
    in_group = (lane >= lo_lane) & (lane < lo_lane + EXPERTS_PER_GROUP)
    el = jnp.where(in_group, lg, neg)
    v1, i1 = first_argmax(el)
    v2, i2 = first_argmax(jnp.where(lane == i1, neg, el))
    e21 = jnp.exp(v2 - v1)
    ew1 = gp / (1.0 + e21)
    ew2 = gp * e21 / (1.0 + e21)

    oh1 = lane == i1
    oh2 = lane == i2
    cnt = oh1.astype(F32) + oh2.astype(F32)
    r_i = lax.broadcasted_iota(jnp.int32, (tm, tm), 0)
    c_i = lax.broadcasted_iota(jnp.int32, (tm, tm), 1)
    before = jnp.dot((r_i > c_i).astype(BF16), cnt.astype(BF16), preferred_element_type=F32)
    tot = before + run_scr[0:1, :]
    rank1 = jnp.sum(jnp.where(oh1, tot, 0.0), axis=-1, keepdims=True)
    rank2 = jnp.sum(jnp.where(oh2, tot, 0.0), axis=-1, keepdims=True)
    run = run_scr[0:1, :] + jnp.sum(cnt, axis=0, keepdims=True)
    run_scr[...] = jnp.broadcast_to(run, run_scr.shape)
    cnt_ref[...] = jnp.broadcast_to(run, cnt_ref.shape)

    slab = jnp.where(lane == 0, i1 - EXPERT_LANE0,
           jnp.where(lane == 1, i2 - EXPERT_LANE0,
           jnp.where(lane == 2, ew1,
           jnp.where(lane == 3, ew2,
           jnp.where(lane == 4, rank1,
           jnp.where(lane == 5, rank2, 0.0))))))
    slab_ref[...] = slab


def _outproj_router(merged, w_out, x, norm_w, w_router, b_router, tm=256):
    n = x.shape[0]
    row = lambda i: (i, 0)
    const = lambda i: (0, 0)
    return pl.pallas_call(
        _outproj_router_kernel,
        grid=(n // tm,),
        in_specs=[
            pl.BlockSpec((tm, D_MODEL), row),
            pl.BlockSpec((D_MODEL, D_MODEL), const),
            pl.BlockSpec((tm, D_MODEL), row),
            pl.BlockSpec((1, D_MODEL), const),
            pl.BlockSpec((D_MODEL, ROUTER_LANES), const),
            pl.BlockSpec((1, ROUTER_LANES), const),
        ],
        out_specs=[
            pl.BlockSpec((tm, D_MODEL), row),
            pl.BlockSpec((tm * ROW_SUB, LANES), row),
            pl.BlockSpec((tm, ROUTER_LANES), row),
            pl.BlockSpec((SUBLANES, ROUTER_LANES), const),
        ],
        out_shape=[
            jax.ShapeDtypeStruct((n, D_MODEL), F32),
            jax.ShapeDtypeStruct((n * ROW_SUB, LANES), F32),
            jax.ShapeDtypeStruct((n, ROUTER_LANES), F32),
            jax.ShapeDtypeStruct((SUBLANES, ROUTER_LANES), F32),
        ],
        scratch_shapes=[pltpu.VMEM((SUBLANES, ROUTER_LANES), F32)],
        compiler_params=_cparams("arbitrary"),
        name="outproj_router",
    )(merged, w_out, x, norm_w, w_router, b_router)


def _dispatch_kernel(slot_ref, hf_ref, xb_in_hbm, xb_hbm, sem, *, tm):
    del xb_in_hbm

    def copy(u, j):
        return pltpu.make_async_copy(_slab(hf_ref, u), _slab(xb_hbm, slot_ref[TOP_K * u + j]), sem)

    def start(u, c):
        for j in range(TOP_K):
            copy(u, j).start(priority=j)
        return c

    def wait(u, c):
        for j in range(TOP_K):
            copy(u, j).wait()
        return c

    lax.fori_loop(0, tm, start, 0, unroll=8)
    lax.fori_loop(0, tm, wait, 0, unroll=8)


def _dispatch(slot_flat, hf, cap, tm=256):
    n = hf.shape[0] // ROW_SUB
    xb0 = jnp.zeros((cap * ROW_SUB, LANES), F32)
    return pl.pallas_call(
        functools.partial(_dispatch_kernel, tm=tm),
        grid=(n // tm,),
        in_specs=[
            pl.BlockSpec((TOP_K * tm,), lambda i: (i,), memory_space=pltpu.SMEM),
            pl.BlockSpec((tm * ROW_SUB, LANES), lambda i: (i, 0)),
            pl.BlockSpec(memory_space=pl.ANY),
        ],
        out_specs=pl.BlockSpec(memory_space=pl.ANY),
        out_shape=jax.ShapeDtypeStruct((cap * ROW_SUB, LANES), F32),
        scratch_shapes=[pltpu.SemaphoreType.DMA(())],
        input_output_aliases={2: 0},
        compiler_params=_cparams("arbitrary"),
        name="moe_dispatch",
    )(slot_flat, hf, xb0)


def _ffn_kernel(be_ref, nu_ref, par_ref, nxt_ref, x_ref, wg_hbm, wu_hbm, wd_hbm, o_ref,
                wg_f, wu_f, wd_f, wg_s, wu_s, wd_s, sems, *, layer):
    b = pl.program_id(0)
    used = b < nu_ref[0]
    prev_e = be_ref[jnp.maximum(b - 1, 0)]
    new_expert = jnp.logical_or(b == 0, be_ref[b] != prev_e)

    def fetch(e, p):
        return [pltpu.make_async_copy(src.at[layer, e], dst.at[p], sems.at[p, i])
                for i, (src, dst) in enumerate(((wg_hbm, wg_f), (wu_hbm, wu_f), (wd_hbm, wd_f)))]

    @pl.when(jnp.logical_and(used, new_expert))
    def _():
        for p in range(2):
            @pl.when(par_ref[b] == p)
            def _():
                @pl.when(b == 0)
                def _():
                    for cp in fetch(be_ref[0], p):
                        cp.start()

                @pl.when(nxt_ref[b] >= 0)
                def _():
                    for cp in fetch(nxt_ref[b], 1 - p):
                        cp.start()

                for cp in fetch(be_ref[b], p):
                    cp.wait()
                wg_s[...] = wg_f[p].astype(BF16)
                wu_s[...] = wu_f[p].astype(BF16)
                wd_s[...] = wd_f[p].astype(BF16)

    @pl.when(used)
    def _():
        x = _rows_to_2d(x_ref, MOE_ROWS).astype(BF16)
        gt = jnp.dot(x, wg_s[...], preferred_element_type=F32)
        up = jnp.dot(x, wu_s[...], preferred_element_type=F32)
        hid = (gt * jax.nn.sigmoid(gt) * up).astype(BF16)
        _store_rows(o_ref, jnp.dot(hid, wd_s[...], preferred_element_type=F32))

    @pl.when(jnp.logical_not(used))
    def _():
        o_ref[...] = jnp.zeros_like(o_ref)


def _expert_ffn(layer, plan, xb, w_gate, w_up, w_down):
    nblk = xb.shape[0] // (MOE_ROWS * ROW_SUB)
    hbm = pl.BlockSpec(memory_space=pl.ANY)
    grid_spec = pltpu.PrefetchScalarGridSpec(
        num_scalar_prefetch=4,
        grid=(nblk,),
        in_specs=[
            pl.BlockSpec((MOE_ROWS * ROW_SUB, LANES), lambda b, be, nu, par, nxt: (jnp.minimum(b, nu[0] - 1), 0)),
            hbm, hbm, hbm,
        ],
        out_specs=pl.BlockSpec((MOE_ROWS * ROW_SUB, LANES), lambda b, be, nu, par, nxt: (b, 0)),
        scratch_shapes=[
            pltpu.VMEM((2, D_MODEL, EXPERT_HIDDEN), F32),
            pltpu.VMEM((2, D_MODEL, EXPERT_HIDDEN), F32),
            pltpu.VMEM((2, EXPERT_HIDDEN, D_MODEL), F32),
            pltpu.VMEM((D_MODEL, EXPERT_HIDDEN), BF16),
            pltpu.VMEM((D_MODEL, EXPERT_HIDDEN), BF16),
            pltpu.VMEM((EXPERT_HIDDEN, D_MODEL), BF16),
            pltpu.SemaphoreType.DMA((2, 3)),
        ],
    )
    return pl.pallas_call(
        functools.partial(_ffn_kernel, layer=layer),
        grid_spec=grid_spec,
        out_shape=jax.ShapeDtypeStruct(xb.shape, F32),
        compiler_params=_cparams("arbitrary"),
        name="expert_ffn",
    )(plan.block_expert, plan.n_used, plan.parity, plan.next_expert, xb, w_gate, w_up, w_down)


def _combine_kernel(slot_ref, slot_next_ref, x_ref, slab_ref, fw_ref, yb_hbm, o_ref,
                    buf00, buf01, buf10, buf11, sems, *, tm, n_tiles, final_norm):
    i = pl.program_id(0)
    bufs = ((buf00, buf01), (buf10, buf11))

    def copies(slots, p, u):
        return [pltpu.make_async_copy(_slab(yb_hbm, slots[TOP_K * u + j]), _slab(bufs[p][j], u), sems.at[p])
                for j in range(TOP_K)]

    def start_all(slots, p):
        def body(u, c):
            for j, cp in enumerate(copies(slots, p, u)):
                cp.start(priority=j)
            return c
        lax.fori_loop(0, tm, body, 0, unroll=8)

    def wait_all(p):
        def body(u, c):
            for cp in copies(slot_ref, p, u):
                cp.wait()
            return c
        lax.fori_loop(0, tm, body, 0, unroll=8)

    @pl.when(i == 0)
    def _():
        start_all(slot_ref, 0)

    for p in range(2):
        @pl.when(i % 2 == p)
        def _():
            @pl.when(i + 1 < n_tiles)
            def _():
                start_all(slot_next_ref, 1 - p)

            wait_all(p)
            slab = slab_ref[...]
            w1 = slab[:, 2:3]
            w2 = slab[:, 3:4]
            out = x_ref[...] + (_rows_to_2d(bufs[p][0], tm) * w1 + _rows_to_2d(bufs[p][1], tm) * w2)
            if final_norm:
                out = _rms(out, fw_ref[...])
            o_ref[...] = out


def _combine(slot_flat, x, slab, final_w, yb, final_norm, tm=128):
    n = x.shape[0]
    n_tiles = n // tm
    row = lambda i: (i, 0)
    row_buf = pltpu.VMEM((tm * ROW_SUB, LANES), F32)
    return pl.pallas_call(
        functools.partial(_combine_kernel, tm=tm, n_tiles=n_tiles, final_norm=final_norm),
        grid=(n_tiles,),
        in_specs=[
            pl.BlockSpec((TOP_K * tm,), lambda i: (i,), memory_space=pltpu.SMEM),
            pl.BlockSpec((TOP_K * tm,), lambda i: (jnp.minimum(i + 1, n_tiles - 1),), memory_space=pltpu.SMEM),
            pl.BlockSpec((tm, D_MODEL), row),
            pl.BlockSpec((tm, ROUTER_LANES), row),
            pl.BlockSpec((1, D_MODEL), lambda i: (0, 0)),
            pl.BlockSpec(memory_space=pl.ANY),
        ],
        out_specs=pl.BlockSpec((tm, D_MODEL), row),
        out_shape=jax.ShapeDtypeStruct((n, D_MODEL), F32),
        scratch_shapes=[row_buf, row_buf, row_buf, row_buf, pltpu.SemaphoreType.DMA((2,))],
        compiler_params=_cparams("arbitrary"),
        name="moe_combine",
    )(slot_flat, slot_flat, x, slab, final_w, yb)


def _pad_rows(w, rows, offset=0):
    out = jnp.zeros((rows, w.shape[1]), BF16)
    return out.at[offset:offset + w.shape[0]].set(w.astype(BF16))


def _pad_cols(vec, cols, offset=0):
    out = jnp.zeros((cols,), F32)
    return out.at[offset:offset + vec.shape[0]].set(vec.astype(F32))


class MoePlan(NamedTuple):
    slot: jax.Array
    block_expert: jax.Array
    n_used: jax.Array
    parity: jax.Array
    next_expert: jax.Array
    cap: int


def _moe_plan(slab, counts_row, n_tokens):
    eid = slab[:, 0:TOP_K].astype(jnp.int32)
    rank = slab[:, 4:4 + TOP_K].astype(jnp.int32)
    counts = counts_row[EXPERT_LANE0:EXPERT_LANE0 + N_EXPERTS].astype(jnp.int32)
    padded = (counts + MOE_ROWS - 1) // MOE_ROWS * MOE_ROWS
    pad_ends = jnp.cumsum(padded)
    pad_starts = pad_ends - padded
    slot = pad_starts[eid.reshape(-1)] + rank.reshape(-1)
    nblk = n_tokens * TOP_K // MOE_ROWS + N_EXPERTS
    block_start = jnp.arange(nblk, dtype=jnp.int32) * MOE_ROWS
    block_expert = jnp.minimum(
        jnp.sum((pad_ends[None, :] <= block_start[:, None]).astype(jnp.int32), axis=1),
        N_EXPERTS - 1).astype(jnp.int32)
    n_used = (pad_ends[-1:] // MOE_ROWS).astype(jnp.int32)
    blk = jnp.arange(nblk, dtype=jnp.int32)
    prev_expert = jnp.concatenate([block_expert[:1], block_expert[:-1]])
    first = (blk < n_used[0]) & ((blk == 0) | (block_expert != prev_expert))
    parity = ((jnp.cumsum(first.astype(jnp.int32)) - 1) % 2).astype(jnp.int32)
    later_first = first[None, :] & (blk[None, :] > blk[:, None])
    next_first = jnp.min(jnp.where(later_first, blk[None, :], nblk), axis=1)
    next_expert = jnp.where(next_first < nblk, block_expert[jnp.minimum(next_first, nblk - 1)], -1)
    return MoePlan(slot, block_expert, n_used, parity, next_expert.astype(jnp.int32), nblk * MOE_ROWS)


def kernel(x, attn_norm_w, w_in, shift_mix, attn_sinks, rwkv_w0, rwkv_w_up, rwkv_a0, rwkv_a_up, rwkv_g_up, rwkv_k_k, rwkv_k_a, rwkv_r_k, rwkv_ln_w, rwkv_ln_b, vres_down, vres_mix, vres_up, vres_v0, w_branch_a, w_branch_b, w_out, ffn_norm_w, router_group_w, router_group_b, router_expert_w, router_expert_b, expert_w_gate, expert_w_up, expert_w_down, final_norm_w):
    batch, seq, _ = x.shape
    n = batch * seq
    depth = w_in.shape[0]
    xf = x.reshape(n, D_MODEL)
    w_in_bf = w_in.astype(BF16)
    v_first = None
    for i in range(depth):
        has_vres = i > 0
        w_gates = w_in_bf[i, :, GATE_COL0:]
        mix_r = shift_mix[i, :RKV_COLS].reshape(1, RKV_COLS)
        mix_l = _pad_cols(shift_mix[i, RKV_COLS:], LORA_COLS).reshape(1, LORA_COLS)
        v0 = vres_v0[i - 1] if has_vres else jnp.zeros((RWKV_WIDTH,), F32)
        zero = jnp.zeros((RWKV_WIDTH,), F32)
        pv_pre = jnp.stack([rwkv_w0[i], rwkv_a0[i], rwkv_k_k[i], rwkv_k_a[i], v0, zero, zero, zero]).astype(F32)
        pv_post = jnp.stack([rwkv_ln_w[i], rwkv_ln_b[i], rwkv_r_k[i].reshape(-1),
                             zero, zero, zero, zero, zero]).astype(F32)
        wup = _pad_rows(rwkv_w_up[i], LANES, 0)
        aup = _pad_rows(rwkv_a_up[i], LANES, DECAY_LORA)
        gup = _pad_rows(rwkv_g_up[i], 2 * LANES, 0)
        if has_vres:
            vd = _pad_rows(vres_down[i - 1].T, LANES, 0).T
            mix_v = _pad_cols(vres_mix[i - 1], LANES).reshape(1, LANES)
            vup = _pad_rows(vres_up[i - 1], LANES, 0)
        else:
            vd, mix_v = None, None
            vup = jnp.zeros((LANES, RWKV_WIDTH), BF16)
        w_router = jnp.concatenate(
            [router_group_w[i], router_expert_w[i],
             jnp.zeros((D_MODEL, ROUTER_LANES - N_GROUPS - N_EXPERTS), F32)], axis=1)
        b_router = _pad_cols(jnp.concatenate([router_group_b[i], router_expert_b[i]]),
                             ROUTER_LANES).reshape(1, ROUTER_LANES)

        proj = _inproj(i, xf, attn_norm_w[i].reshape(1, D_MODEL), w_in_bf, w_gates, vd)
        qkv, rkv, lora, gates = proj[:4]
        zv = proj[4] if has_vres else None
        attn = _attention(qkv, attn_sinks[i], seq)
        r, lw, k2, v, al, be, g = _rwkv_pre(rkv, lora, v_first, zv, mix_r, mix_l, mix_v, pv_pre,
                                            wup, aup, gup, vup, seq)
        if not has_vres:
            v_first = v
        y_raw = _rwkv_scan(r, lw, k2, v, al, be, seq)
        y = _rwkv_post(y_raw, r, k2, v, g, pv_post)
        merged = _merge(i, attn, y, w_branch_a, w_branch_b, gates)

        x_mid, hf, slab, counts = _outproj_router(
            merged, w_out[i].astype(BF16), xf, ffn_norm_w[i].reshape(1, D_MODEL), w_router, b_router)
        plan = _moe_plan(slab, counts[0], n)
        xb = _dispatch(plan.slot, hf, plan.cap)
        yb = _expert_ffn(i, plan, xb, expert_w_gate, expert_w_up, expert_w_down)
        xf = _combine(plan.slot, x_mid, slab, final_norm_w.reshape(1, D_MODEL), yb, final_norm=(i == depth - 1))
    return xf.reshape(batch, seq, D_MODEL)
```

```python
import functools
from typing import NamedTuple

import jax
import jax.numpy as jnp
from jax import lax
from jax.experimental import pallas as pl
from jax.experimental.pallas import tpu as pltpu

F32 = jnp.float32
BF16 = jnp.bfloat16

D_MODEL = 2048
HEAD_DIM = 64
N_Q_HEADS = 16
N_KV_HEADS = 4
GQA_GROUP = N_Q_HEADS // N_KV_HEADS
ATTN_WIDTH = N_Q_HEADS * HEAD_DIM
KV_WIDTH = N_KV_HEADS * HEAD_DIM
WINDOW = 128
ATTN_BLOCK = 128
ATTN_SUB = 2

RWKV_HEAD = 64
N_RWKV_HEADS = 16
RWKV_WIDTH = N_RWKV_HEADS * RWKV_HEAD
DECAY_LORA = 64
AAA_LORA = 64
MV_LORA = 32
GATE_LORA = 160
RWKV_GN_EPS = 64e-5
LORA_WIDTH = DECAY_LORA + AAA_LORA + GATE_LORA

N_GROUPS = 4
EXPERTS_PER_GROUP = 8
N_EXPERTS = N_GROUPS * EXPERTS_PER_GROUP
TOP_K = 2
EXPERT_HIDDEN = D_MODEL // 4
NORM_EPS = 1e-5

LANES = 128
SUBLANES = 8
MXU_COLS = 256

SEG = 512
QKV_COLS = ATTN_WIDTH + 2 * KV_WIDTH
RKV_COLS = 3 * RWKV_WIDTH
LORA_COLS = SEG
GATE_COLS = 2 * D_MODEL
QKV_TILES = QKV_COLS // SEG
RKV_TILES = RKV_COLS // SEG
LORA_TILES = LORA_COLS // SEG
GATE_TILES = GATE_COLS // SEG
IN_TILES = QKV_TILES + RKV_TILES + LORA_TILES + GATE_TILES
FIRST_GATE_TILE = QKV_TILES + RKV_TILES + LORA_TILES
GATE_COL0 = QKV_COLS + RKV_COLS + LORA_WIDTH

SCAN_CHUNK = 64
SCAN_GROUP = 4
MOE_ROWS = 256
ROUTER_LANES = LANES
EXPERT_LANE0 = N_GROUPS

ROW_SUB = D_MODEL // LANES

VMEM_LIMIT = 56 * 1024 * 1024


def _rows_to_2d(slab_ref, rows):
    return jnp.concatenate([slab_ref[pl.ds(s, rows, stride=ROW_SUB), :] for s in range(ROW_SUB)], axis=1)


def _store_rows(slab_ref, val):
    rows = val.shape[0]
    for s in range(ROW_SUB):
        slab_ref[pl.ds(s, rows, stride=ROW_SUB), :] = val[:, s * LANES:(s + 1) * LANES]


def _slab(ref, row):
    return ref.at[pl.ds(pl.multiple_of(row * ROW_SUB, ROW_SUB), ROW_SUB)]


def _cparams(*sem):
    return pltpu.CompilerParams(dimension_semantics=sem, vmem_limit_bytes=VMEM_LIMIT)


def _bdot(a, b):
    return jnp.dot(a.astype(BF16), b.astype(BF16), preferred_element_type=F32)


def _bdot_nt(a, b):
    return lax.dot_general(a.astype(BF16), b.astype(BF16), (((1,), (1,)), ((), ())),
                           preferred_element_type=F32)


def _bdot_tn(a, b):
    return lax.dot_general(a.astype(BF16), b.astype(BF16), (((0,), (0,)), ((), ())),
                           preferred_element_type=F32)


def _dot_split_rhs(w01, x, passes):
    acc, rem = None, x
    for p in range(passes):
        part = rem.astype(BF16)
        d = jnp.dot(w01, part, preferred_element_type=F32)
        acc = d if acc is None else acc + d
        if p + 1 < passes:
            rem = rem - part.astype(F32)
    return acc


def _dot_split_lhs(x, w01, passes):
    acc, rem = None, x
    for p in range(passes):
        part = rem.astype(BF16)
        d = jnp.dot(part, w01, preferred_element_type=F32)
        acc = d if acc is None else acc + d
        if p + 1 < passes:
            rem = rem - part.astype(F32)
    return acc


def _rms(x, w):
    ms = jnp.mean(x * x, axis=-1, keepdims=True)
    return x * lax.rsqrt(ms + NORM_EPS) * w


def _inproj_kernel(*refs, has_vres):
    refs = list(refs)
    x_ref, nw_ref, w_ref, wg_ref = refs[:4]
    refs = refs[4:]
    vd_ref = refs.pop(0) if has_vres else None
    qkv_ref, rkv_ref, lora_ref, gate_ref = refs[:4]
    refs = refs[4:]
    zv_ref = refs.pop(0) if has_vres else None
    (h_scr,) = refs
    j = pl.program_id(1)

    @pl.when(j == 0)
    def _():
        h_scr[...] = _rms(x_ref[...], nw_ref[...]).astype(BF16)
        if has_vres:
            zv_ref[...] = jnp.dot(h_scr[...], vd_ref[...], preferred_element_type=F32)

    def project(wt_ref, out_ref):
        for c in range(SEG // MXU_COLS):
            cs = slice(c * MXU_COLS, (c + 1) * MXU_COLS)
            acc = jnp.dot(h_scr[...], wt_ref[:, cs], preferred_element_type=F32)
            out_ref[:, cs] = acc.astype(out_ref.dtype)

    @pl.when(j < QKV_TILES)
    def _():
        project(w_ref, qkv_ref)

    @pl.when((j >= QKV_TILES) & (j < QKV_TILES + RKV_TILES))
    def _():
        project(w_ref, rkv_ref)

    @pl.when(j == QKV_TILES + RKV_TILES)
    def _():
        project(w_ref, lora_ref)

    @pl.when(j >= FIRST_GATE_TILE)
    def _():
        project(wg_ref, gate_ref)


def _inproj(layer, x, norm_w, w_in, w_gates, vd, tm=1024):
    n = x.shape[0]
    has_vres = vd is not None
    s_rkv = QKV_TILES
    in_specs = [
        pl.BlockSpec((tm, D_MODEL), lambda i, j: (i, 0)),
        pl.BlockSpec((1, D_MODEL), lambda i, j: (0, 0)),
        pl.BlockSpec((None, D_MODEL, SEG), lambda i, j: (layer, 0, jnp.minimum(j, FIRST_GATE_TILE - 1))),
        pl.BlockSpec((D_MODEL, SEG), lambda i, j: (0, jnp.clip(j - FIRST_GATE_TILE, 0, GATE_TILES - 1))),
    ]
    args = [x, norm_w, w_in, w_gates]
    out_specs = [
        pl.BlockSpec((tm, SEG), lambda i, j: (i, jnp.clip(j, 0, QKV_TILES - 1))),
        pl.BlockSpec((tm, SEG), lambda i, j: (i, jnp.clip(j - s_rkv, 0, RKV_TILES - 1))),
        pl.BlockSpec((tm, SEG), lambda i, j: (i, 0)),
        pl.BlockSpec((tm, SEG), lambda i, j: (i, jnp.clip(j - FIRST_GATE_TILE, 0, GATE_TILES - 1))),
    ]
    out_shape = [
        jax.ShapeDtypeStruct((n, QKV_COLS), BF16),
        jax.ShapeDtypeStruct((n, RKV_COLS), F32),
        jax.ShapeDtypeStruct((n, LORA_COLS), F32),
        jax.ShapeDtypeStruct((n, GATE_COLS), BF16),
    ]
    if has_vres:
        in_specs.append(pl.BlockSpec((D_MODEL, LANES), lambda i, j: (0, 0)))
        args.append(vd)
        out_specs.append(pl.BlockSpec((tm, LANES), lambda i, j: (i, 0)))
        out_shape.append(jax.ShapeDtypeStruct((n, LANES), F32))
    return pl.pallas_call(
        functools.partial(_inproj_kernel, has_vres=has_vres),
        grid=(n // tm, IN_TILES),
        in_specs=in_specs,
        out_specs=out_specs,
        out_shape=out_shape,
        scratch_shapes=[pltpu.VMEM((tm, D_MODEL), BF16)],
        compiler_params=_cparams("parallel", "arbitrary"),
        name="inproj",
    )(*args)


def _attn_kernel(sink_ref, q_ref, kp_ref, kc_ref, vp_ref, vc_ref, bias_ref, o_ref, *, tiles_per_seq):
    first = (pl.program_id(0) % tiles_per_seq) == 0
    col = lax.broadcasted_iota(jnp.int32, (ATTN_BLOCK, 2 * ATTN_BLOCK), 1)
    pad_keys = jnp.logical_and(first, col < ATTN_BLOCK)
    scale = HEAD_DIM ** -0.5
    items = [(sb, hk) for sb in range(ATTN_SUB) for hk in range(N_KV_HEADS)]

    def window(prev_ref, cur_ref, sb, hk):
        ks = slice(hk * HEAD_DIM, (hk + 1) * HEAD_DIM)
        rows = slice(sb * ATTN_BLOCK, (sb + 1) * ATTN_BLOCK)
        prev = prev_ref[:, ks] if sb == 0 else cur_ref[(sb - 1) * ATTN_BLOCK:sb * ATTN_BLOCK, ks]
        return jnp.concatenate([prev, cur_ref[rows, ks]], axis=0)

    scores = {}
    for sb, hk in items:
        kw = window(kp_ref, kc_ref, sb, hk)
        for g in range(GQA_GROUP):
            h = hk * GQA_GROUP + g
            qh = q_ref[sb * ATTN_BLOCK:(sb + 1) * ATTN_BLOCK, h * HEAD_DIM:(h + 1) * HEAD_DIM]
            s = lax.dot_general(qh, kw, (((1,), (1,)), ((), ())), preferred_element_type=F32)
            s = s * scale + bias_ref[h]
            if sb == 0:
                s = jnp.where(pad_keys, -jnp.inf, s)
            scores[(sb, h)] = s
    row_max = {key: jnp.max(s, axis=-1, keepdims=True) for key, s in scores.items()}
    m = {key: jnp.maximum(row_max[key], sink_ref[key[1]]) for key in scores}
    p = {key: jnp.exp(scores[key] - m[key]) for key in scores}
    row_sum = {key: jnp.sum(p[key], axis=-1, keepdims=True) for key in scores}
    denoms = {key: row_sum[key] + jnp.exp(sink_ref[key[1]] - m[key]) for key in scores}
    probs = {key: p[key].astype(BF16) for key in scores}
    for sb, hk in items:
        vw = window(vp_ref, vc_ref, sb, hk)
        for g in range(GQA_GROUP):
            h = hk * GQA_GROUP + g
            o = jnp.dot(probs[(sb, h)], vw, preferred_element_type=F32) / denoms[(sb, h)]
            o_ref[sb * ATTN_BLOCK:(sb + 1) * ATTN_BLOCK, h * HEAD_DIM:(h + 1) * HEAD_DIM] = o.astype(BF16)


def _attn_bias():
    qi = jnp.arange(ATTN_BLOCK)[:, None]
    kj = jnp.arange(2 * ATTN_BLOCK)[None, :]
    dist = qi + ATTN_BLOCK - kj
    valid = (dist >= 0) & (dist < WINDOW)
    slopes = jnp.exp2(-8.0 * jnp.arange(1, N_Q_HEADS + 1, dtype=F32) / N_Q_HEADS)
    bias = -slopes[:, None, None] * dist.astype(F32)[None]
    return jnp.where(valid[None], bias, -jnp.inf)


def _attention(qkv, sinks, seq):
    n = qkv.shape[0]
    rows = ATTN_SUB * ATTN_BLOCK
    tiles_per_seq = seq // rows
    kcol = ATTN_WIDTH // KV_WIDTH
    prev = lambda i: jnp.maximum(i * ATTN_SUB - 1, 0)
    return pl.pallas_call(
        functools.partial(_attn_kernel, tiles_per_seq=tiles_per_seq),
        grid=(n // rows,),
        in_specs=[
            pl.BlockSpec(memory_space=pltpu.SMEM),
            pl.BlockSpec((rows, ATTN_WIDTH), lambda i: (i, 0)),
            pl.BlockSpec((ATTN_BLOCK, KV_WIDTH), lambda i: (prev(i), kcol)),
            pl.BlockSpec((rows, KV_WIDTH), lambda i: (i, kcol)),
            pl.BlockSpec((ATTN_BLOCK, KV_WIDTH), lambda i: (prev(i), kcol + 1)),
            pl.BlockSpec((rows, KV_WIDTH), lambda i: (i, kcol + 1)),
            pl.BlockSpec((N_Q_HEADS, ATTN_BLOCK, 2 * ATTN_BLOCK), lambda i: (0, 0, 0)),
        ],
        out_specs=pl.BlockSpec((rows, ATTN_WIDTH), lambda i: (i, 0)),
        out_shape=jax.ShapeDtypeStruct((n, ATTN_WIDTH), BF16),
        compiler_params=_cparams("parallel"),
        name="swa_attention",
    )(sinks.astype(F32), qkv, qkv, qkv, qkv, qkv, _attn_bias())


def _head_ones():
    ch = jnp.arange(MXU_COLS) // RWKV_HEAD
    return (ch[:, None] == ch[None, :]).astype(BF16)


def _head_sums(x, ones, passes):
    parts = [_dot_split_lhs(x[:, c:c + MXU_COLS], ones, passes) for c in range(0, RWKV_WIDTH, MXU_COLS)]
    return jnp.concatenate(parts, axis=1)


def _rwkv_pre_kernel(*refs, has_vres, tiles_per_seq):
    refs = list(refs)
    rkv_ref, rkvp_ref, lo_ref, lop_ref = refs[:4]
    refs = refs[4:]
    if has_vres:
        vf_ref, zv_ref, zvp_ref, mixv_ref = refs[:4]
        refs = refs[4:]
    (mixr_ref, mixl_ref, pv_ref, wup_ref, aup_ref, gup_ref, vup_ref, ones_ref,
     r_out, lw_out, k_out, v_out, al_out, be_out, g_out) = refs

    first = (pl.program_id(0) % tiles_per_seq) == 0

    def shifted(z, zp_ref):
        prev_row = jnp.where(first, 0.0, zp_ref[SUBLANES - 1:SUBLANES, :])
        row = lax.broadcasted_iota(jnp.int32, z.shape, 0)
        return jnp.where(row == 0, prev_row, pltpu.roll(z, 1, 0))

    z = rkv_ref[...]
    zs = z + (shifted(z, rkvp_ref) - z) * mixr_ref[...]
    lo = lo_ref[...]
    los = lo + (shifted(lo, lop_ref) - lo) * mixl_ref[...]

    r = zs[:, :RWKV_WIDTH]
    kr = zs[:, RWKV_WIDTH:2 * RWKV_WIDTH]
    vr = zs[:, 2 * RWKV_WIDTH:]

    col = lax.broadcasted_iota(jnp.int32, los.shape, 1)
    gate_cols = (col >= DECAY_LORA + AAA_LORA) & (col < LORA_WIDTH)
    act = jnp.where(col < DECAY_LORA, jnp.tanh(los), jnp.where(gate_cols, jax.nn.sigmoid(los), los))
    act = act.astype(BF16)
    wa_in = act[:, :LANES]
    dw = jnp.dot(wa_in, wup_ref[...], preferred_element_type=F32)
    da = jnp.dot(wa_in, aup_ref[...], preferred_element_type=F32)
    g = jnp.dot(act[:, LANES:3 * LANES], gup_ref[...], preferred_element_type=F32)

    w0, a0, k_k, k_a, v0 = (pv_ref[i:i + 1, :] for i in range(5))
    u = -(w0 + dw)
    softplus = jnp.maximum(u, 0.0) + jnp.log1p(jnp.exp(-jnp.abs(u)))
    w_log = -softplus - 0.5
    a = jax.nn.sigmoid(a0 + da)
    if has_vres:
        zv = zv_ref[...]
        zvs = zv + (shifted(zv, zvp_ref) - zv) * mixv_ref[...]
        dv = jnp.dot(zvs.astype(BF16), vup_ref[...], preferred_element_type=F32)
        vr = vr + (vf_ref[...] - vr) * jax.nn.sigmoid(v0 + dv)

    kk0 = kr * k_k
    ss = _head_sums(kk0 * kk0, ones_ref[...], 2)
    kk = kk0 * (1.0 / jnp.maximum(jnp.sqrt(ss), 1e-12))

    r_out[...] = r
    lw_out[...] = -jnp.exp(w_log)
    k_out[...] = kr * (1.0 + (a - 1.0) * k_a)
    v_out[...] = vr
    al_out[...] = -kk
    be_out[...] = kk * a
    g_out[...] = g


def _rwkv_pre(rkv, lora, v_first, zv, mix_r, mix_l, mix_v, pvec, wup, aup, gup, vup, seq, tt=256):
    n = rkv.shape[0]
    has_vres = v_first is not None
    row = lambda i: (i, 0)
    prev8 = lambda i: (jnp.maximum(i * (tt // SUBLANES) - 1, 0), 0)
    const = lambda i: (0, 0)
    in_specs = [
        pl.BlockSpec((tt, RKV_COLS), row),
        pl.BlockSpec((SUBLANES, RKV_COLS), prev8),
        pl.BlockSpec((tt, LORA_COLS), row),
        pl.BlockSpec((SUBLANES, LORA_COLS), prev8),
    ]
    args = [rkv, rkv, lora, lora]
    if has_vres:
        in_specs += [
            pl.BlockSpec((tt, RWKV_WIDTH), row),
            pl.BlockSpec((tt, LANES), row),
            pl.BlockSpec((SUBLANES, LANES), prev8),
            pl.BlockSpec((1, LANES), const),
        ]
        args += [v_first, zv, zv, mix_v]
    in_specs += [
        pl.BlockSpec((1, RKV_COLS), const),
        pl.BlockSpec((1, LORA_COLS), const),
        pl.BlockSpec((SUBLANES, RWKV_WIDTH), const),
        pl.BlockSpec((LANES, RWKV_WIDTH), const),
        pl.BlockSpec((LANES, RWKV_WIDTH), const),
        pl.BlockSpec((2 * LANES, RWKV_WIDTH), const),
        pl.BlockSpec((LANES, RWKV_WIDTH), const),
        pl.BlockSpec((MXU_COLS, MXU_COLS), const),
    ]
    args += [mix_r, mix_l, pvec, wup, aup, gup, vup, _head_ones()]
    out = jax.ShapeDtypeStruct((n, RWKV_WIDTH), F32)
    return pl.pallas_call(
        functools.partial(_rwkv_pre_kernel, has_vres=has_vres, tiles_per_seq=seq // tt),
        grid=(n // tt,),
        in_specs=in_specs,
        out_specs=[pl.BlockSpec((tt, RWKV_WIDTH), row)] * 7,
        out_shape=[out] * 7,
        compiler_params=_cparams("parallel"),
        name="rwkv_pre",
    )(*args)


def _scan_kernel(r_ref, lw_ref, k_ref, v_ref, al_ref, be_ref, y_ref, s_scr):
    L = SCAN_CHUNK
    W = SCAN_GROUP * RWKV_HEAD
    n_seq = r_ref.shape[0]
    n_grp = N_RWKV_HEADS // SCAN_GROUP

    @pl.when(pl.program_id(0) == 0)
    def _():
        s_scr[...] = jnp.zeros_like(s_scr)

    row = lax.broadcasted_iota(jnp.int32, (L, L), 0)
    col = lax.broadcasted_iota(jnp.int32, (L, L), 1)
    tri = (row >= col).astype(BF16)

    t_w = lax.broadcasted_iota(jnp.int32, (L, W), 0)
    lane_w = lax.broadcasted_iota(jnp.int32, (L, W), 1)
    s_w = lane_w % RWKV_HEAD
    strict = t_w > s_w
    incl = t_w >= s_w
    eye = (t_w == s_w).astype(F32)
    blk_w = lane_w // RWKV_HEAD
    bd_mask = (lax.broadcasted_iota(jnp.int32, (W, W), 0) // RWKV_HEAD
               == lax.broadcasted_iota(jnp.int32, (W, W), 1) // RWKV_HEAD)

    def blockdiag(w):
        tiled = jnp.concatenate([w.astype(BF16)] * SCAN_GROUP, axis=0)
        return jnp.where(bd_mask, tiled, jnp.zeros_like(tiled))

    items = [(b, g) for b in range(n_seq) for g in range(n_grp)]
    ops = {}
    for b in range(n_seq):
        lw = lw_ref[b]
        cum = _dot_split_rhs(tri, lw, 3)
        p_inc = jnp.exp(cum)
        p_exc = jnp.exp(cum - lw)
        p_inv = jnp.exp(-cum)
        p_last = p_inc[L - 1:L, :]
        a_t = al_ref[b] * p_exc
        r_t = r_ref[b] * p_inc
        b_t = be_ref[b] * p_inv
        k_t = k_ref[b] * p_inv
        b_end = b_t * p_last
        k_end = k_t * p_last
        v = v_ref[b]
        for g in range(n_grp):
            gs = slice(g * W, (g + 1) * W)
            ops[(b, g)] = dict(
                ar=jnp.concatenate([a_t[:, gs], r_t[:, gs]], axis=0),
                bt=b_t[:, gs], kt=k_t[:, gs], v=v[:, gs], p_last=p_last[:, gs],
                bke=jnp.concatenate([b_end[:, gs], k_end[:, gs]], axis=0))

    gm, xs, av, s0 = {}, {}, {}, {}
    for it in items:
        o = ops[it]
        rhs = jnp.concatenate([blockdiag(o['bt']), blockdiag(o['kt'])], axis=0)
        gm[it] = _bdot_nt(o['ar'], rhs)
        s0[it] = s_scr[it[0], it[1]]
        xs[it] = _bdot_nt(o['ar'], blockdiag(s0[it]))
    a_ab, a_rb = {}, {}
    for it in items:
        g = gm[it]
        a_ab[it] = jnp.where(strict, g[:L, :W], 0.0)
        a_rb[it] = jnp.where(incl, g[L:, :W], 0.0)
        a_kk = jnp.concatenate([jnp.where(strict, g[:L, W:], 0.0), jnp.where(incl, g[L:, W:], 0.0)], axis=0)
        av[it] = _bdot(a_kk, blockdiag(ops[it]['v']))

    pw = {it: _bdot(a_ab[it], blockdiag(a_ab[it])) for it in items}
    tinv = {it: eye + a_ab[it] for it in items}
    span = 2
    while 2 * span < L:
        both = {it: _bdot(jnp.concatenate([pw[it], tinv[it]], axis=0), blockdiag(pw[it])) for it in items}
        tinv = {it: tinv[it] + both[it][L:] for it in items}
        pw = {it: both[it][:L] for it in items}
        span *= 2
    tinv = {it: tinv[it] + _bdot(tinv[it], blockdiag(pw[it])) for it in items}

    u = {it: _bdot(tinv[it], blockdiag(xs[it][:L] + av[it][:L])) for it in items}
    for it in items:
        b, g = it
        y_ref[b, :, g * W:(g + 1) * W] = xs[it][L:] + av[it][L:] + _bdot(a_rb[it], blockdiag(u[it]))
    for it in items:
        o = ops[it]
        full = _bdot_tn(jnp.concatenate([u[it], o['v']], axis=0), o['bke'])
        upd = full[:RWKV_HEAD]
        for h in range(1, SCAN_GROUP):
            upd = jnp.where(blk_w == h, full[h * RWKV_HEAD:(h + 1) * RWKV_HEAD], upd)
        s_scr[it[0], it[1]] = s0[it] * o['p_last'] + upd


def _rwkv_scan(r, lw, k, v, al, be, seq):
    n = r.shape[0]
    n_seq = n // seq
    nc = seq // SCAN_CHUNK
    spec = pl.BlockSpec((n_seq, SCAN_CHUNK, RWKV_WIDTH), lambda c: (0, c, 0))
    args = [a.reshape(n_seq, seq, RWKV_WIDTH) for a in (r, lw, k, v, al, be)]
    y = pl.pallas_call(
        _scan_kernel,
        grid=(nc,),
        in_specs=[spec] * 6,
        out_specs=spec,
        out_shape=jax.ShapeDtypeStruct((n_seq, seq, RWKV_WIDTH), F32),
        scratch_shapes=[pltpu.VMEM((n_seq, N_RWKV_HEADS // SCAN_GROUP, RWKV_HEAD, SCAN_GROUP * RWKV_HEAD), F32)],
        compiler_params=_cparams("arbitrary"),
        name="rwkv_scan",
    )(*args)
    return y.reshape(n, RWKV_WIDTH)


def _rwkv_post_kernel(y_ref, r_ref, k_ref, v_ref, g_ref, pv_ref, ones_ref, o_ref):
    ones = ones_ref[...]
    ln_w, ln_b, r_k = (pv_ref[i:i + 1, :] for i in range(3))
    y = y_ref[...]
    inv_n = 1.0 / RWKV_HEAD
    d = y - _head_sums(y, ones, 2) * inv_n
    var = _head_sums(d * d, ones, 2) * inv_n
    yn = d * lax.rsqrt(var + RWKV_GN_EPS) * ln_w + ln_b
    bonus = _head_sums(r_ref[...] * k_ref[...] * r_k, ones, 2) * v_ref[...]
    o_ref[...] = ((yn + bonus) * g_ref[...]).astype(BF16)


def _rwkv_post(y, r, k, v, g, pvec, tt=512):
    n = y.shape[0]
    row = lambda i: (i, 0)
    const = lambda i: (0, 0)
    return pl.pallas_call(
        _rwkv_post_kernel,
        grid=(n // tt,),
        in_specs=[pl.BlockSpec((tt, RWKV_WIDTH), row)] * 5 + [
            pl.BlockSpec((SUBLANES, RWKV_WIDTH), const),
            pl.BlockSpec((MXU_COLS, MXU_COLS), const),
        ],
        out_specs=pl.BlockSpec((tt, RWKV_WIDTH), row),
        out_shape=jax.ShapeDtypeStruct((n, RWKV_WIDTH), BF16),
        compiler_params=_cparams("parallel"),
        name="rwkv_post",
    )(y, r, k, v, g, pvec, _head_ones())


def _merge_kernel(a_ref, y_ref, wa_ref, wb_ref, ga_ref, gb_ref, o_ref):
    for c in range(o_ref.shape[1] // MXU_COLS):
        cs = slice(c * MXU_COLS, (c + 1) * MXU_COLS)
        pa = jnp.dot(a_ref[...], wa_ref[:, cs].astype(BF16), preferred_element_type=F32)
        pb = jnp.dot(y_ref[...], wb_ref[:, cs].astype(BF16), preferred_element_type=F32)
        ga = jax.nn.sigmoid(ga_ref[:, cs].astype(F32))
        gb = jax.nn.sigmoid(gb_ref[:, cs].astype(F32))
        o_ref[:, cs] = (ga * pa + gb * pb).astype(BF16)


def _merge(layer, attn, y, wa, wb, gates, tm=1024, tn=1024):
    n = attn.shape[0]
    nj = D_MODEL // tn
    return pl.pallas_call(
        _merge_kernel,
        grid=(n // tm, nj),
        in_specs=[
            pl.BlockSpec((tm, ATTN_WIDTH), lambda i, j: (i, 0)),
            pl.BlockSpec((tm, RWKV_WIDTH), lambda i, j: (i, 0)),
            pl.BlockSpec((None, ATTN_WIDTH, tn), lambda i, j: (layer, 0, j)),
            pl.BlockSpec((None, RWKV_WIDTH, tn), lambda i, j: (layer, 0, j)),
            pl.BlockSpec((tm, tn), lambda i, j: (i, j)),
            pl.BlockSpec((tm, tn), lambda i, j: (i, j + nj)),
        ],
        out_specs=pl.BlockSpec((tm, tn), lambda i, j: (i, j)),
        out_shape=jax.ShapeDtypeStruct((n, D_MODEL), BF16),
        compiler_params=_cparams("parallel", "arbitrary"),
        name="gated_merge",
    )(attn, y, wa, wb, gates, gates)


def _outproj_router_kernel(m_ref, wo_ref, x_ref, nw_ref, wr_ref, br_ref,
                           xo_ref, hf_ref, slab_ref, cnt_ref, run_scr):
    @pl.when(pl.program_id(0) == 0)
    def _():
        run_scr[...] = jnp.zeros_like(run_scr)

    xn = x_ref[...] + jnp.dot(m_ref[...], wo_ref[...], preferred_element_type=F32)
    xo_ref[...] = xn
    hf = _rms(xn, nw_ref[...])
    _store_rows(hf_ref, hf)

    h_hi = hf.astype(BF16)
    h_lo = (hf - h_hi.astype(F32)).astype(BF16)
    wr = wr_ref[...]
    w_hi = wr.astype(BF16)
    w_lo = (wr - w_hi.astype(F32)).astype(BF16)
    lg = (jnp.dot(h_hi, w_hi, preferred_element_type=F32)
          + jnp.dot(h_hi, w_lo, preferred_element_type=F32)
          + jnp.dot(h_lo, w_hi, preferred_element_type=F32)) + br_ref[...]

    tm = lg.shape[0]
    lane = lax.broadcasted_iota(jnp.int32, lg.shape, 1).astype(F32)
    neg = -jnp.inf
    big = float(ROUTER_LANES)

    def first_argmax(vals):
        mx = jnp.max(vals, axis=-1, keepdims=True)
        idx = jnp.min(jnp.where(vals == mx, lane, big), axis=-1, keepdims=True)
        return mx, idx

    is_group = lane < N_GROUPS
    gmax, gsel = first_argmax(jnp.where(is_group, lg, neg))
    gp = 1.0 / jnp.sum(jnp.where(is_group, jnp.exp(lg - gmax), 0.0), axis=-1, keepdims=True)
    lo_lane = EXPERT_LANE0 + EXPERTS_PER_GROUP * gsel
    in_group = (lane >= lo_lane) & (lane < lo_lane + EXPERTS_PER_GROUP)
    el = jnp.where(in_group, lg, neg)
    v1, i1 = first_argmax(el)
    v2, i2 = first_argmax(jnp.where(lane == i1, neg, el))
    e21 = jnp.exp(v2 - v1)
    ew1 = gp / (1.0 + e21)
    ew2 = gp * e21 / (1.0 + e21)

    oh1 = lane == i1
    oh2 = lane == i2
    cnt = oh1.astype(F32) + oh2.astype(F32)
    r_i = lax.broadcasted_iota(jnp.int32, (tm, tm), 0)
    c_i = lax.broadcasted_iota(jnp.int32, (tm, tm), 1)
    before = jnp.dot((r_i > c_i).astype(BF16), cnt.astype(BF16), preferred_element_type=F32)
    tot = before + run_scr[0:1, :]
    rank1 = jnp.sum(jnp.where(oh1, tot, 0.0), axis=-1, keepdims=True)
    rank2 = jnp.sum(jnp.where(oh2, tot, 0.0), axis=-1, keepdims=True)
    run = run_scr[0:1, :] + jnp.sum(cnt, axis=0, keepdims=True)
    run_scr[...] = jnp.broadcast_to(run, run_scr.shape)
    cnt_ref[...] = jnp.broadcast_to(run, cnt_ref.shape)

    slab = jnp.where(lane == 0, i1 - EXPERT_LANE0,
           jnp.where(lane == 1, i2 - EXPERT_LANE0,
           jnp.where(lane == 2, ew1,
           jnp.where(lane == 3, ew2,
           jnp.where(lane == 4, rank1,
           jnp.where(lane == 5, rank2, 0.0))))))
    slab_ref[...] = slab


def _outproj_router(merged, w_out, x, norm_w, w_router, b_router, tm=256):
    n = x.shape[0]
    row = lambda i: (i, 0)
    const = lambda i: (0, 0)
    return pl.pallas_call(
        _outproj_router_kernel,
        grid=(n // tm,),
        in_specs=[
            pl.BlockSpec((tm, D_MODEL), row),
            pl.BlockSpec((D_MODEL, D_MODEL), const),
            pl.BlockSpec((tm, D_MODEL), row),
            pl.BlockSpec((1, D_MODEL), const),
            pl.BlockSpec((D_MODEL, ROUTER_LANES), const),
            pl.BlockSpec((1, ROUTER_LANES), const),
        ],
        out_specs=[
            pl.BlockSpec((tm, D_MODEL), row),
            pl.BlockSpec((tm * ROW_SUB, LANES), row),
            pl.BlockSpec((tm, ROUTER_LANES), row),
            pl.BlockSpec((SUBLANES, ROUTER_LANES), const),
        ],
        out_shape=[
            jax.ShapeDtypeStruct((n, D_MODEL), F32),
            jax.ShapeDtypeStruct((n * ROW_SUB, LANES), F32),
            jax.ShapeDtypeStruct((n, ROUTER_LANES), F32),
            jax.ShapeDtypeStruct((SUBLANES, ROUTER_LANES), F32),
        ],
        scratch_shapes=[pltpu.VMEM((SUBLANES, ROUTER_LANES), F32)],
        compiler_params=_cparams("arbitrary"),
        name="outproj_router",
    )(merged, w_out, x, norm_w, w_router, b_router)


def _dispatch_kernel(slot_ref, hf_ref, xb_in_hbm, xb_hbm, sem, *, tm):
    del xb_in_hbm

    def copy(u, j):
        return pltpu.make_async_copy(_slab(hf_ref, u), _slab(xb_hbm, slot_ref[TOP_K * u + j]), sem)

    def start(u, c):
        for j in range(TOP_K):
            copy(u, j).start(priority=j)
        return c

    def wait(u, c):
        for j in range(TOP_K):
            copy(u, j).wait()
        return c

    lax.fori_loop(0, tm, start, 0, unroll=8)
    lax.fori_loop(0, tm, wait, 0, unroll=8)


def _dispatch(slot_flat, hf, cap, tm=256):
    n = hf.shape[0] // ROW_SUB
    xb0 = jnp.zeros((cap * ROW_SUB, LANES), F32)
    return pl.pallas_call(
        functools.partial(_dispatch_kernel, tm=tm),
        grid=(n // tm,),
        in_specs=[
            pl.BlockSpec((TOP_K * tm,), lambda i: (i,), memory_space=pltpu.SMEM),
            pl.BlockSpec((tm * ROW_SUB, LANES), lambda i: (i, 0)),
            pl.BlockSpec(memory_space=pl.ANY),
        ],
        out_specs=pl.BlockSpec(memory_space=pl.ANY),
        out_shape=jax.ShapeDtypeStruct((cap * ROW_SUB, LANES), F32),
        scratch_shapes=[pltpu.SemaphoreType.DMA(())],
        input_output_aliases={2: 0},
        compiler_params=_cparams("arbitrary"),
        name="moe_dispatch",
    )(slot_flat, hf, xb0)


def _ffn_kernel(be_ref, nu_ref, par_ref, nxt_ref, x_ref, wg_hbm, wu_hbm, wd_hbm, o_ref,
                wg_f, wu_f, wd_f, wg_s, wu_s, wd_s, sems, *, layer):
    b = pl.program_id(0)
    used = b < nu_ref[0]
    prev_e = be_ref[jnp.maximum(b - 1, 0)]
    new_expert = jnp.logical_or(b == 0, be_ref[b] != prev_e)

    def fetch(e, p):
        return [pltpu.make_async_copy(src.at[layer, e], dst.at[p], sems.at[p, i])
                for i, (src, dst) in enumerate(((wg_hbm, wg_f), (wu_hbm, wu_f), (wd_hbm, wd_f)))]

    @pl.when(jnp.logical_and(used, new_expert))
    def _():
        for p in range(2):
            @pl.when(par_ref[b] == p)
            def _():
                @pl.when(b == 0)
                def _():
                    for cp in fetch(be_ref[0], p):
                        cp.start()

                @pl.when(nxt_ref[b] >= 0)
                def _():
                    for cp in fetch(nxt_ref[b], 1 - p):
                        cp.start()

                for cp in fetch(be_ref[b], p):
                    cp.wait()
                wg_s[...] = wg_f[p].astype(BF16)
                wu_s[...] = wu_f[p].astype(BF16)
                wd_s[...] = wd_f[p].astype(BF16)

    @pl.when(used)
    def _():
        x = _rows_to_2d(x_ref, MOE_ROWS).astype(BF16)
        gt = jnp.dot(x, wg_s[...], preferred_element_type=F32)
        up = jnp.dot(x, wu_s[...], preferred_element_type=F32)
        hid = (gt * jax.nn.sigmoid(gt) * up).astype(BF16)
        _store_rows(o_ref, jnp.dot(hid, wd_s[...], preferred_element_type=F32))

    @pl.when(jnp.logical_not(used))
    def _():
        o_ref[...] = jnp.zeros_like(o_ref)


def _expert_ffn(layer, plan, xb, w_gate, w_up, w_down):
    nblk = xb.shape[0] // (MOE_ROWS * ROW_SUB)
    hbm = pl.BlockSpec(memory_space=pl.ANY)
    grid_spec = pltpu.PrefetchScalarGridSpec(
        num_scalar_prefetch=4,
        grid=(nblk,),
        in_specs=[
            pl.BlockSpec((MOE_ROWS * ROW_SUB, LANES), lambda b, be, nu, par, nxt: (jnp.minimum(b, nu[0] - 1), 0)),
            hbm, hbm, hbm,
        ],
        out_specs=pl.BlockSpec((MOE_ROWS * ROW_SUB, LANES), lambda b, be, nu, par, nxt: (b, 0)),
        scratch_shapes=[
            pltpu.VMEM((2, D_MODEL, EXPERT_HIDDEN), F32),
            pltpu.VMEM((2, D_MODEL, EXPERT_HIDDEN), F32),
            pltpu.VMEM((2, EXPERT_HIDDEN, D_MODEL), F32),
            pltpu.VMEM((D_MODEL, EXPERT_HIDDEN), BF16),
            pltpu.VMEM((D_MODEL, EXPERT_HIDDEN), BF16),
            pltpu.VMEM((EXPERT_HIDDEN, D_MODEL), BF16),
            pltpu.SemaphoreType.DMA((2, 3)),
        ],
    )
    return pl.pallas_call(
        functools.partial(_ffn_kernel, layer=layer),
        grid_spec=grid_spec,
        out_shape=jax.ShapeDtypeStruct(xb.shape, F32),
        compiler_params=_cparams("arbitrary"),
        name="expert_ffn",
    )(plan.block_expert, plan.n_used, plan.parity, plan.next_expert, xb, w_gate, w_up, w_down)


def _combine_kernel(slot_ref, slot_next_ref, x_ref, slab_ref, fw_ref, yb_hbm, o_ref,
                    buf00, buf01, buf10, buf11, sems, *, tm, n_tiles, final_norm):
    i = pl.program_id(0)
    bufs = ((buf00, buf01), (buf10, buf11))

    def copies(slots, p, u):
        return [pltpu.make_async_copy(_slab(yb_hbm, slots[TOP_K * u + j]), _slab(bufs[p][j], u), sems.at[p])
                for j in range(TOP_K)]

    def start_all(slots, p):
        def body(u, c):
            for j, cp in enumerate(copies(slots, p, u)):
                cp.start(priority=j)
            return c
        lax.fori_loop(0, tm, body, 0, unroll=8)

    def wait_all(p):
        def body(u, c):
            for cp in copies(slot_ref, p, u):
                cp.wait()
            return c
        lax.fori_loop(0, tm, body, 0, unroll=8)

    @pl.when(i == 0)
    def _():
        start_all(slot_ref, 0)

    for p in range(2):
        @pl.when(i % 2 == p)
        def _():
            @pl.when(i + 1 < n_tiles)
            def _():
                start_all(slot_next_ref, 1 - p)

            wait_all(p)
            slab = slab_ref[...]
            w1 = slab[:, 2:3]
            w2 = slab[:, 3:4]
            out = x_ref[...] + (_rows_to_2d(bufs[p][0], tm) * w1 + _rows_to_2d(bufs[p][1], tm) * w2)
            if final_norm:
                out = _rms(out, fw_ref[...])
            o_ref[...] = out


def _combine(slot_flat, x, slab, final_w, yb, final_norm, tm=256):
    n = x.shape[0]
    n_tiles = n // tm
    row = lambda i: (i, 0)
    row_buf = pltpu.VMEM((tm * ROW_SUB, LANES), F32)
    return pl.pallas_call(
        functools.partial(_combine_kernel, tm=tm, n_tiles=n_tiles, final_norm=final_norm),
        grid=(n_tiles,),
        in_specs=[
            pl.BlockSpec((TOP_K * tm,), lambda i: (i,), memory_space=pltpu.SMEM),
            pl.BlockSpec((TOP_K * tm,), lambda i: (jnp.minimum(i + 1, n_tiles - 1),), memory_space=pltpu.SMEM),
            pl.BlockSpec((tm, D_MODEL), row),
            pl.BlockSpec((tm, ROUTER_LANES), row),
            pl.BlockSpec((1, D_MODEL), lambda i: (0, 0)),
            pl.BlockSpec(memory_space=pl.ANY),
        ],
        out_specs=pl.BlockSpec((tm, D_MODEL), row),
        out_shape=jax.ShapeDtypeStruct((n, D_MODEL), F32),
        scratch_shapes=[row_buf, row_buf, row_buf, row_buf, pltpu.SemaphoreType.DMA((2,))],
        compiler_params=_cparams("arbitrary"),
        name="moe_combine",
    )(slot_flat, slot_flat, x, slab, final_w, yb)


def _pad_rows(w, rows, offset=0):
    out = jnp.zeros((rows, w.shape[1]), BF16)
    return out.at[offset:offset + w.shape[0]].set(w.astype(BF16))


def _pad_cols(vec, cols, offset=0):
    out = jnp.zeros((cols,), F32)
    return out.at[offset:offset + vec.shape[0]].set(vec.astype(F32))


class MoePlan(NamedTuple):
    slot: jax.Array
    block_expert: jax.Array
    n_used: jax.Array
    parity: jax.Array
    next_expert: jax.Array
    cap: int


def _moe_plan(slab, counts_row, n_tokens):
    eid = slab[:, 0:TOP_K].astype(jnp.int32)
    rank = slab[:, 4:4 + TOP_K].astype(jnp.int32)
    counts = counts_row[EXPERT_LANE0:EXPERT_LANE0 + N_EXPERTS].astype(jnp.int32)
    padded = (counts + MOE_ROWS - 1) // MOE_ROWS * MOE_ROWS
    pad_ends = jnp.cumsum(padded)
    pad_starts = pad_ends - padded
    slot = pad_starts[eid.reshape(-1)] + rank.reshape(-1)
    nblk = n_tokens * TOP_K // MOE_ROWS + N_EXPERTS
    block_start = jnp.arange(nblk, dtype=jnp.int32) * MOE_ROWS
    block_expert = jnp.minimum(
        jnp.sum((pad_ends[None, :] <= block_start[:, None]).astype(jnp.int32), axis=1),
        N_EXPERTS - 1).astype(jnp.int32)
    n_used = (pad_ends[-1:] // MOE_ROWS).astype(jnp.int32)
    blk = jnp.arange(nblk, dtype=jnp.int32)
    prev_expert = jnp.concatenate([block_expert[:1], block_expert[:-1]])
    first = (blk < n_used[0]) & ((blk == 0) | (block_expert != prev_expert))
    parity = ((jnp.cumsum(first.astype(jnp.int32)) - 1) % 2).astype(jnp.int32)
    later_first = first[None, :] & (blk[None, :] > blk[:, None])
    next_first = jnp.min(jnp.where(later_first, blk[None, :], nblk), axis=1)
    next_expert = jnp.where(next_first < nblk, block_expert[jnp.minimum(next_first, nblk - 1)], -1)
    return MoePlan(slot, block_expert, n_used, parity, next_expert.astype(jnp.int32), nblk * MOE_ROWS)


def kernel(x, attn_norm_w, w_in, shift_mix, attn_sinks, rwkv_w0, rwkv_w_up, rwkv_a0, rwkv_a_up, rwkv_g_up, rwkv_k_k, rwkv_k_a, rwkv_r_k, rwkv_ln_w, rwkv_ln_b, vres_down, vres_mix, vres_up, vres_v0, w_branch_a, w_branch_b, w_out, ffn_norm_w, router_group_w, router_group_b, router_expert_w, router_expert_b, expert_w_gate, expert_w_up, expert_w_down, final_norm_w):
    batch, seq, _ = x.shape
    n = batch * seq
    depth = w_in.shape[0]
    xf = x.reshape(n, D_MODEL)
    w_in_bf = w_in.astype(BF16)
    v_first = None
    for i in range(depth):
        has_vres = i > 0
        w_gates = w_in_bf[i, :, GATE_COL0:]
        mix_r = shift_mix[i, :RKV_COLS].reshape(1, RKV_COLS)
        mix_l = _pad_cols(shift_mix[i, RKV_COLS:], LORA_COLS).reshape(1, LORA_COLS)
        v0 = vres_v0[i - 1] if has_vres else jnp.zeros((RWKV_WIDTH,), F32)
        zero = jnp.zeros((RWKV_WIDTH,), F32)
        pv_pre = jnp.stack([rwkv_w0[i], rwkv_a0[i], rwkv_k_k[i], rwkv_k_a[i], v0, zero, zero, zero]).astype(F32)
        pv_post = jnp.stack([rwkv_ln_w[i], rwkv_ln_b[i], rwkv_r_k[i].reshape(-1),
                             zero, zero, zero, zero, zero]).astype(F32)
        wup = _pad_rows(rwkv_w_up[i], LANES, 0)
        aup = _pad_rows(rwkv_a_up[i], LANES, DECAY_LORA)
        gup = _pad_rows(rwkv_g_up[i], 2 * LANES, 0)
        if has_vres:
            vd = _pad_rows(vres_down[i - 1].T, LANES, 0).T
            mix_v = _pad_cols(vres_mix[i - 1], LANES).reshape(1, LANES)
            vup = _pad_rows(vres_up[i - 1], LANES, 0)
        else:
            vd, mix_v = None, None
            vup = jnp.zeros((LANES, RWKV_WIDTH), BF16)
        w_router = jnp.concatenate(
            [router_group_w[i], router_expert_w[i],
             jnp.zeros((D_MODEL, ROUTER_LANES - N_GROUPS - N_EXPERTS), F32)], axis=1)
        b_router = _pad_cols(jnp.concatenate([router_group_b[i], router_expert_b[i]]),
                             ROUTER_LANES).reshape(1, ROUTER_LANES)

        proj = _inproj(i, xf, attn_norm_w[i].reshape(1, D_MODEL), w_in_bf, w_gates, vd)
        qkv, rkv, lora, gates = proj[:4]
        zv = proj[4] if has_vres else None
        attn = _attention(qkv, attn_sinks[i], seq)
        r, lw, k2, v, al, be, g = _rwkv_pre(rkv, lora, v_first, zv, mix_r, mix_l, mix_v, pv_pre,
                                            wup, aup, gup, vup, seq)
        if not has_vres:
            v_first = v
        y_raw = _rwkv_scan(r, lw, k2, v, al, be, seq)
        y = _rwkv_post(y_raw, r, k2, v, g, pv_post)
        merged = _merge(i, attn, y, w_branch_a, w_branch_b, gates)

        x_mid, hf, slab, counts = _outproj_router(
            merged, w_out[i].astype(BF16), xf, ffn_norm_w[i].reshape(1, D_MODEL), w_router, b_router)
        plan = _moe_plan(slab, counts[0], n)
        xb = _dispatch(plan.slot, hf, plan.cap)
        yb = _expert_ffn(i, plan, xb, expert_w_gate, expert_w_up, expert_w_down)
        xf = _combine(plan.slot, x_mid, slab, final_norm_w.reshape(1, D_MODEL), yb, final_norm=(i == depth - 1))
    return xf.reshape(batch, seq, D_MODEL)
```

```python
import functools
from typing import NamedTuple

import jax
import jax.numpy as jnp
from jax import lax
from jax.experimental import pallas as pl
from jax.experimental.pallas import tpu as pltpu

F32 = jnp.float32
BF16 = jnp.bfloat16

D_MODEL = 2048
HEAD_DIM = 64
N_Q_HEADS = 16
N_KV_HEADS = 4
GQA_GROUP = N_Q_HEADS // N_KV_HEADS
ATTN_WIDTH = N_Q_HEADS * HEAD_DIM
KV_WIDTH = N_KV_HEADS * HEAD_DIM
WINDOW = 128
ATTN_BLOCK = 128
ATTN_SUB = 2

RWKV_HEAD = 64
N_RWKV_HEADS = 16
RWKV_WIDTH = N_RWKV_HEADS * RWKV_HEAD
DECAY_LORA = 64
AAA_LORA = 64
MV_LORA = 32
GATE_LORA = 160
RWKV_GN_EPS = 64e-5
LORA_WIDTH = DECAY_LORA + AAA_LORA + GATE_LORA

N_GROUPS = 4
EXPERTS_PER_GROUP = 8
N_EXPERTS = N_GROUPS * EXPERTS_PER_GROUP
TOP_K = 2
EXPERT_HIDDEN = D_MODEL // 4
NORM_EPS = 1e-5

LANES = 128
SUBLANES = 8
MXU_COLS = 256

SEG = 512
QKV_COLS = ATTN_WIDTH + 2 * KV_WIDTH
RKV_COLS = 3 * RWKV_WIDTH
LORA_COLS = SEG
GATE_COLS = 2 * D_MODEL
QKV_TILES = QKV_COLS // SEG
RKV_TILES = RKV_COLS // SEG
LORA_TILES = LORA_COLS // SEG
GATE_TILES = GATE_COLS // SEG
IN_TILES = QKV_TILES + RKV_TILES + LORA_TILES + GATE_TILES
FIRST_GATE_TILE = QKV_TILES + RKV_TILES + LORA_TILES
GATE_COL0 = QKV_COLS + RKV_COLS + LORA_WIDTH

SCAN_CHUNK = 64
SCAN_GROUP = 4
MOE_ROWS = 256
ROUTER_LANES = LANES
ROUTER_ROWS = 128
EXPERT_LANE0 = N_GROUPS

ROW_SUB = D_MODEL // LANES

VMEM_LIMIT = 56 * 1024 * 1024


def _rows_to_2d(slab_ref, rows):
    return jnp.concatenate([slab_ref[pl.ds(s, rows, stride=ROW_SUB), :] for s in range(ROW_SUB)], axis=1)


def _store_rows(slab_ref, val):
    rows = val.shape[0]
    for s in range(ROW_SUB):
        slab_ref[pl.ds(s, rows, stride=ROW_SUB), :] = val[:, s * LANES:(s + 1) * LANES]


def _slab(ref, row):
    return ref.at[pl.ds(pl.multiple_of(row * ROW_SUB, ROW_SUB), ROW_SUB)]


def _cparams(*sem):
    return pltpu.CompilerParams(dimension_semantics=sem, vmem_limit_bytes=VMEM_LIMIT)


def _bdot(a, b):
    return jnp.dot(a.astype(BF16), b.astype(BF16), preferred_element_type=F32)


def _bdot_nt(a, b):
    return lax.dot_general(a.astype(BF16), b.astype(BF16), (((1,), (1,)), ((), ())),
                           preferred_element_type=F32)


def _bdot_tn(a, b):
    return lax.dot_general(a.astype(BF16), b.astype(BF16), (((0,), (0,)), ((), ())),
                           preferred_element_type=F32)


def _dot_split_rhs(w01, x, passes):
    acc, rem = None, x
    for p in range(passes):
        part = rem.astype(BF16)
        d = jnp.dot(w01, part, preferred_element_type=F32)
        acc = d if acc is None else acc + d
        if p + 1 < passes:
            rem = rem - part.astype(F32)
    return acc


def _dot_split_lhs(x, w01, passes):
    acc, rem = None, x
    for p in range(passes):
        part = rem.astype(BF16)
        d = jnp.dot(part, w01, preferred_element_type=F32)
        acc = d if acc is None else acc + d
        if p + 1 < passes:
            rem = rem - part.astype(F32)
    return acc


def _rms(x, w):
    ms = jnp.mean(x * x, axis=-1, keepdims=True)
    return x * lax.rsqrt(ms + NORM_EPS) * w


def _inproj_kernel(*refs, has_vres):
    refs = list(refs)
    x_ref, nw_ref, w_ref, wg_ref = refs[:4]
    refs = refs[4:]
    vd_ref = refs.pop(0) if has_vres else None
    qkv_ref, rkv_ref, lora_ref, gate_ref = refs[:4]
    refs = refs[4:]
    zv_ref = refs.pop(0) if has_vres else None
    (h_scr,) = refs
    j = pl.program_id(1)

    @pl.when(j == 0)
    def _():
        h_scr[...] = _rms(x_ref[...], nw_ref[...]).astype(BF16)
        if has_vres:
            zv_ref[...] = jnp.dot(h_scr[...], vd_ref[...], preferred_element_type=F32)

    def project(wt_ref, out_ref):
        for c in range(SEG // MXU_COLS):
            cs = slice(c * MXU_COLS, (c + 1) * MXU_COLS)
            acc = jnp.dot(h_scr[...], wt_ref[:, cs], preferred_element_type=F32)
            out_ref[:, cs] = acc.astype(out_ref.dtype)

    @pl.when(j < QKV_TILES)
    def _():
        project(w_ref, qkv_ref)

    @pl.when((j >= QKV_TILES) & (j < QKV_TILES + RKV_TILES))
    def _():
        project(w_ref, rkv_ref)

    @pl.when(j == QKV_TILES + RKV_TILES)
    def _():
        project(w_ref, lora_ref)

    @pl.when(j >= FIRST_GATE_TILE)
    def _():
        project(wg_ref, gate_ref)


def _inproj(layer, x, norm_w, w_in, w_gates, vd, tm=1024):
    n = x.shape[0]
    has_vres = vd is not None
    s_rkv = QKV_TILES
    in_specs = [
        pl.BlockSpec((tm, D_MODEL), lambda i, j: (i, 0)),
        pl.BlockSpec((1, D_MODEL), lambda i, j: (0, 0)),
        pl.BlockSpec((None, D_MODEL, SEG), lambda i, j: (layer, 0, jnp.minimum(j, FIRST_GATE_TILE - 1))),
        pl.BlockSpec((D_MODEL, SEG), lambda i, j: (0, jnp.clip(j - FIRST_GATE_TILE, 0, GATE_TILES - 1))),
    ]
    args = [x, norm_w, w_in, w_gates]
    out_specs = [
        pl.BlockSpec((tm, SEG), lambda i, j: (i, jnp.clip(j, 0, QKV_TILES - 1))),
        pl.BlockSpec((tm, SEG), lambda i, j: (i, jnp.clip(j - s_rkv, 0, RKV_TILES - 1))),
        pl.BlockSpec((tm, SEG), lambda i, j: (i, 0)),
        pl.BlockSpec((tm, SEG), lambda i, j: (i, jnp.clip(j - FIRST_GATE_TILE, 0, GATE_TILES - 1))),
    ]
    out_shape = [
        jax.ShapeDtypeStruct((n, QKV_COLS), BF16),
        jax.ShapeDtypeStruct((n, RKV_COLS), F32),
        jax.ShapeDtypeStruct((n, LORA_COLS), F32),
        jax.ShapeDtypeStruct((n, GATE_COLS), BF16),
    ]
    if has_vres:
        in_specs.append(pl.BlockSpec((D_MODEL, LANES), lambda i, j: (0, 0)))
        args.append(vd)
        out_specs.append(pl.BlockSpec((tm, LANES), lambda i, j: (i, 0)))
        out_shape.append(jax.ShapeDtypeStruct((n, LANES), F32))
    return pl.pallas_call(
        functools.partial(_inproj_kernel, has_vres=has_vres),
        grid=(n // tm, IN_TILES),
        in_specs=in_specs,
        out_specs=out_specs,
        out_shape=out_shape,
        scratch_shapes=[pltpu.VMEM((tm, D_MODEL), BF16)],
        compiler_params=_cparams("parallel", "arbitrary"),
        name="inproj",
    )(*args)


def _attn_kernel(sink_ref, q_ref, kp_ref, kc_ref, vp_ref, vc_ref, bias_ref, o_ref, *, tiles_per_seq):
    first = (pl.program_id(0) % tiles_per_seq) == 0
    col = lax.broadcasted_iota(jnp.int32, (ATTN_BLOCK, 2 * ATTN_BLOCK), 1)
    pad_keys = jnp.logical_and(first, col < ATTN_BLOCK)
    scale = HEAD_DIM ** -0.5
    items = [(sb, hk) for sb in range(ATTN_SUB) for hk in range(N_KV_HEADS)]

    def window(prev_ref, cur_ref, sb, hk):
        ks = slice(hk * HEAD_DIM, (hk + 1) * HEAD_DIM)
        rows = slice(sb * ATTN_BLOCK, (sb + 1) * ATTN_BLOCK)
        prev = prev_ref[:, ks] if sb == 0 else cur_ref[(sb - 1) * ATTN_BLOCK:sb * ATTN_BLOCK, ks]
        return jnp.concatenate([prev, cur_ref[rows, ks]], axis=0)

    scores = {}
    for sb, hk in items:
        kw = window(kp_ref, kc_ref, sb, hk)
        for g in range(GQA_GROUP):
            h = hk * GQA_GROUP + g
            qh = q_ref[sb * ATTN_BLOCK:(sb + 1) * ATTN_BLOCK, h * HEAD_DIM:(h + 1) * HEAD_DIM]
            s = lax.dot_general(qh, kw, (((1,), (1,)), ((), ())), preferred_element_type=F32)
            s = s * scale + bias_ref[h]
            if sb == 0:
                s = jnp.where(pad_keys, -jnp.inf, s)
            scores[(sb, h)] = s
    row_max = {key: jnp.max(s, axis=-1, keepdims=True) for key, s in scores.items()}
    m = {key: jnp.maximum(row_max[key], sink_ref[key[1]]) for key in scores}
    p = {key: jnp.exp(scores[key] - m[key]) for key in scores}
    row_sum = {key: jnp.sum(p[key], axis=-1, keepdims=True) for key in scores}
    denoms = {key: row_sum[key] + jnp.exp(sink_ref[key[1]] - m[key]) for key in scores}
    probs = {key: p[key].astype(BF16) for key in scores}
    for sb, hk in items:
        vw = window(vp_ref, vc_ref, sb, hk)
        for g in range(GQA_GROUP):
            h = hk * GQA_GROUP + g
            o = jnp.dot(probs[(sb, h)], vw, preferred_element_type=F32) / denoms[(sb, h)]
            o_ref[sb * ATTN_BLOCK:(sb + 1) * ATTN_BLOCK, h * HEAD_DIM:(h + 1) * HEAD_DIM] = o.astype(BF16)


def _attn_bias():
    qi = jnp.arange(ATTN_BLOCK)[:, None]
    kj = jnp.arange(2 * ATTN_BLOCK)[None, :]
    dist = qi + ATTN_BLOCK - kj
    valid = (dist >= 0) & (dist < WINDOW)
    slopes = jnp.exp2(-8.0 * jnp.arange(1, N_Q_HEADS + 1, dtype=F32) / N_Q_HEADS)
    bias = -slopes[:, None, None] * dist.astype(F32)[None]
    return jnp.where(valid[None], bias, -jnp.inf)


def _attention(qkv, sinks, seq):
    n = qkv.shape[0]
    rows = ATTN_SUB * ATTN_BLOCK
    tiles_per_seq = seq // rows
    kcol = ATTN_WIDTH // KV_WIDTH
    prev = lambda i: jnp.maximum(i * ATTN_SUB - 1, 0)
    return pl.pallas_call(
        functools.partial(_attn_kernel, tiles_per_seq=tiles_per_seq),
        grid=(n // rows,),
        in_specs=[
            pl.BlockSpec(memory_space=pltpu.SMEM),
            pl.BlockSpec((rows, ATTN_WIDTH), lambda i: (i, 0)),
            pl.BlockSpec((ATTN_BLOCK, KV_WIDTH), lambda i: (prev(i), kcol)),
            pl.BlockSpec((rows, KV_WIDTH), lambda i: (i, kcol)),
            pl.BlockSpec((ATTN_BLOCK, KV_WIDTH), lambda i: (prev(i), kcol + 1)),
            pl.BlockSpec((rows, KV_WIDTH), lambda i: (i, kcol + 1)),
            pl.BlockSpec((N_Q_HEADS, ATTN_BLOCK, 2 * ATTN_BLOCK), lambda i: (0, 0, 0)),
        ],
        out_specs=pl.BlockSpec((rows, ATTN_WIDTH), lambda i: (i, 0)),
        out_shape=jax.ShapeDtypeStruct((n, ATTN_WIDTH), BF16),
        compiler_params=_cparams("parallel"),
        name="swa_attention",
    )(sinks.astype(F32), qkv, qkv, qkv, qkv, qkv, _attn_bias())


def _head_ones():
    ch = jnp.arange(MXU_COLS) // RWKV_HEAD
    return (ch[:, None] == ch[None, :]).astype(BF16)


def _head_sums(x, ones, passes):
    parts = [_dot_split_lhs(x[:, c:c + MXU_COLS], ones, passes) for c in range(0, RWKV_WIDTH, MXU_COLS)]
    return jnp.concatenate(parts, axis=1)


def _rwkv_pre_kernel(*refs, has_vres, tiles_per_seq):
    refs = list(refs)
    rkv_ref, rkvp_ref, lo_ref, lop_ref = refs[:4]
    refs = refs[4:]
    if has_vres:
        vf_ref, zv_ref, zvp_ref, mixv_ref = refs[:4]
        refs = refs[4:]
    (mixr_ref, mixl_ref, pv_ref, wup_ref, aup_ref, gup_ref, vup_ref, ones_ref,
     r_out, lw_out, k_out, v_out, al_out, be_out, g_out) = refs

    first = (pl.program_id(0) % tiles_per_seq) == 0

    def shifted(z, zp_ref):
        prev_row = jnp.where(first, 0.0, zp_ref[SUBLANES - 1:SUBLANES, :])
        row = lax.broadcasted_iota(jnp.int32, z.shape, 0)
        return jnp.where(row == 0, prev_row, pltpu.roll(z, 1, 0))

    z = rkv_ref[...]
    zs = z + (shifted(z, rkvp_ref) - z) * mixr_ref[...]
    lo = lo_ref[...]
    los = lo + (shifted(lo, lop_ref) - lo) * mixl_ref[...]

    r = zs[:, :RWKV_WIDTH]
    kr = zs[:, RWKV_WIDTH:2 * RWKV_WIDTH]
    vr = zs[:, 2 * RWKV_WIDTH:]

    col = lax.broadcasted_iota(jnp.int32, los.shape, 1)
    gate_cols = (col >= DECAY_LORA + AAA_LORA) & (col < LORA_WIDTH)
    act = jnp.where(col < DECAY_LORA, jnp.tanh(los), jnp.where(gate_cols, jax.nn.sigmoid(los), los))
    act = act.astype(BF16)
    wa_in = act[:, :LANES]
    dw = jnp.dot(wa_in, wup_ref[...], preferred_element_type=F32)
    da = jnp.dot(wa_in, aup_ref[...], preferred_element_type=F32)
    g = jnp.dot(act[:, LANES:3 * LANES], gup_ref[...], preferred_element_type=F32)

    w0, a0, k_k, k_a, v0 = (pv_ref[i:i + 1, :] for i in range(5))
    u = -(w0 + dw)
    softplus = jnp.maximum(u, 0.0) + jnp.log1p(jnp.exp(-jnp.abs(u)))
    w_log = -softplus - 0.5
    a = jax.nn.sigmoid(a0 + da)
    if has_vres:
        zv = zv_ref[...]
        zvs = zv + (shifted(zv, zvp_ref) - zv) * mixv_ref[...]
        dv = jnp.dot(zvs.astype(BF16), vup_ref[...], preferred_element_type=F32)
        vr = vr + (vf_ref[...] - vr) * jax.nn.sigmoid(v0 + dv)

    kk0 = kr * k_k
    ss = _head_sums(kk0 * kk0, ones_ref[...], 2)
    kk = kk0 * (1.0 / jnp.maximum(jnp.sqrt(ss), 1e-12))

    r_out[...] = r
    lw_out[...] = -jnp.exp(w_log)
    k_out[...] = kr * (1.0 + (a - 1.0) * k_a)
    v_out[...] = vr
    al_out[...] = -kk
    be_out[...] = kk * a
    g_out[...] = g


def _rwkv_pre(rkv, lora, v_first, zv, mix_r, mix_l, mix_v, pvec, wup, aup, gup, vup, seq, tt=256):
    n = rkv.shape[0]
    has_vres = v_first is not None
    row = lambda i: (i, 0)
    prev8 = lambda i: (jnp.maximum(i * (tt // SUBLANES) - 1, 0), 0)
    const = lambda i: (0, 0)
    in_specs = [
        pl.BlockSpec((tt, RKV_COLS), row),
        pl.BlockSpec((SUBLANES, RKV_COLS), prev8),
        pl.BlockSpec((tt, LORA_COLS), row),
        pl.BlockSpec((SUBLANES, LORA_COLS), prev8),
    ]
    args = [rkv, rkv, lora, lora]
    if has_vres:
        in_specs += [
            pl.BlockSpec((tt, RWKV_WIDTH), row),
            pl.BlockSpec((tt, LANES), row),
            pl.BlockSpec((SUBLANES, LANES), prev8),
            pl.BlockSpec((1, LANES), const),
        ]
        args += [v_first, zv, zv, mix_v]
    in_specs += [
        pl.BlockSpec((1, RKV_COLS), const),
        pl.BlockSpec((1, LORA_COLS), const),
        pl.BlockSpec((SUBLANES, RWKV_WIDTH), const),
        pl.BlockSpec((LANES, RWKV_WIDTH), const),
        pl.BlockSpec((LANES, RWKV_WIDTH), const),
        pl.BlockSpec((2 * LANES, RWKV_WIDTH), const),
        pl.BlockSpec((LANES, RWKV_WIDTH), const),
        pl.BlockSpec((MXU_COLS, MXU_COLS), const),
    ]
    args += [mix_r, mix_l, pvec, wup, aup, gup, vup, _head_ones()]
    out = jax.ShapeDtypeStruct((n, RWKV_WIDTH), F32)
    return pl.pallas_call(
        functools.partial(_rwkv_pre_kernel, has_vres=has_vres, tiles_per_seq=seq // tt),
        grid=(n // tt,),
        in_specs=in_specs,
        out_specs=[pl.BlockSpec((tt, RWKV_WIDTH), row)] * 7,
        out_shape=[out] * 7,
        compiler_params=_cparams("parallel"),
        name="rwkv_pre",
    )(*args)


def _scan_kernel(r_ref, lw_ref, k_ref, v_ref, al_ref, be_ref, y_ref, s_scr):
    L = SCAN_CHUNK
    W = SCAN_GROUP * RWKV_HEAD
    n_seq = r_ref.shape[0]
    n_grp = N_RWKV_HEADS // SCAN_GROUP

    @pl.when(pl.program_id(0) == 0)
    def _():
        s_scr[...] = jnp.zeros_like(s_scr)

    row = lax.broadcasted_iota(jnp.int32, (L, L), 0)
    col = lax.broadcasted_iota(jnp.int32, (L, L), 1)
    tri = (row >= col).astype(BF16)

    t_w = lax.broadcasted_iota(jnp.int32, (L, W), 0)
    lane_w = lax.broadcasted_iota(jnp.int32, (L, W), 1)
    s_w = lane_w % RWKV_HEAD
    strict = t_w > s_w
    incl = t_w >= s_w
    eye = (t_w == s_w).astype(F32)
    blk_w = lane_w // RWKV_HEAD
    bd_mask = (lax.broadcasted_iota(jnp.int32, (W, W), 0) // RWKV_HEAD
               == lax.broadcasted_iota(jnp.int32, (W, W), 1) // RWKV_HEAD)

    def blockdiag(w):
        tiled = jnp.concatenate([w.astype(BF16)] * SCAN_GROUP, axis=0)
        return jnp.where(bd_mask, tiled, jnp.zeros_like(tiled))

    items = [(b, g) for b in range(n_seq) for g in range(n_grp)]
    ops = {}
    for b in range(n_seq):
        lw = lw_ref[b]
        cum = _dot_split_rhs(tri, lw, 3)
        p_inc = jnp.exp(cum)
        p_exc = jnp.exp(cum - lw)
        p_inv = jnp.exp(-cum)
        p_last = p_inc[L - 1:L, :]
        a_t = al_ref[b] * p_exc
        r_t = r_ref[b] * p_inc
        b_t = be_ref[b] * p_inv
        k_t = k_ref[b] * p_inv
        b_end = b_t * p_last
        k_end = k_t * p_last
        v = v_ref[b]
        for g in range(n_grp):
            gs = slice(g * W, (g + 1) * W)
            ops[(b, g)] = dict(
                ar=jnp.concatenate([a_t[:, gs], r_t[:, gs]], axis=0),
                bt=b_t[:, gs], kt=k_t[:, gs], v=v[:, gs], p_last=p_last[:, gs],
                bke=jnp.concatenate([b_end[:, gs], k_end[:, gs]], axis=0))

    gm, xs, av, s0 = {}, {}, {}, {}
    for it in items:
        o = ops[it]
        rhs = jnp.concatenate([blockdiag(o['bt']), blockdiag(o['kt'])], axis=0)
        gm[it] = _bdot_nt(o['ar'], rhs)
        s0[it] = s_scr[it[0], it[1]]
        xs[it] = _bdot_nt(o['ar'], blockdiag(s0[it]))
    a_ab, a_rb = {}, {}
    for it in items:
        g = gm[it]
        a_ab[it] = jnp.where(strict, g[:L, :W], 0.0)
        a_rb[it] = jnp.where(incl, g[L:, :W], 0.0)
        a_kk = jnp.concatenate([jnp.where(strict, g[:L, W:], 0.0), jnp.where(incl, g[L:, W:], 0.0)], axis=0)
        av[it] = _bdot(a_kk, blockdiag(ops[it]['v']))

    pw = {it: _bdot(a_ab[it], blockdiag(a_ab[it])) for it in items}
    tinv = {it: eye + a_ab[it] for it in items}
    span = 2
    while 2 * span < L:
        both = {it: _bdot(jnp.concatenate([pw[it], tinv[it]], axis=0), blockdiag(pw[it])) for it in items}
        tinv = {it: tinv[it] + both[it][L:] for it in items}
        pw = {it: both[it][:L] for it in items}
        span *= 2
    tinv = {it: tinv[it] + _bdot(tinv[it], blockdiag(pw[it])) for it in items}

    u = {it: _bdot(tinv[it], blockdiag(xs[it][:L] + av[it][:L])) for it in items}
    for it in items:
        b, g = it
        y_ref[b, :, g * W:(g + 1) * W] = xs[it][L:] + av[it][L:] + _bdot(a_rb[it], blockdiag(u[it]))
    for it in items:
        o = ops[it]
        full = _bdot_tn(jnp.concatenate([u[it], o['v']], axis=0), o['bke'])
        upd = full[:RWKV_HEAD]
        for h in range(1, SCAN_GROUP):
            upd = jnp.where(blk_w == h, full[h * RWKV_HEAD:(h + 1) * RWKV_HEAD], upd)
        s_scr[it[0], it[1]] = s0[it] * o['p_last'] + upd


def _rwkv_scan(r, lw, k, v, al, be, seq):
    n = r.shape[0]
    n_seq = n // seq
    nc = seq // SCAN_CHUNK
    spec = pl.BlockSpec((n_seq, SCAN_CHUNK, RWKV_WIDTH), lambda c: (0, c, 0))
    args = [a.reshape(n_seq, seq, RWKV_WIDTH) for a in (r, lw, k, v, al, be)]
    y = pl.pallas_call(
        _scan_kernel,
        grid=(nc,),
        in_specs=[spec] * 6,
        out_specs=spec,
        out_shape=jax.ShapeDtypeStruct((n_seq, seq, RWKV_WIDTH), F32),
        scratch_shapes=[pltpu.VMEM((n_seq, N_RWKV_HEADS // SCAN_GROUP, RWKV_HEAD, SCAN_GROUP * RWKV_HEAD), F32)],
        compiler_params=_cparams("arbitrary"),
        name="rwkv_scan",
    )(*args)
    return y.reshape(n, RWKV_WIDTH)


def _rwkv_post_kernel(y_ref, r_ref, k_ref, v_ref, g_ref, pv_ref, ones_ref, o_ref):
    ones = ones_ref[...]
    ln_w, ln_b, r_k = (pv_ref[i:i + 1, :] for i in range(3))
    y = y_ref[...]
    inv_n = 1.0 / RWKV_HEAD
    d = y - _head_sums(y, ones, 2) * inv_n
    var = _head_sums(d * d, ones, 2) * inv_n
    yn = d * lax.rsqrt(var + RWKV_GN_EPS) * ln_w + ln_b
    bonus = _head_sums(r_ref[...] * k_ref[...] * r_k, ones, 2) * v_ref[...]
    o_ref[...] = ((yn + bonus) * g_ref[...]).astype(BF16)


def _rwkv_post(y, r, k, v, g, pvec, tt=512):
    n = y.shape[0]
    row = lambda i: (i, 0)
    const = lambda i: (0, 0)
    return pl.pallas_call(
        _rwkv_post_kernel,
        grid=(n // tt,),
        in_specs=[pl.BlockSpec((tt, RWKV_WIDTH), row)] * 5 + [
            pl.BlockSpec((SUBLANES, RWKV_WIDTH), const),
            pl.BlockSpec((MXU_COLS, MXU_COLS), const),
        ],
        out_specs=pl.BlockSpec((tt, RWKV_WIDTH), row),
        out_shape=jax.ShapeDtypeStruct((n, RWKV_WIDTH), BF16),
        compiler_params=_cparams("parallel"),
        name="rwkv_post",
    )(y, r, k, v, g, pvec, _head_ones())


def _merge_kernel(a_ref, y_ref, wa_ref, wb_ref, ga_ref, gb_ref, o_ref):
    for c in range(o_ref.shape[1] // MXU_COLS):
        cs = slice(c * MXU_COLS, (c + 1) * MXU_COLS)
        pa = jnp.dot(a_ref[...], wa_ref[:, cs].astype(BF16), preferred_element_type=F32)
        pb = jnp.dot(y_ref[...], wb_ref[:, cs].astype(BF16), preferred_element_type=F32)
        ga = jax.nn.sigmoid(ga_ref[:, cs].astype(F32))
        gb = jax.nn.sigmoid(gb_ref[:, cs].astype(F32))
        o_ref[:, cs] = (ga * pa + gb * pb).astype(BF16)


def _merge(layer, attn, y, wa, wb, gates, tm=1024, tn=1024):
    n = attn.shape[0]
    nj = D_MODEL // tn
    return pl.pallas_call(
        _merge_kernel,
        grid=(n // tm, nj),
        in_specs=[
            pl.BlockSpec((tm, ATTN_WIDTH), lambda i, j: (i, 0)),
            pl.BlockSpec((tm, RWKV_WIDTH), lambda i, j: (i, 0)),
            pl.BlockSpec((None, ATTN_WIDTH, tn), lambda i, j: (layer, 0, j)),
            pl.BlockSpec((None, RWKV_WIDTH, tn), lambda i, j: (layer, 0, j)),
            pl.BlockSpec((tm, tn), lambda i, j: (i, j)),
            pl.BlockSpec((tm, tn), lambda i, j: (i, j + nj)),
        ],
        out_specs=pl.BlockSpec((tm, tn), lambda i, j: (i, j)),
        out_shape=jax.ShapeDtypeStruct((n, D_MODEL), BF16),
        compiler_params=_cparams("parallel", "arbitrary"),
        name="gated_merge",
    )(attn, y, wa, wb, gates, gates)


def _outproj_router_kernel(m_ref, wo_ref, x_ref, nw_ref, wr_ref, br_ref,
                           xo_ref, hf_ref, slab_ref, cnt_ref, run_scr):
    @pl.when(pl.program_id(0) == 0)
    def _():
        run_scr[...] = jnp.zeros_like(run_scr)

    tm = x_ref.shape[0]
    sq = jnp.zeros((tm, LANES), F32)
    for c in range(D_MODEL // MXU_COLS):
        cs = slice(c * MXU_COLS, (c + 1) * MXU_COLS)
        xn_c = x_ref[:, cs] + jnp.dot(m_ref[...], wo_ref[:, cs], preferred_element_type=F32)
        xo_ref[:, cs] = xn_c
        for l in range(MXU_COLS // LANES):
            piece = xn_c[:, l * LANES:(l + 1) * LANES]
            sq = sq + piece * piece
    ms = jnp.sum(sq, axis=-1, keepdims=True) * (1.0 / D_MODEL)
    hf = xo_ref[...] * lax.rsqrt(ms + NORM_EPS) * nw_ref[...]
    _store_rows(hf_ref, hf)

    h_hi = hf.astype(BF16)
    h_lo = (hf - h_hi.astype(F32)).astype(BF16)
    wr = wr_ref[...]
    w_hi = wr.astype(BF16)
    w_lo = (wr - w_hi.astype(F32)).astype(BF16)
    lg = (jnp.dot(h_hi, w_hi, preferred_element_type=F32)
          + jnp.dot(h_hi, w_lo, preferred_element_type=F32)
          + jnp.dot(h_lo, w_hi, preferred_element_type=F32)) + br_ref[...]

    chunks = range(tm // ROUTER_ROWS)
    lgs = [lg[c * ROUTER_ROWS:(c + 1) * ROUTER_ROWS] for c in chunks]
    lane = lax.broadcasted_iota(jnp.int32, (ROUTER_ROWS, ROUTER_LANES), 1).astype(F32)
    neg = -jnp.inf
    big = float(ROUTER_LANES)
    is_group = lane < N_GROUPS

    def row_max(vals):
        return [jnp.max(v, axis=-1, keepdims=True) for v in vals]

    def first_index(vals, mx):
        return [jnp.min(jnp.where(v == m, lane, big), axis=-1, keepdims=True) for v, m in zip(vals, mx)]

    def row_sum(vals):
        return [jnp.sum(v, axis=-1, keepdims=True) for v in vals]

    gl = [jnp.where(is_group, v, neg) for v in lgs]
    gmax = row_max(gl)
    gsel = first_index(gl, gmax)
    gsum = row_sum([jnp.where(is_group, jnp.exp(v - m), 0.0) for v, m in zip(lgs, gmax)])
    lo_lane = [EXPERT_LANE0 + EXPERTS_PER_GROUP * g for g in gsel]
    el = [jnp.where((lane >= lo) & (lane < lo + EXPERTS_PER_GROUP), v, neg) for v, lo in zip(lgs, lo_lane)]
    v1 = row_max(el)
    i1 = first_index(el, v1)
    el2 = [jnp.where(lane == i, neg, v) for v, i in zip(el, i1)]
    v2 = row_max(el2)
    i2 = first_index(el2, v2)
    oh1 = [lane == i for i in i1]
    oh2 = [lane == i for i in i2]

    cnt = jnp.concatenate([a.astype(F32) + b.astype(F32) for a, b in zip(oh1, oh2)], axis=0)
    r_i = lax.broadcasted_iota(jnp.int32, (tm, tm), 0)
    c_i = lax.broadcasted_iota(jnp.int32, (tm, tm), 1)
    before = jnp.dot((r_i > c_i).astype(BF16), cnt.astype(BF16), preferred_element_type=F32)
    tot = before + run_scr[0:1, :]
    tots = [tot[c * ROUTER_ROWS:(c + 1) * ROUTER_ROWS] for c in chunks]
    rank1 = row_sum([jnp.where(o, t, 0.0) for o, t in zip(oh1, tots)])
    rank2 = row_sum([jnp.where(o, t, 0.0) for o, t in zip(oh2, tots)])
    run = run_scr[0:1, :] + jnp.sum(cnt, axis=0, keepdims=True)
    run_scr[...] = jnp.broadcast_to(run, run_scr.shape)
    cnt_ref[...] = jnp.broadcast_to(run, cnt_ref.shape)

    for c in chunks:
        gp = 1.0 / gsum[c]
        e21 = jnp.exp(v2[c] - v1[c])
        ew1 = gp / (1.0 + e21)
        ew2 = gp * e21 / (1.0 + e21)
        slab = jnp.where(lane == 0, i1[c] - EXPERT_LANE0,
               jnp.where(lane == 1, i2[c] - EXPERT_LANE0,
               jnp.where(lane == 2, ew1,
               jnp.where(lane == 3, ew2,
               jnp.where(lane == 4, rank1[c],
               jnp.where(lane == 5, rank2[c], 0.0))))))
        slab_ref[c * ROUTER_ROWS:(c + 1) * ROUTER_ROWS, :] = slab


def _outproj_router(merged, w_out, x, norm_w, w_router, b_router, tm=512):
    n = x.shape[0]
    row = lambda i: (i, 0)
    const = lambda i: (0, 0)
    return pl.pallas_call(
        _outproj_router_kernel,
        grid=(n // tm,),
        in_specs=[
            pl.BlockSpec((tm, D_MODEL), row),
            pl.BlockSpec((D_MODEL, D_MODEL), const),
            pl.BlockSpec((tm, D_MODEL), row),
            pl.BlockSpec((1, D_MODEL), const),
            pl.BlockSpec((D_MODEL, ROUTER_LANES), const),
            pl.BlockSpec((1, ROUTER_LANES), const),
        ],
        out_specs=[
            pl.BlockSpec((tm, D_MODEL), row),
            pl.BlockSpec((tm * ROW_SUB, LANES), row),
            pl.BlockSpec((tm, ROUTER_LANES), row),
            pl.BlockSpec((SUBLANES, ROUTER_LANES), const),
        ],
        out_shape=[
            jax.ShapeDtypeStruct((n, D_MODEL), F32),
            jax.ShapeDtypeStruct((n * ROW_SUB, LANES), F32),
            jax.ShapeDtypeStruct((n, ROUTER_LANES), F32),
            jax.ShapeDtypeStruct((SUBLANES, ROUTER_LANES), F32),
        ],
        scratch_shapes=[pltpu.VMEM((SUBLANES, ROUTER_LANES), F32)],
        compiler_params=_cparams("arbitrary"),
        name="outproj_router",
    )(merged, w_out, x, norm_w, w_router, b_router)


def _dispatch_kernel(slot_ref, hf_ref, xb_in_hbm, xb_hbm, sem, *, tm):
    del xb_in_hbm

    def copy(u, j):
        return pltpu.make_async_copy(_slab(hf_ref, u), _slab(xb_hbm, slot_ref[TOP_K * u + j]), sem)

    def start(u, c):
        for j in range(TOP_K):
            copy(u, j).start(priority=j)
        return c

    def wait(u, c):
        for j in range(TOP_K):
            copy(u, j).wait()
        return c

    lax.fori_loop(0, tm, start, 0, unroll=8)
    lax.fori_loop(0, tm, wait, 0, unroll=8)


def _dispatch(slot_flat, hf, cap, tm=256):
    n = hf.shape[0] // ROW_SUB
    xb0 = jnp.zeros((cap * ROW_SUB, LANES), F32)
    return pl.pallas_call(
        functools.partial(_dispatch_kernel, tm=tm),
        grid=(n // tm,),
        in_specs=[
            pl.BlockSpec((TOP_K * tm,), lambda i: (i,), memory_space=pltpu.SMEM),
            pl.BlockSpec((tm * ROW_SUB, LANES), lambda i: (i, 0)),
            pl.BlockSpec(memory_space=pl.ANY),
        ],
        out_specs=pl.BlockSpec(memory_space=pl.ANY),
        out_shape=jax.ShapeDtypeStruct((cap * ROW_SUB, LANES), F32),
        scratch_shapes=[pltpu.SemaphoreType.DMA(())],
        input_output_aliases={2: 0},
        compiler_params=_cparams("arbitrary"),
        name="moe_dispatch",
    )(slot_flat, hf, xb0)


def _ffn_kernel(be_ref, nu_ref, par_ref, nxt_ref, x_ref, wg_hbm, wu_hbm, wd_hbm, o_ref,
                wg_f, wu_f, wd_f, wg_s, wu_s, wd_s, sems, *, layer):
    b = pl.program_id(0)
    used = b < nu_ref[0]
    prev_e = be_ref[jnp.maximum(b - 1, 0)]
    new_expert = jnp.logical_or(b == 0, be_ref[b] != prev_e)

    def fetch(e, p):
        return [pltpu.make_async_copy(src.at[layer, e], dst.at[p], sems.at[p, i])
                for i, (src, dst) in enumerate(((wg_hbm, wg_f), (wu_hbm, wu_f), (wd_hbm, wd_f)))]

    @pl.when(jnp.logical_and(used, new_expert))
    def _():
        for p in range(2):
            @pl.when(par_ref[b] == p)
            def _():
                @pl.when(b == 0)
                def _():
                    for cp in fetch(be_ref[0], p):
                        cp.start()

                @pl.when(nxt_ref[b] >= 0)
                def _():
                    for cp in fetch(nxt_ref[b], 1 - p):
                        cp.start()

                for cp in fetch(be_ref[b], p):
                    cp.wait()
                wg_s[...] = wg_f[p].astype(BF16)
                wu_s[...] = wu_f[p].astype(BF16)
                wd_s[...] = wd_f[p].astype(BF16)

    @pl.when(used)
    def _():
        x = _rows_to_2d(x_ref, MOE_ROWS).astype(BF16)
        gt = jnp.dot(x, wg_s[...], preferred_element_type=F32)
        up = jnp.dot(x, wu_s[...], preferred_element_type=F32)
        hid = (gt * jax.nn.sigmoid(gt) * up).astype(BF16)
        _store_rows(o_ref, jnp.dot(hid, wd_s[...], preferred_element_type=F32))

    @pl.when(jnp.logical_not(used))
    def _():
        o_ref[...] = jnp.zeros_like(o_ref)


def _expert_ffn(layer, plan, xb, w_gate, w_up, w_down):
    nblk = xb.shape[0] // (MOE_ROWS * ROW_SUB)
    hbm = pl.BlockSpec(memory_space=pl.ANY)
    grid_spec = pltpu.PrefetchScalarGridSpec(
        num_scalar_prefetch=4,
        grid=(nblk,),
        in_specs=[
            pl.BlockSpec((MOE_ROWS * ROW_SUB, LANES), lambda b, be, nu, par, nxt: (jnp.minimum(b, nu[0] - 1), 0)),
            hbm, hbm, hbm,
        ],
        out_specs=pl.BlockSpec((MOE_ROWS * ROW_SUB, LANES), lambda b, be, nu, par, nxt: (b, 0)),
        scratch_shapes=[
            pltpu.VMEM((2, D_MODEL, EXPERT_HIDDEN), F32),
            pltpu.VMEM((2, D_MODEL, EXPERT_HIDDEN), F32),
            pltpu.VMEM((2, EXPERT_HIDDEN, D_MODEL), F32),
            pltpu.VMEM((D_MODEL, EXPERT_HIDDEN), BF16),
            pltpu.VMEM((D_MODEL, EXPERT_HIDDEN), BF16),
            pltpu.VMEM((EXPERT_HIDDEN, D_MODEL), BF16),
            pltpu.SemaphoreType.DMA((2, 3)),
        ],
    )
    return pl.pallas_call(
        functools.partial(_ffn_kernel, layer=layer),
        grid_spec=grid_spec,
        out_shape=jax.ShapeDtypeStruct(xb.shape, F32),
        compiler_params=_cparams("arbitrary"),
        name="expert_ffn",
    )(plan.block_expert, plan.n_used, plan.parity, plan.next_expert, xb, w_gate, w_up, w_down)


def _combine_kernel(slot_ref, slot_next_ref, x_ref, slab_ref, fw_ref, yb_hbm, o_ref,
                    buf00, buf01, buf10, buf11, sems, *, tm, n_tiles, final_norm):
    i = pl.program_id(0)
    bufs = ((buf00, buf01), (buf10, buf11))

    def copies(slots, p, u):
        return [pltpu.make_async_copy(_slab(yb_hbm, slots[TOP_K * u + j]), _slab(bufs[p][j], u), sems.at[p])
                for j in range(TOP_K)]

    def start_all(slots, p):
        def body(u, c):
            for j, cp in enumerate(copies(slots, p, u)):
                cp.start(priority=j)
            return c
        lax.fori_loop(0, tm, body, 0, unroll=8)

    def wait_all(p):
        def body(u, c):
            for cp in copies(slot_ref, p, u):
                cp.wait()
            return c
        lax.fori_loop(0, tm, body, 0, unroll=8)

    @pl.when(i == 0)
    def _():
        start_all(slot_ref, 0)

    for p in range(2):
        @pl.when(i % 2 == p)
        def _():
            @pl.when(i + 1 < n_tiles)
            def _():
                start_all(slot_next_ref, 1 - p)

            wait_all(p)
            slab = slab_ref[...]
            w1 = slab[:, 2:3]
            w2 = slab[:, 3:4]
            out = x_ref[...] + (_rows_to_2d(bufs[p][0], tm) * w1 + _rows_to_2d(bufs[p][1], tm) * w2)
            if final_norm:
                out = _rms(out, fw_ref[...])
            o_ref[...] = out


def _combine(slot_flat, x, slab, final_w, yb, final_norm, tm=256):
    n = x.shape[0]
    n_tiles = n // tm
    row = lambda i: (i, 0)
    row_buf = pltpu.VMEM((tm * ROW_SUB, LANES), F32)
    return pl.pallas_call(
        functools.partial(_combine_kernel, tm=tm, n_tiles=n_tiles, final_norm=final_norm),
        grid=(n_tiles,),
        in_specs=[
            pl.BlockSpec((TOP_K * tm,), lambda i: (i,), memory_space=pltpu.SMEM),
            pl.BlockSpec((TOP_K * tm,), lambda i: (jnp.minimum(i + 1, n_tiles - 1),), memory_space=pltpu.SMEM),
            pl.BlockSpec((tm, D_MODEL), row),
            pl.BlockSpec((tm, ROUTER_LANES), row),
            pl.BlockSpec((1, D_MODEL), lambda i: (0, 0)),
            pl.BlockSpec(memory_space=pl.ANY),
        ],
        out_specs=pl.BlockSpec((tm, D_MODEL), row),
        out_shape=jax.ShapeDtypeStruct((n, D_MODEL), F32),
        scratch_shapes=[row_buf, row_buf, row_buf, row_buf, pltpu.SemaphoreType.DMA((2,))],
        compiler_params=_cparams("arbitrary"),
        name="moe_combine",
    )(slot_flat, slot_flat, x, slab, final_w, yb)


def _pad_rows(w, rows, offset=0):
    out = jnp.zeros((rows, w.shape[1]), BF16)
    return out.at[offset:offset + w.shape[0]].set(w.astype(BF16))


def _pad_cols(vec, cols, offset=0):
    out = jnp.zeros((cols,), F32)
    return out.at[offset:offset + vec.shape[0]].set(vec.astype(F32))


class MoePlan(NamedTuple):
    slot: jax.Array
    block_expert: jax.Array
    n_used: jax.Array
    parity: jax.Array
    next_expert: jax.Array
    cap: int


def _moe_plan(slab, counts_row, n_tokens):
    eid = slab[:, 0:TOP_K].astype(jnp.int32)
    rank = slab[:, 4:4 + TOP_K].astype(jnp.int32)
    counts = counts_row[EXPERT_LANE0:EXPERT_LANE0 + N_EXPERTS].astype(jnp.int32)
    padded = (counts + MOE_ROWS - 1) // MOE_ROWS * MOE_ROWS
    pad_ends = jnp.cumsum(padded)
    pad_starts = pad_ends - padded
    slot = pad_starts[eid.reshape(-1)] + rank.reshape(-1)
    nblk = n_tokens * TOP_K // MOE_ROWS + N_EXPERTS
    block_start = jnp.arange(nblk, dtype=jnp.int32) * MOE_ROWS
    block_expert = jnp.minimum(
        jnp.sum((pad_ends[None, :] <= block_start[:, None]).astype(jnp.int32), axis=1),
        N_EXPERTS - 1).astype(jnp.int32)
    n_used = (pad_ends[-1:] // MOE_ROWS).astype(jnp.int32)
    blk = jnp.arange(nblk, dtype=jnp.int32)
    prev_expert = jnp.concatenate([block_expert[:1], block_expert[:-1]])
    first = (blk < n_used[0]) & ((blk == 0) | (block_expert != prev_expert))
    parity = ((jnp.cumsum(first.astype(jnp.int32)) - 1) % 2).astype(jnp.int32)
    later_first = first[None, :] & (blk[None, :] > blk[:, None])
    next_first = jnp.min(jnp.where(later_first, blk[None, :], nblk), axis=1)
    next_expert = jnp.where(next_first < nblk, block_expert[jnp.minimum(next_first, nblk - 1)], -1)
    return MoePlan(slot, block_expert, n_used, parity, next_expert.astype(jnp.int32), nblk * MOE_ROWS)


def kernel(x, attn_norm_w, w_in, shift_mix, attn_sinks, rwkv_w0, rwkv_w_up, rwkv_a0, rwkv_a_up, rwkv_g_up, rwkv_k_k, rwkv_k_a, rwkv_r_k, rwkv_ln_w, rwkv_ln_b, vres_down, vres_mix, vres_up, vres_v0, w_branch_a, w_branch_b, w_out, ffn_norm_w, router_group_w, router_group_b, router_expert_w, router_expert_b, expert_w_gate, expert_w_up, expert_w_down, final_norm_w):
    batch, seq, _ = x.shape
    n = batch * seq
    depth = w_in.shape[0]
    xf = x.reshape(n, D_MODEL)
    w_in_bf = w_in.astype(BF16)
    v_first = None
    for i in range(depth):
        has_vres = i > 0
        w_gates = w_in_bf[i, :, GATE_COL0:]
        mix_r = shift_mix[i, :RKV_COLS].reshape(1, RKV_COLS)
        mix_l = _pad_cols(shift_mix[i, RKV_COLS:], LORA_COLS).reshape(1, LORA_COLS)
        v0 = vres_v0[i - 1] if has_vres else jnp.zeros((RWKV_WIDTH,), F32)
        zero = jnp.zeros((RWKV_WIDTH,), F32)
        pv_pre = jnp.stack([rwkv_w0[i], rwkv_a0[i], rwkv_k_k[i], rwkv_k_a[i], v0, zero, zero, zero]).astype(F32)
        pv_post = jnp.stack([rwkv_ln_w[i], rwkv_ln_b[i], rwkv_r_k[i].reshape(-1),
                             zero, zero, zero, zero, zero]).astype(F32)
        wup = _pad_rows(rwkv_w_up[i], LANES, 0)
        aup = _pad_rows(rwkv_a_up[i], LANES, DECAY_LORA)
        gup = _pad_rows(rwkv_g_up[i], 2 * LANES, 0)
        if has_vres:
            vd = _pad_rows(vres_down[i - 1].T, LANES, 0).T
            mix_v = _pad_cols(vres_mix[i - 1], LANES).reshape(1, LANES)
            vup = _pad_rows(vres_up[i - 1], LANES, 0)
        else:
            vd, mix_v = None, None
            vup = jnp.zeros((LANES, RWKV_WIDTH), BF16)
        w_router = jnp.concatenate(
            [router_group_w[i], router_expert_w[i],
             jnp.zeros((D_MODEL, ROUTER_LANES - N_GROUPS - N_EXPERTS), F32)], axis=1)
        b_router = _pad_cols(jnp.concatenate([router_group_b[i], router_expert_b[i]]),
                             ROUTER_LANES).reshape(1, ROUTER_LANES)

        proj = _inproj(i, xf, attn_norm_w[i].reshape(1, D_MODEL), w_in_bf, w_gates, vd)
        qkv, rkv, lora, gates = proj[:4]
        zv = proj[4] if has_vres else None
        attn = _attention(qkv, attn_sinks[i], seq)
        r, lw, k2, v, al, be, g = _rwkv_pre(rkv, lora, v_first, zv, mix_r, mix_l, mix_v, pv_pre,
                                            wup, aup, gup, vup, seq)
        if not has_vres:
            v_first = v
        y_raw = _rwkv_scan(r, lw, k2, v, al, be, seq)
        y = _rwkv_post(y_raw, r, k2, v, g, pv_post)
        merged = _merge(i, attn, y, w_branch_a, w_branch_b, gates)

        x_mid, hf, slab, counts = _outproj_router(
            merged, w_out[i].astype(BF16), xf, ffn_norm_w[i].reshape(1, D_MODEL), w_router, b_router)
        plan = _moe_plan(slab, counts[0], n)
        xb = _dispatch(plan.slot, hf, plan.cap)
        yb = _expert_ffn(i, plan, xb, expert_w_gate, expert_w_up, expert_w_down)
        xf = _combine(plan.slot, x_mid, slab, final_norm_w.reshape(1, D_MODEL), yb, final_norm=(i == depth - 1))
    return xf.reshape(batch, seq, D_MODEL)
```

```python
import functools
from typing import NamedTuple

import jax
import jax.numpy as jnp
from jax import lax
from jax.experimental import pallas as pl
from jax.experimental.pallas import tpu as pltpu

F32 = jnp.float32
BF16 = jnp.bfloat16

D_MODEL = 2048
HEAD_DIM = 64
N_Q_HEADS = 16
N_KV_HEADS = 4
GQA_GROUP = N_Q_HEADS // N_KV_HEADS
ATTN_WIDTH = N_Q_HEADS * HEAD_DIM
KV_WIDTH = N_KV_HEADS * HEAD_DIM
WINDOW = 128
ATTN_BLOCK = 128
ATTN_SUB = 2

RWKV_HEAD = 64
N_RWKV_HEADS = 16
RWKV_WIDTH = N_RWKV_HEADS * RWKV_HEAD
DECAY_LORA = 64
AAA_LORA = 64
MV_LORA = 32
GATE_LORA = 160
RWKV_GN_EPS = 64e-5
LORA_WIDTH = DECAY_LORA + AAA_LORA + GATE_LORA

N_GROUPS = 4
EXPERTS_PER_GROUP = 8
N_EXPERTS = N_GROUPS * EXPERTS_PER_GROUP
TOP_K = 2
EXPERT_HIDDEN = D_MODEL // 4
NORM_EPS = 1e-5

LANES = 128
SUBLANES = 8
MXU_COLS = 256

SEG = 512
QKV_COLS = ATTN_WIDTH + 2 * KV_WIDTH
RKV_COLS = 3 * RWKV_WIDTH
LORA_COLS = SEG
GATE_COLS = 2 * D_MODEL
QKV_TILES = QKV_COLS // SEG
RKV_TILES = RKV_COLS // SEG
LORA_TILES = LORA_COLS // SEG
GATE_TILES = GATE_COLS // SEG
IN_TILES = QKV_TILES + RKV_TILES + LORA_TILES + GATE_TILES
FIRST_GATE_TILE = QKV_TILES + RKV_TILES + LORA_TILES
GATE_COL0 = QKV_COLS + RKV_COLS + LORA_WIDTH

SCAN_CHUNK = 64
SCAN_SUB = 2
SCAN_GROUP = 4
MOE_ROWS = 256
ROUTER_LANES = LANES
ROUTER_ROWS = 128
EXPERT_LANE0 = N_GROUPS

ROW_SUB = D_MODEL // LANES

VMEM_LIMIT = 56 * 1024 * 1024


def _rows_to_2d(slab_ref, rows):
    return jnp.concatenate([slab_ref[pl.ds(s, rows, stride=ROW_SUB), :] for s in range(ROW_SUB)], axis=1)


def _store_rows(slab_ref, val):
    rows = val.shape[0]
    for s in range(ROW_SUB):
        slab_ref[pl.ds(s, rows, stride=ROW_SUB), :] = val[:, s * LANES:(s + 1) * LANES]


def _slab(ref, row):
    return ref.at[pl.ds(pl.multiple_of(row * ROW_SUB, ROW_SUB), ROW_SUB)]


def _cparams(*sem):
    return pltpu.CompilerParams(dimension_semantics=sem, vmem_limit_bytes=VMEM_LIMIT)


def _bdot(a, b):
    return jnp.dot(a.astype(BF16), b.astype(BF16), preferred_element_type=F32)


def _bdot_nt(a, b):
    return lax.dot_general(a.astype(BF16), b.astype(BF16), (((1,), (1,)), ((), ())),
                           preferred_element_type=F32)


def _bdot_tn(a, b):
    return lax.dot_general(a.astype(BF16), b.astype(BF16), (((0,), (0,)), ((), ())),
                           preferred_element_type=F32)


def _dot_split_rhs(w01, x, passes):
    acc, rem = None, x
    for p in range(passes):
        part = rem.astype(BF16)
        d = jnp.dot(w01, part, preferred_element_type=F32)
        acc = d if acc is None else acc + d
        if p + 1 < passes:
            rem = rem - part.astype(F32)
    return acc


def _dot_split_lhs(x, w01, passes):
    acc, rem = None, x
    for p in range(passes):
        part = rem.astype(BF16)
        d = jnp.dot(part, w01, preferred_element_type=F32)
        acc = d if acc is None else acc + d
        if p + 1 < passes:
            rem = rem - part.astype(F32)
    return acc


def _rms(x, w):
    ms = jnp.mean(x * x, axis=-1, keepdims=True)
    return x * lax.rsqrt(ms + NORM_EPS) * w


def _inproj_kernel(*refs, has_vres):
    refs = list(refs)
    x_ref, nw_ref, w_ref, wg_ref = refs[:4]
    refs = refs[4:]
    vd_ref = refs.pop(0) if has_vres else None
    qkv_ref, rkv_ref, lora_ref, gate_ref = refs[:4]
    refs = refs[4:]
    zv_ref = refs.pop(0) if has_vres else None
    (h_scr,) = refs
    j = pl.program_id(1)

    @pl.when(j == 0)
    def _():
        h_scr[...] = _rms(x_ref[...], nw_ref[...]).astype(BF16)
        if has_vres:
            zv_ref[...] = jnp.dot(h_scr[...], vd_ref[...], preferred_element_type=F32)

    def project(wt_ref, out_ref):
        for c in range(SEG // MXU_COLS):
            cs = slice(c * MXU_COLS, (c + 1) * MXU_COLS)
            acc = lax.dot_general(h_scr[...], wt_ref[cs, :], (((1,), (1,)), ((), ())),
                                  preferred_element_type=F32)
            out_ref[:, cs] = acc.astype(out_ref.dtype)

    @pl.when(j < QKV_TILES)
    def _():
        project(w_ref, qkv_ref)

    @pl.when((j >= QKV_TILES) & (j < QKV_TILES + RKV_TILES))
    def _():
        project(w_ref, rkv_ref)

    @pl.when(j == QKV_TILES + RKV_TILES)
    def _():
        project(w_ref, lora_ref)

    @pl.when(j >= FIRST_GATE_TILE)
    def _():
        project(wg_ref, gate_ref)


def _inproj(layer, x, norm_w, w_in, w_gates, vd, tm=1024):
    n = x.shape[0]
    has_vres = vd is not None
    s_rkv = QKV_TILES
    in_specs = [
        pl.BlockSpec((tm, D_MODEL), lambda i, j: (i, 0)),
        pl.BlockSpec((1, D_MODEL), lambda i, j: (0, 0)),
        pl.BlockSpec((None, SEG, D_MODEL), lambda i, j: (layer, jnp.minimum(j, FIRST_GATE_TILE - 1), 0)),
        pl.BlockSpec((SEG, D_MODEL), lambda i, j: (jnp.clip(j - FIRST_GATE_TILE, 0, GATE_TILES - 1), 0)),
    ]
    args = [x, norm_w, w_in, w_gates]
    out_specs = [
        pl.BlockSpec((tm, SEG), lambda i, j: (i, jnp.clip(j, 0, QKV_TILES - 1))),
        pl.BlockSpec((tm, SEG), lambda i, j: (i, jnp.clip(j - s_rkv, 0, RKV_TILES - 1))),
        pl.BlockSpec((tm, SEG), lambda i, j: (i, 0)),
        pl.BlockSpec((tm, SEG), lambda i, j: (i, jnp.clip(j - FIRST_GATE_TILE, 0, GATE_TILES - 1))),
    ]
    out_shape = [
        jax.ShapeDtypeStruct((n, QKV_COLS), BF16),
        jax.ShapeDtypeStruct((n, RKV_COLS), F32),
        jax.ShapeDtypeStruct((n, LORA_COLS), F32),
        jax.ShapeDtypeStruct((n, GATE_COLS), BF16),
    ]
    if has_vres:
        in_specs.append(pl.BlockSpec((D_MODEL, LANES), lambda i, j: (0, 0)))
        args.append(vd)
        out_specs.append(pl.BlockSpec((tm, LANES), lambda i, j: (i, 0)))
        out_shape.append(jax.ShapeDtypeStruct((n, LANES), F32))
    return pl.pallas_call(
        functools.partial(_inproj_kernel, has_vres=has_vres),
        grid=(n // tm, IN_TILES),
        in_specs=in_specs,
        out_specs=out_specs,
        out_shape=out_shape,
        scratch_shapes=[pltpu.VMEM((tm, D_MODEL), BF16)],
        compiler_params=_cparams("parallel", "arbitrary"),
        name="inproj",
    )(*args)


def _attn_kernel(sink_ref, q_ref, kp_ref, kc_ref, vp_ref, vc_ref, bias_ref, o_ref, *, tiles_per_seq):
    first = (pl.program_id(0) % tiles_per_seq) == 0
    col = lax.broadcasted_iota(jnp.int32, (ATTN_BLOCK, 2 * ATTN_BLOCK), 1)
    pad_keys = jnp.logical_and(first, col < ATTN_BLOCK)
    scale = HEAD_DIM ** -0.5
    items = [(sb, hk) for sb in range(ATTN_SUB) for hk in range(N_KV_HEADS)]

    def window(prev_ref, cur_ref, sb, hk):
        ks = slice(hk * HEAD_DIM, (hk + 1) * HEAD_DIM)
        rows = slice(sb * ATTN_BLOCK, (sb + 1) * ATTN_BLOCK)
        prev = prev_ref[:, ks] if sb == 0 else cur_ref[(sb - 1) * ATTN_BLOCK:sb * ATTN_BLOCK, ks]
        return jnp.concatenate([prev, cur_ref[rows, ks]], axis=0)

    scores = {}
    for sb, hk in items:
        kw = window(kp_ref, kc_ref, sb, hk)
        for g in range(GQA_GROUP):
            h = hk * GQA_GROUP + g
            qh = q_ref[sb * ATTN_BLOCK:(sb + 1) * ATTN_BLOCK, h * HEAD_DIM:(h + 1) * HEAD_DIM]
            s = lax.dot_general(qh, kw, (((1,), (1,)), ((), ())), preferred_element_type=F32)
            s = s * scale + bias_ref[h]
            if sb == 0:
                s = jnp.where(pad_keys, -jnp.inf, s)
            scores[(sb, h)] = s
    row_max = {key: jnp.max(s, axis=-1, keepdims=True) for key, s in scores.items()}
    m = {key: jnp.maximum(row_max[key], sink_ref[key[1]]) for key in scores}
    p = {key: jnp.exp(scores[key] - m[key]) for key in scores}
    row_sum = {key: jnp.sum(p[key], axis=-1, keepdims=True) for key in scores}
    denoms = {key: row_sum[key] + jnp.exp(sink_ref[key[1]] - m[key]) for key in scores}
    probs = {key: p[key].astype(BF16) for key in scores}
    for sb, hk in items:
        vw = window(vp_ref, vc_ref, sb, hk)
        for g in range(GQA_GROUP):
            h = hk * GQA_GROUP + g
            o = jnp.dot(probs[(sb, h)], vw, preferred_element_type=F32) / denoms[(sb, h)]
            o_ref[sb * ATTN_BLOCK:(sb + 1) * ATTN_BLOCK, h * HEAD_DIM:(h + 1) * HEAD_DIM] = o.astype(BF16)


def _attn_bias():
    qi = jnp.arange(ATTN_BLOCK)[:, None]
    kj = jnp.arange(2 * ATTN_BLOCK)[None, :]
    dist = qi + ATTN_BLOCK - kj
    valid = (dist >= 0) & (dist < WINDOW)
    slopes = jnp.exp2(-8.0 * jnp.arange(1, N_Q_HEADS + 1, dtype=F32) / N_Q_HEADS)
    bias = -slopes[:, None, None] * dist.astype(F32)[None]
    return jnp.where(valid[None], bias, -jnp.inf)


def _attention(qkv, sinks, seq):
    n = qkv.shape[0]
    rows = ATTN_SUB * ATTN_BLOCK
    tiles_per_seq = seq // rows
    kcol = ATTN_WIDTH // KV_WIDTH
    prev = lambda i: jnp.maximum(i * ATTN_SUB - 1, 0)
    return pl.pallas_call(
        functools.partial(_attn_kernel, tiles_per_seq=tiles_per_seq),
        grid=(n // rows,),
        in_specs=[
            pl.BlockSpec(memory_space=pltpu.SMEM),
            pl.BlockSpec((rows, ATTN_WIDTH), lambda i: (i, 0)),
            pl.BlockSpec((ATTN_BLOCK, KV_WIDTH), lambda i: (prev(i), kcol)),
            pl.BlockSpec((rows, KV_WIDTH), lambda i: (i, kcol)),
            pl.BlockSpec((ATTN_BLOCK, KV_WIDTH), lambda i: (prev(i), kcol + 1)),
            pl.BlockSpec((rows, KV_WIDTH), lambda i: (i, kcol + 1)),
            pl.BlockSpec((N_Q_HEADS, ATTN_BLOCK, 2 * ATTN_BLOCK), lambda i: (0, 0, 0)),
        ],
        out_specs=pl.BlockSpec((rows, ATTN_WIDTH), lambda i: (i, 0)),
        out_shape=jax.ShapeDtypeStruct((n, ATTN_WIDTH), BF16),
        compiler_params=_cparams("parallel"),
        name="swa_attention",
    )(sinks.astype(F32), qkv, qkv, qkv, qkv, qkv, _attn_bias())


def _head_ones():
    ch = jnp.arange(MXU_COLS) // RWKV_HEAD
    return (ch[:, None] == ch[None, :]).astype(BF16)


def _head_sums(x, ones, passes):
    parts = [_dot_split_lhs(x[:, c:c + MXU_COLS], ones, passes) for c in range(0, RWKV_WIDTH, MXU_COLS)]
    return jnp.concatenate(parts, axis=1)


def _rwkv_pre_kernel(*refs, has_vres, tiles_per_seq):
    refs = list(refs)
    rkv_ref, rkvp_ref, lo_ref, lop_ref = refs[:4]
    refs = refs[4:]
    if has_vres:
        vf_ref, zv_ref, zvp_ref, mixv_ref = refs[:4]
        refs = refs[4:]
    (mixr_ref, mixl_ref, pv_ref, wup_ref, aup_ref, gup_ref, vup_ref, ones_ref,
     r_out, lw_out, k_out, v_out, al_out, be_out, g_out) = refs

    first = (pl.program_id(0) % tiles_per_seq) == 0

    def shifted(z, zp_ref):
        prev_row = jnp.where(first, 0.0, zp_ref[SUBLANES - 1:SUBLANES, :])
        row = lax.broadcasted_iota(jnp.int32, z.shape, 0)
        return jnp.where(row == 0, prev_row, pltpu.roll(z, 1, 0))

    z = rkv_ref[...]
    zs = z + (shifted(z, rkvp_ref) - z) * mixr_ref[...]
    lo = lo_ref[...]
    los = lo + (shifted(lo, lop_ref) - lo) * mixl_ref[...]

    r = zs[:, :RWKV_WIDTH]
    kr = zs[:, RWKV_WIDTH:2 * RWKV_WIDTH]
    vr = zs[:, 2 * RWKV_WIDTH:]

    wa = los[:, :LANES]
    col = lax.broadcasted_iota(jnp.int32, wa.shape, 1)
    wa_in = jnp.where(col < DECAY_LORA, jnp.tanh(wa), wa).astype(BF16)
    gd = los[:, LANES:3 * LANES]
    col_g = lax.broadcasted_iota(jnp.int32, gd.shape, 1)
    g_in = jnp.where(col_g < GATE_LORA, jax.nn.sigmoid(gd), gd).astype(BF16)
    dw = jnp.dot(wa_in, wup_ref[...], preferred_element_type=F32)
    da = jnp.dot(wa_in, aup_ref[...], preferred_element_type=F32)
    g = jnp.dot(g_in, gup_ref[...], preferred_element_type=F32)

    w0, a0, k_k, k_a, v0 = (pv_ref[i:i + 1, :] for i in range(5))
    u = -(w0 + dw)
    softplus = jnp.maximum(u, 0.0) + jnp.log1p(jnp.exp(-jnp.abs(u)))
    w_log = -softplus - 0.5
    a = jax.nn.sigmoid(a0 + da)
    if has_vres:
        zv = zv_ref[...]
        zvs = zv + (shifted(zv, zvp_ref) - zv) * mixv_ref[...]
        dv = jnp.dot(zvs.astype(BF16), vup_ref[...], preferred_element_type=F32)
        vr = vr + (vf_ref[...] - vr) * jax.nn.sigmoid(v0 + dv)

    kk0 = kr * k_k
    ss = _head_sums(kk0 * kk0, ones_ref[...], 2)
    kk = kk0 * (1.0 / jnp.maximum(jnp.sqrt(ss), 1e-12))

    r_out[...] = r
    lw_out[...] = -jnp.exp(w_log)
    k_out[...] = kr * (1.0 + (a - 1.0) * k_a)
    v_out[...] = vr
    al_out[...] = -kk
    be_out[...] = kk * a
    g_out[...] = g


def _rwkv_pre(rkv, lora, v_first, zv, mix_r, mix_l, mix_v, pvec, wup, aup, gup, vup, seq, tt=256):
    n = rkv.shape[0]
    has_vres = v_first is not None
    row = lambda i: (i, 0)
    prev8 = lambda i: (jnp.maximum(i * (tt // SUBLANES) - 1, 0), 0)
    const = lambda i: (0, 0)
    in_specs = [
        pl.BlockSpec((tt, RKV_COLS), row),
        pl.BlockSpec((SUBLANES, RKV_COLS), prev8),
        pl.BlockSpec((tt, LORA_COLS), row),
        pl.BlockSpec((SUBLANES, LORA_COLS), prev8),
    ]
    args = [rkv, rkv, lora, lora]
    if has_vres:
        in_specs += [
            pl.BlockSpec((tt, RWKV_WIDTH), row),
            pl.BlockSpec((tt, LANES), row),
            pl.BlockSpec((SUBLANES, LANES), prev8),
            pl.BlockSpec((1, LANES), const),
        ]
        args += [v_first, zv, zv, mix_v]
    in_specs += [
        pl.BlockSpec((1, RKV_COLS), const),
        pl.BlockSpec((1, LORA_COLS), const),
        pl.BlockSpec((SUBLANES, RWKV_WIDTH), const),
        pl.BlockSpec((LANES, RWKV_WIDTH), const),
        pl.BlockSpec((LANES, RWKV_WIDTH), const),
        pl.BlockSpec((2 * LANES, RWKV_WIDTH), const),
        pl.BlockSpec((LANES, RWKV_WIDTH), const),
        pl.BlockSpec((MXU_COLS, MXU_COLS), const),
    ]
    args += [mix_r, mix_l, pvec, wup, aup, gup, vup, _head_ones()]
    out = jax.ShapeDtypeStruct((n, RWKV_WIDTH), F32)
    return pl.pallas_call(
        functools.partial(_rwkv_pre_kernel, has_vres=has_vres, tiles_per_seq=seq // tt),
        grid=(n // tt,),
        in_specs=in_specs,
        out_specs=[pl.BlockSpec((tt, RWKV_WIDTH), row)] * 7,
        out_shape=[out] * 7,
        compiler_params=_cparams("parallel"),
        name="rwkv_pre",
    )(*args)


def _scan_kernel(r_ref, lw_ref, k_ref, v_ref, al_ref, be_ref, y_ref, s_scr):
    L = SCAN_CHUNK
    W = SCAN_GROUP * RWKV_HEAD
    n_seq = r_ref.shape[0]
    n_grp = N_RWKV_HEADS // SCAN_GROUP

    @pl.when(pl.program_id(0) == 0)
    def _():
        s_scr[...] = jnp.zeros_like(s_scr)

    row = lax.broadcasted_iota(jnp.int32, (L, L), 0)
    col = lax.broadcasted_iota(jnp.int32, (L, L), 1)
    tri = (row >= col).astype(BF16)

    t_w = lax.broadcasted_iota(jnp.int32, (L, W), 0)
    lane_w = lax.broadcasted_iota(jnp.int32, (L, W), 1)
    s_w = lane_w % RWKV_HEAD
    strict = t_w > s_w
    incl = t_w >= s_w
    eye = (t_w == s_w).astype(F32)
    blk_w = lane_w // RWKV_HEAD
    bd_mask = (lax.broadcasted_iota(jnp.int32, (W, W), 0) // RWKV_HEAD
               == lax.broadcasted_iota(jnp.int32, (W, W), 1) // RWKV_HEAD)

    def blockdiag(w):
        tiled = jnp.concatenate([w.astype(BF16)] * SCAN_GROUP, axis=0)
        return jnp.where(bd_mask, tiled, jnp.zeros_like(tiled))

    items = [(b, g) for b in range(n_seq) for g in range(n_grp)]
    for sub in range(SCAN_SUB):
        rows = slice(sub * L, (sub + 1) * L)
        ops = {}
        for b in range(n_seq):
            lw = lw_ref[b, rows, :]
            cum = _dot_split_rhs(tri, lw, 3)
            p_inc = jnp.exp(cum)
            p_exc = jnp.exp(cum - lw)
            p_inv = jnp.exp(-cum)
            p_last = p_inc[L - 1:L, :]
            a_t = al_ref[b, rows, :] * p_exc
            r_t = r_ref[b, rows, :] * p_inc
            b_t = be_ref[b, rows, :] * p_inv
            k_t = k_ref[b, rows, :] * p_inv
            b_end = b_t * p_last
            k_end = k_t * p_last
            v = v_ref[b, rows, :]
            for g in range(n_grp):
                gs = slice(g * W, (g + 1) * W)
                ops[(b, g)] = dict(
                    ar=jnp.concatenate([a_t[:, gs], r_t[:, gs]], axis=0),
                    bt=b_t[:, gs], kt=k_t[:, gs], v=v[:, gs], p_last=p_last[:, gs],
                    bke=jnp.concatenate([b_end[:, gs], k_end[:, gs]], axis=0))

        gm, xs, av, s0 = {}, {}, {}, {}
        for it in items:
            o = ops[it]
            rhs = jnp.concatenate([blockdiag(o['bt']), blockdiag(o['kt'])], axis=0)
            gm[it] = _bdot_nt(o['ar'], rhs)
            s0[it] = s_scr[it[0], it[1]]
            xs[it] = _bdot_nt(o['ar'], blockdiag(s0[it]))
        a_ab, a_rb = {}, {}
        for it in items:
            g = gm[it]
            a_ab[it] = jnp.where(strict, g[:L, :W], 0.0)
            a_rb[it] = jnp.where(incl, g[L:, :W], 0.0)
            a_kk = jnp.concatenate([jnp.where(strict, g[:L, W:], 0.0), jnp.where(incl, g[L:, W:], 0.0)], axis=0)
            av[it] = _bdot(a_kk, blockdiag(ops[it]['v']))

        pw = {it: _bdot(a_ab[it], blockdiag(a_ab[it])) for it in items}
        tinv = {it: eye + a_ab[it] for it in items}
        span = 2
        while 2 * span < L:
            both = {it: _bdot(jnp.concatenate([pw[it], tinv[it]], axis=0), blockdiag(pw[it])) for it in items}
            tinv = {it: tinv[it] + both[it][L:] for it in items}
            pw = {it: both[it][:L] for it in items}
            span *= 2
        tinv = {it: tinv[it] + _bdot(tinv[it], blockdiag(pw[it])) for it in items}

        u = {it: _bdot(tinv[it], blockdiag(xs[it][:L] + av[it][:L])) for it in items}
        for it in items:
            b, g = it
            y_ref[b, rows, g * W:(g + 1) * W] = xs[it][L:] + av[it][L:] + _bdot(a_rb[it], blockdiag(u[it]))
        for it in items:
            o = ops[it]
            full = _bdot_tn(jnp.concatenate([u[it], o['v']], axis=0), o['bke'])
            upd = full[:RWKV_HEAD]
            for h in range(1, SCAN_GROUP):
                upd = jnp.where(blk_w == h, full[h * RWKV_HEAD:(h + 1) * RWKV_HEAD], upd)
            s_scr[it[0], it[1]] = s0[it] * o['p_last'] + upd


def _rwkv_scan(r, lw, k, v, al, be, seq):
    n = r.shape[0]
    n_seq = n // seq
    nc = seq // (SCAN_SUB * SCAN_CHUNK)
    spec = pl.BlockSpec((n_seq, SCAN_SUB * SCAN_CHUNK, RWKV_WIDTH), lambda c: (0, c, 0))
    args = [a.reshape(n_seq, seq, RWKV_WIDTH) for a in (r, lw, k, v, al, be)]
    y = pl.pallas_call(
        _scan_kernel,
        grid=(nc,),
        in_specs=[spec] * 6,
        out_specs=spec,
        out_shape=jax.ShapeDtypeStruct((n_seq, seq, RWKV_WIDTH), F32),
        scratch_shapes=[pltpu.VMEM((n_seq, N_RWKV_HEADS // SCAN_GROUP, RWKV_HEAD, SCAN_GROUP * RWKV_HEAD), F32)],
        compiler_params=_cparams("arbitrary"),
        name="rwkv_scan",
    )(*args)
    return y.reshape(n, RWKV_WIDTH)


def _rwkv_post_kernel(y_ref, r_ref, k_ref, v_ref, g_ref, pv_ref, ones_ref, o_ref):
    ones = ones_ref[...]
    ln_w, ln_b, r_k = (pv_ref[i:i + 1, :] for i in range(3))
    y = y_ref[...]
    inv_n = 1.0 / RWKV_HEAD
    d = y - _head_sums(y, ones, 2) * inv_n
    var = _head_sums(d * d, ones, 2) * inv_n
    yn = d * lax.rsqrt(var + RWKV_GN_EPS) * ln_w + ln_b
    bonus = _head_sums(r_ref[...] * k_ref[...] * r_k, ones, 2) * v_ref[...]
    o_ref[...] = ((yn + bonus) * g_ref[...]).astype(BF16)


def _rwkv_post(y, r, k, v, g, pvec, tt=512):
    n = y.shape[0]
    row = lambda i: (i, 0)
    const = lambda i: (0, 0)
    return pl.pallas_call(
        _rwkv_post_kernel,
        grid=(n // tt,),
        in_specs=[pl.BlockSpec((tt, RWKV_WIDTH), row)] * 5 + [
            pl.BlockSpec((SUBLANES, RWKV_WIDTH), const),
            pl.BlockSpec((MXU_COLS, MXU_COLS), const),
        ],
        out_specs=pl.BlockSpec((tt, RWKV_WIDTH), row),
        out_shape=jax.ShapeDtypeStruct((n, RWKV_WIDTH), BF16),
        compiler_params=_cparams("parallel"),
        name="rwkv_post",
    )(y, r, k, v, g, pvec, _head_ones())


def _merge_kernel(a_ref, y_ref, wa_ref, wb_ref, ga_ref, gb_ref, o_ref):
    for c in range(o_ref.shape[1] // MXU_COLS):
        cs = slice(c * MXU_COLS, (c + 1) * MXU_COLS)
        pa = jnp.dot(a_ref[...], wa_ref[:, cs].astype(BF16), preferred_element_type=F32)
        pb = jnp.dot(y_ref[...], wb_ref[:, cs].astype(BF16), preferred_element_type=F32)
        ga = jax.nn.sigmoid(ga_ref[:, cs].astype(F32))
        gb = jax.nn.sigmoid(gb_ref[:, cs].astype(F32))
        o_ref[:, cs] = (ga * pa + gb * pb).astype(BF16)


def _merge(layer, attn, y, wa, wb, gates, tm=1024, tn=1024):
    n = attn.shape[0]
    nj = D_MODEL // tn
    return pl.pallas_call(
        _merge_kernel,
        grid=(n // tm, nj),
        in_specs=[
            pl.BlockSpec((tm, ATTN_WIDTH), lambda i, j: (i, 0)),
            pl.BlockSpec((tm, RWKV_WIDTH), lambda i, j: (i, 0)),
            pl.BlockSpec((None, ATTN_WIDTH, tn), lambda i, j: (layer, 0, j)),
            pl.BlockSpec((None, RWKV_WIDTH, tn), lambda i, j: (layer, 0, j)),
            pl.BlockSpec((tm, tn), lambda i, j: (i, j)),
            pl.BlockSpec((tm, tn), lambda i, j: (i, j + nj)),
        ],
        out_specs=pl.BlockSpec((tm, tn), lambda i, j: (i, j)),
        out_shape=jax.ShapeDtypeStruct((n, D_MODEL), BF16),
        compiler_params=_cparams("parallel", "arbitrary"),
        name="gated_merge",
    )(attn, y, wa, wb, gates, gates)


def _outproj_router_kernel(m_ref, wo_ref, x_ref, nw_ref, wr_ref, br_ref,
                           xo_ref, hf_ref, slab_ref, cnt_ref, run_scr):
    @pl.when(pl.program_id(0) == 0)
    def _():
        run_scr[...] = jnp.zeros_like(run_scr)

    tm = x_ref.shape[0]
    sq = jnp.zeros((tm, LANES), F32)
    for c in range(D_MODEL // MXU_COLS):
        cs = slice(c * MXU_COLS, (c + 1) * MXU_COLS)
        xn_c = x_ref[:, cs] + jnp.dot(m_ref[...], wo_ref[:, cs], preferred_element_type=F32)
        xo_ref[:, cs] = xn_c
        for l in range(MXU_COLS // LANES):
            piece = xn_c[:, l * LANES:(l + 1) * LANES]
            sq = sq + piece * piece
    ms = jnp.sum(sq, axis=-1, keepdims=True) * (1.0 / D_MODEL)
    hf = xo_ref[...] * lax.rsqrt(ms + NORM_EPS) * nw_ref[...]
    _store_rows(hf_ref, hf)

    h_hi = hf.astype(BF16)
    h_lo = (hf - h_hi.astype(F32)).astype(BF16)
    wr = wr_ref[...]
    w_hi = wr.astype(BF16)
    w_lo = (wr - w_hi.astype(F32)).astype(BF16)
    lg = (jnp.dot(h_hi, w_hi, preferred_element_type=F32)
          + jnp.dot(h_hi, w_lo, preferred_element_type=F32)
          + jnp.dot(h_lo, w_hi, preferred_element_type=F32)) + br_ref[...]

    chunks = range(tm // ROUTER_ROWS)
    lgs = [lg[c * ROUTER_ROWS:(c + 1) * ROUTER_ROWS] for c in chunks]
    lane = lax.broadcasted_iota(jnp.int32, (ROUTER_ROWS, ROUTER_LANES), 1).astype(F32)
    neg = -jnp.inf
    big = float(ROUTER_LANES)
    is_group = lane < N_GROUPS

    def row_max(vals):
        return [jnp.max(v, axis=-1, keepdims=True) for v in vals]

    def first_index(vals, mx):
        return [jnp.min(jnp.where(v == m, lane, big), axis=-1, keepdims=True) for v, m in zip(vals, mx)]

    def row_sum(vals):
        return [jnp.sum(v, axis=-1, keepdims=True) for v in vals]

    gl = [jnp.where(is_group, v, neg) for v in lgs]
    gmax = row_max(gl)
    gsel = first_index(gl, gmax)
    gsum = row_sum([jnp.where(is_group, jnp.exp(v - m), 0.0) for v, m in zip(lgs, gmax)])
    lo_lane = [EXPERT_LANE0 + EXPERTS_PER_GROUP * g for g in gsel]
    el = [jnp.where((lane >= lo) & (lane < lo + EXPERTS_PER_GROUP), v, neg) for v, lo in zip(lgs, lo_lane)]
    v1 = row_max(el)
    i1 = first_index(el, v1)
    el2 = [jnp.where(lane == i, neg, v) for v, i in zip(el, i1)]
    v2 = row_max(el2)
    i2 = first_index(el2, v2)
    oh1 = [lane == i for i in i1]
    oh2 = [lane == i for i in i2]

    cnt = jnp.concatenate([a.astype(F32) + b.astype(F32) for a, b in zip(oh1, oh2)], axis=0)
    r_i = lax.broadcasted_iota(jnp.int32, (tm, tm), 0)
    c_i = lax.broadcasted_iota(jnp.int32, (tm, tm), 1)
    before = jnp.dot((r_i > c_i).astype(BF16), cnt.astype(BF16), preferred_element_type=F32)
    tot = before + run_scr[0:1, :]
    tots = [tot[c * ROUTER_ROWS:(c + 1) * ROUTER_ROWS] for c in chunks]
    rank1 = row_sum([jnp.where(o, t, 0.0) for o, t in zip(oh1, tots)])
    rank2 = row_sum([jnp.where(o, t, 0.0) for o, t in zip(oh2, tots)])
    run = run_scr[0:1, :] + jnp.sum(cnt, axis=0, keepdims=True)
    run_scr[...] = jnp.broadcast_to(run, run_scr.shape)
    cnt_ref[...] = jnp.broadcast_to(run, cnt_ref.shape)

    for c in chunks:
        gp = 1.0 / gsum[c]
        e21 = jnp.exp(v2[c] - v1[c])
        ew1 = gp / (1.0 + e21)
        ew2 = gp * e21 / (1.0 + e21)
        slab = jnp.where(lane == 0, i1[c] - EXPERT_LANE0,
               jnp.where(lane == 1, i2[c] - EXPERT_LANE0,
               jnp.where(lane == 2, ew1,
               jnp.where(lane == 3, ew2,
               jnp.where(lane == 4, rank1[c],
               jnp.where(lane == 5, rank2[c], 0.0))))))
        slab_ref[c * ROUTER_ROWS:(c + 1) * ROUTER_ROWS, :] = slab


def _outproj_router(merged, w_out, x, norm_w, w_router, b_router, tm=512):
    n = x.shape[0]
    row = lambda i: (i, 0)
    const = lambda i: (0, 0)
    return pl.pallas_call(
        _outproj_router_kernel,
        grid=(n // tm,),
        in_specs=[
            pl.BlockSpec((tm, D_MODEL), row),
            pl.BlockSpec((D_MODEL, D_MODEL), const),
            pl.BlockSpec((tm, D_MODEL), row),
            pl.BlockSpec((1, D_MODEL), const),
            pl.BlockSpec((D_MODEL, ROUTER_LANES), const),
            pl.BlockSpec((1, ROUTER_LANES), const),
        ],
        out_specs=[
            pl.BlockSpec((tm, D_MODEL), row),
            pl.BlockSpec((tm * ROW_SUB, LANES), row),
            pl.BlockSpec((tm, ROUTER_LANES), row),
            pl.BlockSpec((SUBLANES, ROUTER_LANES), const),
        ],
        out_shape=[
            jax.ShapeDtypeStruct((n, D_MODEL), F32),
            jax.ShapeDtypeStruct((n * ROW_SUB, LANES), F32),
            jax.ShapeDtypeStruct((n, ROUTER_LANES), F32),
            jax.ShapeDtypeStruct((SUBLANES, ROUTER_LANES), F32),
        ],
        scratch_shapes=[pltpu.VMEM((SUBLANES, ROUTER_LANES), F32)],
        compiler_params=_cparams("arbitrary"),
        name="outproj_router",
    )(merged, w_out, x, norm_w, w_router, b_router)


def _dispatch_kernel(slot_ref, hf_ref, xb_in_hbm, xb_hbm, sem, *, tm):
    del xb_in_hbm

    def copy(u, j):
        return pltpu.make_async_copy(_slab(hf_ref, u), _slab(xb_hbm, slot_ref[TOP_K * u + j]), sem)

    def start(u, c):
        for j in range(TOP_K):
            copy(u, j).start(priority=j)
        return c

    def wait(u, c):
        for j in range(TOP_K):
            copy(u, j).wait()
        return c

    lax.fori_loop(0, tm, start, 0, unroll=8)
    lax.fori_loop(0, tm, wait, 0, unroll=8)


def _dispatch(slot_flat, hf, cap, tm=256):
    n = hf.shape[0] // ROW_SUB
    xb0 = jnp.zeros((cap * ROW_SUB, LANES), F32)
    return pl.pallas_call(
        functools.partial(_dispatch_kernel, tm=tm),
        grid=(n // tm,),
        in_specs=[
            pl.BlockSpec((TOP_K * tm,), lambda i: (i,), memory_space=pltpu.SMEM),
            pl.BlockSpec((tm * ROW_SUB, LANES), lambda i: (i, 0)),
            pl.BlockSpec(memory_space=pl.ANY),
        ],
        out_specs=pl.BlockSpec(memory_space=pl.ANY),
        out_shape=jax.ShapeDtypeStruct((cap * ROW_SUB, LANES), F32),
        scratch_shapes=[pltpu.SemaphoreType.DMA(())],
        input_output_aliases={2: 0},
        compiler_params=_cparams("arbitrary"),
        name="moe_dispatch",
    )(slot_flat, hf, xb0)


def _ffn_kernel(be_ref, nu_ref, par_ref, nxt_ref, x_ref, wg_hbm, wu_hbm, wd_hbm, o_ref,
                wg_f, wu_f, wd_f, wg_s, wu_s, wd_s, sems, *, layer):
    b = pl.program_id(0)
    used = b < nu_ref[0]
    prev_e = be_ref[jnp.maximum(b - 1, 0)]
    new_expert = jnp.logical_or(b == 0, be_ref[b] != prev_e)

    def fetch(e, p):
        return [pltpu.make_async_copy(src.at[layer, e], dst.at[p], sems.at[p, i])
                for i, (src, dst) in enumerate(((wg_hbm, wg_f), (wu_hbm, wu_f), (wd_hbm, wd_f)))]

    @pl.when(jnp.logical_and(used, new_expert))
    def _():
        for p in range(2):
            @pl.when(par_ref[b] == p)
            def _():
                @pl.when(b == 0)
                def _():
                    for cp in fetch(be_ref[0], p):
                        cp.start()

                @pl.when(nxt_ref[b] >= 0)
                def _():
                    for cp in fetch(nxt_ref[b], 1 - p):
                        cp.start()

                for cp in fetch(be_ref[b], p):
                    cp.wait()
                wg_s[...] = wg_f[p].astype(BF16)
                wu_s[...] = wu_f[p].astype(BF16)
                wd_s[...] = wd_f[p].astype(BF16)

    @pl.when(used)
    def _():
        x = _rows_to_2d(x_ref, MOE_ROWS).astype(BF16)
        gt = jnp.dot(x, wg_s[...], preferred_element_type=F32)
        up = jnp.dot(x, wu_s[...], preferred_element_type=F32)
        hid = (gt * jax.nn.sigmoid(gt) * up).astype(BF16)
        _store_rows(o_ref, jnp.dot(hid, wd_s[...], preferred_element_type=F32))

    @pl.when(jnp.logical_not(used))
    def _():
        o_ref[...] = jnp.zeros_like(o_ref)


def _expert_ffn(layer, plan, xb, w_gate, w_up, w_down):
    nblk = xb.shape[0] // (MOE_ROWS * ROW_SUB)
    hbm = pl.BlockSpec(memory_space=pl.ANY)
    grid_spec = pltpu.PrefetchScalarGridSpec(
        num_scalar_prefetch=4,
        grid=(nblk,),
        in_specs=[
            pl.BlockSpec((MOE_ROWS * ROW_SUB, LANES), lambda b, be, nu, par, nxt: (jnp.minimum(b, nu[0] - 1), 0)),
            hbm, hbm, hbm,
        ],
        out_specs=pl.BlockSpec((MOE_ROWS * ROW_SUB, LANES), lambda b, be, nu, par, nxt: (b, 0)),
        scratch_shapes=[
            pltpu.VMEM((2, D_MODEL, EXPERT_HIDDEN), F32),
            pltpu.VMEM((2, D_MODEL, EXPERT_HIDDEN), F32),
            pltpu.VMEM((2, EXPERT_HIDDEN, D_MODEL), F32),
            pltpu.VMEM((D_MODEL, EXPERT_HIDDEN), BF16),
            pltpu.VMEM((D_MODEL, EXPERT_HIDDEN), BF16),
            pltpu.VMEM((EXPERT_HIDDEN, D_MODEL), BF16),
            pltpu.SemaphoreType.DMA((2, 3)),
        ],
    )
    return pl.pallas_call(
        functools.partial(_ffn_kernel, layer=layer),
        grid_spec=grid_spec,
        out_shape=jax.ShapeDtypeStruct(xb.shape, F32),
        compiler_params=_cparams("arbitrary"),
        name="expert_ffn",
    )(plan.block_expert, plan.n_used, plan.parity, plan.next_expert, xb, w_gate, w_up, w_down)


def _combine_kernel(slot_ref, slot_next_ref, x_ref, slab_ref, fw_ref, yb_hbm, o_ref,
                    buf00, buf01, buf10, buf11, sems, *, tm, n_tiles, final_norm):
    i = pl.program_id(0)
    bufs = ((buf00, buf01), (buf10, buf11))

    def copies(slots, p, u):
        return [pltpu.make_async_copy(_slab(yb_hbm, slots[TOP_K * u + j]), _slab(bufs[p][j], u), sems.at[p])
                for j in range(TOP_K)]

    def start_all(slots, p):
        def body(u, c):
            for j, cp in enumerate(copies(slots, p, u)):
                cp.start(priority=j)
            return c
        lax.fori_loop(0, tm, body, 0, unroll=8)

    def wait_all(p):
        def body(u, c):
            for cp in copies(slot_ref, p, u):
                cp.wait()
            return c
        lax.fori_loop(0, tm, body, 0, unroll=8)

    @pl.when(i == 0)
    def _():
        start_all(slot_ref, 0)

    for p in range(2):
        @pl.when(i % 2 == p)
        def _():
            @pl.when(i + 1 < n_tiles)
            def _():
                start_all(slot_next_ref, 1 - p)

            wait_all(p)
            slab = slab_ref[...]
            w1 = slab[:, 2:3]
            w2 = slab[:, 3:4]
            out = x_ref[...] + (_rows_to_2d(bufs[p][0], tm) * w1 + _rows_to_2d(bufs[p][1], tm) * w2)
            if final_norm:
                out = _rms(out, fw_ref[...])
            o_ref[...] = out


def _combine(slot_flat, x, slab, final_w, yb, final_norm, tm=256):
    n = x.shape[0]
    n_tiles = n // tm
    row = lambda i: (i, 0)
    row_buf = pltpu.VMEM((tm * ROW_SUB, LANES), F32)
    return pl.pallas_call(
        functools.partial(_combine_kernel, tm=tm, n_tiles=n_tiles, final_norm=final_norm),
        grid=(n_tiles,),
        in_specs=[
            pl.BlockSpec((TOP_K * tm,), lambda i: (i,), memory_space=pltpu.SMEM),
            pl.BlockSpec((TOP_K * tm,), lambda i: (jnp.minimum(i + 1, n_tiles - 1),), memory_space=pltpu.SMEM),
            pl.BlockSpec((tm, D_MODEL), row),
            pl.BlockSpec((tm, ROUTER_LANES), row),
            pl.BlockSpec((1, D_MODEL), lambda i: (0, 0)),
            pl.BlockSpec(memory_space=pl.ANY),
        ],
        out_specs=pl.BlockSpec((tm, D_MODEL), row),
        out_shape=jax.ShapeDtypeStruct((n, D_MODEL), F32),
        scratch_shapes=[row_buf, row_buf, row_buf, row_buf, pltpu.SemaphoreType.DMA((2,))],
        compiler_params=_cparams("arbitrary"),
        name="moe_combine",
    )(slot_flat, slot_flat, x, slab, final_w, yb)


def _pad_rows(w, rows, offset=0):
    out = jnp.zeros((rows, w.shape[1]), BF16)
    return out.at[offset:offset + w.shape[0]].set(w.astype(BF16))


def _pad_cols(vec, cols, offset=0):
    out = jnp.zeros((cols,), F32)
    return out.at[offset:offset + vec.shape[0]].set(vec.astype(F32))


class MoePlan(NamedTuple):
    slot: jax.Array
    block_expert: jax.Array
    n_used: jax.Array
    parity: jax.Array
    next_expert: jax.Array
    cap: int


def _moe_plan(slab, counts_row, n_tokens):
    eid = slab[:, 0:TOP_K].astype(jnp.int32)
    rank = slab[:, 4:4 + TOP_K].astype(jnp.int32)
    counts = counts_row[EXPERT_LANE0:EXPERT_LANE0 + N_EXPERTS].astype(jnp.int32)
    padded = (counts + MOE_ROWS - 1) // MOE_ROWS * MOE_ROWS
    pad_ends = jnp.cumsum(padded)
    pad_starts = pad_ends - padded
    slot = pad_starts[eid.reshape(-1)] + rank.reshape(-1)
    nblk = n_tokens * TOP_K // MOE_ROWS + N_EXPERTS
    block_start = jnp.arange(nblk, dtype=jnp.int32) * MOE_ROWS
    block_expert = jnp.minimum(
        jnp.sum((pad_ends[None, :] <= block_start[:, None]).astype(jnp.int32), axis=1),
        N_EXPERTS - 1).astype(jnp.int32)
    n_used = (pad_ends[-1:] // MOE_ROWS).astype(jnp.int32)
    blk = jnp.arange(nblk, dtype=jnp.int32)
    prev_expert = jnp.concatenate([block_expert[:1], block_expert[:-1]])
    first = (blk < n_used[0]) & ((blk == 0) | (block_expert != prev_expert))
    parity = ((jnp.cumsum(first.astype(jnp.int32)) - 1) % 2).astype(jnp.int32)
    later_first = first[None, :] & (blk[None, :] > blk[:, None])
    next_first = jnp.min(jnp.where(later_first, blk[None, :], nblk), axis=1)
    next_expert = jnp.where(next_first < nblk, block_expert[jnp.minimum(next_first, nblk - 1)], -1)
    return MoePlan(slot, block_expert, n_used, parity, next_expert.astype(jnp.int32), nblk * MOE_ROWS)


def kernel(x, attn_norm_w, w_in, shift_mix, attn_sinks, rwkv_w0, rwkv_w_up, rwkv_a0, rwkv_a_up, rwkv_g_up, rwkv_k_k, rwkv_k_a, rwkv_r_k, rwkv_ln_w, rwkv_ln_b, vres_down, vres_mix, vres_up, vres_v0, w_branch_a, w_branch_b, w_out, ffn_norm_w, router_group_w, router_group_b, router_expert_w, router_expert_b, expert_w_gate, expert_w_up, expert_w_down, final_norm_w):
    batch, seq, _ = x.shape
    n = batch * seq
    depth = w_in.shape[0]
    xf = x.reshape(n, D_MODEL)
    w_in_bf = jnp.swapaxes(w_in, 1, 2).astype(BF16)
    v_first = None
    for i in range(depth):
        has_vres = i > 0
        w_gates = w_in_bf[i, GATE_COL0:, :]
        mix_r = shift_mix[i, :RKV_COLS].reshape(1, RKV_COLS)
        mix_l = _pad_cols(shift_mix[i, RKV_COLS:], LORA_COLS).reshape(1, LORA_COLS)
        v0 = vres_v0[i - 1] if has_vres else jnp.zeros((RWKV_WIDTH,), F32)
        zero = jnp.zeros((RWKV_WIDTH,), F32)
        pv_pre = jnp.stack([rwkv_w0[i], rwkv_a0[i], rwkv_k_k[i], rwkv_k_a[i], v0, zero, zero, zero]).astype(F32)
        pv_post = jnp.stack([rwkv_ln_w[i], rwkv_ln_b[i], rwkv_r_k[i].reshape(-1),
                             zero, zero, zero, zero, zero]).astype(F32)
        wup = _pad_rows(rwkv_w_up[i], LANES, 0)
        aup = _pad_rows(rwkv_a_up[i], LANES, DECAY_LORA)
        gup = _pad_rows(rwkv_g_up[i], 2 * LANES, 0)
        if has_vres:
            vd = _pad_rows(vres_down[i - 1].T, LANES, 0).T
            mix_v = _pad_cols(vres_mix[i - 1], LANES).reshape(1, LANES)
            vup = _pad_rows(vres_up[i - 1], LANES, 0)
        else:
            vd, mix_v = None, None
            vup = jnp.zeros((LANES, RWKV_WIDTH), BF16)
        w_router = jnp.concatenate(
            [router_group_w[i], router_expert_w[i],
             jnp.zeros((D_MODEL, ROUTER_LANES - N_GROUPS - N_EXPERTS), F32)], axis=1)
        b_router = _pad_cols(jnp.concatenate([router_group_b[i], router_expert_b[i]]),
                             ROUTER_LANES).reshape(1, ROUTER_LANES)

        proj = _inproj(i, xf, attn_norm_w[i].reshape(1, D_MODEL), w_in_bf, w_gates, vd)
        qkv, rkv, lora, gates = proj[:4]
        zv = proj[4] if has_vres else None
        attn = _attention(qkv, attn_sinks[i], seq)
        r, lw, k2, v, al, be, g = _rwkv_pre(rkv, lora, v_first, zv, mix_r, mix_l, mix_v, pv_pre,
                                            wup, aup, gup, vup, seq)
        if not has_vres:
            v_first = v
        y_raw = _rwkv_scan(r, lw, k2, v, al, be, seq)
        y = _rwkv_post(y_raw, r, k2, v, g, pv_post)
        merged = _merge(i, attn, y, w_branch_a, w_branch_b, gates)

        x_mid, hf, slab, counts = _outproj_router(
            merged, w_out[i].astype(BF16), xf, ffn_norm_w[i].reshape(1, D_MODEL), w_router, b_router)
        plan = _moe_plan(slab, counts[0], n)
        xb = _dispatch(plan.slot, hf, plan.cap)
        yb = _expert_ffn(i, plan, xb, expert_w_gate, expert_w_up, expert_w_down)
        xf = _combine(plan.slot, x_mid, slab, final_norm_w.reshape(1, D_MODEL), yb, final_norm=(i == depth - 1))
    return xf.reshape(batch, seq, D_MODEL)
```

```python
import functools
from typing import NamedTuple

import jax
import jax.numpy as jnp
from jax import lax
from jax.experimental import pallas as pl
from jax.experimental.pallas import tpu as pltpu

F32 = jnp.float32
BF16 = jnp.bfloat16

D_MODEL = 2048
HEAD_DIM = 64
N_Q_HEADS = 16
N_KV_HEADS = 4
GQA_GROUP = N_Q_HEADS // N_KV_HEADS
ATTN_WIDTH = N_Q_HEADS * HEAD_DIM
KV_WIDTH = N_KV_HEADS * HEAD_DIM
WINDOW = 128
ATTN_BLOCK = 128
ATTN_SUB = 2

RWKV_HEAD = 64
N_RWKV_HEADS = 16
RWKV_WIDTH = N_RWKV_HEADS * RWKV_HEAD
DECAY_LORA = 64
AAA_LORA = 64
MV_LORA = 32
GATE_LORA = 160
RWKV_GN_EPS = 64e-5
LORA_WIDTH = DECAY_LORA + AAA_LORA + GATE_LORA

N_GROUPS = 4
EXPERTS_PER_GROUP = 8
N_EXPERTS = N_GROUPS * EXPERTS_PER_GROUP
TOP_K = 2
EXPERT_HIDDEN = D_MODEL // 4
NORM_EPS = 1e-5

LANES = 128
SUBLANES = 8
MXU_COLS = 256

SEG = 512
QKV_COLS = ATTN_WIDTH + 2 * KV_WIDTH
RKV_COLS = 3 * RWKV_WIDTH
LORA_COLS = SEG
GATE_COLS = 2 * D_MODEL
QKV_TILES = QKV_COLS // SEG
RKV_TILES = RKV_COLS // SEG
LORA_TILES = LORA_COLS // SEG
GATE_TILES = GATE_COLS // SEG
IN_TILES = QKV_TILES + RKV_TILES + LORA_TILES + GATE_TILES
FIRST_GATE_TILE = QKV_TILES + RKV_TILES + LORA_TILES
GATE_COL0 = QKV_COLS + RKV_COLS + LORA_WIDTH

SCAN_CHUNK = 64
SCAN_SUB = 2
SCAN_GROUP = 4
MOE_ROWS = 256
ROUTER_LANES = LANES
ROUTER_ROWS = 128
EXPERT_LANE0 = N_GROUPS

ROW_SUB = D_MODEL // LANES

VMEM_LIMIT = 56 * 1024 * 1024


def _rows_to_2d(slab_ref, rows):
    return jnp.concatenate([slab_ref[pl.ds(s, rows, stride=ROW_SUB), :] for s in range(ROW_SUB)], axis=1)


def _store_rows(slab_ref, val):
    rows = val.shape[0]
    for s in range(ROW_SUB):
        slab_ref[pl.ds(s, rows, stride=ROW_SUB), :] = val[:, s * LANES:(s + 1) * LANES]


def _slab(ref, row):
    return ref.at[pl.ds(pl.multiple_of(row * ROW_SUB, ROW_SUB), ROW_SUB)]


def _cparams(*sem):
    return pltpu.CompilerParams(dimension_semantics=sem, vmem_limit_bytes=VMEM_LIMIT)


def _bdot(a, b):
    return jnp.dot(a.astype(BF16), b.astype(BF16), preferred_element_type=F32)


def _bdot_nt(a, b):
    return lax.dot_general(a.astype(BF16), b.astype(BF16), (((1,), (1,)), ((), ())),
                           preferred_element_type=F32)


def _bdot_tn(a, b):
    return lax.dot_general(a.astype(BF16), b.astype(BF16), (((0,), (0,)), ((), ())),
                           preferred_element_type=F32)


def _dot_split_rhs(w01, x, passes):
    acc, rem = None, x
    for p in range(passes):
        part = rem.astype(BF16)
        d = jnp.dot(w01, part, preferred_element_type=F32)
        acc = d if acc is None else acc + d
        if p + 1 < passes:
            rem = rem - part.astype(F32)
    return acc


def _dot_split_lhs(x, w01, passes):
    acc, rem = None, x
    for p in range(passes):
        part = rem.astype(BF16)
        d = jnp.dot(part, w01, preferred_element_type=F32)
        acc = d if acc is None else acc + d
        if p + 1 < passes:
            rem = rem - part.astype(F32)
    return acc


def _rms(x, w):
    ms = jnp.mean(x * x, axis=-1, keepdims=True)
    return x * lax.rsqrt(ms + NORM_EPS) * w


def _inproj_kernel(*refs, has_vres):
    refs = list(refs)
    x_ref, nw_ref, w_ref, wg_ref = refs[:4]
    refs = refs[4:]
    vd_ref = refs.pop(0) if has_vres else None
    qkv_ref, rkv_ref, lora_ref, gate_ref = refs[:4]
    refs = refs[4:]
    zv_ref = refs.pop(0) if has_vres else None
    (h_scr,) = refs
    j = pl.program_id(1)

    @pl.when(j == 0)
    def _():
        h_scr[...] = _rms(x_ref[...], nw_ref[...]).astype(BF16)
        if has_vres:
            zv_ref[...] = jnp.dot(h_scr[...], vd_ref[...], preferred_element_type=F32)

    def project(wt_ref, out_ref):
        for c in range(SEG // MXU_COLS):
            cs = slice(c * MXU_COLS, (c + 1) * MXU_COLS)
            acc = lax.dot_general(h_scr[...], wt_ref[cs, :], (((1,), (1,)), ((), ())),
                                  preferred_element_type=F32)
            out_ref[:, cs] = acc.astype(out_ref.dtype)

    @pl.when(j < QKV_TILES)
    def _():
        project(w_ref, qkv_ref)

    @pl.when((j >= QKV_TILES) & (j < QKV_TILES + RKV_TILES))
    def _():
        project(w_ref, rkv_ref)

    @pl.when(j == QKV_TILES + RKV_TILES)
    def _():
        project(w_ref, lora_ref)

    @pl.when(j >= FIRST_GATE_TILE)
    def _():
        project(wg_ref, gate_ref)


def _inproj(layer, x, norm_w, w_in, w_gates, vd, tm=1024):
    n = x.shape[0]
    has_vres = vd is not None
    s_rkv = QKV_TILES
    in_specs = [
        pl.BlockSpec((tm, D_MODEL), lambda i, j: (i, 0)),
        pl.BlockSpec((1, D_MODEL), lambda i, j: (0, 0)),
        pl.BlockSpec((None, SEG, D_MODEL), lambda i, j: (layer, jnp.minimum(j, FIRST_GATE_TILE - 1), 0)),
        pl.BlockSpec((SEG, D_MODEL), lambda i, j: (jnp.clip(j - FIRST_GATE_TILE, 0, GATE_TILES - 1), 0)),
    ]
    args = [x, norm_w, w_in, w_gates]
    out_specs = [
        pl.BlockSpec((tm, SEG), lambda i, j: (i, jnp.clip(j, 0, QKV_TILES - 1))),
        pl.BlockSpec((tm, SEG), lambda i, j: (i, jnp.clip(j - s_rkv, 0, RKV_TILES - 1))),
        pl.BlockSpec((tm, SEG), lambda i, j: (i, 0)),
        pl.BlockSpec((tm, SEG), lambda i, j: (i, jnp.clip(j - FIRST_GATE_TILE, 0, GATE_TILES - 1))),
    ]
    out_shape = [
        jax.ShapeDtypeStruct((n, QKV_COLS), BF16),
        jax.ShapeDtypeStruct((n, RKV_COLS), F32),
        jax.ShapeDtypeStruct((n, LORA_COLS), F32),
        jax.ShapeDtypeStruct((n, GATE_COLS), BF16),
    ]
    if has_vres:
        in_specs.append(pl.BlockSpec((D_MODEL, LANES), lambda i, j: (0, 0)))
        args.append(vd)
        out_specs.append(pl.BlockSpec((tm, LANES), lambda i, j: (i, 0)))
        out_shape.append(jax.ShapeDtypeStruct((n, LANES), F32))
    return pl.pallas_call(
        functools.partial(_inproj_kernel, has_vres=has_vres),
        grid=(n // tm, IN_TILES),
        in_specs=in_specs,
        out_specs=out_specs,
        out_shape=out_shape,
        scratch_shapes=[pltpu.VMEM((tm, D_MODEL), BF16)],
        compiler_params=_cparams("parallel", "arbitrary"),
        name="inproj",
    )(*args)


def _attn_kernel(sink_ref, q_ref, kp_ref, kc_ref, vp_ref, vc_ref, bias_ref, o_ref, *, tiles_per_seq):
    first = (pl.program_id(0) % tiles_per_seq) == 0
    col = lax.broadcasted_iota(jnp.int32, (ATTN_BLOCK, 2 * ATTN_BLOCK), 1)
    pad_keys = jnp.logical_and(first, col < ATTN_BLOCK)
    scale = HEAD_DIM ** -0.5
    items = [(sb, hk) for sb in range(ATTN_SUB) for hk in range(N_KV_HEADS)]

    def window(prev_ref, cur_ref, sb, hk):
        ks = slice(hk * HEAD_DIM, (hk + 1) * HEAD_DIM)
        rows = slice(sb * ATTN_BLOCK, (sb + 1) * ATTN_BLOCK)
        prev = prev_ref[:, ks] if sb == 0 else cur_ref[(sb - 1) * ATTN_BLOCK:sb * ATTN_BLOCK, ks]
        return jnp.concatenate([prev, cur_ref[rows, ks]], axis=0)

    scores = {}
    for sb, hk in items:
        kw = window(kp_ref, kc_ref, sb, hk)
        for g in range(GQA_GROUP):
            h = hk * GQA_GROUP + g
            qh = q_ref[sb * ATTN_BLOCK:(sb + 1) * ATTN_BLOCK, h * HEAD_DIM:(h + 1) * HEAD_DIM]
            s = lax.dot_general(qh, kw, (((1,), (1,)), ((), ())), preferred_element_type=F32)
            s = s * scale + bias_ref[h]
            if sb == 0:
                s = jnp.where(pad_keys, -jnp.inf, s)
            scores[(sb, h)] = s
    row_max = {key: jnp.max(s, axis=-1, keepdims=True) for key, s in scores.items()}
    m = {key: jnp.maximum(row_max[key], sink_ref[key[1]]) for key in scores}
    p = {key: jnp.exp(scores[key] - m[key]) for key in scores}
    row_sum = {key: jnp.sum(p[key], axis=-1, keepdims=True) for key in scores}
    denoms = {key: row_sum[key] + jnp.exp(sink_ref[key[1]] - m[key]) for key in scores}
    probs = {key: p[key].astype(BF16) for key in scores}
    for sb, hk in items:
        vw = window(vp_ref, vc_ref, sb, hk)
        for g in range(GQA_GROUP):
            h = hk * GQA_GROUP + g
            o = jnp.dot(probs[(sb, h)], vw, preferred_element_type=F32) / denoms[(sb, h)]
            o_ref[sb * ATTN_BLOCK:(sb + 1) * ATTN_BLOCK, h * HEAD_DIM:(h + 1) * HEAD_DIM] = o.astype(BF16)


def _attn_bias():
    qi = jnp.arange(ATTN_BLOCK)[:, None]
    kj = jnp.arange(2 * ATTN_BLOCK)[None, :]
    dist = qi + ATTN_BLOCK - kj
    valid = (dist >= 0) & (dist < WINDOW)
    slopes = jnp.exp2(-8.0 * jnp.arange(1, N_Q_HEADS + 1, dtype=F32) / N_Q_HEADS)
    bias = -slopes[:, None, None] * dist.astype(F32)[None]
    return jnp.where(valid[None], bias, -jnp.inf)


def _attention(qkv, sinks, seq):
    n = qkv.shape[0]
    rows = ATTN_SUB * ATTN_BLOCK
    tiles_per_seq = seq // rows
    kcol = ATTN_WIDTH // KV_WIDTH
    prev = lambda i: jnp.maximum(i * ATTN_SUB - 1, 0)
    return pl.pallas_call(
        functools.partial(_attn_kernel, tiles_per_seq=tiles_per_seq),
        grid=(n // rows,),
        in_specs=[
            pl.BlockSpec(memory_space=pltpu.SMEM),
            pl.BlockSpec((rows, ATTN_WIDTH), lambda i: (i, 0)),
            pl.BlockSpec((ATTN_BLOCK, KV_WIDTH), lambda i: (prev(i), kcol)),
            pl.BlockSpec((rows, KV_WIDTH), lambda i: (i, kcol)),
            pl.BlockSpec((ATTN_BLOCK, KV_WIDTH), lambda i: (prev(i), kcol + 1)),
            pl.BlockSpec((rows, KV_WIDTH), lambda i: (i, kcol + 1)),
            pl.BlockSpec((N_Q_HEADS, ATTN_BLOCK, 2 * ATTN_BLOCK), lambda i: (0, 0, 0)),
        ],
        out_specs=pl.BlockSpec((rows, ATTN_WIDTH), lambda i: (i, 0)),
        out_shape=jax.ShapeDtypeStruct((n, ATTN_WIDTH), BF16),
        compiler_params=_cparams("parallel"),
        name="swa_attention",
    )(sinks.astype(F32), qkv, qkv, qkv, qkv, qkv, _attn_bias())


def _head_ones():
    ch = jnp.arange(MXU_COLS) // RWKV_HEAD
    return (ch[:, None] == ch[None, :]).astype(BF16)


def _head_sums(x, ones, passes):
    parts = [_dot_split_lhs(x[:, c:c + MXU_COLS], ones, passes) for c in range(0, RWKV_WIDTH, MXU_COLS)]
    return jnp.concatenate(parts, axis=1)


def _rwkv_kernel(*refs, has_vres):
    refs = list(refs)
    rkv_ref, lo_ref = refs[:2]
    refs = refs[2:]
    if has_vres:
        vf_ref, zv_ref, mixv_ref = refs[:3]
        refs = refs[3:]
    mixr_ref, mixl_ref, pv_ref, wup_ref, aup_ref, gup_ref, vup_ref, ones_ref, pp_ref = refs[:9]
    refs = refs[9:]
    y_ref = refs.pop(0)
    v_out = None if has_vres else refs.pop(0)
    s_scr, prev_rkv, prev_lo = refs[:3]
    prev_zv = refs[3] if has_vres else None

    L = SCAN_CHUNK
    W = SCAN_GROUP * RWKV_HEAD
    n_seq, step_rows = rkv_ref.shape[0], rkv_ref.shape[1]
    n_grp = N_RWKV_HEADS // SCAN_GROUP
    ones = ones_ref[...]

    @pl.when(pl.program_id(0) == 0)
    def _():
        s_scr[...] = jnp.zeros_like(s_scr)
        prev_rkv[...] = jnp.zeros_like(prev_rkv)
        prev_lo[...] = jnp.zeros_like(prev_lo)
        if has_vres:
            prev_zv[...] = jnp.zeros_like(prev_zv)

    def lerp_shift(z_ref, carry_ref, b, mix):
        z = z_ref[b]
        row = lax.broadcasted_iota(jnp.int32, z.shape, 0)
        back = jnp.where(row == 0, carry_ref[b, SUBLANES - 1:SUBLANES, :], pltpu.roll(z, 1, 0))
        carry_ref[b] = z_ref[b, step_rows - SUBLANES:step_rows, :]
        return z + (back - z) * mix

    w0, a0, k_k, k_a, v0 = (pv_ref[i:i + 1, :] for i in range(5))
    pre = []
    for b in range(n_seq):
        zs = lerp_shift(rkv_ref, prev_rkv, b, mixr_ref[...])
        los = lerp_shift(lo_ref, prev_lo, b, mixl_ref[...])
        r = zs[:, :RWKV_WIDTH]
        kr = zs[:, RWKV_WIDTH:2 * RWKV_WIDTH]
        vr = zs[:, 2 * RWKV_WIDTH:]
        wa = los[:, :LANES]
        col = lax.broadcasted_iota(jnp.int32, wa.shape, 1)
        wa_in = jnp.where(col < DECAY_LORA, jnp.tanh(wa), wa).astype(BF16)
        gd = los[:, LANES:3 * LANES]
        col_g = lax.broadcasted_iota(jnp.int32, gd.shape, 1)
        g_in = jnp.where(col_g < GATE_LORA, jax.nn.sigmoid(gd), gd).astype(BF16)
        dw = jnp.dot(wa_in, wup_ref[...], preferred_element_type=F32)
        da = jnp.dot(wa_in, aup_ref[...], preferred_element_type=F32)
        g = jnp.dot(g_in, gup_ref[...], preferred_element_type=F32)
        u = -(w0 + dw)
        softplus = jnp.maximum(u, 0.0) + jnp.log1p(jnp.exp(-jnp.abs(u)))
        w_log = -softplus - 0.5
        a = jax.nn.sigmoid(a0 + da)
        if has_vres:
            zvs = lerp_shift(zv_ref, prev_zv, b, mixv_ref[...])
            dv = jnp.dot(zvs.astype(BF16), vup_ref[...], preferred_element_type=F32)
            vr = vr + (vf_ref[b] - vr) * jax.nn.sigmoid(v0 + dv)
        else:
            v_out[b] = vr
        kk0 = kr * k_k
        ss = _head_sums(kk0 * kk0, ones, 2)
        kk = kk0 * (1.0 / jnp.maximum(jnp.sqrt(ss), 1e-12))
        pre.append(dict(r=r, lw=-jnp.exp(w_log),
                        k=kr * (1.0 + (a - 1.0) * k_a), v=vr, al=-kk, be=kk * a, g=g))

    row = lax.broadcasted_iota(jnp.int32, (L, L), 0)
    col = lax.broadcasted_iota(jnp.int32, (L, L), 1)
    tri = (row >= col).astype(BF16)
    t_w = lax.broadcasted_iota(jnp.int32, (L, W), 0)
    lane_w = lax.broadcasted_iota(jnp.int32, (L, W), 1)
    s_w = lane_w % RWKV_HEAD
    strict = t_w > s_w
    incl = t_w >= s_w
    eye = (t_w == s_w).astype(F32)
    blk_w = lane_w // RWKV_HEAD
    bd_mask = (lax.broadcasted_iota(jnp.int32, (W, W), 0) // RWKV_HEAD
               == lax.broadcasted_iota(jnp.int32, (W, W), 1) // RWKV_HEAD)

    def blockdiag(w):
        tiled = jnp.concatenate([w.astype(BF16)] * SCAN_GROUP, axis=0)
        return jnp.where(bd_mask, tiled, jnp.zeros_like(tiled))

    ln_w, ln_b, r_k = (pp_ref[i:i + 1, :] for i in range(3))
    items = [(b, g) for b in range(n_seq) for g in range(n_grp)]
    for sub in range(step_rows // L):
        rows = slice(sub * L, (sub + 1) * L)
        ops = {}
        for b in range(n_seq):
            p = pre[b]
            lw = p['lw'][rows]
            cum = _dot_split_rhs(tri, lw, 3)
            p_inc = jnp.exp(cum)
            p_exc = jnp.exp(cum - lw)
            p_inv = jnp.exp(-cum)
            p_last = p_inc[L - 1:L, :]
            a_t = p['al'][rows] * p_exc
            r_t = p['r'][rows] * p_inc
            b_t = p['be'][rows] * p_inv
            k_t = p['k'][rows] * p_inv
            b_end = b_t * p_last
            k_end = k_t * p_last
            v = p['v'][rows]
            for g in range(n_grp):
                gs = slice(g * W, (g + 1) * W)
                ops[(b, g)] = dict(
                    ar=jnp.concatenate([a_t[:, gs], r_t[:, gs]], axis=0),
                    bt=b_t[:, gs], kt=k_t[:, gs], v=v[:, gs], p_last=p_last[:, gs],
                    bke=jnp.concatenate([b_end[:, gs], k_end[:, gs]], axis=0))

        gm, xs, av, s0 = {}, {}, {}, {}
        for it in items:
            o = ops[it]
            rhs = jnp.concatenate([blockdiag(o['bt']), blockdiag(o['kt'])], axis=0)
            gm[it] = _bdot_nt(o['ar'], rhs)
            s0[it] = s_scr[it[0], it[1]]
            xs[it] = _bdot_nt(o['ar'], blockdiag(s0[it]))
        a_ab, a_rb = {}, {}
        for it in items:
            g = gm[it]
            a_ab[it] = jnp.where(strict, g[:L, :W], 0.0)
            a_rb[it] = jnp.where(incl, g[L:, :W], 0.0)
            a_kk = jnp.concatenate([jnp.where(strict, g[:L, W:], 0.0), jnp.where(incl, g[L:, W:], 0.0)], axis=0)
            av[it] = _bdot(a_kk, blockdiag(ops[it]['v']))

        pw = {it: _bdot(a_ab[it], blockdiag(a_ab[it])) for it in items}
        tinv = {it: eye + a_ab[it] for it in items}
        span = 2
        while 2 * span < L:
            both = {it: _bdot(jnp.concatenate([pw[it], tinv[it]], axis=0), blockdiag(pw[it])) for it in items}
            tinv = {it: tinv[it] + both[it][L:] for it in items}
            pw = {it: both[it][:L] for it in items}
            span *= 2
        tinv = {it: tinv[it] + _bdot(tinv[it], blockdiag(pw[it])) for it in items}

        u = {it: _bdot(tinv[it], blockdiag(xs[it][:L] + av[it][:L])) for it in items}
        y = {it: xs[it][L:] + av[it][L:] + _bdot(a_rb[it], blockdiag(u[it])) for it in items}
        for it in items:
            o = ops[it]
            full = _bdot_tn(jnp.concatenate([u[it], o['v']], axis=0), o['bke'])
            upd = full[:RWKV_HEAD]
            for h in range(1, SCAN_GROUP):
                upd = jnp.where(blk_w == h, full[h * RWKV_HEAD:(h + 1) * RWKV_HEAD], upd)
            s_scr[it[0], it[1]] = s0[it] * o['p_last'] + upd

        inv_n = 1.0 / RWKV_HEAD
        for b in range(n_seq):
            p = pre[b]
            yb = jnp.concatenate([y[(b, g)] for g in range(n_grp)], axis=1)
            d = yb - _head_sums(yb, ones, 2) * inv_n
            var = _head_sums(d * d, ones, 2) * inv_n
            yn = d * lax.rsqrt(var + RWKV_GN_EPS) * ln_w + ln_b
            bonus = _head_sums(p['r'][rows] * p['k'][rows] * r_k, ones, 2) * p['v'][rows]
            y_ref[b, rows, :] = ((yn + bonus) * p['g'][rows]).astype(BF16)


def _rwkv_mixer(rkv, lora, v_first, zv, mix_r, mix_l, mix_v, pv_pre, wup, aup, gup, vup, pv_post, seq):
    n = rkv.shape[0]
    n_seq = n // seq
    has_vres = v_first is not None
    step_rows = SCAN_SUB * SCAN_CHUNK
    timed = lambda cols: pl.BlockSpec((n_seq, step_rows, cols), lambda t: (0, t, 0))
    const = lambda shape: pl.BlockSpec(shape, lambda t: (0, 0))
    by_seq = lambda a: a.reshape(n_seq, seq, a.shape[-1])
    in_specs = [timed(RKV_COLS), timed(LORA_COLS)]
    args = [by_seq(rkv), by_seq(lora)]
    if has_vres:
        in_specs += [timed(RWKV_WIDTH), timed(LANES), const((1, LANES))]
        args += [by_seq(v_first), by_seq(zv), mix_v]
    in_specs += [
        const((1, RKV_COLS)), const((1, LORA_COLS)), const((SUBLANES, RWKV_WIDTH)),
        const((LANES, RWKV_WIDTH)), const((LANES, RWKV_WIDTH)), const((2 * LANES, RWKV_WIDTH)),
        const((LANES, RWKV_WIDTH)), const((MXU_COLS, MXU_COLS)), const((SUBLANES, RWKV_WIDTH)),
    ]
    args += [mix_r, mix_l, pv_pre, wup, aup, gup, vup, _head_ones(), pv_post]
    out_specs = [timed(RWKV_WIDTH)]
    out_shape = [jax.ShapeDtypeStruct((n_seq, seq, RWKV_WIDTH), BF16)]
    if not has_vres:
        out_specs.append(timed(RWKV_WIDTH))
        out_shape.append(jax.ShapeDtypeStruct((n_seq, seq, RWKV_WIDTH), F32))
    scratch = [
        pltpu.VMEM((n_seq, N_RWKV_HEADS // SCAN_GROUP, RWKV_HEAD, SCAN_GROUP * RWKV_HEAD), F32),
        pltpu.VMEM((n_seq, SUBLANES, RKV_COLS), F32),
        pltpu.VMEM((n_seq, SUBLANES, LORA_COLS), F32),
    ]
    if has_vres:
        scratch.append(pltpu.VMEM((n_seq, SUBLANES, LANES), F32))
    outs = pl.pallas_call(
        functools.partial(_rwkv_kernel, has_vres=has_vres),
        grid=(seq // step_rows,),
        in_specs=in_specs,
        out_specs=out_specs,
        out_shape=out_shape,
        scratch_shapes=scratch,
        compiler_params=_cparams("arbitrary"),
        name="rwkv_mixer",
    )(*args)
    y = outs[0].reshape(n, RWKV_WIDTH)
    return y, (None if has_vres else outs[1].reshape(n, RWKV_WIDTH))


def _merge_kernel(a_ref, y_ref, wa_ref, wb_ref, ga_ref, gb_ref, o_ref):
    for c in range(o_ref.shape[1] // MXU_COLS):
        cs = slice(c * MXU_COLS, (c + 1) * MXU_COLS)
        pa = jnp.dot(a_ref[...], wa_ref[:, cs].astype(BF16), preferred_element_type=F32)
        pb = jnp.dot(y_ref[...], wb_ref[:, cs].astype(BF16), preferred_element_type=F32)
        ga = jax.nn.sigmoid(ga_ref[:, cs].astype(F32))
        gb = jax.nn.sigmoid(gb_ref[:, cs].astype(F32))
        o_ref[:, cs] = (ga * pa + gb * pb).astype(BF16)


def _merge(layer, attn, y, wa, wb, gates, tm=1024, tn=1024):
    n = attn.shape[0]
    nj = D_MODEL // tn
    return pl.pallas_call(
        _merge_kernel,
        grid=(n // tm, nj),
        in_specs=[
            pl.BlockSpec((tm, ATTN_WIDTH), lambda i, j: (i, 0)),
            pl.BlockSpec((tm, RWKV_WIDTH), lambda i, j: (i, 0)),
            pl.BlockSpec((None, ATTN_WIDTH, tn), lambda i, j: (layer, 0, j)),
            pl.BlockSpec((None, RWKV_WIDTH, tn), lambda i, j: (layer, 0, j)),
            pl.BlockSpec((tm, tn), lambda i, j: (i, j)),
            pl.BlockSpec((tm, tn), lambda i, j: (i, j + nj)),
        ],
        out_specs=pl.BlockSpec((tm, tn), lambda i, j: (i, j)),
        out_shape=jax.ShapeDtypeStruct((n, D_MODEL), BF16),
        compiler_params=_cparams("parallel", "arbitrary"),
        name="gated_merge",
    )(attn, y, wa, wb, gates, gates)


def _outproj_router_kernel(m_ref, wo_ref, x_ref, nw_ref, wr_ref, br_ref,
                           xo_ref, hf_ref, slab_ref, cnt_ref, run_scr):
    @pl.when(pl.program_id(0) == 0)
    def _():
        run_scr[...] = jnp.zeros_like(run_scr)

    tm = x_ref.shape[0]
    sq = jnp.zeros((tm, LANES), F32)
    for c in range(D_MODEL // MXU_COLS):
        cs = slice(c * MXU_COLS, (c + 1) * MXU_COLS)
        xn_c = x_ref[:, cs] + jnp.dot(m_ref[...], wo_ref[:, cs], preferred_element_type=F32)
        xo_ref[:, cs] = xn_c
        for l in range(MXU_COLS // LANES):
            piece = xn_c[:, l * LANES:(l + 1) * LANES]
            sq = sq + piece * piece
    ms = jnp.sum(sq, axis=-1, keepdims=True) * (1.0 / D_MODEL)
    hf = xo_ref[...] * lax.rsqrt(ms + NORM_EPS) * nw_ref[...]
    _store_rows(hf_ref, hf)

    h_hi = hf.astype(BF16)
    h_lo = (hf - h_hi.astype(F32)).astype(BF16)
    wr = wr_ref[...]
    w_hi = wr.astype(BF16)
    w_lo = (wr - w_hi.astype(F32)).astype(BF16)
    lg = (jnp.dot(h_hi, w_hi, preferred_element_type=F32)
          + jnp.dot(h_hi, w_lo, preferred_element_type=F32)
          + jnp.dot(h_lo, w_hi, preferred_element_type=F32)) + br_ref[...]

    chunks = range(tm // ROUTER_ROWS)
    lgs = [lg[c * ROUTER_ROWS:(c + 1) * ROUTER_ROWS] for c in chunks]
    lane = lax.broadcasted_iota(jnp.int32, (ROUTER_ROWS, ROUTER_LANES), 1).astype(F32)
    neg = -jnp.inf
    big = float(ROUTER_LANES)
    is_group = lane < N_GROUPS

    def row_max(vals):
        return [jnp.max(v, axis=-1, keepdims=True) for v in vals]

    def first_index(vals, mx):
        return [jnp.min(jnp.where(v == m, lane, big), axis=-1, keepdims=True) for v, m in zip(vals, mx)]

    def row_sum(vals):
        return [jnp.sum(v, axis=-1, keepdims=True) for v in vals]

    gl = [jnp.where(is_group, v, neg) for v in lgs]
    gmax = row_max(gl)
    gsel = first_index(gl, gmax)
    gsum = row_sum([jnp.where(is_group, jnp.exp(v - m), 0.0) for v, m in zip(lgs, gmax)])
    lo_lane = [EXPERT_LANE0 + EXPERTS_PER_GROUP * g for g in gsel]
    el = [jnp.where((lane >= lo) & (lane < lo + EXPERTS_PER_GROUP), v, neg) for v, lo in zip(lgs, lo_lane)]
    v1 = row_max(el)
    i1 = first_index(el, v1)
    el2 = [jnp.where(lane == i, neg, v) for v, i in zip(el, i1)]
    v2 = row_max(el2)
    i2 = first_index(el2, v2)
    oh1 = [lane == i for i in i1]
    oh2 = [lane == i for i in i2]

    cnt = jnp.concatenate([a.astype(F32) + b.astype(F32) for a, b in zip(oh1, oh2)], axis=0)
    r_i = lax.broadcasted_iota(jnp.int32, (tm, tm), 0)
    c_i = lax.broadcasted_iota(jnp.int32, (tm, tm), 1)
    before = jnp.dot((r_i > c_i).astype(BF16), cnt.astype(BF16), preferred_element_type=F32)
    tot = before + run_scr[0:1, :]
    tots = [tot[c * ROUTER_ROWS:(c + 1) * ROUTER_ROWS] for c in chunks]
    rank1 = row_sum([jnp.where(o, t, 0.0) for o, t in zip(oh1, tots)])
    rank2 = row_sum([jnp.where(o, t, 0.0) for o, t in zip(oh2, tots)])
    run = run_scr[0:1, :] + jnp.sum(cnt, axis=0, keepdims=True)
    run_scr[...] = jnp.broadcast_to(run, run_scr.shape)
    cnt_ref[...] = jnp.broadcast_to(run, cnt_ref.shape)

    for c in chunks:
        gp = 1.0 / gsum[c]
        e21 = jnp.exp(v2[c] - v1[c])
        ew1 = gp / (1.0 + e21)
        ew2 = gp * e21 / (1.0 + e21)
        slab = jnp.where(lane == 0, i1[c] - EXPERT_LANE0,
               jnp.where(lane == 1, i2[c] - EXPERT_LANE0,
               jnp.where(lane == 2, ew1,
               jnp.where(lane == 3, ew2,
               jnp.where(lane == 4, rank1[c],
               jnp.where(lane == 5, rank2[c], 0.0))))))
        slab_ref[c * ROUTER_ROWS:(c + 1) * ROUTER_ROWS, :] = slab


def _outproj_router(merged, w_out, x, norm_w, w_router, b_router, tm=512):
    n = x.shape[0]
    row = lambda i: (i, 0)
    const = lambda i: (0, 0)
    return pl.pallas_call(
        _outproj_router_kernel,
        grid=(n // tm,),
        in_specs=[
            pl.BlockSpec((tm, D_MODEL), row),
            pl.BlockSpec((D_MODEL, D_MODEL), const),
            pl.BlockSpec((tm, D_MODEL), row),
            pl.BlockSpec((1, D_MODEL), const),
            pl.BlockSpec((D_MODEL, ROUTER_LANES), const),
            pl.BlockSpec((1, ROUTER_LANES), const),
        ],
        out_specs=[
            pl.BlockSpec((tm, D_MODEL), row),
            pl.BlockSpec((tm * ROW_SUB, LANES), row),
            pl.BlockSpec((tm, ROUTER_LANES), row),
            pl.BlockSpec((SUBLANES, ROUTER_LANES), const),
        ],
        out_shape=[
            jax.ShapeDtypeStruct((n, D_MODEL), F32),
            jax.ShapeDtypeStruct((n * ROW_SUB, LANES), F32),
            jax.ShapeDtypeStruct((n, ROUTER_LANES), F32),
            jax.ShapeDtypeStruct((SUBLANES, ROUTER_LANES), F32),
        ],
        scratch_shapes=[pltpu.VMEM((SUBLANES, ROUTER_LANES), F32)],
        compiler_params=_cparams("arbitrary"),
        name="outproj_router",
    )(merged, w_out, x, norm_w, w_router, b_router)


def _dispatch_kernel(slot_ref, hf_ref, xb_in_hbm, xb_hbm, sem, *, tm):
    del xb_in_hbm

    def copy(u, j):
        return pltpu.make_async_copy(_slab(hf_ref, u), _slab(xb_hbm, slot_ref[TOP_K * u + j]), sem)

    def start(u, c):
        for j in range(TOP_K):
            copy(u, j).start(priority=j)
        return c

    def wait(u, c):
        for j in range(TOP_K):
            copy(u, j).wait()
        return c

    lax.fori_loop(0, tm, start, 0, unroll=8)
    lax.fori_loop(0, tm, wait, 0, unroll=8)


def _dispatch(slot_flat, hf, cap, backing, tm=256):
    n = hf.shape[0] // ROW_SUB
    xb0 = jnp.zeros((cap * ROW_SUB, LANES), F32) if backing is None else backing
    return pl.pallas_call(
        functools.partial(_dispatch_kernel, tm=tm),
        grid=(n // tm,),
        in_specs=[
            pl.BlockSpec((TOP_K * tm,), lambda i: (i,), memory_space=pltpu.SMEM),
            pl.BlockSpec((tm * ROW_SUB, LANES), lambda i: (i, 0)),
            pl.BlockSpec(memory_space=pl.ANY),
        ],
        out_specs=pl.BlockSpec(memory_space=pl.ANY),
        out_shape=jax.ShapeDtypeStruct((cap * ROW_SUB, LANES), F32),
        scratch_shapes=[pltpu.SemaphoreType.DMA(())],
        input_output_aliases={2: 0},
        compiler_params=_cparams("arbitrary"),
        name="moe_dispatch",
    )(slot_flat, hf, xb0)


def _ffn_kernel(be_ref, nu_ref, par_ref, nxt_ref, x_ref, wg_hbm, wu_hbm, wd_hbm, o_ref,
                wg_f, wu_f, wd_f, wg_s, wu_s, wd_s, sems, *, layer):
    b = pl.program_id(0)
    used = b < nu_ref[0]
    prev_e = be_ref[jnp.maximum(b - 1, 0)]
    new_expert = jnp.logical_or(b == 0, be_ref[b] != prev_e)

    def fetch(e, p):
        return [pltpu.make_async_copy(src.at[layer, e], dst.at[p], sems.at[p, i])
                for i, (src, dst) in enumerate(((wg_hbm, wg_f), (wu_hbm, wu_f), (wd_hbm, wd_f)))]

    @pl.when(jnp.logical_and(used, new_expert))
    def _():
        for p in range(2):
            @pl.when(par_ref[b] == p)
            def _():
                @pl.when(b == 0)
                def _():
                    for cp in fetch(be_ref[0], p):
                        cp.start()

                @pl.when(nxt_ref[b] >= 0)
                def _():
                    for cp in fetch(nxt_ref[b], 1 - p):
                        cp.start()

                for cp in fetch(be_ref[b], p):
                    cp.wait()
                wg_s[...] = wg_f[p].astype(BF16)
                wu_s[...] = wu_f[p].astype(BF16)
                wd_s[...] = wd_f[p].astype(BF16)

    @pl.when(used)
    def _():
        x = _rows_to_2d(x_ref, MOE_ROWS).astype(BF16)
        gt = jnp.dot(x, wg_s[...], preferred_element_type=F32)
        up = jnp.dot(x, wu_s[...], preferred_element_type=F32)
        hid = (gt * jax.nn.sigmoid(gt) * up).astype(BF16)
        _store_rows(o_ref, jnp.dot(hid, wd_s[...], preferred_element_type=F32))

    @pl.when(jnp.logical_not(used))
    def _():
        o_ref[...] = jnp.zeros_like(o_ref)


def _expert_ffn(layer, plan, xb, w_gate, w_up, w_down):
    nblk = xb.shape[0] // (MOE_ROWS * ROW_SUB)
    hbm = pl.BlockSpec(memory_space=pl.ANY)
    grid_spec = pltpu.PrefetchScalarGridSpec(
        num_scalar_prefetch=4,
        grid=(nblk,),
        in_specs=[
            pl.BlockSpec((MOE_ROWS * ROW_SUB, LANES), lambda b, be, nu, par, nxt: (jnp.minimum(b, nu[0] - 1), 0)),
            hbm, hbm, hbm,
        ],
        out_specs=pl.BlockSpec((MOE_ROWS * ROW_SUB, LANES), lambda b, be, nu, par, nxt: (b, 0)),
        scratch_shapes=[
            pltpu.VMEM((2, D_MODEL, EXPERT_HIDDEN), F32),
            pltpu.VMEM((2, D_MODEL, EXPERT_HIDDEN), F32),
            pltpu.VMEM((2, EXPERT_HIDDEN, D_MODEL), F32),
            pltpu.VMEM((D_MODEL, EXPERT_HIDDEN), BF16),
            pltpu.VMEM((D_MODEL, EXPERT_HIDDEN), BF16),
            pltpu.VMEM((EXPERT_HIDDEN, D_MODEL), BF16),
            pltpu.SemaphoreType.DMA((2, 3)),
        ],
    )
    return pl.pallas_call(
        functools.partial(_ffn_kernel, layer=layer),
        grid_spec=grid_spec,
        out_shape=jax.ShapeDtypeStruct(xb.shape, F32),
        compiler_params=_cparams("arbitrary"),
        name="expert_ffn",
    )(plan.block_expert, plan.n_used, plan.parity, plan.next_expert, xb, w_gate, w_up, w_down)


def _combine_kernel(slot_ref, slot_next_ref, x_ref, slab_ref, fw_ref, yb_hbm, o_ref,
                    buf00, buf01, buf10, buf11, sems, *, tm, n_tiles, final_norm):
    i = pl.program_id(0)
    bufs = ((buf00, buf01), (buf10, buf11))

    def copies(slots, p, u):
        return [pltpu.make_async_copy(_slab(yb_hbm, slots[TOP_K * u + j]), _slab(bufs[p][j], u), sems.at[p])
                for j in range(TOP_K)]

    def start_all(slots, p):
        def body(u, c):
            for j, cp in enumerate(copies(slots, p, u)):
                cp.start(priority=j)
            return c
        lax.fori_loop(0, tm, body, 0, unroll=8)

    def wait_all(p):
        def body(u, c):
            for cp in copies(slot_ref, p, u):
                cp.wait()
            return c
        lax.fori_loop(0, tm, body, 0, unroll=8)

    @pl.when(i == 0)
    def _():
        start_all(slot_ref, 0)

    for p in range(2):
        @pl.when(i % 2 == p)
        def _():
            @pl.when(i + 1 < n_tiles)
            def _():
                start_all(slot_next_ref, 1 - p)

            wait_all(p)
            slab = slab_ref[...]
            w1 = slab[:, 2:3]
            w2 = slab[:, 3:4]
            out = x_ref[...] + (_rows_to_2d(bufs[p][0], tm) * w1 + _rows_to_2d(bufs[p][1], tm) * w2)
            if final_norm:
                out = _rms(out, fw_ref[...])
            o_ref[...] = out


def _combine(slot_flat, x, slab, final_w, yb, final_norm, tm=256):
    n = x.shape[0]
    n_tiles = n // tm
    row = lambda i: (i, 0)
    row_buf = pltpu.VMEM((tm * ROW_SUB, LANES), F32)
    return pl.pallas_call(
        functools.partial(_combine_kernel, tm=tm, n_tiles=n_tiles, final_norm=final_norm),
        grid=(n_tiles,),
        in_specs=[
            pl.BlockSpec((TOP_K * tm,), lambda i: (i,), memory_space=pltpu.SMEM),
            pl.BlockSpec((TOP_K * tm,), lambda i: (jnp.minimum(i + 1, n_tiles - 1),), memory_space=pltpu.SMEM),
            pl.BlockSpec((tm, D_MODEL), row),
            pl.BlockSpec((tm, ROUTER_LANES), row),
            pl.BlockSpec((1, D_MODEL), lambda i: (0, 0)),
            pl.BlockSpec(memory_space=pl.ANY),
        ],
        out_specs=pl.BlockSpec((tm, D_MODEL), row),
        out_shape=jax.ShapeDtypeStruct((n, D_MODEL), F32),
        scratch_shapes=[row_buf, row_buf, row_buf, row_buf, pltpu.SemaphoreType.DMA((2,))],
        compiler_params=_cparams("arbitrary"),
        name="moe_combine",
    )(slot_flat, slot_flat, x, slab, final_w, yb)


def _pad_rows(w, rows, offset=0):
    out = jnp.zeros((rows, w.shape[1]), BF16)
    return out.at[offset:offset + w.shape[0]].set(w.astype(BF16))


def _pad_cols(vec, cols, offset=0):
    out = jnp.zeros((cols,), F32)
    return out.at[offset:offset + vec.shape[0]].set(vec.astype(F32))


class MoePlan(NamedTuple):
    slot: jax.Array
    block_expert: jax.Array
    n_used: jax.Array
    parity: jax.Array
    next_expert: jax.Array
    cap: int


def _moe_plan(slab, counts_row, n_tokens):
    eid = slab[:, 0:TOP_K].astype(jnp.int32)
    rank = slab[:, 4:4 + TOP_K].astype(jnp.int32)
    counts = counts_row[EXPERT_LANE0:EXPERT_LANE0 + N_EXPERTS].astype(jnp.int32)
    padded = (counts + MOE_ROWS - 1) // MOE_ROWS * MOE_ROWS
    pad_ends = jnp.cumsum(padded)
    pad_starts = pad_ends - padded
    slot = pad_starts[eid.reshape(-1)] + rank.reshape(-1)
    nblk = n_tokens * TOP_K // MOE_ROWS + N_EXPERTS
    block_start = jnp.arange(nblk, dtype=jnp.int32) * MOE_ROWS
    block_expert = jnp.minimum(
        jnp.sum((pad_ends[None, :] <= block_start[:, None]).astype(jnp.int32), axis=1),
        N_EXPERTS - 1).astype(jnp.int32)
    n_used = (pad_ends[-1:] // MOE_ROWS).astype(jnp.int32)
    blk = jnp.arange(nblk, dtype=jnp.int32)
    prev_expert = jnp.concatenate([block_expert[:1], block_expert[:-1]])
    first = (blk < n_used[0]) & ((blk == 0) | (block_expert != prev_expert))
    parity = ((jnp.cumsum(first.astype(jnp.int32)) - 1) % 2).astype(jnp.int32)
    later_first = first[None, :] & (blk[None, :] > blk[:, None])
    next_first = jnp.min(jnp.where(later_first, blk[None, :], nblk), axis=1)
    next_expert = jnp.where(next_first < nblk, block_expert[jnp.minimum(next_first, nblk - 1)], -1)
    return MoePlan(slot, block_expert, n_used, parity, next_expert.astype(jnp.int32), nblk * MOE_ROWS)


def kernel(x, attn_norm_w, w_in, shift_mix, attn_sinks, rwkv_w0, rwkv_w_up, rwkv_a0, rwkv_a_up, rwkv_g_up, rwkv_k_k, rwkv_k_a, rwkv_r_k, rwkv_ln_w, rwkv_ln_b, vres_down, vres_mix, vres_up, vres_v0, w_branch_a, w_branch_b, w_out, ffn_norm_w, router_group_w, router_group_b, router_expert_w, router_expert_b, expert_w_gate, expert_w_up, expert_w_down, final_norm_w):
    batch, seq, _ = x.shape
    n = batch * seq
    depth = w_in.shape[0]
    xf = x.reshape(n, D_MODEL)
    w_in_bf = jnp.swapaxes(w_in, 1, 2).astype(BF16)
    v_first = None
    slot_backing = None
    for i in range(depth):
        has_vres = i > 0
        w_gates = w_in_bf[i, GATE_COL0:, :]
        mix_r = shift_mix[i, :RKV_COLS].reshape(1, RKV_COLS)
        mix_l = _pad_cols(shift_mix[i, RKV_COLS:], LORA_COLS).reshape(1, LORA_COLS)
        v0 = vres_v0[i - 1] if has_vres else jnp.zeros((RWKV_WIDTH,), F32)
        zero = jnp.zeros((RWKV_WIDTH,), F32)
        pv_pre = jnp.stack([rwkv_w0[i], rwkv_a0[i], rwkv_k_k[i], rwkv_k_a[i], v0, zero, zero, zero]).astype(F32)
        pv_post = jnp.stack([rwkv_ln_w[i], rwkv_ln_b[i], rwkv_r_k[i].reshape(-1),
                             zero, zero, zero, zero, zero]).astype(F32)
        wup = _pad_rows(rwkv_w_up[i], LANES, 0)
        aup = _pad_rows(rwkv_a_up[i], LANES, DECAY_LORA)
        gup = _pad_rows(rwkv_g_up[i], 2 * LANES, 0)
        if has_vres:
            vd = _pad_rows(vres_down[i - 1].T, LANES, 0).T
            mix_v = _pad_cols(vres_mix[i - 1], LANES).reshape(1, LANES)
            vup = _pad_rows(vres_up[i - 1], LANES, 0)
        else:
            vd, mix_v = None, None
            vup = jnp.zeros((LANES, RWKV_WIDTH), BF16)
        w_router = jnp.concatenate(
            [router_group_w[i], router_expert_w[i],
             jnp.zeros((D_MODEL, ROUTER_LANES - N_GROUPS - N_EXPERTS), F32)], axis=1)
        b_router = _pad_cols(jnp.concatenate([router_group_b[i], router_expert_b[i]]),
                             ROUTER_LANES).reshape(1, ROUTER_LANES)

        proj = _inproj(i, xf, attn_norm_w[i].reshape(1, D_MODEL), w_in_bf, w_gates, vd)
        qkv, rkv, lora, gates = proj[:4]
        zv = proj[4] if has_vres else None
        attn = _attention(qkv, attn_sinks[i], seq)
        y, v_new = _rwkv_mixer(rkv, lora, v_first, zv, mix_r, mix_l, mix_v, pv_pre,
                               wup, aup, gup, vup, pv_post, seq)
        if not has_vres:
            v_first = v_new
        merged = _merge(i, attn, y, w_branch_a, w_branch_b, gates)

        x_mid, hf, slab, counts = _outproj_router(
            merged, w_out[i].astype(BF16), xf, ffn_norm_w[i].reshape(1, D_MODEL), w_router, b_router)
        plan = _moe_plan(slab, counts[0], n)
        xb = _dispatch(plan.slot, hf, plan.cap, slot_backing)
        yb = _expert_ffn(i, plan, xb, expert_w_gate, expert_w_up, expert_w_down)
        slot_backing = yb
        xf = _combine(plan.slot, x_mid, slab, final_norm_w.reshape(1, D_MODEL), yb, final_norm=(i == depth - 1))
    return xf.reshape(batch, seq, D_MODEL)
```

```python
import functools
from typing import NamedTuple

import jax
import jax.numpy as jnp
from jax import lax
from jax.experimental import pallas as pl
from jax.experimental.pallas import tpu as pltpu

F32 = jnp.float32
BF16 = jnp.bfloat16

D_MODEL = 2048
HEAD_DIM = 64
N_Q_HEADS = 16
N_KV_HEADS = 4
GQA_GROUP = N_Q_HEADS // N_KV_HEADS
ATTN_WIDTH = N_Q_HEADS * HEAD_DIM
KV_WIDTH = N_KV_HEADS * HEAD_DIM
WINDOW = 128
ATTN_BLOCK = 128
ATTN_SUB = 2

RWKV_HEAD = 64
N_RWKV_HEADS = 16
RWKV_WIDTH = N_RWKV_HEADS * RWKV_HEAD
DECAY_LORA = 64
AAA_LORA = 64
MV_LORA = 32
GATE_LORA = 160
RWKV_GN_EPS = 64e-5
LORA_WIDTH = DECAY_LORA + AAA_LORA + GATE_LORA

N_GROUPS = 4
EXPERTS_PER_GROUP = 8
N_EXPERTS = N_GROUPS * EXPERTS_PER_GROUP
TOP_K = 2
EXPERT_HIDDEN = D_MODEL // 4
NORM_EPS = 1e-5

LANES = 128
SUBLANES = 8
MXU_COLS = 256

SEG = 512
QKV_COLS = ATTN_WIDTH + 2 * KV_WIDTH
RKV_COLS = 3 * RWKV_WIDTH
LORA_COLS = SEG
GATE_COLS = 2 * D_MODEL
QKV_TILES = QKV_COLS // SEG
RKV_TILES = RKV_COLS // SEG
LORA_TILES = LORA_COLS // SEG
GATE_TILES = GATE_COLS // SEG
IN_TILES = QKV_TILES + RKV_TILES + LORA_TILES + GATE_TILES
FIRST_GATE_TILE = QKV_TILES + RKV_TILES + LORA_TILES
ZERO_STEPS = 16
GATE_COL0 = QKV_COLS + RKV_COLS + LORA_WIDTH

SCAN_CHUNK = 64
SCAN_SUB = 2
SCAN_GROUP = 4
MOE_ROWS = 256
ROUTER_LANES = LANES
ROUTER_ROWS = 128
EXPERT_LANE0 = N_GROUPS

ROW_SUB = D_MODEL // LANES

VMEM_LIMIT = 56 * 1024 * 1024


def _rows_to_2d(slab_ref, rows):
    return jnp.concatenate([slab_ref[pl.ds(s, rows, stride=ROW_SUB), :] for s in range(ROW_SUB)], axis=1)


def _store_rows(slab_ref, val):
    rows = val.shape[0]
    for s in range(ROW_SUB):
        slab_ref[pl.ds(s, rows, stride=ROW_SUB), :] = val[:, s * LANES:(s + 1) * LANES]


def _slab(ref, row):
    return ref.at[pl.ds(pl.multiple_of(row * ROW_SUB, ROW_SUB), ROW_SUB)]


def _cparams(*sem):
    return pltpu.CompilerParams(dimension_semantics=sem, vmem_limit_bytes=VMEM_LIMIT)


def _bdot(a, b):
    return jnp.dot(a.astype(BF16), b.astype(BF16), preferred_element_type=F32)


def _bdot_nt(a, b):
    return lax.dot_general(a.astype(BF16), b.astype(BF16), (((1,), (1,)), ((), ())),
                           preferred_element_type=F32)


def _bdot_tn(a, b):
    return lax.dot_general(a.astype(BF16), b.astype(BF16), (((0,), (0,)), ((), ())),
                           preferred_element_type=F32)


def _dot_split_rhs(w01, x, passes):
    acc, rem = None, x
    for p in range(passes):
        part = rem.astype(BF16)
        d = jnp.dot(w01, part, preferred_element_type=F32)
        acc = d if acc is None else acc + d
        if p + 1 < passes:
            rem = rem - part.astype(F32)
    return acc


def _dot_split_lhs(x, w01, passes):
    acc, rem = None, x
    for p in range(passes):
        part = rem.astype(BF16)
        d = jnp.dot(part, w01, preferred_element_type=F32)
        acc = d if acc is None else acc + d
        if p + 1 < passes:
            rem = rem - part.astype(F32)
    return acc


def _rms(x, w):
    ms = jnp.mean(x * x, axis=-1, keepdims=True)
    return x * lax.rsqrt(ms + NORM_EPS) * w


def _inproj_kernel(*refs, has_vres, has_zeros):
    refs = list(refs)
    x_ref, nw_ref, w_ref, wg_ref = refs[:4]
    refs = refs[4:]
    vd_ref = refs.pop(0) if has_vres else None
    qkv_ref, rkv_ref, lora_ref, gate_ref = refs[:4]
    refs = refs[4:]
    zv_ref = refs.pop(0) if has_vres else None
    zeros_ref = refs.pop(0) if has_zeros else None
    (h_scr,) = refs
    j = pl.program_id(1)

    if has_zeros:
        zeros_ref[...] = jnp.zeros_like(zeros_ref)

    @pl.when(j == 0)
    def _():
        h_scr[...] = _rms(x_ref[...], nw_ref[...]).astype(BF16)
        if has_vres:
            zv_ref[...] = jnp.dot(h_scr[...], vd_ref[...], preferred_element_type=F32)

    def project(wt_ref, out_ref):
        for c in range(SEG // MXU_COLS):
            cs = slice(c * MXU_COLS, (c + 1) * MXU_COLS)
            acc = lax.dot_general(h_scr[...], wt_ref[cs, :], (((1,), (1,)), ((), ())),
                                  preferred_element_type=F32)
            out_ref[:, cs] = acc.astype(out_ref.dtype)

    @pl.when(j < QKV_TILES)
    def _():
        project(w_ref, qkv_ref)

    @pl.when((j >= QKV_TILES) & (j < QKV_TILES + RKV_TILES))
    def _():
        project(w_ref, rkv_ref)

    @pl.when(j == QKV_TILES + RKV_TILES)
    def _():
        project(w_ref, lora_ref)

    @pl.when(j >= FIRST_GATE_TILE)
    def _():
        project(wg_ref, gate_ref)


def _inproj(layer, x, norm_w, w_in, w_gates, vd, zero_rows=0, tm=1024):
    n = x.shape[0]
    has_vres = vd is not None
    s_rkv = QKV_TILES
    in_specs = [
        pl.BlockSpec((tm, D_MODEL), lambda i, j: (i, 0)),
        pl.BlockSpec((1, D_MODEL), lambda i, j: (0, 0)),
        pl.BlockSpec((None, SEG, D_MODEL), lambda i, j: (layer, jnp.minimum(j, FIRST_GATE_TILE - 1), 0)),
        pl.BlockSpec((SEG, D_MODEL), lambda i, j: (jnp.clip(j - FIRST_GATE_TILE, 0, GATE_TILES - 1), 0)),
    ]
    args = [x, norm_w, w_in, w_gates]
    out_specs = [
        pl.BlockSpec((tm, SEG), lambda i, j: (i, jnp.clip(j, 0, QKV_TILES - 1))),
        pl.BlockSpec((tm, SEG), lambda i, j: (i, jnp.clip(j - s_rkv, 0, RKV_TILES - 1))),
        pl.BlockSpec((tm, SEG), lambda i, j: (i, 0)),
        pl.BlockSpec((tm, SEG), lambda i, j: (i, jnp.clip(j - FIRST_GATE_TILE, 0, GATE_TILES - 1))),
    ]
    out_shape = [
        jax.ShapeDtypeStruct((n, QKV_COLS), BF16),
        jax.ShapeDtypeStruct((n, RKV_COLS), F32),
        jax.ShapeDtypeStruct((n, LORA_COLS), F32),
        jax.ShapeDtypeStruct((n, GATE_COLS), BF16),
    ]
    if has_vres:
        in_specs.append(pl.BlockSpec((D_MODEL, LANES), lambda i, j: (0, 0)))
        args.append(vd)
        out_specs.append(pl.BlockSpec((tm, LANES), lambda i, j: (i, 0)))
        out_shape.append(jax.ShapeDtypeStruct((n, LANES), F32))
    if zero_rows:
        blk = zero_rows // ((n // tm) * ZERO_STEPS)
        assert blk * (n // tm) * ZERO_STEPS == zero_rows and blk % SUBLANES == 0
        out_specs.append(pl.BlockSpec((blk, LANES), lambda i, j: (i * ZERO_STEPS + jnp.minimum(j, ZERO_STEPS - 1), 0)))
        out_shape.append(jax.ShapeDtypeStruct((zero_rows, LANES), F32))
    return pl.pallas_call(
        functools.partial(_inproj_kernel, has_vres=has_vres, has_zeros=bool(zero_rows)),
        grid=(n // tm, IN_TILES),
        in_specs=in_specs,
        out_specs=out_specs,
        out_shape=out_shape,
        scratch_shapes=[pltpu.VMEM((tm, D_MODEL), BF16)],
        compiler_params=_cparams("parallel", "arbitrary"),
        name="inproj",
    )(*args)


def _attn_kernel(sink_ref, q_ref, kp_ref, kc_ref, vp_ref, vc_ref, bias_ref, o_ref, *, tiles_per_seq):
    first = (pl.program_id(0) % tiles_per_seq) == 0
    col = lax.broadcasted_iota(jnp.int32, (ATTN_BLOCK, 2 * ATTN_BLOCK), 1)
    pad_keys = jnp.logical_and(first, col < ATTN_BLOCK)
    scale = HEAD_DIM ** -0.5
    items = [(sb, hk) for sb in range(ATTN_SUB) for hk in range(N_KV_HEADS)]

    def window(prev_ref, cur_ref, sb, hk):
        ks = slice(hk * HEAD_DIM, (hk + 1) * HEAD_DIM)
        rows = slice(sb * ATTN_BLOCK, (sb + 1) * ATTN_BLOCK)
        prev = prev_ref[:, ks] if sb == 0 else cur_ref[(sb - 1) * ATTN_BLOCK:sb * ATTN_BLOCK, ks]
        return jnp.concatenate([prev, cur_ref[rows, ks]], axis=0)

    scores = {}
    for sb, hk in items:
        kw = window(kp_ref, kc_ref, sb, hk)
        for g in range(GQA_GROUP):
            h = hk * GQA_GROUP + g
            qh = q_ref[sb * ATTN_BLOCK:(sb + 1) * ATTN_BLOCK, h * HEAD_DIM:(h + 1) * HEAD_DIM]
            s = lax.dot_general(qh, kw, (((1,), (1,)), ((), ())), preferred_element_type=F32)
            s = s * scale + bias_ref[h]
            if sb == 0:
                s = jnp.where(pad_keys, -jnp.inf, s)
            scores[(sb, h)] = s
    row_max = {key: jnp.max(s, axis=-1, keepdims=True) for key, s in scores.items()}
    m = {key: jnp.maximum(row_max[key], sink_ref[key[1]]) for key in scores}
    p = {key: jnp.exp(scores[key] - m[key]) for key in scores}
    row_sum = {key: jnp.sum(p[key], axis=-1, keepdims=True) for key in scores}
    denoms = {key: row_sum[key] + jnp.exp(sink_ref[key[1]] - m[key]) for key in scores}
    probs = {key: p[key].astype(BF16) for key in scores}
    for sb, hk in items:
        vw = window(vp_ref, vc_ref, sb, hk)
        for g in range(GQA_GROUP):
            h = hk * GQA_GROUP + g
            o = jnp.dot(probs[(sb, h)], vw, preferred_element_type=F32) / denoms[(sb, h)]
            o_ref[sb * ATTN_BLOCK:(sb + 1) * ATTN_BLOCK, h * HEAD_DIM:(h + 1) * HEAD_DIM] = o.astype(BF16)


def _attn_bias():
    qi = jnp.arange(ATTN_BLOCK)[:, None]
    kj = jnp.arange(2 * ATTN_BLOCK)[None, :]
    dist = qi + ATTN_BLOCK - kj
    valid = (dist >= 0) & (dist < WINDOW)
    slopes = jnp.exp2(-8.0 * jnp.arange(1, N_Q_HEADS + 1, dtype=F32) / N_Q_HEADS)
    bias = -slopes[:, None, None] * dist.astype(F32)[None]
    return jnp.where(valid[None], bias, -jnp.inf)


def _attention(qkv, sinks, seq):
    n = qkv.shape[0]
    rows = ATTN_SUB * ATTN_BLOCK
    tiles_per_seq = seq // rows
    kcol = ATTN_WIDTH // KV_WIDTH
    prev = lambda i: jnp.maximum(i * ATTN_SUB - 1, 0)
    return pl.pallas_call(
        functools.partial(_attn_kernel, tiles_per_seq=tiles_per_seq),
        grid=(n // rows,),
        in_specs=[
            pl.BlockSpec(memory_space=pltpu.SMEM),
            pl.BlockSpec((rows, ATTN_WIDTH), lambda i: (i, 0)),
            pl.BlockSpec((ATTN_BLOCK, KV_WIDTH), lambda i: (prev(i), kcol)),
            pl.BlockSpec((rows, KV_WIDTH), lambda i: (i, kcol)),
            pl.BlockSpec((ATTN_BLOCK, KV_WIDTH), lambda i: (prev(i), kcol + 1)),
            pl.BlockSpec((rows, KV_WIDTH), lambda i: (i, kcol + 1)),
            pl.BlockSpec((N_Q_HEADS, ATTN_BLOCK, 2 * ATTN_BLOCK), lambda i: (0, 0, 0)),
        ],
        out_specs=pl.BlockSpec((rows, ATTN_WIDTH), lambda i: (i, 0)),
        out_shape=jax.ShapeDtypeStruct((n, ATTN_WIDTH), BF16),
        compiler_params=_cparams("parallel"),
        name="swa_attention",
    )(sinks.astype(F32), qkv, qkv, qkv, qkv, qkv, _attn_bias())


def _head_ones():
    ch = jnp.arange(MXU_COLS) // RWKV_HEAD
    return (ch[:, None] == ch[None, :]).astype(BF16)


def _head_sums(x, ones, passes):
    parts = [_dot_split_lhs(x[:, c:c + MXU_COLS], ones, passes) for c in range(0, RWKV_WIDTH, MXU_COLS)]
    return jnp.concatenate(parts, axis=1)


def _rwkv_kernel(*refs, has_vres):
    refs = list(refs)
    rkv_ref, lo_ref = refs[:2]
    refs = refs[2:]
    if has_vres:
        vf_ref, zv_ref, mixv_ref = refs[:3]
        refs = refs[3:]
    mixr_ref, mixl_ref, pv_ref, wup_ref, aup_ref, gup_ref, vup_ref, ones_ref, pp_ref = refs[:9]
    refs = refs[9:]
    y_ref = refs.pop(0)
    v_out = None if has_vres else refs.pop(0)
    s_scr, prev_rkv, prev_lo = refs[:3]
    prev_zv = refs[3] if has_vres else None

    L = SCAN_CHUNK
    W = SCAN_GROUP * RWKV_HEAD
    n_seq, step_rows = rkv_ref.shape[0], rkv_ref.shape[1]
    n_grp = N_RWKV_HEADS // SCAN_GROUP
    ones = ones_ref[...]

    @pl.when(pl.program_id(0) == 0)
    def _():
        s_scr[...] = jnp.zeros_like(s_scr)
        prev_rkv[...] = jnp.zeros_like(prev_rkv)
        prev_lo[...] = jnp.zeros_like(prev_lo)
        if has_vres:
            prev_zv[...] = jnp.zeros_like(prev_zv)

    def lerp_shift(z_ref, carry_ref, b, mix):
        z = z_ref[b]
        row = lax.broadcasted_iota(jnp.int32, z.shape, 0)
        back = jnp.where(row == 0, carry_ref[b, SUBLANES - 1:SUBLANES, :], pltpu.roll(z, 1, 0))
        carry_ref[b] = z_ref[b, step_rows - SUBLANES:step_rows, :]
        return z + (back - z) * mix

    w0, a0, k_k, k_a, v0 = (pv_ref[i:i + 1, :] for i in range(5))
    pre = []
    for b in range(n_seq):
        zs = lerp_shift(rkv_ref, prev_rkv, b, mixr_ref[...])
        los = lerp_shift(lo_ref, prev_lo, b, mixl_ref[...])
        r = zs[:, :RWKV_WIDTH]
        kr = zs[:, RWKV_WIDTH:2 * RWKV_WIDTH]
        vr = zs[:, 2 * RWKV_WIDTH:]
        wa = los[:, :LANES]
        col = lax.broadcasted_iota(jnp.int32, wa.shape, 1)
        wa_in = jnp.where(col < DECAY_LORA, jnp.tanh(wa), wa).astype(BF16)
        gd = los[:, LANES:3 * LANES]
        col_g = lax.broadcasted_iota(jnp.int32, gd.shape, 1)
        g_in = jnp.where(col_g < GATE_LORA, jax.nn.sigmoid(gd), gd).astype(BF16)
        dw = jnp.dot(wa_in, wup_ref[...], preferred_element_type=F32)
        da = jnp.dot(wa_in, aup_ref[...], preferred_element_type=F32)
        g = jnp.dot(g_in, gup_ref[...], preferred_element_type=F32)
        u = -(w0 + dw)
        softplus = jnp.maximum(u, 0.0) + jnp.log1p(jnp.exp(-jnp.abs(u)))
        w_log = -softplus - 0.5
        a = jax.nn.sigmoid(a0 + da)
        if has_vres:
            zvs = lerp_shift(zv_ref, prev_zv, b, mixv_ref[...])
            dv = jnp.dot(zvs.astype(BF16), vup_ref[...], preferred_element_type=F32)
            vr = vr + (vf_ref[b] - vr) * jax.nn.sigmoid(v0 + dv)
        else:
            v_out[b] = vr
        kk0 = kr * k_k
        ss = _head_sums(kk0 * kk0, ones, 2)
        kk = kk0 * (1.0 / jnp.maximum(jnp.sqrt(ss), 1e-12))
        pre.append(dict(r=r, lw=-jnp.exp(w_log),
                        k=kr * (1.0 + (a - 1.0) * k_a), v=vr, al=-kk, be=kk * a, g=g))

    row = lax.broadcasted_iota(jnp.int32, (L, L), 0)
    col = lax.broadcasted_iota(jnp.int32, (L, L), 1)
    tri = (row >= col).astype(BF16)
    t_w = lax.broadcasted_iota(jnp.int32, (L, W), 0)
    lane_w = lax.broadcasted_iota(jnp.int32, (L, W), 1)
    s_w = lane_w % RWKV_HEAD
    strict = t_w > s_w
    incl = t_w >= s_w
    eye = (t_w == s_w).astype(F32)
    blk_w = lane_w // RWKV_HEAD
    bd_mask = (lax.broadcasted_iota(jnp.int32, (W, W), 0) // RWKV_HEAD
               == lax.broadcasted_iota(jnp.int32, (W, W), 1) // RWKV_HEAD)

    def blockdiag(w):
        tiled = jnp.concatenate([w.astype(BF16)] * SCAN_GROUP, axis=0)
        return jnp.where(bd_mask, tiled, jnp.zeros_like(tiled))

    ln_w, ln_b, r_k = (pp_ref[i:i + 1, :] for i in range(3))
    items = [(b, g) for b in range(n_seq) for g in range(n_grp)]
    for sub in range(step_rows // L):
        rows = slice(sub * L, (sub + 1) * L)
        ops = {}
        for b in range(n_seq):
            p = pre[b]
            lw = p['lw'][rows]
            cum = _dot_split_rhs(tri, lw, 3)
            p_inc = jnp.exp(cum)
            p_exc = jnp.exp(cum - lw)
            p_inv = jnp.exp(-cum)
            p_last = p_inc[L - 1:L, :]
            a_t = p['al'][rows] * p_exc
            r_t = p['r'][rows] * p_inc
            b_t = p['be'][rows] * p_inv
            k_t = p['k'][rows] * p_inv
            b_end = b_t * p_last
            k_end = k_t * p_last
            v = p['v'][rows]
            for g in range(n_grp):
                gs = slice(g * W, (g + 1) * W)
                ops[(b, g)] = dict(
                    ar=jnp.concatenate([a_t[:, gs], r_t[:, gs]], axis=0),
                    bt=b_t[:, gs], kt=k_t[:, gs], v=v[:, gs], p_last=p_last[:, gs],
                    bke=jnp.concatenate([b_end[:, gs], k_end[:, gs]], axis=0))

        gm, xs, av, s0 = {}, {}, {}, {}
        for it in items:
            o = ops[it]
            rhs = jnp.concatenate([blockdiag(o['bt']), blockdiag(o['kt'])], axis=0)
            gm[it] = _bdot_nt(o['ar'], rhs)
            s0[it] = s_scr[it[0], it[1]]
            xs[it] = _bdot_nt(o['ar'], blockdiag(s0[it]))
        a_ab, a_rb = {}, {}
        for it in items:
            g = gm[it]
            a_ab[it] = jnp.where(strict, g[:L, :W], 0.0)
            a_rb[it] = jnp.where(incl, g[L:, :W], 0.0)
            a_kk = jnp.concatenate([jnp.where(strict, g[:L, W:], 0.0), jnp.where(incl, g[L:, W:], 0.0)], axis=0)
            av[it] = _bdot(a_kk, blockdiag(ops[it]['v']))

        pw = {it: _bdot(a_ab[it], blockdiag(a_ab[it])) for it in items}
        tinv = {it: eye + a_ab[it] for it in items}
        span = 2
        while 2 * span < L:
            both = {it: _bdot(jnp.concatenate([pw[it], tinv[it]], axis=0), blockdiag(pw[it])) for it in items}
            tinv = {it: tinv[it] + both[it][L:] for it in items}
            pw = {it: both[it][:L] for it in items}
            span *= 2
        tinv = {it: tinv[it] + _bdot(tinv[it], blockdiag(pw[it])) for it in items}

        u = {it: _bdot(tinv[it], blockdiag(xs[it][:L] + av[it][:L])) for it in items}
        y = {it: xs[it][L:] + av[it][L:] + _bdot(a_rb[it], blockdiag(u[it])) for it in items}
        for it in items:
            o = ops[it]
            full = _bdot_tn(jnp.concatenate([u[it], o['v']], axis=0), o['bke'])
            upd = full[:RWKV_HEAD]
            for h in range(1, SCAN_GROUP):
                upd = jnp.where(blk_w == h, full[h * RWKV_HEAD:(h + 1) * RWKV_HEAD], upd)
            s_scr[it[0], it[1]] = s0[it] * o['p_last'] + upd

        inv_n = 1.0 / RWKV_HEAD
        for b in range(n_seq):
            p = pre[b]
            yb = jnp.concatenate([y[(b, g)] for g in range(n_grp)], axis=1)
            d = yb - _head_sums(yb, ones, 2) * inv_n
            var = _head_sums(d * d, ones, 2) * inv_n
            yn = d * lax.rsqrt(var + RWKV_GN_EPS) * ln_w + ln_b
            bonus = _head_sums(p['r'][rows] * p['k'][rows] * r_k, ones, 2) * p['v'][rows]
            y_ref[b, rows, :] = ((yn + bonus) * p['g'][rows]).astype(BF16)


def _rwkv_mixer(rkv, lora, v_first, zv, mix_r, mix_l, mix_v, pv_pre, wup, aup, gup, vup, pv_post, seq):
    n = rkv.shape[0]
    n_seq = n // seq
    has_vres = v_first is not None
    step_rows = SCAN_SUB * SCAN_CHUNK
    timed = lambda cols: pl.BlockSpec((n_seq, step_rows, cols), lambda t: (0, t, 0))
    const = lambda shape: pl.BlockSpec(shape, lambda t: (0, 0))
    by_seq = lambda a: a.reshape(n_seq, seq, a.shape[-1])
    in_specs = [timed(RKV_COLS), timed(LORA_COLS)]
    args = [by_seq(rkv), by_seq(lora)]
    if has_vres:
        in_specs += [timed(RWKV_WIDTH), timed(LANES), const((1, LANES))]
        args += [by_seq(v_first), by_seq(zv), mix_v]
    in_specs += [
        const((1, RKV_COLS)), const((1, LORA_COLS)), const((SUBLANES, RWKV_WIDTH)),
        const((LANES, RWKV_WIDTH)), const((LANES, RWKV_WIDTH)), const((2 * LANES, RWKV_WIDTH)),
        const((LANES, RWKV_WIDTH)), const((MXU_COLS, MXU_COLS)), const((SUBLANES, RWKV_WIDTH)),
    ]
    args += [mix_r, mix_l, pv_pre, wup, aup, gup, vup, _head_ones(), pv_post]
    out_specs = [timed(RWKV_WIDTH)]
    out_shape = [jax.ShapeDtypeStruct((n_seq, seq, RWKV_WIDTH), BF16)]
    if not has_vres:
        out_specs.append(timed(RWKV_WIDTH))
        out_shape.append(jax.ShapeDtypeStruct((n_seq, seq, RWKV_WIDTH), F32))
    scratch = [
        pltpu.VMEM((n_seq, N_RWKV_HEADS // SCAN_GROUP, RWKV_HEAD, SCAN_GROUP * RWKV_HEAD), F32),
        pltpu.VMEM((n_seq, SUBLANES, RKV_COLS), F32),
        pltpu.VMEM((n_seq, SUBLANES, LORA_COLS), F32),
    ]
    if has_vres:
        scratch.append(pltpu.VMEM((n_seq, SUBLANES, LANES), F32))
    outs = pl.pallas_call(
        functools.partial(_rwkv_kernel, has_vres=has_vres),
        grid=(seq // step_rows,),
        in_specs=in_specs,
        out_specs=out_specs,
        out_shape=out_shape,
        scratch_shapes=scratch,
        compiler_params=_cparams("arbitrary"),
        name="rwkv_mixer",
    )(*args)
    y = outs[0].reshape(n, RWKV_WIDTH)
    return y, (None if has_vres else outs[1].reshape(n, RWKV_WIDTH))


def _merge_kernel(a_ref, y_ref, wa_ref, wb_ref, ga_ref, gb_ref, o_ref):
    for c in range(o_ref.shape[1] // MXU_COLS):
        cs = slice(c * MXU_COLS, (c + 1) * MXU_COLS)
        pa = jnp.dot(a_ref[...], wa_ref[:, cs].astype(BF16), preferred_element_type=F32)
        pb = jnp.dot(y_ref[...], wb_ref[:, cs].astype(BF16), preferred_element_type=F32)
        ga = jax.nn.sigmoid(ga_ref[:, cs].astype(F32))
        gb = jax.nn.sigmoid(gb_ref[:, cs].astype(F32))
        o_ref[:, cs] = (ga * pa + gb * pb).astype(BF16)


def _merge(layer, attn, y, wa, wb, gates, tm=1024, tn=1024):
    n = attn.shape[0]
    nj = D_MODEL // tn
    return pl.pallas_call(
        _merge_kernel,
        grid=(n // tm, nj),
        in_specs=[
            pl.BlockSpec((tm, ATTN_WIDTH), lambda i, j: (i, 0)),
            pl.BlockSpec((tm, RWKV_WIDTH), lambda i, j: (i, 0)),
            pl.BlockSpec((None, ATTN_WIDTH, tn), lambda i, j: (layer, 0, j)),
            pl.BlockSpec((None, RWKV_WIDTH, tn), lambda i, j: (layer, 0, j)),
            pl.BlockSpec((tm, tn), lambda i, j: (i, j)),
            pl.BlockSpec((tm, tn), lambda i, j: (i, j + nj)),
        ],
        out_specs=pl.BlockSpec((tm, tn), lambda i, j: (i, j)),
        out_shape=jax.ShapeDtypeStruct((n, D_MODEL), BF16),
        compiler_params=_cparams("parallel", "arbitrary"),
        name="gated_merge",
    )(attn, y, wa, wb, gates, gates)


def _outproj_router_kernel(m_ref, wo_ref, x_ref, nw_ref, wr_ref, br_ref,
                           xo_ref, hf_ref, slab_ref, cnt_ref, run_scr):
    @pl.when(pl.program_id(0) == 0)
    def _():
        run_scr[...] = jnp.zeros_like(run_scr)

    tm = x_ref.shape[0]
    sq = jnp.zeros((tm, LANES), F32)
    for c in range(D_MODEL // MXU_COLS):
        cs = slice(c * MXU_COLS, (c + 1) * MXU_COLS)
        xn_c = x_ref[:, cs] + jnp.dot(m_ref[...], wo_ref[:, cs], preferred_element_type=F32)
        xo_ref[:, cs] = xn_c
        for l in range(MXU_COLS // LANES):
            piece = xn_c[:, l * LANES:(l + 1) * LANES]
            sq = sq + piece * piece
    ms = jnp.sum(sq, axis=-1, keepdims=True) * (1.0 / D_MODEL)
    hf = xo_ref[...] * lax.rsqrt(ms + NORM_EPS) * nw_ref[...]
    _store_rows(hf_ref, hf)

    h_hi = hf.astype(BF16)
    h_lo = (hf - h_hi.astype(F32)).astype(BF16)
    wr = wr_ref[...]
    w_hi = wr.astype(BF16)
    w_lo = (wr - w_hi.astype(F32)).astype(BF16)
    lg = (jnp.dot(h_hi, w_hi, preferred_element_type=F32)
          + jnp.dot(h_hi, w_lo, preferred_element_type=F32)
          + jnp.dot(h_lo, w_hi, preferred_element_type=F32)) + br_ref[...]

    chunks = range(tm // ROUTER_ROWS)
    lgs = [lg[c * ROUTER_ROWS:(c + 1) * ROUTER_ROWS] for c in chunks]
    lane = lax.broadcasted_iota(jnp.int32, (ROUTER_ROWS, ROUTER_LANES), 1).astype(F32)
    neg = -jnp.inf
    big = float(ROUTER_LANES)
    is_group = lane < N_GROUPS

    def row_max(vals):
        return [jnp.max(v, axis=-1, keepdims=True) for v in vals]

    def first_index(vals, mx):
        return [jnp.min(jnp.where(v == m, lane, big), axis=-1, keepdims=True) for v, m in zip(vals, mx)]

    def row_sum(vals):
        return [jnp.sum(v, axis=-1, keepdims=True) for v in vals]

    gl = [jnp.where(is_group, v, neg) for v in lgs]
    gmax = row_max(gl)
    gsel = first_index(gl, gmax)
    gsum = row_sum([jnp.where(is_group, jnp.exp(v - m), 0.0) for v, m in zip(lgs, gmax)])
    lo_lane = [EXPERT_LANE0 + EXPERTS_PER_GROUP * g for g in gsel]
    el = [jnp.where((lane >= lo) & (lane < lo + EXPERTS_PER_GROUP), v, neg) for v, lo in zip(lgs, lo_lane)]
    v1 = row_max(el)
    i1 = first_index(el, v1)
    el2 = [jnp.where(lane == i, neg, v) for v, i in zip(el, i1)]
    v2 = row_max(el2)
    i2 = first_index(el2, v2)
    oh1 = [lane == i for i in i1]
    oh2 = [lane == i for i in i2]

    cnt = jnp.concatenate([a.astype(F32) + b.astype(F32) for a, b in zip(oh1, oh2)], axis=0)
    r_i = lax.broadcasted_iota(jnp.int32, (tm, tm), 0)
    c_i = lax.broadcasted_iota(jnp.int32, (tm, tm), 1)
    before = jnp.dot((r_i > c_i).astype(BF16), cnt.astype(BF16), preferred_element_type=F32)
    tot = before + run_scr[0:1, :]
    tots = [tot[c * ROUTER_ROWS:(c + 1) * ROUTER_ROWS] for c in chunks]
    rank1 = row_sum([jnp.where(o, t, 0.0) for o, t in zip(oh1, tots)])
    rank2 = row_sum([jnp.where(o, t, 0.0) for o, t in zip(oh2, tots)])
    run = run_scr[0:1, :] + jnp.sum(cnt, axis=0, keepdims=True)
    run_scr[...] = jnp.broadcast_to(run, run_scr.shape)
    cnt_ref[...] = jnp.broadcast_to(run, cnt_ref.shape)

    for c in chunks:
        gp = 1.0 / gsum[c]
        e21 = jnp.exp(v2[c] - v1[c])
        ew1 = gp / (1.0 + e21)
        ew2 = gp * e21 / (1.0 + e21)
        slab = jnp.where(lane == 0, i1[c] - EXPERT_LANE0,
               jnp.where(lane == 1, i2[c] - EXPERT_LANE0,
               jnp.where(lane == 2, ew1,
               jnp.where(lane == 3, ew2,
               jnp.where(lane == 4, rank1[c],
               jnp.where(lane == 5, rank2[c], 0.0))))))
        slab_ref[c * ROUTER_ROWS:(c + 1) * ROUTER_ROWS, :] = slab


def _outproj_router(merged, w_out, x, norm_w, w_router, b_router, tm=512):
    n = x.shape[0]
    row = lambda i: (i, 0)
    const = lambda i: (0, 0)
    return pl.pallas_call(
        _outproj_router_kernel,
        grid=(n // tm,),
        in_specs=[
            pl.BlockSpec((tm, D_MODEL), row),
            pl.BlockSpec((D_MODEL, D_MODEL), const),
            pl.BlockSpec((tm, D_MODEL), row),
            pl.BlockSpec((1, D_MODEL), const),
            pl.BlockSpec((D_MODEL, ROUTER_LANES), const),
            pl.BlockSpec((1, ROUTER_LANES), const),
        ],
        out_specs=[
            pl.BlockSpec((tm, D_MODEL), row),
            pl.BlockSpec((tm * ROW_SUB, LANES), row),
            pl.BlockSpec((tm, ROUTER_LANES), row),
            pl.BlockSpec((SUBLANES, ROUTER_LANES), const),
        ],
        out_shape=[
            jax.ShapeDtypeStruct((n, D_MODEL), F32),
            jax.ShapeDtypeStruct((n * ROW_SUB, LANES), F32),
            jax.ShapeDtypeStruct((n, ROUTER_LANES), F32),
            jax.ShapeDtypeStruct((SUBLANES, ROUTER_LANES), F32),
        ],
        scratch_shapes=[pltpu.VMEM((SUBLANES, ROUTER_LANES), F32)],
        compiler_params=_cparams("arbitrary"),
        name="outproj_router",
    )(merged, w_out, x, norm_w, w_router, b_router)


def _dispatch_kernel(slot_ref, hf_ref, xb_in_hbm, xb_hbm, sem, *, tm):
    del xb_in_hbm

    def copy(u, j):
        return pltpu.make_async_copy(_slab(hf_ref, u), _slab(xb_hbm, slot_ref[TOP_K * u + j]), sem)

    def start(u, c):
        for j in range(TOP_K):
            copy(u, j).start(priority=j)
        return c

    def wait(u, c):
        for j in range(TOP_K):
            copy(u, j).wait()
        return c

    lax.fori_loop(0, tm, start, 0, unroll=8)
    lax.fori_loop(0, tm, wait, 0, unroll=8)


def _dispatch(slot_flat, hf, cap, backing, tm=256):
    n = hf.shape[0] // ROW_SUB
    assert backing.shape == (cap * ROW_SUB, LANES)
    return pl.pallas_call(
        functools.partial(_dispatch_kernel, tm=tm),
        grid=(n // tm,),
        in_specs=[
            pl.BlockSpec((TOP_K * tm,), lambda i: (i,), memory_space=pltpu.SMEM),
            pl.BlockSpec((tm * ROW_SUB, LANES), lambda i: (i, 0)),
            pl.BlockSpec(memory_space=pl.ANY),
        ],
        out_specs=pl.BlockSpec(memory_space=pl.ANY),
        out_shape=jax.ShapeDtypeStruct((cap * ROW_SUB, LANES), F32),
        scratch_shapes=[pltpu.SemaphoreType.DMA(())],
        input_output_aliases={2: 0},
        compiler_params=_cparams("arbitrary"),
        name="moe_dispatch",
    )(slot_flat, hf, backing)


def _ffn_kernel(be_ref, nu_ref, par_ref, nxt_ref, x_ref, wg_hbm, wu_hbm, wd_hbm, o_ref,
                wg_f, wu_f, wd_f, wg_s, wu_s, wd_s, sems, *, layer):
    b = pl.program_id(0)
    used = b < nu_ref[0]
    prev_e = be_ref[jnp.maximum(b - 1, 0)]
    new_expert = jnp.logical_or(b == 0, be_ref[b] != prev_e)

    def fetch(e, p):
        return [pltpu.make_async_copy(src.at[layer, e], dst.at[p], sems.at[p, i])
                for i, (src, dst) in enumerate(((wg_hbm, wg_f), (wu_hbm, wu_f), (wd_hbm, wd_f)))]

    @pl.when(jnp.logical_and(used, new_expert))
    def _():
        for p in range(2):
            @pl.when(par_ref[b] == p)
            def _():
                @pl.when(b == 0)
                def _():
                    for cp in fetch(be_ref[0], p):
                        cp.start()

                @pl.when(nxt_ref[b] >= 0)
                def _():
                    for cp in fetch(nxt_ref[b], 1 - p):
                        cp.start()

                for cp in fetch(be_ref[b], p):
                    cp.wait()
                wg_s[...] = wg_f[p].astype(BF16)
                wu_s[...] = wu_f[p].astype(BF16)
                wd_s[...] = wd_f[p].astype(BF16)

    @pl.when(used)
    def _():
        slabs = MXU_COLS // LANES
        gt = up = None
        for kc in range(D_MODEL // MXU_COLS):
            xk = jnp.concatenate([x_ref[pl.ds(kc * slabs + s, MOE_ROWS, stride=ROW_SUB), :] for s in range(slabs)],
                                 axis=1).astype(BF16)
            ks = slice(kc * MXU_COLS, (kc + 1) * MXU_COLS)
            g_k = jnp.dot(xk, wg_s[ks, :], preferred_element_type=F32)
            u_k = jnp.dot(xk, wu_s[ks, :], preferred_element_type=F32)
            gt = g_k if gt is None else gt + g_k
            up = u_k if up is None else up + u_k
        hid = (gt * jax.nn.sigmoid(gt) * up).astype(BF16)
        for c in range(D_MODEL // MXU_COLS):
            y_c = jnp.dot(hid, wd_s[:, c * MXU_COLS:(c + 1) * MXU_COLS], preferred_element_type=F32)
            for s in range(slabs):
                o_ref[pl.ds(c * slabs + s, MOE_ROWS, stride=ROW_SUB), :] = y_c[:, s * LANES:(s + 1) * LANES]

    @pl.when(jnp.logical_not(used))
    def _():
        o_ref[...] = jnp.zeros_like(o_ref)


def _expert_ffn(layer, plan, xb, w_gate, w_up, w_down):
    nblk = xb.shape[0] // (MOE_ROWS * ROW_SUB)
    hbm = pl.BlockSpec(memory_space=pl.ANY)
    grid_spec = pltpu.PrefetchScalarGridSpec(
        num_scalar_prefetch=4,
        grid=(nblk,),
        in_specs=[
            pl.BlockSpec((MOE_ROWS * ROW_SUB, LANES), lambda b, be, nu, par, nxt: (jnp.minimum(b, nu[0] - 1), 0)),
            hbm, hbm, hbm,
        ],
        out_specs=pl.BlockSpec((MOE_ROWS * ROW_SUB, LANES), lambda b, be, nu, par, nxt: (b, 0)),
        scratch_shapes=[
            pltpu.VMEM((2, D_MODEL, EXPERT_HIDDEN), F32),
            pltpu.VMEM((2, D_MODEL, EXPERT_HIDDEN), F32),
            pltpu.VMEM((2, EXPERT_HIDDEN, D_MODEL), F32),
            pltpu.VMEM((D_MODEL, EXPERT_HIDDEN), BF16),
            pltpu.VMEM((D_MODEL, EXPERT_HIDDEN), BF16),
            pltpu.VMEM((EXPERT_HIDDEN, D_MODEL), BF16),
            pltpu.SemaphoreType.DMA((2, 3)),
        ],
    )
    return pl.pallas_call(
        functools.partial(_ffn_kernel, layer=layer),
        grid_spec=grid_spec,
        out_shape=jax.ShapeDtypeStruct(xb.shape, F32),
        compiler_params=_cparams("arbitrary"),
        name="expert_ffn",
    )(plan.block_expert, plan.n_used, plan.parity, plan.next_expert, xb, w_gate, w_up, w_down)


def _combine_kernel(slot_ref, slot_next_ref, x_ref, slab_ref, fw_ref, yb_hbm, o_ref,
                    buf00, buf01, buf10, buf11, sems, *, tm, n_tiles, final_norm):
    i = pl.program_id(0)
    bufs = ((buf00, buf01), (buf10, buf11))

    def copies(slots, p, u):
        return [pltpu.make_async_copy(_slab(yb_hbm, slots[TOP_K * u + j]), _slab(bufs[p][j], u), sems.at[p])
                for j in range(TOP_K)]

    def start_all(slots, p):
        def body(u, c):
            for j, cp in enumerate(copies(slots, p, u)):
                cp.start(priority=j)
            return c
        lax.fori_loop(0, tm, body, 0, unroll=8)

    def wait_all(p):
        def body(u, c):
            for cp in copies(slot_ref, p, u):
                cp.wait()
            return c
        lax.fori_loop(0, tm, body, 0, unroll=8)

    @pl.when(i == 0)
    def _():
        start_all(slot_ref, 0)

    for p in range(2):
        @pl.when(i % 2 == p)
        def _():
            @pl.when(i + 1 < n_tiles)
            def _():
                start_all(slot_next_ref, 1 - p)

            wait_all(p)
            slab = slab_ref[...]
            w1 = slab[:, 2:3]
            w2 = slab[:, 3:4]
            out = x_ref[...] + (_rows_to_2d(bufs[p][0], tm) * w1 + _rows_to_2d(bufs[p][1], tm) * w2)
            if final_norm:
                out = _rms(out, fw_ref[...])
            o_ref[...] = out


def _combine(slot_flat, x, slab, final_w, yb, final_norm, tm=256):
    n = x.shape[0]
    n_tiles = n // tm
    row = lambda i: (i, 0)
    row_buf = pltpu.VMEM((tm * ROW_SUB, LANES), F32)
    return pl.pallas_call(
        functools.partial(_combine_kernel, tm=tm, n_tiles=n_tiles, final_norm=final_norm),
        grid=(n_tiles,),
        in_specs=[
            pl.BlockSpec((TOP_K * tm,), lambda i: (i,), memory_space=pltpu.SMEM),
            pl.BlockSpec((TOP_K * tm,), lambda i: (jnp.minimum(i + 1, n_tiles - 1),), memory_space=pltpu.SMEM),
            pl.BlockSpec((tm, D_MODEL), row),
            pl.BlockSpec((tm, ROUTER_LANES), row),
            pl.BlockSpec((1, D_MODEL), lambda i: (0, 0)),
            pl.BlockSpec(memory_space=pl.ANY),
        ],
        out_specs=pl.BlockSpec((tm, D_MODEL), row),
        out_shape=jax.ShapeDtypeStruct((n, D_MODEL), F32),
        scratch_shapes=[row_buf, row_buf, row_buf, row_buf, pltpu.SemaphoreType.DMA((2,))],
        compiler_params=_cparams("arbitrary"),
        name="moe_combine",
    )(slot_flat, slot_flat, x, slab, final_w, yb)


def _pad_rows(w, rows, offset=0):
    out = jnp.zeros((rows, w.shape[1]), BF16)
    return out.at[offset:offset + w.shape[0]].set(w.astype(BF16))


def _pad_cols(vec, cols, offset=0):
    out = jnp.zeros((cols,), F32)
    return out.at[offset:offset + vec.shape[0]].set(vec.astype(F32))


class MoePlan(NamedTuple):
    slot: jax.Array
    block_expert: jax.Array
    n_used: jax.Array
    parity: jax.Array
    next_expert: jax.Array
    cap: int


def _moe_plan(slab, counts_row, n_tokens):
    eid = slab[:, 0:TOP_K].astype(jnp.int32)
    rank = slab[:, 4:4 + TOP_K].astype(jnp.int32)
    counts = counts_row[EXPERT_LANE0:EXPERT_LANE0 + N_EXPERTS].astype(jnp.int32)
    padded = (counts + MOE_ROWS - 1) // MOE_ROWS * MOE_ROWS
    pad_ends = jnp.cumsum(padded)
    pad_starts = pad_ends - padded
    slot = pad_starts[eid.reshape(-1)] + rank.reshape(-1)
    nblk = n_tokens * TOP_K // MOE_ROWS + N_EXPERTS
    block_start = jnp.arange(nblk, dtype=jnp.int32) * MOE_ROWS
    block_expert = jnp.minimum(
        jnp.sum((pad_ends[None, :] <= block_start[:, None]).astype(jnp.int32), axis=1),
        N_EXPERTS - 1).astype(jnp.int32)
    n_used = (pad_ends[-1:] // MOE_ROWS).astype(jnp.int32)
    blk = jnp.arange(nblk, dtype=jnp.int32)
    prev_expert = jnp.concatenate([block_expert[:1], block_expert[:-1]])
    first = (blk < n_used[0]) & ((blk == 0) | (block_expert != prev_expert))
    parity = ((jnp.cumsum(first.astype(jnp.int32)) - 1) % 2).astype(jnp.int32)
    later_first = first[None, :] & (blk[None, :] > blk[:, None])
    next_first = jnp.min(jnp.where(later_first, blk[None, :], nblk), axis=1)
    next_expert = jnp.where(next_first < nblk, block_expert[jnp.minimum(next_first, nblk - 1)], -1)
    return MoePlan(slot, block_expert, n_used, parity, next_expert.astype(jnp.int32), nblk * MOE_ROWS)


def kernel(x, attn_norm_w, w_in, shift_mix, attn_sinks, rwkv_w0, rwkv_w_up, rwkv_a0, rwkv_a_up, rwkv_g_up, rwkv_k_k, rwkv_k_a, rwkv_r_k, rwkv_ln_w, rwkv_ln_b, vres_down, vres_mix, vres_up, vres_v0, w_branch_a, w_branch_b, w_out, ffn_norm_w, router_group_w, router_group_b, router_expert_w, router_expert_b, expert_w_gate, expert_w_up, expert_w_down, final_norm_w):
    batch, seq, _ = x.shape
    n = batch * seq
    depth = w_in.shape[0]
    xf = x.reshape(n, D_MODEL)
    w_in_bf = jnp.swapaxes(w_in, 1, 2).astype(BF16)
    v_first = None
    slot_backing = None
    slot_rows = (n * TOP_K // MOE_ROWS + N_EXPERTS) * MOE_ROWS
    for i in range(depth):
        has_vres = i > 0
        w_gates = w_in_bf[i, GATE_COL0:, :]
        mix_r = shift_mix[i, :RKV_COLS].reshape(1, RKV_COLS)
        mix_l = _pad_cols(shift_mix[i, RKV_COLS:], LORA_COLS).reshape(1, LORA_COLS)
        v0 = vres_v0[i - 1] if has_vres else jnp.zeros((RWKV_WIDTH,), F32)
        zero = jnp.zeros((RWKV_WIDTH,), F32)
        pv_pre = jnp.stack([rwkv_w0[i], rwkv_a0[i], rwkv_k_k[i], rwkv_k_a[i], v0, zero, zero, zero]).astype(F32)
        pv_post = jnp.stack([rwkv_ln_w[i], rwkv_ln_b[i], rwkv_r_k[i].reshape(-1),
                             zero, zero, zero, zero, zero]).astype(F32)
        wup = _pad_rows(rwkv_w_up[i], LANES, 0)
        aup = _pad_rows(rwkv_a_up[i], LANES, DECAY_LORA)
        gup = _pad_rows(rwkv_g_up[i], 2 * LANES, 0)
        if has_vres:
            vd = _pad_rows(vres_down[i - 1].T, LANES, 0).T
            mix_v = _pad_cols(vres_mix[i - 1], LANES).reshape(1, LANES)
            vup = _pad_rows(vres_up[i - 1], LANES, 0)
        else:
            vd, mix_v = None, None
            vup = jnp.zeros((LANES, RWKV_WIDTH), BF16)
        w_router = jnp.concatenate(
            [router_group_w[i], router_expert_w[i],
             jnp.zeros((D_MODEL, ROUTER_LANES - N_GROUPS - N_EXPERTS), F32)], axis=1)
        b_router = _pad_cols(jnp.concatenate([router_group_b[i], router_expert_b[i]]),
                             ROUTER_LANES).reshape(1, ROUTER_LANES)

        need_zeros = slot_backing is None
        proj = _inproj(i, xf, attn_norm_w[i].reshape(1, D_MODEL), w_in_bf, w_gates, vd,
                       zero_rows=slot_rows * ROW_SUB if need_zeros else 0)
        qkv, rkv, lora, gates = proj[:4]
        zv = proj[4] if has_vres else None
        if need_zeros:
            slot_backing = proj[-1]
        attn = _attention(qkv, attn_sinks[i], seq)
        y, v_new = _rwkv_mixer(rkv, lora, v_first, zv, mix_r, mix_l, mix_v, pv_pre,
                               wup, aup, gup, vup, pv_post, seq)
        if not has_vres:
            v_first = v_new
        merged = _merge(i, attn, y, w_branch_a, w_branch_b, gates)

        x_mid, hf, slab, counts = _outproj_router(
            merged, w_out[i].astype(BF16), xf, ffn_norm_w[i].reshape(1, D_MODEL), w_router, b_router)
        plan = _moe_plan(slab, counts[0], n)
        xb = _dispatch(plan.slot, hf, plan.cap, slot_backing)
        yb = _expert_ffn(i, plan, xb, expert_w_gate, expert_w_up, expert_w_down)
        slot_backing = yb
        xf = _combine(plan.slot, x_mid, slab, final_norm_w.reshape(1, D_MODEL), yb, final_norm=(i == depth - 1))
    return xf.reshape(batch, seq, D_MODEL)
```

```python
import functools
import math
from typing import NamedTuple

import jax
import jax.numpy as jnp
from jax import lax
from jax.experimental import pallas as pl
from jax.experimental.pallas import tpu as pltpu

F32 = jnp.float32
BF16 = jnp.bfloat16

D_MODEL = 2048
HEAD_DIM = 64
N_Q_HEADS = 16
N_KV_HEADS = 4
GQA_GROUP = N_Q_HEADS // N_KV_HEADS
ATTN_WIDTH = N_Q_HEADS * HEAD_DIM
KV_WIDTH = N_KV_HEADS * HEAD_DIM
WINDOW = 128
ATTN_BLOCK = 128
ATTN_SUB = 2

RWKV_HEAD = 64
N_RWKV_HEADS = 16
RWKV_WIDTH = N_RWKV_HEADS * RWKV_HEAD
DECAY_LORA = 64
AAA_LORA = 64
MV_LORA = 32
GATE_LORA = 160
RWKV_GN_EPS = 64e-5
DECAY_SCALE = math.exp(-0.5)
KK_NORM_FLOOR = 1e-12
LORA_WIDTH = DECAY_LORA + AAA_LORA + GATE_LORA

N_GROUPS = 4
EXPERTS_PER_GROUP = 8
N_EXPERTS = N_GROUPS * EXPERTS_PER_GROUP
TOP_K = 2
EXPERT_HIDDEN = D_MODEL // 4
NORM_EPS = 1e-5

LANES = 128
SUBLANES = 8
MXU_COLS = 256

SEG = 512
QKV_COLS = ATTN_WIDTH + 2 * KV_WIDTH
RKV_COLS = 3 * RWKV_WIDTH
LORA_COLS = SEG
GATE_COLS = 2 * D_MODEL
QKV_TILES = QKV_COLS // SEG
RKV_TILES = RKV_COLS // SEG
LORA_TILES = LORA_COLS // SEG
GATE_TILES = GATE_COLS // SEG
IN_TILES = QKV_TILES + RKV_TILES + LORA_TILES + GATE_TILES
FIRST_GATE_TILE = QKV_TILES + RKV_TILES + LORA_TILES
ZERO_STEPS = 16
GATE_COL0 = QKV_COLS + RKV_COLS + LORA_WIDTH

SCAN_CHUNK = 64
SCAN_SUB = 2
SCAN_GROUP = 4
MOE_ROWS = 256
ROUTER_LANES = LANES
ROUTER_ROWS = 128
EXPERT_LANE0 = N_GROUPS

ROW_SUB = D_MODEL // LANES

VMEM_LIMIT = 56 * 1024 * 1024


def _rows_to_2d(slab_ref, rows):
    return jnp.concatenate([slab_ref[pl.ds(s, rows, stride=ROW_SUB), :] for s in range(ROW_SUB)], axis=1)


def _store_rows(slab_ref, val):
    rows = val.shape[0]
    for s in range(ROW_SUB):
        slab_ref[pl.ds(s, rows, stride=ROW_SUB), :] = val[:, s * LANES:(s + 1) * LANES]


def _slab(ref, row):
    return ref.at[pl.ds(pl.multiple_of(row * ROW_SUB, ROW_SUB), ROW_SUB)]


def _cparams(*sem):
    return pltpu.CompilerParams(dimension_semantics=sem, vmem_limit_bytes=VMEM_LIMIT)


def _bdot(a, b):
    return jnp.dot(a.astype(BF16), b.astype(BF16), preferred_element_type=F32)


def _bdot_nt(a, b):
    return lax.dot_general(a.astype(BF16), b.astype(BF16), (((1,), (1,)), ((), ())),
                           preferred_element_type=F32)


def _bdot_tn(a, b):
    return lax.dot_general(a.astype(BF16), b.astype(BF16), (((0,), (0,)), ((), ())),
                           preferred_element_type=F32)


def _dot_split_rhs(w01, x, passes):
    acc, rem = None, x
    for p in range(passes):
        part = rem.astype(BF16)
        d = jnp.dot(w01, part, preferred_element_type=F32)
        acc = d if acc is None else acc + d
        if p + 1 < passes:
            rem = rem - part.astype(F32)
    return acc


def _dot_split_lhs(x, w01, passes):
    acc, rem = None, x
    for p in range(passes):
        part = rem.astype(BF16)
        d = jnp.dot(part, w01, preferred_element_type=F32)
        acc = d if acc is None else acc + d
        if p + 1 < passes:
            rem = rem - part.astype(F32)
    return acc


def _rms(x, w):
    ms = jnp.mean(x * x, axis=-1, keepdims=True)
    return x * lax.rsqrt(ms + NORM_EPS) * w


def _inproj_kernel(*refs, has_vres, has_zeros):
    refs = list(refs)
    x_ref, nw_ref, w_ref, wg_ref = refs[:4]
    refs = refs[4:]
    vd_ref = refs.pop(0) if has_vres else None
    qkv_ref, rkv_ref, lora_ref, gate_ref = refs[:4]
    refs = refs[4:]
    zv_ref = refs.pop(0) if has_vres else None
    zeros_ref = refs.pop(0) if has_zeros else None
    (h_scr,) = refs
    j = pl.program_id(1)

    if has_zeros:
        zeros_ref[...] = jnp.zeros_like(zeros_ref)

    @pl.when(j == 0)
    def _():
        h_scr[...] = _rms(x_ref[...], nw_ref[...]).astype(BF16)
        if has_vres:
            zv_ref[...] = jnp.dot(h_scr[...], vd_ref[...], preferred_element_type=F32)

    def project(wt_ref, out_ref):
        for c in range(SEG // MXU_COLS):
            cs = slice(c * MXU_COLS, (c + 1) * MXU_COLS)
            acc = lax.dot_general(h_scr[...], wt_ref[cs, :], (((1,), (1,)), ((), ())),
                                  preferred_element_type=F32)
            out_ref[:, cs] = acc.astype(out_ref.dtype)

    @pl.when(j < QKV_TILES)
    def _():
        project(w_ref, qkv_ref)

    @pl.when((j >= QKV_TILES) & (j < QKV_TILES + RKV_TILES))
    def _():
        project(w_ref, rkv_ref)

    @pl.when(j == QKV_TILES + RKV_TILES)
    def _():
        project(w_ref, lora_ref)

    @pl.when(j >= FIRST_GATE_TILE)
    def _():
        project(wg_ref, gate_ref)


def _inproj(layer, x, norm_w, w_in, w_gates, vd, zero_rows=0, tm=1024):
    n = x.shape[0]
    has_vres = vd is not None
    s_rkv = QKV_TILES
    in_specs = [
        pl.BlockSpec((tm, D_MODEL), lambda i, j: (i, 0)),
        pl.BlockSpec((1, D_MODEL), lambda i, j: (0, 0)),
        pl.BlockSpec((None, SEG, D_MODEL), lambda i, j: (layer, jnp.minimum(j, FIRST_GATE_TILE - 1), 0)),
        pl.BlockSpec((SEG, D_MODEL), lambda i, j: (jnp.clip(j - FIRST_GATE_TILE, 0, GATE_TILES - 1), 0)),
    ]
    args = [x, norm_w, w_in, w_gates]
    out_specs = [
        pl.BlockSpec((tm, SEG), lambda i, j: (i, jnp.clip(j, 0, QKV_TILES - 1))),
        pl.BlockSpec((tm, SEG), lambda i, j: (i, jnp.clip(j - s_rkv, 0, RKV_TILES - 1))),
        pl.BlockSpec((tm, SEG), lambda i, j: (i, 0)),
        pl.BlockSpec((tm, SEG), lambda i, j: (i, jnp.clip(j - FIRST_GATE_TILE, 0, GATE_TILES - 1))),
    ]
    out_shape = [
        jax.ShapeDtypeStruct((n, QKV_COLS), BF16),
        jax.ShapeDtypeStruct((n, RKV_COLS), F32),
        jax.ShapeDtypeStruct((n, LORA_COLS), F32),
        jax.ShapeDtypeStruct((n, GATE_COLS), BF16),
    ]
    if has_vres:
        in_specs.append(pl.BlockSpec((D_MODEL, LANES), lambda i, j: (0, 0)))
        args.append(vd)
        out_specs.append(pl.BlockSpec((tm, LANES), lambda i, j: (i, 0)))
        out_shape.append(jax.ShapeDtypeStruct((n, LANES), F32))
    if zero_rows:
        blk = zero_rows // ((n // tm) * ZERO_STEPS)
        assert blk * (n // tm) * ZERO_STEPS == zero_rows and blk % SUBLANES == 0
        out_specs.append(pl.BlockSpec((blk, LANES), lambda i, j: (i * ZERO_STEPS + jnp.minimum(j, ZERO_STEPS - 1), 0)))
        out_shape.append(jax.ShapeDtypeStruct((zero_rows, LANES), F32))
    return pl.pallas_call(
        functools.partial(_inproj_kernel, has_vres=has_vres, has_zeros=bool(zero_rows)),
        grid=(n // tm, IN_TILES),
        in_specs=in_specs,
        out_specs=out_specs,
        out_shape=out_shape,
        scratch_shapes=[pltpu.VMEM((tm, D_MODEL), BF16)],
        compiler_params=_cparams("parallel", "arbitrary"),
        name="inproj",
    )(*args)


def _attn_kernel(sink_ref, q_ref, kp_ref, kc_ref, vp_ref, vc_ref, bias_ref, o_ref, *, tiles_per_seq):
    first = (pl.program_id(0) % tiles_per_seq) == 0
    col = lax.broadcasted_iota(jnp.int32, (ATTN_BLOCK, 2 * ATTN_BLOCK), 1)
    pad_keys = jnp.logical_and(first, col < ATTN_BLOCK)
    scale = HEAD_DIM ** -0.5
    items = [(sb, hk) for sb in range(ATTN_SUB) for hk in range(N_KV_HEADS)]

    def window(prev_ref, cur_ref, sb, hk):
        ks = slice(hk * HEAD_DIM, (hk + 1) * HEAD_DIM)
        rows = slice(sb * ATTN_BLOCK, (sb + 1) * ATTN_BLOCK)
        prev = prev_ref[:, ks] if sb == 0 else cur_ref[(sb - 1) * ATTN_BLOCK:sb * ATTN_BLOCK, ks]
        return jnp.concatenate([prev, cur_ref[rows, ks]], axis=0)

    scores = {}
    for sb, hk in items:
        kw = window(kp_ref, kc_ref, sb, hk)
        for g in range(GQA_GROUP):
            h = hk * GQA_GROUP + g
            qh = q_ref[sb * ATTN_BLOCK:(sb + 1) * ATTN_BLOCK, h * HEAD_DIM:(h + 1) * HEAD_DIM]
            s = lax.dot_general(qh, kw, (((1,), (1,)), ((), ())), preferred_element_type=F32)
            s = s * scale + bias_ref[h]
            if sb == 0:
                s = jnp.where(pad_keys, -jnp.inf, s)
            scores[(sb, h)] = s
    row_max = {key: jnp.max(s, axis=-1, keepdims=True) for key, s in scores.items()}
    m = {key: jnp.maximum(row_max[key], sink_ref[key[1]]) for key in scores}
    p = {key: jnp.exp(scores[key] - m[key]) for key in scores}
    row_sum = {key: jnp.sum(p[key], axis=-1, keepdims=True) for key in scores}
    denoms = {key: row_sum[key] + jnp.exp(sink_ref[key[1]] - m[key]) for key in scores}
    probs = {key: p[key].astype(BF16) for key in scores}
    for sb, hk in items:
        vw = window(vp_ref, vc_ref, sb, hk)
        for g in range(GQA_GROUP):
            h = hk * GQA_GROUP + g
            o = jnp.dot(probs[(sb, h)], vw, preferred_element_type=F32) / denoms[(sb, h)]
            o_ref[sb * ATTN_BLOCK:(sb + 1) * ATTN_BLOCK, h * HEAD_DIM:(h + 1) * HEAD_DIM] = o.astype(BF16)


def _attn_bias():
    qi = jnp.arange(ATTN_BLOCK)[:, None]
    kj = jnp.arange(2 * ATTN_BLOCK)[None, :]
    dist = qi + ATTN_BLOCK - kj
    valid = (dist >= 0) & (dist < WINDOW)
    slopes = jnp.exp2(-8.0 * jnp.arange(1, N_Q_HEADS + 1, dtype=F32) / N_Q_HEADS)
    bias = -slopes[:, None, None] * dist.astype(F32)[None]
    return jnp.where(valid[None], bias, -jnp.inf)


def _attention(qkv, sinks, seq):
    n = qkv.shape[0]
    rows = ATTN_SUB * ATTN_BLOCK
    tiles_per_seq = seq // rows
    kcol = ATTN_WIDTH // KV_WIDTH
    prev = lambda i: jnp.maximum(i * ATTN_SUB - 1, 0)
    return pl.pallas_call(
        functools.partial(_attn_kernel, tiles_per_seq=tiles_per_seq),
        grid=(n // rows,),
        in_specs=[
            pl.BlockSpec(memory_space=pltpu.SMEM),
            pl.BlockSpec((rows, ATTN_WIDTH), lambda i: (i, 0)),
            pl.BlockSpec((ATTN_BLOCK, KV_WIDTH), lambda i: (prev(i), kcol)),
            pl.BlockSpec((rows, KV_WIDTH), lambda i: (i, kcol)),
            pl.BlockSpec((ATTN_BLOCK, KV_WIDTH), lambda i: (prev(i), kcol + 1)),
            pl.BlockSpec((rows, KV_WIDTH), lambda i: (i, kcol + 1)),
            pl.BlockSpec((N_Q_HEADS, ATTN_BLOCK, 2 * ATTN_BLOCK), lambda i: (0, 0, 0)),
        ],
        out_specs=pl.BlockSpec((rows, ATTN_WIDTH), lambda i: (i, 0)),
        out_shape=jax.ShapeDtypeStruct((n, ATTN_WIDTH), BF16),
        compiler_params=_cparams("parallel"),
        name="swa_attention",
    )(sinks, qkv, qkv, qkv, qkv, qkv, _attn_bias())


def _head_ones():
    ch = jnp.arange(MXU_COLS) // RWKV_HEAD
    return (ch[:, None] == ch[None, :]).astype(BF16)


def _head_sums(x, ones, passes):
    parts = [_dot_split_lhs(x[:, c:c + MXU_COLS], ones, passes) for c in range(0, RWKV_WIDTH, MXU_COLS)]
    return jnp.concatenate(parts, axis=1)


def _rwkv_kernel(*refs, has_vres):
    refs = list(refs)
    rkv_ref, lo_ref = refs[:2]
    refs = refs[2:]
    if has_vres:
        vf_ref, zv_ref, mixv_ref = refs[:3]
        refs = refs[3:]
    mixr_ref, mixl_ref, pv_ref, wup_ref, aup_ref, gup_ref, vup_ref, ones_ref, pp_ref = refs[:9]
    refs = refs[9:]
    y_ref = refs.pop(0)
    v_out = None if has_vres else refs.pop(0)
    s_scr, prev_rkv, prev_lo = refs[:3]
    prev_zv = refs[3] if has_vres else None

    L = SCAN_CHUNK
    W = SCAN_GROUP * RWKV_HEAD
    n_seq, step_rows = rkv_ref.shape[0], rkv_ref.shape[1]
    n_grp = N_RWKV_HEADS // SCAN_GROUP
    ones = ones_ref[...]

    @pl.when(pl.program_id(0) == 0)
    def _():
        s_scr[...] = jnp.zeros_like(s_scr)
        prev_rkv[...] = jnp.zeros_like(prev_rkv)
        prev_lo[...] = jnp.zeros_like(prev_lo)
        if has_vres:
            prev_zv[...] = jnp.zeros_like(prev_zv)

    def lerp_shift(z_ref, carry_ref, b, mix):
        z = z_ref[b]
        row = lax.broadcasted_iota(jnp.int32, z.shape, 0)
        back = jnp.where(row == 0, carry_ref[b, SUBLANES - 1:SUBLANES, :], pltpu.roll(z, 1, 0))
        carry_ref[b] = z_ref[b, step_rows - SUBLANES:step_rows, :]
        return z + (back - z) * mix

    w0, a0, k_k, k_a, v0 = (pv_ref[i:i + 1, :] for i in range(5))
    pre = []
    for b in range(n_seq):
        zs = lerp_shift(rkv_ref, prev_rkv, b, mixr_ref[...])
        los = lerp_shift(lo_ref, prev_lo, b, mixl_ref[...])
        r = zs[:, :RWKV_WIDTH]
        kr = zs[:, RWKV_WIDTH:2 * RWKV_WIDTH]
        vr = zs[:, 2 * RWKV_WIDTH:]
        wa = los[:, :LANES]
        col = lax.broadcasted_iota(jnp.int32, wa.shape, 1)
        wa_in = jnp.where(col < DECAY_LORA, jnp.tanh(wa), wa).astype(BF16)
        gd = los[:, LANES:3 * LANES]
        col_g = lax.broadcasted_iota(jnp.int32, gd.shape, 1)
        g_in = jnp.where(col_g < GATE_LORA, jax.nn.sigmoid(gd), gd).astype(BF16)
        dw = jnp.dot(wa_in, wup_ref[...], preferred_element_type=F32)
        da = jnp.dot(wa_in, aup_ref[...], preferred_element_type=F32)
        g = jnp.dot(g_in, gup_ref[...], preferred_element_type=F32)
        lw = -DECAY_SCALE * jax.nn.sigmoid(w0 + dw)
        a = jax.nn.sigmoid(a0 + da)
        if has_vres:
            zvs = lerp_shift(zv_ref, prev_zv, b, mixv_ref[...])
            dv = jnp.dot(zvs.astype(BF16), vup_ref[...], preferred_element_type=F32)
            vr = vr + (vf_ref[b] - vr) * jax.nn.sigmoid(v0 + dv)
        else:
            v_out[b] = vr
        kk0 = kr * k_k
        ss = _head_sums(kk0 * kk0, ones, 2)
        kk = kk0 * lax.rsqrt(jnp.maximum(ss, KK_NORM_FLOOR ** 2))
        pre.append(dict(r=r, lw=lw, k=kr * (1.0 + (a - 1.0) * k_a), v=vr, al=-kk, be=kk * a, g=g))

    row = lax.broadcasted_iota(jnp.int32, (L, L), 0)
    col = lax.broadcasted_iota(jnp.int32, (L, L), 1)
    tri = (row >= col).astype(BF16)
    t_w = lax.broadcasted_iota(jnp.int32, (L, W), 0)
    lane_w = lax.broadcasted_iota(jnp.int32, (L, W), 1)
    s_w = lane_w % RWKV_HEAD
    strict = t_w > s_w
    incl = t_w >= s_w
    eye = (t_w == s_w).astype(F32)
    blk_w = lane_w // RWKV_HEAD
    bd_mask = (lax.broadcasted_iota(jnp.int32, (W, W), 0) // RWKV_HEAD
               == lax.broadcasted_iota(jnp.int32, (W, W), 1) // RWKV_HEAD)

    def blockdiag(w):
        tiled = jnp.concatenate([w.astype(BF16)] * SCAN_GROUP, axis=0)
        return jnp.where(bd_mask, tiled, jnp.zeros_like(tiled))

    ln_w, ln_b, r_k = (pp_ref[i:i + 1, :] for i in range(3))
    items = [(b, g) for b in range(n_seq) for g in range(n_grp)]
    for sub in range(step_rows // L):
        rows = slice(sub * L, (sub + 1) * L)
        ops = {}
        for b in range(n_seq):
            p = pre[b]
            lw = p['lw'][rows]
            cum = _dot_split_rhs(tri, lw, 3)
            p_inc = jnp.exp(cum)
            p_exc = jnp.exp(cum - lw)
            p_inv = jnp.exp(-cum)
            p_last = p_inc[L - 1:L, :]
            a_t = p['al'][rows] * p_exc
            r_t = p['r'][rows] * p_inc
            b_t = p['be'][rows] * p_inv
            k_t = p['k'][rows] * p_inv
            b_end = b_t * p_last
            k_end = k_t * p_last
            v = p['v'][rows]
            for g in range(n_grp):
                gs = slice(g * W, (g + 1) * W)
                ops[(b, g)] = dict(
                    ar=jnp.concatenate([a_t[:, gs], r_t[:, gs]], axis=0),
                    bt=b_t[:, gs], kt=k_t[:, gs], v=v[:, gs], p_last=p_last[:, gs],
                    bke=jnp.concatenate([b_end[:, gs], k_end[:, gs]], axis=0))

        gm, xs, av, s0 = {}, {}, {}, {}
        for it in items:
            o = ops[it]
            rhs = jnp.concatenate([blockdiag(o['bt']), blockdiag(o['kt'])], axis=0)
            gm[it] = _bdot_nt(o['ar'], rhs)
            s0[it] = s_scr[it[0], it[1]]
            xs[it] = _bdot_nt(o['ar'], blockdiag(s0[it]))
        a_ab, a_rb = {}, {}
        for it in items:
            g = gm[it]
            a_ab[it] = jnp.where(strict, g[:L, :W], 0.0)
            a_rb[it] = jnp.where(incl, g[L:, :W], 0.0)
            a_kk = jnp.concatenate([jnp.where(strict, g[:L, W:], 0.0), jnp.where(incl, g[L:, W:], 0.0)], axis=0)
            av[it] = _bdot(a_kk, blockdiag(ops[it]['v']))

        pw = {it: _bdot(a_ab[it], blockdiag(a_ab[it])) for it in items}
        tinv = {it: eye + a_ab[it] for it in items}
        span = 2
        while 2 * span < L:
            both = {it: _bdot(jnp.concatenate([pw[it], tinv[it]], axis=0), blockdiag(pw[it])) for it in items}
            tinv = {it: tinv[it] + both[it][L:] for it in items}
            pw = {it: both[it][:L] for it in items}
            span *= 2
        tinv = {it: tinv[it] + _bdot(tinv[it], blockdiag(pw[it])) for it in items}

        u = {it: _bdot(tinv[it], blockdiag(xs[it][:L] + av[it][:L])) for it in items}
        y = {it: xs[it][L:] + av[it][L:] + _bdot(a_rb[it], blockdiag(u[it])) for it in items}
        for it in items:
            o = ops[it]
            full = _bdot_tn(jnp.concatenate([u[it], o['v']], axis=0), o['bke'])
            upd = full[:RWKV_HEAD]
            for h in range(1, SCAN_GROUP):
                upd = jnp.where(blk_w == h, full[h * RWKV_HEAD:(h + 1) * RWKV_HEAD], upd)
            s_scr[it[0], it[1]] = s0[it] * o['p_last'] + upd

        inv_n = 1.0 / RWKV_HEAD
        for b in range(n_seq):
            p = pre[b]
            yb = jnp.concatenate([y[(b, g)] for g in range(n_grp)], axis=1)
            d = yb - _head_sums(yb, ones, 2) * inv_n
            var = _head_sums(d * d, ones, 2) * inv_n
            yn = d * lax.rsqrt(var + RWKV_GN_EPS) * ln_w + ln_b
            bonus = _head_sums(p['r'][rows] * p['k'][rows] * r_k, ones, 2) * p['v'][rows]
            y_ref[b, rows, :] = ((yn + bonus) * p['g'][rows]).astype(BF16)


def _rwkv_mixer(rkv, lora, v_first, zv, mix_r, mix_l, mix_v, pv_pre, wup, aup, gup, vup, pv_post, seq):
    n = rkv.shape[0]
    n_seq = n // seq
    has_vres = v_first is not None
    step_rows = SCAN_SUB * SCAN_CHUNK
    timed = lambda cols: pl.BlockSpec((n_seq, step_rows, cols), lambda t: (0, t, 0))
    const = lambda shape: pl.BlockSpec(shape, lambda t: (0, 0))
    by_seq = lambda a: a.reshape(n_seq, seq, a.shape[-1])
    in_specs = [timed(RKV_COLS), timed(LORA_COLS)]
    args = [by_seq(rkv), by_seq(lora)]
    if has_vres:
        in_specs += [timed(RWKV_WIDTH), timed(LANES), const((1, LANES))]
        args += [by_seq(v_first), by_seq(zv), mix_v]
    in_specs += [
        const((1, RKV_COLS)), const((1, LORA_COLS)), const((SUBLANES, RWKV_WIDTH)),
        const((LANES, RWKV_WIDTH)), const((LANES, RWKV_WIDTH)), const((2 * LANES, RWKV_WIDTH)),
        const((LANES, RWKV_WIDTH)), const((MXU_COLS, MXU_COLS)), const((SUBLANES, RWKV_WIDTH)),
    ]
    args += [mix_r, mix_l, pv_pre, wup, aup, gup, vup, _head_ones(), pv_post]
    out_specs = [timed(RWKV_WIDTH)]
    out_shape = [jax.ShapeDtypeStruct((n_seq, seq, RWKV_WIDTH), BF16)]
    if not has_vres:
        out_specs.append(timed(RWKV_WIDTH))
        out_shape.append(jax.ShapeDtypeStruct((n_seq, seq, RWKV_WIDTH), F32))
    scratch = [
        pltpu.VMEM((n_seq, N_RWKV_HEADS // SCAN_GROUP, RWKV_HEAD, SCAN_GROUP * RWKV_HEAD), F32),
        pltpu.VMEM((n_seq, SUBLANES, RKV_COLS), F32),
        pltpu.VMEM((n_seq, SUBLANES, LORA_COLS), F32),
    ]
    if has_vres:
        scratch.append(pltpu.VMEM((n_seq, SUBLANES, LANES), F32))
    outs = pl.pallas_call(
        functools.partial(_rwkv_kernel, has_vres=has_vres),
        grid=(seq // step_rows,),
        in_specs=in_specs,
        out_specs=out_specs,
        out_shape=out_shape,
        scratch_shapes=scratch,
        compiler_params=_cparams("arbitrary"),
        name="rwkv_mixer",
    )(*args)
    y = outs[0].reshape(n, RWKV_WIDTH)
    return y, (None if has_vres else outs[1].reshape(n, RWKV_WIDTH))


def _merge_kernel(a_ref, y_ref, wa_ref, wb_ref, ga_ref, gb_ref, o_ref):
    for c in range(o_ref.shape[1] // MXU_COLS):
        cs = slice(c * MXU_COLS, (c + 1) * MXU_COLS)
        pa = jnp.dot(a_ref[...], wa_ref[:, cs].astype(BF16), preferred_element_type=F32)
        pb = jnp.dot(y_ref[...], wb_ref[:, cs].astype(BF16), preferred_element_type=F32)
        ga = jax.nn.sigmoid(ga_ref[:, cs].astype(F32))
        gb = jax.nn.sigmoid(gb_ref[:, cs].astype(F32))
        o_ref[:, cs] = (ga * pa + gb * pb).astype(BF16)


def _merge(layer, attn, y, wa, wb, gates, tm=1024, tn=1024):
    n = attn.shape[0]
    nj = D_MODEL // tn
    return pl.pallas_call(
        _merge_kernel,
        grid=(n // tm, nj),
        in_specs=[
            pl.BlockSpec((tm, ATTN_WIDTH), lambda i, j: (i, 0)),
            pl.BlockSpec((tm, RWKV_WIDTH), lambda i, j: (i, 0)),
            pl.BlockSpec((None, ATTN_WIDTH, tn), lambda i, j: (layer, 0, j)),
            pl.BlockSpec((None, RWKV_WIDTH, tn), lambda i, j: (layer, 0, j)),
            pl.BlockSpec((tm, tn), lambda i, j: (i, j)),
            pl.BlockSpec((tm, tn), lambda i, j: (i, j + nj)),
        ],
        out_specs=pl.BlockSpec((tm, tn), lambda i, j: (i, j)),
        out_shape=jax.ShapeDtypeStruct((n, D_MODEL), BF16),
        compiler_params=_cparams("parallel", "arbitrary"),
        name="gated_merge",
    )(attn, y, wa, wb, gates, gates)


def _outproj_router_kernel(m_ref, wo_ref, x_ref, nw_ref, wr_ref, br_ref,
                           xo_ref, hf_ref, slab_ref, cnt_ref, run_scr):
    @pl.when(pl.program_id(0) == 0)
    def _():
        run_scr[...] = jnp.zeros_like(run_scr)

    tm = x_ref.shape[0]
    sq = jnp.zeros((tm, LANES), F32)
    for c in range(D_MODEL // MXU_COLS):
        cs = slice(c * MXU_COLS, (c + 1) * MXU_COLS)
        xn_c = x_ref[:, cs] + jnp.dot(m_ref[...], wo_ref[:, cs], preferred_element_type=F32)
        xo_ref[:, cs] = xn_c
        for l in range(MXU_COLS // LANES):
            piece = xn_c[:, l * LANES:(l + 1) * LANES]
            sq = sq + piece * piece
    ms = jnp.sum(sq, axis=-1, keepdims=True) * (1.0 / D_MODEL)
    hf = xo_ref[...] * lax.rsqrt(ms + NORM_EPS) * nw_ref[...]
    _store_rows(hf_ref, hf)

    h_hi = hf.astype(BF16)
    h_lo = (hf - h_hi.astype(F32)).astype(BF16)
    wr = wr_ref[...]
    w_hi = wr.astype(BF16)
    w_lo = (wr - w_hi.astype(F32)).astype(BF16)
    lg = (jnp.dot(h_hi, w_hi, preferred_element_type=F32)
          + jnp.dot(h_hi, w_lo, preferred_element_type=F32)
          + jnp.dot(h_lo, w_hi, preferred_element_type=F32)) + br_ref[...]

    chunks = range(tm // ROUTER_ROWS)
    lgs = [lg[c * ROUTER_ROWS:(c + 1) * ROUTER_ROWS] for c in chunks]
    lane = lax.broadcasted_iota(jnp.int32, (ROUTER_ROWS, ROUTER_LANES), 1).astype(F32)
    neg = -jnp.inf
    big = float(ROUTER_LANES)
    is_group = lane < N_GROUPS

    def row_max(vals):
        return [jnp.max(v, axis=-1, keepdims=True) for v in vals]

    def first_index(vals, mx):
        return [jnp.min(jnp.where(v == m, lane, big), axis=-1, keepdims=True) for v, m in zip(vals, mx)]

    def row_sum(vals):
        return [jnp.sum(v, axis=-1, keepdims=True) for v in vals]

    gl = [jnp.where(is_group, v, neg) for v in lgs]
    gmax = row_max(gl)
    gsel = first_index(gl, gmax)
    gsum = row_sum([jnp.where(is_group, jnp.exp(v - m), 0.0) for v, m in zip(lgs, gmax)])
    lo_lane = [EXPERT_LANE0 + EXPERTS_PER_GROUP * g for g in gsel]
    el = [jnp.where((lane >= lo) & (lane < lo + EXPERTS_PER_GROUP), v, neg) for v, lo in zip(lgs, lo_lane)]
    v1 = row_max(el)
    i1 = first_index(el, v1)
    el2 = [jnp.where(lane == i, neg, v) for v, i in zip(el, i1)]
    v2 = row_max(el2)
    i2 = first_index(el2, v2)
    oh1 = [lane == i for i in i1]
    oh2 = [lane == i for i in i2]

    cnt = jnp.concatenate([a.astype(F32) + b.astype(F32) for a, b in zip(oh1, oh2)], axis=0)
    r_i = lax.broadcasted_iota(jnp.int32, (tm, tm), 0)
    c_i = lax.broadcasted_iota(jnp.int32, (tm, tm), 1)
    before = jnp.dot((r_i > c_i).astype(BF16), cnt.astype(BF16), preferred_element_type=F32)
    tot = before + run_scr[0:1, :]
    tots = [tot[c * ROUTER_ROWS:(c + 1) * ROUTER_ROWS] for c in chunks]
    rank1 = row_sum([jnp.where(o, t, 0.0) for o, t in zip(oh1, tots)])
    rank2 = row_sum([jnp.where(o, t, 0.0) for o, t in zip(oh2, tots)])
    run = run_scr[0:1, :] + jnp.sum(cnt, axis=0, keepdims=True)
    run_scr[...] = jnp.broadcast_to(run, run_scr.shape)
    cnt_ref[...] = jnp.broadcast_to(run, cnt_ref.shape)

    for c in chunks:
        gp = 1.0 / gsum[c]
        e21 = jnp.exp(v2[c] - v1[c])
        ew1 = gp / (1.0 + e21)
        ew2 = gp * e21 / (1.0 + e21)
        slab = jnp.where(lane == 0, i1[c] - EXPERT_LANE0,
               jnp.where(lane == 1, i2[c] - EXPERT_LANE0,
               jnp.where(lane == 2, ew1,
               jnp.where(lane == 3, ew2,
               jnp.where(lane == 4, rank1[c],
               jnp.where(lane == 5, rank2[c], 0.0))))))
        slab_ref[c * ROUTER_ROWS:(c + 1) * ROUTER_ROWS, :] = slab


def _outproj_router(merged, w_out, x, norm_w, w_router, b_router, tm=512):
    n = x.shape[0]
    row = lambda i: (i, 0)
    const = lambda i: (0, 0)
    return pl.pallas_call(
        _outproj_router_kernel,
        grid=(n // tm,),
        in_specs=[
            pl.BlockSpec((tm, D_MODEL), row),
            pl.BlockSpec((D_MODEL, D_MODEL), const),
            pl.BlockSpec((tm, D_MODEL), row),
            pl.BlockSpec((1, D_MODEL), const),
            pl.BlockSpec((D_MODEL, ROUTER_LANES), const),
            pl.BlockSpec((1, ROUTER_LANES), const),
        ],
        out_specs=[
            pl.BlockSpec((tm, D_MODEL), row),
            pl.BlockSpec((tm * ROW_SUB, LANES), row),
            pl.BlockSpec((tm, ROUTER_LANES), row),
            pl.BlockSpec((SUBLANES, ROUTER_LANES), const),
        ],
        out_shape=[
            jax.ShapeDtypeStruct((n, D_MODEL), F32),
            jax.ShapeDtypeStruct((n * ROW_SUB, LANES), F32),
            jax.ShapeDtypeStruct((n, ROUTER_LANES), F32),
            jax.ShapeDtypeStruct((SUBLANES, ROUTER_LANES), F32),
        ],
        scratch_shapes=[pltpu.VMEM((SUBLANES, ROUTER_LANES), F32)],
        compiler_params=_cparams("arbitrary"),
        name="outproj_router",
    )(merged, w_out, x, norm_w, w_router, b_router)


def _dispatch_kernel(slot_ref, hf_ref, xb_in_hbm, xb_hbm, sem, *, tm):
    del xb_in_hbm

    def copy(u, j):
        return pltpu.make_async_copy(_slab(hf_ref, u), _slab(xb_hbm, slot_ref[TOP_K * u + j]), sem)

    def start(u, c):
        for j in range(TOP_K):
            copy(u, j).start(priority=j)
        return c

    def wait(u, c):
        for j in range(TOP_K):
            copy(u, j).wait()
        return c

    lax.fori_loop(0, tm, start, 0, unroll=8)
    lax.fori_loop(0, tm, wait, 0, unroll=8)


def _dispatch(slot_flat, hf, cap, backing, tm=256):
    n = hf.shape[0] // ROW_SUB
    assert backing.shape == (cap * ROW_SUB, LANES)
    return pl.pallas_call(
        functools.partial(_dispatch_kernel, tm=tm),
        grid=(n // tm,),
        in_specs=[
            pl.BlockSpec((TOP_K * tm,), lambda i: (i,), memory_space=pltpu.SMEM),
            pl.BlockSpec((tm * ROW_SUB, LANES), lambda i: (i, 0)),
            pl.BlockSpec(memory_space=pl.ANY),
        ],
        out_specs=pl.BlockSpec(memory_space=pl.ANY),
        out_shape=jax.ShapeDtypeStruct((cap * ROW_SUB, LANES), F32),
        scratch_shapes=[pltpu.SemaphoreType.DMA(())],
        input_output_aliases={2: 0},
        compiler_params=_cparams("arbitrary"),
        name="moe_dispatch",
    )(slot_flat, hf, backing)


def _ffn_kernel(be_ref, nu_ref, par_ref, nxt_ref, x_ref, wg_hbm, wu_hbm, wd_hbm, o_ref,
                wg_f, wu_f, wd_f, wg_s, wu_s, wd_s, sems, *, layer):
    b = pl.program_id(0)
    used = b < nu_ref[0]
    prev_e = be_ref[jnp.maximum(b - 1, 0)]
    new_expert = jnp.logical_or(b == 0, be_ref[b] != prev_e)

    def fetch(e, p):
        return [pltpu.make_async_copy(src.at[layer, e], dst.at[p], sems.at[p, i])
                for i, (src, dst) in enumerate(((wg_hbm, wg_f), (wu_hbm, wu_f), (wd_hbm, wd_f)))]

    @pl.when(jnp.logical_and(used, new_expert))
    def _():
        for p in range(2):
            @pl.when(par_ref[b] == p)
            def _():
                @pl.when(b == 0)
                def _():
                    for cp in fetch(be_ref[0], p):
                        cp.start()

                @pl.when(nxt_ref[b] >= 0)
                def _():
                    for cp in fetch(nxt_ref[b], 1 - p):
                        cp.start()

                for cp in fetch(be_ref[b], p):
                    cp.wait()
                wg_s[...] = wg_f[p].astype(BF16)
                wu_s[...] = wu_f[p].astype(BF16)
                wd_s[...] = wd_f[p].astype(BF16)

    @pl.when(used)
    def _():
        slabs = MXU_COLS // LANES
        gt = up = None
        for kc in range(D_MODEL // MXU_COLS):
            xk = jnp.concatenate([x_ref[pl.ds(kc * slabs + s, MOE_ROWS, stride=ROW_SUB), :] for s in range(slabs)],
                                 axis=1).astype(BF16)
            ks = slice(kc * MXU_COLS, (kc + 1) * MXU_COLS)
            g_k = jnp.dot(xk, wg_s[ks, :], preferred_element_type=F32)
            u_k = jnp.dot(xk, wu_s[ks, :], preferred_element_type=F32)
            gt = g_k if gt is None else gt + g_k
            up = u_k if up is None else up + u_k
        hid = (gt * jax.nn.sigmoid(gt) * up).astype(BF16)
        for c in range(D_MODEL // MXU_COLS):
            y_c = jnp.dot(hid, wd_s[:, c * MXU_COLS:(c + 1) * MXU_COLS], preferred_element_type=F32)
            for s in range(slabs):
                o_ref[pl.ds(c * slabs + s, MOE_ROWS, stride=ROW_SUB), :] = y_c[:, s * LANES:(s + 1) * LANES]

    @pl.when(jnp.logical_not(used))
    def _():
        o_ref[...] = jnp.zeros_like(o_ref)


def _expert_ffn(layer, plan, xb, w_gate, w_up, w_down):
    nblk = xb.shape[0] // (MOE_ROWS * ROW_SUB)
    hbm = pl.BlockSpec(memory_space=pl.ANY)
    grid_spec = pltpu.PrefetchScalarGridSpec(
        num_scalar_prefetch=4,
        grid=(nblk,),
        in_specs=[
            pl.BlockSpec((MOE_ROWS * ROW_SUB, LANES), lambda b, be, nu, par, nxt: (jnp.minimum(b, nu[0] - 1), 0)),
            hbm, hbm, hbm,
        ],
        out_specs=pl.BlockSpec((MOE_ROWS * ROW_SUB, LANES), lambda b, be, nu, par, nxt: (b, 0)),
        scratch_shapes=[
            pltpu.VMEM((2, D_MODEL, EXPERT_HIDDEN), F32),
            pltpu.VMEM((2, D_MODEL, EXPERT_HIDDEN), F32),
            pltpu.VMEM((2, EXPERT_HIDDEN, D_MODEL), F32),
            pltpu.VMEM((D_MODEL, EXPERT_HIDDEN), BF16),
            pltpu.VMEM((D_MODEL, EXPERT_HIDDEN), BF16),
            pltpu.VMEM((EXPERT_HIDDEN, D_MODEL), BF16),
            pltpu.SemaphoreType.DMA((2, 3)),
        ],
    )
    return pl.pallas_call(
        functools.partial(_ffn_kernel, layer=layer),
        grid_spec=grid_spec,
        out_shape=jax.ShapeDtypeStruct(xb.shape, F32),
        compiler_params=_cparams("arbitrary"),
        name="expert_ffn",
    )(plan.block_expert, plan.n_used, plan.parity, plan.next_expert, xb, w_gate, w_up, w_down)


def _combine_kernel(slot_ref, slot_next_ref, x_ref, slab_ref, fw_ref, yb_hbm, o_ref,
                    buf00, buf01, buf10, buf11, sems, *, tm, n_tiles, final_norm):
    i = pl.program_id(0)
    bufs = ((buf00, buf01), (buf10, buf11))

    def copies(slots, p, u):
        return [pltpu.make_async_copy(_slab(yb_hbm, slots[TOP_K * u + j]), _slab(bufs[p][j], u), sems.at[p])
                for j in range(TOP_K)]

    def start_all(slots, p):
        def body(u, c):
            for j, cp in enumerate(copies(slots, p, u)):
                cp.start(priority=j)
            return c
        lax.fori_loop(0, tm, body, 0, unroll=8)

    def wait_all(p):
        def body(u, c):
            for cp in copies(slot_ref, p, u):
                cp.wait()
            return c
        lax.fori_loop(0, tm, body, 0, unroll=8)

    @pl.when(i == 0)
    def _():
        start_all(slot_ref, 0)

    for p in range(2):
        @pl.when(i % 2 == p)
        def _():
            @pl.when(i + 1 < n_tiles)
            def _():
                start_all(slot_next_ref, 1 - p)

            wait_all(p)
            slab = slab_ref[...]
            w1 = slab[:, 2:3]
            w2 = slab[:, 3:4]
            out = x_ref[...] + (_rows_to_2d(bufs[p][0], tm) * w1 + _rows_to_2d(bufs[p][1], tm) * w2)
            if final_norm:
                out = _rms(out, fw_ref[...])
            o_ref[...] = out


def _combine(slot_flat, x, slab, final_w, yb, final_norm, tm=256):
    n = x.shape[0]
    n_tiles = n // tm
    row = lambda i: (i, 0)
    row_buf = pltpu.VMEM((tm * ROW_SUB, LANES), F32)
    return pl.pallas_call(
        functools.partial(_combine_kernel, tm=tm, n_tiles=n_tiles, final_norm=final_norm),
        grid=(n_tiles,),
        in_specs=[
            pl.BlockSpec((TOP_K * tm,), lambda i: (i,), memory_space=pltpu.SMEM),
            pl.BlockSpec((TOP_K * tm,), lambda i: (jnp.minimum(i + 1, n_tiles - 1),), memory_space=pltpu.SMEM),
            pl.BlockSpec((tm, D_MODEL), row),
            pl.BlockSpec((tm, ROUTER_LANES), row),
            pl.BlockSpec((1, D_MODEL), lambda i: (0, 0)),
            pl.BlockSpec(memory_space=pl.ANY),
        ],
        out_specs=pl.BlockSpec((tm, D_MODEL), row),
        out_shape=jax.ShapeDtypeStruct((n, D_MODEL), F32),
        scratch_shapes=[row_buf, row_buf, row_buf, row_buf, pltpu.SemaphoreType.DMA((2,))],
        compiler_params=_cparams("arbitrary"),
        name="moe_combine",
    )(slot_flat, slot_flat, x, slab, final_w, yb)


class MoePlan(NamedTuple):
    slot: jax.Array
    block_expert: jax.Array
    n_used: jax.Array
    parity: jax.Array
    next_expert: jax.Array
    cap: int


def _moe_plan(slab, counts_row, n_tokens):
    eid = slab[:, 0:TOP_K].astype(jnp.int32)
    rank = slab[:, 4:4 + TOP_K].astype(jnp.int32)
    counts = counts_row[EXPERT_LANE0:EXPERT_LANE0 + N_EXPERTS].astype(jnp.int32)
    padded = (counts + MOE_ROWS - 1) // MOE_ROWS * MOE_ROWS
    pad_ends = jnp.cumsum(padded)
    pad_starts = pad_ends - padded
    slot = pad_starts[eid.reshape(-1)] + rank.reshape(-1)
    nblk = n_tokens * TOP_K // MOE_ROWS + N_EXPERTS
    block_start = jnp.arange(nblk, dtype=jnp.int32) * MOE_ROWS
    block_expert = jnp.minimum(
        jnp.sum((pad_ends[None, :] <= block_start[:, None]).astype(jnp.int32), axis=1),
        N_EXPERTS - 1).astype(jnp.int32)
    n_used = (pad_ends[-1:] // MOE_ROWS).astype(jnp.int32)
    blk = jnp.arange(nblk, dtype=jnp.int32)
    prev_expert = jnp.concatenate([block_expert[:1], block_expert[:-1]])
    first = (blk < n_used[0]) & ((blk == 0) | (block_expert != prev_expert))
    parity = ((jnp.cumsum(first.astype(jnp.int32)) - 1) % 2).astype(jnp.int32)
    later_first = first[None, :] & (blk[None, :] > blk[:, None])
    next_first = jnp.min(jnp.where(later_first, blk[None, :], nblk), axis=1)
    next_expert = jnp.where(next_first < nblk, block_expert[jnp.minimum(next_first, nblk - 1)], -1)
    return MoePlan(slot, block_expert, n_used, parity, next_expert.astype(jnp.int32), nblk * MOE_ROWS)


def kernel(x, attn_norm_w, w_in, shift_mix, attn_sinks, rwkv_w0, rwkv_w_up, rwkv_a0, rwkv_a_up, rwkv_g_up, rwkv_k_k, rwkv_k_a, rwkv_r_k, rwkv_ln_w, rwkv_ln_b, vres_down, vres_mix, vres_up, vres_v0, w_branch_a, w_branch_b, w_out, ffn_norm_w, router_group_w, router_group_b, router_expert_w, router_expert_b, expert_w_gate, expert_w_up, expert_w_down, final_norm_w):
    batch, seq, _ = x.shape
    n = batch * seq
    depth = w_in.shape[0]
    xf = x.reshape(n, D_MODEL)
    w_in_bf = jnp.swapaxes(w_in, 1, 2).astype(BF16)

    zero_c = jnp.zeros((depth, RWKV_WIDTH), F32)
    v0_all = jnp.concatenate([zero_c[:1], vres_v0.astype(F32)], axis=0)
    pv_pre_all = jnp.stack([rwkv_w0, rwkv_a0, rwkv_k_k, rwkv_k_a, v0_all, zero_c, zero_c, zero_c], axis=1)
    pv_post_all = jnp.stack([rwkv_ln_w, rwkv_ln_b, rwkv_r_k.reshape(depth, RWKV_WIDTH),
                             zero_c, zero_c, zero_c, zero_c, zero_c], axis=1)
    mix_r_all = shift_mix[:, None, :RKV_COLS]
    mix_l_all = jnp.pad(shift_mix[:, None, RKV_COLS:], ((0, 0), (0, 0), (0, LORA_COLS - LORA_WIDTH)))
    wup_all = jnp.pad(rwkv_w_up.astype(BF16), ((0, 0), (0, LANES - DECAY_LORA), (0, 0)))
    aup_all = jnp.pad(rwkv_a_up.astype(BF16), ((0, 0), (DECAY_LORA, LANES - DECAY_LORA - AAA_LORA), (0, 0)))
    gup_all = jnp.pad(rwkv_g_up.astype(BF16), ((0, 0), (0, 2 * LANES - GATE_LORA), (0, 0)))
    vup_all = jnp.pad(vres_up.astype(BF16), ((1, 0), (0, LANES - MV_LORA), (0, 0)))
    vd_all = jnp.pad(vres_down.astype(BF16), ((0, 0), (0, 0), (0, LANES - MV_LORA)))
    mix_v_all = jnp.pad(vres_mix, ((0, 0), (0, LANES - MV_LORA)))[:, None, :]
    router_pad = ROUTER_LANES - N_GROUPS - N_EXPERTS
    w_router_all = jnp.pad(jnp.concatenate([router_group_w, router_expert_w], axis=2),
                           ((0, 0), (0, 0), (0, router_pad)))
    b_router_all = jnp.pad(jnp.concatenate([router_group_b, router_expert_b], axis=1),
                           ((0, 0), (0, router_pad)))[:, None, :]
    attn_norm_all = attn_norm_w[:, None, :]
    ffn_norm_all = ffn_norm_w[:, None, :]
    w_out_bf = w_out.astype(BF16)
    sinks_all = attn_sinks.astype(F32)

    v_first = None
    slot_backing = None
    slot_rows = (n * TOP_K // MOE_ROWS + N_EXPERTS) * MOE_ROWS
    for i in range(depth):
        has_vres = i > 0
        w_gates = w_in_bf[i, GATE_COL0:, :]
        mix_r, mix_l, pv_pre, pv_post = mix_r_all[i], mix_l_all[i], pv_pre_all[i], pv_post_all[i]
        wup, aup, gup, vup = wup_all[i], aup_all[i], gup_all[i], vup_all[i]
        vd, mix_v = (vd_all[i - 1], mix_v_all[i - 1]) if has_vres else (None, None)
        w_router, b_router = w_router_all[i], b_router_all[i]

        need_zeros = slot_backing is None
        proj = _inproj(i, xf, attn_norm_all[i], w_in_bf, w_gates, vd,
                       zero_rows=slot_rows * ROW_SUB if need_zeros else 0)
        qkv, rkv, lora, gates = proj[:4]
        zv = proj[4] if has_vres else None
        if need_zeros:
            slot_backing = proj[-1]
        attn = _attention(qkv, sinks_all[i], seq)
        y, v_new = _rwkv_mixer(rkv, lora, v_first, zv, mix_r, mix_l, mix_v, pv_pre,
                               wup, aup, gup, vup, pv_post, seq)
        if not has_vres:
            v_first = v_new
        merged = _merge(i, attn, y, w_branch_a, w_branch_b, gates)

        x_mid, hf, slab, counts = _outproj_router(
            merged, w_out_bf[i], xf, ffn_norm_all[i], w_router, b_router)
        plan = _moe_plan(slab, counts[0], n)
        xb = _dispatch(plan.slot, hf, plan.cap, slot_backing)
        yb = _expert_ffn(i, plan, xb, expert_w_gate, expert_w_up, expert_w_down)
        slot_backing = yb
        xf = _combine(plan.slot, x_mid, slab, final_norm_w.reshape(1, D_MODEL), yb, final_norm=(i == depth - 1))
    return xf.reshape(batch, seq, D_MODEL)
```

```python
import functools
import math
from typing import NamedTuple

import jax
import jax.numpy as jnp
from jax import lax
from jax.experimental import pallas as pl
from jax.experimental.pallas import tpu as pltpu

F32 = jnp.float32
BF16 = jnp.bfloat16

D_MODEL = 2048
HEAD_DIM = 64
N_Q_HEADS = 16
N_KV_HEADS = 4
GQA_GROUP = N_Q_HEADS // N_KV_HEADS
ATTN_WIDTH = N_Q_HEADS * HEAD_DIM
KV_WIDTH = N_KV_HEADS * HEAD_DIM
WINDOW = 128
ATTN_BLOCK = 128
ATTN_SUB = 2

RWKV_HEAD = 64
N_RWKV_HEADS = 16
RWKV_WIDTH = N_RWKV_HEADS * RWKV_HEAD
DECAY_LORA = 64
AAA_LORA = 64
MV_LORA = 32
GATE_LORA = 160
RWKV_GN_EPS = 64e-5
DECAY_SCALE = math.exp(-0.5)
KK_NORM_FLOOR = 1e-12
LORA_WIDTH = DECAY_LORA + AAA_LORA + GATE_LORA

N_GROUPS = 4
EXPERTS_PER_GROUP = 8
N_EXPERTS = N_GROUPS * EXPERTS_PER_GROUP
TOP_K = 2
EXPERT_HIDDEN = D_MODEL // 4
NORM_EPS = 1e-5

LANES = 128
SUBLANES = 8
MXU_COLS = 256

SEG = 512
QKV_COLS = ATTN_WIDTH + 2 * KV_WIDTH
RKV_COLS = 3 * RWKV_WIDTH
LORA_COLS = SEG
GATE_COLS = 2 * D_MODEL
QKV_TILES = QKV_COLS // SEG
RKV_TILES = RKV_COLS // SEG
LORA_TILES = LORA_COLS // SEG
GATE_TILES = GATE_COLS // SEG
IN_TILES = QKV_TILES + RKV_TILES + LORA_TILES + GATE_TILES
FIRST_GATE_TILE = QKV_TILES + RKV_TILES + LORA_TILES
ZERO_STEPS = 16
GATE_COL0 = QKV_COLS + RKV_COLS + LORA_WIDTH

SCAN_CHUNK = 64
SCAN_SUB = 2
SCAN_GROUP = 4
MOE_ROWS = 256
ROUTER_LANES = LANES
ROUTER_ROWS = 128
EXPERT_LANE0 = N_GROUPS

ROW_SUB = D_MODEL // LANES

VMEM_LIMIT = 56 * 1024 * 1024


def _rows_to_2d(slab_ref, rows):
    return jnp.concatenate([slab_ref[pl.ds(s, rows, stride=ROW_SUB), :] for s in range(ROW_SUB)], axis=1)


def _store_rows(slab_ref, val):
    rows = val.shape[0]
    for s in range(ROW_SUB):
        slab_ref[pl.ds(s, rows, stride=ROW_SUB), :] = val[:, s * LANES:(s + 1) * LANES]


def _slab(ref, row):
    return ref.at[pl.ds(pl.multiple_of(row * ROW_SUB, ROW_SUB), ROW_SUB)]


def _cparams(*sem):
    return pltpu.CompilerParams(dimension_semantics=sem, vmem_limit_bytes=VMEM_LIMIT)


def _bdot(a, b):
    return jnp.dot(a.astype(BF16), b.astype(BF16), preferred_element_type=F32)


def _bdot_nt(a, b):
    return lax.dot_general(a.astype(BF16), b.astype(BF16), (((1,), (1,)), ((), ())),
                           preferred_element_type=F32)


def _bdot_tn(a, b):
    return lax.dot_general(a.astype(BF16), b.astype(BF16), (((0,), (0,)), ((), ())),
                           preferred_element_type=F32)


def _dot_split_rhs(w01, x, passes):
    acc, rem = None, x
    for p in range(passes):
        part = rem.astype(BF16)
        d = jnp.dot(w01, part, preferred_element_type=F32)
        acc = d if acc is None else acc + d
        if p + 1 < passes:
            rem = rem - part.astype(F32)
    return acc


def _dot_split_lhs(x, w01, passes):
    acc, rem = None, x
    for p in range(passes):
        part = rem.astype(BF16)
        d = jnp.dot(part, w01, preferred_element_type=F32)
        acc = d if acc is None else acc + d
        if p + 1 < passes:
            rem = rem - part.astype(F32)
    return acc


def _rms(x, w):
    ms = jnp.mean(x * x, axis=-1, keepdims=True)
    return x * lax.rsqrt(ms + NORM_EPS) * w


def _inproj_kernel(*refs, has_vres, has_zeros):
    refs = list(refs)
    x_ref, nw_ref, w_ref, wg_ref = refs[:4]
    refs = refs[4:]
    vd_ref = refs.pop(0) if has_vres else None
    qkv_ref, rkv_ref, lora_ref, gate_ref = refs[:4]
    refs = refs[4:]
    zv_ref = refs.pop(0) if has_vres else None
    zeros_ref = refs.pop(0) if has_zeros else None
    (h_scr,) = refs
    j = pl.program_id(1)

    if has_zeros:
        zeros_ref[...] = jnp.zeros_like(zeros_ref)

    @pl.when(j == 0)
    def _():
        h_scr[...] = _rms(x_ref[...], nw_ref[...]).astype(BF16)
        if has_vres:
            zv_ref[...] = jnp.dot(h_scr[...], vd_ref[...], preferred_element_type=F32)

    def project(wt_ref, out_ref):
        for c in range(SEG // MXU_COLS):
            cs = slice(c * MXU_COLS, (c + 1) * MXU_COLS)
            acc = lax.dot_general(h_scr[...], wt_ref[cs, :], (((1,), (1,)), ((), ())),
                                  preferred_element_type=F32)
            out_ref[:, cs] = acc.astype(out_ref.dtype)

    @pl.when(j < QKV_TILES)
    def _():
        project(w_ref, qkv_ref)

    @pl.when((j >= QKV_TILES) & (j < QKV_TILES + RKV_TILES))
    def _():
        project(w_ref, rkv_ref)

    @pl.when(j == QKV_TILES + RKV_TILES)
    def _():
        project(w_ref, lora_ref)

    @pl.when(j >= FIRST_GATE_TILE)
    def _():
        project(wg_ref, gate_ref)


def _inproj(layer, x, norm_w, w_in, w_gates, vd, zero_rows=0, tm=1024):
    n = x.shape[0]
    has_vres = vd is not None
    s_rkv = QKV_TILES
    in_specs = [
        pl.BlockSpec((tm, D_MODEL), lambda i, j: (i, 0)),
        pl.BlockSpec((1, D_MODEL), lambda i, j: (0, 0)),
        pl.BlockSpec((None, SEG, D_MODEL), lambda i, j: (layer, jnp.minimum(j, FIRST_GATE_TILE - 1), 0)),
        pl.BlockSpec((SEG, D_MODEL), lambda i, j: (jnp.clip(j - FIRST_GATE_TILE, 0, GATE_TILES - 1), 0)),
    ]
    args = [x, norm_w, w_in, w_gates]
    out_specs = [
        pl.BlockSpec((tm, SEG), lambda i, j: (i, jnp.clip(j, 0, QKV_TILES - 1))),
        pl.BlockSpec((tm, SEG), lambda i, j: (i, jnp.clip(j - s_rkv, 0, RKV_TILES - 1))),
        pl.BlockSpec((tm, SEG), lambda i, j: (i, 0)),
        pl.BlockSpec((tm, SEG), lambda i, j: (i, jnp.clip(j - FIRST_GATE_TILE, 0, GATE_TILES - 1))),
    ]
    out_shape = [
        jax.ShapeDtypeStruct((n, QKV_COLS), BF16),
        jax.ShapeDtypeStruct((n, RKV_COLS), F32),
        jax.ShapeDtypeStruct((n, LORA_COLS), F32),
        jax.ShapeDtypeStruct((n, GATE_COLS), BF16),
    ]
    if has_vres:
        in_specs.append(pl.BlockSpec((D_MODEL, LANES), lambda i, j: (0, 0)))
        args.append(vd)
        out_specs.append(pl.BlockSpec((tm, LANES), lambda i, j: (i, 0)))
        out_shape.append(jax.ShapeDtypeStruct((n, LANES), F32))
    if zero_rows:
        blk = zero_rows // ((n // tm) * ZERO_STEPS)
        assert blk * (n // tm) * ZERO_STEPS == zero_rows and blk % SUBLANES == 0
        out_specs.append(pl.BlockSpec((blk, LANES), lambda i, j: (i * ZERO_STEPS + jnp.minimum(j, ZERO_STEPS - 1), 0)))
        out_shape.append(jax.ShapeDtypeStruct((zero_rows, LANES), F32))
    return pl.pallas_call(
        functools.partial(_inproj_kernel, has_vres=has_vres, has_zeros=bool(zero_rows)),
        grid=(n // tm, IN_TILES),
        in_specs=in_specs,
        out_specs=out_specs,
        out_shape=out_shape,
        scratch_shapes=[pltpu.VMEM((tm, D_MODEL), BF16)],
        compiler_params=_cparams("parallel", "arbitrary"),
        name="inproj",
    )(*args)


def _attn_kernel(sink_ref, q_ref, kp_ref, kc_ref, vp_ref, vc_ref, bias_ref, o_ref, *, tiles_per_seq):
    first = (pl.program_id(0) % tiles_per_seq) == 0
    col = lax.broadcasted_iota(jnp.int32, (ATTN_BLOCK, 2 * ATTN_BLOCK), 1)
    pad_keys = jnp.logical_and(first, col < ATTN_BLOCK)
    scale = HEAD_DIM ** -0.5
    items = [(sb, hk) for sb in range(ATTN_SUB) for hk in range(N_KV_HEADS)]

    def window(prev_ref, cur_ref, sb, hk):
        ks = slice(hk * HEAD_DIM, (hk + 1) * HEAD_DIM)
        rows = slice(sb * ATTN_BLOCK, (sb + 1) * ATTN_BLOCK)
        prev = prev_ref[:, ks] if sb == 0 else cur_ref[(sb - 1) * ATTN_BLOCK:sb * ATTN_BLOCK, ks]
        return jnp.concatenate([prev, cur_ref[rows, ks]], axis=0)

    scores = {}
    for sb, hk in items:
        kw = window(kp_ref, kc_ref, sb, hk)
        for g in range(GQA_GROUP):
            h = hk * GQA_GROUP + g
            qh = q_ref[sb * ATTN_BLOCK:(sb + 1) * ATTN_BLOCK, h * HEAD_DIM:(h + 1) * HEAD_DIM]
            s = lax.dot_general(qh, kw, (((1,), (1,)), ((), ())), preferred_element_type=F32)
            s = s * scale + bias_ref[h]
            if sb == 0:
                s = jnp.where(pad_keys, -jnp.inf, s)
            scores[(sb, h)] = s
    row_max = {key: jnp.max(s, axis=-1, keepdims=True) for key, s in scores.items()}
    m = {key: jnp.maximum(row_max[key], sink_ref[key[1]]) for key in scores}
    p = {key: jnp.exp(scores[key] - m[key]) for key in scores}
    row_sum = {key: jnp.sum(p[key], axis=-1, keepdims=True) for key in scores}
    denoms = {key: row_sum[key] + jnp.exp(sink_ref[key[1]] - m[key]) for key in scores}
    probs = {key: p[key].astype(BF16) for key in scores}
    for sb, hk in items:
        vw = window(vp_ref, vc_ref, sb, hk)
        for g in range(GQA_GROUP):
            h = hk * GQA_GROUP + g
            o = jnp.dot(probs[(sb, h)], vw, preferred_element_type=F32) / denoms[(sb, h)]
            o_ref[sb * ATTN_BLOCK:(sb + 1) * ATTN_BLOCK, h * HEAD_DIM:(h + 1) * HEAD_DIM] = o.astype(BF16)


def _attn_bias():
    qi = jnp.arange(ATTN_BLOCK)[:, None]
    kj = jnp.arange(2 * ATTN_BLOCK)[None, :]
    dist = qi + ATTN_BLOCK - kj
    valid = (dist >= 0) & (dist < WINDOW)
    slopes = jnp.exp2(-8.0 * jnp.arange(1, N_Q_HEADS + 1, dtype=F32) / N_Q_HEADS)
    bias = -slopes[:, None, None] * dist.astype(F32)[None]
    return jnp.where(valid[None], bias, -jnp.inf)


def _attention(qkv, sinks, seq):
    n = qkv.shape[0]
    rows = ATTN_SUB * ATTN_BLOCK
    tiles_per_seq = seq // rows
    kcol = ATTN_WIDTH // KV_WIDTH
    prev = lambda i: jnp.maximum(i * ATTN_SUB - 1, 0)
    return pl.pallas_call(
        functools.partial(_attn_kernel, tiles_per_seq=tiles_per_seq),
        grid=(n // rows,),
        in_specs=[
            pl.BlockSpec(memory_space=pltpu.SMEM),
            pl.BlockSpec((rows, ATTN_WIDTH), lambda i: (i, 0)),
            pl.BlockSpec((ATTN_BLOCK, KV_WIDTH), lambda i: (prev(i), kcol)),
            pl.BlockSpec((rows, KV_WIDTH), lambda i: (i, kcol)),
            pl.BlockSpec((ATTN_BLOCK, KV_WIDTH), lambda i: (prev(i), kcol + 1)),
            pl.BlockSpec((rows, KV_WIDTH), lambda i: (i, kcol + 1)),
            pl.BlockSpec((N_Q_HEADS, ATTN_BLOCK, 2 * ATTN_BLOCK), lambda i: (0, 0, 0)),
        ],
        out_specs=pl.BlockSpec((rows, ATTN_WIDTH), lambda i: (i, 0)),
        out_shape=jax.ShapeDtypeStruct((n, ATTN_WIDTH), BF16),
        compiler_params=_cparams("parallel"),
        name="swa_attention",
    )(sinks, qkv, qkv, qkv, qkv, qkv, _attn_bias())


def _head_ones():
    ch = jnp.arange(MXU_COLS) // RWKV_HEAD
    return (ch[:, None] == ch[None, :]).astype(BF16)


def _head_sums(x, ones, passes):
    parts = [_dot_split_lhs(x[:, c:c + MXU_COLS], ones, passes) for c in range(0, RWKV_WIDTH, MXU_COLS)]
    return jnp.concatenate(parts, axis=1)


def _rwkv_kernel(*refs, has_vres):
    refs = list(refs)
    rkv_ref, lo_ref = refs[:2]
    refs = refs[2:]
    if has_vres:
        vf_ref, zv_ref, mixv_ref = refs[:3]
        refs = refs[3:]
    mixr_ref, mixl_ref, pv_ref, wup_ref, aup_ref, gup_ref, vup_ref, ones_ref, pp_ref = refs[:9]
    refs = refs[9:]
    y_ref = refs.pop(0)
    v_out = None if has_vres else refs.pop(0)
    s_scr, prev_rkv, prev_lo = refs[:3]
    prev_zv = refs[3] if has_vres else None

    L = SCAN_CHUNK
    W = SCAN_GROUP * RWKV_HEAD
    n_seq, step_rows = rkv_ref.shape[0], rkv_ref.shape[1]
    n_grp = N_RWKV_HEADS // SCAN_GROUP
    ones = ones_ref[...]

    @pl.when(pl.program_id(0) == 0)
    def _():
        s_scr[...] = jnp.zeros_like(s_scr)
        prev_rkv[...] = jnp.zeros_like(prev_rkv)
        prev_lo[...] = jnp.zeros_like(prev_lo)
        if has_vres:
            prev_zv[...] = jnp.zeros_like(prev_zv)

    def lerp_shift(z_ref, carry_ref, b, mix):
        z = z_ref[b]
        row = lax.broadcasted_iota(jnp.int32, z.shape, 0)
        back = jnp.where(row == 0, carry_ref[b, SUBLANES - 1:SUBLANES, :], pltpu.roll(z, 1, 0))
        carry_ref[b] = z_ref[b, step_rows - SUBLANES:step_rows, :]
        return z + (back - z) * mix

    w0, a0, k_k, k_a, v0 = (pv_ref[i:i + 1, :] for i in range(5))
    pre = []
    for b in range(n_seq):
        zs = lerp_shift(rkv_ref, prev_rkv, b, mixr_ref[...])
        los = lerp_shift(lo_ref, prev_lo, b, mixl_ref[...])
        r = zs[:, :RWKV_WIDTH]
        kr = zs[:, RWKV_WIDTH:2 * RWKV_WIDTH]
        vr = zs[:, 2 * RWKV_WIDTH:]
        wa = los[:, :LANES]
        col = lax.broadcasted_iota(jnp.int32, wa.shape, 1)
        wa_in = jnp.where(col < DECAY_LORA, jnp.tanh(wa), wa).astype(BF16)
        gd = los[:, LANES:3 * LANES]
        col_g = lax.broadcasted_iota(jnp.int32, gd.shape, 1)
        g_in = jnp.where(col_g < GATE_LORA, jax.nn.sigmoid(gd), gd).astype(BF16)
        dw = jnp.dot(wa_in, wup_ref[...], preferred_element_type=F32)
        da = jnp.dot(wa_in, aup_ref[...], preferred_element_type=F32)
        g = jnp.dot(g_in, gup_ref[...], preferred_element_type=F32)
        lw = -DECAY_SCALE * jax.nn.sigmoid(w0 + dw)
        a = jax.nn.sigmoid(a0 + da)
        if has_vres:
            zvs = lerp_shift(zv_ref, prev_zv, b, mixv_ref[...])
            dv = jnp.dot(zvs.astype(BF16), vup_ref[...], preferred_element_type=F32)
            vr = vr + (vf_ref[b] - vr) * jax.nn.sigmoid(v0 + dv)
        else:
            v_out[b] = vr
        kk0 = kr * k_k
        ss = _head_sums(kk0 * kk0, ones, 2)
        kk = kk0 * lax.rsqrt(jnp.maximum(ss, KK_NORM_FLOOR ** 2))
        pre.append(dict(r=r, lw=lw, k=kr * (1.0 + (a - 1.0) * k_a), v=vr, al=-kk, be=kk * a, g=g))

    row = lax.broadcasted_iota(jnp.int32, (L, L), 0)
    col = lax.broadcasted_iota(jnp.int32, (L, L), 1)
    tri = (row >= col).astype(BF16)
    t_w = lax.broadcasted_iota(jnp.int32, (L, W), 0)
    lane_w = lax.broadcasted_iota(jnp.int32, (L, W), 1)
    s_w = lane_w % RWKV_HEAD
    strict = t_w > s_w
    incl = t_w >= s_w
    eye = (t_w == s_w).astype(F32)
    blk_w = lane_w // RWKV_HEAD
    bd_mask = (lax.broadcasted_iota(jnp.int32, (W, W), 0) // RWKV_HEAD
               == lax.broadcasted_iota(jnp.int32, (W, W), 1) // RWKV_HEAD)

    def blockdiag(w):
        tiled = jnp.concatenate([w.astype(BF16)] * SCAN_GROUP, axis=0)
        return jnp.where(bd_mask, tiled, jnp.zeros_like(tiled))

    ln_w, ln_b, r_k = (pp_ref[i:i + 1, :] for i in range(3))
    items = [(b, g) for b in range(n_seq) for g in range(n_grp)]
    for sub in range(step_rows // L):
        rows = slice(sub * L, (sub + 1) * L)
        ops = {}
        for b in range(n_seq):
            p = pre[b]
            lw = p['lw'][rows]
            cum = _dot_split_rhs(tri, lw, 2)
            p_inc = jnp.exp(cum)
            p_exc = jnp.exp(cum - lw)
            p_inv = jnp.exp(-cum)
            p_last = p_inc[L - 1:L, :]
            a_t = p['al'][rows] * p_exc
            r_t = p['r'][rows] * p_inc
            b_t = p['be'][rows] * p_inv
            k_t = p['k'][rows] * p_inv
            b_end = b_t * p_last
            k_end = k_t * p_last
            v = p['v'][rows]
            for g in range(n_grp):
                gs = slice(g * W, (g + 1) * W)
                ops[(b, g)] = dict(
                    ar=jnp.concatenate([a_t[:, gs], r_t[:, gs]], axis=0),
                    bt=b_t[:, gs], kt=k_t[:, gs], v=v[:, gs], p_last=p_last[:, gs],
                    bke=jnp.concatenate([b_end[:, gs], k_end[:, gs]], axis=0))

        gm, xs, av, s0 = {}, {}, {}, {}
        for it in items:
            o = ops[it]
            rhs = jnp.concatenate([blockdiag(o['bt']), blockdiag(o['kt'])], axis=0)
            gm[it] = _bdot_nt(o['ar'], rhs)
            s0[it] = s_scr[it[0], it[1]]
            xs[it] = _bdot_nt(o['ar'], blockdiag(s0[it]))
        a_ab, a_rb = {}, {}
        for it in items:
            g = gm[it]
            a_ab[it] = jnp.where(strict, g[:L, :W], 0.0)
            a_rb[it] = jnp.where(incl, g[L:, :W], 0.0)
            a_kk = jnp.concatenate([jnp.where(strict, g[:L, W:], 0.0), jnp.where(incl, g[L:, W:], 0.0)], axis=0)
            av[it] = _bdot(a_kk, blockdiag(ops[it]['v']))

        pw = {it: _bdot(a_ab[it], blockdiag(a_ab[it])) for it in items}
        tinv = {it: eye + a_ab[it] for it in items}
        span = 2
        while 2 * span < L:
            both = {it: _bdot(jnp.concatenate([pw[it], tinv[it]], axis=0), blockdiag(pw[it])) for it in items}
            tinv = {it: tinv[it] + both[it][L:] for it in items}
            pw = {it: both[it][:L] for it in items}
            span *= 2
        tinv = {it: tinv[it] + _bdot(tinv[it], blockdiag(pw[it])) for it in items}

        u = {it: _bdot(tinv[it], blockdiag(xs[it][:L] + av[it][:L])) for it in items}
        y = {it: xs[it][L:] + av[it][L:] + _bdot(a_rb[it], blockdiag(u[it])) for it in items}
        for it in items:
            o = ops[it]
            full = _bdot_tn(jnp.concatenate([u[it], o['v']], axis=0), o['bke'])
            upd = full[:RWKV_HEAD]
            for h in range(1, SCAN_GROUP):
                upd = jnp.where(blk_w == h, full[h * RWKV_HEAD:(h + 1) * RWKV_HEAD], upd)
            s_scr[it[0], it[1]] = s0[it] * o['p_last'] + upd

        inv_n = 1.0 / RWKV_HEAD
        for b in range(n_seq):
            p = pre[b]
            yb = jnp.concatenate([y[(b, g)] for g in range(n_grp)], axis=1)
            d = yb - _head_sums(yb, ones, 2) * inv_n
            var = _head_sums(d * d, ones, 2) * inv_n
            yn = d * lax.rsqrt(var + RWKV_GN_EPS) * ln_w + ln_b
            bonus = _head_sums(p['r'][rows] * p['k'][rows] * r_k, ones, 2) * p['v'][rows]
            y_ref[b, rows, :] = ((yn + bonus) * p['g'][rows]).astype(BF16)


def _rwkv_mixer(rkv, lora, v_first, zv, mix_r, mix_l, mix_v, pv_pre, wup, aup, gup, vup, pv_post, seq):
    n = rkv.shape[0]
    n_seq = n // seq
    has_vres = v_first is not None
    step_rows = SCAN_SUB * SCAN_CHUNK
    timed = lambda cols: pl.BlockSpec((n_seq, step_rows, cols), lambda t: (0, t, 0))
    const = lambda shape: pl.BlockSpec(shape, lambda t: (0, 0))
    by_seq = lambda a: a.reshape(n_seq, seq, a.shape[-1])
    in_specs = [timed(RKV_COLS), timed(LORA_COLS)]
    args = [by_seq(rkv), by_seq(lora)]
    if has_vres:
        in_specs += [timed(RWKV_WIDTH), timed(LANES), const((1, LANES))]
        args += [by_seq(v_first), by_seq(zv), mix_v]
    in_specs += [
        const((1, RKV_COLS)), const((1, LORA_COLS)), const((SUBLANES, RWKV_WIDTH)),
        const((LANES, RWKV_WIDTH)), const((LANES, RWKV_WIDTH)), const((2 * LANES, RWKV_WIDTH)),
        const((LANES, RWKV_WIDTH)), const((MXU_COLS, MXU_COLS)), const((SUBLANES, RWKV_WIDTH)),
    ]
    args += [mix_r, mix_l, pv_pre, wup, aup, gup, vup, _head_ones(), pv_post]
    out_specs = [timed(RWKV_WIDTH)]
    out_shape = [jax.ShapeDtypeStruct((n_seq, seq, RWKV_WIDTH), BF16)]
    if not has_vres:
        out_specs.append(timed(RWKV_WIDTH))
        out_shape.append(jax.ShapeDtypeStruct((n_seq, seq, RWKV_WIDTH), F32))
    scratch = [
        pltpu.VMEM((n_seq, N_RWKV_HEADS // SCAN_GROUP, RWKV_HEAD, SCAN_GROUP * RWKV_HEAD), F32),
        pltpu.VMEM((n_seq, SUBLANES, RKV_COLS), F32),
        pltpu.VMEM((n_seq, SUBLANES, LORA_COLS), F32),
    ]
    if has_vres:
        scratch.append(pltpu.VMEM((n_seq, SUBLANES, LANES), F32))
    outs = pl.pallas_call(
        functools.partial(_rwkv_kernel, has_vres=has_vres),
        grid=(seq // step_rows,),
        in_specs=in_specs,
        out_specs=out_specs,
        out_shape=out_shape,
        scratch_shapes=scratch,
        compiler_params=_cparams("arbitrary"),
        name="rwkv_mixer",
    )(*args)
    y = outs[0].reshape(n, RWKV_WIDTH)
    return y, (None if has_vres else outs[1].reshape(n, RWKV_WIDTH))


def _merge_kernel(a_ref, y_ref, wa_ref, wb_ref, ga_ref, gb_ref, o_ref):
    for c in range(o_ref.shape[1] // MXU_COLS):
        cs = slice(c * MXU_COLS, (c + 1) * MXU_COLS)
        pa = jnp.dot(a_ref[...], wa_ref[:, cs].astype(BF16), preferred_element_type=F32)
        pb = jnp.dot(y_ref[...], wb_ref[:, cs].astype(BF16), preferred_element_type=F32)
        ga = jax.nn.sigmoid(ga_ref[:, cs].astype(F32))
        gb = jax.nn.sigmoid(gb_ref[:, cs].astype(F32))
        o_ref[:, cs] = (ga * pa + gb * pb).astype(BF16)


def _merge(layer, attn, y, wa, wb, gates, tm=1024, tn=1024):
    n = attn.shape[0]
    nj = D_MODEL // tn
    return pl.pallas_call(
        _merge_kernel,
        grid=(n // tm, nj),
        in_specs=[
            pl.BlockSpec((tm, ATTN_WIDTH), lambda i, j: (i, 0)),
            pl.BlockSpec((tm, RWKV_WIDTH), lambda i, j: (i, 0)),
            pl.BlockSpec((None, ATTN_WIDTH, tn), lambda i, j: (layer, 0, j)),
            pl.BlockSpec((None, RWKV_WIDTH, tn), lambda i, j: (layer, 0, j)),
            pl.BlockSpec((tm, tn), lambda i, j: (i, j)),
            pl.BlockSpec((tm, tn), lambda i, j: (i, j + nj)),
        ],
        out_specs=pl.BlockSpec((tm, tn), lambda i, j: (i, j)),
        out_shape=jax.ShapeDtypeStruct((n, D_MODEL), BF16),
        compiler_params=_cparams("parallel", "arbitrary"),
        name="gated_merge",
    )(attn, y, wa, wb, gates, gates)


def _outproj_router_kernel(m_ref, wo_ref, x_ref, nw_ref, wr_ref, br_ref,
                           xo_ref, hf_ref, slab_ref, plan_ref, cnt_ref, run_scr):
    @pl.when(pl.program_id(0) == 0)
    def _():
        run_scr[...] = jnp.zeros_like(run_scr)

    tm = x_ref.shape[0]
    sq = jnp.zeros((tm, LANES), F32)
    for c in range(D_MODEL // MXU_COLS):
        cs = slice(c * MXU_COLS, (c + 1) * MXU_COLS)
        xn_c = x_ref[:, cs] + jnp.dot(m_ref[...], wo_ref[:, cs], preferred_element_type=F32)
        xo_ref[:, cs] = xn_c
        for l in range(MXU_COLS // LANES):
            piece = xn_c[:, l * LANES:(l + 1) * LANES]
            sq = sq + piece * piece
    ms = jnp.sum(sq, axis=-1, keepdims=True) * (1.0 / D_MODEL)
    hf = xo_ref[...] * lax.rsqrt(ms + NORM_EPS) * nw_ref[...]
    _store_rows(hf_ref, hf)

    h_hi = hf.astype(BF16)
    h_lo = (hf - h_hi.astype(F32)).astype(BF16)
    wr = wr_ref[...]
    w_hi = wr.astype(BF16)
    w_lo = (wr - w_hi.astype(F32)).astype(BF16)
    lg = (jnp.dot(h_hi, w_hi, preferred_element_type=F32)
          + jnp.dot(h_hi, w_lo, preferred_element_type=F32)
          + jnp.dot(h_lo, w_hi, preferred_element_type=F32)) + br_ref[...]

    chunks = range(tm // ROUTER_ROWS)
    lgs = [lg[c * ROUTER_ROWS:(c + 1) * ROUTER_ROWS] for c in chunks]
    lane = lax.broadcasted_iota(jnp.int32, (ROUTER_ROWS, ROUTER_LANES), 1).astype(F32)
    neg = -jnp.inf
    big = float(ROUTER_LANES)
    is_group = lane < N_GROUPS

    def row_max(vals):
        return [jnp.max(v, axis=-1, keepdims=True) for v in vals]

    def first_index(vals, mx):
        return [jnp.min(jnp.where(v == m, lane, big), axis=-1, keepdims=True) for v, m in zip(vals, mx)]

    def row_sum(vals):
        return [jnp.sum(v, axis=-1, keepdims=True) for v in vals]

    gl = [jnp.where(is_group, v, neg) for v in lgs]
    gmax = row_max(gl)
    gsel = first_index(gl, gmax)
    gsum = row_sum([jnp.where(is_group, jnp.exp(v - m), 0.0) for v, m in zip(lgs, gmax)])
    lo_lane = [EXPERT_LANE0 + EXPERTS_PER_GROUP * g for g in gsel]
    el = [jnp.where((lane >= lo) & (lane < lo + EXPERTS_PER_GROUP), v, neg) for v, lo in zip(lgs, lo_lane)]
    v1 = row_max(el)
    i1 = first_index(el, v1)
    el2 = [jnp.where(lane == i, neg, v) for v, i in zip(el, i1)]
    v2 = row_max(el2)
    i2 = first_index(el2, v2)
    oh1 = [lane == i for i in i1]
    oh2 = [lane == i for i in i2]

    cnt = jnp.concatenate([a.astype(F32) + b.astype(F32) for a, b in zip(oh1, oh2)], axis=0)
    r_i = lax.broadcasted_iota(jnp.int32, (tm, tm), 0)
    c_i = lax.broadcasted_iota(jnp.int32, (tm, tm), 1)
    before = jnp.dot((r_i > c_i).astype(BF16), cnt.astype(BF16), preferred_element_type=F32)
    tot = before + run_scr[0:1, :]
    tots = [tot[c * ROUTER_ROWS:(c + 1) * ROUTER_ROWS] for c in chunks]
    rank1 = row_sum([jnp.where(o, t, 0.0) for o, t in zip(oh1, tots)])
    rank2 = row_sum([jnp.where(o, t, 0.0) for o, t in zip(oh2, tots)])
    run = run_scr[0:1, :] + jnp.sum(cnt, axis=0, keepdims=True)
    run_scr[...] = jnp.broadcast_to(run, run_scr.shape)
    cnt_ref[...] = jnp.broadcast_to(run, cnt_ref.shape)

    for c in chunks:
        gp = 1.0 / gsum[c]
        e21 = jnp.exp(v2[c] - v1[c])
        ew1 = gp / (1.0 + e21)
        ew2 = gp * e21 / (1.0 + e21)
        slab = jnp.where(lane == 0, i1[c] - EXPERT_LANE0,
               jnp.where(lane == 1, i2[c] - EXPERT_LANE0,
               jnp.where(lane == 2, ew1,
               jnp.where(lane == 3, ew2,
               jnp.where(lane == 4, rank1[c],
               jnp.where(lane == 5, rank2[c], 0.0))))))
        slab_ref[c * ROUTER_ROWS:(c + 1) * ROUTER_ROWS, :] = slab
        plan_ref[:, c * ROUTER_ROWS:(c + 1) * ROUTER_ROWS] = jnp.transpose(slab)[:SUBLANES, :]


def _outproj_router(merged, w_out, x, norm_w, w_router, b_router, tm=512):
    n = x.shape[0]
    row = lambda i: (i, 0)
    const = lambda i: (0, 0)
    return pl.pallas_call(
        _outproj_router_kernel,
        grid=(n // tm,),
        in_specs=[
            pl.BlockSpec((tm, D_MODEL), row),
            pl.BlockSpec((D_MODEL, D_MODEL), const),
            pl.BlockSpec((tm, D_MODEL), row),
            pl.BlockSpec((1, D_MODEL), const),
            pl.BlockSpec((D_MODEL, ROUTER_LANES), const),
            pl.BlockSpec((1, ROUTER_LANES), const),
        ],
        out_specs=[
            pl.BlockSpec((tm, D_MODEL), row),
            pl.BlockSpec((tm * ROW_SUB, LANES), row),
            pl.BlockSpec((tm, ROUTER_LANES), row),
            pl.BlockSpec((SUBLANES, tm), lambda i: (0, i)),
            pl.BlockSpec((SUBLANES, ROUTER_LANES), const),
        ],
        out_shape=[
            jax.ShapeDtypeStruct((n, D_MODEL), F32),
            jax.ShapeDtypeStruct((n * ROW_SUB, LANES), F32),
            jax.ShapeDtypeStruct((n, ROUTER_LANES), F32),
            jax.ShapeDtypeStruct((SUBLANES, n), F32),
            jax.ShapeDtypeStruct((SUBLANES, ROUTER_LANES), F32),
        ],
        scratch_shapes=[pltpu.VMEM((SUBLANES, ROUTER_LANES), F32)],
        compiler_params=_cparams("arbitrary"),
        name="outproj_router",
    )(merged, w_out, x, norm_w, w_router, b_router)


def _dispatch_kernel(slot0_ref, slot1_ref, hf_ref, xb_in_hbm, xb_hbm, sem, *, tm):
    del xb_in_hbm

    def copy(u, j):
        return pltpu.make_async_copy(_slab(hf_ref, u), _slab(xb_hbm, (slot0_ref, slot1_ref)[j][u]), sem)

    def start(u, c):
        for j in range(TOP_K):
            copy(u, j).start(priority=j)
        return c

    def wait(u, c):
        for j in range(TOP_K):
            copy(u, j).wait()
        return c

    lax.fori_loop(0, tm, start, 0, unroll=8)
    lax.fori_loop(0, tm, wait, 0, unroll=8)


def _dispatch(slots, hf, cap, backing, tm=256):
    n = hf.shape[0] // ROW_SUB
    assert backing.shape == (cap * ROW_SUB, LANES)
    return pl.pallas_call(
        functools.partial(_dispatch_kernel, tm=tm),
        grid=(n // tm,),
        in_specs=[
            pl.BlockSpec((tm,), lambda i: (i,), memory_space=pltpu.SMEM),
            pl.BlockSpec((tm,), lambda i: (i,), memory_space=pltpu.SMEM),
            pl.BlockSpec((tm * ROW_SUB, LANES), lambda i: (i, 0)),
            pl.BlockSpec(memory_space=pl.ANY),
        ],
        out_specs=pl.BlockSpec(memory_space=pl.ANY),
        out_shape=jax.ShapeDtypeStruct((cap * ROW_SUB, LANES), F32),
        scratch_shapes=[pltpu.SemaphoreType.DMA(())],
        input_output_aliases={3: 0},
        compiler_params=_cparams("arbitrary"),
        name="moe_dispatch",
    )(slots[0], slots[1], hf, backing)


def _ffn_kernel(be_ref, nu_ref, par_ref, nxt_ref, x_ref, wg_hbm, wu_hbm, wd_hbm, o_ref,
                wg_f, wu_f, wd_f, wg_s, wu_s, wd_s, sems, *, layer):
    b = pl.program_id(0)
    used = b < nu_ref[0]
    prev_e = be_ref[jnp.maximum(b - 1, 0)]
    new_expert = jnp.logical_or(b == 0, be_ref[b] != prev_e)

    def fetch(e, p):
        return [pltpu.make_async_copy(src.at[layer, e], dst.at[p], sems.at[p, i])
                for i, (src, dst) in enumerate(((wg_hbm, wg_f), (wu_hbm, wu_f), (wd_hbm, wd_f)))]

    @pl.when(jnp.logical_and(used, new_expert))
    def _():
        for p in range(2):
            @pl.when(par_ref[b] == p)
            def _():
                @pl.when(b == 0)
                def _():
                    for cp in fetch(be_ref[0], p):
                        cp.start()

                @pl.when(nxt_ref[b] >= 0)
                def _():
                    for cp in fetch(nxt_ref[b], 1 - p):
                        cp.start()

                for cp in fetch(be_ref[b], p):
                    cp.wait()
                wg_s[...] = wg_f[p].astype(BF16)
                wu_s[...] = wu_f[p].astype(BF16)
                wd_s[...] = wd_f[p].astype(BF16)

    @pl.when(used)
    def _():
        slabs = MXU_COLS // LANES
        gt = up = None
        for kc in range(D_MODEL // MXU_COLS):
            xk = jnp.concatenate([x_ref[pl.ds(kc * slabs + s, MOE_ROWS, stride=ROW_SUB), :] for s in range(slabs)],
                                 axis=1).astype(BF16)
            ks = slice(kc * MXU_COLS, (kc + 1) * MXU_COLS)
            g_k = jnp.dot(xk, wg_s[ks, :], preferred_element_type=F32)
            u_k = jnp.dot(xk, wu_s[ks, :], preferred_element_type=F32)
            gt = g_k if gt is None else gt + g_k
            up = u_k if up is None else up + u_k
        hid = (gt * jax.nn.sigmoid(gt) * up).astype(BF16)
        for c in range(D_MODEL // MXU_COLS):
            y_c = jnp.dot(hid, wd_s[:, c * MXU_COLS:(c + 1) * MXU_COLS], preferred_element_type=F32)
            for s in range(slabs):
                o_ref[pl.ds(c * slabs + s, MOE_ROWS, stride=ROW_SUB), :] = y_c[:, s * LANES:(s + 1) * LANES]

    @pl.when(jnp.logical_not(used))
    def _():
        o_ref[...] = jnp.zeros_like(o_ref)


def _expert_ffn(layer, plan, xb, w_gate, w_up, w_down):
    nblk = xb.shape[0] // (MOE_ROWS * ROW_SUB)
    hbm = pl.BlockSpec(memory_space=pl.ANY)
    grid_spec = pltpu.PrefetchScalarGridSpec(
        num_scalar_prefetch=4,
        grid=(nblk,),
        in_specs=[
            pl.BlockSpec((MOE_ROWS * ROW_SUB, LANES), lambda b, be, nu, par, nxt: (jnp.minimum(b, nu[0] - 1), 0)),
            hbm, hbm, hbm,
        ],
        out_specs=pl.BlockSpec((MOE_ROWS * ROW_SUB, LANES), lambda b, be, nu, par, nxt: (b, 0)),
        scratch_shapes=[
            pltpu.VMEM((2, D_MODEL, EXPERT_HIDDEN), F32),
            pltpu.VMEM((2, D_MODEL, EXPERT_HIDDEN), F32),
            pltpu.VMEM((2, EXPERT_HIDDEN, D_MODEL), F32),
            pltpu.VMEM((D_MODEL, EXPERT_HIDDEN), BF16),
            pltpu.VMEM((D_MODEL, EXPERT_HIDDEN), BF16),
            pltpu.VMEM((EXPERT_HIDDEN, D_MODEL), BF16),
            pltpu.SemaphoreType.DMA((2, 3)),
        ],
    )
    return pl.pallas_call(
        functools.partial(_ffn_kernel, layer=layer),
        grid_spec=grid_spec,
        out_shape=jax.ShapeDtypeStruct(xb.shape, F32),
        compiler_params=_cparams("arbitrary"),
        name="expert_ffn",
    )(plan.block_expert, plan.n_used, plan.parity, plan.next_expert, xb, w_gate, w_up, w_down)


def _combine_kernel(slot0_ref, slot1_ref, next0_ref, next1_ref, x_ref, slab_ref, fw_ref, yb_hbm, o_ref,
                    buf00, buf01, buf10, buf11, sems, *, tm, n_tiles, final_norm):
    i = pl.program_id(0)
    bufs = ((buf00, buf01), (buf10, buf11))
    cur = (slot0_ref, slot1_ref)

    def copies(slots, p, u):
        return [pltpu.make_async_copy(_slab(yb_hbm, slots[j][u]), _slab(bufs[p][j], u), sems.at[p])
                for j in range(TOP_K)]

    def start_all(slots, p):
        def body(u, c):
            for j, cp in enumerate(copies(slots, p, u)):
                cp.start(priority=j)
            return c
        lax.fori_loop(0, tm, body, 0, unroll=8)

    def wait_all(p):
        def body(u, c):
            for cp in copies(cur, p, u):
                cp.wait()
            return c
        lax.fori_loop(0, tm, body, 0, unroll=8)

    @pl.when(i == 0)
    def _():
        start_all(cur, 0)

    for p in range(2):
        @pl.when(i % 2 == p)
        def _():
            @pl.when(i + 1 < n_tiles)
            def _():
                start_all((next0_ref, next1_ref), 1 - p)

            wait_all(p)
            slab = slab_ref[...]
            w1 = slab[:, 2:3]
            w2 = slab[:, 3:4]
            out = x_ref[...] + (_rows_to_2d(bufs[p][0], tm) * w1 + _rows_to_2d(bufs[p][1], tm) * w2)
            if final_norm:
                out = _rms(out, fw_ref[...])
            o_ref[...] = out


def _combine(slots, x, slab, final_w, yb, final_norm, tm=256):
    n = x.shape[0]
    n_tiles = n // tm
    row = lambda i: (i, 0)
    row_buf = pltpu.VMEM((tm * ROW_SUB, LANES), F32)
    return pl.pallas_call(
        functools.partial(_combine_kernel, tm=tm, n_tiles=n_tiles, final_norm=final_norm),
        grid=(n_tiles,),
        in_specs=[
            pl.BlockSpec((tm,), lambda i: (i,), memory_space=pltpu.SMEM),
            pl.BlockSpec((tm,), lambda i: (i,), memory_space=pltpu.SMEM),
            pl.BlockSpec((tm,), lambda i: (jnp.minimum(i + 1, n_tiles - 1),), memory_space=pltpu.SMEM),
            pl.BlockSpec((tm,), lambda i: (jnp.minimum(i + 1, n_tiles - 1),), memory_space=pltpu.SMEM),
            pl.BlockSpec((tm, D_MODEL), row),
            pl.BlockSpec((tm, ROUTER_LANES), row),
            pl.BlockSpec((1, D_MODEL), lambda i: (0, 0)),
            pl.BlockSpec(memory_space=pl.ANY),
        ],
        out_specs=pl.BlockSpec((tm, D_MODEL), row),
        out_shape=jax.ShapeDtypeStruct((n, D_MODEL), F32),
        scratch_shapes=[row_buf, row_buf, row_buf, row_buf, pltpu.SemaphoreType.DMA((2,))],
        compiler_params=_cparams("arbitrary"),
        name="moe_combine",
    )(slots[0], slots[1], slots[0], slots[1], x, slab, final_w, yb)


class MoePlan(NamedTuple):
    slot: tuple
    block_expert: jax.Array
    n_used: jax.Array
    parity: jax.Array
    next_expert: jax.Array
    cap: int


def _moe_plan(plan_t, counts_row, n_tokens):
    eid = plan_t[0:TOP_K].astype(jnp.int32)
    rank = plan_t[4:4 + TOP_K].astype(jnp.int32)
    counts = counts_row[EXPERT_LANE0:EXPERT_LANE0 + N_EXPERTS].astype(jnp.int32)
    padded = (counts + MOE_ROWS - 1) // MOE_ROWS * MOE_ROWS
    pad_ends = jnp.cumsum(padded)
    pad_starts = pad_ends - padded
    slot = pad_starts[eid] + rank
    nblk = n_tokens * TOP_K // MOE_ROWS + N_EXPERTS
    block_start = jnp.arange(nblk, dtype=jnp.int32) * MOE_ROWS
    block_expert = jnp.minimum(
        jnp.sum((pad_ends[None, :] <= block_start[:, None]).astype(jnp.int32), axis=1),
        N_EXPERTS - 1).astype(jnp.int32)
    n_used = (pad_ends[-1:] // MOE_ROWS).astype(jnp.int32)
    blk = jnp.arange(nblk, dtype=jnp.int32)
    prev_expert = jnp.concatenate([block_expert[:1], block_expert[:-1]])
    first = (blk < n_used[0]) & ((blk == 0) | (block_expert != prev_expert))
    parity = ((jnp.cumsum(first.astype(jnp.int32)) - 1) % 2).astype(jnp.int32)
    later_first = first[None, :] & (blk[None, :] > blk[:, None])
    next_first = jnp.min(jnp.where(later_first, blk[None, :], nblk), axis=1)
    next_expert = jnp.where(next_first < nblk, block_expert[jnp.minimum(next_first, nblk - 1)], -1)
    return MoePlan(tuple(slot[j] for j in range(TOP_K)), block_expert, n_used, parity,
                   next_expert.astype(jnp.int32), nblk * MOE_ROWS)


def kernel(x, attn_norm_w, w_in, shift_mix, attn_sinks, rwkv_w0, rwkv_w_up, rwkv_a0, rwkv_a_up, rwkv_g_up, rwkv_k_k, rwkv_k_a, rwkv_r_k, rwkv_ln_w, rwkv_ln_b, vres_down, vres_mix, vres_up, vres_v0, w_branch_a, w_branch_b, w_out, ffn_norm_w, router_group_w, router_group_b, router_expert_w, router_expert_b, expert_w_gate, expert_w_up, expert_w_down, final_norm_w):
    batch, seq, _ = x.shape
    n = batch * seq
    depth = w_in.shape[0]
    xf = x.reshape(n, D_MODEL)
    w_in_bf = jnp.swapaxes(w_in, 1, 2).astype(BF16)

    zero_c = jnp.zeros((depth, RWKV_WIDTH), F32)
    v0_all = jnp.concatenate([zero_c[:1], vres_v0.astype(F32)], axis=0)
    pv_pre_all = jnp.stack([rwkv_w0, rwkv_a0, rwkv_k_k, rwkv_k_a, v0_all, zero_c, zero_c, zero_c], axis=1)
    pv_post_all = jnp.stack([rwkv_ln_w, rwkv_ln_b, rwkv_r_k.reshape(depth, RWKV_WIDTH),
                             zero_c, zero_c, zero_c, zero_c, zero_c], axis=1)
    mix_r_all = shift_mix[:, None, :RKV_COLS]
    mix_l_all = jnp.pad(shift_mix[:, None, RKV_COLS:], ((0, 0), (0, 0), (0, LORA_COLS - LORA_WIDTH)))
    wup_all = jnp.pad(rwkv_w_up.astype(BF16), ((0, 0), (0, LANES - DECAY_LORA), (0, 0)))
    aup_all = jnp.pad(rwkv_a_up.astype(BF16), ((0, 0), (DECAY_LORA, LANES - DECAY_LORA - AAA_LORA), (0, 0)))
    gup_all = jnp.pad(rwkv_g_up.astype(BF16), ((0, 0), (0, 2 * LANES - GATE_LORA), (0, 0)))
    vup_all = jnp.pad(vres_up.astype(BF16), ((1, 0), (0, LANES - MV_LORA), (0, 0)))
    vd_all = jnp.pad(vres_down.astype(BF16), ((0, 0), (0, 0), (0, LANES - MV_LORA)))
    mix_v_all = jnp.pad(vres_mix, ((0, 0), (0, LANES - MV_LORA)))[:, None, :]
    router_pad = ROUTER_LANES - N_GROUPS - N_EXPERTS
    w_router_all = jnp.pad(jnp.concatenate([router_group_w, router_expert_w], axis=2),
                           ((0, 0), (0, 0), (0, router_pad)))
    b_router_all = jnp.pad(jnp.concatenate([router_group_b, router_expert_b], axis=1),
                           ((0, 0), (0, router_pad)))[:, None, :]
    attn_norm_all = attn_norm_w[:, None, :]
    ffn_norm_all = ffn_norm_w[:, None, :]
    w_out_bf = w_out.astype(BF16)
    sinks_all = attn_sinks.astype(F32)

    v_first = None
    slot_backing = None
    slot_rows = (n * TOP_K // MOE_ROWS + N_EXPERTS) * MOE_ROWS
    for i in range(depth):
        has_vres = i > 0
        w_gates = w_in_bf[i, GATE_COL0:, :]
        mix_r, mix_l, pv_pre, pv_post = mix_r_all[i], mix_l_all[i], pv_pre_all[i], pv_post_all[i]
        wup, aup, gup, vup = wup_all[i], aup_all[i], gup_all[i], vup_all[i]
        vd, mix_v = (vd_all[i - 1], mix_v_all[i - 1]) if has_vres else (None, None)
        w_router, b_router = w_router_all[i], b_router_all[i]

        need_zeros = slot_backing is None
        proj = _inproj(i, xf, attn_norm_all[i], w_in_bf, w_gates, vd,
                       zero_rows=slot_rows * ROW_SUB if need_zeros else 0)
        qkv, rkv, lora, gates = proj[:4]
        zv = proj[4] if has_vres else None
        if need_zeros:
            slot_backing = proj[-1]
        attn = _attention(qkv, sinks_all[i], seq)
        y, v_new = _rwkv_mixer(rkv, lora, v_first, zv, mix_r, mix_l, mix_v, pv_pre,
                               wup, aup, gup, vup, pv_post, seq)
        if not has_vres:
            v_first = v_new
        merged = _merge(i, attn, y, w_branch_a, w_branch_b, gates)

        x_mid, hf, slab, plan_t, counts = _outproj_router(
            merged, w_out_bf[i], xf, ffn_norm_all[i], w_router, b_router)
        plan = _moe_plan(plan_t, counts[0], n)
        xb = _dispatch(plan.slot, hf, plan.cap, slot_backing)
        yb = _expert_ffn(i, plan, xb, expert_w_gate, expert_w_up, expert_w_down)
        slot_backing = yb
        xf = _combine(plan.slot, x_mid, slab, final_norm_w.reshape(1, D_MODEL), yb, final_norm=(i == depth - 1))
    return xf.reshape(batch, seq, D_MODEL)
```

```python
import functools
import math
from typing import NamedTuple

import jax
import jax.numpy as jnp
from jax import lax
from jax.experimental import pallas as pl
from jax.experimental.pallas import tpu as pltpu

F32 = jnp.float32
BF16 = jnp.bfloat16

D_MODEL = 2048
HEAD_DIM = 64
N_Q_HEADS = 16
N_KV_HEADS = 4
GQA_GROUP = N_Q_HEADS // N_KV_HEADS
ATTN_WIDTH = N_Q_HEADS * HEAD_DIM
KV_WIDTH = N_KV_HEADS * HEAD_DIM
WINDOW = 128
ATTN_BLOCK = 128
ATTN_SUB = 2

RWKV_HEAD = 64
N_RWKV_HEADS = 16
RWKV_WIDTH = N_RWKV_HEADS * RWKV_HEAD
DECAY_LORA = 64
AAA_LORA = 64
MV_LORA = 32
GATE_LORA = 160
RWKV_GN_EPS = 64e-5
DECAY_SCALE = math.exp(-0.5)
KK_NORM_FLOOR = 1e-12
LORA_WIDTH = DECAY_LORA + AAA_LORA + GATE_LORA

N_GROUPS = 4
EXPERTS_PER_GROUP = 8
N_EXPERTS = N_GROUPS * EXPERTS_PER_GROUP
TOP_K = 2
EXPERT_HIDDEN = D_MODEL // 4
NORM_EPS = 1e-5

LANES = 128
SUBLANES = 8
MXU_COLS = 256

SEG = 512
QKV_COLS = ATTN_WIDTH + 2 * KV_WIDTH
RKV_COLS = 3 * RWKV_WIDTH
LORA_COLS = SEG
GATE_COLS = 2 * D_MODEL
QKV_TILES = QKV_COLS // SEG
RKV_TILES = RKV_COLS // SEG
LORA_TILES = LORA_COLS // SEG
GATE_TILES = GATE_COLS // SEG
IN_TILES = QKV_TILES + RKV_TILES + LORA_TILES + GATE_TILES
FIRST_GATE_TILE = QKV_TILES + RKV_TILES + LORA_TILES
ZERO_STEPS = 16
GATE_COL0 = QKV_COLS + RKV_COLS + LORA_WIDTH

SCAN_CHUNK = 64
SCAN_SUB = 2
SCAN_GROUP = 4
MOE_ROWS = 256
ROUTER_LANES = LANES
ROUTER_ROWS = 128
EXPERT_LANE0 = N_GROUPS

ROW_SUB = D_MODEL // LANES

VMEM_LIMIT = 56 * 1024 * 1024


def _rows_to_2d(slab_ref, rows):
    return jnp.concatenate([slab_ref[pl.ds(s, rows, stride=ROW_SUB), :] for s in range(ROW_SUB)], axis=1)


def _store_rows(slab_ref, val):
    rows = val.shape[0]
    for s in range(ROW_SUB):
        slab_ref[pl.ds(s, rows, stride=ROW_SUB), :] = val[:, s * LANES:(s + 1) * LANES]


def _slab(ref, row):
    return ref.at[pl.ds(pl.multiple_of(row * ROW_SUB, ROW_SUB), ROW_SUB)]


def _cparams(*sem):
    return pltpu.CompilerParams(dimension_semantics=sem, vmem_limit_bytes=VMEM_LIMIT)


def _bdot(a, b):
    return jnp.dot(a.astype(BF16), b.astype(BF16), preferred_element_type=F32)


def _bdot_nt(a, b):
    return lax.dot_general(a.astype(BF16), b.astype(BF16), (((1,), (1,)), ((), ())),
                           preferred_element_type=F32)


def _bdot_tn(a, b):
    return lax.dot_general(a.astype(BF16), b.astype(BF16), (((0,), (0,)), ((), ())),
                           preferred_element_type=F32)


def _dot_split_rhs(w01, x, passes):
    acc, rem = None, x
    for p in range(passes):
        part = rem.astype(BF16)
        d = jnp.dot(w01, part, preferred_element_type=F32)
        acc = d if acc is None else acc + d
        if p + 1 < passes:
            rem = rem - part.astype(F32)
    return acc


def _dot_split_lhs(x, w01, passes):
    acc, rem = None, x
    for p in range(passes):
        part = rem.astype(BF16)
        d = jnp.dot(part, w01, preferred_element_type=F32)
        acc = d if acc is None else acc + d
        if p + 1 < passes:
            rem = rem - part.astype(F32)
    return acc


def _rms(x, w):
    ms = jnp.mean(x * x, axis=-1, keepdims=True)
    return x * lax.rsqrt(ms + NORM_EPS) * w


def _inproj_kernel(*refs, has_vres, has_zeros):
    refs = list(refs)
    x_ref, nw_ref, w_ref, wg_ref = refs[:4]
    refs = refs[4:]
    vd_ref = refs.pop(0) if has_vres else None
    qkv_ref, rkv_ref, lora_ref, gate_ref = refs[:4]
    refs = refs[4:]
    zv_ref = refs.pop(0) if has_vres else None
    zeros_ref = refs.pop(0) if has_zeros else None
    (h_scr,) = refs
    j = pl.program_id(1)

    if has_zeros:
        zeros_ref[...] = jnp.zeros_like(zeros_ref)

    @pl.when(j == 0)
    def _():
        h_scr[...] = _rms(x_ref[...], nw_ref[...]).astype(BF16)
        if has_vres:
            zv_ref[...] = jnp.dot(h_scr[...], vd_ref[...], preferred_element_type=F32)

    def project(wt_ref, out_ref):
        for c in range(SEG // MXU_COLS):
            cs = slice(c * MXU_COLS, (c + 1) * MXU_COLS)
            acc = lax.dot_general(h_scr[...], wt_ref[cs, :], (((1,), (1,)), ((), ())),
                                  preferred_element_type=F32)
            out_ref[:, cs] = acc.astype(out_ref.dtype)

    @pl.when(j < QKV_TILES)
    def _():
        project(w_ref, qkv_ref)

    @pl.when((j >= QKV_TILES) & (j < QKV_TILES + RKV_TILES))
    def _():
        project(w_ref, rkv_ref)

    @pl.when(j == QKV_TILES + RKV_TILES)
    def _():
        project(w_ref, lora_ref)

    @pl.when(j >= FIRST_GATE_TILE)
    def _():
        project(wg_ref, gate_ref)


def _inproj(layer, x, norm_w, w_in, w_gates, vd, zero_rows=0, tm=1024):
    n = x.shape[0]
    has_vres = vd is not None
    s_rkv = QKV_TILES
    in_specs = [
        pl.BlockSpec((tm, D_MODEL), lambda i, j: (i, 0)),
        pl.BlockSpec((1, D_MODEL), lambda i, j: (0, 0)),
        pl.BlockSpec((None, SEG, D_MODEL), lambda i, j: (layer, jnp.minimum(j, FIRST_GATE_TILE - 1), 0)),
        pl.BlockSpec((SEG, D_MODEL), lambda i, j: (jnp.clip(j - FIRST_GATE_TILE, 0, GATE_TILES - 1), 0)),
    ]
    args = [x, norm_w, w_in, w_gates]
    out_specs = [
        pl.BlockSpec((tm, SEG), lambda i, j: (i, jnp.clip(j, 0, QKV_TILES - 1))),
        pl.BlockSpec((tm, SEG), lambda i, j: (i, jnp.clip(j - s_rkv, 0, RKV_TILES - 1))),
        pl.BlockSpec((tm, SEG), lambda i, j: (i, 0)),
        pl.BlockSpec((tm, SEG), lambda i, j: (i, jnp.clip(j - FIRST_GATE_TILE, 0, GATE_TILES - 1))),
    ]
    out_shape = [
        jax.ShapeDtypeStruct((n, QKV_COLS), BF16),
        jax.ShapeDtypeStruct((n, RKV_COLS), F32),
        jax.ShapeDtypeStruct((n, LORA_COLS), F32),
        jax.ShapeDtypeStruct((n, GATE_COLS), BF16),
    ]
    if has_vres:
        in_specs.append(pl.BlockSpec((D_MODEL, LANES), lambda i, j: (0, 0)))
        args.append(vd)
        out_specs.append(pl.BlockSpec((tm, LANES), lambda i, j: (i, 0)))
        out_shape.append(jax.ShapeDtypeStruct((n, LANES), F32))
    if zero_rows:
        blk = zero_rows // ((n // tm) * ZERO_STEPS)
        assert blk * (n // tm) * ZERO_STEPS == zero_rows and blk % SUBLANES == 0
        out_specs.append(pl.BlockSpec((blk, LANES), lambda i, j: (i * ZERO_STEPS + jnp.minimum(j, ZERO_STEPS - 1), 0)))
        out_shape.append(jax.ShapeDtypeStruct((zero_rows, LANES), F32))
    return pl.pallas_call(
        functools.partial(_inproj_kernel, has_vres=has_vres, has_zeros=bool(zero_rows)),
        grid=(n // tm, IN_TILES),
        in_specs=in_specs,
        out_specs=out_specs,
        out_shape=out_shape,
        scratch_shapes=[pltpu.VMEM((tm, D_MODEL), BF16)],
        compiler_params=_cparams("parallel", "arbitrary"),
        name="inproj",
    )(*args)


def _attn_kernel(sink_ref, q_ref, kp_ref, kc_ref, vp_ref, vc_ref, bias_ref, o_ref, *, tiles_per_seq):
    first = (pl.program_id(0) % tiles_per_seq) == 0
    col = lax.broadcasted_iota(jnp.int32, (ATTN_BLOCK, 2 * ATTN_BLOCK), 1)
    pad_keys = jnp.logical_and(first, col < ATTN_BLOCK)
    scale = HEAD_DIM ** -0.5
    items = [(sb, hk) for sb in range(ATTN_SUB) for hk in range(N_KV_HEADS)]

    def window(prev_ref, cur_ref, sb, hk):
        ks = slice(hk * HEAD_DIM, (hk + 1) * HEAD_DIM)
        rows = slice(sb * ATTN_BLOCK, (sb + 1) * ATTN_BLOCK)
        prev = prev_ref[:, ks] if sb == 0 else cur_ref[(sb - 1) * ATTN_BLOCK:sb * ATTN_BLOCK, ks]
        return jnp.concatenate([prev, cur_ref[rows, ks]], axis=0)

    scores = {}
    for sb, hk in items:
        kw = window(kp_ref, kc_ref, sb, hk)
        for g in range(GQA_GROUP):
            h = hk * GQA_GROUP + g
            qh = q_ref[sb * ATTN_BLOCK:(sb + 1) * ATTN_BLOCK, h * HEAD_DIM:(h + 1) * HEAD_DIM]
            s = lax.dot_general(qh, kw, (((1,), (1,)), ((), ())), preferred_element_type=F32)
            s = s * scale + bias_ref[h]
            if sb == 0:
                s = jnp.where(pad_keys, -jnp.inf, s)
            scores[(sb, h)] = s
    row_max = {key: jnp.max(s, axis=-1, keepdims=True) for key, s in scores.items()}
    m = {key: jnp.maximum(row_max[key], sink_ref[key[1]]) for key in scores}
    p = {key: jnp.exp(scores[key] - m[key]) for key in scores}
    row_sum = {key: jnp.sum(p[key], axis=-1, keepdims=True) for key in scores}
    denoms = {key: row_sum[key] + jnp.exp(sink_ref[key[1]] - m[key]) for key in scores}
    probs = {key: p[key].astype(BF16) for key in scores}
    for sb, hk in items:
        vw = window(vp_ref, vc_ref, sb, hk)
        for g in range(GQA_GROUP):
            h = hk * GQA_GROUP + g
            o = jnp.dot(probs[(sb, h)], vw, preferred_element_type=F32) / denoms[(sb, h)]
            o_ref[sb * ATTN_BLOCK:(sb + 1) * ATTN_BLOCK, h * HEAD_DIM:(h + 1) * HEAD_DIM] = o.astype(BF16)


def _attn_bias():
    qi = jnp.arange(ATTN_BLOCK)[:, None]
    kj = jnp.arange(2 * ATTN_BLOCK)[None, :]
    dist = qi + ATTN_BLOCK - kj
    valid = (dist >= 0) & (dist < WINDOW)
    slopes = jnp.exp2(-8.0 * jnp.arange(1, N_Q_HEADS + 1, dtype=F32) / N_Q_HEADS)
    bias = -slopes[:, None, None] * dist.astype(F32)[None]
    return jnp.where(valid[None], bias, -jnp.inf)


def _attention(qkv, sinks, seq):
    n = qkv.shape[0]
    rows = ATTN_SUB * ATTN_BLOCK
    tiles_per_seq = seq // rows
    kcol = ATTN_WIDTH // KV_WIDTH
    prev = lambda i: jnp.maximum(i * ATTN_SUB - 1, 0)
    return pl.pallas_call(
        functools.partial(_attn_kernel, tiles_per_seq=tiles_per_seq),
        grid=(n // rows,),
        in_specs=[
            pl.BlockSpec(memory_space=pltpu.SMEM),
            pl.BlockSpec((rows, ATTN_WIDTH), lambda i: (i, 0)),
            pl.BlockSpec((ATTN_BLOCK, KV_WIDTH), lambda i: (prev(i), kcol)),
            pl.BlockSpec((rows, KV_WIDTH), lambda i: (i, kcol)),
            pl.BlockSpec((ATTN_BLOCK, KV_WIDTH), lambda i: (prev(i), kcol + 1)),
            pl.BlockSpec((rows, KV_WIDTH), lambda i: (i, kcol + 1)),
            pl.BlockSpec((N_Q_HEADS, ATTN_BLOCK, 2 * ATTN_BLOCK), lambda i: (0, 0, 0)),
        ],
        out_specs=pl.BlockSpec((rows, ATTN_WIDTH), lambda i: (i, 0)),
        out_shape=jax.ShapeDtypeStruct((n, ATTN_WIDTH), BF16),
        compiler_params=_cparams("parallel"),
        name="swa_attention",
    )(sinks, qkv, qkv, qkv, qkv, qkv, _attn_bias())


def _head_ones():
    ch = jnp.arange(MXU_COLS) // RWKV_HEAD
    return (ch[:, None] == ch[None, :]).astype(BF16)


def _head_sums(x, ones, passes):
    parts = [_dot_split_lhs(x[:, c:c + MXU_COLS], ones, passes) for c in range(0, RWKV_WIDTH, MXU_COLS)]
    return jnp.concatenate(parts, axis=1)


def _rwkv_kernel(*refs, has_vres):
    refs = list(refs)
    rkv_ref, lo_ref = refs[:2]
    refs = refs[2:]
    if has_vres:
        vf_ref, zv_ref, mixv_ref = refs[:3]
        refs = refs[3:]
    mixr_ref, mixl_ref, pv_ref, wup_ref, aup_ref, gup_ref, vup_ref, ones_ref, pp_ref = refs[:9]
    refs = refs[9:]
    y_ref = refs.pop(0)
    v_out = None if has_vres else refs.pop(0)
    s_scr, prev_rkv, prev_lo = refs[:3]
    prev_zv = refs[3] if has_vres else None

    L = SCAN_CHUNK
    W = SCAN_GROUP * RWKV_HEAD
    n_seq, step_rows = rkv_ref.shape[0], rkv_ref.shape[1]
    n_grp = N_RWKV_HEADS // SCAN_GROUP
    ones = ones_ref[...]

    @pl.when(pl.program_id(0) == 0)
    def _():
        s_scr[...] = jnp.zeros_like(s_scr)
        prev_rkv[...] = jnp.zeros_like(prev_rkv)
        prev_lo[...] = jnp.zeros_like(prev_lo)
        if has_vres:
            prev_zv[...] = jnp.zeros_like(prev_zv)

    def lerp_shift(z_ref, carry_ref, b, mix):
        z = z_ref[b]
        row = lax.broadcasted_iota(jnp.int32, z.shape, 0)
        back = jnp.where(row == 0, carry_ref[b, SUBLANES - 1:SUBLANES, :], pltpu.roll(z, 1, 0))
        carry_ref[b] = z_ref[b, step_rows - SUBLANES:step_rows, :]
        return z + (back - z) * mix

    w0, a0, k_k, k_a, v0 = (pv_ref[i:i + 1, :] for i in range(5))
    pre = []
    for b in range(n_seq):
        zs = lerp_shift(rkv_ref, prev_rkv, b, mixr_ref[...])
        los = lerp_shift(lo_ref, prev_lo, b, mixl_ref[...])
        r = zs[:, :RWKV_WIDTH]
        kr = zs[:, RWKV_WIDTH:2 * RWKV_WIDTH]
        vr = zs[:, 2 * RWKV_WIDTH:]
        wa = los[:, :LANES]
        col = lax.broadcasted_iota(jnp.int32, wa.shape, 1)
        wa_in = jnp.where(col < DECAY_LORA, jnp.tanh(wa), wa).astype(BF16)
        gd = los[:, LANES:3 * LANES]
        col_g = lax.broadcasted_iota(jnp.int32, gd.shape, 1)
        g_in = jnp.where(col_g < GATE_LORA, jax.nn.sigmoid(gd), gd).astype(BF16)
        dw = jnp.dot(wa_in, wup_ref[...], preferred_element_type=F32)
        da = jnp.dot(wa_in, aup_ref[...], preferred_element_type=F32)
        g = jnp.dot(g_in, gup_ref[...], preferred_element_type=F32)
        lw = -DECAY_SCALE * jax.nn.sigmoid(w0 + dw)
        a = jax.nn.sigmoid(a0 + da)
        if has_vres:
            zvs = lerp_shift(zv_ref, prev_zv, b, mixv_ref[...])
            dv = jnp.dot(zvs.astype(BF16), vup_ref[...], preferred_element_type=F32)
            vr = vr + (vf_ref[b] - vr) * jax.nn.sigmoid(v0 + dv)
        else:
            v_out[b] = vr
        kk0 = kr * k_k
        ss = _head_sums(kk0 * kk0, ones, 2)
        kk = kk0 * lax.rsqrt(jnp.maximum(ss, KK_NORM_FLOOR ** 2))
        pre.append(dict(r=r, lw=lw, k=kr * (1.0 + (a - 1.0) * k_a), v=vr, al=-kk, be=kk * a, g=g))

    row = lax.broadcasted_iota(jnp.int32, (L, L), 0)
    col = lax.broadcasted_iota(jnp.int32, (L, L), 1)
    tri = (row >= col).astype(BF16)
    t_w = lax.broadcasted_iota(jnp.int32, (L, W), 0)
    lane_w = lax.broadcasted_iota(jnp.int32, (L, W), 1)
    s_w = lane_w % RWKV_HEAD
    strict = t_w > s_w
    incl = t_w >= s_w
    eye = (t_w == s_w).astype(F32)
    blk_w = lane_w // RWKV_HEAD
    bd_mask = (lax.broadcasted_iota(jnp.int32, (W, W), 0) // RWKV_HEAD
               == lax.broadcasted_iota(jnp.int32, (W, W), 1) // RWKV_HEAD)

    def blockdiag(w):
        tiled = jnp.concatenate([w.astype(BF16)] * SCAN_GROUP, axis=0)
        return jnp.where(bd_mask, tiled, jnp.zeros_like(tiled))

    ln_w, ln_b, r_k = (pp_ref[i:i + 1, :] for i in range(3))
    items = [(b, g) for b in range(n_seq) for g in range(n_grp)]
    for sub in range(step_rows // L):
        rows = slice(sub * L, (sub + 1) * L)
        ops = {}
        for b in range(n_seq):
            p = pre[b]
            lw = p['lw'][rows]
            cum = _dot_split_rhs(tri, lw, 2)
            p_inc = jnp.exp(cum)
            p_exc = jnp.exp(cum - lw)
            p_inv = jnp.exp(-cum)
            p_last = p_inc[L - 1:L, :]
            a_t = p['al'][rows] * p_exc
            r_t = p['r'][rows] * p_inc
            b_t = p['be'][rows] * p_inv
            k_t = p['k'][rows] * p_inv
            b_end = b_t * p_last
            k_end = k_t * p_last
            v = p['v'][rows]
            for g in range(n_grp):
                gs = slice(g * W, (g + 1) * W)
                ops[(b, g)] = dict(
                    ar=jnp.concatenate([a_t[:, gs], r_t[:, gs]], axis=0),
                    bt=b_t[:, gs], kt=k_t[:, gs], v=v[:, gs], p_last=p_last[:, gs],
                    bke=jnp.concatenate([b_end[:, gs], k_end[:, gs]], axis=0))

        gm, xs, av, s0 = {}, {}, {}, {}
        for it in items:
            o = ops[it]
            rhs = jnp.concatenate([blockdiag(o['bt']), blockdiag(o['kt'])], axis=0)
            gm[it] = _bdot_nt(o['ar'], rhs)
            s0[it] = s_scr[it[0], it[1]]
            xs[it] = _bdot_nt(o['ar'], blockdiag(s0[it]))
        a_ab, a_rb = {}, {}
        for it in items:
            g = gm[it]
            a_ab[it] = jnp.where(strict, g[:L, :W], 0.0)
            a_rb[it] = jnp.where(incl, g[L:, :W], 0.0)
            a_kk = jnp.concatenate([jnp.where(strict, g[:L, W:], 0.0), jnp.where(incl, g[L:, W:], 0.0)], axis=0)
            av[it] = _bdot(a_kk, blockdiag(ops[it]['v']))

        pw = {it: _bdot(a_ab[it], blockdiag(a_ab[it])) for it in items}
        tinv = {it: eye + a_ab[it] for it in items}
        span = 2
        while 2 * span < L:
            both = {it: _bdot(jnp.concatenate([pw[it], tinv[it]], axis=0), blockdiag(pw[it])) for it in items}
            tinv = {it: tinv[it] + both[it][L:] for it in items}
            pw = {it: both[it][:L] for it in items}
            span *= 2
        tinv = {it: tinv[it] + _bdot(tinv[it], blockdiag(pw[it])) for it in items}

        u = {it: _bdot(tinv[it], blockdiag(xs[it][:L] + av[it][:L])) for it in items}
        y = {it: xs[it][L:] + av[it][L:] + _bdot(a_rb[it], blockdiag(u[it])) for it in items}
        for it in items:
            o = ops[it]
            full = _bdot_tn(jnp.concatenate([u[it], o['v']], axis=0), o['bke'])
            upd = full[:RWKV_HEAD]
            for h in range(1, SCAN_GROUP):
                upd = jnp.where(blk_w == h, full[h * RWKV_HEAD:(h + 1) * RWKV_HEAD], upd)
            s_scr[it[0], it[1]] = s0[it] * o['p_last'] + upd

        inv_n = 1.0 / RWKV_HEAD
        for b in range(n_seq):
            p = pre[b]
            yb = jnp.concatenate([y[(b, g)] for g in range(n_grp)], axis=1)
            d = yb - _head_sums(yb, ones, 2) * inv_n
            var = _head_sums(d * d, ones, 2) * inv_n
            yn = d * lax.rsqrt(var + RWKV_GN_EPS) * ln_w + ln_b
            bonus = _head_sums(p['r'][rows] * p['k'][rows] * r_k, ones, 2) * p['v'][rows]
            y_ref[b, rows, :] = ((yn + bonus) * p['g'][rows]).astype(BF16)


def _rwkv_mixer(rkv, lora, v_first, zv, mix_r, mix_l, mix_v, pv_pre, wup, aup, gup, vup, pv_post, seq):
    n = rkv.shape[0]
    n_seq = n // seq
    has_vres = v_first is not None
    step_rows = SCAN_SUB * SCAN_CHUNK
    timed = lambda cols: pl.BlockSpec((n_seq, step_rows, cols), lambda t: (0, t, 0))
    const = lambda shape: pl.BlockSpec(shape, lambda t: (0, 0))
    by_seq = lambda a: a.reshape(n_seq, seq, a.shape[-1])
    in_specs = [timed(RKV_COLS), timed(LORA_COLS)]
    args = [by_seq(rkv), by_seq(lora)]
    if has_vres:
        in_specs += [timed(RWKV_WIDTH), timed(LANES), const((1, LANES))]
        args += [by_seq(v_first), by_seq(zv), mix_v]
    in_specs += [
        const((1, RKV_COLS)), const((1, LORA_COLS)), const((SUBLANES, RWKV_WIDTH)),
        const((LANES, RWKV_WIDTH)), const((LANES, RWKV_WIDTH)), const((2 * LANES, RWKV_WIDTH)),
        const((LANES, RWKV_WIDTH)), const((MXU_COLS, MXU_COLS)), const((SUBLANES, RWKV_WIDTH)),
    ]
    args += [mix_r, mix_l, pv_pre, wup, aup, gup, vup, _head_ones(), pv_post]
    out_specs = [timed(RWKV_WIDTH)]
    out_shape = [jax.ShapeDtypeStruct((n_seq, seq, RWKV_WIDTH), BF16)]
    if not has_vres:
        out_specs.append(timed(RWKV_WIDTH))
        out_shape.append(jax.ShapeDtypeStruct((n_seq, seq, RWKV_WIDTH), F32))
    scratch = [
        pltpu.VMEM((n_seq, N_RWKV_HEADS // SCAN_GROUP, RWKV_HEAD, SCAN_GROUP * RWKV_HEAD), F32),
        pltpu.VMEM((n_seq, SUBLANES, RKV_COLS), F32),
        pltpu.VMEM((n_seq, SUBLANES, LORA_COLS), F32),
    ]
    if has_vres:
        scratch.append(pltpu.VMEM((n_seq, SUBLANES, LANES), F32))
    outs = pl.pallas_call(
        functools.partial(_rwkv_kernel, has_vres=has_vres),
        grid=(seq // step_rows,),
        in_specs=in_specs,
        out_specs=out_specs,
        out_shape=out_shape,
        scratch_shapes=scratch,
        compiler_params=_cparams("arbitrary"),
        name="rwkv_mixer",
    )(*args)
    y = outs[0].reshape(n, RWKV_WIDTH)
    return y, (None if has_vres else outs[1].reshape(n, RWKV_WIDTH))


def _merge_kernel(a_ref, y_ref, wa_ref, wb_ref, ga_ref, gb_ref, o_ref):
    for c in range(o_ref.shape[1] // MXU_COLS):
        cs = slice(c * MXU_COLS, (c + 1) * MXU_COLS)
        pa = jnp.dot(a_ref[...], wa_ref[:, cs].astype(BF16), preferred_element_type=F32)
        pb = jnp.dot(y_ref[...], wb_ref[:, cs].astype(BF16), preferred_element_type=F32)
        ga = jax.nn.sigmoid(ga_ref[:, cs].astype(F32))
        gb = jax.nn.sigmoid(gb_ref[:, cs].astype(F32))
        o_ref[:, cs] = (ga * pa + gb * pb).astype(BF16)


def _merge(layer, attn, y, wa, wb, gates, tm=1024, tn=1024):
    n = attn.shape[0]
    nj = D_MODEL // tn
    return pl.pallas_call(
        _merge_kernel,
        grid=(n // tm, nj),
        in_specs=[
            pl.BlockSpec((tm, ATTN_WIDTH), lambda i, j: (i, 0)),
            pl.BlockSpec((tm, RWKV_WIDTH), lambda i, j: (i, 0)),
            pl.BlockSpec((None, ATTN_WIDTH, tn), lambda i, j: (layer, 0, j)),
            pl.BlockSpec((None, RWKV_WIDTH, tn), lambda i, j: (layer, 0, j)),
            pl.BlockSpec((tm, tn), lambda i, j: (i, j)),
            pl.BlockSpec((tm, tn), lambda i, j: (i, j + nj)),
        ],
        out_specs=pl.BlockSpec((tm, tn), lambda i, j: (i, j)),
        out_shape=jax.ShapeDtypeStruct((n, D_MODEL), BF16),
        compiler_params=_cparams("parallel", "arbitrary"),
        name="gated_merge",
    )(attn, y, wa, wb, gates, gates)


def _outproj_router_kernel(m_ref, wo_ref, x_ref, nw_ref, wr_ref, br_ref,
                           xo_ref, hf_ref, slab_ref, plan_ref, cnt_ref, run_scr):
    @pl.when(pl.program_id(0) == 0)
    def _():
        run_scr[...] = jnp.zeros_like(run_scr)

    tm = x_ref.shape[0]
    sq = jnp.zeros((tm, LANES), F32)
    for c in range(D_MODEL // MXU_COLS):
        cs = slice(c * MXU_COLS, (c + 1) * MXU_COLS)
        xn_c = x_ref[:, cs] + jnp.dot(m_ref[...], wo_ref[:, cs], preferred_element_type=F32)
        xo_ref[:, cs] = xn_c
        for l in range(MXU_COLS // LANES):
            piece = xn_c[:, l * LANES:(l + 1) * LANES]
            sq = sq + piece * piece
    ms = jnp.sum(sq, axis=-1, keepdims=True) * (1.0 / D_MODEL)
    hf = xo_ref[...] * lax.rsqrt(ms + NORM_EPS) * nw_ref[...]
    _store_rows(hf_ref, hf)

    h_hi = hf.astype(BF16)
    h_lo = (hf - h_hi.astype(F32)).astype(BF16)
    wr = wr_ref[...]
    w_hi = wr.astype(BF16)
    w_lo = (wr - w_hi.astype(F32)).astype(BF16)
    lg = (jnp.dot(h_hi, w_hi, preferred_element_type=F32)
          + jnp.dot(h_hi, w_lo, preferred_element_type=F32)
          + jnp.dot(h_lo, w_hi, preferred_element_type=F32)) + br_ref[...]

    chunks = range(tm // ROUTER_ROWS)
    lgs = [lg[c * ROUTER_ROWS:(c + 1) * ROUTER_ROWS] for c in chunks]
    lane = lax.broadcasted_iota(jnp.int32, (ROUTER_ROWS, ROUTER_LANES), 1).astype(F32)
    neg = -jnp.inf
    big = float(ROUTER_LANES)
    is_group = lane < N_GROUPS

    def row_max(vals):
        return [jnp.max(v, axis=-1, keepdims=True) for v in vals]

    def first_index(vals, mx):
        return [jnp.min(jnp.where(v == m, lane, big), axis=-1, keepdims=True) for v, m in zip(vals, mx)]

    def row_sum(vals):
        return [jnp.sum(v, axis=-1, keepdims=True) for v in vals]

    gl = [jnp.where(is_group, v, neg) for v in lgs]
    gmax = row_max(gl)
    gsel = first_index(gl, gmax)
    gsum = row_sum([jnp.where(is_group, jnp.exp(v - m), 0.0) for v, m in zip(lgs, gmax)])
    lo_lane = [EXPERT_LANE0 + EXPERTS_PER_GROUP * g for g in gsel]
    el = [jnp.where((lane >= lo) & (lane < lo + EXPERTS_PER_GROUP), v, neg) for v, lo in zip(lgs, lo_lane)]
    v1 = row_max(el)
    i1 = first_index(el, v1)
    el2 = [jnp.where(lane == i, neg, v) for v, i in zip(el, i1)]
    v2 = row_max(el2)
    i2 = first_index(el2, v2)
    oh1 = [lane == i for i in i1]
    oh2 = [lane == i for i in i2]

    cnt = jnp.concatenate([a.astype(F32) + b.astype(F32) for a, b in zip(oh1, oh2)], axis=0)
    r_i = lax.broadcasted_iota(jnp.int32, (tm, tm), 0)
    c_i = lax.broadcasted_iota(jnp.int32, (tm, tm), 1)
    before = jnp.dot((r_i > c_i).astype(BF16), cnt.astype(BF16), preferred_element_type=F32)
    tot = before + run_scr[0:1, :]
    tots = [tot[c * ROUTER_ROWS:(c + 1) * ROUTER_ROWS] for c in chunks]
    rank1 = row_sum([jnp.where(o, t, 0.0) for o, t in zip(oh1, tots)])
    rank2 = row_sum([jnp.where(o, t, 0.0) for o, t in zip(oh2, tots)])
    run = run_scr[0:1, :] + jnp.sum(cnt, axis=0, keepdims=True)
    run_scr[...] = jnp.broadcast_to(run, run_scr.shape)
    cnt_ref[...] = jnp.broadcast_to(run, cnt_ref.shape)

    for c in chunks:
        gp = 1.0 / gsum[c]
        e21 = jnp.exp(v2[c] - v1[c])
        ew1 = gp / (1.0 + e21)
        ew2 = gp * e21 / (1.0 + e21)
        slab = jnp.where(lane == 0, i1[c] - EXPERT_LANE0,
               jnp.where(lane == 1, i2[c] - EXPERT_LANE0,
               jnp.where(lane == 2, ew1,
               jnp.where(lane == 3, ew2,
               jnp.where(lane == 4, rank1[c],
               jnp.where(lane == 5, rank2[c], 0.0))))))
        slab_ref[c * ROUTER_ROWS:(c + 1) * ROUTER_ROWS, :] = slab
        plan_ref[:, c * ROUTER_ROWS:(c + 1) * ROUTER_ROWS] = jnp.transpose(slab)[:SUBLANES, :]


def _outproj_router(merged, w_out, x, norm_w, w_router, b_router, tm=512):
    n = x.shape[0]
    row = lambda i: (i, 0)
    const = lambda i: (0, 0)
    return pl.pallas_call(
        _outproj_router_kernel,
        grid=(n // tm,),
        in_specs=[
            pl.BlockSpec((tm, D_MODEL), row),
            pl.BlockSpec((D_MODEL, D_MODEL), const),
            pl.BlockSpec((tm, D_MODEL), row),
            pl.BlockSpec((1, D_MODEL), const),
            pl.BlockSpec((D_MODEL, ROUTER_LANES), const),
            pl.BlockSpec((1, ROUTER_LANES), const),
        ],
        out_specs=[
            pl.BlockSpec((tm, D_MODEL), row),
            pl.BlockSpec((tm * ROW_SUB, LANES), row),
            pl.BlockSpec((tm, ROUTER_LANES), row),
            pl.BlockSpec((SUBLANES, tm), lambda i: (0, i)),
            pl.BlockSpec((SUBLANES, ROUTER_LANES), const),
        ],
        out_shape=[
            jax.ShapeDtypeStruct((n, D_MODEL), F32),
            jax.ShapeDtypeStruct((n * ROW_SUB, LANES), F32),
            jax.ShapeDtypeStruct((n, ROUTER_LANES), F32),
            jax.ShapeDtypeStruct((SUBLANES, n), F32),
            jax.ShapeDtypeStruct((SUBLANES, ROUTER_LANES), F32),
        ],
        scratch_shapes=[pltpu.VMEM((SUBLANES, ROUTER_LANES), F32)],
        compiler_params=_cparams("arbitrary"),
        name="outproj_router",
    )(merged, w_out, x, norm_w, w_router, b_router)


def _dispatch_kernel(slot0_ref, slot1_ref, hf_ref, xb_in_hbm, xb_hbm, sem, *, tm):
    del xb_in_hbm

    def copy(u, j):
        return pltpu.make_async_copy(_slab(hf_ref, u), _slab(xb_hbm, (slot0_ref, slot1_ref)[j][u]), sem)

    def start(u, c):
        for j in range(TOP_K):
            copy(u, j).start(priority=j)
        return c

    def wait(u, c):
        for j in range(TOP_K):
            copy(u, j).wait()
        return c

    lax.fori_loop(0, tm, start, 0, unroll=8)
    lax.fori_loop(0, tm, wait, 0, unroll=8)


def _dispatch(slots, hf, cap, backing, tm=256):
    n = hf.shape[0] // ROW_SUB
    assert backing.shape == (cap * ROW_SUB, LANES)
    return pl.pallas_call(
        functools.partial(_dispatch_kernel, tm=tm),
        grid=(n // tm,),
        in_specs=[
            pl.BlockSpec((tm,), lambda i: (i,), memory_space=pltpu.SMEM),
            pl.BlockSpec((tm,), lambda i: (i,), memory_space=pltpu.SMEM),
            pl.BlockSpec((tm * ROW_SUB, LANES), lambda i: (i, 0)),
            pl.BlockSpec(memory_space=pl.ANY),
        ],
        out_specs=pl.BlockSpec(memory_space=pl.ANY),
        out_shape=jax.ShapeDtypeStruct((cap * ROW_SUB, LANES), F32),
        scratch_shapes=[pltpu.SemaphoreType.DMA(())],
        input_output_aliases={3: 0},
        compiler_params=_cparams("arbitrary"),
        name="moe_dispatch",
    )(slots[0], slots[1], hf, backing)


def _ffn_kernel(be_ref, nu_ref, par_ref, nxt_ref, x_ref, wg_hbm, wu_hbm, wd_hbm, o_ref,
                wg_f, wu_f, wd_f, wg_s, wu_s, wd_s, sems, *, layer):
    b = pl.program_id(0)
    used = b < nu_ref[0]
    prev_e = be_ref[jnp.maximum(b - 1, 0)]
    new_expert = jnp.logical_or(b == 0, be_ref[b] != prev_e)

    def fetch(e, p):
        return [pltpu.make_async_copy(src.at[layer, e], dst.at[p], sems.at[p, i])
                for i, (src, dst) in enumerate(((wg_hbm, wg_f), (wu_hbm, wu_f), (wd_hbm, wd_f)))]

    @pl.when(jnp.logical_and(used, new_expert))
    def _():
        for p in range(2):
            @pl.when(par_ref[b] == p)
            def _():
                @pl.when(b == 0)
                def _():
                    for cp in fetch(be_ref[0], p):
                        cp.start()

                @pl.when(nxt_ref[b] >= 0)
                def _():
                    for cp in fetch(nxt_ref[b], 1 - p):
                        cp.start()

                for cp in fetch(be_ref[b], p):
                    cp.wait()
                wg_s[...] = wg_f[p].astype(BF16)
                wu_s[...] = wu_f[p].astype(BF16)
                wd_s[...] = wd_f[p].astype(BF16)

    @pl.when(used)
    def _():
        slabs = MXU_COLS // LANES
        gt = up = None
        for kc in range(D_MODEL // MXU_COLS):
            xk = jnp.concatenate([x_ref[pl.ds(kc * slabs + s, MOE_ROWS, stride=ROW_SUB), :] for s in range(slabs)],
                                 axis=1).astype(BF16)
            ks = slice(kc * MXU_COLS, (kc + 1) * MXU_COLS)
            g_k = jnp.dot(xk, wg_s[ks, :], preferred_element_type=F32)
            u_k = jnp.dot(xk, wu_s[ks, :], preferred_element_type=F32)
            gt = g_k if gt is None else gt + g_k
            up = u_k if up is None else up + u_k
        hid = (gt * jax.nn.sigmoid(gt) * up).astype(BF16)
        for c in range(D_MODEL // MXU_COLS):
            y_c = jnp.dot(hid, wd_s[:, c * MXU_COLS:(c + 1) * MXU_COLS], preferred_element_type=F32)
            for s in range(slabs):
                o_ref[pl.ds(c * slabs + s, MOE_ROWS, stride=ROW_SUB), :] = y_c[:, s * LANES:(s + 1) * LANES]

    @pl.when(jnp.logical_not(used))
    def _():
        o_ref[...] = jnp.zeros_like(o_ref)


def _expert_ffn(layer, plan, xb, w_gate, w_up, w_down):
    nblk = xb.shape[0] // (MOE_ROWS * ROW_SUB)
    hbm = pl.BlockSpec(memory_space=pl.ANY)
    grid_spec = pltpu.PrefetchScalarGridSpec(
        num_scalar_prefetch=4,
        grid=(nblk,),
        in_specs=[
            pl.BlockSpec((MOE_ROWS * ROW_SUB, LANES), lambda b, be, nu, par, nxt: (jnp.minimum(b, nu[0] - 1), 0)),
            hbm, hbm, hbm,
        ],
        out_specs=pl.BlockSpec((MOE_ROWS * ROW_SUB, LANES), lambda b, be, nu, par, nxt: (b, 0)),
        scratch_shapes=[
            pltpu.VMEM((2, D_MODEL, EXPERT_HIDDEN), F32),
            pltpu.VMEM((2, D_MODEL, EXPERT_HIDDEN), F32),
            pltpu.VMEM((2, EXPERT_HIDDEN, D_MODEL), F32),
            pltpu.VMEM((D_MODEL, EXPERT_HIDDEN), BF16),
            pltpu.VMEM((D_MODEL, EXPERT_HIDDEN), BF16),
            pltpu.VMEM((EXPERT_HIDDEN, D_MODEL), BF16),
            pltpu.SemaphoreType.DMA((2, 3)),
        ],
    )
    return pl.pallas_call(
        functools.partial(_ffn_kernel, layer=layer),
        grid_spec=grid_spec,
        out_shape=jax.ShapeDtypeStruct(xb.shape, F32),
        compiler_params=_cparams("arbitrary"),
        name="expert_ffn",
    )(plan.block_expert, plan.n_used, plan.parity, plan.next_expert, xb, w_gate, w_up, w_down)


def _combine_kernel(slot0_ref, slot1_ref, next0_ref, next1_ref, x_ref, slab_ref, fw_ref, yb_hbm, o_ref,
                    buf00, buf01, buf10, buf11, sems, *, tm, n_tiles, final_norm):
    i = pl.program_id(0)
    bufs = ((buf00, buf01), (buf10, buf11))
    cur = (slot0_ref, slot1_ref)

    def copies(slots, p, u):
        return [pltpu.make_async_copy(_slab(yb_hbm, slots[j][u]), _slab(bufs[p][j], u), sems.at[p])
                for j in range(TOP_K)]

    def start_all(slots, p):
        def body(u, c):
            for j, cp in enumerate(copies(slots, p, u)):
                cp.start(priority=j)
            return c
        lax.fori_loop(0, tm, body, 0, unroll=8)

    def wait_all(p):
        def body(u, c):
            for cp in copies(cur, p, u):
                cp.wait()
            return c
        lax.fori_loop(0, tm, body, 0, unroll=8)

    @pl.when(i == 0)
    def _():
        start_all(cur, 0)

    for p in range(2):
        @pl.when(i % 2 == p)
        def _():
            @pl.when(i + 1 < n_tiles)
            def _():
                start_all((next0_ref, next1_ref), 1 - p)

            wait_all(p)
            slab = slab_ref[...]
            w1 = slab[:, 2:3]
            w2 = slab[:, 3:4]
            out = x_ref[...] + (_rows_to_2d(bufs[p][0], tm) * w1 + _rows_to_2d(bufs[p][1], tm) * w2)
            if final_norm:
                out = _rms(out, fw_ref[...])
            o_ref[...] = out


def _combine(slots, x, slab, final_w, yb, final_norm, tm=256):
    n = x.shape[0]
    n_tiles = n // tm
    row = lambda i: (i, 0)
    row_buf = pltpu.VMEM((tm * ROW_SUB, LANES), F32)
    return pl.pallas_call(
        functools.partial(_combine_kernel, tm=tm, n_tiles=n_tiles, final_norm=final_norm),
        grid=(n_tiles,),
        in_specs=[
            pl.BlockSpec((tm,), lambda i: (i,), memory_space=pltpu.SMEM),
            pl.BlockSpec((tm,), lambda i: (i,), memory_space=pltpu.SMEM),
            pl.BlockSpec((tm,), lambda i: (jnp.minimum(i + 1, n_tiles - 1),), memory_space=pltpu.SMEM),
            pl.BlockSpec((tm,), lambda i: (jnp.minimum(i + 1, n_tiles - 1),), memory_space=pltpu.SMEM),
            pl.BlockSpec((tm, D_MODEL), row),
            pl.BlockSpec((tm, ROUTER_LANES), row),
            pl.BlockSpec((1, D_MODEL), lambda i: (0, 0)),
            pl.BlockSpec(memory_space=pl.ANY),
        ],
        out_specs=pl.BlockSpec((tm, D_MODEL), row),
        out_shape=jax.ShapeDtypeStruct((n, D_MODEL), F32),
        scratch_shapes=[row_buf, row_buf, row_buf, row_buf, pltpu.SemaphoreType.DMA((2,))],
        compiler_params=_cparams("arbitrary"),
        name="moe_combine",
    )(slots[0], slots[1], slots[0], slots[1], x, slab, final_w, yb)


class MoePlan(NamedTuple):
    slot: tuple
    block_expert: jax.Array
    n_used: jax.Array
    parity: jax.Array
    next_expert: jax.Array
    cap: int


def _moe_plan(plan_t, counts_row, n_tokens):
    eid = plan_t[0:TOP_K].astype(jnp.int32)
    rank = plan_t[4:4 + TOP_K].astype(jnp.int32)
    counts = counts_row[EXPERT_LANE0:EXPERT_LANE0 + N_EXPERTS].astype(jnp.int32)
    padded = (counts + MOE_ROWS - 1) // MOE_ROWS * MOE_ROWS
    pad_ends = jnp.cumsum(padded)
    pad_starts = pad_ends - padded
    expert = jnp.arange(N_EXPERTS, dtype=jnp.int32)[:, None, None]
    region = jnp.sum(jnp.where(eid[None] == expert, pad_starts[:, None, None], 0), axis=0)
    slot = region + rank
    nblk = n_tokens * TOP_K // MOE_ROWS + N_EXPERTS
    block_start = jnp.arange(nblk, dtype=jnp.int32) * MOE_ROWS
    block_expert = jnp.minimum(
        jnp.sum((pad_ends[None, :] <= block_start[:, None]).astype(jnp.int32), axis=1),
        N_EXPERTS - 1).astype(jnp.int32)
    n_used = (pad_ends[-1:] // MOE_ROWS).astype(jnp.int32)
    blk = jnp.arange(nblk, dtype=jnp.int32)
    prev_expert = jnp.concatenate([block_expert[:1], block_expert[:-1]])
    first = (blk < n_used[0]) & ((blk == 0) | (block_expert != prev_expert))
    parity = ((jnp.cumsum(first.astype(jnp.int32)) - 1) % 2).astype(jnp.int32)
    later_first = first[None, :] & (blk[None, :] > blk[:, None])
    next_first = jnp.min(jnp.where(later_first, blk[None, :], nblk), axis=1)
    next_expert = jnp.where(next_first < nblk, block_expert[jnp.minimum(next_first, nblk - 1)], -1)
    return MoePlan(tuple(slot[j] for j in range(TOP_K)), block_expert, n_used, parity,
                   next_expert.astype(jnp.int32), nblk * MOE_ROWS)


def kernel(x, attn_norm_w, w_in, shift_mix, attn_sinks, rwkv_w0, rwkv_w_up, rwkv_a0, rwkv_a_up, rwkv_g_up, rwkv_k_k, rwkv_k_a, rwkv_r_k, rwkv_ln_w, rwkv_ln_b, vres_down, vres_mix, vres_up, vres_v0, w_branch_a, w_branch_b, w_out, ffn_norm_w, router_group_w, router_group_b, router_expert_w, router_expert_b, expert_w_gate, expert_w_up, expert_w_down, final_norm_w):
    batch, seq, _ = x.shape
    n = batch * seq
    depth = w_in.shape[0]
    xf = x.reshape(n, D_MODEL)
    w_in_bf = jnp.swapaxes(w_in, 1, 2).astype(BF16)

    zero_c = jnp.zeros((depth, RWKV_WIDTH), F32)
    v0_all = jnp.concatenate([zero_c[:1], vres_v0.astype(F32)], axis=0)
    pv_pre_all = jnp.stack([rwkv_w0, rwkv_a0, rwkv_k_k, rwkv_k_a, v0_all, zero_c, zero_c, zero_c], axis=1)
    pv_post_all = jnp.stack([rwkv_ln_w, rwkv_ln_b, rwkv_r_k.reshape(depth, RWKV_WIDTH),
                             zero_c, zero_c, zero_c, zero_c, zero_c], axis=1)
    mix_r_all = shift_mix[:, None, :RKV_COLS]
    mix_l_all = jnp.pad(shift_mix[:, None, RKV_COLS:], ((0, 0), (0, 0), (0, LORA_COLS - LORA_WIDTH)))
    wup_all = jnp.pad(rwkv_w_up.astype(BF16), ((0, 0), (0, LANES - DECAY_LORA), (0, 0)))
    aup_all = jnp.pad(rwkv_a_up.astype(BF16), ((0, 0), (DECAY_LORA, LANES - DECAY_LORA - AAA_LORA), (0, 0)))
    gup_all = jnp.pad(rwkv_g_up.astype(BF16), ((0, 0), (0, 2 * LANES - GATE_LORA), (0, 0)))
    vup_all = jnp.pad(vres_up.astype(BF16), ((1, 0), (0, LANES - MV_LORA), (0, 0)))
    vd_all = jnp.pad(vres_down.astype(BF16), ((0, 0), (0, 0), (0, LANES - MV_LORA)))
    mix_v_all = jnp.pad(vres_mix, ((0, 0), (0, LANES - MV_LORA)))[:, None, :]
    router_pad = ROUTER_LANES - N_GROUPS - N_EXPERTS
    w_router_all = jnp.pad(jnp.concatenate([router_group_w, router_expert_w], axis=2),
                           ((0, 0), (0, 0), (0, router_pad)))
    b_router_all = jnp.pad(jnp.concatenate([router_group_b, router_expert_b], axis=1),
                           ((0, 0), (0, router_pad)))[:, None, :]
    attn_norm_all = attn_norm_w[:, None, :]
    ffn_norm_all = ffn_norm_w[:, None, :]
    w_out_bf = w_out.astype(BF16)
    sinks_all = attn_sinks.astype(F32)

    v_first = None
    slot_backing = None
    slot_rows = (n * TOP_K // MOE_ROWS + N_EXPERTS) * MOE_ROWS
    for i in range(depth):
        has_vres = i > 0
        w_gates = w_in_bf[i, GATE_COL0:, :]
        mix_r, mix_l, pv_pre, pv_post = mix_r_all[i], mix_l_all[i], pv_pre_all[i], pv_post_all[i]
        wup, aup, gup, vup = wup_all[i], aup_all[i], gup_all[i], vup_all[i]
        vd, mix_v = (vd_all[i - 1], mix_v_all[i - 1]) if has_vres else (None, None)
        w_router, b_router = w_router_all[i], b_router_all[i]

        need_zeros = slot_backing is None
        proj = _inproj(i, xf, attn_norm_all[i], w_in_bf, w_gates, vd,
                       zero_rows=slot_rows * ROW_SUB if need_zeros else 0)
        qkv, rkv, lora, gates = proj[:4]
        zv = proj[4] if has_vres else None
        if need_zeros:
            slot_backing = proj[-1]
        attn = _attention(qkv, sinks_all[i], seq)
        y, v_new = _rwkv_mixer(rkv, lora, v_first, zv, mix_r, mix_l, mix_v, pv_pre,
                               wup, aup, gup, vup, pv_post, seq)
        if not has_vres:
            v_first = v_new
        merged = _merge(i, attn, y, w_branch_a, w_branch_b, gates)

        x_mid, hf, slab, plan_t, counts = _outproj_router(
            merged, w_out_bf[i], xf, ffn_norm_all[i], w_router, b_router)
        plan = _moe_plan(plan_t, counts[0], n)
        xb = _dispatch(plan.slot, hf, plan.cap, slot_backing)
        yb = _expert_ffn(i, plan, xb, expert_w_gate, expert_w_up, expert_w_down)
        slot_backing = yb
        xf = _combine(plan.slot, x_mid, slab, final_norm_w.reshape(1, D_MODEL), yb, final_norm=(i == depth - 1))
    return xf.reshape(batch, seq, D_MODEL)
```

```python
import functools
import math
from typing import NamedTuple

import jax
import jax.numpy as jnp
from jax import lax
from jax.experimental import pallas as pl
from jax.experimental.pallas import tpu as pltpu

F32 = jnp.float32
BF16 = jnp.bfloat16

D_MODEL = 2048
HEAD_DIM = 64
N_Q_HEADS = 16
N_KV_HEADS = 4
GQA_GROUP = N_Q_HEADS // N_KV_HEADS
ATTN_WIDTH = N_Q_HEADS * HEAD_DIM
KV_WIDTH = N_KV_HEADS * HEAD_DIM
WINDOW = 128
ATTN_BLOCK = 128
ATTN_SUB = 2

RWKV_HEAD = 64
N_RWKV_HEADS = 16
RWKV_WIDTH = N_RWKV_HEADS * RWKV_HEAD
DECAY_LORA = 64
AAA_LORA = 64
MV_LORA = 32
GATE_LORA = 160
RWKV_GN_EPS = 64e-5
DECAY_SCALE = math.exp(-0.5)
KK_NORM_FLOOR = 1e-12
LORA_WIDTH = DECAY_LORA + AAA_LORA + GATE_LORA

N_GROUPS = 4
EXPERTS_PER_GROUP = 8
N_EXPERTS = N_GROUPS * EXPERTS_PER_GROUP
TOP_K = 2
EXPERT_HIDDEN = D_MODEL // 4
NORM_EPS = 1e-5

LANES = 128
SUBLANES = 8
MXU_COLS = 256

SEG = 512
QKV_COLS = ATTN_WIDTH + 2 * KV_WIDTH
RKV_COLS = 3 * RWKV_WIDTH
LORA_COLS = SEG
GATE_COLS = 2 * D_MODEL
QKV_TILES = QKV_COLS // SEG
RKV_TILES = RKV_COLS // SEG
LORA_TILES = LORA_COLS // SEG
GATE_TILES = GATE_COLS // SEG
IN_TILES = QKV_TILES + RKV_TILES + LORA_TILES + GATE_TILES
FIRST_GATE_TILE = QKV_TILES + RKV_TILES + LORA_TILES
ZERO_STEPS = 16
GATE_COL0 = QKV_COLS + RKV_COLS + LORA_WIDTH

SCAN_CHUNK = 64
SCAN_SUB = 2
SCAN_GROUP = 4
MOE_ROWS = 256
ROUTER_LANES = LANES
ROUTER_ROWS = 128
EXPERT_LANE0 = N_GROUPS

ROW_SUB = D_MODEL // LANES

VMEM_LIMIT = 56 * 1024 * 1024


def _rows_to_2d(slab_ref, rows):
    return jnp.concatenate([slab_ref[pl.ds(s, rows, stride=ROW_SUB), :] for s in range(ROW_SUB)], axis=1)


def _store_rows(slab_ref, val):
    rows = val.shape[0]
    for s in range(ROW_SUB):
        slab_ref[pl.ds(s, rows, stride=ROW_SUB), :] = val[:, s * LANES:(s + 1) * LANES]


def _slab(ref, row):
    return ref.at[pl.ds(pl.multiple_of(row * ROW_SUB, ROW_SUB), ROW_SUB)]


def _cparams(*sem):
    return pltpu.CompilerParams(dimension_semantics=sem, vmem_limit_bytes=VMEM_LIMIT)


def _bdot(a, b):
    return jnp.dot(a.astype(BF16), b.astype(BF16), preferred_element_type=F32)


def _bdot_nt(a, b):
    return lax.dot_general(a.astype(BF16), b.astype(BF16), (((1,), (1,)), ((), ())),
                           preferred_element_type=F32)


def _bdot_tn(a, b):
    return lax.dot_general(a.astype(BF16), b.astype(BF16), (((0,), (0,)), ((), ())),
                           preferred_element_type=F32)


def _dot_split_rhs(w01, x, passes):
    acc, rem = None, x
    for p in range(passes):
        part = rem.astype(BF16)
        d = jnp.dot(w01, part, preferred_element_type=F32)
        acc = d if acc is None else acc + d
        if p + 1 < passes:
            rem = rem - part.astype(F32)
    return acc


def _dot_split_lhs(x, w01, passes):
    acc, rem = None, x
    for p in range(passes):
        part = rem.astype(BF16)
        d = jnp.dot(part, w01, preferred_element_type=F32)
        acc = d if acc is None else acc + d
        if p + 1 < passes:
            rem = rem - part.astype(F32)
    return acc


def _rms(x, w):
    ms = jnp.mean(x * x, axis=-1, keepdims=True)
    return x * lax.rsqrt(ms + NORM_EPS) * w


def _inproj_kernel(*refs, has_vres, has_zeros):
    refs = list(refs)
    x_ref, nw_ref, w_ref, wg_ref = refs[:4]
    refs = refs[4:]
    vd_ref = refs.pop(0) if has_vres else None
    qkv_ref, rkv_ref, lora_ref, gate_ref = refs[:4]
    refs = refs[4:]
    zv_ref = refs.pop(0) if has_vres else None
    zeros_ref = refs.pop(0) if has_zeros else None
    (h_scr,) = refs
    j = pl.program_id(1)

    if has_zeros:
        zeros_ref[...] = jnp.zeros_like(zeros_ref)

    @pl.when(j == 0)
    def _():
        h_scr[...] = _rms(x_ref[...], nw_ref[...]).astype(BF16)
        if has_vres:
            zv_ref[...] = jnp.dot(h_scr[...], vd_ref[...], preferred_element_type=F32)

    def project(wt_ref, out_ref):
        for c in range(SEG // MXU_COLS):
            cs = slice(c * MXU_COLS, (c + 1) * MXU_COLS)
            acc = lax.dot_general(h_scr[...], wt_ref[cs, :], (((1,), (1,)), ((), ())),
                                  preferred_element_type=F32)
            out_ref[:, cs] = acc.astype(out_ref.dtype)

    @pl.when(j < QKV_TILES)
    def _():
        project(w_ref, qkv_ref)

    @pl.when((j >= QKV_TILES) & (j < QKV_TILES + RKV_TILES))
    def _():
        project(w_ref, rkv_ref)

    @pl.when(j == QKV_TILES + RKV_TILES)
    def _():
        project(w_ref, lora_ref)

    @pl.when(j >= FIRST_GATE_TILE)
    def _():
        project(wg_ref.at[0], gate_ref)


def _inproj(layer, x, norm_w, w_in, vd, zero_rows=0, tm=1024):
    n = x.shape[0]
    has_vres = vd is not None
    s_rkv = QKV_TILES
    in_specs = [
        pl.BlockSpec((tm, D_MODEL), lambda i, j: (i, 0)),
        pl.BlockSpec((1, D_MODEL), lambda i, j: (0, 0)),
        pl.BlockSpec((None, SEG, D_MODEL), lambda i, j: (layer, jnp.minimum(j, FIRST_GATE_TILE - 1), 0)),
        pl.BlockSpec((pl.Element(1), pl.Element(SEG), pl.Element(D_MODEL)),
                     lambda i, j: (layer, pl.multiple_of(
                         GATE_COL0 + SEG * jnp.clip(j - FIRST_GATE_TILE, 0, GATE_TILES - 1), 2 * SUBLANES), 0)),
    ]
    args = [x, norm_w, w_in, w_in]
    out_specs = [
        pl.BlockSpec((tm, SEG), lambda i, j: (i, jnp.clip(j, 0, QKV_TILES - 1))),
        pl.BlockSpec((tm, SEG), lambda i, j: (i, jnp.clip(j - s_rkv, 0, RKV_TILES - 1))),
        pl.BlockSpec((tm, SEG), lambda i, j: (i, 0)),
        pl.BlockSpec((tm, SEG), lambda i, j: (i, jnp.clip(j - FIRST_GATE_TILE, 0, GATE_TILES - 1))),
    ]
    out_shape = [
        jax.ShapeDtypeStruct((n, QKV_COLS), BF16),
        jax.ShapeDtypeStruct((n, RKV_COLS), F32),
        jax.ShapeDtypeStruct((n, LORA_COLS), F32),
        jax.ShapeDtypeStruct((n, GATE_COLS), BF16),
    ]
    if has_vres:
        in_specs.append(pl.BlockSpec((D_MODEL, LANES), lambda i, j: (0, 0)))
        args.append(vd)
        out_specs.append(pl.BlockSpec((tm, LANES), lambda i, j: (i, 0)))
        out_shape.append(jax.ShapeDtypeStruct((n, LANES), F32))
    if zero_rows:
        blk = zero_rows // ((n // tm) * ZERO_STEPS)
        assert blk * (n // tm) * ZERO_STEPS == zero_rows and blk % SUBLANES == 0
        out_specs.append(pl.BlockSpec((blk, LANES), lambda i, j: (i * ZERO_STEPS + jnp.minimum(j, ZERO_STEPS - 1), 0)))
        out_shape.append(jax.ShapeDtypeStruct((zero_rows, LANES), F32))
    return pl.pallas_call(
        functools.partial(_inproj_kernel, has_vres=has_vres, has_zeros=bool(zero_rows)),
        grid=(n // tm, IN_TILES),
        in_specs=in_specs,
        out_specs=out_specs,
        out_shape=out_shape,
        scratch_shapes=[pltpu.VMEM((tm, D_MODEL), BF16)],
        compiler_params=_cparams("parallel", "arbitrary"),
        name="inproj",
    )(*args)


def _attn_kernel(sink_ref, q_ref, kp_ref, kc_ref, vp_ref, vc_ref, bias_ref, o_ref, *, tiles_per_seq):
    first = (pl.program_id(0) % tiles_per_seq) == 0
    col = lax.broadcasted_iota(jnp.int32, (ATTN_BLOCK, 2 * ATTN_BLOCK), 1)
    pad_keys = jnp.logical_and(first, col < ATTN_BLOCK)
    scale = HEAD_DIM ** -0.5
    items = [(sb, hk) for sb in range(ATTN_SUB) for hk in range(N_KV_HEADS)]

    def window(prev_ref, cur_ref, sb, hk):
        ks = slice(hk * HEAD_DIM, (hk + 1) * HEAD_DIM)
        rows = slice(sb * ATTN_BLOCK, (sb + 1) * ATTN_BLOCK)
        prev = prev_ref[:, ks] if sb == 0 else cur_ref[(sb - 1) * ATTN_BLOCK:sb * ATTN_BLOCK, ks]
        return jnp.concatenate([prev, cur_ref[rows, ks]], axis=0)

    scores = {}
    for sb, hk in items:
        kw = window(kp_ref, kc_ref, sb, hk)
        for g in range(GQA_GROUP):
            h = hk * GQA_GROUP + g
            qh = q_ref[sb * ATTN_BLOCK:(sb + 1) * ATTN_BLOCK, h * HEAD_DIM:(h + 1) * HEAD_DIM]
            s = lax.dot_general(qh, kw, (((1,), (1,)), ((), ())), preferred_element_type=F32)
            s = s * scale + bias_ref[h]
            if sb == 0:
                s = jnp.where(pad_keys, -jnp.inf, s)
            scores[(sb, h)] = s
    row_max = {key: jnp.max(s, axis=-1, keepdims=True) for key, s in scores.items()}
    m = {key: jnp.maximum(row_max[key], sink_ref[key[1]]) for key in scores}
    p = {key: jnp.exp(scores[key] - m[key]) for key in scores}
    row_sum = {key: jnp.sum(p[key], axis=-1, keepdims=True) for key in scores}
    denoms = {key: row_sum[key] + jnp.exp(sink_ref[key[1]] - m[key]) for key in scores}
    probs = {key: p[key].astype(BF16) for key in scores}
    for sb, hk in items:
        vw = window(vp_ref, vc_ref, sb, hk)
        for g in range(GQA_GROUP):
            h = hk * GQA_GROUP + g
            o = jnp.dot(probs[(sb, h)], vw, preferred_element_type=F32) / denoms[(sb, h)]
            o_ref[sb * ATTN_BLOCK:(sb + 1) * ATTN_BLOCK, h * HEAD_DIM:(h + 1) * HEAD_DIM] = o.astype(BF16)


def _attn_bias():
    qi = jnp.arange(ATTN_BLOCK)[:, None]
    kj = jnp.arange(2 * ATTN_BLOCK)[None, :]
    dist = qi + ATTN_BLOCK - kj
    valid = (dist >= 0) & (dist < WINDOW)
    slopes = jnp.exp2(-8.0 * jnp.arange(1, N_Q_HEADS + 1, dtype=F32) / N_Q_HEADS)
    bias = -slopes[:, None, None] * dist.astype(F32)[None]
    return jnp.where(valid[None], bias, -jnp.inf)


def _attention(qkv, sinks, seq):
    n = qkv.shape[0]
    rows = ATTN_SUB * ATTN_BLOCK
    tiles_per_seq = seq // rows
    kcol = ATTN_WIDTH // KV_WIDTH
    prev = lambda i: jnp.maximum(i * ATTN_SUB - 1, 0)
    return pl.pallas_call(
        functools.partial(_attn_kernel, tiles_per_seq=tiles_per_seq),
        grid=(n // rows,),
        in_specs=[
            pl.BlockSpec(memory_space=pltpu.SMEM),
            pl.BlockSpec((rows, ATTN_WIDTH), lambda i: (i, 0)),
            pl.BlockSpec((ATTN_BLOCK, KV_WIDTH), lambda i: (prev(i), kcol)),
            pl.BlockSpec((rows, KV_WIDTH), lambda i: (i, kcol)),
            pl.BlockSpec((ATTN_BLOCK, KV_WIDTH), lambda i: (prev(i), kcol + 1)),
            pl.BlockSpec((rows, KV_WIDTH), lambda i: (i, kcol + 1)),
            pl.BlockSpec((N_Q_HEADS, ATTN_BLOCK, 2 * ATTN_BLOCK), lambda i: (0, 0, 0)),
        ],
        out_specs=pl.BlockSpec((rows, ATTN_WIDTH), lambda i: (i, 0)),
        out_shape=jax.ShapeDtypeStruct((n, ATTN_WIDTH), BF16),
        compiler_params=_cparams("parallel"),
        name="swa_attention",
    )(sinks, qkv, qkv, qkv, qkv, qkv, _attn_bias())


def _head_ones():
    ch = jnp.arange(MXU_COLS) // RWKV_HEAD
    return (ch[:, None] == ch[None, :]).astype(BF16)


def _head_sums(x, ones, passes):
    parts = [_dot_split_lhs(x[:, c:c + MXU_COLS], ones, passes) for c in range(0, RWKV_WIDTH, MXU_COLS)]
    return jnp.concatenate(parts, axis=1)


def _rwkv_kernel(*refs, has_vres):
    refs = list(refs)
    rkv_ref, lo_ref = refs[:2]
    refs = refs[2:]
    if has_vres:
        vf_ref, zv_ref, mixv_ref = refs[:3]
        refs = refs[3:]
    mixr_ref, mixl_ref, pv_ref, wup_ref, aup_ref, gup_ref, vup_ref, ones_ref, pp_ref = refs[:9]
    refs = refs[9:]
    y_ref = refs.pop(0)
    v_out = None if has_vres else refs.pop(0)
    s_scr, prev_rkv, prev_lo = refs[:3]
    prev_zv = refs[3] if has_vres else None

    L = SCAN_CHUNK
    W = SCAN_GROUP * RWKV_HEAD
    n_seq, step_rows = rkv_ref.shape[0], rkv_ref.shape[1]
    n_grp = N_RWKV_HEADS // SCAN_GROUP
    ones = ones_ref[...]

    @pl.when(pl.program_id(0) == 0)
    def _():
        s_scr[...] = jnp.zeros_like(s_scr)
        prev_rkv[...] = jnp.zeros_like(prev_rkv)
        prev_lo[...] = jnp.zeros_like(prev_lo)
        if has_vres:
            prev_zv[...] = jnp.zeros_like(prev_zv)

    def lerp_shift(z_ref, carry_ref, b, mix):
        z = z_ref[b]
        row = lax.broadcasted_iota(jnp.int32, z.shape, 0)
        back = jnp.where(row == 0, carry_ref[b, SUBLANES - 1:SUBLANES, :], pltpu.roll(z, 1, 0))
        carry_ref[b] = z_ref[b, step_rows - SUBLANES:step_rows, :]
        return z + (back - z) * mix

    w0, a0, k_k, k_a, v0 = (pv_ref[i:i + 1, :] for i in range(5))
    pre = []
    for b in range(n_seq):
        zs = lerp_shift(rkv_ref, prev_rkv, b, mixr_ref[...])
        los = lerp_shift(lo_ref, prev_lo, b, mixl_ref[...])
        r = zs[:, :RWKV_WIDTH]
        kr = zs[:, RWKV_WIDTH:2 * RWKV_WIDTH]
        vr = zs[:, 2 * RWKV_WIDTH:]
        wa = los[:, :LANES]
        col = lax.broadcasted_iota(jnp.int32, wa.shape, 1)
        wa_in = jnp.where(col < DECAY_LORA, jnp.tanh(wa), wa).astype(BF16)
        gd = los[:, LANES:3 * LANES]
        col_g = lax.broadcasted_iota(jnp.int32, gd.shape, 1)
        g_in = jnp.where(col_g < GATE_LORA, jax.nn.sigmoid(gd), gd).astype(BF16)
        dw = jnp.dot(wa_in, wup_ref[...], preferred_element_type=F32)
        da = jnp.dot(wa_in, aup_ref[...], preferred_element_type=F32)
        g = jnp.dot(g_in, gup_ref[...], preferred_element_type=F32)
        lw = -DECAY_SCALE * jax.nn.sigmoid(w0 + dw)
        a = jax.nn.sigmoid(a0 + da)
        if has_vres:
            zvs = lerp_shift(zv_ref, prev_zv, b, mixv_ref[...])
            dv = jnp.dot(zvs.astype(BF16), vup_ref[...], preferred_element_type=F32)
            vr = vr + (vf_ref[b] - vr) * jax.nn.sigmoid(v0 + dv)
        else:
            v_out[b] = vr
        kk0 = kr * k_k
        ss = _head_sums(kk0 * kk0, ones, 2)
        kk = kk0 * lax.rsqrt(jnp.maximum(ss, KK_NORM_FLOOR ** 2))
        pre.append(dict(r=r, lw=lw, k=kr * (1.0 + (a - 1.0) * k_a), v=vr, al=-kk, be=kk * a, g=g))

    row = lax.broadcasted_iota(jnp.int32, (L, L), 0)
    col = lax.broadcasted_iota(jnp.int32, (L, L), 1)
    tri = (row >= col).astype(BF16)
    t_w = lax.broadcasted_iota(jnp.int32, (L, W), 0)
    lane_w = lax.broadcasted_iota(jnp.int32, (L, W), 1)
    s_w = lane_w % RWKV_HEAD
    strict = t_w > s_w
    incl = t_w >= s_w
    eye = (t_w == s_w).astype(F32)
    blk_w = lane_w // RWKV_HEAD
    bd_mask = (lax.broadcasted_iota(jnp.int32, (W, W), 0) // RWKV_HEAD
               == lax.broadcasted_iota(jnp.int32, (W, W), 1) // RWKV_HEAD)

    def blockdiag(w):
        tiled = jnp.concatenate([w.astype(BF16)] * SCAN_GROUP, axis=0)
        return jnp.where(bd_mask, tiled, jnp.zeros_like(tiled))

    ln_w, ln_b, r_k = (pp_ref[i:i + 1, :] for i in range(3))
    items = [(b, g) for b in range(n_seq) for g in range(n_grp)]
    for sub in range(step_rows // L):
        rows = slice(sub * L, (sub + 1) * L)
        ops = {}
        for b in range(n_seq):
            p = pre[b]
            lw = p['lw'][rows]
            cum = _dot_split_rhs(tri, lw, 2)
            p_inc = jnp.exp(cum)
            p_exc = jnp.exp(cum - lw)
            p_inv = jnp.exp(-cum)
            p_last = p_inc[L - 1:L, :]
            a_t = p['al'][rows] * p_exc
            r_t = p['r'][rows] * p_inc
            b_t = p['be'][rows] * p_inv
            k_t = p['k'][rows] * p_inv
            b_end = b_t * p_last
            k_end = k_t * p_last
            v = p['v'][rows]
            for g in range(n_grp):
                gs = slice(g * W, (g + 1) * W)
                ops[(b, g)] = dict(
                    ar=jnp.concatenate([a_t[:, gs], r_t[:, gs]], axis=0),
                    bt=b_t[:, gs], kt=k_t[:, gs], v=v[:, gs], p_last=p_last[:, gs],
                    bke=jnp.concatenate([b_end[:, gs], k_end[:, gs]], axis=0))

        gm, xs, av, s0 = {}, {}, {}, {}
        for it in items:
            o = ops[it]
            rhs = jnp.concatenate([blockdiag(o['bt']), blockdiag(o['kt'])], axis=0)
            gm[it] = _bdot_nt(o['ar'], rhs)
            s0[it] = s_scr[it[0], it[1]]
            xs[it] = _bdot_nt(o['ar'], blockdiag(s0[it]))
        a_ab, a_rb = {}, {}
        for it in items:
            g = gm[it]
            a_ab[it] = jnp.where(strict, g[:L, :W], 0.0)
            a_rb[it] = jnp.where(incl, g[L:, :W], 0.0)
            a_kk = jnp.concatenate([jnp.where(strict, g[:L, W:], 0.0), jnp.where(incl, g[L:, W:], 0.0)], axis=0)
            av[it] = _bdot(a_kk, blockdiag(ops[it]['v']))

        pw = {it: _bdot(a_ab[it], blockdiag(a_ab[it])) for it in items}
        tinv = {it: eye + a_ab[it] for it in items}
        span = 2
        while 2 * span < L:
            both = {it: _bdot(jnp.concatenate([pw[it], tinv[it]], axis=0), blockdiag(pw[it])) for it in items}
            tinv = {it: tinv[it] + both[it][L:] for it in items}
            pw = {it: both[it][:L] for it in items}
            span *= 2
        tinv = {it: tinv[it] + _bdot(tinv[it], blockdiag(pw[it])) for it in items}

        u = {it: _bdot(tinv[it], blockdiag(xs[it][:L] + av[it][:L])) for it in items}
        y = {it: xs[it][L:] + av[it][L:] + _bdot(a_rb[it], blockdiag(u[it])) for it in items}
        for it in items:
            o = ops[it]
            full = _bdot_tn(jnp.concatenate([u[it], o['v']], axis=0), o['bke'])
            upd = full[:RWKV_HEAD]
            for h in range(1, SCAN_GROUP):
                upd = jnp.where(blk_w == h, full[h * RWKV_HEAD:(h + 1) * RWKV_HEAD], upd)
            s_scr[it[0], it[1]] = s0[it] * o['p_last'] + upd

        inv_n = 1.0 / RWKV_HEAD
        for b in range(n_seq):
            p = pre[b]
            yb = jnp.concatenate([y[(b, g)] for g in range(n_grp)], axis=1)
            d = yb - _head_sums(yb, ones, 2) * inv_n
            var = _head_sums(d * d, ones, 2) * inv_n
            yn = d * lax.rsqrt(var + RWKV_GN_EPS) * ln_w + ln_b
            bonus = _head_sums(p['r'][rows] * p['k'][rows] * r_k, ones, 2) * p['v'][rows]
            y_ref[b, rows, :] = ((yn + bonus) * p['g'][rows]).astype(BF16)


def _rwkv_mixer(rkv, lora, v_first, zv, mix_r, mix_l, mix_v, pv_pre, wup, aup, gup, vup, pv_post, seq):
    n = rkv.shape[0]
    n_seq = n // seq
    has_vres = v_first is not None
    step_rows = SCAN_SUB * SCAN_CHUNK
    timed = lambda cols: pl.BlockSpec((n_seq, step_rows, cols), lambda t: (0, t, 0))
    const = lambda shape: pl.BlockSpec(shape, lambda t: (0, 0))
    by_seq = lambda a: a.reshape(n_seq, seq, a.shape[-1])
    in_specs = [timed(RKV_COLS), timed(LORA_COLS)]
    args = [by_seq(rkv), by_seq(lora)]
    if has_vres:
        in_specs += [timed(RWKV_WIDTH), timed(LANES), const((1, LANES))]
        args += [by_seq(v_first), by_seq(zv), mix_v]
    in_specs += [
        const((1, RKV_COLS)), const((1, LORA_COLS)), const((SUBLANES, RWKV_WIDTH)),
        const((LANES, RWKV_WIDTH)), const((LANES, RWKV_WIDTH)), const((2 * LANES, RWKV_WIDTH)),
        const((LANES, RWKV_WIDTH)), const((MXU_COLS, MXU_COLS)), const((SUBLANES, RWKV_WIDTH)),
    ]
    args += [mix_r, mix_l, pv_pre, wup, aup, gup, vup, _head_ones(), pv_post]
    out_specs = [timed(RWKV_WIDTH)]
    out_shape = [jax.ShapeDtypeStruct((n_seq, seq, RWKV_WIDTH), BF16)]
    if not has_vres:
        out_specs.append(timed(RWKV_WIDTH))
        out_shape.append(jax.ShapeDtypeStruct((n_seq, seq, RWKV_WIDTH), F32))
    scratch = [
        pltpu.VMEM((n_seq, N_RWKV_HEADS // SCAN_GROUP, RWKV_HEAD, SCAN_GROUP * RWKV_HEAD), F32),
        pltpu.VMEM((n_seq, SUBLANES, RKV_COLS), F32),
        pltpu.VMEM((n_seq, SUBLANES, LORA_COLS), F32),
    ]
    if has_vres:
        scratch.append(pltpu.VMEM((n_seq, SUBLANES, LANES), F32))
    outs = pl.pallas_call(
        functools.partial(_rwkv_kernel, has_vres=has_vres),
        grid=(seq // step_rows,),
        in_specs=in_specs,
        out_specs=out_specs,
        out_shape=out_shape,
        scratch_shapes=scratch,
        compiler_params=_cparams("arbitrary"),
        name="rwkv_mixer",
    )(*args)
    y = outs[0].reshape(n, RWKV_WIDTH)
    return y, (None if has_vres else outs[1].reshape(n, RWKV_WIDTH))


def _merge_kernel(a_ref, y_ref, wa_ref, wb_ref, ga_ref, gb_ref, o_ref):
    for c in range(o_ref.shape[1] // MXU_COLS):
        cs = slice(c * MXU_COLS, (c + 1) * MXU_COLS)
        pa = jnp.dot(a_ref[...], wa_ref[:, cs].astype(BF16), preferred_element_type=F32)
        pb = jnp.dot(y_ref[...], wb_ref[:, cs].astype(BF16), preferred_element_type=F32)
        ga = jax.nn.sigmoid(ga_ref[:, cs].astype(F32))
        gb = jax.nn.sigmoid(gb_ref[:, cs].astype(F32))
        o_ref[:, cs] = (ga * pa + gb * pb).astype(BF16)


def _merge(layer, attn, y, wa, wb, gates, tm=1024, tn=1024):
    n = attn.shape[0]
    nj = D_MODEL // tn
    return pl.pallas_call(
        _merge_kernel,
        grid=(n // tm, nj),
        in_specs=[
            pl.BlockSpec((tm, ATTN_WIDTH), lambda i, j: (i, 0)),
            pl.BlockSpec((tm, RWKV_WIDTH), lambda i, j: (i, 0)),
            pl.BlockSpec((None, ATTN_WIDTH, tn), lambda i, j: (layer, 0, j)),
            pl.BlockSpec((None, RWKV_WIDTH, tn), lambda i, j: (layer, 0, j)),
            pl.BlockSpec((tm, tn), lambda i, j: (i, j)),
            pl.BlockSpec((tm, tn), lambda i, j: (i, j + nj)),
        ],
        out_specs=pl.BlockSpec((tm, tn), lambda i, j: (i, j)),
        out_shape=jax.ShapeDtypeStruct((n, D_MODEL), BF16),
        compiler_params=_cparams("parallel", "arbitrary"),
        name="gated_merge",
    )(attn, y, wa, wb, gates, gates)


def _outproj_router_kernel(m_ref, wo_ref, x_ref, nw_ref, wr_ref, br_ref,
                           xo_ref, hf_ref, slab_ref, plan_ref, cnt_ref, run_scr):
    @pl.when(pl.program_id(0) == 0)
    def _():
        run_scr[...] = jnp.zeros_like(run_scr)

    tm = x_ref.shape[0]
    sq = jnp.zeros((tm, LANES), F32)
    for c in range(D_MODEL // MXU_COLS):
        cs = slice(c * MXU_COLS, (c + 1) * MXU_COLS)
        xn_c = x_ref[:, cs] + jnp.dot(m_ref[...], wo_ref[:, cs], preferred_element_type=F32)
        xo_ref[:, cs] = xn_c
        for l in range(MXU_COLS // LANES):
            piece = xn_c[:, l * LANES:(l + 1) * LANES]
            sq = sq + piece * piece
    ms = jnp.sum(sq, axis=-1, keepdims=True) * (1.0 / D_MODEL)
    hf = xo_ref[...] * lax.rsqrt(ms + NORM_EPS) * nw_ref[...]
    _store_rows(hf_ref, hf)

    h_hi = hf.astype(BF16)
    h_lo = (hf - h_hi.astype(F32)).astype(BF16)
    wr = wr_ref[...]
    w_hi = wr.astype(BF16)
    w_lo = (wr - w_hi.astype(F32)).astype(BF16)
    lg = (jnp.dot(h_hi, w_hi, preferred_element_type=F32)
          + jnp.dot(h_hi, w_lo, preferred_element_type=F32)
          + jnp.dot(h_lo, w_hi, preferred_element_type=F32)) + br_ref[...]

    chunks = range(tm // ROUTER_ROWS)
    lgs = [lg[c * ROUTER_ROWS:(c + 1) * ROUTER_ROWS] for c in chunks]
    lane = lax.broadcasted_iota(jnp.int32, (ROUTER_ROWS, ROUTER_LANES), 1).astype(F32)
    neg = -jnp.inf
    big = float(ROUTER_LANES)
    is_group = lane < N_GROUPS

    def row_max(vals):
        return [jnp.max(v, axis=-1, keepdims=True) for v in vals]

    def first_index(vals, mx):
        return [jnp.min(jnp.where(v == m, lane, big), axis=-1, keepdims=True) for v, m in zip(vals, mx)]

    def row_sum(vals):
        return [jnp.sum(v, axis=-1, keepdims=True) for v in vals]

    gl = [jnp.where(is_group, v, neg) for v in lgs]
    gmax = row_max(gl)
    gsel = first_index(gl, gmax)
    gsum = row_sum([jnp.where(is_group, jnp.exp(v - m), 0.0) for v, m in zip(lgs, gmax)])
    lo_lane = [EXPERT_LANE0 + EXPERTS_PER_GROUP * g for g in gsel]
    el = [jnp.where((lane >= lo) & (lane < lo + EXPERTS_PER_GROUP), v, neg) for v, lo in zip(lgs, lo_lane)]
    v1 = row_max(el)
    i1 = first_index(el, v1)
    el2 = [jnp.where(lane == i, neg, v) for v, i in zip(el, i1)]
    v2 = row_max(el2)
    i2 = first_index(el2, v2)
    oh1 = [lane == i for i in i1]
    oh2 = [lane == i for i in i2]

    cnt = jnp.concatenate([a.astype(F32) + b.astype(F32) for a, b in zip(oh1, oh2)], axis=0)
    r_i = lax.broadcasted_iota(jnp.int32, (tm, tm), 0)
    c_i = lax.broadcasted_iota(jnp.int32, (tm, tm), 1)
    before = jnp.dot((r_i > c_i).astype(BF16), cnt.astype(BF16), preferred_element_type=F32)
    tot = before + run_scr[0:1, :]
    tots = [tot[c * ROUTER_ROWS:(c + 1) * ROUTER_ROWS] for c in chunks]
    rank1 = row_sum([jnp.where(o, t, 0.0) for o, t in zip(oh1, tots)])
    rank2 = row_sum([jnp.where(o, t, 0.0) for o, t in zip(oh2, tots)])
    run = run_scr[0:1, :] + jnp.sum(cnt, axis=0, keepdims=True)
    run_scr[...] = jnp.broadcast_to(run, run_scr.shape)
    cnt_ref[...] = jnp.broadcast_to(run, cnt_ref.shape)

    for c in chunks:
        gp = 1.0 / gsum[c]
        e21 = jnp.exp(v2[c] - v1[c])
        ew1 = gp / (1.0 + e21)
        ew2 = gp * e21 / (1.0 + e21)
        slab = jnp.where(lane == 0, i1[c] - EXPERT_LANE0,
               jnp.where(lane == 1, i2[c] - EXPERT_LANE0,
               jnp.where(lane == 2, ew1,
               jnp.where(lane == 3, ew2,
               jnp.where(lane == 4, rank1[c],
               jnp.where(lane == 5, rank2[c], 0.0))))))
        slab_ref[c * ROUTER_ROWS:(c + 1) * ROUTER_ROWS, :] = slab
        plan_ref[:, c * ROUTER_ROWS:(c + 1) * ROUTER_ROWS] = jnp.transpose(slab)[:SUBLANES, :]


def _outproj_router(merged, w_out, x, norm_w, w_router, b_router, tm=512):
    n = x.shape[0]
    row = lambda i: (i, 0)
    const = lambda i: (0, 0)
    return pl.pallas_call(
        _outproj_router_kernel,
        grid=(n // tm,),
        in_specs=[
            pl.BlockSpec((tm, D_MODEL), row),
            pl.BlockSpec((D_MODEL, D_MODEL), const),
            pl.BlockSpec((tm, D_MODEL), row),
            pl.BlockSpec((1, D_MODEL), const),
            pl.BlockSpec((D_MODEL, ROUTER_LANES), const),
            pl.BlockSpec((1, ROUTER_LANES), const),
        ],
        out_specs=[
            pl.BlockSpec((tm, D_MODEL), row),
            pl.BlockSpec((tm * ROW_SUB, LANES), row),
            pl.BlockSpec((tm, ROUTER_LANES), row),
            pl.BlockSpec((SUBLANES, tm), lambda i: (0, i)),
            pl.BlockSpec((SUBLANES, ROUTER_LANES), const),
        ],
        out_shape=[
            jax.ShapeDtypeStruct((n, D_MODEL), F32),
            jax.ShapeDtypeStruct((n * ROW_SUB, LANES), F32),
            jax.ShapeDtypeStruct((n, ROUTER_LANES), F32),
            jax.ShapeDtypeStruct((SUBLANES, n), F32),
            jax.ShapeDtypeStruct((SUBLANES, ROUTER_LANES), F32),
        ],
        scratch_shapes=[pltpu.VMEM((SUBLANES, ROUTER_LANES), F32)],
        compiler_params=_cparams("arbitrary"),
        name="outproj_router",
    )(merged, w_out, x, norm_w, w_router, b_router)


def _dispatch_kernel(slot0_ref, slot1_ref, hf_ref, xb_in_hbm, xb_hbm, sem, *, tm):
    del xb_in_hbm

    def copy(u, j):
        return pltpu.make_async_copy(_slab(hf_ref, u), _slab(xb_hbm, (slot0_ref, slot1_ref)[j][u]), sem)

    def start(u, c):
        for j in range(TOP_K):
            copy(u, j).start(priority=j)
        return c

    def wait(u, c):
        for j in range(TOP_K):
            copy(u, j).wait()
        return c

    lax.fori_loop(0, tm, start, 0, unroll=8)
    lax.fori_loop(0, tm, wait, 0, unroll=8)


def _dispatch(slots, hf, cap, backing, tm=256):
    n = hf.shape[0] // ROW_SUB
    assert backing.shape == (cap * ROW_SUB, LANES)
    return pl.pallas_call(
        functools.partial(_dispatch_kernel, tm=tm),
        grid=(n // tm,),
        in_specs=[
            pl.BlockSpec((tm,), lambda i: (i,), memory_space=pltpu.SMEM),
            pl.BlockSpec((tm,), lambda i: (i,), memory_space=pltpu.SMEM),
            pl.BlockSpec((tm * ROW_SUB, LANES), lambda i: (i, 0)),
            pl.BlockSpec(memory_space=pl.ANY),
        ],
        out_specs=pl.BlockSpec(memory_space=pl.ANY),
        out_shape=jax.ShapeDtypeStruct((cap * ROW_SUB, LANES), F32),
        scratch_shapes=[pltpu.SemaphoreType.DMA(())],
        input_output_aliases={3: 0},
        compiler_params=_cparams("arbitrary"),
        name="moe_dispatch",
    )(slots[0], slots[1], hf, backing)


def _ffn_kernel(be_ref, nu_ref, par_ref, nxt_ref, x_ref, wg_hbm, wu_hbm, wd_hbm, o_ref,
                wg_f, wu_f, wd_f, wg_s, wu_s, wd_s, sems, *, layer):
    b = pl.program_id(0)
    used = b < nu_ref[0]
    prev_e = be_ref[jnp.maximum(b - 1, 0)]
    new_expert = jnp.logical_or(b == 0, be_ref[b] != prev_e)

    def fetch(e, p):
        return [pltpu.make_async_copy(src.at[layer, e], dst.at[p], sems.at[p, i])
                for i, (src, dst) in enumerate(((wg_hbm, wg_f), (wu_hbm, wu_f), (wd_hbm, wd_f)))]

    @pl.when(jnp.logical_and(used, new_expert))
    def _():
        for p in range(2):
            @pl.when(par_ref[b] == p)
            def _():
                @pl.when(b == 0)
                def _():
                    for cp in fetch(be_ref[0], p):
                        cp.start()

                @pl.when(nxt_ref[b] >= 0)
                def _():
                    for cp in fetch(nxt_ref[b], 1 - p):
                        cp.start()

                for cp in fetch(be_ref[b], p):
                    cp.wait()
                wg_s[...] = wg_f[p].astype(BF16)
                wu_s[...] = wu_f[p].astype(BF16)
                wd_s[...] = wd_f[p].astype(BF16)

    @pl.when(used)
    def _():
        slabs = MXU_COLS // LANES
        gt = up = None
        for kc in range(D_MODEL // MXU_COLS):
            xk = jnp.concatenate([x_ref[pl.ds(kc * slabs + s, MOE_ROWS, stride=ROW_SUB), :] for s in range(slabs)],
                                 axis=1).astype(BF16)
            ks = slice(kc * MXU_COLS, (kc + 1) * MXU_COLS)
            g_k = jnp.dot(xk, wg_s[ks, :], preferred_element_type=F32)
            u_k = jnp.dot(xk, wu_s[ks, :], preferred_element_type=F32)
            gt = g_k if gt is None else gt + g_k
            up = u_k if up is None else up + u_k
        hid = (gt * jax.nn.sigmoid(gt) * up).astype(BF16)
        for c in range(D_MODEL // MXU_COLS):
            y_c = jnp.dot(hid, wd_s[:, c * MXU_COLS:(c + 1) * MXU_COLS], preferred_element_type=F32)
            for s in range(slabs):
                o_ref[pl.ds(c * slabs + s, MOE_ROWS, stride=ROW_SUB), :] = y_c[:, s * LANES:(s + 1) * LANES]

    @pl.when(jnp.logical_not(used))
    def _():
        o_ref[...] = jnp.zeros_like(o_ref)


def _expert_ffn(layer, plan, xb, w_gate, w_up, w_down):
    nblk = xb.shape[0] // (MOE_ROWS * ROW_SUB)
    hbm = pl.BlockSpec(memory_space=pl.ANY)
    grid_spec = pltpu.PrefetchScalarGridSpec(
        num_scalar_prefetch=4,
        grid=(nblk,),
        in_specs=[
            pl.BlockSpec((MOE_ROWS * ROW_SUB, LANES), lambda b, be, nu, par, nxt: (jnp.minimum(b, nu[0] - 1), 0)),
            hbm, hbm, hbm,
        ],
        out_specs=pl.BlockSpec((MOE_ROWS * ROW_SUB, LANES), lambda b, be, nu, par, nxt: (b, 0)),
        scratch_shapes=[
            pltpu.VMEM((2, D_MODEL, EXPERT_HIDDEN), F32),
            pltpu.VMEM((2, D_MODEL, EXPERT_HIDDEN), F32),
            pltpu.VMEM((2, EXPERT_HIDDEN, D_MODEL), F32),
            pltpu.VMEM((D_MODEL, EXPERT_HIDDEN), BF16),
            pltpu.VMEM((D_MODEL, EXPERT_HIDDEN), BF16),
            pltpu.VMEM((EXPERT_HIDDEN, D_MODEL), BF16),
            pltpu.SemaphoreType.DMA((2, 3)),
        ],
    )
    return pl.pallas_call(
        functools.partial(_ffn_kernel, layer=layer),
        grid_spec=grid_spec,
        out_shape=jax.ShapeDtypeStruct(xb.shape, F32),
        compiler_params=_cparams("arbitrary"),
        name="expert_ffn",
    )(plan.block_expert, plan.n_used, plan.parity, plan.next_expert, xb, w_gate, w_up, w_down)


def _combine_kernel(slot0_ref, slot1_ref, next0_ref, next1_ref, x_ref, slab_ref, fw_ref, yb_hbm, o_ref,
                    buf00, buf01, buf10, buf11, sems, *, tm, n_tiles, final_norm):
    i = pl.program_id(0)
    bufs = ((buf00, buf01), (buf10, buf11))
    cur = (slot0_ref, slot1_ref)

    def copies(slots, p, u):
        return [pltpu.make_async_copy(_slab(yb_hbm, slots[j][u]), _slab(bufs[p][j], u), sems.at[p])
                for j in range(TOP_K)]

    def start_all(slots, p):
        def body(u, c):
            for j, cp in enumerate(copies(slots, p, u)):
                cp.start(priority=j)
            return c
        lax.fori_loop(0, tm, body, 0, unroll=8)

    def wait_all(p):
        def body(u, c):
            for cp in copies(cur, p, u):
                cp.wait()
            return c
        lax.fori_loop(0, tm, body, 0, unroll=8)

    @pl.when(i == 0)
    def _():
        start_all(cur, 0)

    for p in range(2):
        @pl.when(i % 2 == p)
        def _():
            @pl.when(i + 1 < n_tiles)
            def _():
                start_all((next0_ref, next1_ref), 1 - p)

            wait_all(p)
            slab = slab_ref[...]
            w1 = slab[:, 2:3]
            w2 = slab[:, 3:4]
            out = x_ref[...] + (_rows_to_2d(bufs[p][0], tm) * w1 + _rows_to_2d(bufs[p][1], tm) * w2)
            if final_norm:
                out = _rms(out, fw_ref[...])
            o_ref[...] = out


def _combine(slots, x, slab, final_w, yb, final_norm, tm=256):
    n = x.shape[0]
    n_tiles = n // tm
    row = lambda i: (i, 0)
    row_buf = pltpu.VMEM((tm * ROW_SUB, LANES), F32)
    return pl.pallas_call(
        functools.partial(_combine_kernel, tm=tm, n_tiles=n_tiles, final_norm=final_norm),
        grid=(n_tiles,),
        in_specs=[
            pl.BlockSpec((tm,), lambda i: (i,), memory_space=pltpu.SMEM),
            pl.BlockSpec((tm,), lambda i: (i,), memory_space=pltpu.SMEM),
            pl.BlockSpec((tm,), lambda i: (jnp.minimum(i + 1, n_tiles - 1),), memory_space=pltpu.SMEM),
            pl.BlockSpec((tm,), lambda i: (jnp.minimum(i + 1, n_tiles - 1),), memory_space=pltpu.SMEM),
            pl.BlockSpec((tm, D_MODEL), row),
            pl.BlockSpec((tm, ROUTER_LANES), row),
            pl.BlockSpec((1, D_MODEL), lambda i: (0, 0)),
            pl.BlockSpec(memory_space=pl.ANY),
        ],
        out_specs=pl.BlockSpec((tm, D_MODEL), row),
        out_shape=jax.ShapeDtypeStruct((n, D_MODEL), F32),
        scratch_shapes=[row_buf, row_buf, row_buf, row_buf, pltpu.SemaphoreType.DMA((2,))],
        compiler_params=_cparams("arbitrary"),
        name="moe_combine",
    )(slots[0], slots[1], slots[0], slots[1], x, slab, final_w, yb)


class MoePlan(NamedTuple):
    slot: tuple
    block_expert: jax.Array
    n_used: jax.Array
    parity: jax.Array
    next_expert: jax.Array
    cap: int


def _moe_plan(plan_t, counts_row, n_tokens):
    eid = plan_t[0:TOP_K].astype(jnp.int32)
    rank = plan_t[4:4 + TOP_K].astype(jnp.int32)
    counts = counts_row[EXPERT_LANE0:EXPERT_LANE0 + N_EXPERTS].astype(jnp.int32)
    padded = (counts + MOE_ROWS - 1) // MOE_ROWS * MOE_ROWS
    pad_ends = jnp.cumsum(padded)
    pad_starts = pad_ends - padded
    expert = jnp.arange(N_EXPERTS, dtype=jnp.int32)[:, None, None]
    region = jnp.sum(jnp.where(eid[None] == expert, pad_starts[:, None, None], 0), axis=0)
    slot = region + rank
    nblk = n_tokens * TOP_K // MOE_ROWS + N_EXPERTS
    block_start = jnp.arange(nblk, dtype=jnp.int32) * MOE_ROWS
    block_expert = jnp.minimum(
        jnp.sum((pad_ends[None, :] <= block_start[:, None]).astype(jnp.int32), axis=1),
        N_EXPERTS - 1).astype(jnp.int32)
    n_used = (pad_ends[-1:] // MOE_ROWS).astype(jnp.int32)
    blk = jnp.arange(nblk, dtype=jnp.int32)
    prev_expert = jnp.concatenate([block_expert[:1], block_expert[:-1]])
    first = (blk < n_used[0]) & ((blk == 0) | (block_expert != prev_expert))
    parity = ((jnp.cumsum(first.astype(jnp.int32)) - 1) % 2).astype(jnp.int32)
    later_first = first[None, :] & (blk[None, :] > blk[:, None])
    next_first = jnp.min(jnp.where(later_first, blk[None, :], nblk), axis=1)
    next_expert = jnp.where(next_first < nblk, block_expert[jnp.minimum(next_first, nblk - 1)], -1)
    return MoePlan(tuple(slot[j] for j in range(TOP_K)), block_expert, n_used, parity,
                   next_expert.astype(jnp.int32), nblk * MOE_ROWS)


def kernel(x, attn_norm_w, w_in, shift_mix, attn_sinks, rwkv_w0, rwkv_w_up, rwkv_a0, rwkv_a_up, rwkv_g_up, rwkv_k_k, rwkv_k_a, rwkv_r_k, rwkv_ln_w, rwkv_ln_b, vres_down, vres_mix, vres_up, vres_v0, w_branch_a, w_branch_b, w_out, ffn_norm_w, router_group_w, router_group_b, router_expert_w, router_expert_b, expert_w_gate, expert_w_up, expert_w_down, final_norm_w):
    batch, seq, _ = x.shape
    n = batch * seq
    depth = w_in.shape[0]
    xf = x.reshape(n, D_MODEL)
    w_in_bf = jnp.swapaxes(w_in, 1, 2).astype(BF16)

    zero_c = jnp.zeros((depth, RWKV_WIDTH), F32)
    v0_all = jnp.concatenate([zero_c[:1], vres_v0.astype(F32)], axis=0)
    pv_pre_all = jnp.stack([rwkv_w0, rwkv_a0, rwkv_k_k, rwkv_k_a, v0_all, zero_c, zero_c, zero_c], axis=1)
    pv_post_all = jnp.stack([rwkv_ln_w, rwkv_ln_b, rwkv_r_k.reshape(depth, RWKV_WIDTH),
                             zero_c, zero_c, zero_c, zero_c, zero_c], axis=1)
    mix_r_all = shift_mix[:, None, :RKV_COLS]
    mix_l_all = jnp.pad(shift_mix[:, None, RKV_COLS:], ((0, 0), (0, 0), (0, LORA_COLS - LORA_WIDTH)))
    wup_all = jnp.pad(rwkv_w_up.astype(BF16), ((0, 0), (0, LANES - DECAY_LORA), (0, 0)))
    aup_all = jnp.pad(rwkv_a_up.astype(BF16), ((0, 0), (DECAY_LORA, LANES - DECAY_LORA - AAA_LORA), (0, 0)))
    gup_all = jnp.pad(rwkv_g_up.astype(BF16), ((0, 0), (0, 2 * LANES - GATE_LORA), (0, 0)))
    vup_all = jnp.pad(vres_up.astype(BF16), ((1, 0), (0, LANES - MV_LORA), (0, 0)))
    vd_all = jnp.pad(vres_down.astype(BF16), ((0, 0), (0, 0), (0, LANES - MV_LORA)))
    mix_v_all = jnp.pad(vres_mix, ((0, 0), (0, LANES - MV_LORA)))[:, None, :]
    router_pad = ROUTER_LANES - N_GROUPS - N_EXPERTS
    w_router_all = jnp.pad(jnp.concatenate([router_group_w, router_expert_w], axis=2),
                           ((0, 0), (0, 0), (0, router_pad)))
    b_router_all = jnp.pad(jnp.concatenate([router_group_b, router_expert_b], axis=1),
                           ((0, 0), (0, router_pad)))[:, None, :]
    attn_norm_all = attn_norm_w[:, None, :]
    ffn_norm_all = ffn_norm_w[:, None, :]
    w_out_bf = w_out.astype(BF16)
    sinks_all = attn_sinks.astype(F32)

    v_first = None
    slot_backing = None
    slot_rows = (n * TOP_K // MOE_ROWS + N_EXPERTS) * MOE_ROWS
    for i in range(depth):
        has_vres = i > 0
        mix_r, mix_l, pv_pre, pv_post = mix_r_all[i], mix_l_all[i], pv_pre_all[i], pv_post_all[i]
        wup, aup, gup, vup = wup_all[i], aup_all[i], gup_all[i], vup_all[i]
        vd, mix_v = (vd_all[i - 1], mix_v_all[i - 1]) if has_vres else (None, None)
        w_router, b_router = w_router_all[i], b_router_all[i]

        need_zeros = slot_backing is None
        proj = _inproj(i, xf, attn_norm_all[i], w_in_bf, vd,
                       zero_rows=slot_rows * ROW_SUB if need_zeros else 0)
        qkv, rkv, lora, gates = proj[:4]
        zv = proj[4] if has_vres else None
        if need_zeros:
            slot_backing = proj[-1]
        attn = _attention(qkv, sinks_all[i], seq)
        y, v_new = _rwkv_mixer(rkv, lora, v_first, zv, mix_r, mix_l, mix_v, pv_pre,
                               wup, aup, gup, vup, pv_post, seq)
        if not has_vres:
            v_first = v_new
        merged = _merge(i, attn, y, w_branch_a, w_branch_b, gates)

        x_mid, hf, slab, plan_t, counts = _outproj_router(
            merged, w_out_bf[i], xf, ffn_norm_all[i], w_router, b_router)
        plan = _moe_plan(plan_t, counts[0], n)
        xb = _dispatch(plan.slot, hf, plan.cap, slot_backing)
        yb = _expert_ffn(i, plan, xb, expert_w_gate, expert_w_up, expert_w_down)
        slot_backing = yb
        xf = _combine(plan.slot, x_mid, slab, final_norm_w.reshape(1, D_MODEL), yb, final_norm=(i == depth - 1))
    return xf.reshape(batch, seq, D_MODEL)
```

```python
import functools
import math
from typing import NamedTuple

import jax
import jax.numpy as jnp
from jax import lax
from jax.experimental import pallas as pl
from jax.experimental.pallas import tpu as pltpu

F32 = jnp.float32
BF16 = jnp.bfloat16

D_MODEL = 2048
HEAD_DIM = 64
N_Q_HEADS = 16
N_KV_HEADS = 4
GQA_GROUP = N_Q_HEADS // N_KV_HEADS
ATTN_WIDTH = N_Q_HEADS * HEAD_DIM
KV_WIDTH = N_KV_HEADS * HEAD_DIM
WINDOW = 128
ATTN_BLOCK = 128
ATTN_SUB = 2

RWKV_HEAD = 64
N_RWKV_HEADS = 16
RWKV_WIDTH = N_RWKV_HEADS * RWKV_HEAD
DECAY_LORA = 64
AAA_LORA = 64
MV_LORA = 32
GATE_LORA = 160
RWKV_GN_EPS = 64e-5
DECAY_SCALE = math.exp(-0.5)
KK_NORM_FLOOR = 1e-12
LORA_WIDTH = DECAY_LORA + AAA_LORA + GATE_LORA

N_GROUPS = 4
EXPERTS_PER_GROUP = 8
N_EXPERTS = N_GROUPS * EXPERTS_PER_GROUP
TOP_K = 2
EXPERT_HIDDEN = D_MODEL // 4
NORM_EPS = 1e-5

LANES = 128
SUBLANES = 8
MXU_COLS = 256

SEG = 512
QKV_COLS = ATTN_WIDTH + 2 * KV_WIDTH
RKV_COLS = 3 * RWKV_WIDTH
LORA_COLS = SEG
GATE_COLS = 2 * D_MODEL
QKV_TILES = QKV_COLS // SEG
RKV_TILES = RKV_COLS // SEG
LORA_TILES = LORA_COLS // SEG
GATE_TILES = GATE_COLS // SEG
IN_TILES = QKV_TILES + RKV_TILES + LORA_TILES + GATE_TILES
FIRST_GATE_TILE = QKV_TILES + RKV_TILES + LORA_TILES
ZERO_STEPS = 16
GATE_COL0 = QKV_COLS + RKV_COLS + LORA_WIDTH

SCAN_CHUNK = 64
SCAN_SUB = 2
SCAN_GROUP = 4
MOE_ROWS = 256
ROUTER_LANES = LANES
ROUTER_ROWS = 128
EXPERT_LANE0 = N_GROUPS

ROW_SUB = D_MODEL // LANES

VMEM_LIMIT = 56 * 1024 * 1024


def _rows_to_2d(slab_ref, rows):
    return jnp.concatenate([slab_ref[pl.ds(s, rows, stride=ROW_SUB), :] for s in range(ROW_SUB)], axis=1)


def _store_rows(slab_ref, val):
    rows = val.shape[0]
    for s in range(ROW_SUB):
        slab_ref[pl.ds(s, rows, stride=ROW_SUB), :] = val[:, s * LANES:(s + 1) * LANES]


def _slab(ref, row):
    return ref.at[pl.ds(pl.multiple_of(row * ROW_SUB, ROW_SUB), ROW_SUB)]


def _cparams(*sem):
    return pltpu.CompilerParams(dimension_semantics=sem, vmem_limit_bytes=VMEM_LIMIT)


def _bdot(a, b):
    return jnp.dot(a.astype(BF16), b.astype(BF16), preferred_element_type=F32)


def _bdot_nt(a, b):
    return lax.dot_general(a.astype(BF16), b.astype(BF16), (((1,), (1,)), ((), ())),
                           preferred_element_type=F32)


def _bdot_tn(a, b):
    return lax.dot_general(a.astype(BF16), b.astype(BF16), (((0,), (0,)), ((), ())),
                           preferred_element_type=F32)


def _dot_split_rhs(w01, x, passes):
    acc, rem = None, x
    for p in range(passes):
        part = rem.astype(BF16)
        d = jnp.dot(w01, part, preferred_element_type=F32)
        acc = d if acc is None else acc + d
        if p + 1 < passes:
            rem = rem - part.astype(F32)
    return acc


def _dot_split_lhs(x, w01, passes):
    acc, rem = None, x
    for p in range(passes):
        part = rem.astype(BF16)
        d = jnp.dot(part, w01, preferred_element_type=F32)
        acc = d if acc is None else acc + d
        if p + 1 < passes:
            rem = rem - part.astype(F32)
    return acc


def _rms(x, w):
    ms = jnp.mean(x * x, axis=-1, keepdims=True)
    return x * lax.rsqrt(ms + NORM_EPS) * w


def _inproj_kernel(*refs, has_vres, has_zeros):
    refs = list(refs)
    x_ref, nw_ref, w_ref, wg_ref = refs[:4]
    refs = refs[4:]
    vd_ref = refs.pop(0) if has_vres else None
    qkv_ref, rkv_ref, lora_ref, gate_ref = refs[:4]
    refs = refs[4:]
    zv_ref = refs.pop(0) if has_vres else None
    zeros_ref = refs.pop(0) if has_zeros else None
    (h_scr,) = refs
    j = pl.program_id(1)

    if has_zeros:
        zeros_ref[...] = jnp.zeros_like(zeros_ref)

    @pl.when(j == 0)
    def _():
        h_scr[...] = _rms(x_ref[...], nw_ref[...]).astype(BF16)
        if has_vres:
            zv_ref[...] = jnp.dot(h_scr[...], vd_ref[...], preferred_element_type=F32)

    def project(wt_ref, out_ref):
        for c in range(SEG // MXU_COLS):
            cs = slice(c * MXU_COLS, (c + 1) * MXU_COLS)
            acc = lax.dot_general(h_scr[...], wt_ref[cs, :], (((1,), (1,)), ((), ())),
                                  preferred_element_type=F32)
            out_ref[:, cs] = acc.astype(out_ref.dtype)

    @pl.when(j < QKV_TILES)
    def _():
        project(w_ref, qkv_ref)

    @pl.when((j >= QKV_TILES) & (j < QKV_TILES + RKV_TILES))
    def _():
        project(w_ref, rkv_ref)

    @pl.when(j == QKV_TILES + RKV_TILES)
    def _():
        project(w_ref, lora_ref)

    @pl.when(j >= FIRST_GATE_TILE)
    def _():
        project(wg_ref.at[0], gate_ref)


def _inproj(layer, x, norm_w, w_in, vd, zero_rows=0, tm=1024):
    n = x.shape[0]
    has_vres = vd is not None
    s_rkv = QKV_TILES
    in_specs = [
        pl.BlockSpec((tm, D_MODEL), lambda i, j: (i, 0)),
        pl.BlockSpec((1, D_MODEL), lambda i, j: (0, 0)),
        pl.BlockSpec((None, SEG, D_MODEL), lambda i, j: (layer, jnp.minimum(j, FIRST_GATE_TILE - 1), 0)),
        pl.BlockSpec((pl.Element(1), pl.Element(SEG), pl.Element(D_MODEL)),
                     lambda i, j: (layer, pl.multiple_of(
                         GATE_COL0 + SEG * jnp.clip(j - FIRST_GATE_TILE, 0, GATE_TILES - 1), 2 * SUBLANES), 0)),
    ]
    args = [x, norm_w, w_in, w_in]
    out_specs = [
        pl.BlockSpec((tm, SEG), lambda i, j: (i, jnp.clip(j, 0, QKV_TILES - 1))),
        pl.BlockSpec((tm, SEG), lambda i, j: (i, jnp.clip(j - s_rkv, 0, RKV_TILES - 1))),
        pl.BlockSpec((tm, SEG), lambda i, j: (i, 0)),
        pl.BlockSpec((tm, SEG), lambda i, j: (i, jnp.clip(j - FIRST_GATE_TILE, 0, GATE_TILES - 1))),
    ]
    out_shape = [
        jax.ShapeDtypeStruct((n, QKV_COLS), BF16),
        jax.ShapeDtypeStruct((n, RKV_COLS), F32),
        jax.ShapeDtypeStruct((n, LORA_COLS), F32),
        jax.ShapeDtypeStruct((n, GATE_COLS), BF16),
    ]
    if has_vres:
        in_specs.append(pl.BlockSpec((D_MODEL, LANES), lambda i, j: (0, 0)))
        args.append(vd)
        out_specs.append(pl.BlockSpec((tm, LANES), lambda i, j: (i, 0)))
        out_shape.append(jax.ShapeDtypeStruct((n, LANES), F32))
    if zero_rows:
        blk = zero_rows // ((n // tm) * ZERO_STEPS)
        assert blk * (n // tm) * ZERO_STEPS == zero_rows and blk % SUBLANES == 0
        out_specs.append(pl.BlockSpec((blk, LANES), lambda i, j: (i * ZERO_STEPS + jnp.minimum(j, ZERO_STEPS - 1), 0)))
        out_shape.append(jax.ShapeDtypeStruct((zero_rows, LANES), F32))
    return pl.pallas_call(
        functools.partial(_inproj_kernel, has_vres=has_vres, has_zeros=bool(zero_rows)),
        grid=(n // tm, IN_TILES),
        in_specs=in_specs,
        out_specs=out_specs,
        out_shape=out_shape,
        scratch_shapes=[pltpu.VMEM((tm, D_MODEL), BF16)],
        compiler_params=_cparams("parallel", "arbitrary"),
        name="inproj",
    )(*args)


def _attn_kernel(sink_ref, q_ref, kp_ref, kc_ref, vp_ref, vc_ref, bias_ref, o_ref, *, tiles_per_seq):
    first = (pl.program_id(0) % tiles_per_seq) == 0
    col = lax.broadcasted_iota(jnp.int32, (ATTN_BLOCK, 2 * ATTN_BLOCK), 1)
    pad_keys = jnp.logical_and(first, col < ATTN_BLOCK)
    scale = HEAD_DIM ** -0.5
    items = [(sb, hk) for sb in range(ATTN_SUB) for hk in range(N_KV_HEADS)]

    def window(prev_ref, cur_ref, sb, hk):
        ks = slice(hk * HEAD_DIM, (hk + 1) * HEAD_DIM)
        rows = slice(sb * ATTN_BLOCK, (sb + 1) * ATTN_BLOCK)
        prev = prev_ref[:, ks] if sb == 0 else cur_ref[(sb - 1) * ATTN_BLOCK:sb * ATTN_BLOCK, ks]
        return jnp.concatenate([prev, cur_ref[rows, ks]], axis=0)

    scores = {}
    for sb, hk in items:
        kw = window(kp_ref, kc_ref, sb, hk)
        for g in range(GQA_GROUP):
            h = hk * GQA_GROUP + g
            qh = q_ref[sb * ATTN_BLOCK:(sb + 1) * ATTN_BLOCK, h * HEAD_DIM:(h + 1) * HEAD_DIM]
            s = lax.dot_general(qh, kw, (((1,), (1,)), ((), ())), preferred_element_type=F32)
            s = s * scale + bias_ref[h]
            if sb == 0:
                s = jnp.where(pad_keys, -jnp.inf, s)
            scores[(sb, h)] = s
    row_max = {key: jnp.max(s, axis=-1, keepdims=True) for key, s in scores.items()}
    m = {key: jnp.maximum(row_max[key], sink_ref[key[1]]) for key in scores}
    p = {key: jnp.exp(scores[key] - m[key]) for key in scores}
    row_sum = {key: jnp.sum(p[key], axis=-1, keepdims=True) for key in scores}
    denoms = {key: row_sum[key] + jnp.exp(sink_ref[key[1]] - m[key]) for key in scores}
    probs = {key: p[key].astype(BF16) for key in scores}
    for sb, hk in items:
        vw = window(vp_ref, vc_ref, sb, hk)
        for g in range(GQA_GROUP):
            h = hk * GQA_GROUP + g
            o = jnp.dot(probs[(sb, h)], vw, preferred_element_type=F32) / denoms[(sb, h)]
            o_ref[sb * ATTN_BLOCK:(sb + 1) * ATTN_BLOCK, h * HEAD_DIM:(h + 1) * HEAD_DIM] = o.astype(BF16)


def _attn_bias():
    qi = jnp.arange(ATTN_BLOCK)[:, None]
    kj = jnp.arange(2 * ATTN_BLOCK)[None, :]
    dist = qi + ATTN_BLOCK - kj
    valid = (dist >= 0) & (dist < WINDOW)
    slopes = jnp.exp2(-8.0 * jnp.arange(1, N_Q_HEADS + 1, dtype=F32) / N_Q_HEADS)
    bias = -slopes[:, None, None] * dist.astype(F32)[None]
    return jnp.where(valid[None], bias, -jnp.inf)


def _attention(qkv, sinks, seq):
    n = qkv.shape[0]
    rows = ATTN_SUB * ATTN_BLOCK
    tiles_per_seq = seq // rows
    kcol = ATTN_WIDTH // KV_WIDTH
    prev = lambda i: jnp.maximum(i * ATTN_SUB - 1, 0)
    return pl.pallas_call(
        functools.partial(_attn_kernel, tiles_per_seq=tiles_per_seq),
        grid=(n // rows,),
        in_specs=[
            pl.BlockSpec(memory_space=pltpu.SMEM),
            pl.BlockSpec((rows, ATTN_WIDTH), lambda i: (i, 0)),
            pl.BlockSpec((ATTN_BLOCK, KV_WIDTH), lambda i: (prev(i), kcol)),
            pl.BlockSpec((rows, KV_WIDTH), lambda i: (i, kcol)),
            pl.BlockSpec((ATTN_BLOCK, KV_WIDTH), lambda i: (prev(i), kcol + 1)),
            pl.BlockSpec((rows, KV_WIDTH), lambda i: (i, kcol + 1)),
            pl.BlockSpec((N_Q_HEADS, ATTN_BLOCK, 2 * ATTN_BLOCK), lambda i: (0, 0, 0)),
        ],
        out_specs=pl.BlockSpec((rows, ATTN_WIDTH), lambda i: (i, 0)),
        out_shape=jax.ShapeDtypeStruct((n, ATTN_WIDTH), BF16),
        compiler_params=_cparams("parallel"),
        name="swa_attention",
    )(sinks, qkv, qkv, qkv, qkv, qkv, _attn_bias())


def _head_ones():
    ch = jnp.arange(MXU_COLS) // RWKV_HEAD
    return (ch[:, None] == ch[None, :]).astype(BF16)


def _head_sums(x, ones, passes):
    parts = [_dot_split_lhs(x[:, c:c + MXU_COLS], ones, passes) for c in range(0, RWKV_WIDTH, MXU_COLS)]
    return jnp.concatenate(parts, axis=1)


def _rwkv_kernel(*refs, has_vres):
    refs = list(refs)
    rkv_ref, lo_ref = refs[:2]
    refs = refs[2:]
    if has_vres:
        vf_ref, zv_ref, mixv_ref = refs[:3]
        refs = refs[3:]
    mixr_ref, mixl_ref, pv_ref, wup_ref, aup_ref, gup_ref, vup_ref, ones_ref, pp_ref = refs[:9]
    refs = refs[9:]
    y_ref = refs.pop(0)
    v_out = None if has_vres else refs.pop(0)
    s_scr, prev_rkv, prev_lo = refs[:3]
    prev_zv = refs[3] if has_vres else None

    L = SCAN_CHUNK
    W = SCAN_GROUP * RWKV_HEAD
    n_seq, step_rows = rkv_ref.shape[0], rkv_ref.shape[1]
    n_grp = N_RWKV_HEADS // SCAN_GROUP
    ones = ones_ref[...]

    @pl.when(pl.program_id(0) == 0)
    def _():
        s_scr[...] = jnp.zeros_like(s_scr)
        prev_rkv[...] = jnp.zeros_like(prev_rkv)
        prev_lo[...] = jnp.zeros_like(prev_lo)
        if has_vres:
            prev_zv[...] = jnp.zeros_like(prev_zv)

    def lerp_shift(z_ref, carry_ref, b, mix):
        z = z_ref[b]
        row = lax.broadcasted_iota(jnp.int32, z.shape, 0)
        back = jnp.where(row == 0, carry_ref[b, SUBLANES - 1:SUBLANES, :], pltpu.roll(z, 1, 0))
        carry_ref[b] = z_ref[b, step_rows - SUBLANES:step_rows, :]
        return z + (back - z) * mix

    w0, a0, k_k, k_a, v0 = (pv_ref[i:i + 1, :] for i in range(5))
    pre = []
    for b in range(n_seq):
        zs = lerp_shift(rkv_ref, prev_rkv, b, mixr_ref[...])
        los = lerp_shift(lo_ref, prev_lo, b, mixl_ref[...])
        r = zs[:, :RWKV_WIDTH]
        kr = zs[:, RWKV_WIDTH:2 * RWKV_WIDTH]
        vr = zs[:, 2 * RWKV_WIDTH:]
        wa = los[:, :LANES]
        col = lax.broadcasted_iota(jnp.int32, wa.shape, 1)
        wa_in = jnp.where(col < DECAY_LORA, jnp.tanh(wa), wa).astype(BF16)
        gd = los[:, LANES:3 * LANES]
        col_g = lax.broadcasted_iota(jnp.int32, gd.shape, 1)
        g_in = jnp.where(col_g < GATE_LORA, jax.nn.sigmoid(gd), gd).astype(BF16)
        dw = jnp.dot(wa_in, wup_ref[...], preferred_element_type=F32)
        da = jnp.dot(wa_in, aup_ref[...], preferred_element_type=F32)
        g = jnp.dot(g_in, gup_ref[...], preferred_element_type=F32)
        lw = -DECAY_SCALE * jax.nn.sigmoid(w0 + dw)
        a = jax.nn.sigmoid(a0 + da)
        if has_vres:
            zvs = lerp_shift(zv_ref, prev_zv, b, mixv_ref[...])
            dv = jnp.dot(zvs.astype(BF16), vup_ref[...], preferred_element_type=F32)
            vr = vr + (vf_ref[b] - vr) * jax.nn.sigmoid(v0 + dv)
        else:
            v_out[b] = vr
        kk0 = kr * k_k
        ss = _head_sums(kk0 * kk0, ones, 2)
        kk = kk0 * lax.rsqrt(jnp.maximum(ss, KK_NORM_FLOOR ** 2))
        pre.append(dict(r=r, lw=lw, k=kr * (1.0 + (a - 1.0) * k_a), v=vr, al=-kk, be=kk * a, g=g))

    row = lax.broadcasted_iota(jnp.int32, (L, L), 0)
    col = lax.broadcasted_iota(jnp.int32, (L, L), 1)
    tri = (row >= col).astype(BF16)
    t_w = lax.broadcasted_iota(jnp.int32, (L, W), 0)
    lane_w = lax.broadcasted_iota(jnp.int32, (L, W), 1)
    s_w = lane_w % RWKV_HEAD
    strict = t_w > s_w
    incl = t_w >= s_w
    eye = (t_w == s_w).astype(F32)
    blk_w = lane_w // RWKV_HEAD
    bd_mask = (lax.broadcasted_iota(jnp.int32, (W, W), 0) // RWKV_HEAD
               == lax.broadcasted_iota(jnp.int32, (W, W), 1) // RWKV_HEAD)

    def blockdiag(w):
        tiled = jnp.concatenate([w.astype(BF16)] * SCAN_GROUP, axis=0)
        return jnp.where(bd_mask, tiled, jnp.zeros_like(tiled))

    ln_w, ln_b, r_k = (pp_ref[i:i + 1, :] for i in range(3))
    items = [(b, g) for b in range(n_seq) for g in range(n_grp)]
    for sub in range(step_rows // L):
        rows = slice(sub * L, (sub + 1) * L)
        ops = {}
        for b in range(n_seq):
            p = pre[b]
            lw = p['lw'][rows]
            cum = _dot_split_rhs(tri, lw, 2)
            p_inc = jnp.exp(cum)
            p_exc = jnp.exp(cum - lw)
            p_inv = jnp.exp(-cum)
            p_last = p_inc[L - 1:L, :]
            a_t = p['al'][rows] * p_exc
            r_t = p['r'][rows] * p_inc
            b_t = p['be'][rows] * p_inv
            k_t = p['k'][rows] * p_inv
            b_end = b_t * p_last
            k_end = k_t * p_last
            v = p['v'][rows]
            for g in range(n_grp):
                gs = slice(g * W, (g + 1) * W)
                ops[(b, g)] = dict(
                    ar=jnp.concatenate([a_t[:, gs], r_t[:, gs]], axis=0),
                    bt=b_t[:, gs], kt=k_t[:, gs], v=v[:, gs], p_last=p_last[:, gs],
                    bke=jnp.concatenate([b_end[:, gs], k_end[:, gs]], axis=0))

        gm, xs, av, s0 = {}, {}, {}, {}
        for it in items:
            o = ops[it]
            rhs = jnp.concatenate([blockdiag(o['bt']), blockdiag(o['kt'])], axis=0)
            gm[it] = _bdot_nt(o['ar'], rhs)
            s0[it] = s_scr[it[0], it[1]]
            xs[it] = _bdot_nt(o['ar'], blockdiag(s0[it]))
        a_ab, a_rb = {}, {}
        for it in items:
            g = gm[it]
            a_ab[it] = jnp.where(strict, g[:L, :W], 0.0)
            a_rb[it] = jnp.where(incl, g[L:, :W], 0.0)
            a_kk = jnp.concatenate([jnp.where(strict, g[:L, W:], 0.0), jnp.where(incl, g[L:, W:], 0.0)], axis=0)
            av[it] = _bdot(a_kk, blockdiag(ops[it]['v']))

        pw = {it: _bdot(a_ab[it], blockdiag(a_ab[it])) for it in items}
        tinv = {it: eye + a_ab[it] for it in items}
        span = 2
        while 2 * span < L:
            both = {it: _bdot(jnp.concatenate([pw[it], tinv[it]], axis=0), blockdiag(pw[it])) for it in items}
            tinv = {it: tinv[it] + both[it][L:] for it in items}
            pw = {it: both[it][:L] for it in items}
            span *= 2
        tinv = {it: tinv[it] + _bdot(tinv[it], blockdiag(pw[it])) for it in items}

        u = {it: _bdot(tinv[it], blockdiag(xs[it][:L] + av[it][:L])) for it in items}
        y = {it: xs[it][L:] + av[it][L:] + _bdot(a_rb[it], blockdiag(u[it])) for it in items}
        for it in items:
            o = ops[it]
            full = _bdot_tn(jnp.concatenate([u[it], o['v']], axis=0), o['bke'])
            upd = full[:RWKV_HEAD]
            for h in range(1, SCAN_GROUP):
                upd = jnp.where(blk_w == h, full[h * RWKV_HEAD:(h + 1) * RWKV_HEAD], upd)
            s_scr[it[0], it[1]] = s0[it] * o['p_last'] + upd

        inv_n = 1.0 / RWKV_HEAD
        for b in range(n_seq):
            p = pre[b]
            yb = jnp.concatenate([y[(b, g)] for g in range(n_grp)], axis=1)
            d = yb - _head_sums(yb, ones, 1) * inv_n
            var = _head_sums(d * d, ones, 1) * inv_n
            yn = d * lax.rsqrt(var + RWKV_GN_EPS) * ln_w + ln_b
            bonus = _head_sums(p['r'][rows] * p['k'][rows] * r_k, ones, 1) * p['v'][rows]
            y_ref[b, rows, :] = ((yn + bonus) * p['g'][rows]).astype(BF16)


def _rwkv_mixer(rkv, lora, v_first, zv, mix_r, mix_l, mix_v, pv_pre, wup, aup, gup, vup, pv_post, seq):
    n = rkv.shape[0]
    n_seq = n // seq
    has_vres = v_first is not None
    step_rows = SCAN_SUB * SCAN_CHUNK
    timed = lambda cols: pl.BlockSpec((n_seq, step_rows, cols), lambda t: (0, t, 0))
    const = lambda shape: pl.BlockSpec(shape, lambda t: (0, 0))
    by_seq = lambda a: a.reshape(n_seq, seq, a.shape[-1])
    in_specs = [timed(RKV_COLS), timed(LORA_COLS)]
    args = [by_seq(rkv), by_seq(lora)]
    if has_vres:
        in_specs += [timed(RWKV_WIDTH), timed(LANES), const((1, LANES))]
        args += [by_seq(v_first), by_seq(zv), mix_v]
    in_specs += [
        const((1, RKV_COLS)), const((1, LORA_COLS)), const((SUBLANES, RWKV_WIDTH)),
        const((LANES, RWKV_WIDTH)), const((LANES, RWKV_WIDTH)), const((2 * LANES, RWKV_WIDTH)),
        const((LANES, RWKV_WIDTH)), const((MXU_COLS, MXU_COLS)), const((SUBLANES, RWKV_WIDTH)),
    ]
    args += [mix_r, mix_l, pv_pre, wup, aup, gup, vup, _head_ones(), pv_post]
    out_specs = [timed(RWKV_WIDTH)]
    out_shape = [jax.ShapeDtypeStruct((n_seq, seq, RWKV_WIDTH), BF16)]
    if not has_vres:
        out_specs.append(timed(RWKV_WIDTH))
        out_shape.append(jax.ShapeDtypeStruct((n_seq, seq, RWKV_WIDTH), F32))
    scratch = [
        pltpu.VMEM((n_seq, N_RWKV_HEADS // SCAN_GROUP, RWKV_HEAD, SCAN_GROUP * RWKV_HEAD), F32),
        pltpu.VMEM((n_seq, SUBLANES, RKV_COLS), F32),
        pltpu.VMEM((n_seq, SUBLANES, LORA_COLS), F32),
    ]
    if has_vres:
        scratch.append(pltpu.VMEM((n_seq, SUBLANES, LANES), F32))
    outs = pl.pallas_call(
        functools.partial(_rwkv_kernel, has_vres=has_vres),
        grid=(seq // step_rows,),
        in_specs=in_specs,
        out_specs=out_specs,
        out_shape=out_shape,
        scratch_shapes=scratch,
        compiler_params=_cparams("arbitrary"),
        name="rwkv_mixer",
    )(*args)
    y = outs[0].reshape(n, RWKV_WIDTH)
    return y, (None if has_vres else outs[1].reshape(n, RWKV_WIDTH))


def _merge_kernel(a_ref, y_ref, wa_ref, wb_ref, ga_ref, gb_ref, o_ref):
    for c in range(o_ref.shape[1] // MXU_COLS):
        cs = slice(c * MXU_COLS, (c + 1) * MXU_COLS)
        pa = jnp.dot(a_ref[...], wa_ref[:, cs].astype(BF16), preferred_element_type=F32)
        pb = jnp.dot(y_ref[...], wb_ref[:, cs].astype(BF16), preferred_element_type=F32)
        ga = jax.nn.sigmoid(ga_ref[:, cs].astype(F32))
        gb = jax.nn.sigmoid(gb_ref[:, cs].astype(F32))
        o_ref[:, cs] = (ga * pa + gb * pb).astype(BF16)


def _merge(layer, attn, y, wa, wb, gates, tm=1024, tn=1024):
    n = attn.shape[0]
    nj = D_MODEL // tn
    return pl.pallas_call(
        _merge_kernel,
        grid=(n // tm, nj),
        in_specs=[
            pl.BlockSpec((tm, ATTN_WIDTH), lambda i, j: (i, 0)),
            pl.BlockSpec((tm, RWKV_WIDTH), lambda i, j: (i, 0)),
            pl.BlockSpec((None, ATTN_WIDTH, tn), lambda i, j: (layer, 0, j)),
            pl.BlockSpec((None, RWKV_WIDTH, tn), lambda i, j: (layer, 0, j)),
            pl.BlockSpec((tm, tn), lambda i, j: (i, j)),
            pl.BlockSpec((tm, tn), lambda i, j: (i, j + nj)),
        ],
        out_specs=pl.BlockSpec((tm, tn), lambda i, j: (i, j)),
        out_shape=jax.ShapeDtypeStruct((n, D_MODEL), BF16),
        compiler_params=_cparams("parallel", "arbitrary"),
        name="gated_merge",
    )(attn, y, wa, wb, gates, gates)


def _outproj_router_kernel(m_ref, wo_ref, x_ref, nw_ref, wr_ref, br_ref,
                           xo_ref, hf_ref, slab_ref, plan_ref, cnt_ref, run_scr):
    @pl.when(pl.program_id(0) == 0)
    def _():
        run_scr[...] = jnp.zeros_like(run_scr)

    tm = x_ref.shape[0]
    sq = jnp.zeros((tm, LANES), F32)
    for c in range(D_MODEL // MXU_COLS):
        cs = slice(c * MXU_COLS, (c + 1) * MXU_COLS)
        xn_c = x_ref[:, cs] + jnp.dot(m_ref[...], wo_ref[:, cs], preferred_element_type=F32)
        xo_ref[:, cs] = xn_c
        for l in range(MXU_COLS // LANES):
            piece = xn_c[:, l * LANES:(l + 1) * LANES]
            sq = sq + piece * piece
    ms = jnp.sum(sq, axis=-1, keepdims=True) * (1.0 / D_MODEL)
    hf = xo_ref[...] * lax.rsqrt(ms + NORM_EPS) * nw_ref[...]
    _store_rows(hf_ref, hf)

    h_hi = hf.astype(BF16)
    h_lo = (hf - h_hi.astype(F32)).astype(BF16)
    wr = wr_ref[...]
    w_hi = wr.astype(BF16)
    w_lo = (wr - w_hi.astype(F32)).astype(BF16)
    lg = (jnp.dot(h_hi, w_hi, preferred_element_type=F32)
          + jnp.dot(h_hi, w_lo, preferred_element_type=F32)
          + jnp.dot(h_lo, w_hi, preferred_element_type=F32)) + br_ref[...]

    chunks = range(tm // ROUTER_ROWS)
    lgs = [lg[c * ROUTER_ROWS:(c + 1) * ROUTER_ROWS] for c in chunks]
    lane = lax.broadcasted_iota(jnp.int32, (ROUTER_ROWS, ROUTER_LANES), 1).astype(F32)
    neg = -jnp.inf
    big = float(ROUTER_LANES)
    is_group = lane < N_GROUPS

    def row_max(vals):
        return [jnp.max(v, axis=-1, keepdims=True) for v in vals]

    def first_index(vals, mx):
        return [jnp.min(jnp.where(v == m, lane, big), axis=-1, keepdims=True) for v, m in zip(vals, mx)]

    def row_sum(vals):
        return [jnp.sum(v, axis=-1, keepdims=True) for v in vals]

    gl = [jnp.where(is_group, v, neg) for v in lgs]
    gmax = row_max(gl)
    gsel = first_index(gl, gmax)
    gsum = row_sum([jnp.where(is_group, jnp.exp(v - m), 0.0) for v, m in zip(lgs, gmax)])
    lo_lane = [EXPERT_LANE0 + EXPERTS_PER_GROUP * g for g in gsel]
    el = [jnp.where((lane >= lo) & (lane < lo + EXPERTS_PER_GROUP), v, neg) for v, lo in zip(lgs, lo_lane)]
    v1 = row_max(el)
    i1 = first_index(el, v1)
    el2 = [jnp.where(lane == i, neg, v) for v, i in zip(el, i1)]
    v2 = row_max(el2)
    i2 = first_index(el2, v2)
    oh1 = [lane == i for i in i1]
    oh2 = [lane == i for i in i2]

    cnt = jnp.concatenate([a.astype(F32) + b.astype(F32) for a, b in zip(oh1, oh2)], axis=0)
    r_i = lax.broadcasted_iota(jnp.int32, (tm, tm), 0)
    c_i = lax.broadcasted_iota(jnp.int32, (tm, tm), 1)
    before = jnp.dot((r_i > c_i).astype(BF16), cnt.astype(BF16), preferred_element_type=F32)
    tot = before + run_scr[0:1, :]
    tots = [tot[c * ROUTER_ROWS:(c + 1) * ROUTER_ROWS] for c in chunks]
    rank1 = row_sum([jnp.where(o, t, 0.0) for o, t in zip(oh1, tots)])
    rank2 = row_sum([jnp.where(o, t, 0.0) for o, t in zip(oh2, tots)])
    run = run_scr[0:1, :] + jnp.sum(cnt, axis=0, keepdims=True)
    run_scr[...] = jnp.broadcast_to(run, run_scr.shape)
    cnt_ref[...] = jnp.broadcast_to(run, cnt_ref.shape)

    for c in chunks:
        gp = 1.0 / gsum[c]
        e21 = jnp.exp(v2[c] - v1[c])
        ew1 = gp / (1.0 + e21)
        ew2 = gp * e21 / (1.0 + e21)
        slab = jnp.where(lane == 0, i1[c] - EXPERT_LANE0,
               jnp.where(lane == 1, i2[c] - EXPERT_LANE0,
               jnp.where(lane == 2, ew1,
               jnp.where(lane == 3, ew2,
               jnp.where(lane == 4, rank1[c],
               jnp.where(lane == 5, rank2[c], 0.0))))))
        slab_ref[c * ROUTER_ROWS:(c + 1) * ROUTER_ROWS, :] = slab
        plan_ref[:, c * ROUTER_ROWS:(c + 1) * ROUTER_ROWS] = jnp.transpose(slab)[:SUBLANES, :]


def _outproj_router(merged, w_out, x, norm_w, w_router, b_router, tm=512):
    n = x.shape[0]
    row = lambda i: (i, 0)
    const = lambda i: (0, 0)
    return pl.pallas_call(
        _outproj_router_kernel,
        grid=(n // tm,),
        in_specs=[
            pl.BlockSpec((tm, D_MODEL), row),
            pl.BlockSpec((D_MODEL, D_MODEL), const),
            pl.BlockSpec((tm, D_MODEL), row),
            pl.BlockSpec((1, D_MODEL), const),
            pl.BlockSpec((D_MODEL, ROUTER_LANES), const),
            pl.BlockSpec((1, ROUTER_LANES), const),
        ],
        out_specs=[
            pl.BlockSpec((tm, D_MODEL), row),
            pl.BlockSpec((tm * ROW_SUB, LANES), row),
            pl.BlockSpec((tm, ROUTER_LANES), row),
            pl.BlockSpec((SUBLANES, tm), lambda i: (0, i)),
            pl.BlockSpec((SUBLANES, ROUTER_LANES), const),
        ],
        out_shape=[
            jax.ShapeDtypeStruct((n, D_MODEL), F32),
            jax.ShapeDtypeStruct((n * ROW_SUB, LANES), F32),
            jax.ShapeDtypeStruct((n, ROUTER_LANES), F32),
            jax.ShapeDtypeStruct((SUBLANES, n), F32),
            jax.ShapeDtypeStruct((SUBLANES, ROUTER_LANES), F32),
        ],
        scratch_shapes=[pltpu.VMEM((SUBLANES, ROUTER_LANES), F32)],
        compiler_params=_cparams("arbitrary"),
        name="outproj_router",
    )(merged, w_out, x, norm_w, w_router, b_router)


def _dispatch_kernel(slot0_ref, slot1_ref, hf_ref, xb_in_hbm, xb_hbm, sem, *, tm):
    del xb_in_hbm

    def copy(u, j):
        return pltpu.make_async_copy(_slab(hf_ref, u), _slab(xb_hbm, (slot0_ref, slot1_ref)[j][u]), sem)

    def start(u, c):
        for j in range(TOP_K):
            copy(u, j).start(priority=j)
        return c

    def wait(u, c):
        for j in range(TOP_K):
            copy(u, j).wait()
        return c

    lax.fori_loop(0, tm, start, 0, unroll=8)
    lax.fori_loop(0, tm, wait, 0, unroll=8)


def _dispatch(slots, hf, cap, backing, tm=256):
    n = hf.shape[0] // ROW_SUB
    assert backing.shape == (cap * ROW_SUB, LANES)
    return pl.pallas_call(
        functools.partial(_dispatch_kernel, tm=tm),
        grid=(n // tm,),
        in_specs=[
            pl.BlockSpec((tm,), lambda i: (i,), memory_space=pltpu.SMEM),
            pl.BlockSpec((tm,), lambda i: (i,), memory_space=pltpu.SMEM),
            pl.BlockSpec((tm * ROW_SUB, LANES), lambda i: (i, 0)),
            pl.BlockSpec(memory_space=pl.ANY),
        ],
        out_specs=pl.BlockSpec(memory_space=pl.ANY),
        out_shape=jax.ShapeDtypeStruct((cap * ROW_SUB, LANES), F32),
        scratch_shapes=[pltpu.SemaphoreType.DMA(())],
        input_output_aliases={3: 0},
        compiler_params=_cparams("arbitrary"),
        name="moe_dispatch",
    )(slots[0], slots[1], hf, backing)


def _ffn_kernel(be_ref, nu_ref, par_ref, nxt_ref, x_ref, wg_hbm, wu_hbm, wd_hbm, o_ref,
                wg_f, wu_f, wd_f, wg_s, wu_s, wd_s, sems, *, layer):
    b = pl.program_id(0)
    used = b < nu_ref[0]
    prev_e = be_ref[jnp.maximum(b - 1, 0)]
    new_expert = jnp.logical_or(b == 0, be_ref[b] != prev_e)

    def fetch(e, p):
        return [pltpu.make_async_copy(src.at[layer, e], dst.at[p], sems.at[p, i])
                for i, (src, dst) in enumerate(((wg_hbm, wg_f), (wu_hbm, wu_f), (wd_hbm, wd_f)))]

    @pl.when(jnp.logical_and(used, new_expert))
    def _():
        for p in range(2):
            @pl.when(par_ref[b] == p)
            def _():
                @pl.when(b == 0)
                def _():
                    for cp in fetch(be_ref[0], p):
                        cp.start()

                @pl.when(nxt_ref[b] >= 0)
                def _():
                    for cp in fetch(nxt_ref[b], 1 - p):
                        cp.start()

                for cp in fetch(be_ref[b], p):
                    cp.wait()
                wg_s[...] = wg_f[p].astype(BF16)
                wu_s[...] = wu_f[p].astype(BF16)
                wd_s[...] = wd_f[p].astype(BF16)

    @pl.when(used)
    def _():
        slabs = MXU_COLS // LANES
        gt = up = None
        for kc in range(D_MODEL // MXU_COLS):
            xk = jnp.concatenate([x_ref[pl.ds(kc * slabs + s, MOE_ROWS, stride=ROW_SUB), :] for s in range(slabs)],
                                 axis=1).astype(BF16)
            ks = slice(kc * MXU_COLS, (kc + 1) * MXU_COLS)
            g_k = jnp.dot(xk, wg_s[ks, :], preferred_element_type=F32)
            u_k = jnp.dot(xk, wu_s[ks, :], preferred_element_type=F32)
            gt = g_k if gt is None else gt + g_k
            up = u_k if up is None else up + u_k
        hid = (gt * jax.nn.sigmoid(gt) * up).astype(BF16)
        for c in range(D_MODEL // MXU_COLS):
            y_c = jnp.dot(hid, wd_s[:, c * MXU_COLS:(c + 1) * MXU_COLS], preferred_element_type=F32)
            for s in range(slabs):
                o_ref[pl.ds(c * slabs + s, MOE_ROWS, stride=ROW_SUB), :] = y_c[:, s * LANES:(s + 1) * LANES]

    @pl.when(jnp.logical_not(used))
    def _():
        o_ref[...] = jnp.zeros_like(o_ref)


def _expert_ffn(layer, plan, xb, w_gate, w_up, w_down):
    nblk = xb.shape[0] // (MOE_ROWS * ROW_SUB)
    hbm = pl.BlockSpec(memory_space=pl.ANY)
    grid_spec = pltpu.PrefetchScalarGridSpec(
        num_scalar_prefetch=4,
        grid=(nblk,),
        in_specs=[
            pl.BlockSpec((MOE_ROWS * ROW_SUB, LANES), lambda b, be, nu, par, nxt: (jnp.minimum(b, nu[0] - 1), 0)),
            hbm, hbm, hbm,
        ],
        out_specs=pl.BlockSpec((MOE_ROWS * ROW_SUB, LANES), lambda b, be, nu, par, nxt: (b, 0)),
        scratch_shapes=[
            pltpu.VMEM((2, D_MODEL, EXPERT_HIDDEN), F32),
            pltpu.VMEM((2, D_MODEL, EXPERT_HIDDEN), F32),
            pltpu.VMEM((2, EXPERT_HIDDEN, D_MODEL), F32),
            pltpu.VMEM((D_MODEL, EXPERT_HIDDEN), BF16),
            pltpu.VMEM((D_MODEL, EXPERT_HIDDEN), BF16),
            pltpu.VMEM((EXPERT_HIDDEN, D_MODEL), BF16),
            pltpu.SemaphoreType.DMA((2, 3)),
        ],
    )
    return pl.pallas_call(
        functools.partial(_ffn_kernel, layer=layer),
        grid_spec=grid_spec,
        out_shape=jax.ShapeDtypeStruct(xb.shape, F32),
        compiler_params=_cparams("arbitrary"),
        name="expert_ffn",
    )(plan.block_expert, plan.n_used, plan.parity, plan.next_expert, xb, w_gate, w_up, w_down)


def _combine_kernel(slot0_ref, slot1_ref, next0_ref, next1_ref, x_ref, slab_ref, fw_ref, yb_hbm, o_ref,
                    buf00, buf01, buf10, buf11, sems, *, tm, n_tiles, final_norm):
    i = pl.program_id(0)
    bufs = ((buf00, buf01), (buf10, buf11))
    cur = (slot0_ref, slot1_ref)

    def copies(slots, p, u):
        return [pltpu.make_async_copy(_slab(yb_hbm, slots[j][u]), _slab(bufs[p][j], u), sems.at[p])
                for j in range(TOP_K)]

    def start_all(slots, p):
        def body(u, c):
            for j, cp in enumerate(copies(slots, p, u)):
                cp.start(priority=j)
            return c
        lax.fori_loop(0, tm, body, 0, unroll=8)

    def wait_all(p):
        def body(u, c):
            for cp in copies(cur, p, u):
                cp.wait()
            return c
        lax.fori_loop(0, tm, body, 0, unroll=8)

    @pl.when(i == 0)
    def _():
        start_all(cur, 0)

    for p in range(2):
        @pl.when(i % 2 == p)
        def _():
            @pl.when(i + 1 < n_tiles)
            def _():
                start_all((next0_ref, next1_ref), 1 - p)

            wait_all(p)
            slab = slab_ref[...]
            w1 = slab[:, 2:3]
            w2 = slab[:, 3:4]
            out = x_ref[...] + (_rows_to_2d(bufs[p][0], tm) * w1 + _rows_to_2d(bufs[p][1], tm) * w2)
            if final_norm:
                out = _rms(out, fw_ref[...])
            o_ref[...] = out


def _combine(slots, x, slab, final_w, yb, final_norm, tm=256):
    n = x.shape[0]
    n_tiles = n // tm
    row = lambda i: (i, 0)
    row_buf = pltpu.VMEM((tm * ROW_SUB, LANES), F32)
    return pl.pallas_call(
        functools.partial(_combine_kernel, tm=tm, n_tiles=n_tiles, final_norm=final_norm),
        grid=(n_tiles,),
        in_specs=[
            pl.BlockSpec((tm,), lambda i: (i,), memory_space=pltpu.SMEM),
            pl.BlockSpec((tm,), lambda i: (i,), memory_space=pltpu.SMEM),
            pl.BlockSpec((tm,), lambda i: (jnp.minimum(i + 1, n_tiles - 1),), memory_space=pltpu.SMEM),
            pl.BlockSpec((tm,), lambda i: (jnp.minimum(i + 1, n_tiles - 1),), memory_space=pltpu.SMEM),
            pl.BlockSpec((tm, D_MODEL), row),
            pl.BlockSpec((tm, ROUTER_LANES), row),
            pl.BlockSpec((1, D_MODEL), lambda i: (0, 0)),
            pl.BlockSpec(memory_space=pl.ANY),
        ],
        out_specs=pl.BlockSpec((tm, D_MODEL), row),
        out_shape=jax.ShapeDtypeStruct((n, D_MODEL), F32),
        scratch_shapes=[row_buf, row_buf, row_buf, row_buf, pltpu.SemaphoreType.DMA((2,))],
        compiler_params=_cparams("arbitrary"),
        name="moe_combine",
    )(slots[0], slots[1], slots[0], slots[1], x, slab, final_w, yb)


class MoePlan(NamedTuple):
    slot: tuple
    block_expert: jax.Array
    n_used: jax.Array
    parity: jax.Array
    next_expert: jax.Array
    cap: int


def _moe_plan(plan_t, counts_row, n_tokens):
    eid = plan_t[0:TOP_K].astype(jnp.int32)
    rank = plan_t[4:4 + TOP_K].astype(jnp.int32)
    counts = counts_row[EXPERT_LANE0:EXPERT_LANE0 + N_EXPERTS].astype(jnp.int32)
    padded = (counts + MOE_ROWS - 1) // MOE_ROWS * MOE_ROWS
    pad_ends = jnp.cumsum(padded)
    pad_starts = pad_ends - padded
    expert = jnp.arange(N_EXPERTS, dtype=jnp.int32)[:, None, None]
    region = jnp.sum(jnp.where(eid[None] == expert, pad_starts[:, None, None], 0), axis=0)
    slot = region + rank
    nblk = n_tokens * TOP_K // MOE_ROWS + N_EXPERTS
    block_start = jnp.arange(nblk, dtype=jnp.int32) * MOE_ROWS
    block_expert = jnp.minimum(
        jnp.sum((pad_ends[None, :] <= block_start[:, None]).astype(jnp.int32), axis=1),
        N_EXPERTS - 1).astype(jnp.int32)
    n_used = (pad_ends[-1:] // MOE_ROWS).astype(jnp.int32)
    blk = jnp.arange(nblk, dtype=jnp.int32)
    prev_expert = jnp.concatenate([block_expert[:1], block_expert[:-1]])
    first = (blk < n_used[0]) & ((blk == 0) | (block_expert != prev_expert))
    parity = ((jnp.cumsum(first.astype(jnp.int32)) - 1) % 2).astype(jnp.int32)
    later_first = first[None, :] & (blk[None, :] > blk[:, None])
    next_first = jnp.min(jnp.where(later_first, blk[None, :], nblk), axis=1)
    next_expert = jnp.where(next_first < nblk, block_expert[jnp.minimum(next_first, nblk - 1)], -1)
    return MoePlan(tuple(slot[j] for j in range(TOP_K)), block_expert, n_used, parity,
                   next_expert.astype(jnp.int32), nblk * MOE_ROWS)


def kernel(x, attn_norm_w, w_in, shift_mix, attn_sinks, rwkv_w0, rwkv_w_up, rwkv_a0, rwkv_a_up, rwkv_g_up, rwkv_k_k, rwkv_k_a, rwkv_r_k, rwkv_ln_w, rwkv_ln_b, vres_down, vres_mix, vres_up, vres_v0, w_branch_a, w_branch_b, w_out, ffn_norm_w, router_group_w, router_group_b, router_expert_w, router_expert_b, expert_w_gate, expert_w_up, expert_w_down, final_norm_w):
    batch, seq, _ = x.shape
    n = batch * seq
    depth = w_in.shape[0]
    xf = x.reshape(n, D_MODEL)
    w_in_bf = jnp.swapaxes(w_in, 1, 2).astype(BF16)

    zero_c = jnp.zeros((depth, RWKV_WIDTH), F32)
    v0_all = jnp.concatenate([zero_c[:1], vres_v0.astype(F32)], axis=0)
    pv_pre_all = jnp.stack([rwkv_w0, rwkv_a0, rwkv_k_k, rwkv_k_a, v0_all, zero_c, zero_c, zero_c], axis=1)
    pv_post_all = jnp.stack([rwkv_ln_w, rwkv_ln_b, rwkv_r_k.reshape(depth, RWKV_WIDTH),
                             zero_c, zero_c, zero_c, zero_c, zero_c], axis=1)
    mix_r_all = shift_mix[:, None, :RKV_COLS]
    mix_l_all = jnp.pad(shift_mix[:, None, RKV_COLS:], ((0, 0), (0, 0), (0, LORA_COLS - LORA_WIDTH)))
    wup_all = jnp.pad(rwkv_w_up.astype(BF16), ((0, 0), (0, LANES - DECAY_LORA), (0, 0)))
    aup_all = jnp.pad(rwkv_a_up.astype(BF16), ((0, 0), (DECAY_LORA, LANES - DECAY_LORA - AAA_LORA), (0, 0)))
    gup_all = jnp.pad(rwkv_g_up.astype(BF16), ((0, 0), (0, 2 * LANES - GATE_LORA), (0, 0)))
    vup_all = jnp.pad(vres_up.astype(BF16), ((1, 0), (0, LANES - MV_LORA), (0, 0)))
    vd_all = jnp.pad(vres_down.astype(BF16), ((0, 0), (0, 0), (0, LANES - MV_LORA)))
    mix_v_all = jnp.pad(vres_mix, ((0, 0), (0, LANES - MV_LORA)))[:, None, :]
    router_pad = ROUTER_LANES - N_GROUPS - N_EXPERTS
    w_router_all = jnp.pad(jnp.concatenate([router_group_w, router_expert_w], axis=2),
                           ((0, 0), (0, 0), (0, router_pad)))
    b_router_all = jnp.pad(jnp.concatenate([router_group_b, router_expert_b], axis=1),
                           ((0, 0), (0, router_pad)))[:, None, :]
    attn_norm_all = attn_norm_w[:, None, :]
    ffn_norm_all = ffn_norm_w[:, None, :]
    w_out_bf = w_out.astype(BF16)
    sinks_all = attn_sinks.astype(F32)

    v_first = None
    slot_backing = None
    slot_rows = (n * TOP_K // MOE_ROWS + N_EXPERTS) * MOE_ROWS
    for i in range(depth):
        has_vres = i > 0
        mix_r, mix_l, pv_pre, pv_post = mix_r_all[i], mix_l_all[i], pv_pre_all[i], pv_post_all[i]
        wup, aup, gup, vup = wup_all[i], aup_all[i], gup_all[i], vup_all[i]
        vd, mix_v = (vd_all[i - 1], mix_v_all[i - 1]) if has_vres else (None, None)
        w_router, b_router = w_router_all[i], b_router_all[i]

        need_zeros = slot_backing is None
        proj = _inproj(i, xf, attn_norm_all[i], w_in_bf, vd,
                       zero_rows=slot_rows * ROW_SUB if need_zeros else 0)
        qkv, rkv, lora, gates = proj[:4]
        zv = proj[4] if has_vres else None
        if need_zeros:
            slot_backing = proj[-1]
        attn = _attention(qkv, sinks_all[i], seq)
        y, v_new = _rwkv_mixer(rkv, lora, v_first, zv, mix_r, mix_l, mix_v, pv_pre,
                               wup, aup, gup, vup, pv_post, seq)
        if not has_vres:
            v_first = v_new
        merged = _merge(i, attn, y, w_branch_a, w_branch_b, gates)

        x_mid, hf, slab, plan_t, counts = _outproj_router(
            merged, w_out_bf[i], xf, ffn_norm_all[i], w_router, b_router)
        plan = _moe_plan(plan_t, counts[0], n)
        xb = _dispatch(plan.slot, hf, plan.cap, slot_backing)
        yb = _expert_ffn(i, plan, xb, expert_w_gate, expert_w_up, expert_w_down)
        slot_backing = yb
        xf = _combine(plan.slot, x_mid, slab, final_norm_w.reshape(1, D_MODEL), yb, final_norm=(i == depth - 1))
    return xf.reshape(batch, seq, D_MODEL)
```

```python
import functools
import math
from typing import NamedTuple

import jax
import jax.numpy as jnp
from jax import lax
from jax.experimental import pallas as pl
from jax.experimental.pallas import tpu as pltpu

F32 = jnp.float32
BF16 = jnp.bfloat16

D_MODEL = 2048
HEAD_DIM = 64
N_Q_HEADS = 16
N_KV_HEADS = 4
GQA_GROUP = N_Q_HEADS // N_KV_HEADS
ATTN_WIDTH = N_Q_HEADS * HEAD_DIM
KV_WIDTH = N_KV_HEADS * HEAD_DIM
WINDOW = 128
ATTN_BLOCK = 128
ATTN_SUB = 2

RWKV_HEAD = 64
N_RWKV_HEADS = 16
RWKV_WIDTH = N_RWKV_HEADS * RWKV_HEAD
DECAY_LORA = 64
AAA_LORA = 64
MV_LORA = 32
GATE_LORA = 160
RWKV_GN_EPS = 64e-5
DECAY_SCALE = math.exp(-0.5)
KK_NORM_FLOOR = 1e-12
LORA_WIDTH = DECAY_LORA + AAA_LORA + GATE_LORA

N_GROUPS = 4
EXPERTS_PER_GROUP = 8
N_EXPERTS = N_GROUPS * EXPERTS_PER_GROUP
TOP_K = 2
EXPERT_HIDDEN = D_MODEL // 4
NORM_EPS = 1e-5

LANES = 128
SUBLANES = 8
MXU_COLS = 256

SEG = 512
QKV_COLS = ATTN_WIDTH + 2 * KV_WIDTH
RKV_COLS = 3 * RWKV_WIDTH
LORA_COLS = SEG
GATE_COLS = 2 * D_MODEL
QKV_TILES = QKV_COLS // SEG
RKV_TILES = RKV_COLS // SEG
LORA_TILES = LORA_COLS // SEG
GATE_TILES = GATE_COLS // SEG
IN_TILES = QKV_TILES + RKV_TILES + LORA_TILES + GATE_TILES
FIRST_GATE_TILE = QKV_TILES + RKV_TILES + LORA_TILES
ZERO_STEPS = 16
GATE_COL0 = QKV_COLS + RKV_COLS + LORA_WIDTH

SCAN_CHUNK = 64
SCAN_SUB = 2
SCAN_GROUP = 4
MOE_ROWS = 256
ROUTER_LANES = LANES
ROUTER_ROWS = 128
EXPERT_LANE0 = N_GROUPS

ROW_SUB = D_MODEL // LANES

VMEM_LIMIT = 56 * 1024 * 1024


def _rows_to_2d(slab_ref, rows):
    return jnp.concatenate([slab_ref[pl.ds(s, rows, stride=ROW_SUB), :] for s in range(ROW_SUB)], axis=1)


def _store_rows(slab_ref, val):
    rows = val.shape[0]
    for s in range(ROW_SUB):
        slab_ref[pl.ds(s, rows, stride=ROW_SUB), :] = val[:, s * LANES:(s + 1) * LANES]


def _slab(ref, row):
    return ref.at[pl.ds(pl.multiple_of(row * ROW_SUB, ROW_SUB), ROW_SUB)]


def _cparams(*sem):
    return pltpu.CompilerParams(dimension_semantics=sem, vmem_limit_bytes=VMEM_LIMIT)


def _bdot(a, b):
    return jnp.dot(a.astype(BF16), b.astype(BF16), preferred_element_type=F32)


def _bdot_nt(a, b):
    return lax.dot_general(a.astype(BF16), b.astype(BF16), (((1,), (1,)), ((), ())),
                           preferred_element_type=F32)


def _bdot_tn(a, b):
    return lax.dot_general(a.astype(BF16), b.astype(BF16), (((0,), (0,)), ((), ())),
                           preferred_element_type=F32)


def _dot_split_rhs(w01, x, passes):
    acc, rem = None, x
    for p in range(passes):
        part = rem.astype(BF16)
        d = jnp.dot(w01, part, preferred_element_type=F32)
        acc = d if acc is None else acc + d
        if p + 1 < passes:
            rem = rem - part.astype(F32)
    return acc


def _dot_split_lhs(x, w01, passes):
    acc, rem = None, x
    for p in range(passes):
        part = rem.astype(BF16)
        d = jnp.dot(part, w01, preferred_element_type=F32)
        acc = d if acc is None else acc + d
        if p + 1 < passes:
            rem = rem - part.astype(F32)
    return acc


def _rms(x, w):
    ms = jnp.mean(x * x, axis=-1, keepdims=True)
    return x * lax.rsqrt(ms + NORM_EPS) * w


def _inproj_kernel(*refs, has_vres, has_zeros):
    refs = list(refs)
    x_ref, nw_ref, w_ref, wg_ref = refs[:4]
    refs = refs[4:]
    vd_ref = refs.pop(0) if has_vres else None
    qkv_ref, rkv_ref, lora_ref, gate_ref = refs[:4]
    refs = refs[4:]
    zv_ref = refs.pop(0) if has_vres else None
    zeros_ref = refs.pop(0) if has_zeros else None
    (h_scr,) = refs
    j = pl.program_id(1)

    if has_zeros:
        zeros_ref[...] = jnp.zeros_like(zeros_ref)

    @pl.when(j == 0)
    def _():
        h_scr[...] = _rms(x_ref[...], nw_ref[...]).astype(BF16)
        if has_vres:
            zv_ref[...] = jnp.dot(h_scr[...], vd_ref[...], preferred_element_type=F32)

    def project(wt_ref, out_ref):
        for c in range(SEG // MXU_COLS):
            cs = slice(c * MXU_COLS, (c + 1) * MXU_COLS)
            acc = lax.dot_general(h_scr[...], wt_ref[cs, :], (((1,), (1,)), ((), ())),
                                  preferred_element_type=F32)
            out_ref[:, cs] = acc.astype(out_ref.dtype)

    @pl.when(j < QKV_TILES)
    def _():
        project(w_ref, qkv_ref)

    @pl.when((j >= QKV_TILES) & (j < QKV_TILES + RKV_TILES))
    def _():
        project(w_ref, rkv_ref)

    @pl.when(j == QKV_TILES + RKV_TILES)
    def _():
        project(w_ref, lora_ref)

    @pl.when(j >= FIRST_GATE_TILE)
    def _():
        project(wg_ref.at[0], gate_ref)


def _inproj(layer, x, norm_w, w_in, vd, zero_rows=0, tm=1024):
    n = x.shape[0]
    has_vres = vd is not None
    s_rkv = QKV_TILES
    in_specs = [
        pl.BlockSpec((tm, D_MODEL), lambda i, j: (i, 0)),
        pl.BlockSpec((1, D_MODEL), lambda i, j: (0, 0)),
        pl.BlockSpec((None, SEG, D_MODEL), lambda i, j: (layer, jnp.minimum(j, FIRST_GATE_TILE - 1), 0)),
        pl.BlockSpec((pl.Element(1), pl.Element(SEG), pl.Element(D_MODEL)),
                     lambda i, j: (layer, pl.multiple_of(
                         GATE_COL0 + SEG * jnp.clip(j - FIRST_GATE_TILE, 0, GATE_TILES - 1), 2 * SUBLANES), 0)),
    ]
    args = [x, norm_w, w_in, w_in]
    out_specs = [
        pl.BlockSpec((tm, SEG), lambda i, j: (i, jnp.clip(j, 0, QKV_TILES - 1))),
        pl.BlockSpec((tm, SEG), lambda i, j: (i, jnp.clip(j - s_rkv, 0, RKV_TILES - 1))),
        pl.BlockSpec((tm, SEG), lambda i, j: (i, 0)),
        pl.BlockSpec((tm, SEG), lambda i, j: (i, jnp.clip(j - FIRST_GATE_TILE, 0, GATE_TILES - 1))),
    ]
    out_shape = [
        jax.ShapeDtypeStruct((n, QKV_COLS), BF16),
        jax.ShapeDtypeStruct((n, RKV_COLS), F32),
        jax.ShapeDtypeStruct((n, LORA_COLS), F32),
        jax.ShapeDtypeStruct((n, GATE_COLS), BF16),
    ]
    if has_vres:
        in_specs.append(pl.BlockSpec((D_MODEL, LANES), lambda i, j: (0, 0)))
        args.append(vd)
        out_specs.append(pl.BlockSpec((tm, LANES), lambda i, j: (i, 0)))
        out_shape.append(jax.ShapeDtypeStruct((n, LANES), F32))
    if zero_rows:
        blk = zero_rows // ((n // tm) * ZERO_STEPS)
        assert blk * (n // tm) * ZERO_STEPS == zero_rows and blk % SUBLANES == 0
        out_specs.append(pl.BlockSpec((blk, LANES), lambda i, j: (i * ZERO_STEPS + jnp.minimum(j, ZERO_STEPS - 1), 0)))
        out_shape.append(jax.ShapeDtypeStruct((zero_rows, LANES), F32))
    return pl.pallas_call(
        functools.partial(_inproj_kernel, has_vres=has_vres, has_zeros=bool(zero_rows)),
        grid=(n // tm, IN_TILES),
        in_specs=in_specs,
        out_specs=out_specs,
        out_shape=out_shape,
        scratch_shapes=[pltpu.VMEM((tm, D_MODEL), BF16)],
        compiler_params=_cparams("parallel", "arbitrary"),
        name="inproj",
    )(*args)


def _attn_kernel(sink_ref, q_ref, kp_ref, kc_ref, vp_ref, vc_ref, bias_ref, o_ref, *, tiles_per_seq):
    first = (pl.program_id(0) % tiles_per_seq) == 0
    col = lax.broadcasted_iota(jnp.int32, (ATTN_BLOCK, 2 * ATTN_BLOCK), 1)
    pad_keys = jnp.logical_and(first, col < ATTN_BLOCK)
    scale = HEAD_DIM ** -0.5
    items = [(sb, hk) for sb in range(ATTN_SUB) for hk in range(N_KV_HEADS)]

    def window(prev_ref, cur_ref, sb, hk):
        ks = slice(hk * HEAD_DIM, (hk + 1) * HEAD_DIM)
        rows = slice(sb * ATTN_BLOCK, (sb + 1) * ATTN_BLOCK)
        prev = prev_ref[:, ks] if sb == 0 else cur_ref[(sb - 1) * ATTN_BLOCK:sb * ATTN_BLOCK, ks]
        return jnp.concatenate([prev, cur_ref[rows, ks]], axis=0)

    scores = {}
    for sb, hk in items:
        kw = window(kp_ref, kc_ref, sb, hk)
        for g in range(GQA_GROUP):
            h = hk * GQA_GROUP + g
            qh = q_ref[sb * ATTN_BLOCK:(sb + 1) * ATTN_BLOCK, h * HEAD_DIM:(h + 1) * HEAD_DIM] * scale
            s = lax.dot_general(qh, kw, (((1,), (1,)), ((), ())), preferred_element_type=F32)
            s = s + bias_ref[h]
            if sb == 0:
                s = jnp.where(pad_keys, -jnp.inf, s)
            scores[(sb, h)] = s
    row_max = {key: jnp.max(s, axis=-1, keepdims=True) for key, s in scores.items()}
    m = {key: jnp.maximum(row_max[key], sink_ref[key[1]]) for key in scores}
    p = {key: jnp.exp(scores[key] - m[key]) for key in scores}
    row_sum = {key: jnp.sum(p[key], axis=-1, keepdims=True) for key in scores}
    denoms = {key: row_sum[key] + jnp.exp(sink_ref[key[1]] - m[key]) for key in scores}
    probs = {key: p[key].astype(BF16) for key in scores}
    for sb, hk in items:
        vw = window(vp_ref, vc_ref, sb, hk)
        for g in range(GQA_GROUP):
            h = hk * GQA_GROUP + g
            o = jnp.dot(probs[(sb, h)], vw, preferred_element_type=F32) / denoms[(sb, h)]
            o_ref[sb * ATTN_BLOCK:(sb + 1) * ATTN_BLOCK, h * HEAD_DIM:(h + 1) * HEAD_DIM] = o.astype(BF16)


def _attn_bias():
    qi = jnp.arange(ATTN_BLOCK)[:, None]
    kj = jnp.arange(2 * ATTN_BLOCK)[None, :]
    dist = qi + ATTN_BLOCK - kj
    valid = (dist >= 0) & (dist < WINDOW)
    slopes = jnp.exp2(-8.0 * jnp.arange(1, N_Q_HEADS + 1, dtype=F32) / N_Q_HEADS)
    bias = -slopes[:, None, None] * dist.astype(F32)[None]
    return jnp.where(valid[None], bias, -jnp.inf)


def _attention(qkv, sinks, seq):
    n = qkv.shape[0]
    rows = ATTN_SUB * ATTN_BLOCK
    tiles_per_seq = seq // rows
    kcol = ATTN_WIDTH // KV_WIDTH
    prev = lambda i: jnp.maximum(i * ATTN_SUB - 1, 0)
    return pl.pallas_call(
        functools.partial(_attn_kernel, tiles_per_seq=tiles_per_seq),
        grid=(n // rows,),
        in_specs=[
            pl.BlockSpec(memory_space=pltpu.SMEM),
            pl.BlockSpec((rows, ATTN_WIDTH), lambda i: (i, 0)),
            pl.BlockSpec((ATTN_BLOCK, KV_WIDTH), lambda i: (prev(i), kcol)),
            pl.BlockSpec((rows, KV_WIDTH), lambda i: (i, kcol)),
            pl.BlockSpec((ATTN_BLOCK, KV_WIDTH), lambda i: (prev(i), kcol + 1)),
            pl.BlockSpec((rows, KV_WIDTH), lambda i: (i, kcol + 1)),
            pl.BlockSpec((N_Q_HEADS, ATTN_BLOCK, 2 * ATTN_BLOCK), lambda i: (0, 0, 0)),
        ],
        out_specs=pl.BlockSpec((rows, ATTN_WIDTH), lambda i: (i, 0)),
        out_shape=jax.ShapeDtypeStruct((n, ATTN_WIDTH), BF16),
        compiler_params=_cparams("parallel"),
        name="swa_attention",
    )(sinks, qkv, qkv, qkv, qkv, qkv, _attn_bias())


def _head_ones():
    ch = jnp.arange(MXU_COLS) // RWKV_HEAD
    return (ch[:, None] == ch[None, :]).astype(BF16)


def _head_sums(x, ones, passes):
    parts = [_dot_split_lhs(x[:, c:c + MXU_COLS], ones, passes) for c in range(0, RWKV_WIDTH, MXU_COLS)]
    return jnp.concatenate(parts, axis=1)


def _rwkv_kernel(*refs, has_vres):
    refs = list(refs)
    rkv_ref, lo_ref = refs[:2]
    refs = refs[2:]
    if has_vres:
        vf_ref, zv_ref, mixv_ref = refs[:3]
        refs = refs[3:]
    mixr_ref, mixl_ref, pv_ref, wup_ref, aup_ref, gup_ref, vup_ref, ones_ref, pp_ref = refs[:9]
    refs = refs[9:]
    y_ref = refs.pop(0)
    v_out = None if has_vres else refs.pop(0)
    s_scr, prev_rkv, prev_lo = refs[:3]
    prev_zv = refs[3] if has_vres else None

    L = SCAN_CHUNK
    W = SCAN_GROUP * RWKV_HEAD
    n_seq, step_rows = rkv_ref.shape[0], rkv_ref.shape[1]
    n_grp = N_RWKV_HEADS // SCAN_GROUP
    ones = ones_ref[...]

    @pl.when(pl.program_id(0) == 0)
    def _():
        s_scr[...] = jnp.zeros_like(s_scr)
        prev_rkv[...] = jnp.zeros_like(prev_rkv)
        prev_lo[...] = jnp.zeros_like(prev_lo)
        if has_vres:
            prev_zv[...] = jnp.zeros_like(prev_zv)

    def lerp_shift(z_ref, carry_ref, b, mix):
        z = z_ref[b]
        row = lax.broadcasted_iota(jnp.int32, z.shape, 0)
        back = jnp.where(row == 0, carry_ref[b, SUBLANES - 1:SUBLANES, :], pltpu.roll(z, 1, 0))
        carry_ref[b] = z_ref[b, step_rows - SUBLANES:step_rows, :]
        return z + (back - z) * mix

    w0, a0, k_k, k_a, v0 = (pv_ref[i:i + 1, :] for i in range(5))
    pre = []
    for b in range(n_seq):
        zs = lerp_shift(rkv_ref, prev_rkv, b, mixr_ref[...])
        los = lerp_shift(lo_ref, prev_lo, b, mixl_ref[...])
        r = zs[:, :RWKV_WIDTH]
        kr = zs[:, RWKV_WIDTH:2 * RWKV_WIDTH]
        vr = zs[:, 2 * RWKV_WIDTH:]
        wa = los[:, :LANES]
        col = lax.broadcasted_iota(jnp.int32, wa.shape, 1)
        wa_in = jnp.where(col < DECAY_LORA, jnp.tanh(wa), wa).astype(BF16)
        gd = los[:, LANES:3 * LANES]
        col_g = lax.broadcasted_iota(jnp.int32, gd.shape, 1)
        g_in = jnp.where(col_g < GATE_LORA, jax.nn.sigmoid(gd), gd).astype(BF16)
        dw = jnp.dot(wa_in, wup_ref[...], preferred_element_type=F32)
        da = jnp.dot(wa_in, aup_ref[...], preferred_element_type=F32)
        g = jnp.dot(g_in, gup_ref[...], preferred_element_type=F32)
        lw = -DECAY_SCALE * jax.nn.sigmoid(w0 + dw)
        a = jax.nn.sigmoid(a0 + da)
        if has_vres:
            zvs = lerp_shift(zv_ref, prev_zv, b, mixv_ref[...])
            dv = jnp.dot(zvs.astype(BF16), vup_ref[...], preferred_element_type=F32)
            vr = vr + (vf_ref[b] - vr) * jax.nn.sigmoid(v0 + dv)
        else:
            v_out[b] = vr
        kk0 = kr * k_k
        ss = _head_sums(kk0 * kk0, ones, 1)
        kk = kk0 * lax.rsqrt(jnp.maximum(ss, KK_NORM_FLOOR ** 2))
        pre.append(dict(r=r, lw=lw, k=kr * (1.0 + (a - 1.0) * k_a), v=vr, al=-kk, be=kk * a, g=g))

    row = lax.broadcasted_iota(jnp.int32, (L, L), 0)
    col = lax.broadcasted_iota(jnp.int32, (L, L), 1)
    tri = (row >= col).astype(BF16)
    t_w = lax.broadcasted_iota(jnp.int32, (L, W), 0)
    lane_w = lax.broadcasted_iota(jnp.int32, (L, W), 1)
    s_w = lane_w % RWKV_HEAD
    strict = t_w > s_w
    incl = t_w >= s_w
    eye = (t_w == s_w).astype(F32)
    blk_w = lane_w // RWKV_HEAD
    bd_mask = (lax.broadcasted_iota(jnp.int32, (W, W), 0) // RWKV_HEAD
               == lax.broadcasted_iota(jnp.int32, (W, W), 1) // RWKV_HEAD)

    def blockdiag(w):
        tiled = jnp.concatenate([w.astype(BF16)] * SCAN_GROUP, axis=0)
        return jnp.where(bd_mask, tiled, jnp.zeros_like(tiled))

    ln_w, ln_b, r_k = (pp_ref[i:i + 1, :] for i in range(3))
    items = [(b, g) for b in range(n_seq) for g in range(n_grp)]
    for sub in range(step_rows // L):
        rows = slice(sub * L, (sub + 1) * L)
        ops = {}
        for b in range(n_seq):
            p = pre[b]
            lw = p['lw'][rows]
            cum = _dot_split_rhs(tri, lw, 2)
            p_inc = jnp.exp(cum)
            p_exc = jnp.exp(cum - lw)
            p_inv = jnp.exp(-cum)
            p_last = p_inc[L - 1:L, :]
            a_t = p['al'][rows] * p_exc
            r_t = p['r'][rows] * p_inc
            b_t = p['be'][rows] * p_inv
            k_t = p['k'][rows] * p_inv
            b_end = b_t * p_last
            k_end = k_t * p_last
            v = p['v'][rows]
            for g in range(n_grp):
                gs = slice(g * W, (g + 1) * W)
                ops[(b, g)] = dict(
                    ar=jnp.concatenate([a_t[:, gs], r_t[:, gs]], axis=0),
                    bt=b_t[:, gs], kt=k_t[:, gs], v=v[:, gs], p_last=p_last[:, gs],
                    bke=jnp.concatenate([b_end[:, gs], k_end[:, gs]], axis=0))

        gm, xs, av, s0 = {}, {}, {}, {}
        for it in items:
            o = ops[it]
            rhs = jnp.concatenate([blockdiag(o['bt']), blockdiag(o['kt'])], axis=0)
            gm[it] = _bdot_nt(o['ar'], rhs)
            s0[it] = s_scr[it[0], it[1]]
            xs[it] = _bdot_nt(o['ar'], blockdiag(s0[it]))
        a_ab, a_rb = {}, {}
        for it in items:
            g = gm[it]
            a_ab[it] = jnp.where(strict, g[:L, :W], 0.0)
            a_rb[it] = jnp.where(incl, g[L:, :W], 0.0)
            a_kk = jnp.concatenate([jnp.where(strict, g[:L, W:], 0.0), jnp.where(incl, g[L:, W:], 0.0)], axis=0)
            av[it] = _bdot(a_kk, blockdiag(ops[it]['v']))

        pw = {it: _bdot(a_ab[it], blockdiag(a_ab[it])) for it in items}
        tinv = {it: eye + a_ab[it] for it in items}
        span = 2
        while 2 * span < L:
            both = {it: _bdot(jnp.concatenate([pw[it], tinv[it]], axis=0), blockdiag(pw[it])) for it in items}
            tinv = {it: tinv[it] + both[it][L:] for it in items}
            pw = {it: both[it][:L] for it in items}
            span *= 2
        tinv = {it: tinv[it] + _bdot(tinv[it], blockdiag(pw[it])) for it in items}

        u = {it: _bdot(tinv[it], blockdiag(xs[it][:L] + av[it][:L])) for it in items}
        y = {it: xs[it][L:] + av[it][L:] + _bdot(a_rb[it], blockdiag(u[it])) for it in items}
        for it in items:
            o = ops[it]
            full = _bdot_tn(jnp.concatenate([u[it], o['v']], axis=0), o['bke'])
            upd = full[:RWKV_HEAD]
            for h in range(1, SCAN_GROUP):
                upd = jnp.where(blk_w == h, full[h * RWKV_HEAD:(h + 1) * RWKV_HEAD], upd)
            s_scr[it[0], it[1]] = s0[it] * o['p_last'] + upd

        inv_n = 1.0 / RWKV_HEAD
        for b in range(n_seq):
            p = pre[b]
            yb = jnp.concatenate([y[(b, g)] for g in range(n_grp)], axis=1)
            d = yb - _head_sums(yb, ones, 1) * inv_n
            var = _head_sums(d * d, ones, 1) * inv_n
            yn = d * lax.rsqrt(var + RWKV_GN_EPS) * ln_w + ln_b
            bonus = _head_sums(p['r'][rows] * p['k'][rows] * r_k, ones, 1) * p['v'][rows]
            y_ref[b, rows, :] = ((yn + bonus) * p['g'][rows]).astype(BF16)


def _rwkv_mixer(rkv, lora, v_first, zv, mix_r, mix_l, mix_v, pv_pre, wup, aup, gup, vup, pv_post, seq):
    n = rkv.shape[0]
    n_seq = n // seq
    has_vres = v_first is not None
    step_rows = SCAN_SUB * SCAN_CHUNK
    timed = lambda cols: pl.BlockSpec((n_seq, step_rows, cols), lambda t: (0, t, 0))
    const = lambda shape: pl.BlockSpec(shape, lambda t: (0, 0))
    by_seq = lambda a: a.reshape(n_seq, seq, a.shape[-1])
    in_specs = [timed(RKV_COLS), timed(LORA_COLS)]
    args = [by_seq(rkv), by_seq(lora)]
    if has_vres:
        in_specs += [timed(RWKV_WIDTH), timed(LANES), const((1, LANES))]
        args += [by_seq(v_first), by_seq(zv), mix_v]
    in_specs += [
        const((1, RKV_COLS)), const((1, LORA_COLS)), const((SUBLANES, RWKV_WIDTH)),
        const((LANES, RWKV_WIDTH)), const((LANES, RWKV_WIDTH)), const((2 * LANES, RWKV_WIDTH)),
        const((LANES, RWKV_WIDTH)), const((MXU_COLS, MXU_COLS)), const((SUBLANES, RWKV_WIDTH)),
    ]
    args += [mix_r, mix_l, pv_pre, wup, aup, gup, vup, _head_ones(), pv_post]
    out_specs = [timed(RWKV_WIDTH)]
    out_shape = [jax.ShapeDtypeStruct((n_seq, seq, RWKV_WIDTH), BF16)]
    if not has_vres:
        out_specs.append(timed(RWKV_WIDTH))
        out_shape.append(jax.ShapeDtypeStruct((n_seq, seq, RWKV_WIDTH), F32))
    scratch = [
        pltpu.VMEM((n_seq, N_RWKV_HEADS // SCAN_GROUP, RWKV_HEAD, SCAN_GROUP * RWKV_HEAD), F32),
        pltpu.VMEM((n_seq, SUBLANES, RKV_COLS), F32),
        pltpu.VMEM((n_seq, SUBLANES, LORA_COLS), F32),
    ]
    if has_vres:
        scratch.append(pltpu.VMEM((n_seq, SUBLANES, LANES), F32))
    outs = pl.pallas_call(
        functools.partial(_rwkv_kernel, has_vres=has_vres),
        grid=(seq // step_rows,),
        in_specs=in_specs,
        out_specs=out_specs,
        out_shape=out_shape,
        scratch_shapes=scratch,
        compiler_params=_cparams("arbitrary"),
        name="rwkv_mixer",
    )(*args)
    y = outs[0].reshape(n, RWKV_WIDTH)
    return y, (None if has_vres else outs[1].reshape(n, RWKV_WIDTH))


def _merge_kernel(a_ref, y_ref, wa_ref, wb_ref, ga_ref, gb_ref, o_ref):
    for c in range(o_ref.shape[1] // MXU_COLS):
        cs = slice(c * MXU_COLS, (c + 1) * MXU_COLS)
        pa = jnp.dot(a_ref[...], wa_ref[:, cs].astype(BF16), preferred_element_type=F32)
        pb = jnp.dot(y_ref[...], wb_ref[:, cs].astype(BF16), preferred_element_type=F32)
        ga = jax.nn.sigmoid(ga_ref[:, cs].astype(F32))
        gb = jax.nn.sigmoid(gb_ref[:, cs].astype(F32))
        o_ref[:, cs] = (ga * pa + gb * pb).astype(BF16)


def _merge(layer, attn, y, wa, wb, gates, tm=1024, tn=1024):
    n = attn.shape[0]
    nj = D_MODEL // tn
    return pl.pallas_call(
        _merge_kernel,
        grid=(n // tm, nj),
        in_specs=[
            pl.BlockSpec((tm, ATTN_WIDTH), lambda i, j: (i, 0)),
            pl.BlockSpec((tm, RWKV_WIDTH), lambda i, j: (i, 0)),
            pl.BlockSpec((None, ATTN_WIDTH, tn), lambda i, j: (layer, 0, j)),
            pl.BlockSpec((None, RWKV_WIDTH, tn), lambda i, j: (layer, 0, j)),
            pl.BlockSpec((tm, tn), lambda i, j: (i, j)),
            pl.BlockSpec((tm, tn), lambda i, j: (i, j + nj)),
        ],
        out_specs=pl.BlockSpec((tm, tn), lambda i, j: (i, j)),
        out_shape=jax.ShapeDtypeStruct((n, D_MODEL), BF16),
        compiler_params=_cparams("parallel", "arbitrary"),
        name="gated_merge",
    )(attn, y, wa, wb, gates, gates)


def _outproj_router_kernel(m_ref, wo_ref, x_ref, nw_ref, wr_ref, br_ref,
                           xo_ref, hf_ref, slab_ref, plan_ref, cnt_ref, run_scr):
    @pl.when(pl.program_id(0) == 0)
    def _():
        run_scr[...] = jnp.zeros_like(run_scr)

    tm = x_ref.shape[0]
    sq = jnp.zeros((tm, LANES), F32)
    for c in range(D_MODEL // MXU_COLS):
        cs = slice(c * MXU_COLS, (c + 1) * MXU_COLS)
        xn_c = x_ref[:, cs] + jnp.dot(m_ref[...], wo_ref[:, cs], preferred_element_type=F32)
        xo_ref[:, cs] = xn_c
        for l in range(MXU_COLS // LANES):
            piece = xn_c[:, l * LANES:(l + 1) * LANES]
            sq = sq + piece * piece
    ms = jnp.sum(sq, axis=-1, keepdims=True) * (1.0 / D_MODEL)
    hf = xo_ref[...] * lax.rsqrt(ms + NORM_EPS) * nw_ref[...]
    _store_rows(hf_ref, hf)

    h_hi = hf.astype(BF16)
    h_lo = (hf - h_hi.astype(F32)).astype(BF16)
    wr = wr_ref[...]
    w_hi = wr.astype(BF16)
    w_lo = (wr - w_hi.astype(F32)).astype(BF16)
    lg = (jnp.dot(h_hi, w_hi, preferred_element_type=F32)
          + jnp.dot(h_hi, w_lo, preferred_element_type=F32)
          + jnp.dot(h_lo, w_hi, preferred_element_type=F32)) + br_ref[...]

    chunks = range(tm // ROUTER_ROWS)
    lgs = [lg[c * ROUTER_ROWS:(c + 1) * ROUTER_ROWS] for c in chunks]
    lane = lax.broadcasted_iota(jnp.int32, (ROUTER_ROWS, ROUTER_LANES), 1).astype(F32)
    neg = -jnp.inf
    big = float(ROUTER_LANES)
    is_group = lane < N_GROUPS

    def row_max(vals):
        return [jnp.max(v, axis=-1, keepdims=True) for v in vals]

    def first_index(vals, mx):
        return [jnp.min(jnp.where(v == m, lane, big), axis=-1, keepdims=True) for v, m in zip(vals, mx)]

    def row_sum(vals):
        return [jnp.sum(v, axis=-1, keepdims=True) for v in vals]

    gl = [jnp.where(is_group, v, neg) for v in lgs]
    gmax = row_max(gl)
    gsel = first_index(gl, gmax)
    gsum = row_sum([jnp.where(is_group, jnp.exp(v - m), 0.0) for v, m in zip(lgs, gmax)])
    lo_lane = [EXPERT_LANE0 + EXPERTS_PER_GROUP * g for g in gsel]
    el = [jnp.where((lane >= lo) & (lane < lo + EXPERTS_PER_GROUP), v, neg) for v, lo in zip(lgs, lo_lane)]
    v1 = row_max(el)
    i1 = first_index(el, v1)
    el2 = [jnp.where(lane == i, neg, v) for v, i in zip(el, i1)]
    v2 = row_max(el2)
    i2 = first_index(el2, v2)
    oh1 = [lane == i for i in i1]
    oh2 = [lane == i for i in i2]

    cnt = jnp.concatenate([a.astype(F32) + b.astype(F32) for a, b in zip(oh1, oh2)], axis=0)
    r_i = lax.broadcasted_iota(jnp.int32, (tm, tm), 0)
    c_i = lax.broadcasted_iota(jnp.int32, (tm, tm), 1)
    before = jnp.dot((r_i > c_i).astype(BF16), cnt.astype(BF16), preferred_element_type=F32)
    tot = before + run_scr[0:1, :]
    tots = [tot[c * ROUTER_ROWS:(c + 1) * ROUTER_ROWS] for c in chunks]
    rank1 = row_sum([jnp.where(o, t, 0.0) for o, t in zip(oh1, tots)])
    rank2 = row_sum([jnp.where(o, t, 0.0) for o, t in zip(oh2, tots)])
    run = run_scr[0:1, :] + jnp.sum(cnt, axis=0, keepdims=True)
    run_scr[...] = jnp.broadcast_to(run, run_scr.shape)
    cnt_ref[...] = jnp.broadcast_to(run, cnt_ref.shape)

    for c in chunks:
        gp = 1.0 / gsum[c]
        e21 = jnp.exp(v2[c] - v1[c])
        ew1 = gp / (1.0 + e21)
        ew2 = gp * e21 / (1.0 + e21)
        slab = jnp.where(lane == 0, i1[c] - EXPERT_LANE0,
               jnp.where(lane == 1, i2[c] - EXPERT_LANE0,
               jnp.where(lane == 2, ew1,
               jnp.where(lane == 3, ew2,
               jnp.where(lane == 4, rank1[c],
               jnp.where(lane == 5, rank2[c], 0.0))))))
        slab_ref[c * ROUTER_ROWS:(c + 1) * ROUTER_ROWS, :] = slab
        plan_ref[:, c * ROUTER_ROWS:(c + 1) * ROUTER_ROWS] = jnp.transpose(slab)[:SUBLANES, :]


def _outproj_router(merged, w_out, x, norm_w, w_router, b_router, tm=512):
    n = x.shape[0]
    row = lambda i: (i, 0)
    const = lambda i: (0, 0)
    return pl.pallas_call(
        _outproj_router_kernel,
        grid=(n // tm,),
        in_specs=[
            pl.BlockSpec((tm, D_MODEL), row),
            pl.BlockSpec((D_MODEL, D_MODEL), const),
            pl.BlockSpec((tm, D_MODEL), row),
            pl.BlockSpec((1, D_MODEL), const),
            pl.BlockSpec((D_MODEL, ROUTER_LANES), const),
            pl.BlockSpec((1, ROUTER_LANES), const),
        ],
        out_specs=[
            pl.BlockSpec((tm, D_MODEL), row),
            pl.BlockSpec((tm * ROW_SUB, LANES), row),
            pl.BlockSpec((tm, ROUTER_LANES), row),
            pl.BlockSpec((SUBLANES, tm), lambda i: (0, i)),
            pl.BlockSpec((SUBLANES, ROUTER_LANES), const),
        ],
        out_shape=[
            jax.ShapeDtypeStruct((n, D_MODEL), F32),
            jax.ShapeDtypeStruct((n * ROW_SUB, LANES), F32),
            jax.ShapeDtypeStruct((n, ROUTER_LANES), F32),
            jax.ShapeDtypeStruct((SUBLANES, n), F32),
            jax.ShapeDtypeStruct((SUBLANES, ROUTER_LANES), F32),
        ],
        scratch_shapes=[pltpu.VMEM((SUBLANES, ROUTER_LANES), F32)],
        compiler_params=_cparams("arbitrary"),
        name="outproj_router",
    )(merged, w_out, x, norm_w, w_router, b_router)


def _dispatch_kernel(slot0_ref, slot1_ref, hf_ref, xb_in_hbm, xb_hbm, sem, *, tm):
    del xb_in_hbm

    def copy(u, j):
        return pltpu.make_async_copy(_slab(hf_ref, u), _slab(xb_hbm, (slot0_ref, slot1_ref)[j][u]), sem)

    def start(u, c):
        for j in range(TOP_K):
            copy(u, j).start(priority=j)
        return c

    def wait(u, c):
        for j in range(TOP_K):
            copy(u, j).wait()
        return c

    lax.fori_loop(0, tm, start, 0, unroll=8)
    lax.fori_loop(0, tm, wait, 0, unroll=8)


def _dispatch(slots, hf, cap, backing, tm=256):
    n = hf.shape[0] // ROW_SUB
    assert backing.shape == (cap * ROW_SUB, LANES)
    return pl.pallas_call(
        functools.partial(_dispatch_kernel, tm=tm),
        grid=(n // tm,),
        in_specs=[
            pl.BlockSpec((tm,), lambda i: (i,), memory_space=pltpu.SMEM),
            pl.BlockSpec((tm,), lambda i: (i,), memory_space=pltpu.SMEM),
            pl.BlockSpec((tm * ROW_SUB, LANES), lambda i: (i, 0)),
            pl.BlockSpec(memory_space=pl.ANY),
        ],
        out_specs=pl.BlockSpec(memory_space=pl.ANY),
        out_shape=jax.ShapeDtypeStruct((cap * ROW_SUB, LANES), F32),
        scratch_shapes=[pltpu.SemaphoreType.DMA(())],
        input_output_aliases={3: 0},
        compiler_params=_cparams("arbitrary"),
        name="moe_dispatch",
    )(slots[0], slots[1], hf, backing)


def _ffn_kernel(be_ref, nu_ref, par_ref, nxt_ref, x_ref, wg_hbm, wu_hbm, wd_hbm, o_ref,
                wg_f, wu_f, wd_f, wg_s, wu_s, wd_s, sems, *, layer):
    b = pl.program_id(0)
    used = b < nu_ref[0]
    prev_e = be_ref[jnp.maximum(b - 1, 0)]
    new_expert = jnp.logical_or(b == 0, be_ref[b] != prev_e)

    def fetch(e, p):
        return [pltpu.make_async_copy(src.at[layer, e], dst.at[p], sems.at[p, i])
                for i, (src, dst) in enumerate(((wg_hbm, wg_f), (wu_hbm, wu_f), (wd_hbm, wd_f)))]

    @pl.when(jnp.logical_and(used, new_expert))
    def _():
        for p in range(2):
            @pl.when(par_ref[b] == p)
            def _():
                @pl.when(b == 0)
                def _():
                    for cp in fetch(be_ref[0], p):
                        cp.start()

                @pl.when(nxt_ref[b] >= 0)
                def _():
                    for cp in fetch(nxt_ref[b], 1 - p):
                        cp.start()

                for cp in fetch(be_ref[b], p):
                    cp.wait()
                wg_s[...] = wg_f[p].astype(BF16)
                wu_s[...] = wu_f[p].astype(BF16)
                wd_s[...] = wd_f[p].astype(BF16)

    @pl.when(used)
    def _():
        slabs = MXU_COLS // LANES
        gt = up = None
        for kc in range(D_MODEL // MXU_COLS):
            xk = jnp.concatenate([x_ref[pl.ds(kc * slabs + s, MOE_ROWS, stride=ROW_SUB), :] for s in range(slabs)],
                                 axis=1).astype(BF16)
            ks = slice(kc * MXU_COLS, (kc + 1) * MXU_COLS)
            g_k = jnp.dot(xk, wg_s[ks, :], preferred_element_type=F32)
            u_k = jnp.dot(xk, wu_s[ks, :], preferred_element_type=F32)
            gt = g_k if gt is None else gt + g_k
            up = u_k if up is None else up + u_k
        hid = (gt * jax.nn.sigmoid(gt) * up).astype(BF16)
        for c in range(D_MODEL // MXU_COLS):
            y_c = jnp.dot(hid, wd_s[:, c * MXU_COLS:(c + 1) * MXU_COLS], preferred_element_type=F32)
            for s in range(slabs):
                o_ref[pl.ds(c * slabs + s, MOE_ROWS, stride=ROW_SUB), :] = y_c[:, s * LANES:(s + 1) * LANES]

    @pl.when(jnp.logical_not(used))
    def _():
        o_ref[...] = jnp.zeros_like(o_ref)


def _expert_ffn(layer, plan, xb, w_gate, w_up, w_down):
    nblk = xb.shape[0] // (MOE_ROWS * ROW_SUB)
    hbm = pl.BlockSpec(memory_space=pl.ANY)
    grid_spec = pltpu.PrefetchScalarGridSpec(
        num_scalar_prefetch=4,
        grid=(nblk,),
        in_specs=[
            pl.BlockSpec((MOE_ROWS * ROW_SUB, LANES), lambda b, be, nu, par, nxt: (jnp.minimum(b, nu[0] - 1), 0)),
            hbm, hbm, hbm,
        ],
        out_specs=pl.BlockSpec((MOE_ROWS * ROW_SUB, LANES), lambda b, be, nu, par, nxt: (b, 0)),
        scratch_shapes=[
            pltpu.VMEM((2, D_MODEL, EXPERT_HIDDEN), F32),
            pltpu.VMEM((2, D_MODEL, EXPERT_HIDDEN), F32),
            pltpu.VMEM((2, EXPERT_HIDDEN, D_MODEL), F32),
            pltpu.VMEM((D_MODEL, EXPERT_HIDDEN), BF16),
            pltpu.VMEM((D_MODEL, EXPERT_HIDDEN), BF16),
            pltpu.VMEM((EXPERT_HIDDEN, D_MODEL), BF16),
            pltpu.SemaphoreType.DMA((2, 3)),
        ],
    )
    return pl.pallas_call(
        functools.partial(_ffn_kernel, layer=layer),
        grid_spec=grid_spec,
        out_shape=jax.ShapeDtypeStruct(xb.shape, F32),
        compiler_params=_cparams("arbitrary"),
        name="expert_ffn",
    )(plan.block_expert, plan.n_used, plan.parity, plan.next_expert, xb, w_gate, w_up, w_down)


def _combine_kernel(slot0_ref, slot1_ref, next0_ref, next1_ref, x_ref, slab_ref, fw_ref, yb_hbm, o_ref,
                    buf00, buf01, buf10, buf11, sems, *, tm, n_tiles, final_norm):
    i = pl.program_id(0)
    bufs = ((buf00, buf01), (buf10, buf11))
    cur = (slot0_ref, slot1_ref)

    def copies(slots, p, u):
        return [pltpu.make_async_copy(_slab(yb_hbm, slots[j][u]), _slab(bufs[p][j], u), sems.at[p])
                for j in range(TOP_K)]

    def start_all(slots, p):
        def body(u, c):
            for j, cp in enumerate(copies(slots, p, u)):
                cp.start(priority=j)
            return c
        lax.fori_loop(0, tm, body, 0, unroll=8)

    def wait_all(p):
        def body(u, c):
            for cp in copies(cur, p, u):
                cp.wait()
            return c
        lax.fori_loop(0, tm, body, 0, unroll=8)

    @pl.when(i == 0)
    def _():
        start_all(cur, 0)

    for p in range(2):
        @pl.when(i % 2 == p)
        def _():
            @pl.when(i + 1 < n_tiles)
            def _():
                start_all((next0_ref, next1_ref), 1 - p)

            wait_all(p)
            slab = slab_ref[...]
            w1 = slab[:, 2:3]
            w2 = slab[:, 3:4]
            out = x_ref[...] + (_rows_to_2d(bufs[p][0], tm) * w1 + _rows_to_2d(bufs[p][1], tm) * w2)
            if final_norm:
                out = _rms(out, fw_ref[...])
            o_ref[...] = out


def _combine(slots, x, slab, final_w, yb, final_norm, tm=256):
    n = x.shape[0]
    n_tiles = n // tm
    row = lambda i: (i, 0)
    row_buf = pltpu.VMEM((tm * ROW_SUB, LANES), F32)
    return pl.pallas_call(
        functools.partial(_combine_kernel, tm=tm, n_tiles=n_tiles, final_norm=final_norm),
        grid=(n_tiles,),
        in_specs=[
            pl.BlockSpec((tm,), lambda i: (i,), memory_space=pltpu.SMEM),
            pl.BlockSpec((tm,), lambda i: (i,), memory_space=pltpu.SMEM),
            pl.BlockSpec((tm,), lambda i: (jnp.minimum(i + 1, n_tiles - 1),), memory_space=pltpu.SMEM),
            pl.BlockSpec((tm,), lambda i: (jnp.minimum(i + 1, n_tiles - 1),), memory_space=pltpu.SMEM),
            pl.BlockSpec((tm, D_MODEL), row),
            pl.BlockSpec((tm, ROUTER_LANES), row),
            pl.BlockSpec((1, D_MODEL), lambda i: (0, 0)),
            pl.BlockSpec(memory_space=pl.ANY),
        ],
        out_specs=pl.BlockSpec((tm, D_MODEL), row),
        out_shape=jax.ShapeDtypeStruct((n, D_MODEL), F32),
        scratch_shapes=[row_buf, row_buf, row_buf, row_buf, pltpu.SemaphoreType.DMA((2,))],
        compiler_params=_cparams("arbitrary"),
        name="moe_combine",
    )(slots[0], slots[1], slots[0], slots[1], x, slab, final_w, yb)


class MoePlan(NamedTuple):
    slot: tuple
    block_expert: jax.Array
    n_used: jax.Array
    parity: jax.Array
    next_expert: jax.Array
    cap: int


def _moe_plan(plan_t, counts_row, n_tokens):
    eid = plan_t[0:TOP_K].astype(jnp.int32)
    rank = plan_t[4:4 + TOP_K].astype(jnp.int32)
    counts = counts_row[EXPERT_LANE0:EXPERT_LANE0 + N_EXPERTS].astype(jnp.int32)
    padded = (counts + MOE_ROWS - 1) // MOE_ROWS * MOE_ROWS
    pad_ends = jnp.cumsum(padded)
    pad_starts = pad_ends - padded
    expert = jnp.arange(N_EXPERTS, dtype=jnp.int32)[:, None, None]
    region = jnp.sum(jnp.where(eid[None] == expert, pad_starts[:, None, None], 0), axis=0)
    slot = region + rank
    nblk = n_tokens * TOP_K // MOE_ROWS + N_EXPERTS
    block_start = jnp.arange(nblk, dtype=jnp.int32) * MOE_ROWS
    block_expert = jnp.minimum(
        jnp.sum((pad_ends[None, :] <= block_start[:, None]).astype(jnp.int32), axis=1),
        N_EXPERTS - 1).astype(jnp.int32)
    n_used = (pad_ends[-1:] // MOE_ROWS).astype(jnp.int32)
    blk = jnp.arange(nblk, dtype=jnp.int32)
    prev_expert = jnp.concatenate([block_expert[:1], block_expert[:-1]])
    first = (blk < n_used[0]) & ((blk == 0) | (block_expert != prev_expert))
    parity = ((jnp.cumsum(first.astype(jnp.int32)) - 1) % 2).astype(jnp.int32)
    later_first = first[None, :] & (blk[None, :] > blk[:, None])
    next_first = jnp.min(jnp.where(later_first, blk[None, :], nblk), axis=1)
    next_expert = jnp.where(next_first < nblk, block_expert[jnp.minimum(next_first, nblk - 1)], -1)
    return MoePlan(tuple(slot[j] for j in range(TOP_K)), block_expert, n_used, parity,
                   next_expert.astype(jnp.int32), nblk * MOE_ROWS)


def kernel(x, attn_norm_w, w_in, shift_mix, attn_sinks, rwkv_w0, rwkv_w_up, rwkv_a0, rwkv_a_up, rwkv_g_up, rwkv_k_k, rwkv_k_a, rwkv_r_k, rwkv_ln_w, rwkv_ln_b, vres_down, vres_mix, vres_up, vres_v0, w_branch_a, w_branch_b, w_out, ffn_norm_w, router_group_w, router_group_b, router_expert_w, router_expert_b, expert_w_gate, expert_w_up, expert_w_down, final_norm_w):
    batch, seq, _ = x.shape
    n = batch * seq
    depth = w_in.shape[0]
    xf = x.reshape(n, D_MODEL)
    w_in_bf = jnp.swapaxes(w_in, 1, 2).astype(BF16)

    zero_c = jnp.zeros((depth, RWKV_WIDTH), F32)
    v0_all = jnp.concatenate([zero_c[:1], vres_v0.astype(F32)], axis=0)
    pv_pre_all = jnp.stack([rwkv_w0, rwkv_a0, rwkv_k_k, rwkv_k_a, v0_all, zero_c, zero_c, zero_c], axis=1)
    pv_post_all = jnp.stack([rwkv_ln_w, rwkv_ln_b, rwkv_r_k.reshape(depth, RWKV_WIDTH),
                             zero_c, zero_c, zero_c, zero_c, zero_c], axis=1)
    mix_r_all = shift_mix[:, None, :RKV_COLS]
    mix_l_all = jnp.pad(shift_mix[:, None, RKV_COLS:], ((0, 0), (0, 0), (0, LORA_COLS - LORA_WIDTH)))
    wup_all = jnp.pad(rwkv_w_up.astype(BF16), ((0, 0), (0, LANES - DECAY_LORA), (0, 0)))
    aup_all = jnp.pad(rwkv_a_up.astype(BF16), ((0, 0), (DECAY_LORA, LANES - DECAY_LORA - AAA_LORA), (0, 0)))
    gup_all = jnp.pad(rwkv_g_up.astype(BF16), ((0, 0), (0, 2 * LANES - GATE_LORA), (0, 0)))
    vup_all = jnp.pad(vres_up.astype(BF16), ((1, 0), (0, LANES - MV_LORA), (0, 0)))
    vd_all = jnp.pad(vres_down.astype(BF16), ((0, 0), (0, 0), (0, LANES - MV_LORA)))
    mix_v_all = jnp.pad(vres_mix, ((0, 0), (0, LANES - MV_LORA)))[:, None, :]
    router_pad = ROUTER_LANES - N_GROUPS - N_EXPERTS
    w_router_all = jnp.pad(jnp.concatenate([router_group_w, router_expert_w], axis=2),
                           ((0, 0), (0, 0), (0, router_pad)))
    b_router_all = jnp.pad(jnp.concatenate([router_group_b, router_expert_b], axis=1),
                           ((0, 0), (0, router_pad)))[:, None, :]
    attn_norm_all = attn_norm_w[:, None, :]
    ffn_norm_all = ffn_norm_w[:, None, :]
    w_out_bf = w_out.astype(BF16)
    sinks_all = attn_sinks.astype(F32)

    v_first = None
    slot_backing = None
    slot_rows = (n * TOP_K // MOE_ROWS + N_EXPERTS) * MOE_ROWS
    for i in range(depth):
        has_vres = i > 0
        mix_r, mix_l, pv_pre, pv_post = mix_r_all[i], mix_l_all[i], pv_pre_all[i], pv_post_all[i]
        wup, aup, gup, vup = wup_all[i], aup_all[i], gup_all[i], vup_all[i]
        vd, mix_v = (vd_all[i - 1], mix_v_all[i - 1]) if has_vres else (None, None)
        w_router, b_router = w_router_all[i], b_router_all[i]

        need_zeros = slot_backing is None
        proj = _inproj(i, xf, attn_norm_all[i], w_in_bf, vd,
                       zero_rows=slot_rows * ROW_SUB if need_zeros else 0)
        qkv, rkv, lora, gates = proj[:4]
        zv = proj[4] if has_vres else None
        if need_zeros:
            slot_backing = proj[-1]
        attn = _attention(qkv, sinks_all[i], seq)
        y, v_new = _rwkv_mixer(rkv, lora, v_first, zv, mix_r, mix_l, mix_v, pv_pre,
                               wup, aup, gup, vup, pv_post, seq)
        if not has_vres:
            v_first = v_new
        merged = _merge(i, attn, y, w_branch_a, w_branch_b, gates)

        x_mid, hf, slab, plan_t, counts = _outproj_router(
            merged, w_out_bf[i], xf, ffn_norm_all[i], w_router, b_router)
        plan = _moe_plan(plan_t, counts[0], n)
        xb = _dispatch(plan.slot, hf, plan.cap, slot_backing)
        yb = _expert_ffn(i, plan, xb, expert_w_gate, expert_w_up, expert_w_down)
        slot_backing = yb
        xf = _combine(plan.slot, x_mid, slab, final_norm_w.reshape(1, D_MODEL), yb, final_norm=(i == depth - 1))
    return xf.reshape(batch, seq, D_MODEL)
```

```python
import functools
import math
from typing import NamedTuple

import jax
import jax.numpy as jnp
from jax import lax
from jax.experimental import pallas as pl
from jax.experimental.pallas import tpu as pltpu

F32 = jnp.float32
BF16 = jnp.bfloat16

D_MODEL = 2048
HEAD_DIM = 64
N_Q_HEADS = 16
N_KV_HEADS = 4
GQA_GROUP = N_Q_HEADS // N_KV_HEADS
ATTN_WIDTH = N_Q_HEADS * HEAD_DIM
KV_WIDTH = N_KV_HEADS * HEAD_DIM
WINDOW = 128
ATTN_BLOCK = 128
ATTN_SUB = 4

RWKV_HEAD = 64
N_RWKV_HEADS = 16
RWKV_WIDTH = N_RWKV_HEADS * RWKV_HEAD
DECAY_LORA = 64
AAA_LORA = 64
MV_LORA = 32
GATE_LORA = 160
RWKV_GN_EPS = 64e-5
DECAY_SCALE = math.exp(-0.5)
KK_NORM_FLOOR = 1e-12
LORA_WIDTH = DECAY_LORA + AAA_LORA + GATE_LORA

N_GROUPS = 4
EXPERTS_PER_GROUP = 8
N_EXPERTS = N_GROUPS * EXPERTS_PER_GROUP
TOP_K = 2
EXPERT_HIDDEN = D_MODEL // 4
NORM_EPS = 1e-5

LANES = 128
SUBLANES = 8
MXU_COLS = 256

SEG = 512
QKV_COLS = ATTN_WIDTH + 2 * KV_WIDTH
RKV_COLS = 3 * RWKV_WIDTH
LORA_COLS = SEG
GATE_COLS = 2 * D_MODEL
QKV_TILES = QKV_COLS // SEG
RKV_TILES = RKV_COLS // SEG
LORA_TILES = LORA_COLS // SEG
GATE_TILES = GATE_COLS // SEG
IN_TILES = QKV_TILES + RKV_TILES + LORA_TILES + GATE_TILES
FIRST_GATE_TILE = QKV_TILES + RKV_TILES + LORA_TILES
ZERO_STEPS = 16
GATE_COL0 = QKV_COLS + RKV_COLS + LORA_WIDTH

SCAN_CHUNK = 64
SCAN_SUB = 2
SCAN_GROUP = 4
MOE_ROWS = 256
ROUTER_LANES = LANES
ROUTER_ROWS = 128
EXPERT_LANE0 = N_GROUPS

ROW_SUB = D_MODEL // LANES

VMEM_LIMIT = 56 * 1024 * 1024


def _rows_to_2d(slab_ref, rows):
    return jnp.concatenate([slab_ref[pl.ds(s, rows, stride=ROW_SUB), :] for s in range(ROW_SUB)], axis=1)


def _store_rows(slab_ref, val):
    rows = val.shape[0]
    for s in range(ROW_SUB):
        slab_ref[pl.ds(s, rows, stride=ROW_SUB), :] = val[:, s * LANES:(s + 1) * LANES]


def _slab(ref, row):
    return ref.at[pl.ds(pl.multiple_of(row * ROW_SUB, ROW_SUB), ROW_SUB)]


def _cparams(*sem):
    return pltpu.CompilerParams(dimension_semantics=sem, vmem_limit_bytes=VMEM_LIMIT)


def _bdot(a, b):
    return jnp.dot(a.astype(BF16), b.astype(BF16), preferred_element_type=F32)


def _bdot_nt(a, b):
    return lax.dot_general(a.astype(BF16), b.astype(BF16), (((1,), (1,)), ((), ())),
                           preferred_element_type=F32)


def _bdot_tn(a, b):
    return lax.dot_general(a.astype(BF16), b.astype(BF16), (((0,), (0,)), ((), ())),
                           preferred_element_type=F32)


def _dot_split_rhs(w01, x, passes):
    acc, rem = None, x
    for p in range(passes):
        part = rem.astype(BF16)
        d = jnp.dot(w01, part, preferred_element_type=F32)
        acc = d if acc is None else acc + d
        if p + 1 < passes:
            rem = rem - part.astype(F32)
    return acc


def _dot_split_lhs(x, w01, passes):
    acc, rem = None, x
    for p in range(passes):
        part = rem.astype(BF16)
        d = jnp.dot(part, w01, preferred_element_type=F32)
        acc = d if acc is None else acc + d
        if p + 1 < passes:
            rem = rem - part.astype(F32)
    return acc


def _rms(x, w):
    ms = jnp.mean(x * x, axis=-1, keepdims=True)
    return x * lax.rsqrt(ms + NORM_EPS) * w


def _inproj_kernel(*refs, has_vres, has_zeros):
    refs = list(refs)
    x_ref, nw_ref, w_ref, wg_ref = refs[:4]
    refs = refs[4:]
    vd_ref = refs.pop(0) if has_vres else None
    qkv_ref, rkv_ref, lora_ref, gate_ref = refs[:4]
    refs = refs[4:]
    zv_ref = refs.pop(0) if has_vres else None
    zeros_ref = refs.pop(0) if has_zeros else None
    (h_scr,) = refs
    j = pl.program_id(1)

    if has_zeros:
        zeros_ref[...] = jnp.zeros_like(zeros_ref)

    @pl.when(j == 0)
    def _():
        h_scr[...] = _rms(x_ref[...], nw_ref[...]).astype(BF16)
        if has_vres:
            zv_ref[...] = jnp.dot(h_scr[...], vd_ref[...], preferred_element_type=F32)

    def project(wt_ref, out_ref):
        for c in range(SEG // MXU_COLS):
            cs = slice(c * MXU_COLS, (c + 1) * MXU_COLS)
            acc = lax.dot_general(h_scr[...], wt_ref[cs, :], (((1,), (1,)), ((), ())),
                                  preferred_element_type=F32)
            out_ref[:, cs] = acc.astype(out_ref.dtype)

    @pl.when(j < QKV_TILES)
    def _():
        project(w_ref, qkv_ref)

    @pl.when((j >= QKV_TILES) & (j < QKV_TILES + RKV_TILES))
    def _():
        project(w_ref, rkv_ref)

    @pl.when(j == QKV_TILES + RKV_TILES)
    def _():
        project(w_ref, lora_ref)

    @pl.when(j >= FIRST_GATE_TILE)
    def _():
        project(wg_ref.at[0], gate_ref)


def _inproj(layer, x, norm_w, w_in, vd, zero_rows=0, tm=1024):
    n = x.shape[0]
    has_vres = vd is not None
    s_rkv = QKV_TILES
    in_specs = [
        pl.BlockSpec((tm, D_MODEL), lambda i, j: (i, 0)),
        pl.BlockSpec((1, D_MODEL), lambda i, j: (0, 0)),
        pl.BlockSpec((None, SEG, D_MODEL), lambda i, j: (layer, jnp.minimum(j, FIRST_GATE_TILE - 1), 0)),
        pl.BlockSpec((pl.Element(1), pl.Element(SEG), pl.Element(D_MODEL)),
                     lambda i, j: (layer, pl.multiple_of(
                         GATE_COL0 + SEG * jnp.clip(j - FIRST_GATE_TILE, 0, GATE_TILES - 1), 2 * SUBLANES), 0)),
    ]
    args = [x, norm_w, w_in, w_in]
    out_specs = [
        pl.BlockSpec((tm, SEG), lambda i, j: (i, jnp.clip(j, 0, QKV_TILES - 1))),
        pl.BlockSpec((tm, SEG), lambda i, j: (i, jnp.clip(j - s_rkv, 0, RKV_TILES - 1))),
        pl.BlockSpec((tm, SEG), lambda i, j: (i, 0)),
        pl.BlockSpec((tm, SEG), lambda i, j: (i, jnp.clip(j - FIRST_GATE_TILE, 0, GATE_TILES - 1))),
    ]
    out_shape = [
        jax.ShapeDtypeStruct((n, QKV_COLS), BF16),
        jax.ShapeDtypeStruct((n, RKV_COLS), F32),
        jax.ShapeDtypeStruct((n, LORA_COLS), F32),
        jax.ShapeDtypeStruct((n, GATE_COLS), BF16),
    ]
    if has_vres:
        in_specs.append(pl.BlockSpec((D_MODEL, LANES), lambda i, j: (0, 0)))
        args.append(vd)
        out_specs.append(pl.BlockSpec((tm, LANES), lambda i, j: (i, 0)))
        out_shape.append(jax.ShapeDtypeStruct((n, LANES), F32))
    if zero_rows:
        blk = zero_rows // ((n // tm) * ZERO_STEPS)
        assert blk * (n // tm) * ZERO_STEPS == zero_rows and blk % SUBLANES == 0
        out_specs.append(pl.BlockSpec((blk, LANES), lambda i, j: (i * ZERO_STEPS + jnp.minimum(j, ZERO_STEPS - 1), 0)))
        out_shape.append(jax.ShapeDtypeStruct((zero_rows, LANES), F32))
    return pl.pallas_call(
        functools.partial(_inproj_kernel, has_vres=has_vres, has_zeros=bool(zero_rows)),
        grid=(n // tm, IN_TILES),
        in_specs=in_specs,
        out_specs=out_specs,
        out_shape=out_shape,
        scratch_shapes=[pltpu.VMEM((tm, D_MODEL), BF16)],
        compiler_params=_cparams("parallel", "arbitrary"),
        name="inproj",
    )(*args)


def _attn_kernel(sink_ref, q_ref, kp_ref, kc_ref, vp_ref, vc_ref, bias_ref, o_ref, *, tiles_per_seq):
    first = (pl.program_id(0) % tiles_per_seq) == 0
    col = lax.broadcasted_iota(jnp.int32, (ATTN_BLOCK, 2 * ATTN_BLOCK), 1)
    pad_keys = jnp.logical_and(first, col < ATTN_BLOCK)
    scale = HEAD_DIM ** -0.5
    items = [(sb, hk) for sb in range(ATTN_SUB) for hk in range(N_KV_HEADS)]

    def window(prev_ref, cur_ref, sb, hk):
        ks = slice(hk * HEAD_DIM, (hk + 1) * HEAD_DIM)
        rows = slice(sb * ATTN_BLOCK, (sb + 1) * ATTN_BLOCK)
        prev = prev_ref[:, ks] if sb == 0 else cur_ref[(sb - 1) * ATTN_BLOCK:sb * ATTN_BLOCK, ks]
        return jnp.concatenate([prev, cur_ref[rows, ks]], axis=0)

    scores = {}
    for sb, hk in items:
        kw = window(kp_ref, kc_ref, sb, hk)
        for g in range(GQA_GROUP):
            h = hk * GQA_GROUP + g
            qh = q_ref[sb * ATTN_BLOCK:(sb + 1) * ATTN_BLOCK, h * HEAD_DIM:(h + 1) * HEAD_DIM] * scale
            s = lax.dot_general(qh, kw, (((1,), (1,)), ((), ())), preferred_element_type=F32)
            s = s + bias_ref[h]
            if sb == 0:
                s = jnp.where(pad_keys, -jnp.inf, s)
            scores[(sb, h)] = s
    row_max = {key: jnp.max(s, axis=-1, keepdims=True) for key, s in scores.items()}
    m = {key: jnp.maximum(row_max[key], sink_ref[key[1]]) for key in scores}
    p = {key: jnp.exp(scores[key] - m[key]) for key in scores}
    row_sum = {key: jnp.sum(p[key], axis=-1, keepdims=True) for key in scores}
    denoms = {key: row_sum[key] + jnp.exp(sink_ref[key[1]] - m[key]) for key in scores}
    probs = {key: p[key].astype(BF16) for key in scores}
    for sb, hk in items:
        vw = window(vp_ref, vc_ref, sb, hk)
        for g in range(GQA_GROUP):
            h = hk * GQA_GROUP + g
            o = jnp.dot(probs[(sb, h)], vw, preferred_element_type=F32) / denoms[(sb, h)]
            o_ref[sb * ATTN_BLOCK:(sb + 1) * ATTN_BLOCK, h * HEAD_DIM:(h + 1) * HEAD_DIM] = o.astype(BF16)


def _attn_bias():
    qi = jnp.arange(ATTN_BLOCK)[:, None]
    kj = jnp.arange(2 * ATTN_BLOCK)[None, :]
    dist = qi + ATTN_BLOCK - kj
    valid = (dist >= 0) & (dist < WINDOW)
    slopes = jnp.exp2(-8.0 * jnp.arange(1, N_Q_HEADS + 1, dtype=F32) / N_Q_HEADS)
    bias = -slopes[:, None, None] * dist.astype(F32)[None]
    return jnp.where(valid[None], bias, -jnp.inf)


def _attention(qkv, sinks, seq):
    n = qkv.shape[0]
    rows = ATTN_SUB * ATTN_BLOCK
    tiles_per_seq = seq // rows
    kcol = ATTN_WIDTH // KV_WIDTH
    prev = lambda i: jnp.maximum(i * ATTN_SUB - 1, 0)
    return pl.pallas_call(
        functools.partial(_attn_kernel, tiles_per_seq=tiles_per_seq),
        grid=(n // rows,),
        in_specs=[
            pl.BlockSpec(memory_space=pltpu.SMEM),
            pl.BlockSpec((rows, ATTN_WIDTH), lambda i: (i, 0)),
            pl.BlockSpec((ATTN_BLOCK, KV_WIDTH), lambda i: (prev(i), kcol)),
            pl.BlockSpec((rows, KV_WIDTH), lambda i: (i, kcol)),
            pl.BlockSpec((ATTN_BLOCK, KV_WIDTH), lambda i: (prev(i), kcol + 1)),
            pl.BlockSpec((rows, KV_WIDTH), lambda i: (i, kcol + 1)),
            pl.BlockSpec((N_Q_HEADS, ATTN_BLOCK, 2 * ATTN_BLOCK), lambda i: (0, 0, 0)),
        ],
        out_specs=pl.BlockSpec((rows, ATTN_WIDTH), lambda i: (i, 0)),
        out_shape=jax.ShapeDtypeStruct((n, ATTN_WIDTH), BF16),
        compiler_params=_cparams("parallel"),
        name="swa_attention",
    )(sinks, qkv, qkv, qkv, qkv, qkv, _attn_bias())


def _head_ones():
    ch = jnp.arange(MXU_COLS) // RWKV_HEAD
    return (ch[:, None] == ch[None, :]).astype(BF16)


def _head_sums(x, ones, passes):
    parts = [_dot_split_lhs(x[:, c:c + MXU_COLS], ones, passes) for c in range(0, RWKV_WIDTH, MXU_COLS)]
    return jnp.concatenate(parts, axis=1)


def _rwkv_kernel(*refs, has_vres):
    refs = list(refs)
    rkv_ref, lo_ref = refs[:2]
    refs = refs[2:]
    if has_vres:
        vf_ref, zv_ref, mixv_ref = refs[:3]
        refs = refs[3:]
    mixr_ref, mixl_ref, pv_ref, wup_ref, aup_ref, gup_ref, vup_ref, ones_ref, pp_ref = refs[:9]
    refs = refs[9:]
    y_ref = refs.pop(0)
    v_out = None if has_vres else refs.pop(0)
    s_scr, prev_rkv, prev_lo = refs[:3]
    prev_zv = refs[3] if has_vres else None

    L = SCAN_CHUNK
    W = SCAN_GROUP * RWKV_HEAD
    n_seq, step_rows = rkv_ref.shape[0], rkv_ref.shape[1]
    n_grp = N_RWKV_HEADS // SCAN_GROUP
    ones = ones_ref[...]

    @pl.when(pl.program_id(0) == 0)
    def _():
        s_scr[...] = jnp.zeros_like(s_scr)
        prev_rkv[...] = jnp.zeros_like(prev_rkv)
        prev_lo[...] = jnp.zeros_like(prev_lo)
        if has_vres:
            prev_zv[...] = jnp.zeros_like(prev_zv)

    def lerp_shift(z_ref, carry_ref, b, mix):
        z = z_ref[b]
        row = lax.broadcasted_iota(jnp.int32, z.shape, 0)
        back = jnp.where(row == 0, carry_ref[b, SUBLANES - 1:SUBLANES, :], pltpu.roll(z, 1, 0))
        carry_ref[b] = z_ref[b, step_rows - SUBLANES:step_rows, :]
        return z + (back - z) * mix

    w0, a0, k_k, k_a, v0 = (pv_ref[i:i + 1, :] for i in range(5))
    pre = []
    for b in range(n_seq):
        zs = lerp_shift(rkv_ref, prev_rkv, b, mixr_ref[...])
        los = lerp_shift(lo_ref, prev_lo, b, mixl_ref[...])
        r = zs[:, :RWKV_WIDTH]
        kr = zs[:, RWKV_WIDTH:2 * RWKV_WIDTH]
        vr = zs[:, 2 * RWKV_WIDTH:]
        wa = los[:, :LANES]
        col = lax.broadcasted_iota(jnp.int32, wa.shape, 1)
        wa_in = jnp.where(col < DECAY_LORA, jnp.tanh(wa), wa).astype(BF16)
        gd = los[:, LANES:3 * LANES]
        col_g = lax.broadcasted_iota(jnp.int32, gd.shape, 1)
        g_in = jnp.where(col_g < GATE_LORA, jax.nn.sigmoid(gd), gd).astype(BF16)
        dw = jnp.dot(wa_in, wup_ref[...], preferred_element_type=F32)
        da = jnp.dot(wa_in, aup_ref[...], preferred_element_type=F32)
        g = jnp.dot(g_in, gup_ref[...], preferred_element_type=F32)
        lw = -DECAY_SCALE * jax.nn.sigmoid(w0 + dw)
        a = jax.nn.sigmoid(a0 + da)
        if has_vres:
            zvs = lerp_shift(zv_ref, prev_zv, b, mixv_ref[...])
            dv = jnp.dot(zvs.astype(BF16), vup_ref[...], preferred_element_type=F32)
            vr = vr + (vf_ref[b] - vr) * jax.nn.sigmoid(v0 + dv)
        else:
            v_out[b] = vr
        kk0 = kr * k_k
        ss = _head_sums(kk0 * kk0, ones, 1)
        kk = kk0 * lax.rsqrt(jnp.maximum(ss, KK_NORM_FLOOR ** 2))
        pre.append(dict(r=r, lw=lw, k=kr * (1.0 + (a - 1.0) * k_a), v=vr, al=-kk, be=kk * a, g=g))

    row = lax.broadcasted_iota(jnp.int32, (L, L), 0)
    col = lax.broadcasted_iota(jnp.int32, (L, L), 1)
    tri = (row >= col).astype(BF16)
    t_w = lax.broadcasted_iota(jnp.int32, (L, W), 0)
    lane_w = lax.broadcasted_iota(jnp.int32, (L, W), 1)
    s_w = lane_w % RWKV_HEAD
    strict = t_w > s_w
    incl = t_w >= s_w
    eye = (t_w == s_w).astype(F32)
    blk_w = lane_w // RWKV_HEAD
    bd_mask = (lax.broadcasted_iota(jnp.int32, (W, W), 0) // RWKV_HEAD
               == lax.broadcasted_iota(jnp.int32, (W, W), 1) // RWKV_HEAD)

    def blockdiag(w):
        tiled = jnp.concatenate([w.astype(BF16)] * SCAN_GROUP, axis=0)
        return jnp.where(bd_mask, tiled, jnp.zeros_like(tiled))

    ln_w, ln_b, r_k = (pp_ref[i:i + 1, :] for i in range(3))
    items = [(b, g) for b in range(n_seq) for g in range(n_grp)]
    for sub in range(step_rows // L):
        rows = slice(sub * L, (sub + 1) * L)
        ops = {}
        for b in range(n_seq):
            p = pre[b]
            lw = p['lw'][rows]
            cum = _dot_split_rhs(tri, lw, 2)
            p_inc = jnp.exp(cum)
            p_exc = jnp.exp(cum - lw)
            p_inv = jnp.exp(-cum)
            p_last = p_inc[L - 1:L, :]
            a_t = p['al'][rows] * p_exc
            r_t = p['r'][rows] * p_inc
            b_t = p['be'][rows] * p_inv
            k_t = p['k'][rows] * p_inv
            b_end = b_t * p_last
            k_end = k_t * p_last
            v = p['v'][rows]
            for g in range(n_grp):
                gs = slice(g * W, (g + 1) * W)
                ops[(b, g)] = dict(
                    ar=jnp.concatenate([a_t[:, gs], r_t[:, gs]], axis=0),
                    bt=b_t[:, gs], kt=k_t[:, gs], v=v[:, gs], p_last=p_last[:, gs],
                    bke=jnp.concatenate([b_end[:, gs], k_end[:, gs]], axis=0))

        gm, xs, av, s0 = {}, {}, {}, {}
        for it in items:
            o = ops[it]
            rhs = jnp.concatenate([blockdiag(o['bt']), blockdiag(o['kt'])], axis=0)
            gm[it] = _bdot_nt(o['ar'], rhs)
            s0[it] = s_scr[it[0], it[1]]
            xs[it] = _bdot_nt(o['ar'], blockdiag(s0[it]))
        a_ab, a_rb = {}, {}
        for it in items:
            g = gm[it]
            a_ab[it] = jnp.where(strict, g[:L, :W], 0.0)
            a_rb[it] = jnp.where(incl, g[L:, :W], 0.0)
            a_kk = jnp.concatenate([jnp.where(strict, g[:L, W:], 0.0), jnp.where(incl, g[L:, W:], 0.0)], axis=0)
            av[it] = _bdot(a_kk, blockdiag(ops[it]['v']))

        pw = {it: _bdot(a_ab[it], blockdiag(a_ab[it])) for it in items}
        tinv = {it: eye + a_ab[it] for it in items}
        span = 2
        while 2 * span < L:
            both = {it: _bdot(jnp.concatenate([pw[it], tinv[it]], axis=0), blockdiag(pw[it])) for it in items}
            tinv = {it: tinv[it] + both[it][L:] for it in items}
            pw = {it: both[it][:L] for it in items}
            span *= 2
        tinv = {it: tinv[it] + _bdot(tinv[it], blockdiag(pw[it])) for it in items}

        u = {it: _bdot(tinv[it], blockdiag(xs[it][:L] + av[it][:L])) for it in items}
        y = {it: xs[it][L:] + av[it][L:] + _bdot(a_rb[it], blockdiag(u[it])) for it in items}
        for it in items:
            o = ops[it]
            full = _bdot_tn(jnp.concatenate([u[it], o['v']], axis=0), o['bke'])
            upd = full[:RWKV_HEAD]
            for h in range(1, SCAN_GROUP):
                upd = jnp.where(blk_w == h, full[h * RWKV_HEAD:(h + 1) * RWKV_HEAD], upd)
            s_scr[it[0], it[1]] = s0[it] * o['p_last'] + upd

        inv_n = 1.0 / RWKV_HEAD
        for b in range(n_seq):
            p = pre[b]
            yb = jnp.concatenate([y[(b, g)] for g in range(n_grp)], axis=1)
            d = yb - _head_sums(yb, ones, 1) * inv_n
            var = _head_sums(d * d, ones, 1) * inv_n
            yn = d * lax.rsqrt(var + RWKV_GN_EPS) * ln_w + ln_b
            bonus = _head_sums(p['r'][rows] * p['k'][rows] * r_k, ones, 1) * p['v'][rows]
            y_ref[b, rows, :] = ((yn + bonus) * p['g'][rows]).astype(BF16)


def _rwkv_mixer(rkv, lora, v_first, zv, mix_r, mix_l, mix_v, pv_pre, wup, aup, gup, vup, pv_post, seq):
    n = rkv.shape[0]
    n_seq = n // seq
    has_vres = v_first is not None
    step_rows = SCAN_SUB * SCAN_CHUNK
    timed = lambda cols: pl.BlockSpec((n_seq, step_rows, cols), lambda t: (0, t, 0))
    const = lambda shape: pl.BlockSpec(shape, lambda t: (0, 0))
    by_seq = lambda a: a.reshape(n_seq, seq, a.shape[-1])
    in_specs = [timed(RKV_COLS), timed(LORA_COLS)]
    args = [by_seq(rkv), by_seq(lora)]
    if has_vres:
        in_specs += [timed(RWKV_WIDTH), timed(LANES), const((1, LANES))]
        args += [by_seq(v_first), by_seq(zv), mix_v]
    in_specs += [
        const((1, RKV_COLS)), const((1, LORA_COLS)), const((SUBLANES, RWKV_WIDTH)),
        const((LANES, RWKV_WIDTH)), const((LANES, RWKV_WIDTH)), const((2 * LANES, RWKV_WIDTH)),
        const((LANES, RWKV_WIDTH)), const((MXU_COLS, MXU_COLS)), const((SUBLANES, RWKV_WIDTH)),
    ]
    args += [mix_r, mix_l, pv_pre, wup, aup, gup, vup, _head_ones(), pv_post]
    out_specs = [timed(RWKV_WIDTH)]
    out_shape = [jax.ShapeDtypeStruct((n_seq, seq, RWKV_WIDTH), BF16)]
    if not has_vres:
        out_specs.append(timed(RWKV_WIDTH))
        out_shape.append(jax.ShapeDtypeStruct((n_seq, seq, RWKV_WIDTH), F32))
    scratch = [
        pltpu.VMEM((n_seq, N_RWKV_HEADS // SCAN_GROUP, RWKV_HEAD, SCAN_GROUP * RWKV_HEAD), F32),
        pltpu.VMEM((n_seq, SUBLANES, RKV_COLS), F32),
        pltpu.VMEM((n_seq, SUBLANES, LORA_COLS), F32),
    ]
    if has_vres:
        scratch.append(pltpu.VMEM((n_seq, SUBLANES, LANES), F32))
    outs = pl.pallas_call(
        functools.partial(_rwkv_kernel, has_vres=has_vres),
        grid=(seq // step_rows,),
        in_specs=in_specs,
        out_specs=out_specs,
        out_shape=out_shape,
        scratch_shapes=scratch,
        compiler_params=_cparams("arbitrary"),
        name="rwkv_mixer",
    )(*args)
    y = outs[0].reshape(n, RWKV_WIDTH)
    return y, (None if has_vres else outs[1].reshape(n, RWKV_WIDTH))


def _merge_kernel(a_ref, y_ref, wa_ref, wb_ref, ga_ref, gb_ref, o_ref):
    for c in range(o_ref.shape[1] // MXU_COLS):
        cs = slice(c * MXU_COLS, (c + 1) * MXU_COLS)
        pa = jnp.dot(a_ref[...], wa_ref[:, cs].astype(BF16), preferred_element_type=F32)
        pb = jnp.dot(y_ref[...], wb_ref[:, cs].astype(BF16), preferred_element_type=F32)
        ga = jax.nn.sigmoid(ga_ref[:, cs].astype(F32))
        gb = jax.nn.sigmoid(gb_ref[:, cs].astype(F32))
        o_ref[:, cs] = (ga * pa + gb * pb).astype(BF16)


def _merge(layer, attn, y, wa, wb, gates, tm=1024, tn=1024):
    n = attn.shape[0]
    nj = D_MODEL // tn
    return pl.pallas_call(
        _merge_kernel,
        grid=(n // tm, nj),
        in_specs=[
            pl.BlockSpec((tm, ATTN_WIDTH), lambda i, j: (i, 0)),
            pl.BlockSpec((tm, RWKV_WIDTH), lambda i, j: (i, 0)),
            pl.BlockSpec((None, ATTN_WIDTH, tn), lambda i, j: (layer, 0, j)),
            pl.BlockSpec((None, RWKV_WIDTH, tn), lambda i, j: (layer, 0, j)),
            pl.BlockSpec((tm, tn), lambda i, j: (i, j)),
            pl.BlockSpec((tm, tn), lambda i, j: (i, j + nj)),
        ],
        out_specs=pl.BlockSpec((tm, tn), lambda i, j: (i, j)),
        out_shape=jax.ShapeDtypeStruct((n, D_MODEL), BF16),
        compiler_params=_cparams("parallel", "arbitrary"),
        name="gated_merge",
    )(attn, y, wa, wb, gates, gates)


def _outproj_router_kernel(m_ref, wo_ref, x_ref, nw_ref, wr_ref, br_ref,
                           xo_ref, hf_ref, slab_ref, plan_ref, cnt_ref, run_scr):
    @pl.when(pl.program_id(0) == 0)
    def _():
        run_scr[...] = jnp.zeros_like(run_scr)

    tm = x_ref.shape[0]
    sq = jnp.zeros((tm, LANES), F32)
    for c in range(D_MODEL // MXU_COLS):
        cs = slice(c * MXU_COLS, (c + 1) * MXU_COLS)
        xn_c = x_ref[:, cs] + jnp.dot(m_ref[...], wo_ref[:, cs], preferred_element_type=F32)
        xo_ref[:, cs] = xn_c
        for l in range(MXU_COLS // LANES):
            piece = xn_c[:, l * LANES:(l + 1) * LANES]
            sq = sq + piece * piece
    ms = jnp.sum(sq, axis=-1, keepdims=True) * (1.0 / D_MODEL)
    hf = xo_ref[...] * lax.rsqrt(ms + NORM_EPS) * nw_ref[...]
    _store_rows(hf_ref, hf)

    h_hi = hf.astype(BF16)
    h_lo = (hf - h_hi.astype(F32)).astype(BF16)
    wr = wr_ref[...]
    w_hi = wr.astype(BF16)
    w_lo = (wr - w_hi.astype(F32)).astype(BF16)
    lg = (jnp.dot(h_hi, w_hi, preferred_element_type=F32)
          + jnp.dot(h_hi, w_lo, preferred_element_type=F32)
          + jnp.dot(h_lo, w_hi, preferred_element_type=F32)) + br_ref[...]

    chunks = range(tm // ROUTER_ROWS)
    lgs = [lg[c * ROUTER_ROWS:(c + 1) * ROUTER_ROWS] for c in chunks]
    lane = lax.broadcasted_iota(jnp.int32, (ROUTER_ROWS, ROUTER_LANES), 1).astype(F32)
    neg = -jnp.inf
    big = float(ROUTER_LANES)
    is_group = lane < N_GROUPS

    def row_max(vals):
        return [jnp.max(v, axis=-1, keepdims=True) for v in vals]

    def first_index(vals, mx):
        return [jnp.min(jnp.where(v == m, lane, big), axis=-1, keepdims=True) for v, m in zip(vals, mx)]

    def row_sum(vals):
        return [jnp.sum(v, axis=-1, keepdims=True) for v in vals]

    gl = [jnp.where(is_group, v, neg) for v in lgs]
    gmax = row_max(gl)
    gsel = first_index(gl, gmax)
    gsum = row_sum([jnp.where(is_group, jnp.exp(v - m), 0.0) for v, m in zip(lgs, gmax)])
    lo_lane = [EXPERT_LANE0 + EXPERTS_PER_GROUP * g for g in gsel]
    el = [jnp.where((lane >= lo) & (lane < lo + EXPERTS_PER_GROUP), v, neg) for v, lo in zip(lgs, lo_lane)]
    v1 = row_max(el)
    i1 = first_index(el, v1)
    el2 = [jnp.where(lane == i, neg, v) for v, i in zip(el, i1)]
    v2 = row_max(el2)
    i2 = first_index(el2, v2)
    oh1 = [lane == i for i in i1]
    oh2 = [lane == i for i in i2]

    cnt = jnp.concatenate([a.astype(F32) + b.astype(F32) for a, b in zip(oh1, oh2)], axis=0)
    r_i = lax.broadcasted_iota(jnp.int32, (tm, tm), 0)
    c_i = lax.broadcasted_iota(jnp.int32, (tm, tm), 1)
    before = jnp.dot((r_i > c_i).astype(BF16), cnt.astype(BF16), preferred_element_type=F32)
    tot = before + run_scr[0:1, :]
    tots = [tot[c * ROUTER_ROWS:(c + 1) * ROUTER_ROWS] for c in chunks]
    rank1 = row_sum([jnp.where(o, t, 0.0) for o, t in zip(oh1, tots)])
    rank2 = row_sum([jnp.where(o, t, 0.0) for o, t in zip(oh2, tots)])
    run = run_scr[0:1, :] + jnp.sum(cnt, axis=0, keepdims=True)
    run_scr[...] = jnp.broadcast_to(run, run_scr.shape)
    cnt_ref[...] = jnp.broadcast_to(run, cnt_ref.shape)

    for c in chunks:
        gp = 1.0 / gsum[c]
        e21 = jnp.exp(v2[c] - v1[c])
        ew1 = gp / (1.0 + e21)
        ew2 = gp * e21 / (1.0 + e21)
        slab = jnp.where(lane == 0, i1[c] - EXPERT_LANE0,
               jnp.where(lane == 1, i2[c] - EXPERT_LANE0,
               jnp.where(lane == 2, ew1,
               jnp.where(lane == 3, ew2,
               jnp.where(lane == 4, rank1[c],
               jnp.where(lane == 5, rank2[c], 0.0))))))
        slab_ref[c * ROUTER_ROWS:(c + 1) * ROUTER_ROWS, :] = slab
        plan_ref[:, c * ROUTER_ROWS:(c + 1) * ROUTER_ROWS] = jnp.transpose(slab)[:SUBLANES, :]


def _outproj_router(merged, w_out, x, norm_w, w_router, b_router, tm=512):
    n = x.shape[0]
    row = lambda i: (i, 0)
    const = lambda i: (0, 0)
    return pl.pallas_call(
        _outproj_router_kernel,
        grid=(n // tm,),
        in_specs=[
            pl.BlockSpec((tm, D_MODEL), row),
            pl.BlockSpec((D_MODEL, D_MODEL), const),
            pl.BlockSpec((tm, D_MODEL), row),
            pl.BlockSpec((1, D_MODEL), const),
            pl.BlockSpec((D_MODEL, ROUTER_LANES), const),
            pl.BlockSpec((1, ROUTER_LANES), const),
        ],
        out_specs=[
            pl.BlockSpec((tm, D_MODEL), row),
            pl.BlockSpec((tm * ROW_SUB, LANES), row),
            pl.BlockSpec((tm, ROUTER_LANES), row),
            pl.BlockSpec((SUBLANES, tm), lambda i: (0, i)),
            pl.BlockSpec((SUBLANES, ROUTER_LANES), const),
        ],
        out_shape=[
            jax.ShapeDtypeStruct((n, D_MODEL), F32),
            jax.ShapeDtypeStruct((n * ROW_SUB, LANES), F32),
            jax.ShapeDtypeStruct((n, ROUTER_LANES), F32),
            jax.ShapeDtypeStruct((SUBLANES, n), F32),
            jax.ShapeDtypeStruct((SUBLANES, ROUTER_LANES), F32),
        ],
        scratch_shapes=[pltpu.VMEM((SUBLANES, ROUTER_LANES), F32)],
        compiler_params=_cparams("arbitrary"),
        name="outproj_router",
    )(merged, w_out, x, norm_w, w_router, b_router)


def _dispatch_kernel(slot0_ref, slot1_ref, hf_ref, xb_in_hbm, xb_hbm, sem, *, tm):
    del xb_in_hbm

    def copy(u, j):
        return pltpu.make_async_copy(_slab(hf_ref, u), _slab(xb_hbm, (slot0_ref, slot1_ref)[j][u]), sem)

    def start(u, c):
        for j in range(TOP_K):
            copy(u, j).start(priority=j)
        return c

    def wait(u, c):
        for j in range(TOP_K):
            copy(u, j).wait()
        return c

    lax.fori_loop(0, tm, start, 0, unroll=8)
    lax.fori_loop(0, tm, wait, 0, unroll=8)


def _dispatch(slots, hf, cap, backing, tm=512):
    n = hf.shape[0] // ROW_SUB
    assert backing.shape == (cap * ROW_SUB, LANES)
    return pl.pallas_call(
        functools.partial(_dispatch_kernel, tm=tm),
        grid=(n // tm,),
        in_specs=[
            pl.BlockSpec((tm,), lambda i: (i,), memory_space=pltpu.SMEM),
            pl.BlockSpec((tm,), lambda i: (i,), memory_space=pltpu.SMEM),
            pl.BlockSpec((tm * ROW_SUB, LANES), lambda i: (i, 0)),
            pl.BlockSpec(memory_space=pl.ANY),
        ],
        out_specs=pl.BlockSpec(memory_space=pl.ANY),
        out_shape=jax.ShapeDtypeStruct((cap * ROW_SUB, LANES), F32),
        scratch_shapes=[pltpu.SemaphoreType.DMA(())],
        input_output_aliases={3: 0},
        compiler_params=_cparams("arbitrary"),
        name="moe_dispatch",
    )(slots[0], slots[1], hf, backing)


def _ffn_kernel(be_ref, nu_ref, par_ref, nxt_ref, x_ref, wg_hbm, wu_hbm, wd_hbm, o_ref,
                wg_f, wu_f, wd_f, wg_s, wu_s, wd_s, sems, *, layer):
    b = pl.program_id(0)
    used = b < nu_ref[0]
    prev_e = be_ref[jnp.maximum(b - 1, 0)]
    new_expert = jnp.logical_or(b == 0, be_ref[b] != prev_e)

    def fetch(e, p):
        return [pltpu.make_async_copy(src.at[layer, e], dst.at[p], sems.at[p, i])
                for i, (src, dst) in enumerate(((wg_hbm, wg_f), (wu_hbm, wu_f), (wd_hbm, wd_f)))]

    @pl.when(jnp.logical_and(used, new_expert))
    def _():
        for p in range(2):
            @pl.when(par_ref[b] == p)
            def _():
                @pl.when(b == 0)
                def _():
                    for cp in fetch(be_ref[0], p):
                        cp.start()

                @pl.when(nxt_ref[b] >= 0)
                def _():
                    for cp in fetch(nxt_ref[b], 1 - p):
                        cp.start()

                for cp in fetch(be_ref[b], p):
                    cp.wait()
                wg_s[...] = wg_f[p].astype(BF16)
                wu_s[...] = wu_f[p].astype(BF16)
                wd_s[...] = wd_f[p].astype(BF16)

    @pl.when(used)
    def _():
        slabs = MXU_COLS // LANES
        gt = up = None
        for kc in range(D_MODEL // MXU_COLS):
            xk = jnp.concatenate([x_ref[pl.ds(kc * slabs + s, MOE_ROWS, stride=ROW_SUB), :] for s in range(slabs)],
                                 axis=1).astype(BF16)
            ks = slice(kc * MXU_COLS, (kc + 1) * MXU_COLS)
            g_k = jnp.dot(xk, wg_s[ks, :], preferred_element_type=F32)
            u_k = jnp.dot(xk, wu_s[ks, :], preferred_element_type=F32)
            gt = g_k if gt is None else gt + g_k
            up = u_k if up is None else up + u_k
        hid = (gt * jax.nn.sigmoid(gt) * up).astype(BF16)
        for c in range(D_MODEL // MXU_COLS):
            y_c = jnp.dot(hid, wd_s[:, c * MXU_COLS:(c + 1) * MXU_COLS], preferred_element_type=F32)
            for s in range(slabs):
                o_ref[pl.ds(c * slabs + s, MOE_ROWS, stride=ROW_SUB), :] = y_c[:, s * LANES:(s + 1) * LANES]

    @pl.when(jnp.logical_not(used))
    def _():
        o_ref[...] = jnp.zeros_like(o_ref)


def _expert_ffn(layer, plan, xb, w_gate, w_up, w_down):
    nblk = xb.shape[0] // (MOE_ROWS * ROW_SUB)
    hbm = pl.BlockSpec(memory_space=pl.ANY)
    grid_spec = pltpu.PrefetchScalarGridSpec(
        num_scalar_prefetch=4,
        grid=(nblk,),
        in_specs=[
            pl.BlockSpec((MOE_ROWS * ROW_SUB, LANES), lambda b, be, nu, par, nxt: (jnp.minimum(b, nu[0] - 1), 0)),
            hbm, hbm, hbm,
        ],
        out_specs=pl.BlockSpec((MOE_ROWS * ROW_SUB, LANES), lambda b, be, nu, par, nxt: (b, 0)),
        scratch_shapes=[
            pltpu.VMEM((2, D_MODEL, EXPERT_HIDDEN), F32),
            pltpu.VMEM((2, D_MODEL, EXPERT_HIDDEN), F32),
            pltpu.VMEM((2, EXPERT_HIDDEN, D_MODEL), F32),
            pltpu.VMEM((D_MODEL, EXPERT_HIDDEN), BF16),
            pltpu.VMEM((D_MODEL, EXPERT_HIDDEN), BF16),
            pltpu.VMEM((EXPERT_HIDDEN, D_MODEL), BF16),
            pltpu.SemaphoreType.DMA((2, 3)),
        ],
    )
    return pl.pallas_call(
        functools.partial(_ffn_kernel, layer=layer),
        grid_spec=grid_spec,
        out_shape=jax.ShapeDtypeStruct(xb.shape, F32),
        compiler_params=_cparams("arbitrary"),
        name="expert_ffn",
    )(plan.block_expert, plan.n_used, plan.parity, plan.next_expert, xb, w_gate, w_up, w_down)


def _combine_kernel(slot0_ref, slot1_ref, next0_ref, next1_ref, x_ref, slab_ref, fw_ref, yb_hbm, o_ref,
                    buf00, buf01, buf10, buf11, sems, *, tm, n_tiles, final_norm):
    i = pl.program_id(0)
    bufs = ((buf00, buf01), (buf10, buf11))
    cur = (slot0_ref, slot1_ref)

    def copies(slots, p, u):
        return [pltpu.make_async_copy(_slab(yb_hbm, slots[j][u]), _slab(bufs[p][j], u), sems.at[p])
                for j in range(TOP_K)]

    def start_all(slots, p):
        def body(u, c):
            for j, cp in enumerate(copies(slots, p, u)):
                cp.start(priority=j)
            return c
        lax.fori_loop(0, tm, body, 0, unroll=8)

    def wait_all(p):
        def body(u, c):
            for cp in copies(cur, p, u):
                cp.wait()
            return c
        lax.fori_loop(0, tm, body, 0, unroll=8)

    @pl.when(i == 0)
    def _():
        start_all(cur, 0)

    for p in range(2):
        @pl.when(i % 2 == p)
        def _():
            @pl.when(i + 1 < n_tiles)
            def _():
                start_all((next0_ref, next1_ref), 1 - p)

            wait_all(p)
            slab = slab_ref[...]
            w1 = slab[:, 2:3]
            w2 = slab[:, 3:4]
            out = x_ref[...] + (_rows_to_2d(bufs[p][0], tm) * w1 + _rows_to_2d(bufs[p][1], tm) * w2)
            if final_norm:
                out = _rms(out, fw_ref[...])
            o_ref[...] = out


def _combine(slots, x, slab, final_w, yb, final_norm, tm=512):
    n = x.shape[0]
    n_tiles = n // tm
    row = lambda i: (i, 0)
    row_buf = pltpu.VMEM((tm * ROW_SUB, LANES), F32)
    return pl.pallas_call(
        functools.partial(_combine_kernel, tm=tm, n_tiles=n_tiles, final_norm=final_norm),
        grid=(n_tiles,),
        in_specs=[
            pl.BlockSpec((tm,), lambda i: (i,), memory_space=pltpu.SMEM),
            pl.BlockSpec((tm,), lambda i: (i,), memory_space=pltpu.SMEM),
            pl.BlockSpec((tm,), lambda i: (jnp.minimum(i + 1, n_tiles - 1),), memory_space=pltpu.SMEM),
            pl.BlockSpec((tm,), lambda i: (jnp.minimum(i + 1, n_tiles - 1),), memory_space=pltpu.SMEM),
            pl.BlockSpec((tm, D_MODEL), row),
            pl.BlockSpec((tm, ROUTER_LANES), row),
            pl.BlockSpec((1, D_MODEL), lambda i: (0, 0)),
            pl.BlockSpec(memory_space=pl.ANY),
        ],
        out_specs=pl.BlockSpec((tm, D_MODEL), row),
        out_shape=jax.ShapeDtypeStruct((n, D_MODEL), F32),
        scratch_shapes=[row_buf, row_buf, row_buf, row_buf, pltpu.SemaphoreType.DMA((2,))],
        compiler_params=_cparams("arbitrary"),
        name="moe_combine",
    )(slots[0], slots[1], slots[0], slots[1], x, slab, final_w, yb)


class MoePlan(NamedTuple):
    slot: tuple
    block_expert: jax.Array
    n_used: jax.Array
    parity: jax.Array
    next_expert: jax.Array
    cap: int


def _moe_plan(plan_t, counts_row, n_tokens):
    eid = plan_t[0:TOP_K].astype(jnp.int32)
    rank = plan_t[4:4 + TOP_K].astype(jnp.int32)
    counts = counts_row[EXPERT_LANE0:EXPERT_LANE0 + N_EXPERTS].astype(jnp.int32)
    padded = (counts + MOE_ROWS - 1) // MOE_ROWS * MOE_ROWS
    pad_ends = jnp.cumsum(padded)
    pad_starts = pad_ends - padded
    expert = jnp.arange(N_EXPERTS, dtype=jnp.int32)[:, None, None]
    region = jnp.sum(jnp.where(eid[None] == expert, pad_starts[:, None, None], 0), axis=0)
    slot = region + rank
    nblk = n_tokens * TOP_K // MOE_ROWS + N_EXPERTS
    block_start = jnp.arange(nblk, dtype=jnp.int32) * MOE_ROWS
    block_expert = jnp.minimum(
        jnp.sum((pad_ends[None, :] <= block_start[:, None]).astype(jnp.int32), axis=1),
        N_EXPERTS - 1).astype(jnp.int32)
    n_used = (pad_ends[-1:] // MOE_ROWS).astype(jnp.int32)
    blk = jnp.arange(nblk, dtype=jnp.int32)
    prev_expert = jnp.concatenate([block_expert[:1], block_expert[:-1]])
    first = (blk < n_used[0]) & ((blk == 0) | (block_expert != prev_expert))
    parity = ((jnp.cumsum(first.astype(jnp.int32)) - 1) % 2).astype(jnp.int32)
    later_first = first[None, :] & (blk[None, :] > blk[:, None])
    next_first = jnp.min(jnp.where(later_first, blk[None, :], nblk), axis=1)
    next_expert = jnp.where(next_first < nblk, block_expert[jnp.minimum(next_first, nblk - 1)], -1)
    return MoePlan(tuple(slot[j] for j in range(TOP_K)), block_expert, n_used, parity,
                   next_expert.astype(jnp.int32), nblk * MOE_ROWS)


def kernel(x, attn_norm_w, w_in, shift_mix, attn_sinks, rwkv_w0, rwkv_w_up, rwkv_a0, rwkv_a_up, rwkv_g_up, rwkv_k_k, rwkv_k_a, rwkv_r_k, rwkv_ln_w, rwkv_ln_b, vres_down, vres_mix, vres_up, vres_v0, w_branch_a, w_branch_b, w_out, ffn_norm_w, router_group_w, router_group_b, router_expert_w, router_expert_b, expert_w_gate, expert_w_up, expert_w_down, final_norm_w):
    batch, seq, _ = x.shape
    n = batch * seq
    depth = w_in.shape[0]
    xf = x.reshape(n, D_MODEL)
    w_in_bf = jnp.swapaxes(w_in, 1, 2).astype(BF16)

    zero_c = jnp.zeros((depth, RWKV_WIDTH), F32)
    v0_all = jnp.concatenate([zero_c[:1], vres_v0.astype(F32)], axis=0)
    pv_pre_all = jnp.stack([rwkv_w0, rwkv_a0, rwkv_k_k, rwkv_k_a, v0_all, zero_c, zero_c, zero_c], axis=1)
    pv_post_all = jnp.stack([rwkv_ln_w, rwkv_ln_b, rwkv_r_k.reshape(depth, RWKV_WIDTH),
                             zero_c, zero_c, zero_c, zero_c, zero_c], axis=1)
    mix_r_all = shift_mix[:, None, :RKV_COLS]
    mix_l_all = jnp.pad(shift_mix[:, None, RKV_COLS:], ((0, 0), (0, 0), (0, LORA_COLS - LORA_WIDTH)))
    wup_all = jnp.pad(rwkv_w_up.astype(BF16), ((0, 0), (0, LANES - DECAY_LORA), (0, 0)))
    aup_all = jnp.pad(rwkv_a_up.astype(BF16), ((0, 0), (DECAY_LORA, LANES - DECAY_LORA - AAA_LORA), (0, 0)))
    gup_all = jnp.pad(rwkv_g_up.astype(BF16), ((0, 0), (0, 2 * LANES - GATE_LORA), (0, 0)))
    vup_all = jnp.pad(vres_up.astype(BF16), ((1, 0), (0, LANES - MV_LORA), (0, 0)))
    vd_all = jnp.pad(vres_down.astype(BF16), ((0, 0), (0, 0), (0, LANES - MV_LORA)))
    mix_v_all = jnp.pad(vres_mix, ((0, 0), (0, LANES - MV_LORA)))[:, None, :]
    router_pad = ROUTER_LANES - N_GROUPS - N_EXPERTS
    w_router_all = jnp.pad(jnp.concatenate([router_group_w, router_expert_w], axis=2),
                           ((0, 0), (0, 0), (0, router_pad)))
    b_router_all = jnp.pad(jnp.concatenate([router_group_b, router_expert_b], axis=1),
                           ((0, 0), (0, router_pad)))[:, None, :]
    attn_norm_all = attn_norm_w[:, None, :]
    ffn_norm_all = ffn_norm_w[:, None, :]
    w_out_bf = w_out.astype(BF16)
    sinks_all = attn_sinks.astype(F32)

    v_first = None
    slot_backing = None
    slot_rows = (n * TOP_K // MOE_ROWS + N_EXPERTS) * MOE_ROWS
    for i in range(depth):
        has_vres = i > 0
        mix_r, mix_l, pv_pre, pv_post = mix_r_all[i], mix_l_all[i], pv_pre_all[i], pv_post_all[i]
        wup, aup, gup, vup = wup_all[i], aup_all[i], gup_all[i], vup_all[i]
        vd, mix_v = (vd_all[i - 1], mix_v_all[i - 1]) if has_vres else (None, None)
        w_router, b_router = w_router_all[i], b_router_all[i]

        need_zeros = slot_backing is None
        proj = _inproj(i, xf, attn_norm_all[i], w_in_bf, vd,
                       zero_rows=slot_rows * ROW_SUB if need_zeros else 0)
        qkv, rkv, lora, gates = proj[:4]
        zv = proj[4] if has_vres else None
        if need_zeros:
            slot_backing = proj[-1]
        attn = _attention(qkv, sinks_all[i], seq)
        y, v_new = _rwkv_mixer(rkv, lora, v_first, zv, mix_r, mix_l, mix_v, pv_pre,
                               wup, aup, gup, vup, pv_post, seq)
        if not has_vres:
            v_first = v_new
        merged = _merge(i, attn, y, w_branch_a, w_branch_b, gates)

        x_mid, hf, slab, plan_t, counts = _outproj_router(
            merged, w_out_bf[i], xf, ffn_norm_all[i], w_router, b_router)
        plan = _moe_plan(plan_t, counts[0], n)
        xb = _dispatch(plan.slot, hf, plan.cap, slot_backing)
        yb = _expert_ffn(i, plan, xb, expert_w_gate, expert_w_up, expert_w_down)
        slot_backing = yb
        xf = _combine(plan.slot, x_mid, slab, final_norm_w.reshape(1, D_MODEL), yb, final_norm=(i == depth - 1))
    return xf.reshape(batch, seq, D_MODEL)
```

```python
import functools
import math
from typing import NamedTuple

import jax
import jax.numpy as jnp
from jax import lax
from jax.experimental import pallas as pl
from jax.experimental.pallas import tpu as pltpu

F32 = jnp.float32
BF16 = jnp.bfloat16

D_MODEL = 2048
HEAD_DIM = 64
N_Q_HEADS = 16
N_KV_HEADS = 4
GQA_GROUP = N_Q_HEADS // N_KV_HEADS
ATTN_WIDTH = N_Q_HEADS * HEAD_DIM
KV_WIDTH = N_KV_HEADS * HEAD_DIM
WINDOW = 128
ATTN_BLOCK = 128
ATTN_SUB = 4

RWKV_HEAD = 64
N_RWKV_HEADS = 16
RWKV_WIDTH = N_RWKV_HEADS * RWKV_HEAD
DECAY_LORA = 64
AAA_LORA = 64
MV_LORA = 32
GATE_LORA = 160
RWKV_GN_EPS = 64e-5
DECAY_SCALE = math.exp(-0.5)
KK_NORM_FLOOR = 1e-12
LORA_WIDTH = DECAY_LORA + AAA_LORA + GATE_LORA

N_GROUPS = 4
EXPERTS_PER_GROUP = 8
N_EXPERTS = N_GROUPS * EXPERTS_PER_GROUP
TOP_K = 2
EXPERT_HIDDEN = D_MODEL // 4
NORM_EPS = 1e-5

LANES = 128
SUBLANES = 8
MXU_COLS = 256

SEG = 512
QKV_COLS = ATTN_WIDTH + 2 * KV_WIDTH
RKV_COLS = 3 * RWKV_WIDTH
LORA_COLS = SEG
GATE_COLS = 2 * D_MODEL
QKV_TILES = QKV_COLS // SEG
RKV_TILES = RKV_COLS // SEG
LORA_TILES = LORA_COLS // SEG
GATE_TILES = GATE_COLS // SEG
IN_TILES = QKV_TILES + RKV_TILES + LORA_TILES + GATE_TILES
FIRST_GATE_TILE = QKV_TILES + RKV_TILES + LORA_TILES
ZERO_STEPS = 16
GATE_COL0 = QKV_COLS + RKV_COLS + LORA_WIDTH

SCAN_CHUNK = 64
SCAN_SUB = 2
SCAN_GROUP = 4
MOE_ROWS = 256
ROUTER_LANES = LANES
ROUTER_ROWS = 128
EXPERT_LANE0 = N_GROUPS

ROW_SUB = D_MODEL // LANES

VMEM_LIMIT = 56 * 1024 * 1024


def _rows_to_2d(slab_ref, rows):
    return jnp.concatenate([slab_ref[pl.ds(s, rows, stride=ROW_SUB), :] for s in range(ROW_SUB)], axis=1)


def _store_rows(slab_ref, val):
    rows = val.shape[0]
    for s in range(ROW_SUB):
        slab_ref[pl.ds(s, rows, stride=ROW_SUB), :] = val[:, s * LANES:(s + 1) * LANES]


def _slab(ref, row):
    return ref.at[pl.ds(pl.multiple_of(row * ROW_SUB, ROW_SUB), ROW_SUB)]


def _cparams(*sem):
    return pltpu.CompilerParams(dimension_semantics=sem, vmem_limit_bytes=VMEM_LIMIT)


def _bdot(a, b):
    return jnp.dot(a.astype(BF16), b.astype(BF16), preferred_element_type=F32)


def _bdot_nt(a, b):
    return lax.dot_general(a.astype(BF16), b.astype(BF16), (((1,), (1,)), ((), ())),
                           preferred_element_type=F32)


def _bdot_tn(a, b):
    return lax.dot_general(a.astype(BF16), b.astype(BF16), (((0,), (0,)), ((), ())),
                           preferred_element_type=F32)


def _dot_split_rhs(w01, x, passes):
    acc, rem = None, x
    for p in range(passes):
        part = rem.astype(BF16)
        d = jnp.dot(w01, part, preferred_element_type=F32)
        acc = d if acc is None else acc + d
        if p + 1 < passes:
            rem = rem - part.astype(F32)
    return acc


def _dot_split_lhs(x, w01, passes):
    acc, rem = None, x
    for p in range(passes):
        part = rem.astype(BF16)
        d = jnp.dot(part, w01, preferred_element_type=F32)
        acc = d if acc is None else acc + d
        if p + 1 < passes:
            rem = rem - part.astype(F32)
    return acc


def _rms(x, w):
    ms = jnp.mean(x * x, axis=-1, keepdims=True)
    return x * lax.rsqrt(ms + NORM_EPS) * w


def _inproj_kernel(*refs, has_vres, has_zeros):
    refs = list(refs)
    x_ref, nw_ref, w_ref, wg_ref = refs[:4]
    refs = refs[4:]
    vd_ref = refs.pop(0) if has_vres else None
    qkv_ref, rkv_ref, lora_ref, gate_ref = refs[:4]
    refs = refs[4:]
    zv_ref = refs.pop(0) if has_vres else None
    zeros_ref = refs.pop(0) if has_zeros else None
    (h_scr,) = refs
    j = pl.program_id(1)

    if has_zeros:
        zeros_ref[...] = jnp.zeros_like(zeros_ref)

    @pl.when(j == 0)
    def _():
        h_scr[...] = _rms(x_ref[...], nw_ref[...]).astype(BF16)
        if has_vres:
            zv_ref[...] = jnp.dot(h_scr[...], vd_ref[...], preferred_element_type=F32)

    def project(wt_ref, out_ref):
        for c in range(SEG // MXU_COLS):
            cs = slice(c * MXU_COLS, (c + 1) * MXU_COLS)
            acc = lax.dot_general(h_scr[...], wt_ref[cs, :], (((1,), (1,)), ((), ())),
                                  preferred_element_type=F32)
            out_ref[:, cs] = acc.astype(out_ref.dtype)

    @pl.when(j < QKV_TILES)
    def _():
        project(w_ref, qkv_ref)

    @pl.when((j >= QKV_TILES) & (j < QKV_TILES + RKV_TILES))
    def _():
        project(w_ref, rkv_ref)

    @pl.when(j == QKV_TILES + RKV_TILES)
    def _():
        project(w_ref, lora_ref)

    @pl.when(j >= FIRST_GATE_TILE)
    def _():
        project(wg_ref.at[0], gate_ref)


def _inproj(layer, x, norm_w, w_in, vd, zero_rows=0, tm=1024):
    n = x.shape[0]
    has_vres = vd is not None
    s_rkv = QKV_TILES
    in_specs = [
        pl.BlockSpec((tm, D_MODEL), lambda i, j: (i, 0)),
        pl.BlockSpec((1, D_MODEL), lambda i, j: (0, 0)),
        pl.BlockSpec((None, SEG, D_MODEL), lambda i, j: (layer, jnp.minimum(j, FIRST_GATE_TILE - 1), 0)),
        pl.BlockSpec((pl.Element(1), pl.Element(SEG), pl.Element(D_MODEL)),
                     lambda i, j: (layer, pl.multiple_of(
                         GATE_COL0 + SEG * jnp.clip(j - FIRST_GATE_TILE, 0, GATE_TILES - 1), 2 * SUBLANES), 0)),
    ]
    args = [x, norm_w, w_in, w_in]
    out_specs = [
        pl.BlockSpec((tm, SEG), lambda i, j: (i, jnp.clip(j, 0, QKV_TILES - 1))),
        pl.BlockSpec((tm, SEG), lambda i, j: (i, jnp.clip(j - s_rkv, 0, RKV_TILES - 1))),
        pl.BlockSpec((tm, SEG), lambda i, j: (i, 0)),
        pl.BlockSpec((tm, SEG), lambda i, j: (i, jnp.clip(j - FIRST_GATE_TILE, 0, GATE_TILES - 1))),
    ]
    out_shape = [
        jax.ShapeDtypeStruct((n, QKV_COLS), BF16),
        jax.ShapeDtypeStruct((n, RKV_COLS), F32),
        jax.ShapeDtypeStruct((n, LORA_COLS), F32),
        jax.ShapeDtypeStruct((n, GATE_COLS), BF16),
    ]
    if has_vres:
        in_specs.append(pl.BlockSpec((D_MODEL, LANES), lambda i, j: (0, 0)))
        args.append(vd)
        out_specs.append(pl.BlockSpec((tm, LANES), lambda i, j: (i, 0)))
        out_shape.append(jax.ShapeDtypeStruct((n, LANES), F32))
    if zero_rows:
        blk = zero_rows // ((n // tm) * ZERO_STEPS)
        assert blk * (n // tm) * ZERO_STEPS == zero_rows and blk % SUBLANES == 0
        out_specs.append(pl.BlockSpec((blk, LANES), lambda i, j: (i * ZERO_STEPS + jnp.minimum(j, ZERO_STEPS - 1), 0)))
        out_shape.append(jax.ShapeDtypeStruct((zero_rows, LANES), F32))
    return pl.pallas_call(
        functools.partial(_inproj_kernel, has_vres=has_vres, has_zeros=bool(zero_rows)),
        grid=(n // tm, IN_TILES),
        in_specs=in_specs,
        out_specs=out_specs,
        out_shape=out_shape,
        scratch_shapes=[pltpu.VMEM((tm, D_MODEL), BF16)],
        compiler_params=_cparams("parallel", "arbitrary"),
        name="inproj",
    )(*args)


def _attn_kernel(sink_ref, q_ref, kp_ref, kc_ref, vp_ref, vc_ref, bias_ref, o_ref, *, tiles_per_seq):
    first = (pl.program_id(0) % tiles_per_seq) == 0
    col = lax.broadcasted_iota(jnp.int32, (ATTN_BLOCK, 2 * ATTN_BLOCK), 1)
    pad_keys = jnp.logical_and(first, col < ATTN_BLOCK)
    scale = HEAD_DIM ** -0.5
    items = [(sb, hk) for sb in range(ATTN_SUB) for hk in range(N_KV_HEADS)]

    def window(prev_ref, cur_ref, sb, hk):
        ks = slice(hk * HEAD_DIM, (hk + 1) * HEAD_DIM)
        rows = slice(sb * ATTN_BLOCK, (sb + 1) * ATTN_BLOCK)
        prev = prev_ref[:, ks] if sb == 0 else cur_ref[(sb - 1) * ATTN_BLOCK:sb * ATTN_BLOCK, ks]
        return jnp.concatenate([prev, cur_ref[rows, ks]], axis=0)

    scores = {}
    for sb, hk in items:
        kw = window(kp_ref, kc_ref, sb, hk)
        for g in range(GQA_GROUP):
            h = hk * GQA_GROUP + g
            qh = q_ref[sb * ATTN_BLOCK:(sb + 1) * ATTN_BLOCK, h * HEAD_DIM:(h + 1) * HEAD_DIM] * scale
            s = lax.dot_general(qh, kw, (((1,), (1,)), ((), ())), preferred_element_type=F32)
            s = s + bias_ref[h]
            if sb == 0:
                s = jnp.where(pad_keys, -jnp.inf, s)
            scores[(sb, h)] = s
    row_max = {key: jnp.max(s, axis=-1, keepdims=True) for key, s in scores.items()}
    m = {key: jnp.maximum(row_max[key], sink_ref[key[1]]) for key in scores}
    p = {key: jnp.exp(scores[key] - m[key]) for key in scores}
    row_sum = {key: jnp.sum(p[key], axis=-1, keepdims=True) for key in scores}
    denoms = {key: row_sum[key] + jnp.exp(sink_ref[key[1]] - m[key]) for key in scores}
    probs = {key: p[key].astype(BF16) for key in scores}
    for sb, hk in items:
        vw = window(vp_ref, vc_ref, sb, hk)
        for g in range(GQA_GROUP):
            h = hk * GQA_GROUP + g
            o = jnp.dot(probs[(sb, h)], vw, preferred_element_type=F32) / denoms[(sb, h)]
            o_ref[sb * ATTN_BLOCK:(sb + 1) * ATTN_BLOCK, h * HEAD_DIM:(h + 1) * HEAD_DIM] = o.astype(BF16)


def _attn_bias():
    qi = jnp.arange(ATTN_BLOCK)[:, None]
    kj = jnp.arange(2 * ATTN_BLOCK)[None, :]
    dist = qi + ATTN_BLOCK - kj
    valid = (dist >= 0) & (dist < WINDOW)
    slopes = jnp.exp2(-8.0 * jnp.arange(1, N_Q_HEADS + 1, dtype=F32) / N_Q_HEADS)
    bias = -slopes[:, None, None] * dist.astype(F32)[None]
    return jnp.where(valid[None], bias, -jnp.inf)


def _attention(qkv, sinks, seq):
    n = qkv.shape[0]
    rows = ATTN_SUB * ATTN_BLOCK
    tiles_per_seq = seq // rows
    kcol = ATTN_WIDTH // KV_WIDTH
    prev = lambda i: jnp.maximum(i * ATTN_SUB - 1, 0)
    return pl.pallas_call(
        functools.partial(_attn_kernel, tiles_per_seq=tiles_per_seq),
        grid=(n // rows,),
        in_specs=[
            pl.BlockSpec(memory_space=pltpu.SMEM),
            pl.BlockSpec((rows, ATTN_WIDTH), lambda i: (i, 0)),
            pl.BlockSpec((ATTN_BLOCK, KV_WIDTH), lambda i: (prev(i), kcol)),
            pl.BlockSpec((rows, KV_WIDTH), lambda i: (i, kcol)),
            pl.BlockSpec((ATTN_BLOCK, KV_WIDTH), lambda i: (prev(i), kcol + 1)),
            pl.BlockSpec((rows, KV_WIDTH), lambda i: (i, kcol + 1)),
            pl.BlockSpec((N_Q_HEADS, ATTN_BLOCK, 2 * ATTN_BLOCK), lambda i: (0, 0, 0)),
        ],
        out_specs=pl.BlockSpec((rows, ATTN_WIDTH), lambda i: (i, 0)),
        out_shape=jax.ShapeDtypeStruct((n, ATTN_WIDTH), BF16),
        compiler_params=_cparams("parallel"),
        name="swa_attention",
    )(sinks, qkv, qkv, qkv, qkv, qkv, _attn_bias())


def _head_ones():
    ch = jnp.arange(MXU_COLS) // RWKV_HEAD
    return (ch[:, None] == ch[None, :]).astype(BF16)


def _head_sums(x, ones, passes):
    parts = [_dot_split_lhs(x[:, c:c + MXU_COLS], ones, passes) for c in range(0, RWKV_WIDTH, MXU_COLS)]
    return jnp.concatenate(parts, axis=1)


def _rwkv_kernel(*refs, has_vres):
    refs = list(refs)
    rkv_ref, lo_ref = refs[:2]
    refs = refs[2:]
    if has_vres:
        vf_ref, zv_ref, mixv_ref = refs[:3]
        refs = refs[3:]
    mixr_ref, mixl_ref, pv_ref, wup_ref, aup_ref, gup_ref, vup_ref, ones_ref, pp_ref = refs[:9]
    refs = refs[9:]
    y_ref = refs.pop(0)
    v_out = None if has_vres else refs.pop(0)
    s_scr, prev_rkv, prev_lo = refs[:3]
    prev_zv = refs[3] if has_vres else None

    L = SCAN_CHUNK
    W = SCAN_GROUP * RWKV_HEAD
    n_seq, step_rows = rkv_ref.shape[0], rkv_ref.shape[1]
    n_grp = N_RWKV_HEADS // SCAN_GROUP
    ones = ones_ref[...]

    @pl.when(pl.program_id(0) == 0)
    def _():
        s_scr[...] = jnp.zeros_like(s_scr)
        prev_rkv[...] = jnp.zeros_like(prev_rkv)
        prev_lo[...] = jnp.zeros_like(prev_lo)
        if has_vres:
            prev_zv[...] = jnp.zeros_like(prev_zv)

    def lerp_shift(z_ref, carry_ref, b, mix):
        z = z_ref[b]
        row = lax.broadcasted_iota(jnp.int32, z.shape, 0)
        back = jnp.where(row == 0, carry_ref[b, SUBLANES - 1:SUBLANES, :], pltpu.roll(z, 1, 0))
        carry_ref[b] = z_ref[b, step_rows - SUBLANES:step_rows, :]
        return z + (back - z) * mix

    w0, a0, k_k, k_a, v0 = (pv_ref[i:i + 1, :] for i in range(5))
    pre = []
    for b in range(n_seq):
        zs = lerp_shift(rkv_ref, prev_rkv, b, mixr_ref[...])
        los = lerp_shift(lo_ref, prev_lo, b, mixl_ref[...])
        r = zs[:, :RWKV_WIDTH]
        kr = zs[:, RWKV_WIDTH:2 * RWKV_WIDTH]
        vr = zs[:, 2 * RWKV_WIDTH:]
        wa = los[:, :LANES]
        col = lax.broadcasted_iota(jnp.int32, wa.shape, 1)
        wa_in = jnp.where(col < DECAY_LORA, jnp.tanh(wa), wa).astype(BF16)
        gd = los[:, LANES:3 * LANES]
        col_g = lax.broadcasted_iota(jnp.int32, gd.shape, 1)
        g_in = jnp.where(col_g < GATE_LORA, jax.nn.sigmoid(gd), gd).astype(BF16)
        dw = jnp.dot(wa_in, wup_ref[...], preferred_element_type=F32)
        da = jnp.dot(wa_in, aup_ref[...], preferred_element_type=F32)
        g = jnp.dot(g_in, gup_ref[...], preferred_element_type=F32)
        lw = -DECAY_SCALE * jax.nn.sigmoid(w0 + dw)
        a = jax.nn.sigmoid(a0 + da)
        if has_vres:
            zvs = lerp_shift(zv_ref, prev_zv, b, mixv_ref[...])
            dv = jnp.dot(zvs.astype(BF16), vup_ref[...], preferred_element_type=F32)
            vr = vr + (vf_ref[b] - vr) * jax.nn.sigmoid(v0 + dv)
        else:
            v_out[b] = vr
        kk0 = kr * k_k
        ss = _head_sums(kk0 * kk0, ones, 1)
        kk = kk0 * lax.rsqrt(jnp.maximum(ss, KK_NORM_FLOOR ** 2))
        pre.append(dict(r=r, lw=lw, k=kr * (1.0 + (a - 1.0) * k_a), v=vr, al=-kk, be=kk * a, g=g))

    row = lax.broadcasted_iota(jnp.int32, (L, L), 0)
    col = lax.broadcasted_iota(jnp.int32, (L, L), 1)
    tri = (row >= col).astype(BF16)
    t_w = lax.broadcasted_iota(jnp.int32, (L, W), 0)
    lane_w = lax.broadcasted_iota(jnp.int32, (L, W), 1)
    s_w = lane_w % RWKV_HEAD
    strict = t_w > s_w
    incl = t_w >= s_w
    eye = (t_w == s_w).astype(F32)
    blk_w = lane_w // RWKV_HEAD
    bd_mask = (lax.broadcasted_iota(jnp.int32, (W, W), 0) // RWKV_HEAD
               == lax.broadcasted_iota(jnp.int32, (W, W), 1) // RWKV_HEAD)

    def blockdiag(w):
        tiled = jnp.concatenate([w.astype(BF16)] * SCAN_GROUP, axis=0)
        return jnp.where(bd_mask, tiled, jnp.zeros_like(tiled))

    ln_w, ln_b, r_k = (pp_ref[i:i + 1, :] for i in range(3))
    items = [(b, g) for b in range(n_seq) for g in range(n_grp)]
    for sub in range(step_rows // L):
        rows = slice(sub * L, (sub + 1) * L)
        ops = {}
        for b in range(n_seq):
            p = pre[b]
            lw = p['lw'][rows]
            cum = _dot_split_rhs(tri, lw, 2)
            p_inc = jnp.exp(cum)
            p_exc = jnp.exp(cum - lw)
            p_inv = jnp.exp(-cum)
            p_last = p_inc[L - 1:L, :]
            a_t = p['al'][rows] * p_exc
            r_t = p['r'][rows] * p_inc
            b_t = p['be'][rows] * p_inv
            k_t = p['k'][rows] * p_inv
            b_end = b_t * p_last
            k_end = k_t * p_last
            v = p['v'][rows]
            for g in range(n_grp):
                gs = slice(g * W, (g + 1) * W)
                ops[(b, g)] = dict(
                    ar=jnp.concatenate([a_t[:, gs], r_t[:, gs]], axis=0),
                    bt=b_t[:, gs], kt=k_t[:, gs], v=v[:, gs], p_last=p_last[:, gs],
                    bke=jnp.concatenate([b_end[:, gs], k_end[:, gs]], axis=0))

        gm, xs, av, s0 = {}, {}, {}, {}
        for it in items:
            o = ops[it]
            rhs = jnp.concatenate([blockdiag(o['bt']), blockdiag(o['kt'])], axis=0)
            gm[it] = _bdot_nt(o['ar'], rhs)
            s0[it] = s_scr[it[0], it[1]]
            xs[it] = _bdot_nt(o['ar'], blockdiag(s0[it]))
        a_ab, a_rb = {}, {}
        for it in items:
            g = gm[it]
            a_ab[it] = jnp.where(strict, g[:L, :W], 0.0)
            a_rb[it] = jnp.where(incl, g[L:, :W], 0.0)
            a_kk = jnp.concatenate([jnp.where(strict, g[:L, W:], 0.0), jnp.where(incl, g[L:, W:], 0.0)], axis=0)
            av[it] = _bdot(a_kk, blockdiag(ops[it]['v']))

        pw = {it: _bdot(a_ab[it], blockdiag(a_ab[it])) for it in items}
        tinv = {it: eye + a_ab[it] for it in items}
        span = 2
        while 2 * span < L:
            both = {it: _bdot(jnp.concatenate([pw[it], tinv[it]], axis=0), blockdiag(pw[it])) for it in items}
            tinv = {it: tinv[it] + both[it][L:] for it in items}
            pw = {it: both[it][:L] for it in items}
            span *= 2
        tinv = {it: tinv[it] + _bdot(tinv[it], blockdiag(pw[it])) for it in items}

        u = {it: _bdot(tinv[it], blockdiag(xs[it][:L] + av[it][:L])) for it in items}
        y = {it: xs[it][L:] + av[it][L:] + _bdot(a_rb[it], blockdiag(u[it])) for it in items}
        for it in items:
            o = ops[it]
            full = _bdot_tn(jnp.concatenate([u[it], o['v']], axis=0), o['bke'])
            upd = full[:RWKV_HEAD]
            for h in range(1, SCAN_GROUP):
                upd = jnp.where(blk_w == h, full[h * RWKV_HEAD:(h + 1) * RWKV_HEAD], upd)
            s_scr[it[0], it[1]] = s0[it] * o['p_last'] + upd

        inv_n = 1.0 / RWKV_HEAD
        for b in range(n_seq):
            p = pre[b]
            yb = jnp.concatenate([y[(b, g)] for g in range(n_grp)], axis=1)
            d = yb - _head_sums(yb, ones, 1) * inv_n
            var = _head_sums(d * d, ones, 1) * inv_n
            yn = d * lax.rsqrt(var + RWKV_GN_EPS) * ln_w + ln_b
            bonus = _head_sums(p['r'][rows] * p['k'][rows] * r_k, ones, 1) * p['v'][rows]
            y_ref[b, rows, :] = ((yn + bonus) * p['g'][rows]).astype(BF16)


def _rwkv_mixer(rkv, lora, v_first, zv, mix_r, mix_l, mix_v, pv_pre, wup, aup, gup, vup, pv_post, seq):
    n = rkv.shape[0]
    n_seq = n // seq
    has_vres = v_first is not None
    step_rows = SCAN_SUB * SCAN_CHUNK
    timed = lambda cols: pl.BlockSpec((n_seq, step_rows, cols), lambda t: (0, t, 0))
    const = lambda shape: pl.BlockSpec(shape, lambda t: (0, 0))
    by_seq = lambda a: a.reshape(n_seq, seq, a.shape[-1])
    in_specs = [timed(RKV_COLS), timed(LORA_COLS)]
    args = [by_seq(rkv), by_seq(lora)]
    if has_vres:
        in_specs += [timed(RWKV_WIDTH), timed(LANES), const((1, LANES))]
        args += [by_seq(v_first), by_seq(zv), mix_v]
    in_specs += [
        const((1, RKV_COLS)), const((1, LORA_COLS)), const((SUBLANES, RWKV_WIDTH)),
        const((LANES, RWKV_WIDTH)), const((LANES, RWKV_WIDTH)), const((2 * LANES, RWKV_WIDTH)),
        const((LANES, RWKV_WIDTH)), const((MXU_COLS, MXU_COLS)), const((SUBLANES, RWKV_WIDTH)),
    ]
    args += [mix_r, mix_l, pv_pre, wup, aup, gup, vup, _head_ones(), pv_post]
    out_specs = [timed(RWKV_WIDTH)]
    out_shape = [jax.ShapeDtypeStruct((n_seq, seq, RWKV_WIDTH), BF16)]
    if not has_vres:
        out_specs.append(timed(RWKV_WIDTH))
        out_shape.append(jax.ShapeDtypeStruct((n_seq, seq, RWKV_WIDTH), F32))
    scratch = [
        pltpu.VMEM((n_seq, N_RWKV_HEADS // SCAN_GROUP, RWKV_HEAD, SCAN_GROUP * RWKV_HEAD), F32),
        pltpu.VMEM((n_seq, SUBLANES, RKV_COLS), F32),
        pltpu.VMEM((n_seq, SUBLANES, LORA_COLS), F32),
    ]
    if has_vres:
        scratch.append(pltpu.VMEM((n_seq, SUBLANES, LANES), F32))
    outs = pl.pallas_call(
        functools.partial(_rwkv_kernel, has_vres=has_vres),
        grid=(seq // step_rows,),
        in_specs=in_specs,
        out_specs=out_specs,
        out_shape=out_shape,
        scratch_shapes=scratch,
        compiler_params=_cparams("arbitrary"),
        name="rwkv_mixer",
    )(*args)
    y = outs[0].reshape(n, RWKV_WIDTH)
    return y, (None if has_vres else outs[1].reshape(n, RWKV_WIDTH))


def _merge_kernel(a_ref, y_ref, wa_ref, wb_ref, ga_ref, gb_ref, o_ref):
    for c in range(o_ref.shape[1] // MXU_COLS):
        cs = slice(c * MXU_COLS, (c + 1) * MXU_COLS)
        pa = jnp.dot(a_ref[...], wa_ref[:, cs].astype(BF16), preferred_element_type=F32)
        pb = jnp.dot(y_ref[...], wb_ref[:, cs].astype(BF16), preferred_element_type=F32)
        ga = jax.nn.sigmoid(ga_ref[:, cs].astype(F32))
        gb = jax.nn.sigmoid(gb_ref[:, cs].astype(F32))
        o_ref[:, cs] = (ga * pa + gb * pb).astype(BF16)


def _merge(layer, attn, y, wa, wb, gates, tm=1024, tn=1024):
    n = attn.shape[0]
    nj = D_MODEL // tn
    return pl.pallas_call(
        _merge_kernel,
        grid=(n // tm, nj),
        in_specs=[
            pl.BlockSpec((tm, ATTN_WIDTH), lambda i, j: (i, 0)),
            pl.BlockSpec((tm, RWKV_WIDTH), lambda i, j: (i, 0)),
            pl.BlockSpec((None, ATTN_WIDTH, tn), lambda i, j: (layer, 0, j)),
            pl.BlockSpec((None, RWKV_WIDTH, tn), lambda i, j: (layer, 0, j)),
            pl.BlockSpec((tm, tn), lambda i, j: (i, j)),
            pl.BlockSpec((tm, tn), lambda i, j: (i, j + nj)),
        ],
        out_specs=pl.BlockSpec((tm, tn), lambda i, j: (i, j)),
        out_shape=jax.ShapeDtypeStruct((n, D_MODEL), BF16),
        compiler_params=_cparams("parallel", "arbitrary"),
        name="gated_merge",
    )(attn, y, wa, wb, gates, gates)


def _outproj_router_kernel(m_ref, wo_ref, x_ref, nw_ref, wr_ref, br_ref,
                           xo_ref, hf_ref, slab_ref, plan_ref, cnt_ref, run_scr):
    @pl.when(pl.program_id(0) == 0)
    def _():
        run_scr[...] = jnp.zeros_like(run_scr)

    tm = x_ref.shape[0]
    sq = jnp.zeros((tm, LANES), F32)
    for c in range(D_MODEL // MXU_COLS):
        cs = slice(c * MXU_COLS, (c + 1) * MXU_COLS)
        xn_c = x_ref[:, cs] + jnp.dot(m_ref[...], wo_ref[:, cs], preferred_element_type=F32)
        xo_ref[:, cs] = xn_c
        for l in range(MXU_COLS // LANES):
            piece = xn_c[:, l * LANES:(l + 1) * LANES]
            sq = sq + piece * piece
    ms = jnp.sum(sq, axis=-1, keepdims=True) * (1.0 / D_MODEL)
    hf = xo_ref[...] * lax.rsqrt(ms + NORM_EPS) * nw_ref[...]
    _store_rows(hf_ref, hf)

    h_hi = hf.astype(BF16)
    h_lo = (hf - h_hi.astype(F32)).astype(BF16)
    wr = wr_ref[...]
    w_hi = wr.astype(BF16)
    w_lo = (wr - w_hi.astype(F32)).astype(BF16)
    lg = (jnp.dot(h_hi, w_hi, preferred_element_type=F32)
          + jnp.dot(h_hi, w_lo, preferred_element_type=F32)
          + jnp.dot(h_lo, w_hi, preferred_element_type=F32)) + br_ref[...]

    chunks = range(tm // ROUTER_ROWS)
    lgs = [lg[c * ROUTER_ROWS:(c + 1) * ROUTER_ROWS] for c in chunks]
    lane = lax.broadcasted_iota(jnp.int32, (ROUTER_ROWS, ROUTER_LANES), 1).astype(F32)
    neg = -jnp.inf
    big = float(ROUTER_LANES)
    is_group = lane < N_GROUPS

    def row_max(vals):
        return [jnp.max(v, axis=-1, keepdims=True) for v in vals]

    def first_index(vals, mx):
        return [jnp.min(jnp.where(v == m, lane, big), axis=-1, keepdims=True) for v, m in zip(vals, mx)]

    def row_sum(vals):
        return [jnp.sum(v, axis=-1, keepdims=True) for v in vals]

    gl = [jnp.where(is_group, v, neg) for v in lgs]
    gmax = row_max(gl)
    gsel = first_index(gl, gmax)
    gsum = row_sum([jnp.where(is_group, jnp.exp(v - m), 0.0) for v, m in zip(lgs, gmax)])
    lo_lane = [EXPERT_LANE0 + EXPERTS_PER_GROUP * g for g in gsel]
    el = [jnp.where((lane >= lo) & (lane < lo + EXPERTS_PER_GROUP), v, neg) for v, lo in zip(lgs, lo_lane)]
    v1 = row_max(el)
    i1 = first_index(el, v1)
    el2 = [jnp.where(lane == i, neg, v) for v, i in zip(el, i1)]
    v2 = row_max(el2)
    i2 = first_index(el2, v2)
    oh1 = [lane == i for i in i1]
    oh2 = [lane == i for i in i2]

    cnt = jnp.concatenate([a.astype(F32) + b.astype(F32) for a, b in zip(oh1, oh2)], axis=0)
    r_i = lax.broadcasted_iota(jnp.int32, (tm, tm), 0)
    c_i = lax.broadcasted_iota(jnp.int32, (tm, tm), 1)
    before = jnp.dot((r_i > c_i).astype(BF16), cnt.astype(BF16), preferred_element_type=F32)
    tot = before + run_scr[0:1, :]
    tots = [tot[c * ROUTER_ROWS:(c + 1) * ROUTER_ROWS] for c in chunks]
    rank1 = row_sum([jnp.where(o, t, 0.0) for o, t in zip(oh1, tots)])
    rank2 = row_sum([jnp.where(o, t, 0.0) for o, t in zip(oh2, tots)])
    run = run_scr[0:1, :] + jnp.sum(cnt, axis=0, keepdims=True)
    run_scr[...] = jnp.broadcast_to(run, run_scr.shape)
    cnt_ref[...] = jnp.broadcast_to(run, cnt_ref.shape)

    for c in chunks:
        gp = 1.0 / gsum[c]
        e21 = jnp.exp(v2[c] - v1[c])
        ew1 = gp / (1.0 + e21)
        ew2 = gp * e21 / (1.0 + e21)
        slab = jnp.where(lane == 0, i1[c] - EXPERT_LANE0,
               jnp.where(lane == 1, i2[c] - EXPERT_LANE0,
               jnp.where(lane == 2, ew1,
               jnp.where(lane == 3, ew2,
               jnp.where(lane == 4, rank1[c],
               jnp.where(lane == 5, rank2[c], 0.0))))))
        slab_ref[c * ROUTER_ROWS:(c + 1) * ROUTER_ROWS, :] = slab
        plan_ref[:, c * ROUTER_ROWS:(c + 1) * ROUTER_ROWS] = jnp.transpose(slab)[:SUBLANES, :]


def _outproj_router(merged, w_out, x, norm_w, w_router, b_router, tm=512):
    n = x.shape[0]
    row = lambda i: (i, 0)
    const = lambda i: (0, 0)
    return pl.pallas_call(
        _outproj_router_kernel,
        grid=(n // tm,),
        in_specs=[
            pl.BlockSpec((tm, D_MODEL), row),
            pl.BlockSpec((D_MODEL, D_MODEL), const),
            pl.BlockSpec((tm, D_MODEL), row),
            pl.BlockSpec((1, D_MODEL), const),
            pl.BlockSpec((D_MODEL, ROUTER_LANES), const),
            pl.BlockSpec((1, ROUTER_LANES), const),
        ],
        out_specs=[
            pl.BlockSpec((tm, D_MODEL), row),
            pl.BlockSpec((tm * ROW_SUB, LANES), row),
            pl.BlockSpec((tm, ROUTER_LANES), row),
            pl.BlockSpec((SUBLANES, tm), lambda i: (0, i)),
            pl.BlockSpec((SUBLANES, ROUTER_LANES), const),
        ],
        out_shape=[
            jax.ShapeDtypeStruct((n, D_MODEL), F32),
            jax.ShapeDtypeStruct((n * ROW_SUB, LANES), F32),
            jax.ShapeDtypeStruct((n, ROUTER_LANES), F32),
            jax.ShapeDtypeStruct((SUBLANES, n), F32),
            jax.ShapeDtypeStruct((SUBLANES, ROUTER_LANES), F32),
        ],
        scratch_shapes=[pltpu.VMEM((SUBLANES, ROUTER_LANES), F32)],
        compiler_params=_cparams("arbitrary"),
        name="outproj_router",
    )(merged, w_out, x, norm_w, w_router, b_router)


def _dispatch_kernel(slot0_ref, slot1_ref, hf_ref, xb_in_hbm, xb_hbm, sem, *, tm):
    del xb_in_hbm

    def copy(u, j):
        return pltpu.make_async_copy(_slab(hf_ref, u), _slab(xb_hbm, (slot0_ref, slot1_ref)[j][u]), sem)

    def start(u, c):
        for j in range(TOP_K):
            copy(u, j).start(priority=j)
        return c

    def wait(u, c):
        for j in range(TOP_K):
            copy(u, j).wait()
        return c

    lax.fori_loop(0, tm, start, 0, unroll=8)
    lax.fori_loop(0, tm, wait, 0, unroll=8)


def _dispatch(slots, hf, cap, backing, tm=1024):
    n = hf.shape[0] // ROW_SUB
    assert backing.shape == (cap * ROW_SUB, LANES)
    return pl.pallas_call(
        functools.partial(_dispatch_kernel, tm=tm),
        grid=(n // tm,),
        in_specs=[
            pl.BlockSpec((tm,), lambda i: (i,), memory_space=pltpu.SMEM),
            pl.BlockSpec((tm,), lambda i: (i,), memory_space=pltpu.SMEM),
            pl.BlockSpec((tm * ROW_SUB, LANES), lambda i: (i, 0)),
            pl.BlockSpec(memory_space=pl.ANY),
        ],
        out_specs=pl.BlockSpec(memory_space=pl.ANY),
        out_shape=jax.ShapeDtypeStruct((cap * ROW_SUB, LANES), F32),
        scratch_shapes=[pltpu.SemaphoreType.DMA(())],
        input_output_aliases={3: 0},
        compiler_params=_cparams("arbitrary"),
        name="moe_dispatch",
    )(slots[0], slots[1], hf, backing)


def _ffn_kernel(be_ref, nu_ref, par_ref, nxt_ref, x_ref, wg_hbm, wu_hbm, wd_hbm, o_ref,
                wg_f, wu_f, wd_f, wg_s, wu_s, wd_s, sems, *, layer):
    b = pl.program_id(0)
    used = b < nu_ref[0]
    prev_e = be_ref[jnp.maximum(b - 1, 0)]
    new_expert = jnp.logical_or(b == 0, be_ref[b] != prev_e)

    def fetch(e, p):
        return [pltpu.make_async_copy(src.at[layer, e], dst.at[p], sems.at[p, i])
                for i, (src, dst) in enumerate(((wg_hbm, wg_f), (wu_hbm, wu_f), (wd_hbm, wd_f)))]

    @pl.when(jnp.logical_and(used, new_expert))
    def _():
        for p in range(2):
            @pl.when(par_ref[b] == p)
            def _():
                @pl.when(b == 0)
                def _():
                    for cp in fetch(be_ref[0], p):
                        cp.start()

                @pl.when(nxt_ref[b] >= 0)
                def _():
                    for cp in fetch(nxt_ref[b], 1 - p):
                        cp.start()

                for cp in fetch(be_ref[b], p):
                    cp.wait()
                wg_s[...] = wg_f[p].astype(BF16)
                wu_s[...] = wu_f[p].astype(BF16)
                wd_s[...] = wd_f[p].astype(BF16)

    @pl.when(used)
    def _():
        slabs = MXU_COLS // LANES
        gt = up = None
        for kc in range(D_MODEL // MXU_COLS):
            xk = jnp.concatenate([x_ref[pl.ds(kc * slabs + s, MOE_ROWS, stride=ROW_SUB), :] for s in range(slabs)],
                                 axis=1).astype(BF16)
            ks = slice(kc * MXU_COLS, (kc + 1) * MXU_COLS)
            g_k = jnp.dot(xk, wg_s[ks, :], preferred_element_type=F32)
            u_k = jnp.dot(xk, wu_s[ks, :], preferred_element_type=F32)
            gt = g_k if gt is None else gt + g_k
            up = u_k if up is None else up + u_k
        hid = (gt * jax.nn.sigmoid(gt) * up).astype(BF16)
        for c in range(D_MODEL // MXU_COLS):
            y_c = jnp.dot(hid, wd_s[:, c * MXU_COLS:(c + 1) * MXU_COLS], preferred_element_type=F32)
            for s in range(slabs):
                o_ref[pl.ds(c * slabs + s, MOE_ROWS, stride=ROW_SUB), :] = y_c[:, s * LANES:(s + 1) * LANES]

    @pl.when(jnp.logical_not(used))
    def _():
        o_ref[...] = jnp.zeros_like(o_ref)


def _expert_ffn(layer, plan, xb, w_gate, w_up, w_down):
    nblk = xb.shape[0] // (MOE_ROWS * ROW_SUB)
    hbm = pl.BlockSpec(memory_space=pl.ANY)
    grid_spec = pltpu.PrefetchScalarGridSpec(
        num_scalar_prefetch=4,
        grid=(nblk,),
        in_specs=[
            pl.BlockSpec((MOE_ROWS * ROW_SUB, LANES), lambda b, be, nu, par, nxt: (jnp.minimum(b, nu[0] - 1), 0)),
            hbm, hbm, hbm,
        ],
        out_specs=pl.BlockSpec((MOE_ROWS * ROW_SUB, LANES), lambda b, be, nu, par, nxt: (b, 0)),
        scratch_shapes=[
            pltpu.VMEM((2, D_MODEL, EXPERT_HIDDEN), F32),
            pltpu.VMEM((2, D_MODEL, EXPERT_HIDDEN), F32),
            pltpu.VMEM((2, EXPERT_HIDDEN, D_MODEL), F32),
            pltpu.VMEM((D_MODEL, EXPERT_HIDDEN), BF16),
            pltpu.VMEM((D_MODEL, EXPERT_HIDDEN), BF16),
            pltpu.VMEM((EXPERT_HIDDEN, D_MODEL), BF16),
            pltpu.SemaphoreType.DMA((2, 3)),
        ],
    )
    return pl.pallas_call(
        functools.partial(_ffn_kernel, layer=layer),
        grid_spec=grid_spec,
        out_shape=jax.ShapeDtypeStruct(xb.shape, F32),
        compiler_params=_cparams("arbitrary"),
        name="expert_ffn",
    )(plan.block_expert, plan.n_used, plan.parity, plan.next_expert, xb, w_gate, w_up, w_down)


def _combine_kernel(slot0_ref, slot1_ref, next0_ref, next1_ref, x_ref, slab_ref, fw_ref, yb_hbm, o_ref,
                    buf00, buf01, buf10, buf11, sems, *, tm, n_tiles, final_norm):
    i = pl.program_id(0)
    bufs = ((buf00, buf01), (buf10, buf11))
    cur = (slot0_ref, slot1_ref)

    def copies(slots, p, u):
        return [pltpu.make_async_copy(_slab(yb_hbm, slots[j][u]), _slab(bufs[p][j], u), sems.at[p])
                for j in range(TOP_K)]

    def start_all(slots, p):
        def body(u, c):
            for j, cp in enumerate(copies(slots, p, u)):
                cp.start(priority=j)
            return c
        lax.fori_loop(0, tm, body, 0, unroll=8)

    def wait_all(p):
        def body(u, c):
            for cp in copies(cur, p, u):
                cp.wait()
            return c
        lax.fori_loop(0, tm, body, 0, unroll=8)

    @pl.when(i == 0)
    def _():
        start_all(cur, 0)

    for p in range(2):
        @pl.when(i % 2 == p)
        def _():
            @pl.when(i + 1 < n_tiles)
            def _():
                start_all((next0_ref, next1_ref), 1 - p)

            wait_all(p)
            slab = slab_ref[...]
            w1 = slab[:, 2:3]
            w2 = slab[:, 3:4]
            out = x_ref[...] + (_rows_to_2d(bufs[p][0], tm) * w1 + _rows_to_2d(bufs[p][1], tm) * w2)
            if final_norm:
                out = _rms(out, fw_ref[...])
            o_ref[...] = out


def _combine(slots, x, slab, final_w, yb, final_norm, tm=256):
    n = x.shape[0]
    n_tiles = n // tm
    row = lambda i: (i, 0)
    row_buf = pltpu.VMEM((tm * ROW_SUB, LANES), F32)
    return pl.pallas_call(
        functools.partial(_combine_kernel, tm=tm, n_tiles=n_tiles, final_norm=final_norm),
        grid=(n_tiles,),
        in_specs=[
            pl.BlockSpec((tm,), lambda i: (i,), memory_space=pltpu.SMEM),
            pl.BlockSpec((tm,), lambda i: (i,), memory_space=pltpu.SMEM),
            pl.BlockSpec((tm,), lambda i: (jnp.minimum(i + 1, n_tiles - 1),), memory_space=pltpu.SMEM),
            pl.BlockSpec((tm,), lambda i: (jnp.minimum(i + 1, n_tiles - 1),), memory_space=pltpu.SMEM),
            pl.BlockSpec((tm, D_MODEL), row),
            pl.BlockSpec((tm, ROUTER_LANES), row),
            pl.BlockSpec((1, D_MODEL), lambda i: (0, 0)),
            pl.BlockSpec(memory_space=pl.ANY),
        ],
        out_specs=pl.BlockSpec((tm, D_MODEL), row),
        out_shape=jax.ShapeDtypeStruct((n, D_MODEL), F32),
        scratch_shapes=[row_buf, row_buf, row_buf, row_buf, pltpu.SemaphoreType.DMA((2,))],
        compiler_params=_cparams("arbitrary"),
        name="moe_combine",
    )(slots[0], slots[1], slots[0], slots[1], x, slab, final_w, yb)


class MoePlan(NamedTuple):
    slot: tuple
    block_expert: jax.Array
    n_used: jax.Array
    parity: jax.Array
    next_expert: jax.Array
    cap: int


def _moe_plan(plan_t, counts_row, n_tokens):
    eid = plan_t[0:TOP_K].astype(jnp.int32)
    rank = plan_t[4:4 + TOP_K].astype(jnp.int32)
    counts = counts_row[EXPERT_LANE0:EXPERT_LANE0 + N_EXPERTS].astype(jnp.int32)
    padded = (counts + MOE_ROWS - 1) // MOE_ROWS * MOE_ROWS
    pad_ends = jnp.cumsum(padded)
    pad_starts = pad_ends - padded
    expert = jnp.arange(N_EXPERTS, dtype=jnp.int32)[:, None, None]
    region = jnp.sum(jnp.where(eid[None] == expert, pad_starts[:, None, None], 0), axis=0)
    slot = region + rank
    nblk = n_tokens * TOP_K // MOE_ROWS + N_EXPERTS
    block_start = jnp.arange(nblk, dtype=jnp.int32) * MOE_ROWS
    block_expert = jnp.minimum(
        jnp.sum((pad_ends[None, :] <= block_start[:, None]).astype(jnp.int32), axis=1),
        N_EXPERTS - 1).astype(jnp.int32)
    n_used = (pad_ends[-1:] // MOE_ROWS).astype(jnp.int32)
    blk = jnp.arange(nblk, dtype=jnp.int32)
    prev_expert = jnp.concatenate([block_expert[:1], block_expert[:-1]])
    first = (blk < n_used[0]) & ((blk == 0) | (block_expert != prev_expert))
    parity = ((jnp.cumsum(first.astype(jnp.int32)) - 1) % 2).astype(jnp.int32)
    later_first = first[None, :] & (blk[None, :] > blk[:, None])
    next_first = jnp.min(jnp.where(later_first, blk[None, :], nblk), axis=1)
    next_expert = jnp.where(next_first < nblk, block_expert[jnp.minimum(next_first, nblk - 1)], -1)
    return MoePlan(tuple(slot[j] for j in range(TOP_K)), block_expert, n_used, parity,
                   next_expert.astype(jnp.int32), nblk * MOE_ROWS)


def kernel(x, attn_norm_w, w_in, shift_mix, attn_sinks, rwkv_w0, rwkv_w_up, rwkv_a0, rwkv_a_up, rwkv_g_up, rwkv_k_k, rwkv_k_a, rwkv_r_k, rwkv_ln_w, rwkv_ln_b, vres_down, vres_mix, vres_up, vres_v0, w_branch_a, w_branch_b, w_out, ffn_norm_w, router_group_w, router_group_b, router_expert_w, router_expert_b, expert_w_gate, expert_w_up, expert_w_down, final_norm_w):
    batch, seq, _ = x.shape
    n = batch * seq
    depth = w_in.shape[0]
    xf = x.reshape(n, D_MODEL)
    w_in_bf = jnp.swapaxes(w_in, 1, 2).astype(BF16)

    zero_c = jnp.zeros((depth, RWKV_WIDTH), F32)
    v0_all = jnp.concatenate([zero_c[:1], vres_v0.astype(F32)], axis=0)
    pv_pre_all = jnp.stack([rwkv_w0, rwkv_a0, rwkv_k_k, rwkv_k_a, v0_all, zero_c, zero_c, zero_c], axis=1)
    pv_post_all = jnp.stack([rwkv_ln_w, rwkv_ln_b, rwkv_r_k.reshape(depth, RWKV_WIDTH),
                             zero_c, zero_c, zero_c, zero_c, zero_c], axis=1)
    mix_r_all = shift_mix[:, None, :RKV_COLS]
    mix_l_all = jnp.pad(shift_mix[:, None, RKV_COLS:], ((0, 0), (0, 0), (0, LORA_COLS - LORA_WIDTH)))
    wup_all = jnp.pad(rwkv_w_up.astype(BF16), ((0, 0), (0, LANES - DECAY_LORA), (0, 0)))
    aup_all = jnp.pad(rwkv_a_up.astype(BF16), ((0, 0), (DECAY_LORA, LANES - DECAY_LORA - AAA_LORA), (0, 0)))
    gup_all = jnp.pad(rwkv_g_up.astype(BF16), ((0, 0), (0, 2 * LANES - GATE_LORA), (0, 0)))
    vup_all = jnp.pad(vres_up.astype(BF16), ((1, 0), (0, LANES - MV_LORA), (0, 0)))
    vd_all = jnp.pad(vres_down.astype(BF16), ((0, 0), (0, 0), (0, LANES - MV_LORA)))
    mix_v_all = jnp.pad(vres_mix, ((0, 0), (0, LANES - MV_LORA)))[:, None, :]
    router_pad = ROUTER_LANES - N_GROUPS - N_EXPERTS
    w_router_all = jnp.pad(jnp.concatenate([router_group_w, router_expert_w], axis=2),
                           ((0, 0), (0, 0), (0, router_pad)))
    b_router_all = jnp.pad(jnp.concatenate([router_group_b, router_expert_b], axis=1),
                           ((0, 0), (0, router_pad)))[:, None, :]
    attn_norm_all = attn_norm_w[:, None, :]
    ffn_norm_all = ffn_norm_w[:, None, :]
    w_out_bf = w_out.astype(BF16)
    sinks_all = attn_sinks.astype(F32)

    v_first = None
    slot_backing = None
    slot_rows = (n * TOP_K // MOE_ROWS + N_EXPERTS) * MOE_ROWS
    for i in range(depth):
        has_vres = i > 0
        mix_r, mix_l, pv_pre, pv_post = mix_r_all[i], mix_l_all[i], pv_pre_all[i], pv_post_all[i]
        wup, aup, gup, vup = wup_all[i], aup_all[i], gup_all[i], vup_all[i]
        vd, mix_v = (vd_all[i - 1], mix_v_all[i - 1]) if has_vres else (None, None)
        w_router, b_router = w_router_all[i], b_router_all[i]

        need_zeros = slot_backing is None
        proj = _inproj(i, xf, attn_norm_all[i], w_in_bf, vd,
                       zero_rows=slot_rows * ROW_SUB if need_zeros else 0)
        qkv, rkv, lora, gates = proj[:4]
        zv = proj[4] if has_vres else None
        if need_zeros:
            slot_backing = proj[-1]
        attn = _attention(qkv, sinks_all[i], seq)
        y, v_new = _rwkv_mixer(rkv, lora, v_first, zv, mix_r, mix_l, mix_v, pv_pre,
                               wup, aup, gup, vup, pv_post, seq)
        if not has_vres:
            v_first = v_new
        merged = _merge(i, attn, y, w_branch_a, w_branch_b, gates)

        x_mid, hf, slab, plan_t, counts = _outproj_router(
            merged, w_out_bf[i], xf, ffn_norm_all[i], w_router, b_router)
        plan = _moe_plan(plan_t, counts[0], n)
        xb = _dispatch(plan.slot, hf, plan.cap, slot_backing)
        yb = _expert_ffn(i, plan, xb, expert_w_gate, expert_w_up, expert_w_down)
        slot_backing = yb
        xf = _combine(plan.slot, x_mid, slab, final_norm_w.reshape(1, D_MODEL), yb, final_norm=(i == depth - 1))
    return xf.reshape(batch, seq, D_MODEL)
```

```python
import functools
import math
from typing import NamedTuple

import jax
import jax.numpy as jnp
from jax import lax
from jax.experimental import pallas as pl
from jax.experimental.pallas import tpu as pltpu

F32 = jnp.float32
BF16 = jnp.bfloat16

D_MODEL = 2048
HEAD_DIM = 64
N_Q_HEADS = 16
N_KV_HEADS = 4
GQA_GROUP = N_Q_HEADS // N_KV_HEADS
ATTN_WIDTH = N_Q_HEADS * HEAD_DIM
KV_WIDTH = N_KV_HEADS * HEAD_DIM
WINDOW = 128
ATTN_BLOCK = 128
ATTN_SUB = 4

RWKV_HEAD = 64
N_RWKV_HEADS = 16
RWKV_WIDTH = N_RWKV_HEADS * RWKV_HEAD
DECAY_LORA = 64
AAA_LORA = 64
MV_LORA = 32
GATE_LORA = 160
RWKV_GN_EPS = 64e-5
DECAY_SCALE = math.exp(-0.5)
KK_NORM_FLOOR = 1e-12
LORA_WIDTH = DECAY_LORA + AAA_LORA + GATE_LORA

N_GROUPS = 4
EXPERTS_PER_GROUP = 8
N_EXPERTS = N_GROUPS * EXPERTS_PER_GROUP
TOP_K = 2
EXPERT_HIDDEN = D_MODEL // 4
NORM_EPS = 1e-5

LANES = 128
SUBLANES = 8
MXU_COLS = 256

SEG = 512
QKV_COLS = ATTN_WIDTH + 2 * KV_WIDTH
RKV_COLS = 3 * RWKV_WIDTH
LORA_COLS = SEG
GATE_COLS = 2 * D_MODEL
QKV_TILES = QKV_COLS // SEG
RKV_TILES = RKV_COLS // SEG
LORA_TILES = LORA_COLS // SEG
GATE_TILES = GATE_COLS // SEG
IN_TILES = QKV_TILES + RKV_TILES + LORA_TILES + GATE_TILES
FIRST_GATE_TILE = QKV_TILES + RKV_TILES + LORA_TILES
ZERO_STEPS = 16
GATE_COL0 = QKV_COLS + RKV_COLS + LORA_WIDTH

SCAN_CHUNK = 64
SCAN_SUB = 2
SCAN_GROUP = 4
MOE_ROWS = 256
ROUTER_LANES = LANES
ROUTER_ROWS = 128
EXPERT_LANE0 = N_GROUPS

ROW_SUB = D_MODEL // LANES

VMEM_LIMIT = 56 * 1024 * 1024


def _rows_to_2d(slab_ref, rows):
    return jnp.concatenate([slab_ref[pl.ds(s, rows, stride=ROW_SUB), :] for s in range(ROW_SUB)], axis=1)


def _store_rows(slab_ref, val):
    rows = val.shape[0]
    for s in range(ROW_SUB):
        slab_ref[pl.ds(s, rows, stride=ROW_SUB), :] = val[:, s * LANES:(s + 1) * LANES]


def _slab(ref, row):
    return ref.at[pl.ds(pl.multiple_of(row * ROW_SUB, ROW_SUB), ROW_SUB)]


def _cparams(*sem):
    return pltpu.CompilerParams(dimension_semantics=sem, vmem_limit_bytes=VMEM_LIMIT)


def _bdot(a, b):
    return jnp.dot(a.astype(BF16), b.astype(BF16), preferred_element_type=F32)


def _bdot_nt(a, b):
    return lax.dot_general(a.astype(BF16), b.astype(BF16), (((1,), (1,)), ((), ())),
                           preferred_element_type=F32)


def _bdot_tn(a, b):
    return lax.dot_general(a.astype(BF16), b.astype(BF16), (((0,), (0,)), ((), ())),
                           preferred_element_type=F32)


def _dot_split_rhs(w01, x, passes):
    acc, rem = None, x
    for p in range(passes):
        part = rem.astype(BF16)
        d = jnp.dot(w01, part, preferred_element_type=F32)
        acc = d if acc is None else acc + d
        if p + 1 < passes:
            rem = rem - part.astype(F32)
    return acc


def _dot_split_lhs(x, w01, passes):
    acc, rem = None, x
    for p in range(passes):
        part = rem.astype(BF16)
        d = jnp.dot(part, w01, preferred_element_type=F32)
        acc = d if acc is None else acc + d
        if p + 1 < passes:
            rem = rem - part.astype(F32)
    return acc


def _rms(x, w):
    ms = jnp.mean(x * x, axis=-1, keepdims=True)
    return x * lax.rsqrt(ms + NORM_EPS) * w


def _inproj_kernel(*refs, has_vres, has_zeros):
    refs = list(refs)
    x_ref, nw_ref, w_ref, wg_ref = refs[:4]
    refs = refs[4:]
    vd_ref = refs.pop(0) if has_vres else None
    qkv_ref, rkv_ref, lora_ref, gate_ref = refs[:4]
    refs = refs[4:]
    zv_ref = refs.pop(0) if has_vres else None
    zeros_ref = refs.pop(0) if has_zeros else None
    (h_scr,) = refs
    j = pl.program_id(1)

    if has_zeros:
        zeros_ref[...] = jnp.zeros_like(zeros_ref)

    @pl.when(j == 0)
    def _():
        h_scr[...] = _rms(x_ref[...], nw_ref[...]).astype(BF16)
        if has_vres:
            zv_ref[...] = jnp.dot(h_scr[...], vd_ref[...], preferred_element_type=F32)

    def project(wt_ref, out_ref):
        for c in range(SEG // MXU_COLS):
            cs = slice(c * MXU_COLS, (c + 1) * MXU_COLS)
            acc = lax.dot_general(h_scr[...], wt_ref[cs, :], (((1,), (1,)), ((), ())),
                                  preferred_element_type=F32)
            out_ref[:, cs] = acc.astype(out_ref.dtype)

    @pl.when(j < QKV_TILES)
    def _():
        project(w_ref, qkv_ref)

    @pl.when((j >= QKV_TILES) & (j < QKV_TILES + RKV_TILES))
    def _():
        project(w_ref, rkv_ref)

    @pl.when(j == QKV_TILES + RKV_TILES)
    def _():
        project(w_ref, lora_ref)

    @pl.when(j >= FIRST_GATE_TILE)
    def _():
        project(wg_ref.at[0], gate_ref)


def _inproj(layer, x, norm_w, w_in, vd, zero_rows=0, tm=1024):
    n = x.shape[0]
    has_vres = vd is not None
    s_rkv = QKV_TILES
    in_specs = [
        pl.BlockSpec((tm, D_MODEL), lambda i, j: (i, 0)),
        pl.BlockSpec((1, D_MODEL), lambda i, j: (0, 0)),
        pl.BlockSpec((None, SEG, D_MODEL), lambda i, j: (layer, jnp.minimum(j, FIRST_GATE_TILE - 1), 0)),
        pl.BlockSpec((pl.Element(1), pl.Element(SEG), pl.Element(D_MODEL)),
                     lambda i, j: (layer, pl.multiple_of(
                         GATE_COL0 + SEG * jnp.clip(j - FIRST_GATE_TILE, 0, GATE_TILES - 1), 2 * SUBLANES), 0)),
    ]
    args = [x, norm_w, w_in, w_in]
    out_specs = [
        pl.BlockSpec((tm, SEG), lambda i, j: (i, jnp.clip(j, 0, QKV_TILES - 1))),
        pl.BlockSpec((tm, SEG), lambda i, j: (i, jnp.clip(j - s_rkv, 0, RKV_TILES - 1))),
        pl.BlockSpec((tm, SEG), lambda i, j: (i, 0)),
        pl.BlockSpec((tm, SEG), lambda i, j: (i, jnp.clip(j - FIRST_GATE_TILE, 0, GATE_TILES - 1))),
    ]
    out_shape = [
        jax.ShapeDtypeStruct((n, QKV_COLS), BF16),
        jax.ShapeDtypeStruct((n, RKV_COLS), F32),
        jax.ShapeDtypeStruct((n, LORA_COLS), F32),
        jax.ShapeDtypeStruct((n, GATE_COLS), BF16),
    ]
    if has_vres:
        in_specs.append(pl.BlockSpec((D_MODEL, LANES), lambda i, j: (0, 0)))
        args.append(vd)
        out_specs.append(pl.BlockSpec((tm, LANES), lambda i, j: (i, 0)))
        out_shape.append(jax.ShapeDtypeStruct((n, LANES), F32))
    if zero_rows:
        blk = zero_rows // ((n // tm) * ZERO_STEPS)
        assert blk * (n // tm) * ZERO_STEPS == zero_rows and blk % SUBLANES == 0
        out_specs.append(pl.BlockSpec((blk, LANES), lambda i, j: (i * ZERO_STEPS + jnp.minimum(j, ZERO_STEPS - 1), 0)))
        out_shape.append(jax.ShapeDtypeStruct((zero_rows, LANES), F32))
    return pl.pallas_call(
        functools.partial(_inproj_kernel, has_vres=has_vres, has_zeros=bool(zero_rows)),
        grid=(n // tm, IN_TILES),
        in_specs=in_specs,
        out_specs=out_specs,
        out_shape=out_shape,
        scratch_shapes=[pltpu.VMEM((tm, D_MODEL), BF16)],
        compiler_params=_cparams("parallel", "arbitrary"),
        name="inproj",
    )(*args)


def _attn_kernel(sink_ref, q_ref, kp_ref, kc_ref, vp_ref, vc_ref, bias_ref, o_ref, *, tiles_per_seq):
    first = (pl.program_id(0) % tiles_per_seq) == 0
    col = lax.broadcasted_iota(jnp.int32, (ATTN_BLOCK, 2 * ATTN_BLOCK), 1)
    pad_keys = jnp.logical_and(first, col < ATTN_BLOCK)
    scale = HEAD_DIM ** -0.5
    items = [(sb, hk) for sb in range(ATTN_SUB) for hk in range(N_KV_HEADS)]

    def window(prev_ref, cur_ref, sb, hk):
        ks = slice(hk * HEAD_DIM, (hk + 1) * HEAD_DIM)
        rows = slice(sb * ATTN_BLOCK, (sb + 1) * ATTN_BLOCK)
        prev = prev_ref[:, ks] if sb == 0 else cur_ref[(sb - 1) * ATTN_BLOCK:sb * ATTN_BLOCK, ks]
        return jnp.concatenate([prev, cur_ref[rows, ks]], axis=0)

    scores = {}
    for sb, hk in items:
        kw = window(kp_ref, kc_ref, sb, hk)
        for g in range(GQA_GROUP):
            h = hk * GQA_GROUP + g
            qh = q_ref[sb * ATTN_BLOCK:(sb + 1) * ATTN_BLOCK, h * HEAD_DIM:(h + 1) * HEAD_DIM] * scale
            s = lax.dot_general(qh, kw, (((1,), (1,)), ((), ())), preferred_element_type=F32)
            s = s + bias_ref[h]
            if sb == 0:
                s = jnp.where(pad_keys, -jnp.inf, s)
            scores[(sb, h)] = s
    row_max = {key: jnp.max(s, axis=-1, keepdims=True) for key, s in scores.items()}
    m = {key: jnp.maximum(row_max[key], sink_ref[key[1]]) for key in scores}
    p = {key: jnp.exp(scores[key] - m[key]) for key in scores}
    row_sum = {key: jnp.sum(p[key], axis=-1, keepdims=True) for key in scores}
    denoms = {key: row_sum[key] + jnp.exp(sink_ref[key[1]] - m[key]) for key in scores}
    probs = {key: p[key].astype(BF16) for key in scores}
    for sb, hk in items:
        vw = window(vp_ref, vc_ref, sb, hk)
        for g in range(GQA_GROUP):
            h = hk * GQA_GROUP + g
            o = jnp.dot(probs[(sb, h)], vw, preferred_element_type=F32) / denoms[(sb, h)]
            o_ref[sb * ATTN_BLOCK:(sb + 1) * ATTN_BLOCK, h * HEAD_DIM:(h + 1) * HEAD_DIM] = o.astype(BF16)


def _attn_bias():
    qi = jnp.arange(ATTN_BLOCK)[:, None]
    kj = jnp.arange(2 * ATTN_BLOCK)[None, :]
    dist = qi + ATTN_BLOCK - kj
    valid = (dist >= 0) & (dist < WINDOW)
    slopes = jnp.exp2(-8.0 * jnp.arange(1, N_Q_HEADS + 1, dtype=F32) / N_Q_HEADS)
    bias = -slopes[:, None, None] * dist.astype(F32)[None]
    return jnp.where(valid[None], bias, -jnp.inf)


def _attention(qkv, sinks, seq):
    n = qkv.shape[0]
    rows = ATTN_SUB * ATTN_BLOCK
    tiles_per_seq = seq // rows
    kcol = ATTN_WIDTH // KV_WIDTH
    prev = lambda i: jnp.maximum(i * ATTN_SUB - 1, 0)
    return pl.pallas_call(
        functools.partial(_attn_kernel, tiles_per_seq=tiles_per_seq),
        grid=(n // rows,),
        in_specs=[
            pl.BlockSpec(memory_space=pltpu.SMEM),
            pl.BlockSpec((rows, ATTN_WIDTH), lambda i: (i, 0)),
            pl.BlockSpec((ATTN_BLOCK, KV_WIDTH), lambda i: (prev(i), kcol)),
            pl.BlockSpec((rows, KV_WIDTH), lambda i: (i, kcol)),
            pl.BlockSpec((ATTN_BLOCK, KV_WIDTH), lambda i: (prev(i), kcol + 1)),
            pl.BlockSpec((rows, KV_WIDTH), lambda i: (i, kcol + 1)),
            pl.BlockSpec((N_Q_HEADS, ATTN_BLOCK, 2 * ATTN_BLOCK), lambda i: (0, 0, 0)),
        ],
        out_specs=pl.BlockSpec((rows, ATTN_WIDTH), lambda i: (i, 0)),
        out_shape=jax.ShapeDtypeStruct((n, ATTN_WIDTH), BF16),
        compiler_params=_cparams("parallel"),
        name="swa_attention",
    )(sinks, qkv, qkv, qkv, qkv, qkv, _attn_bias())


def _head_ones():
    ch = jnp.arange(MXU_COLS) // RWKV_HEAD
    return (ch[:, None] == ch[None, :]).astype(BF16)


def _head_sums(x, ones, passes):
    parts = [_dot_split_lhs(x[:, c:c + MXU_COLS], ones, passes) for c in range(0, RWKV_WIDTH, MXU_COLS)]
    return jnp.concatenate(parts, axis=1)


def _rwkv_kernel(*refs, has_vres):
    refs = list(refs)
    rkv_ref, lo_ref = refs[:2]
    refs = refs[2:]
    if has_vres:
        vf_ref, zv_ref, mixv_ref = refs[:3]
        refs = refs[3:]
    mixr_ref, mixl_ref, pv_ref, wup_ref, aup_ref, gup_ref, vup_ref, ones_ref, pp_ref = refs[:9]
    refs = refs[9:]
    y_ref = refs.pop(0)
    v_out = None if has_vres else refs.pop(0)
    s_scr, prev_rkv, prev_lo = refs[:3]
    prev_zv = refs[3] if has_vres else None

    L = SCAN_CHUNK
    W = SCAN_GROUP * RWKV_HEAD
    n_seq, step_rows = rkv_ref.shape[0], rkv_ref.shape[1]
    n_grp = N_RWKV_HEADS // SCAN_GROUP
    ones = ones_ref[...]

    @pl.when(pl.program_id(0) == 0)
    def _():
        s_scr[...] = jnp.zeros_like(s_scr)
        prev_rkv[...] = jnp.zeros_like(prev_rkv)
        prev_lo[...] = jnp.zeros_like(prev_lo)
        if has_vres:
            prev_zv[...] = jnp.zeros_like(prev_zv)

    def lerp_shift(z_ref, carry_ref, b, mix):
        z = z_ref[b]
        row = lax.broadcasted_iota(jnp.int32, z.shape, 0)
        back = jnp.where(row == 0, carry_ref[b, SUBLANES - 1:SUBLANES, :], pltpu.roll(z, 1, 0))
        carry_ref[b] = z_ref[b, step_rows - SUBLANES:step_rows, :]
        return z + (back - z) * mix

    w0, a0, k_k, k_a, v0 = (pv_ref[i:i + 1, :] for i in range(5))
    pre = []
    for b in range(n_seq):
        zs = lerp_shift(rkv_ref, prev_rkv, b, mixr_ref[...])
        los = lerp_shift(lo_ref, prev_lo, b, mixl_ref[...])
        r = zs[:, :RWKV_WIDTH]
        kr = zs[:, RWKV_WIDTH:2 * RWKV_WIDTH]
        vr = zs[:, 2 * RWKV_WIDTH:]
        wa = los[:, :LANES]
        col = lax.broadcasted_iota(jnp.int32, wa.shape, 1)
        wa_in = jnp.where(col < DECAY_LORA, jnp.tanh(wa), wa).astype(BF16)
        gd = los[:, LANES:3 * LANES]
        col_g = lax.broadcasted_iota(jnp.int32, gd.shape, 1)
        g_in = jnp.where(col_g < GATE_LORA, jax.nn.sigmoid(gd), gd).astype(BF16)
        dw = jnp.dot(wa_in, wup_ref[...], preferred_element_type=F32)
        da = jnp.dot(wa_in, aup_ref[...], preferred_element_type=F32)
        g = jnp.dot(g_in, gup_ref[...], preferred_element_type=F32)
        lw = -DECAY_SCALE * jax.nn.sigmoid(w0 + dw)
        a = jax.nn.sigmoid(a0 + da)
        if has_vres:
            zvs = lerp_shift(zv_ref, prev_zv, b, mixv_ref[...])
            dv = jnp.dot(zvs.astype(BF16), vup_ref[...], preferred_element_type=F32)
            vr = vr + (vf_ref[b] - vr) * jax.nn.sigmoid(v0 + dv)
        else:
            v_out[b] = vr
        kk0 = kr * k_k
        ss = _head_sums(kk0 * kk0, ones, 1)
        kk = kk0 * lax.rsqrt(jnp.maximum(ss, KK_NORM_FLOOR ** 2))
        pre.append(dict(r=r, lw=lw, k=kr * (1.0 + (a - 1.0) * k_a), v=vr, al=-kk, be=kk * a, g=g))

    row = lax.broadcasted_iota(jnp.int32, (L, L), 0)
    col = lax.broadcasted_iota(jnp.int32, (L, L), 1)
    tri = (row >= col).astype(BF16)
    t_w = lax.broadcasted_iota(jnp.int32, (L, W), 0)
    lane_w = lax.broadcasted_iota(jnp.int32, (L, W), 1)
    s_w = lane_w % RWKV_HEAD
    strict = t_w > s_w
    incl = t_w >= s_w
    eye = (t_w == s_w).astype(F32)
    blk_w = lane_w // RWKV_HEAD
    bd_mask = (lax.broadcasted_iota(jnp.int32, (W, W), 0) // RWKV_HEAD
               == lax.broadcasted_iota(jnp.int32, (W, W), 1) // RWKV_HEAD)

    def blockdiag(w):
        tiled = jnp.concatenate([w.astype(BF16)] * SCAN_GROUP, axis=0)
        return jnp.where(bd_mask, tiled, jnp.zeros_like(tiled))

    ln_w, ln_b, r_k = (pp_ref[i:i + 1, :] for i in range(3))
    items = [(b, g) for b in range(n_seq) for g in range(n_grp)]
    for sub in range(step_rows // L):
        rows = slice(sub * L, (sub + 1) * L)
        ops = {}
        for b in range(n_seq):
            p = pre[b]
            lw = p['lw'][rows]
            cum = _dot_split_rhs(tri, lw, 2)
            p_inc = jnp.exp(cum)
            p_exc = jnp.exp(cum - lw)
            p_inv = jnp.exp(-cum)
            p_last = p_inc[L - 1:L, :]
            a_t = p['al'][rows] * p_exc
            r_t = p['r'][rows] * p_inc
            b_t = p['be'][rows] * p_inv
            k_t = p['k'][rows] * p_inv
            b_end = b_t * p_last
            k_end = k_t * p_last
            v = p['v'][rows]
            for g in range(n_grp):
                gs = slice(g * W, (g + 1) * W)
                ops[(b, g)] = dict(
                    ar=jnp.concatenate([a_t[:, gs], r_t[:, gs]], axis=0),
                    bt=b_t[:, gs], kt=k_t[:, gs], v=v[:, gs], p_last=p_last[:, gs],
                    bke=jnp.concatenate([b_end[:, gs], k_end[:, gs]], axis=0))

        gm, xs, av, s0 = {}, {}, {}, {}
        for it in items:
            o = ops[it]
            rhs = jnp.concatenate([blockdiag(o['bt']), blockdiag(o['kt'])], axis=0)
            gm[it] = _bdot_nt(o['ar'], rhs)
            s0[it] = s_scr[it[0], it[1]]
            xs[it] = _bdot_nt(o['ar'], blockdiag(s0[it]))
        a_ab, a_rb = {}, {}
        for it in items:
            g = gm[it]
            a_ab[it] = jnp.where(strict, g[:L, :W], 0.0)
            a_rb[it] = jnp.where(incl, g[L:, :W], 0.0)
            a_kk = jnp.concatenate([jnp.where(strict, g[:L, W:], 0.0), jnp.where(incl, g[L:, W:], 0.0)], axis=0)
            av[it] = _bdot(a_kk, blockdiag(ops[it]['v']))

        pw = {it: _bdot(a_ab[it], blockdiag(a_ab[it])) for it in items}
        tinv = {it: eye + a_ab[it] for it in items}
        span = 2
        while 2 * span < L:
            both = {it: _bdot(jnp.concatenate([pw[it], tinv[it]], axis=0), blockdiag(pw[it])) for it in items}
            tinv = {it: tinv[it] + both[it][L:] for it in items}
            pw = {it: both[it][:L] for it in items}
            span *= 2
        tinv = {it: tinv[it] + _bdot(tinv[it], blockdiag(pw[it])) for it in items}

        u = {it: _bdot(tinv[it], blockdiag(xs[it][:L] + av[it][:L])) for it in items}
        y = {it: xs[it][L:] + av[it][L:] + _bdot(a_rb[it], blockdiag(u[it])) for it in items}
        for it in items:
            o = ops[it]
            full = _bdot_tn(jnp.concatenate([u[it], o['v']], axis=0), o['bke'])
            upd = full[:RWKV_HEAD]
            for h in range(1, SCAN_GROUP):
                upd = jnp.where(blk_w == h, full[h * RWKV_HEAD:(h + 1) * RWKV_HEAD], upd)
            s_scr[it[0], it[1]] = s0[it] * o['p_last'] + upd

        inv_n = 1.0 / RWKV_HEAD
        for b in range(n_seq):
            p = pre[b]
            yb = jnp.concatenate([y[(b, g)] for g in range(n_grp)], axis=1)
            d = yb - _head_sums(yb, ones, 1) * inv_n
            var = _head_sums(d * d, ones, 1) * inv_n
            yn = d * lax.rsqrt(var + RWKV_GN_EPS) * ln_w + ln_b
            bonus = _head_sums(p['r'][rows] * p['k'][rows] * r_k, ones, 1) * p['v'][rows]
            y_ref[b, rows, :] = ((yn + bonus) * p['g'][rows]).astype(BF16)


def _rwkv_mixer(rkv, lora, v_first, zv, mix_r, mix_l, mix_v, pv_pre, wup, aup, gup, vup, pv_post, seq):
    n = rkv.shape[0]
    n_seq = n // seq
    has_vres = v_first is not None
    step_rows = SCAN_SUB * SCAN_CHUNK
    timed = lambda cols: pl.BlockSpec((n_seq, step_rows, cols), lambda t: (0, t, 0))
    const = lambda shape: pl.BlockSpec(shape, lambda t: (0, 0))
    by_seq = lambda a: a.reshape(n_seq, seq, a.shape[-1])
    in_specs = [timed(RKV_COLS), timed(LORA_COLS)]
    args = [by_seq(rkv), by_seq(lora)]
    if has_vres:
        in_specs += [timed(RWKV_WIDTH), timed(LANES), const((1, LANES))]
        args += [by_seq(v_first), by_seq(zv), mix_v]
    in_specs += [
        const((1, RKV_COLS)), const((1, LORA_COLS)), const((SUBLANES, RWKV_WIDTH)),
        const((LANES, RWKV_WIDTH)), const((LANES, RWKV_WIDTH)), const((2 * LANES, RWKV_WIDTH)),
        const((LANES, RWKV_WIDTH)), const((MXU_COLS, MXU_COLS)), const((SUBLANES, RWKV_WIDTH)),
    ]
    args += [mix_r, mix_l, pv_pre, wup, aup, gup, vup, _head_ones(), pv_post]
    out_specs = [timed(RWKV_WIDTH)]
    out_shape = [jax.ShapeDtypeStruct((n_seq, seq, RWKV_WIDTH), BF16)]
    if not has_vres:
        out_specs.append(timed(RWKV_WIDTH))
        out_shape.append(jax.ShapeDtypeStruct((n_seq, seq, RWKV_WIDTH), F32))
    scratch = [
        pltpu.VMEM((n_seq, N_RWKV_HEADS // SCAN_GROUP, RWKV_HEAD, SCAN_GROUP * RWKV_HEAD), F32),
        pltpu.VMEM((n_seq, SUBLANES, RKV_COLS), F32),
        pltpu.VMEM((n_seq, SUBLANES, LORA_COLS), F32),
    ]
    if has_vres:
        scratch.append(pltpu.VMEM((n_seq, SUBLANES, LANES), F32))
    outs = pl.pallas_call(
        functools.partial(_rwkv_kernel, has_vres=has_vres),
        grid=(seq // step_rows,),
        in_specs=in_specs,
        out_specs=out_specs,
        out_shape=out_shape,
        scratch_shapes=scratch,
        compiler_params=_cparams("arbitrary"),
        name="rwkv_mixer",
    )(*args)
    y = outs[0].reshape(n, RWKV_WIDTH)
    return y, (None if has_vres else outs[1].reshape(n, RWKV_WIDTH))


def _merge_kernel(a_ref, y_ref, wa_ref, wb_ref, ga_ref, gb_ref, o_ref):
    for c in range(o_ref.shape[1] // MXU_COLS):
        cs = slice(c * MXU_COLS, (c + 1) * MXU_COLS)
        pa = jnp.dot(a_ref[...], wa_ref[:, cs].astype(BF16), preferred_element_type=F32)
        pb = jnp.dot(y_ref[...], wb_ref[:, cs].astype(BF16), preferred_element_type=F32)
        ga = jax.nn.sigmoid(ga_ref[:, cs].astype(F32))
        gb = jax.nn.sigmoid(gb_ref[:, cs].astype(F32))
        o_ref[:, cs] = (ga * pa + gb * pb).astype(BF16)


def _merge(layer, attn, y, wa, wb, gates, tm=1024, tn=1024):
    n = attn.shape[0]
    nj = D_MODEL // tn
    return pl.pallas_call(
        _merge_kernel,
        grid=(n // tm, nj),
        in_specs=[
            pl.BlockSpec((tm, ATTN_WIDTH), lambda i, j: (i, 0)),
            pl.BlockSpec((tm, RWKV_WIDTH), lambda i, j: (i, 0)),
            pl.BlockSpec((None, ATTN_WIDTH, tn), lambda i, j: (layer, 0, j)),
            pl.BlockSpec((None, RWKV_WIDTH, tn), lambda i, j: (layer, 0, j)),
            pl.BlockSpec((tm, tn), lambda i, j: (i, j)),
            pl.BlockSpec((tm, tn), lambda i, j: (i, j + nj)),
        ],
        out_specs=pl.BlockSpec((tm, tn), lambda i, j: (i, j)),
        out_shape=jax.ShapeDtypeStruct((n, D_MODEL), BF16),
        compiler_params=_cparams("parallel", "arbitrary"),
        name="gated_merge",
    )(attn, y, wa, wb, gates, gates)


def _outproj_router_kernel(m_ref, wo_ref, x_ref, nw_ref, wr_ref, br_ref,
                           xo_ref, hf_ref, slab_ref, plan_ref, cnt_ref, run_scr):
    @pl.when(pl.program_id(0) == 0)
    def _():
        run_scr[...] = jnp.zeros_like(run_scr)

    tm = x_ref.shape[0]
    sq = jnp.zeros((tm, LANES), F32)
    for c in range(D_MODEL // MXU_COLS):
        cs = slice(c * MXU_COLS, (c + 1) * MXU_COLS)
        xn_c = x_ref[:, cs] + jnp.dot(m_ref[...], wo_ref[:, cs], preferred_element_type=F32)
        xo_ref[:, cs] = xn_c
        for l in range(MXU_COLS // LANES):
            piece = xn_c[:, l * LANES:(l + 1) * LANES]
            sq = sq + piece * piece
    ms = jnp.sum(sq, axis=-1, keepdims=True) * (1.0 / D_MODEL)
    hf = xo_ref[...] * lax.rsqrt(ms + NORM_EPS) * nw_ref[...]
    _store_rows(hf_ref, hf)

    h_hi = hf.astype(BF16)
    h_lo = (hf - h_hi.astype(F32)).astype(BF16)
    wr = wr_ref[...]
    w_hi = wr.astype(BF16)
    w_lo = (wr - w_hi.astype(F32)).astype(BF16)
    lg = (jnp.dot(h_hi, w_hi, preferred_element_type=F32)
          + jnp.dot(h_hi, w_lo, preferred_element_type=F32)
          + jnp.dot(h_lo, w_hi, preferred_element_type=F32)) + br_ref[...]

    chunks = range(tm // ROUTER_ROWS)
    lgs = [lg[c * ROUTER_ROWS:(c + 1) * ROUTER_ROWS] for c in chunks]
    lane = lax.broadcasted_iota(jnp.int32, (ROUTER_ROWS, ROUTER_LANES), 1).astype(F32)
    neg = -jnp.inf
    big = float(ROUTER_LANES)
    is_group = lane < N_GROUPS

    def row_max(vals):
        return [jnp.max(v, axis=-1, keepdims=True) for v in vals]

    def first_index(vals, mx):
        return [jnp.min(jnp.where(v == m, lane, big), axis=-1, keepdims=True) for v, m in zip(vals, mx)]

    def row_sum(vals):
        return [jnp.sum(v, axis=-1, keepdims=True) for v in vals]

    gl = [jnp.where(is_group, v, neg) for v in lgs]
    gmax = row_max(gl)
    gsel = first_index(gl, gmax)
    gsum = row_sum([jnp.where(is_group, jnp.exp(v - m), 0.0) for v, m in zip(lgs, gmax)])
    lo_lane = [EXPERT_LANE0 + EXPERTS_PER_GROUP * g for g in gsel]
    el = [jnp.where((lane >= lo) & (lane < lo + EXPERTS_PER_GROUP), v, neg) for v, lo in zip(lgs, lo_lane)]
    v1 = row_max(el)
    i1 = first_index(el, v1)
    el2 = [jnp.where(lane == i, neg, v) for v, i in zip(el, i1)]
    v2 = row_max(el2)
    i2 = first_index(el2, v2)
    oh1 = [lane == i for i in i1]
    oh2 = [lane == i for i in i2]

    cnt = jnp.concatenate([a.astype(F32) + b.astype(F32) for a, b in zip(oh1, oh2)], axis=0)
    r_i = lax.broadcasted_iota(jnp.int32, (tm, tm), 0)
    c_i = lax.broadcasted_iota(jnp.int32, (tm, tm), 1)
    before = jnp.dot((r_i > c_i).astype(BF16), cnt.astype(BF16), preferred_element_type=F32)
    tot = before + run_scr[0:1, :]
    tots = [tot[c * ROUTER_ROWS:(c + 1) * ROUTER_ROWS] for c in chunks]
    rank1 = row_sum([jnp.where(o, t, 0.0) for o, t in zip(oh1, tots)])
    rank2 = row_sum([jnp.where(o, t, 0.0) for o, t in zip(oh2, tots)])
    run = run_scr[0:1, :] + jnp.sum(cnt, axis=0, keepdims=True)
    run_scr[...] = jnp.broadcast_to(run, run_scr.shape)
    cnt_ref[...] = jnp.broadcast_to(run, cnt_ref.shape)

    for c in chunks:
        gp = 1.0 / gsum[c]
        e21 = jnp.exp(v2[c] - v1[c])
        ew1 = gp / (1.0 + e21)
        ew2 = gp * e21 / (1.0 + e21)
        slab = jnp.where(lane == 0, i1[c] - EXPERT_LANE0,
               jnp.where(lane == 1, i2[c] - EXPERT_LANE0,
               jnp.where(lane == 2, ew1,
               jnp.where(lane == 3, ew2,
               jnp.where(lane == 4, rank1[c],
               jnp.where(lane == 5, rank2[c], 0.0))))))
        slab_ref[c * ROUTER_ROWS:(c + 1) * ROUTER_ROWS, :] = slab
        plan_ref[:, c * ROUTER_ROWS:(c + 1) * ROUTER_ROWS] = jnp.transpose(slab)[:SUBLANES, :]


def _outproj_router(merged, w_out, x, norm_w, w_router, b_router, tm=512):
    n = x.shape[0]
    row = lambda i: (i, 0)
    const = lambda i: (0, 0)
    return pl.pallas_call(
        _outproj_router_kernel,
        grid=(n // tm,),
        in_specs=[
            pl.BlockSpec((tm, D_MODEL), row),
            pl.BlockSpec((D_MODEL, D_MODEL), const),
            pl.BlockSpec((tm, D_MODEL), row),
            pl.BlockSpec((1, D_MODEL), const),
            pl.BlockSpec((D_MODEL, ROUTER_LANES), const),
            pl.BlockSpec((1, ROUTER_LANES), const),
        ],
        out_specs=[
            pl.BlockSpec((tm, D_MODEL), row),
            pl.BlockSpec((tm * ROW_SUB, LANES), row),
            pl.BlockSpec((tm, ROUTER_LANES), row),
            pl.BlockSpec((SUBLANES, tm), lambda i: (0, i)),
            pl.BlockSpec((SUBLANES, ROUTER_LANES), const),
        ],
        out_shape=[
            jax.ShapeDtypeStruct((n, D_MODEL), F32),
            jax.ShapeDtypeStruct((n * ROW_SUB, LANES), F32),
            jax.ShapeDtypeStruct((n, ROUTER_LANES), F32),
            jax.ShapeDtypeStruct((SUBLANES, n), F32),
            jax.ShapeDtypeStruct((SUBLANES, ROUTER_LANES), F32),
        ],
        scratch_shapes=[pltpu.VMEM((SUBLANES, ROUTER_LANES), F32)],
        compiler_params=_cparams("arbitrary"),
        name="outproj_router",
    )(merged, w_out, x, norm_w, w_router, b_router)


def _dispatch_kernel(slot0_ref, slot1_ref, hf_ref, xb_in_hbm, xb_hbm, sem, *, tm):
    del xb_in_hbm

    def copy(u, j):
        return pltpu.make_async_copy(_slab(hf_ref, u), _slab(xb_hbm, (slot0_ref, slot1_ref)[j][u]), sem)

    def start(u, c):
        for j in range(TOP_K):
            copy(u, j).start(priority=j)
        return c

    def wait(u, c):
        for j in range(TOP_K):
            copy(u, j).wait()
        return c

    lax.fori_loop(0, tm, start, 0, unroll=8)
    lax.fori_loop(0, tm, wait, 0, unroll=8)


def _dispatch(slots, hf, cap, backing, tm=2048):
    n = hf.shape[0] // ROW_SUB
    tm = min(tm, n)
    assert backing.shape == (cap * ROW_SUB, LANES)
    return pl.pallas_call(
        functools.partial(_dispatch_kernel, tm=tm),
        grid=(n // tm,),
        in_specs=[
            pl.BlockSpec((tm,), lambda i: (i,), memory_space=pltpu.SMEM),
            pl.BlockSpec((tm,), lambda i: (i,), memory_space=pltpu.SMEM),
            pl.BlockSpec((tm * ROW_SUB, LANES), lambda i: (i, 0)),
            pl.BlockSpec(memory_space=pl.ANY),
        ],
        out_specs=pl.BlockSpec(memory_space=pl.ANY),
        out_shape=jax.ShapeDtypeStruct((cap * ROW_SUB, LANES), F32),
        scratch_shapes=[pltpu.SemaphoreType.DMA(())],
        input_output_aliases={3: 0},
        compiler_params=_cparams("arbitrary"),
        name="moe_dispatch",
    )(slots[0], slots[1], hf, backing)


def _ffn_kernel(be_ref, nu_ref, par_ref, nxt_ref, x_ref, wg_hbm, wu_hbm, wd_hbm, o_ref,
                wg_f, wu_f, wd_f, wg_s, wu_s, wd_s, sems, *, layer):
    b = pl.program_id(0)
    used = b < nu_ref[0]
    prev_e = be_ref[jnp.maximum(b - 1, 0)]
    new_expert = jnp.logical_or(b == 0, be_ref[b] != prev_e)

    def fetch(e, p):
        return [pltpu.make_async_copy(src.at[layer, e], dst.at[p], sems.at[p, i])
                for i, (src, dst) in enumerate(((wg_hbm, wg_f), (wu_hbm, wu_f), (wd_hbm, wd_f)))]

    @pl.when(jnp.logical_and(used, new_expert))
    def _():
        for p in range(2):
            @pl.when(par_ref[b] == p)
            def _():
                @pl.when(b == 0)
                def _():
                    for cp in fetch(be_ref[0], p):
                        cp.start()

                @pl.when(nxt_ref[b] >= 0)
                def _():
                    for cp in fetch(nxt_ref[b], 1 - p):
                        cp.start()

                for cp in fetch(be_ref[b], p):
                    cp.wait()
                wg_s[...] = wg_f[p].astype(BF16)
                wu_s[...] = wu_f[p].astype(BF16)
                wd_s[...] = wd_f[p].astype(BF16)

    @pl.when(used)
    def _():
        slabs = MXU_COLS // LANES
        gt = up = None
        for kc in range(D_MODEL // MXU_COLS):
            xk = jnp.concatenate([x_ref[pl.ds(kc * slabs + s, MOE_ROWS, stride=ROW_SUB), :] for s in range(slabs)],
                                 axis=1).astype(BF16)
            ks = slice(kc * MXU_COLS, (kc + 1) * MXU_COLS)
            g_k = jnp.dot(xk, wg_s[ks, :], preferred_element_type=F32)
            u_k = jnp.dot(xk, wu_s[ks, :], preferred_element_type=F32)
            gt = g_k if gt is None else gt + g_k
            up = u_k if up is None else up + u_k
        hid = (gt * jax.nn.sigmoid(gt) * up).astype(BF16)
        for c in range(D_MODEL // MXU_COLS):
            y_c = jnp.dot(hid, wd_s[:, c * MXU_COLS:(c + 1) * MXU_COLS], preferred_element_type=F32)
            for s in range(slabs):
                o_ref[pl.ds(c * slabs + s, MOE_ROWS, stride=ROW_SUB), :] = y_c[:, s * LANES:(s + 1) * LANES]

    @pl.when(jnp.logical_not(used))
    def _():
        o_ref[...] = jnp.zeros_like(o_ref)


def _expert_ffn(layer, plan, xb, w_gate, w_up, w_down):
    nblk = xb.shape[0] // (MOE_ROWS * ROW_SUB)
    hbm = pl.BlockSpec(memory_space=pl.ANY)
    grid_spec = pltpu.PrefetchScalarGridSpec(
        num_scalar_prefetch=4,
        grid=(nblk,),
        in_specs=[
            pl.BlockSpec((MOE_ROWS * ROW_SUB, LANES), lambda b, be, nu, par, nxt: (jnp.minimum(b, nu[0] - 1), 0)),
            hbm, hbm, hbm,
        ],
        out_specs=pl.BlockSpec((MOE_ROWS * ROW_SUB, LANES), lambda b, be, nu, par, nxt: (b, 0)),
        scratch_shapes=[
            pltpu.VMEM((2, D_MODEL, EXPERT_HIDDEN), F32),
            pltpu.VMEM((2, D_MODEL, EXPERT_HIDDEN), F32),
            pltpu.VMEM((2, EXPERT_HIDDEN, D_MODEL), F32),
            pltpu.VMEM((D_MODEL, EXPERT_HIDDEN), BF16),
            pltpu.VMEM((D_MODEL, EXPERT_HIDDEN), BF16),
            pltpu.VMEM((EXPERT_HIDDEN, D_MODEL), BF16),
            pltpu.SemaphoreType.DMA((2, 3)),
        ],
    )
    return pl.pallas_call(
        functools.partial(_ffn_kernel, layer=layer),
        grid_spec=grid_spec,
        out_shape=jax.ShapeDtypeStruct(xb.shape, F32),
        compiler_params=_cparams("arbitrary"),
        name="expert_ffn",
    )(plan.block_expert, plan.n_used, plan.parity, plan.next_expert, xb, w_gate, w_up, w_down)


def _combine_kernel(slot0_ref, slot1_ref, next0_ref, next1_ref, x_ref, slab_ref, fw_ref, yb_hbm, o_ref,
                    buf00, buf01, buf10, buf11, sems, *, tm, n_tiles, final_norm):
    i = pl.program_id(0)
    bufs = ((buf00, buf01), (buf10, buf11))
    cur = (slot0_ref, slot1_ref)

    def copies(slots, p, u):
        return [pltpu.make_async_copy(_slab(yb_hbm, slots[j][u]), _slab(bufs[p][j], u), sems.at[p])
                for j in range(TOP_K)]

    def start_all(slots, p):
        def body(u, c):
            for j, cp in enumerate(copies(slots, p, u)):
                cp.start(priority=j)
            return c
        lax.fori_loop(0, tm, body, 0, unroll=8)

    def wait_all(p):
        def body(u, c):
            for cp in copies(cur, p, u):
                cp.wait()
            return c
        lax.fori_loop(0, tm, body, 0, unroll=8)

    @pl.when(i == 0)
    def _():
        start_all(cur, 0)

    for p in range(2):
        @pl.when(i % 2 == p)
        def _():
            @pl.when(i + 1 < n_tiles)
            def _():
                start_all((next0_ref, next1_ref), 1 - p)

            wait_all(p)
            slab = slab_ref[...]
            w1 = slab[:, 2:3]
            w2 = slab[:, 3:4]
            out = x_ref[...] + (_rows_to_2d(bufs[p][0], tm) * w1 + _rows_to_2d(bufs[p][1], tm) * w2)
            if final_norm:
                out = _rms(out, fw_ref[...])
            o_ref[...] = out


def _combine(slots, x, slab, final_w, yb, final_norm, tm=256):
    n = x.shape[0]
    n_tiles = n // tm
    row = lambda i: (i, 0)
    row_buf = pltpu.VMEM((tm * ROW_SUB, LANES), F32)
    return pl.pallas_call(
        functools.partial(_combine_kernel, tm=tm, n_tiles=n_tiles, final_norm=final_norm),
        grid=(n_tiles,),
        in_specs=[
            pl.BlockSpec((tm,), lambda i: (i,), memory_space=pltpu.SMEM),
            pl.BlockSpec((tm,), lambda i: (i,), memory_space=pltpu.SMEM),
            pl.BlockSpec((tm,), lambda i: (jnp.minimum(i + 1, n_tiles - 1),), memory_space=pltpu.SMEM),
            pl.BlockSpec((tm,), lambda i: (jnp.minimum(i + 1, n_tiles - 1),), memory_space=pltpu.SMEM),
            pl.BlockSpec((tm, D_MODEL), row),
            pl.BlockSpec((tm, ROUTER_LANES), row),
            pl.BlockSpec((1, D_MODEL), lambda i: (0, 0)),
            pl.BlockSpec(memory_space=pl.ANY),
        ],
        out_specs=pl.BlockSpec((tm, D_MODEL), row),
        out_shape=jax.ShapeDtypeStruct((n, D_MODEL), F32),
        scratch_shapes=[row_buf, row_buf, row_buf, row_buf, pltpu.SemaphoreType.DMA((2,))],
        compiler_params=_cparams("arbitrary"),
        name="moe_combine",
    )(slots[0], slots[1], slots[0], slots[1], x, slab, final_w, yb)


class MoePlan(NamedTuple):
    slot: tuple
    block_expert: jax.Array
    n_used: jax.Array
    parity: jax.Array
    next_expert: jax.Array
    cap: int


def _moe_plan(plan_t, counts_row, n_tokens):
    eid = plan_t[0:TOP_K].astype(jnp.int32)
    rank = plan_t[4:4 + TOP_K].astype(jnp.int32)
    counts = counts_row[EXPERT_LANE0:EXPERT_LANE0 + N_EXPERTS].astype(jnp.int32)
    padded = (counts + MOE_ROWS - 1) // MOE_ROWS * MOE_ROWS
    pad_ends = jnp.cumsum(padded)
    pad_starts = pad_ends - padded
    expert = jnp.arange(N_EXPERTS, dtype=jnp.int32)[:, None, None]
    region = jnp.sum(jnp.where(eid[None] == expert, pad_starts[:, None, None], 0), axis=0)
    slot = region + rank
    nblk = n_tokens * TOP_K // MOE_ROWS + N_EXPERTS
    block_start = jnp.arange(nblk, dtype=jnp.int32) * MOE_ROWS
    block_expert = jnp.minimum(
        jnp.sum((pad_ends[None, :] <= block_start[:, None]).astype(jnp.int32), axis=1),
        N_EXPERTS - 1).astype(jnp.int32)
    n_used = (pad_ends[-1:] // MOE_ROWS).astype(jnp.int32)
    blk = jnp.arange(nblk, dtype=jnp.int32)
    prev_expert = jnp.concatenate([block_expert[:1], block_expert[:-1]])
    first = (blk < n_used[0]) & ((blk == 0) | (block_expert != prev_expert))
    parity = ((jnp.cumsum(first.astype(jnp.int32)) - 1) % 2).astype(jnp.int32)
    later_first = first[None, :] & (blk[None, :] > blk[:, None])
    next_first = jnp.min(jnp.where(later_first, blk[None, :], nblk), axis=1)
    next_expert = jnp.where(next_first < nblk, block_expert[jnp.minimum(next_first, nblk - 1)], -1)
    return MoePlan(tuple(slot[j] for j in range(TOP_K)), block_expert, n_used, parity,
                   next_expert.astype(jnp.int32), nblk * MOE_ROWS)


def kernel(x, attn_norm_w, w_in, shift_mix, attn_sinks, rwkv_w0, rwkv_w_up, rwkv_a0, rwkv_a_up, rwkv_g_up, rwkv_k_k, rwkv_k_a, rwkv_r_k, rwkv_ln_w, rwkv_ln_b, vres_down, vres_mix, vres_up, vres_v0, w_branch_a, w_branch_b, w_out, ffn_norm_w, router_group_w, router_group_b, router_expert_w, router_expert_b, expert_w_gate, expert_w_up, expert_w_down, final_norm_w):
    batch, seq, _ = x.shape
    n = batch * seq
    depth = w_in.shape[0]
    xf = x.reshape(n, D_MODEL)
    w_in_bf = jnp.swapaxes(w_in, 1, 2).astype(BF16)

    zero_c = jnp.zeros((depth, RWKV_WIDTH), F32)
    v0_all = jnp.concatenate([zero_c[:1], vres_v0.astype(F32)], axis=0)
    pv_pre_all = jnp.stack([rwkv_w0, rwkv_a0, rwkv_k_k, rwkv_k_a, v0_all, zero_c, zero_c, zero_c], axis=1)
    pv_post_all = jnp.stack([rwkv_ln_w, rwkv_ln_b, rwkv_r_k.reshape(depth, RWKV_WIDTH),
                             zero_c, zero_c, zero_c, zero_c, zero_c], axis=1)
    mix_r_all = shift_mix[:, None, :RKV_COLS]
    mix_l_all = jnp.pad(shift_mix[:, None, RKV_COLS:], ((0, 0), (0, 0), (0, LORA_COLS - LORA_WIDTH)))
    wup_all = jnp.pad(rwkv_w_up.astype(BF16), ((0, 0), (0, LANES - DECAY_LORA), (0, 0)))
    aup_all = jnp.pad(rwkv_a_up.astype(BF16), ((0, 0), (DECAY_LORA, LANES - DECAY_LORA - AAA_LORA), (0, 0)))
    gup_all = jnp.pad(rwkv_g_up.astype(BF16), ((0, 0), (0, 2 * LANES - GATE_LORA), (0, 0)))
    vup_all = jnp.pad(vres_up.astype(BF16), ((1, 0), (0, LANES - MV_LORA), (0, 0)))
    vd_all = jnp.pad(vres_down.astype(BF16), ((0, 0), (0, 0), (0, LANES - MV_LORA)))
    mix_v_all = jnp.pad(vres_mix, ((0, 0), (0, LANES - MV_LORA)))[:, None, :]
    router_pad = ROUTER_LANES - N_GROUPS - N_EXPERTS
    w_router_all = jnp.pad(jnp.concatenate([router_group_w, router_expert_w], axis=2),
                           ((0, 0), (0, 0), (0, router_pad)))
    b_router_all = jnp.pad(jnp.concatenate([router_group_b, router_expert_b], axis=1),
                           ((0, 0), (0, router_pad)))[:, None, :]
    attn_norm_all = attn_norm_w[:, None, :]
    ffn_norm_all = ffn_norm_w[:, None, :]
    w_out_bf = w_out.astype(BF16)
    sinks_all = attn_sinks.astype(F32)

    v_first = None
    slot_backing = None
    slot_rows = (n * TOP_K // MOE_ROWS + N_EXPERTS) * MOE_ROWS
    for i in range(depth):
        has_vres = i > 0
        mix_r, mix_l, pv_pre, pv_post = mix_r_all[i], mix_l_all[i], pv_pre_all[i], pv_post_all[i]
        wup, aup, gup, vup = wup_all[i], aup_all[i], gup_all[i], vup_all[i]
        vd, mix_v = (vd_all[i - 1], mix_v_all[i - 1]) if has_vres else (None, None)
        w_router, b_router = w_router_all[i], b_router_all[i]

        need_zeros = slot_backing is None
        proj = _inproj(i, xf, attn_norm_all[i], w_in_bf, vd,
                       zero_rows=slot_rows * ROW_SUB if need_zeros else 0)
        qkv, rkv, lora, gates = proj[:4]
        zv = proj[4] if has_vres else None
        if need_zeros:
            slot_backing = proj[-1]
        attn = _attention(qkv, sinks_all[i], seq)
        y, v_new = _rwkv_mixer(rkv, lora, v_first, zv, mix_r, mix_l, mix_v, pv_pre,
                               wup, aup, gup, vup, pv_post, seq)
        if not has_vres:
            v_first = v_new
        merged = _merge(i, attn, y, w_branch_a, w_branch_b, gates)

        x_mid, hf, slab, plan_t, counts = _outproj_router(
            merged, w_out_bf[i], xf, ffn_norm_all[i], w_router, b_router)
        plan = _moe_plan(plan_t, counts[0], n)
        xb = _dispatch(plan.slot, hf, plan.cap, slot_backing)
        yb = _expert_ffn(i, plan, xb, expert_w_gate, expert_w_up, expert_w_down)
        slot_backing = yb
        xf = _combine(plan.slot, x_mid, slab, final_norm_w.reshape(1, D_MODEL), yb, final_norm=(i == depth - 1))
    return xf.reshape(batch, seq, D_MODEL)
```

```python
import functools
import math
from typing import NamedTuple

import jax
import jax.numpy as jnp
from jax import lax
from jax.experimental import pallas as pl
from jax.experimental.pallas import tpu as pltpu

F32 = jnp.float32
BF16 = jnp.bfloat16

D_MODEL = 2048
HEAD_DIM = 64
N_Q_HEADS = 16
N_KV_HEADS = 4
GQA_GROUP = N_Q_HEADS // N_KV_HEADS
ATTN_WIDTH = N_Q_HEADS * HEAD_DIM
KV_WIDTH = N_KV_HEADS * HEAD_DIM
WINDOW = 128
ATTN_BLOCK = 128
ATTN_SUB = 4

RWKV_HEAD = 64
N_RWKV_HEADS = 16
RWKV_WIDTH = N_RWKV_HEADS * RWKV_HEAD
DECAY_LORA = 64
AAA_LORA = 64
MV_LORA = 32
GATE_LORA = 160
RWKV_GN_EPS = 64e-5
DECAY_SCALE = math.exp(-0.5)
KK_NORM_FLOOR = 1e-12
LORA_WIDTH = DECAY_LORA + AAA_LORA + GATE_LORA

N_GROUPS = 4
EXPERTS_PER_GROUP = 8
N_EXPERTS = N_GROUPS * EXPERTS_PER_GROUP
TOP_K = 2
EXPERT_HIDDEN = D_MODEL // 4
NORM_EPS = 1e-5

LANES = 128
SUBLANES = 8
MXU_COLS = 256

SEG = 512
QKV_COLS = ATTN_WIDTH + 2 * KV_WIDTH
RKV_COLS = 3 * RWKV_WIDTH
LORA_COLS = SEG
GATE_COLS = 2 * D_MODEL
QKV_TILES = QKV_COLS // SEG
RKV_TILES = RKV_COLS // SEG
LORA_TILES = LORA_COLS // SEG
GATE_TILES = GATE_COLS // SEG
IN_TILES = QKV_TILES + RKV_TILES + LORA_TILES + GATE_TILES
FIRST_GATE_TILE = QKV_TILES + RKV_TILES + LORA_TILES
ZERO_STEPS = 16
GATE_COL0 = QKV_COLS + RKV_COLS + LORA_WIDTH

SCAN_CHUNK = 64
SCAN_SUB = 2
SCAN_GROUP = 4
MOE_ROWS = 256
ROUTER_LANES = LANES
ROUTER_ROWS = 128
EXPERT_LANE0 = N_GROUPS

ROW_SUB = D_MODEL // LANES

VMEM_LIMIT = 56 * 1024 * 1024


def _rows_to_2d(slab_ref, rows):
    return jnp.concatenate([slab_ref[pl.ds(s, rows, stride=ROW_SUB), :] for s in range(ROW_SUB)], axis=1)


def _store_rows(slab_ref, val):
    rows = val.shape[0]
    for s in range(ROW_SUB):
        slab_ref[pl.ds(s, rows, stride=ROW_SUB), :] = val[:, s * LANES:(s + 1) * LANES]


def _slab(ref, row):
    return ref.at[pl.ds(pl.multiple_of(row * ROW_SUB, ROW_SUB), ROW_SUB)]


def _cparams(*sem):
    return pltpu.CompilerParams(dimension_semantics=sem, vmem_limit_bytes=VMEM_LIMIT)


def _bdot(a, b):
    return jnp.dot(a.astype(BF16), b.astype(BF16), preferred_element_type=F32)


def _bdot_nt(a, b):
    return lax.dot_general(a.astype(BF16), b.astype(BF16), (((1,), (1,)), ((), ())),
                           preferred_element_type=F32)


def _bdot_tn(a, b):
    return lax.dot_general(a.astype(BF16), b.astype(BF16), (((0,), (0,)), ((), ())),
                           preferred_element_type=F32)


def _dot_split_rhs(w01, x, passes):
    acc, rem = None, x
    for p in range(passes):
        part = rem.astype(BF16)
        d = jnp.dot(w01, part, preferred_element_type=F32)
        acc = d if acc is None else acc + d
        if p + 1 < passes:
            rem = rem - part.astype(F32)
    return acc


def _dot_split_lhs(x, w01, passes):
    acc, rem = None, x
    for p in range(passes):
        part = rem.astype(BF16)
        d = jnp.dot(part, w01, preferred_element_type=F32)
        acc = d if acc is None else acc + d
        if p + 1 < passes:
            rem = rem - part.astype(F32)
    return acc


def _rms(x, w):
    ms = jnp.mean(x * x, axis=-1, keepdims=True)
    return x * lax.rsqrt(ms + NORM_EPS) * w


def _inproj_kernel(*refs, has_vres, has_zeros):
    refs = list(refs)
    x_ref, nw_ref, w_ref, wg_ref = refs[:4]
    refs = refs[4:]
    vd_ref = refs.pop(0) if has_vres else None
    qkv_ref, rkv_ref, lora_ref, gate_ref = refs[:4]
    refs = refs[4:]
    zv_ref = refs.pop(0) if has_vres else None
    zeros_ref = refs.pop(0) if has_zeros else None
    (h_scr,) = refs
    j = pl.program_id(1)

    if has_zeros:
        zeros_ref[...] = jnp.zeros_like(zeros_ref)

    @pl.when(j == 0)
    def _():
        h_scr[...] = _rms(x_ref[...], nw_ref[...]).astype(BF16)
        if has_vres:
            zv_ref[...] = jnp.dot(h_scr[...], vd_ref[...], preferred_element_type=F32)

    def project(wt_ref, out_ref):
        for c in range(SEG // MXU_COLS):
            cs = slice(c * MXU_COLS, (c + 1) * MXU_COLS)
            acc = lax.dot_general(h_scr[...], wt_ref[cs, :].astype(BF16), (((1,), (1,)), ((), ())),
                                  preferred_element_type=F32)
            out_ref[:, cs] = acc.astype(out_ref.dtype)

    @pl.when(j < QKV_TILES)
    def _():
        project(w_ref, qkv_ref)

    @pl.when((j >= QKV_TILES) & (j < QKV_TILES + RKV_TILES))
    def _():
        project(w_ref, rkv_ref)

    @pl.when(j == QKV_TILES + RKV_TILES)
    def _():
        project(w_ref, lora_ref)

    @pl.when(j >= FIRST_GATE_TILE)
    def _():
        project(wg_ref.at[0], gate_ref)


def _inproj(layer, x, norm_w, w_in, vd, zero_rows=0, tm=1024):
    n = x.shape[0]
    has_vres = vd is not None
    s_rkv = QKV_TILES
    in_specs = [
        pl.BlockSpec((tm, D_MODEL), lambda i, j: (i, 0)),
        pl.BlockSpec((1, D_MODEL), lambda i, j: (0, 0)),
        pl.BlockSpec((None, SEG, D_MODEL), lambda i, j: (layer, jnp.minimum(j, FIRST_GATE_TILE - 1), 0)),
        pl.BlockSpec((pl.Element(1), pl.Element(SEG), pl.Element(D_MODEL)),
                     lambda i, j: (layer, pl.multiple_of(
                         GATE_COL0 + SEG * jnp.clip(j - FIRST_GATE_TILE, 0, GATE_TILES - 1), 2 * SUBLANES), 0)),
    ]
    args = [x, norm_w, w_in, w_in]
    out_specs = [
        pl.BlockSpec((tm, SEG), lambda i, j: (i, jnp.clip(j, 0, QKV_TILES - 1))),
        pl.BlockSpec((tm, SEG), lambda i, j: (i, jnp.clip(j - s_rkv, 0, RKV_TILES - 1))),
        pl.BlockSpec((tm, SEG), lambda i, j: (i, 0)),
        pl.BlockSpec((tm, SEG), lambda i, j: (i, jnp.clip(j - FIRST_GATE_TILE, 0, GATE_TILES - 1))),
    ]
    out_shape = [
        jax.ShapeDtypeStruct((n, QKV_COLS), BF16),
        jax.ShapeDtypeStruct((n, RKV_COLS), F32),
        jax.ShapeDtypeStruct((n, LORA_COLS), F32),
        jax.ShapeDtypeStruct((n, GATE_COLS), BF16),
    ]
    if has_vres:
        in_specs.append(pl.BlockSpec((D_MODEL, LANES), lambda i, j: (0, 0)))
        args.append(vd)
        out_specs.append(pl.BlockSpec((tm, LANES), lambda i, j: (i, 0)))
        out_shape.append(jax.ShapeDtypeStruct((n, LANES), F32))
    if zero_rows:
        blk = zero_rows // ((n // tm) * ZERO_STEPS)
        assert blk * (n // tm) * ZERO_STEPS == zero_rows and blk % SUBLANES == 0
        out_specs.append(pl.BlockSpec((blk, LANES), lambda i, j: (i * ZERO_STEPS + jnp.minimum(j, ZERO_STEPS - 1), 0)))
        out_shape.append(jax.ShapeDtypeStruct((zero_rows, LANES), F32))
    return pl.pallas_call(
        functools.partial(_inproj_kernel, has_vres=has_vres, has_zeros=bool(zero_rows)),
        grid=(n // tm, IN_TILES),
        in_specs=in_specs,
        out_specs=out_specs,
        out_shape=out_shape,
        scratch_shapes=[pltpu.VMEM((tm, D_MODEL), BF16)],
        compiler_params=_cparams("parallel", "arbitrary"),
        name="inproj",
    )(*args)


def _attn_kernel(sink_ref, q_ref, kp_ref, kc_ref, vp_ref, vc_ref, bias_ref, o_ref, *, tiles_per_seq):
    first = (pl.program_id(0) % tiles_per_seq) == 0
    col = lax.broadcasted_iota(jnp.int32, (ATTN_BLOCK, 2 * ATTN_BLOCK), 1)
    pad_keys = jnp.logical_and(first, col < ATTN_BLOCK)
    scale = HEAD_DIM ** -0.5
    items = [(sb, hk) for sb in range(ATTN_SUB) for hk in range(N_KV_HEADS)]

    def window(prev_ref, cur_ref, sb, hk):
        ks = slice(hk * HEAD_DIM, (hk + 1) * HEAD_DIM)
        rows = slice(sb * ATTN_BLOCK, (sb + 1) * ATTN_BLOCK)
        prev = prev_ref[:, ks] if sb == 0 else cur_ref[(sb - 1) * ATTN_BLOCK:sb * ATTN_BLOCK, ks]
        return jnp.concatenate([prev, cur_ref[rows, ks]], axis=0)

    scores = {}
    for sb, hk in items:
        kw = window(kp_ref, kc_ref, sb, hk)
        for g in range(GQA_GROUP):
            h = hk * GQA_GROUP + g
            qh = q_ref[sb * ATTN_BLOCK:(sb + 1) * ATTN_BLOCK, h * HEAD_DIM:(h + 1) * HEAD_DIM] * scale
            s = lax.dot_general(qh, kw, (((1,), (1,)), ((), ())), preferred_element_type=F32)
            s = s + bias_ref[h]
            if sb == 0:
                s = jnp.where(pad_keys, -jnp.inf, s)
            scores[(sb, h)] = s
    row_max = {key: jnp.max(s, axis=-1, keepdims=True) for key, s in scores.items()}
    m = {key: jnp.maximum(row_max[key], sink_ref[key[1]]) for key in scores}
    p = {key: jnp.exp(scores[key] - m[key]) for key in scores}
    row_sum = {key: jnp.sum(p[key], axis=-1, keepdims=True) for key in scores}
    denoms = {key: row_sum[key] + jnp.exp(sink_ref[key[1]] - m[key]) for key in scores}
    probs = {key: p[key].astype(BF16) for key in scores}
    for sb, hk in items:
        vw = window(vp_ref, vc_ref, sb, hk)
        for g in range(GQA_GROUP):
            h = hk * GQA_GROUP + g
            o = jnp.dot(probs[(sb, h)], vw, preferred_element_type=F32) / denoms[(sb, h)]
            o_ref[sb * ATTN_BLOCK:(sb + 1) * ATTN_BLOCK, h * HEAD_DIM:(h + 1) * HEAD_DIM] = o.astype(BF16)


def _attn_bias():
    qi = jnp.arange(ATTN_BLOCK)[:, None]
    kj = jnp.arange(2 * ATTN_BLOCK)[None, :]
    dist = qi + ATTN_BLOCK - kj
    valid = (dist >= 0) & (dist < WINDOW)
    slopes = jnp.exp2(-8.0 * jnp.arange(1, N_Q_HEADS + 1, dtype=F32) / N_Q_HEADS)
    bias = -slopes[:, None, None] * dist.astype(F32)[None]
    return jnp.where(valid[None], bias, -jnp.inf)


def _attention(qkv, sinks, seq):
    n = qkv.shape[0]
    rows = ATTN_SUB * ATTN_BLOCK
    tiles_per_seq = seq // rows
    kcol = ATTN_WIDTH // KV_WIDTH
    prev = lambda i: jnp.maximum(i * ATTN_SUB - 1, 0)
    return pl.pallas_call(
        functools.partial(_attn_kernel, tiles_per_seq=tiles_per_seq),
        grid=(n // rows,),
        in_specs=[
            pl.BlockSpec(memory_space=pltpu.SMEM),
            pl.BlockSpec((rows, ATTN_WIDTH), lambda i: (i, 0)),
            pl.BlockSpec((ATTN_BLOCK, KV_WIDTH), lambda i: (prev(i), kcol)),
            pl.BlockSpec((rows, KV_WIDTH), lambda i: (i, kcol)),
            pl.BlockSpec((ATTN_BLOCK, KV_WIDTH), lambda i: (prev(i), kcol + 1)),
            pl.BlockSpec((rows, KV_WIDTH), lambda i: (i, kcol + 1)),
            pl.BlockSpec((N_Q_HEADS, ATTN_BLOCK, 2 * ATTN_BLOCK), lambda i: (0, 0, 0)),
        ],
        out_specs=pl.BlockSpec((rows, ATTN_WIDTH), lambda i: (i, 0)),
        out_shape=jax.ShapeDtypeStruct((n, ATTN_WIDTH), BF16),
        compiler_params=_cparams("parallel"),
        name="swa_attention",
    )(sinks, qkv, qkv, qkv, qkv, qkv, _attn_bias())


def _head_ones():
    ch = jnp.arange(MXU_COLS) // RWKV_HEAD
    return (ch[:, None] == ch[None, :]).astype(BF16)


def _head_sums(x, ones, passes):
    parts = [_dot_split_lhs(x[:, c:c + MXU_COLS], ones, passes) for c in range(0, RWKV_WIDTH, MXU_COLS)]
    return jnp.concatenate(parts, axis=1)


def _rwkv_kernel(*refs, has_vres):
    refs = list(refs)
    rkv_ref, lo_ref = refs[:2]
    refs = refs[2:]
    if has_vres:
        vf_ref, zv_ref, mixv_ref = refs[:3]
        refs = refs[3:]
    mixr_ref, mixl_ref, pv_ref, wup_ref, aup_ref, gup_ref, vup_ref, ones_ref, pp_ref = refs[:9]
    refs = refs[9:]
    y_ref = refs.pop(0)
    v_out = None if has_vres else refs.pop(0)
    s_scr, prev_rkv, prev_lo = refs[:3]
    prev_zv = refs[3] if has_vres else None

    L = SCAN_CHUNK
    W = SCAN_GROUP * RWKV_HEAD
    n_seq, step_rows = rkv_ref.shape[0], rkv_ref.shape[1]
    n_grp = N_RWKV_HEADS // SCAN_GROUP
    ones = ones_ref[...]

    @pl.when(pl.program_id(0) == 0)
    def _():
        s_scr[...] = jnp.zeros_like(s_scr)
        prev_rkv[...] = jnp.zeros_like(prev_rkv)
        prev_lo[...] = jnp.zeros_like(prev_lo)
        if has_vres:
            prev_zv[...] = jnp.zeros_like(prev_zv)

    def lerp_shift(z_ref, carry_ref, b, mix):
        z = z_ref[b]
        row = lax.broadcasted_iota(jnp.int32, z.shape, 0)
        back = jnp.where(row == 0, carry_ref[b, SUBLANES - 1:SUBLANES, :], pltpu.roll(z, 1, 0))
        carry_ref[b] = z_ref[b, step_rows - SUBLANES:step_rows, :]
        return z + (back - z) * mix

    w0, a0, k_k, k_a, v0 = (pv_ref[i:i + 1, :] for i in range(5))
    pre = []
    for b in range(n_seq):
        zs = lerp_shift(rkv_ref, prev_rkv, b, mixr_ref[...])
        los = lerp_shift(lo_ref, prev_lo, b, mixl_ref[...])
        r = zs[:, :RWKV_WIDTH]
        kr = zs[:, RWKV_WIDTH:2 * RWKV_WIDTH]
        vr = zs[:, 2 * RWKV_WIDTH:]
        wa = los[:, :LANES]
        col = lax.broadcasted_iota(jnp.int32, wa.shape, 1)
        wa_in = jnp.where(col < DECAY_LORA, jnp.tanh(wa), wa).astype(BF16)
        gd = los[:, LANES:3 * LANES]
        col_g = lax.broadcasted_iota(jnp.int32, gd.shape, 1)
        g_in = jnp.where(col_g < GATE_LORA, jax.nn.sigmoid(gd), gd).astype(BF16)
        dw = jnp.dot(wa_in, wup_ref[...], preferred_element_type=F32)
        da = jnp.dot(wa_in, aup_ref[...], preferred_element_type=F32)
        g = jnp.dot(g_in, gup_ref[...], preferred_element_type=F32)
        lw = -DECAY_SCALE * jax.nn.sigmoid(w0 + dw)
        a = jax.nn.sigmoid(a0 + da)
        if has_vres:
            zvs = lerp_shift(zv_ref, prev_zv, b, mixv_ref[...])
            dv = jnp.dot(zvs.astype(BF16), vup_ref[...], preferred_element_type=F32)
            vr = vr + (vf_ref[b] - vr) * jax.nn.sigmoid(v0 + dv)
        else:
            v_out[b] = vr
        kk0 = kr * k_k
        ss = _head_sums(kk0 * kk0, ones, 1)
        kk = kk0 * lax.rsqrt(jnp.maximum(ss, KK_NORM_FLOOR ** 2))
        pre.append(dict(r=r, lw=lw, k=kr * (1.0 + (a - 1.0) * k_a), v=vr, al=-kk, be=kk * a, g=g))

    row = lax.broadcasted_iota(jnp.int32, (L, L), 0)
    col = lax.broadcasted_iota(jnp.int32, (L, L), 1)
    tri = (row >= col).astype(BF16)
    t_w = lax.broadcasted_iota(jnp.int32, (L, W), 0)
    lane_w = lax.broadcasted_iota(jnp.int32, (L, W), 1)
    s_w = lane_w % RWKV_HEAD
    strict = t_w > s_w
    incl = t_w >= s_w
    eye = (t_w == s_w).astype(F32)
    blk_w = lane_w // RWKV_HEAD
    bd_mask = (lax.broadcasted_iota(jnp.int32, (W, W), 0) // RWKV_HEAD
               == lax.broadcasted_iota(jnp.int32, (W, W), 1) // RWKV_HEAD)

    def blockdiag(w):
        tiled = jnp.concatenate([w.astype(BF16)] * SCAN_GROUP, axis=0)
        return jnp.where(bd_mask, tiled, jnp.zeros_like(tiled))

    ln_w, ln_b, r_k = (pp_ref[i:i + 1, :] for i in range(3))
    items = [(b, g) for b in range(n_seq) for g in range(n_grp)]
    for sub in range(step_rows // L):
        rows = slice(sub * L, (sub + 1) * L)
        ops = {}
        for b in range(n_seq):
            p = pre[b]
            lw = p['lw'][rows]
            cum = _dot_split_rhs(tri, lw, 2)
            p_inc = jnp.exp(cum)
            p_exc = jnp.exp(cum - lw)
            p_inv = jnp.exp(-cum)
            p_last = p_inc[L - 1:L, :]
            a_t = p['al'][rows] * p_exc
            r_t = p['r'][rows] * p_inc
            b_t = p['be'][rows] * p_inv
            k_t = p['k'][rows] * p_inv
            b_end = b_t * p_last
            k_end = k_t * p_last
            v = p['v'][rows]
            for g in range(n_grp):
                gs = slice(g * W, (g + 1) * W)
                ops[(b, g)] = dict(
                    ar=jnp.concatenate([a_t[:, gs], r_t[:, gs]], axis=0),
                    bt=b_t[:, gs], kt=k_t[:, gs], v=v[:, gs], p_last=p_last[:, gs],
                    bke=jnp.concatenate([b_end[:, gs], k_end[:, gs]], axis=0))

        gm, xs, av, s0 = {}, {}, {}, {}
        for it in items:
            o = ops[it]
            rhs = jnp.concatenate([blockdiag(o['bt']), blockdiag(o['kt'])], axis=0)
            gm[it] = _bdot_nt(o['ar'], rhs)
            s0[it] = s_scr[it[0], it[1]]
            xs[it] = _bdot_nt(o['ar'], blockdiag(s0[it]))
        a_ab, a_rb = {}, {}
        for it in items:
            g = gm[it]
            a_ab[it] = jnp.where(strict, g[:L, :W], 0.0)
            a_rb[it] = jnp.where(incl, g[L:, :W], 0.0)
            a_kk = jnp.concatenate([jnp.where(strict, g[:L, W:], 0.0), jnp.where(incl, g[L:, W:], 0.0)], axis=0)
            av[it] = _bdot(a_kk, blockdiag(ops[it]['v']))

        pw = {it: _bdot(a_ab[it], blockdiag(a_ab[it])) for it in items}
        tinv = {it: eye + a_ab[it] for it in items}
        span = 2
        while 2 * span < L:
            both = {it: _bdot(jnp.concatenate([pw[it], tinv[it]], axis=0), blockdiag(pw[it])) for it in items}
            tinv = {it: tinv[it] + both[it][L:] for it in items}
            pw = {it: both[it][:L] for it in items}
            span *= 2
        tinv = {it: tinv[it] + _bdot(tinv[it], blockdiag(pw[it])) for it in items}

        u = {it: _bdot(tinv[it], blockdiag(xs[it][:L] + av[it][:L])) for it in items}
        y = {it: xs[it][L:] + av[it][L:] + _bdot(a_rb[it], blockdiag(u[it])) for it in items}
        for it in items:
            o = ops[it]
            full = _bdot_tn(jnp.concatenate([u[it], o['v']], axis=0), o['bke'])
            upd = full[:RWKV_HEAD]
            for h in range(1, SCAN_GROUP):
                upd = jnp.where(blk_w == h, full[h * RWKV_HEAD:(h + 1) * RWKV_HEAD], upd)
            s_scr[it[0], it[1]] = s0[it] * o['p_last'] + upd

        inv_n = 1.0 / RWKV_HEAD
        for b in range(n_seq):
            p = pre[b]
            yb = jnp.concatenate([y[(b, g)] for g in range(n_grp)], axis=1)
            d = yb - _head_sums(yb, ones, 1) * inv_n
            var = _head_sums(d * d, ones, 1) * inv_n
            yn = d * lax.rsqrt(var + RWKV_GN_EPS) * ln_w + ln_b
            bonus = _head_sums(p['r'][rows] * p['k'][rows] * r_k, ones, 1) * p['v'][rows]
            y_ref[b, rows, :] = ((yn + bonus) * p['g'][rows]).astype(BF16)


def _rwkv_mixer(rkv, lora, v_first, zv, mix_r, mix_l, mix_v, pv_pre, wup, aup, gup, vup, pv_post, seq):
    n = rkv.shape[0]
    n_seq = n // seq
    has_vres = v_first is not None
    step_rows = SCAN_SUB * SCAN_CHUNK
    timed = lambda cols: pl.BlockSpec((n_seq, step_rows, cols), lambda t: (0, t, 0))
    const = lambda shape: pl.BlockSpec(shape, lambda t: (0, 0))
    by_seq = lambda a: a.reshape(n_seq, seq, a.shape[-1])
    in_specs = [timed(RKV_COLS), timed(LORA_COLS)]
    args = [by_seq(rkv), by_seq(lora)]
    if has_vres:
        in_specs += [timed(RWKV_WIDTH), timed(LANES), const((1, LANES))]
        args += [by_seq(v_first), by_seq(zv), mix_v]
    in_specs += [
        const((1, RKV_COLS)), const((1, LORA_COLS)), const((SUBLANES, RWKV_WIDTH)),
        const((LANES, RWKV_WIDTH)), const((LANES, RWKV_WIDTH)), const((2 * LANES, RWKV_WIDTH)),
        const((LANES, RWKV_WIDTH)), const((MXU_COLS, MXU_COLS)), const((SUBLANES, RWKV_WIDTH)),
    ]
    args += [mix_r, mix_l, pv_pre, wup, aup, gup, vup, _head_ones(), pv_post]
    out_specs = [timed(RWKV_WIDTH)]
    out_shape = [jax.ShapeDtypeStruct((n_seq, seq, RWKV_WIDTH), BF16)]
    if not has_vres:
        out_specs.append(timed(RWKV_WIDTH))
        out_shape.append(jax.ShapeDtypeStruct((n_seq, seq, RWKV_WIDTH), F32))
    scratch = [
        pltpu.VMEM((n_seq, N_RWKV_HEADS // SCAN_GROUP, RWKV_HEAD, SCAN_GROUP * RWKV_HEAD), F32),
        pltpu.VMEM((n_seq, SUBLANES, RKV_COLS), F32),
        pltpu.VMEM((n_seq, SUBLANES, LORA_COLS), F32),
    ]
    if has_vres:
        scratch.append(pltpu.VMEM((n_seq, SUBLANES, LANES), F32))
    outs = pl.pallas_call(
        functools.partial(_rwkv_kernel, has_vres=has_vres),
        grid=(seq // step_rows,),
        in_specs=in_specs,
        out_specs=out_specs,
        out_shape=out_shape,
        scratch_shapes=scratch,
        compiler_params=_cparams("arbitrary"),
        name="rwkv_mixer",
    )(*args)
    y = outs[0].reshape(n, RWKV_WIDTH)
    return y, (None if has_vres else outs[1].reshape(n, RWKV_WIDTH))


def _merge_kernel(a_ref, y_ref, wa_ref, wb_ref, ga_ref, gb_ref, o_ref):
    for c in range(o_ref.shape[1] // MXU_COLS):
        cs = slice(c * MXU_COLS, (c + 1) * MXU_COLS)
        pa = jnp.dot(a_ref[...], wa_ref[:, cs].astype(BF16), preferred_element_type=F32)
        pb = jnp.dot(y_ref[...], wb_ref[:, cs].astype(BF16), preferred_element_type=F32)
        ga = jax.nn.sigmoid(ga_ref[:, cs].astype(F32))
        gb = jax.nn.sigmoid(gb_ref[:, cs].astype(F32))
        o_ref[:, cs] = (ga * pa + gb * pb).astype(BF16)


def _merge(layer, attn, y, wa, wb, gates, tm=1024, tn=1024):
    n = attn.shape[0]
    nj = D_MODEL // tn
    return pl.pallas_call(
        _merge_kernel,
        grid=(n // tm, nj),
        in_specs=[
            pl.BlockSpec((tm, ATTN_WIDTH), lambda i, j: (i, 0)),
            pl.BlockSpec((tm, RWKV_WIDTH), lambda i, j: (i, 0)),
            pl.BlockSpec((None, ATTN_WIDTH, tn), lambda i, j: (layer, 0, j)),
            pl.BlockSpec((None, RWKV_WIDTH, tn), lambda i, j: (layer, 0, j)),
            pl.BlockSpec((tm, tn), lambda i, j: (i, j)),
            pl.BlockSpec((tm, tn), lambda i, j: (i, j + nj)),
        ],
        out_specs=pl.BlockSpec((tm, tn), lambda i, j: (i, j)),
        out_shape=jax.ShapeDtypeStruct((n, D_MODEL), BF16),
        compiler_params=_cparams("parallel", "arbitrary"),
        name="gated_merge",
    )(attn, y, wa, wb, gates, gates)


def _outproj_router_kernel(m_ref, wo_ref, x_ref, nw_ref, wr_ref, br_ref,
                           xo_ref, hf_ref, slab_ref, plan_ref, cnt_ref, run_scr):
    @pl.when(pl.program_id(0) == 0)
    def _():
        run_scr[...] = jnp.zeros_like(run_scr)

    tm = x_ref.shape[0]
    sq = jnp.zeros((tm, LANES), F32)
    for c in range(D_MODEL // MXU_COLS):
        cs = slice(c * MXU_COLS, (c + 1) * MXU_COLS)
        xn_c = x_ref[:, cs] + jnp.dot(m_ref[...], wo_ref[:, cs], preferred_element_type=F32)
        xo_ref[:, cs] = xn_c
        for l in range(MXU_COLS // LANES):
            piece = xn_c[:, l * LANES:(l + 1) * LANES]
            sq = sq + piece * piece
    ms = jnp.sum(sq, axis=-1, keepdims=True) * (1.0 / D_MODEL)
    hf = xo_ref[...] * lax.rsqrt(ms + NORM_EPS) * nw_ref[...]
    _store_rows(hf_ref, hf)

    h_hi = hf.astype(BF16)
    h_lo = (hf - h_hi.astype(F32)).astype(BF16)
    wr = wr_ref[...]
    w_hi = wr.astype(BF16)
    w_lo = (wr - w_hi.astype(F32)).astype(BF16)
    lg = (jnp.dot(h_hi, w_hi, preferred_element_type=F32)
          + jnp.dot(h_hi, w_lo, preferred_element_type=F32)
          + jnp.dot(h_lo, w_hi, preferred_element_type=F32)) + br_ref[...]

    chunks = range(tm // ROUTER_ROWS)
    lgs = [lg[c * ROUTER_ROWS:(c + 1) * ROUTER_ROWS] for c in chunks]
    lane = lax.broadcasted_iota(jnp.int32, (ROUTER_ROWS, ROUTER_LANES), 1).astype(F32)
    neg = -jnp.inf
    big = float(ROUTER_LANES)
    is_group = lane < N_GROUPS

    def row_max(vals):
        return [jnp.max(v, axis=-1, keepdims=True) for v in vals]

    def first_index(vals, mx):
        return [jnp.min(jnp.where(v == m, lane, big), axis=-1, keepdims=True) for v, m in zip(vals, mx)]

    def row_sum(vals):
        return [jnp.sum(v, axis=-1, keepdims=True) for v in vals]

    gl = [jnp.where(is_group, v, neg) for v in lgs]
    gmax = row_max(gl)
    gsel = first_index(gl, gmax)
    gsum = row_sum([jnp.where(is_group, jnp.exp(v - m), 0.0) for v, m in zip(lgs, gmax)])
    lo_lane = [EXPERT_LANE0 + EXPERTS_PER_GROUP * g for g in gsel]
    el = [jnp.where((lane >= lo) & (lane < lo + EXPERTS_PER_GROUP), v, neg) for v, lo in zip(lgs, lo_lane)]
    v1 = row_max(el)
    i1 = first_index(el, v1)
    el2 = [jnp.where(lane == i, neg, v) for v, i in zip(el, i1)]
    v2 = row_max(el2)
    i2 = first_index(el2, v2)
    oh1 = [lane == i for i in i1]
    oh2 = [lane == i for i in i2]

    cnt = jnp.concatenate([a.astype(F32) + b.astype(F32) for a, b in zip(oh1, oh2)], axis=0)
    r_i = lax.broadcasted_iota(jnp.int32, (tm, tm), 0)
    c_i = lax.broadcasted_iota(jnp.int32, (tm, tm), 1)
    before = jnp.dot((r_i > c_i).astype(BF16), cnt.astype(BF16), preferred_element_type=F32)
    tot = before + run_scr[0:1, :]
    tots = [tot[c * ROUTER_ROWS:(c + 1) * ROUTER_ROWS] for c in chunks]
    rank1 = row_sum([jnp.where(o, t, 0.0) for o, t in zip(oh1, tots)])
    rank2 = row_sum([jnp.where(o, t, 0.0) for o, t in zip(oh2, tots)])
    run = run_scr[0:1, :] + jnp.sum(cnt, axis=0, keepdims=True)
    run_scr[...] = jnp.broadcast_to(run, run_scr.shape)
    cnt_ref[...] = jnp.broadcast_to(run, cnt_ref.shape)

    for c in chunks:
        gp = 1.0 / gsum[c]
        e21 = jnp.exp(v2[c] - v1[c])
        ew1 = gp / (1.0 + e21)
        ew2 = gp * e21 / (1.0 + e21)
        slab = jnp.where(lane == 0, i1[c] - EXPERT_LANE0,
               jnp.where(lane == 1, i2[c] - EXPERT_LANE0,
               jnp.where(lane == 2, ew1,
               jnp.where(lane == 3, ew2,
               jnp.where(lane == 4, rank1[c],
               jnp.where(lane == 5, rank2[c], 0.0))))))
        slab_ref[c * ROUTER_ROWS:(c + 1) * ROUTER_ROWS, :] = slab
        plan_ref[:, c * ROUTER_ROWS:(c + 1) * ROUTER_ROWS] = jnp.transpose(slab)[:SUBLANES, :]


def _outproj_router(merged, w_out, x, norm_w, w_router, b_router, tm=512):
    n = x.shape[0]
    row = lambda i: (i, 0)
    const = lambda i: (0, 0)
    return pl.pallas_call(
        _outproj_router_kernel,
        grid=(n // tm,),
        in_specs=[
            pl.BlockSpec((tm, D_MODEL), row),
            pl.BlockSpec((D_MODEL, D_MODEL), const),
            pl.BlockSpec((tm, D_MODEL), row),
            pl.BlockSpec((1, D_MODEL), const),
            pl.BlockSpec((D_MODEL, ROUTER_LANES), const),
            pl.BlockSpec((1, ROUTER_LANES), const),
        ],
        out_specs=[
            pl.BlockSpec((tm, D_MODEL), row),
            pl.BlockSpec((tm * ROW_SUB, LANES), row),
            pl.BlockSpec((tm, ROUTER_LANES), row),
            pl.BlockSpec((SUBLANES, tm), lambda i: (0, i)),
            pl.BlockSpec((SUBLANES, ROUTER_LANES), const),
        ],
        out_shape=[
            jax.ShapeDtypeStruct((n, D_MODEL), F32),
            jax.ShapeDtypeStruct((n * ROW_SUB, LANES), F32),
            jax.ShapeDtypeStruct((n, ROUTER_LANES), F32),
            jax.ShapeDtypeStruct((SUBLANES, n), F32),
            jax.ShapeDtypeStruct((SUBLANES, ROUTER_LANES), F32),
        ],
        scratch_shapes=[pltpu.VMEM((SUBLANES, ROUTER_LANES), F32)],
        compiler_params=_cparams("arbitrary"),
        name="outproj_router",
    )(merged, w_out, x, norm_w, w_router, b_router)


def _dispatch_kernel(slot0_ref, slot1_ref, hf_ref, xb_in_hbm, xb_hbm, sem, *, tm):
    del xb_in_hbm

    def copy(u, j):
        return pltpu.make_async_copy(_slab(hf_ref, u), _slab(xb_hbm, (slot0_ref, slot1_ref)[j][u]), sem)

    def start(u, c):
        for j in range(TOP_K):
            copy(u, j).start(priority=j)
        return c

    def wait(u, c):
        for j in range(TOP_K):
            copy(u, j).wait()
        return c

    lax.fori_loop(0, tm, start, 0, unroll=8)
    lax.fori_loop(0, tm, wait, 0, unroll=8)


def _dispatch(slots, hf, cap, backing, tm=2048):
    n = hf.shape[0] // ROW_SUB
    tm = min(tm, n)
    assert backing.shape == (cap * ROW_SUB, LANES)
    return pl.pallas_call(
        functools.partial(_dispatch_kernel, tm=tm),
        grid=(n // tm,),
        in_specs=[
            pl.BlockSpec((tm,), lambda i: (i,), memory_space=pltpu.SMEM),
            pl.BlockSpec((tm,), lambda i: (i,), memory_space=pltpu.SMEM),
            pl.BlockSpec((tm * ROW_SUB, LANES), lambda i: (i, 0)),
            pl.BlockSpec(memory_space=pl.ANY),
        ],
        out_specs=pl.BlockSpec(memory_space=pl.ANY),
        out_shape=jax.ShapeDtypeStruct((cap * ROW_SUB, LANES), F32),
        scratch_shapes=[pltpu.SemaphoreType.DMA(())],
        input_output_aliases={3: 0},
        compiler_params=_cparams("arbitrary"),
        name="moe_dispatch",
    )(slots[0], slots[1], hf, backing)


def _ffn_kernel(be_ref, nu_ref, par_ref, nxt_ref, x_ref, wg_hbm, wu_hbm, wd_hbm, o_ref,
                wg_f, wu_f, wd_f, wg_s, wu_s, wd_s, sems, *, layer):
    b = pl.program_id(0)
    used = b < nu_ref[0]
    prev_e = be_ref[jnp.maximum(b - 1, 0)]
    new_expert = jnp.logical_or(b == 0, be_ref[b] != prev_e)

    def fetch(e, p):
        return [pltpu.make_async_copy(src.at[layer, e], dst.at[p], sems.at[p, i])
                for i, (src, dst) in enumerate(((wg_hbm, wg_f), (wu_hbm, wu_f), (wd_hbm, wd_f)))]

    @pl.when(jnp.logical_and(used, new_expert))
    def _():
        for p in range(2):
            @pl.when(par_ref[b] == p)
            def _():
                @pl.when(b == 0)
                def _():
                    for cp in fetch(be_ref[0], p):
                        cp.start()

                @pl.when(nxt_ref[b] >= 0)
                def _():
                    for cp in fetch(nxt_ref[b], 1 - p):
                        cp.start()

                for cp in fetch(be_ref[b], p):
                    cp.wait()
                wg_s[...] = wg_f[p].astype(BF16)
                wu_s[...] = wu_f[p].astype(BF16)
                wd_s[...] = wd_f[p].astype(BF16)

    @pl.when(used)
    def _():
        slabs = MXU_COLS // LANES
        gt = up = None
        for kc in range(D_MODEL // MXU_COLS):
            xk = jnp.concatenate([x_ref[pl.ds(kc * slabs + s, MOE_ROWS, stride=ROW_SUB), :] for s in range(slabs)],
                                 axis=1).astype(BF16)
            ks = slice(kc * MXU_COLS, (kc + 1) * MXU_COLS)
            g_k = jnp.dot(xk, wg_s[ks, :], preferred_element_type=F32)
            u_k = jnp.dot(xk, wu_s[ks, :], preferred_element_type=F32)
            gt = g_k if gt is None else gt + g_k
            up = u_k if up is None else up + u_k
        hid = (gt * jax.nn.sigmoid(gt) * up).astype(BF16)
        for c in range(D_MODEL // MXU_COLS):
            y_c = jnp.dot(hid, wd_s[:, c * MXU_COLS:(c + 1) * MXU_COLS], preferred_element_type=F32)
            for s in range(slabs):
                o_ref[pl.ds(c * slabs + s, MOE_ROWS, stride=ROW_SUB), :] = y_c[:, s * LANES:(s + 1) * LANES]

    @pl.when(jnp.logical_not(used))
    def _():
        o_ref[...] = jnp.zeros_like(o_ref)


def _expert_ffn(layer, plan, xb, w_gate, w_up, w_down):
    nblk = xb.shape[0] // (MOE_ROWS * ROW_SUB)
    hbm = pl.BlockSpec(memory_space=pl.ANY)
    grid_spec = pltpu.PrefetchScalarGridSpec(
        num_scalar_prefetch=4,
        grid=(nblk,),
        in_specs=[
            pl.BlockSpec((MOE_ROWS * ROW_SUB, LANES), lambda b, be, nu, par, nxt: (jnp.minimum(b, nu[0] - 1), 0)),
            hbm, hbm, hbm,
        ],
        out_specs=pl.BlockSpec((MOE_ROWS * ROW_SUB, LANES), lambda b, be, nu, par, nxt: (b, 0)),
        scratch_shapes=[
            pltpu.VMEM((2, D_MODEL, EXPERT_HIDDEN), F32),
            pltpu.VMEM((2, D_MODEL, EXPERT_HIDDEN), F32),
            pltpu.VMEM((2, EXPERT_HIDDEN, D_MODEL), F32),
            pltpu.VMEM((D_MODEL, EXPERT_HIDDEN), BF16),
            pltpu.VMEM((D_MODEL, EXPERT_HIDDEN), BF16),
            pltpu.VMEM((EXPERT_HIDDEN, D_MODEL), BF16),
            pltpu.SemaphoreType.DMA((2, 3)),
        ],
    )
    return pl.pallas_call(
        functools.partial(_ffn_kernel, layer=layer),
        grid_spec=grid_spec,
        out_shape=jax.ShapeDtypeStruct(xb.shape, F32),
        compiler_params=_cparams("arbitrary"),
        name="expert_ffn",
    )(plan.block_expert, plan.n_used, plan.parity, plan.next_expert, xb, w_gate, w_up, w_down)


def _combine_kernel(slot0_ref, slot1_ref, next0_ref, next1_ref, x_ref, slab_ref, fw_ref, yb_hbm, o_ref,
                    buf00, buf01, buf10, buf11, sems, *, tm, n_tiles, final_norm):
    i = pl.program_id(0)
    bufs = ((buf00, buf01), (buf10, buf11))
    cur = (slot0_ref, slot1_ref)

    def copies(slots, p, u):
        return [pltpu.make_async_copy(_slab(yb_hbm, slots[j][u]), _slab(bufs[p][j], u), sems.at[p])
                for j in range(TOP_K)]

    def start_all(slots, p):
        def body(u, c):
            for j, cp in enumerate(copies(slots, p, u)):
                cp.start(priority=j)
            return c
        lax.fori_loop(0, tm, body, 0, unroll=8)

    def wait_all(p):
        def body(u, c):
            for cp in copies(cur, p, u):
                cp.wait()
            return c
        lax.fori_loop(0, tm, body, 0, unroll=8)

    @pl.when(i == 0)
    def _():
        start_all(cur, 0)

    for p in range(2):
        @pl.when(i % 2 == p)
        def _():
            @pl.when(i + 1 < n_tiles)
            def _():
                start_all((next0_ref, next1_ref), 1 - p)

            wait_all(p)
            slab = slab_ref[...]
            w1 = slab[:, 2:3]
            w2 = slab[:, 3:4]
            out = x_ref[...] + (_rows_to_2d(bufs[p][0], tm) * w1 + _rows_to_2d(bufs[p][1], tm) * w2)
            if final_norm:
                out = _rms(out, fw_ref[...])
            o_ref[...] = out


def _combine(slots, x, slab, final_w, yb, final_norm, tm=256):
    n = x.shape[0]
    n_tiles = n // tm
    row = lambda i: (i, 0)
    row_buf = pltpu.VMEM((tm * ROW_SUB, LANES), F32)
    return pl.pallas_call(
        functools.partial(_combine_kernel, tm=tm, n_tiles=n_tiles, final_norm=final_norm),
        grid=(n_tiles,),
        in_specs=[
            pl.BlockSpec((tm,), lambda i: (i,), memory_space=pltpu.SMEM),
            pl.BlockSpec((tm,), lambda i: (i,), memory_space=pltpu.SMEM),
            pl.BlockSpec((tm,), lambda i: (jnp.minimum(i + 1, n_tiles - 1),), memory_space=pltpu.SMEM),
            pl.BlockSpec((tm,), lambda i: (jnp.minimum(i + 1, n_tiles - 1),), memory_space=pltpu.SMEM),
            pl.BlockSpec((tm, D_MODEL), row),
            pl.BlockSpec((tm, ROUTER_LANES), row),
            pl.BlockSpec((1, D_MODEL), lambda i: (0, 0)),
            pl.BlockSpec(memory_space=pl.ANY),
        ],
        out_specs=pl.BlockSpec((tm, D_MODEL), row),
        out_shape=jax.ShapeDtypeStruct((n, D_MODEL), F32),
        scratch_shapes=[row_buf, row_buf, row_buf, row_buf, pltpu.SemaphoreType.DMA((2,))],
        compiler_params=_cparams("arbitrary"),
        name="moe_combine",
    )(slots[0], slots[1], slots[0], slots[1], x, slab, final_w, yb)


class MoePlan(NamedTuple):
    slot: tuple
    block_expert: jax.Array
    n_used: jax.Array
    parity: jax.Array
    next_expert: jax.Array
    cap: int


def _moe_plan(plan_t, counts_row, n_tokens):
    eid = plan_t[0:TOP_K].astype(jnp.int32)
    rank = plan_t[4:4 + TOP_K].astype(jnp.int32)
    counts = counts_row[EXPERT_LANE0:EXPERT_LANE0 + N_EXPERTS].astype(jnp.int32)
    padded = (counts + MOE_ROWS - 1) // MOE_ROWS * MOE_ROWS
    pad_ends = jnp.cumsum(padded)
    pad_starts = pad_ends - padded
    expert = jnp.arange(N_EXPERTS, dtype=jnp.int32)[:, None, None]
    region = jnp.sum(jnp.where(eid[None] == expert, pad_starts[:, None, None], 0), axis=0)
    slot = region + rank
    nblk = n_tokens * TOP_K // MOE_ROWS + N_EXPERTS
    block_start = jnp.arange(nblk, dtype=jnp.int32) * MOE_ROWS
    block_expert = jnp.minimum(
        jnp.sum((pad_ends[None, :] <= block_start[:, None]).astype(jnp.int32), axis=1),
        N_EXPERTS - 1).astype(jnp.int32)
    n_used = (pad_ends[-1:] // MOE_ROWS).astype(jnp.int32)
    blk = jnp.arange(nblk, dtype=jnp.int32)
    prev_expert = jnp.concatenate([block_expert[:1], block_expert[:-1]])
    first = (blk < n_used[0]) & ((blk == 0) | (block_expert != prev_expert))
    parity = ((jnp.cumsum(first.astype(jnp.int32)) - 1) % 2).astype(jnp.int32)
    later_first = first[None, :] & (blk[None, :] > blk[:, None])
    next_first = jnp.min(jnp.where(later_first, blk[None, :], nblk), axis=1)
    next_expert = jnp.where(next_first < nblk, block_expert[jnp.minimum(next_first, nblk - 1)], -1)
    return MoePlan(tuple(slot[j] for j in range(TOP_K)), block_expert, n_used, parity,
                   next_expert.astype(jnp.int32), nblk * MOE_ROWS)


def kernel(x, attn_norm_w, w_in, shift_mix, attn_sinks, rwkv_w0, rwkv_w_up, rwkv_a0, rwkv_a_up, rwkv_g_up, rwkv_k_k, rwkv_k_a, rwkv_r_k, rwkv_ln_w, rwkv_ln_b, vres_down, vres_mix, vres_up, vres_v0, w_branch_a, w_branch_b, w_out, ffn_norm_w, router_group_w, router_group_b, router_expert_w, router_expert_b, expert_w_gate, expert_w_up, expert_w_down, final_norm_w):
    batch, seq, _ = x.shape
    n = batch * seq
    depth = w_in.shape[0]
    xf = x.reshape(n, D_MODEL)
    w_in_bf = jnp.swapaxes(w_in, 1, 2)

    zero_c = jnp.zeros((depth, RWKV_WIDTH), F32)
    v0_all = jnp.concatenate([zero_c[:1], vres_v0.astype(F32)], axis=0)
    pv_pre_all = jnp.stack([rwkv_w0, rwkv_a0, rwkv_k_k, rwkv_k_a, v0_all, zero_c, zero_c, zero_c], axis=1)
    pv_post_all = jnp.stack([rwkv_ln_w, rwkv_ln_b, rwkv_r_k.reshape(depth, RWKV_WIDTH),
                             zero_c, zero_c, zero_c, zero_c, zero_c], axis=1)
    mix_r_all = shift_mix[:, None, :RKV_COLS]
    mix_l_all = jnp.pad(shift_mix[:, None, RKV_COLS:], ((0, 0), (0, 0), (0, LORA_COLS - LORA_WIDTH)))
    wup_all = jnp.pad(rwkv_w_up.astype(BF16), ((0, 0), (0, LANES - DECAY_LORA), (0, 0)))
    aup_all = jnp.pad(rwkv_a_up.astype(BF16), ((0, 0), (DECAY_LORA, LANES - DECAY_LORA - AAA_LORA), (0, 0)))
    gup_all = jnp.pad(rwkv_g_up.astype(BF16), ((0, 0), (0, 2 * LANES - GATE_LORA), (0, 0)))
    vup_all = jnp.pad(vres_up.astype(BF16), ((1, 0), (0, LANES - MV_LORA), (0, 0)))
    vd_all = jnp.pad(vres_down.astype(BF16), ((0, 0), (0, 0), (0, LANES - MV_LORA)))
    mix_v_all = jnp.pad(vres_mix, ((0, 0), (0, LANES - MV_LORA)))[:, None, :]
    router_pad = ROUTER_LANES - N_GROUPS - N_EXPERTS
    w_router_all = jnp.pad(jnp.concatenate([router_group_w, router_expert_w], axis=2),
                           ((0, 0), (0, 0), (0, router_pad)))
    b_router_all = jnp.pad(jnp.concatenate([router_group_b, router_expert_b], axis=1),
                           ((0, 0), (0, router_pad)))[:, None, :]
    attn_norm_all = attn_norm_w[:, None, :]
    ffn_norm_all = ffn_norm_w[:, None, :]
    w_out_bf = w_out.astype(BF16)
    sinks_all = attn_sinks.astype(F32)

    v_first = None
    slot_backing = None
    slot_rows = (n * TOP_K // MOE_ROWS + N_EXPERTS) * MOE_ROWS
    for i in range(depth):
        has_vres = i > 0
        mix_r, mix_l, pv_pre, pv_post = mix_r_all[i], mix_l_all[i], pv_pre_all[i], pv_post_all[i]
        wup, aup, gup, vup = wup_all[i], aup_all[i], gup_all[i], vup_all[i]
        vd, mix_v = (vd_all[i - 1], mix_v_all[i - 1]) if has_vres else (None, None)
        w_router, b_router = w_router_all[i], b_router_all[i]

        need_zeros = slot_backing is None
        proj = _inproj(i, xf, attn_norm_all[i], w_in_bf, vd,
                       zero_rows=slot_rows * ROW_SUB if need_zeros else 0)
        qkv, rkv, lora, gates = proj[:4]
        zv = proj[4] if has_vres else None
        if need_zeros:
            slot_backing = proj[-1]
        attn = _attention(qkv, sinks_all[i], seq)
        y, v_new = _rwkv_mixer(rkv, lora, v_first, zv, mix_r, mix_l, mix_v, pv_pre,
                               wup, aup, gup, vup, pv_post, seq)
        if not has_vres:
            v_first = v_new
        merged = _merge(i, attn, y, w_branch_a, w_branch_b, gates)

        x_mid, hf, slab, plan_t, counts = _outproj_router(
            merged, w_out_bf[i], xf, ffn_norm_all[i], w_router, b_router)
        plan = _moe_plan(plan_t, counts[0], n)
        xb = _dispatch(plan.slot, hf, plan.cap, slot_backing)
        yb = _expert_ffn(i, plan, xb, expert_w_gate, expert_w_up, expert_w_down)
        slot_backing = yb
        xf = _combine(plan.slot, x_mid, slab, final_norm_w.reshape(1, D_MODEL), yb, final_norm=(i == depth - 1))
    return xf.reshape(batch, seq, D_MODEL)
```
